```python
import math
import jax, jax.numpy as jnp
from jax import lax
import numpy as np

D_MODEL = 1024
BATCH = 8
SEQ = 4096
DEPTH = 4

CHUNK = 64
N_MIXERS = 2
N_LAYERS_A = (DEPTH + 1) // 2
N_LAYERS_B = DEPTH // 2
FOX_HEADS = 16
FOX_HEAD_DIM = D_MODEL // FOX_HEADS
Q_BLOCK = 128
S5_GROUP = 16
S5_GROUPS = D_MODEL // S5_GROUP
S5_STATE = 64
D_FF = 256 * ((8 * D_MODEL // 3 + 255) // 256)
ALPHA = (2.0 * DEPTH) ** 0.25
BETA = (8.0 * DEPTH) ** -0.25
LN_EPS = 1e-5
NEG_INF = -1e30

kernel_name = "fox_s5_macaron_deepnorm_hybrid"


def layer_norm(x, g, b):
    xf = x.astype(jnp.float32)
    mu = jnp.mean(xf, axis=-1, keepdims=True)
    var = jnp.mean(jnp.square(xf - mu), axis=-1, keepdims=True)
    y = (xf - mu) * lax.rsqrt(var + LN_EPS) * g.astype(jnp.float32) + b.astype(jnp.float32)
    return y.astype(x.dtype)


def swiglu(x, w_in, w_out):
    gate, up = jnp.split(x @ w_in, 2, axis=-1)
    return (jax.nn.silu(gate) * up) @ w_out


def fox_attention(x, w_in, b_f, w_o):
    bsz, seq, d = x.shape
    proj = x @ w_in
    q = proj[..., :d].reshape(bsz, seq, FOX_HEADS, FOX_HEAD_DIM)
    k = proj[..., d:2 * d].reshape(bsz, seq, FOX_HEADS, FOX_HEAD_DIM)
    v = proj[..., 2 * d:3 * d].reshape(bsz, seq, FOX_HEADS, FOX_HEAD_DIM)
    f_logit = proj[..., 3 * d:].astype(jnp.float32) + b_f.astype(jnp.float32)
    log_f = jax.nn.log_sigmoid(f_logit)
    cum = jnp.cumsum(log_f, axis=1).transpose(0, 2, 1)
    n_blk = seq // Q_BLOCK
    q_blocks = q.reshape(bsz, n_blk, Q_BLOCK, FOX_HEADS, FOX_HEAD_DIM).transpose(1, 0, 2, 3, 4)
    c_blocks = cum.reshape(bsz, FOX_HEADS, n_blk, Q_BLOCK).transpose(2, 0, 1, 3)
    starts = jnp.arange(n_blk, dtype=jnp.int32) * Q_BLOCK
    k_pos = jnp.arange(seq, dtype=jnp.int32)
    scale = 1.0 / math.sqrt(FOX_HEAD_DIM)

    def one_block(args):
        q_i, c_i, start = args
        s = jnp.einsum('bqhd,bkhd->bhqk', q_i, k).astype(jnp.float32) * scale
        s = s + (c_i[..., :, None] - cum[:, :, None, :])
        q_pos = start + jnp.arange(Q_BLOCK, dtype=jnp.int32)
        mask = k_pos[None, :] <= q_pos[:, None]
        s = jnp.where(mask, s, NEG_INF)
        p = jax.nn.softmax(s, axis=-1).astype(v.dtype)
        return jnp.einsum('bhqk,bkhd->bqhd', p, v)

    o = lax.map(one_block, (q_blocks, c_blocks, starts))
    o = o.transpose(1, 0, 2, 3, 4).reshape(bsz, seq, d)
    return o @ w_o


def s5_mixer(x, a_re, a_im, log_dt, b_re, b_im, c_re, c_im, d_skip, w_out):
    bsz, seq, d = x.shape
    u = x.reshape(bsz, seq, S5_GROUPS, S5_GROUP).astype(jnp.float32)
    lr = a_re.astype(jnp.float32)
    li = a_im.astype(jnp.float32)
    dt = jnp.exp(log_dt.astype(jnp.float32))[:, None]
    mag = jnp.exp(lr * dt)
    ang = li * dt
    lb_re = mag * jnp.cos(ang)
    lb_im = mag * jnp.sin(ang)
    den = lr * lr + li * li
    nr = lb_re - 1.0
    ni = lb_im
    z_re = (nr * lr + ni * li) / den
    z_im = (ni * lr - nr * li) / den
    br = b_re.astype(jnp.float32)
    bi = b_im.astype(jnp.float32)
    bb_re = z_re[..., None] * br - z_im[..., None] * bi
    bb_im = z_re[..., None] * bi + z_im[..., None] * br
    bu_re = jnp.einsum('bsgh,gph->bsgp', u, bb_re)
    bu_im = jnp.einsum('bsgh,gph->bsgp', u, bb_im)
    a_re_s = jnp.broadcast_to(lb_re[None, None], (1, seq, S5_GROUPS, S5_STATE))
    a_im_s = jnp.broadcast_to(lb_im[None, None], (1, seq, S5_GROUPS, S5_STATE))

    def combine(e1, e2):
        ar1, ai1, hr1, hi1 = e1
        ar2, ai2, hr2, hi2 = e2
        return (ar2 * ar1 - ai2 * ai1,
                ar2 * ai1 + ai2 * ar1,
                ar2 * hr1 - ai2 * hi1 + hr2,
                ar2 * hi1 + ai2 * hr1 + hi2)

    _, _, h_re, h_im = lax.associative_scan(combine, (a_re_s, a_im_s, bu_re, bu_im), axis=1)
    y = (jnp.einsum('ghp,bsgp->bsgh', c_re.astype(jnp.float32), h_re)
         - jnp.einsum('ghp,bsgp->bsgh', c_im.astype(jnp.float32), h_im)
         + d_skip.astype(jnp.float32) * u)
    y = jax.nn.gelu(y.reshape(bsz, seq, d)).astype(x.dtype)
    val, gate = jnp.split(y @ w_out, 2, axis=-1)
    return val * jax.nn.sigmoid(gate)


def _fwd_setup_inputs(seed: int = 0) -> dict:
    key = jax.random.key(seed)
    ks = jax.random.split(key, 32)
    f32 = jnp.float32
    D, F, H = D_MODEL, D_FF, FOX_HEADS
    G, P, GH = S5_GROUPS, S5_STATE, S5_GROUP

    def nrm(k, shape, fan_in, scale=1.0):
        return jax.random.normal(k, shape, f32) * (scale * fan_in ** -0.5)

    def gain(k, shape):
        return 1.0 + 0.02 * jax.random.normal(k, shape, f32)

    def bias(k, shape):
        return 0.02 * jax.random.normal(k, shape, f32)

    x = jax.random.normal(ks[0], (BATCH, SEQ, D), f32)
    ffn1_w_in = nrm(ks[1], (DEPTH, D, 2 * F), D)
    ffn1_w_out = nrm(ks[2], (DEPTH, F, D), F, BETA)
    ln1_g = gain(ks[3], (DEPTH, D))
    ln1_b = bias(ks[4], (DEPTH, D))
    lnm_g = gain(ks[5], (DEPTH, D))
    lnm_b = bias(ks[6], (DEPTH, D))
    ffn2_w_in = nrm(ks[7], (DEPTH, D, 2 * F), D)
    ffn2_w_out = nrm(ks[8], (DEPTH, F, D), F, BETA)
    ln2_g = gain(ks[9], (DEPTH, D))
    ln2_b = bias(ks[10], (DEPTH, D))

    fox_w_in = jnp.concatenate([
        nrm(ks[11], (N_LAYERS_A, D, 2 * D), D),
        nrm(ks[12], (N_LAYERS_A, D, D), D, BETA),
        nrm(ks[13], (N_LAYERS_A, D, H), D, 0.5),
    ], axis=-1)
    fox_b_f = (jnp.linspace(1.0, 6.0, H, dtype=f32)[None, :]
               + 0.1 * jax.random.normal(ks[14], (N_LAYERS_A, H), f32))
    fox_w_o = nrm(ks[15], (N_LAYERS_A, D, D), D, BETA)

    s5_a_re = -0.5 * jnp.exp(0.05 * jax.random.normal(ks[16], (N_LAYERS_B, G, P), f32))
    s5_a_im = (jnp.pi * jnp.arange(P, dtype=f32))[None, None, :] \
        + 1e-3 * jax.random.normal(ks[17], (N_LAYERS_B, G, P), f32)
    s5_log_dt = jax.random.uniform(ks[18], (N_LAYERS_B, G), f32,
                                   minval=math.log(1e-3), maxval=math.log(1e-1))
    s5_b_re = nrm(ks[19], (N_LAYERS_B, G, P, GH), 2 * GH)
    s5_b_im = nrm(ks[20], (N_LAYERS_B, G, P, GH), 2 * GH)
    s5_c_re = nrm(ks[21], (N_LAYERS_B, G, GH, P), 2 * P)
    s5_c_im = nrm(ks[22], (N_LAYERS_B, G, GH, P), 2 * P)
    s5_d = jax.random.normal(ks[23], (N_LAYERS_B, G, GH), f32)
    s5_w_out = jnp.concatenate([
        nrm(ks[24], (N_LAYERS_B, D, D), D, BETA),
        nrm(ks[25], (N_LAYERS_B, D, D), D),
    ], axis=-1)

    return {"x": x,
            "ffn1_w_in": ffn1_w_in, "ffn1_w_out": ffn1_w_out, "ln1_g": ln1_g, "ln1_b": ln1_b,
            "lnm_g": lnm_g, "lnm_b": lnm_b,
            "ffn2_w_in": ffn2_w_in, "ffn2_w_out": ffn2_w_out, "ln2_g": ln2_g, "ln2_b": ln2_b,
            "fox_w_in": fox_w_in, "fox_b_f": fox_b_f, "fox_w_o": fox_w_o,
            "s5_a_re": s5_a_re, "s5_a_im": s5_a_im, "s5_log_dt": s5_log_dt,
            "s5_b_re": s5_b_re, "s5_b_im": s5_b_im, "s5_c_re": s5_c_re, "s5_c_im": s5_c_im,
            "s5_d": s5_d, "s5_w_out": s5_w_out}


def _fwd_reference(x, ffn1_w_in, ffn1_w_out, ln1_g, ln1_b, lnm_g, lnm_b,
              ffn2_w_in, ffn2_w_out, ln2_g, ln2_b,
              fox_w_in, fox_b_f, fox_w_o,
              s5_a_re, s5_a_im, s5_log_dt, s5_b_re, s5_b_im, s5_c_re, s5_c_im,
              s5_d, s5_w_out):
    for i in range(DEPTH):
        x = layer_norm(ALPHA * x + 0.5 * swiglu(x, ffn1_w_in[i], ffn1_w_out[i]), ln1_g[i], ln1_b[i])
        j = i // N_MIXERS
        if i % N_MIXERS == 0:
            m = fox_attention(x, fox_w_in[j], fox_b_f[j], fox_w_o[j])
        else:
            m = s5_mixer(x, s5_a_re[j], s5_a_im[j], s5_log_dt[j], s5_b_re[j], s5_b_im[j],
                         s5_c_re[j], s5_c_im[j], s5_d[j], s5_w_out[j])
        x = layer_norm(ALPHA * x + m, lnm_g[i], lnm_b[i])
        x = layer_norm(ALPHA * x + 0.5 * swiglu(x, ffn2_w_in[i], ffn2_w_out[i]), ln2_g[i], ln2_b[i])
    return x


import jax as _jax
import jax.numpy as _jnp

TWIN_FORMAT = 'train_step'
FWD_PARAMS = ['x', 'ffn1_w_in', 'ffn1_w_out', 'ln1_g', 'ln1_b', 'lnm_g', 'lnm_b', 'ffn2_w_in', 'ffn2_w_out', 'ln2_g', 'ln2_b', 'fox_w_in', 'fox_b_f', 'fox_w_o', 's5_a_re', 's5_a_im', 's5_log_dt', 's5_b_re', 's5_b_im', 's5_c_re', 's5_c_im', 's5_d', 's5_w_out']
TWIN_WEIGHTS = ['ffn1_w_in', 'ffn1_w_out', 'ln1_g', 'ln1_b', 'lnm_g', 'lnm_b', 'ffn2_w_in', 'ffn2_w_out', 'ln2_g', 'ln2_b', 'fox_w_in', 'fox_b_f', 'fox_w_o', 's5_a_re', 's5_a_im', 's5_log_dt', 's5_b_re', 's5_b_im', 's5_c_re', 's5_c_im', 's5_d', 's5_w_out']
TWIN_DIFF_INPUT = 'x'
TWIN_INPUTS = ['x', 'ffn1_w_in', 'ffn1_w_out', 'ln1_g', 'ln1_b', 'lnm_g', 'lnm_b', 'ffn2_w_in', 'ffn2_w_out', 'ln2_g', 'ln2_b', 'fox_w_in', 'fox_b_f', 'fox_w_o', 's5_a_re', 's5_a_im', 's5_log_dt', 's5_b_re', 's5_b_im', 's5_c_re', 's5_c_im', 's5_d', 's5_w_out', 'loss_target', 'm_ffn1_w_in', 'm_ffn1_w_out', 'm_ln1_g', 'm_ln1_b', 'm_lnm_g', 'm_lnm_b', 'm_ffn2_w_in', 'm_ffn2_w_out', 'm_ln2_g', 'm_ln2_b', 'm_fox_w_in', 'm_fox_b_f', 'm_fox_w_o', 'm_s5_a_re', 'm_s5_a_im', 'm_s5_log_dt', 'm_s5_b_re', 'm_s5_b_im', 'm_s5_c_re', 'm_s5_c_im', 'm_s5_d', 'm_s5_w_out', 'v_ffn1_w_in', 'v_ffn1_w_out', 'v_ln1_g', 'v_ln1_b', 'v_lnm_g', 'v_lnm_b', 'v_ffn2_w_in', 'v_ffn2_w_out', 'v_ln2_g', 'v_ln2_b', 'v_fox_w_in', 'v_fox_b_f', 'v_fox_w_o', 'v_s5_a_re', 'v_s5_a_im', 'v_s5_log_dt', 'v_s5_b_re', 'v_s5_b_im', 'v_s5_c_re', 'v_s5_c_im', 'v_s5_d', 'v_s5_w_out']
TWIN_OUTPUTS = ['loss', 'grad_x', 'grad_ffn1_w_in', 'grad_ffn1_w_out', 'grad_ln1_g', 'grad_ln1_b', 'grad_lnm_g', 'grad_lnm_b', 'grad_ffn2_w_in', 'grad_ffn2_w_out', 'grad_ln2_g', 'grad_ln2_b', 'grad_fox_w_in', 'grad_fox_b_f', 'grad_fox_w_o', 'grad_s5_a_re', 'grad_s5_a_im', 'grad_s5_log_dt', 'grad_s5_b_re', 'grad_s5_b_im', 'grad_s5_c_re', 'grad_s5_c_im', 'grad_s5_d', 'grad_s5_w_out', 'delta_ffn1_w_in', 'delta_ffn1_w_out', 'delta_ln1_g', 'delta_ln1_b', 'delta_lnm_g', 'delta_lnm_b', 'delta_ffn2_w_in', 'delta_ffn2_w_out', 'delta_ln2_g', 'delta_ln2_b', 'delta_fox_w_in', 'delta_fox_b_f', 'delta_fox_w_o', 'delta_s5_a_re', 'delta_s5_a_im', 'delta_s5_log_dt', 'delta_s5_b_re', 'delta_s5_b_im', 'delta_s5_c_re', 'delta_s5_c_im', 'delta_s5_d', 'delta_s5_w_out', 'new_m_ffn1_w_in', 'new_m_ffn1_w_out', 'new_m_ln1_g', 'new_m_ln1_b', 'new_m_lnm_g', 'new_m_lnm_b', 'new_m_ffn2_w_in', 'new_m_ffn2_w_out', 'new_m_ln2_g', 'new_m_ln2_b', 'new_m_fox_w_in', 'new_m_fox_b_f', 'new_m_fox_w_o', 'new_m_s5_a_re', 'new_m_s5_a_im', 'new_m_s5_log_dt', 'new_m_s5_b_re', 'new_m_s5_b_im', 'new_m_s5_c_re', 'new_m_s5_c_im', 'new_m_s5_d', 'new_m_s5_w_out', 'new_v_ffn1_w_in', 'new_v_ffn1_w_out', 'new_v_ln1_g', 'new_v_ln1_b', 'new_v_lnm_g', 'new_v_lnm_b', 'new_v_ffn2_w_in', 'new_v_ffn2_w_out', 'new_v_ln2_g', 'new_v_ln2_b', 'new_v_fox_w_in', 'new_v_fox_b_f', 'new_v_fox_w_o', 'new_v_s5_a_re', 'new_v_s5_a_im', 'new_v_s5_log_dt', 'new_v_s5_b_re', 'new_v_s5_b_im', 'new_v_s5_c_re', 'new_v_s5_c_im', 'new_v_s5_d', 'new_v_s5_w_out']
TWIN_LEAF_KINDS = {'loss': 'loss', 'grad_x': 'grad_x', 'grad_ffn1_w_in': 'grad_w', 'grad_ffn1_w_out': 'grad_w', 'grad_ln1_g': 'grad_w', 'grad_ln1_b': 'grad_w', 'grad_lnm_g': 'grad_w', 'grad_lnm_b': 'grad_w', 'grad_ffn2_w_in': 'grad_w', 'grad_ffn2_w_out': 'grad_w', 'grad_ln2_g': 'grad_w', 'grad_ln2_b': 'grad_w', 'grad_fox_w_in': 'grad_w', 'grad_fox_b_f': 'grad_w', 'grad_fox_w_o': 'grad_w', 'grad_s5_a_re': 'grad_w', 'grad_s5_a_im': 'grad_w', 'grad_s5_log_dt': 'grad_w', 'grad_s5_b_re': 'grad_w', 'grad_s5_b_im': 'grad_w', 'grad_s5_c_re': 'grad_w', 'grad_s5_c_im': 'grad_w', 'grad_s5_d': 'grad_w', 'grad_s5_w_out': 'grad_w', 'delta_ffn1_w_in': 'delta_w', 'delta_ffn1_w_out': 'delta_w', 'delta_ln1_g': 'delta_w', 'delta_ln1_b': 'delta_w', 'delta_lnm_g': 'delta_w', 'delta_lnm_b': 'delta_w', 'delta_ffn2_w_in': 'delta_w', 'delta_ffn2_w_out': 'delta_w', 'delta_ln2_g': 'delta_w', 'delta_ln2_b': 'delta_w', 'delta_fox_w_in': 'delta_w', 'delta_fox_b_f': 'delta_w', 'delta_fox_w_o': 'delta_w', 'delta_s5_a_re': 'delta_w', 'delta_s5_a_im': 'delta_w', 'delta_s5_log_dt': 'delta_w', 'delta_s5_b_re': 'delta_w', 'delta_s5_b_im': 'delta_w', 'delta_s5_c_re': 'delta_w', 'delta_s5_c_im': 'delta_w', 'delta_s5_d': 'delta_w', 'delta_s5_w_out': 'delta_w', 'new_m_ffn1_w_in': 'new_m', 'new_m_ffn1_w_out': 'new_m', 'new_m_ln1_g': 'new_m', 'new_m_ln1_b': 'new_m', 'new_m_lnm_g': 'new_m', 'new_m_lnm_b': 'new_m', 'new_m_ffn2_w_in': 'new_m', 'new_m_ffn2_w_out': 'new_m', 'new_m_ln2_g': 'new_m', 'new_m_ln2_b': 'new_m', 'new_m_fox_w_in': 'new_m', 'new_m_fox_b_f': 'new_m', 'new_m_fox_w_o': 'new_m', 'new_m_s5_a_re': 'new_m', 'new_m_s5_a_im': 'new_m', 'new_m_s5_log_dt': 'new_m', 'new_m_s5_b_re': 'new_m', 'new_m_s5_b_im': 'new_m', 'new_m_s5_c_re': 'new_m', 'new_m_s5_c_im': 'new_m', 'new_m_s5_d': 'new_m', 'new_m_s5_w_out': 'new_m', 'new_v_ffn1_w_in': 'new_v', 'new_v_ffn1_w_out': 'new_v', 'new_v_ln1_g': 'new_v', 'new_v_ln1_b': 'new_v', 'new_v_lnm_g': 'new_v', 'new_v_lnm_b': 'new_v', 'new_v_ffn2_w_in': 'new_v', 'new_v_ffn2_w_out': 'new_v', 'new_v_ln2_g': 'new_v', 'new_v_ln2_b': 'new_v', 'new_v_fox_w_in': 'new_v', 'new_v_fox_b_f': 'new_v', 'new_v_fox_w_o': 'new_v', 'new_v_s5_a_re': 'new_v', 'new_v_s5_a_im': 'new_v', 'new_v_s5_log_dt': 'new_v', 'new_v_s5_b_re': 'new_v', 'new_v_s5_b_im': 'new_v', 'new_v_s5_c_re': 'new_v', 'new_v_s5_c_im': 'new_v', 'new_v_s5_d': 'new_v', 'new_v_s5_w_out': 'new_v'}


def _forward(args):
    return _fwd_reference(*[args[k] for k in FWD_PARAMS])


def _output_shape():
    def fwd():
        inp = _fwd_setup_inputs(0)
        return _fwd_reference(*[inp[k] for k in FWD_PARAMS])
    out = _jax.eval_shape(fwd)
    return out.shape, out.dtype

N_MICROBATCH = 1
ADAM_LR = 0.001
ADAM_B1 = 0.9
ADAM_B2 = 0.999
ADAM_EPS = 1e-08
ADAM_WD = 0.01
ADAM_STEP = 10
PER_EXAMPLE_BATCH_AXIS = {'x': 0, 'loss_target': 0}
SHARED_INPUTS = []
_WEIGHT_DTYPES = {'ffn1_w_in': _jnp.float32, 'ffn1_w_out': _jnp.float32, 'ln1_g': _jnp.float32, 'ln1_b': _jnp.float32, 'lnm_g': _jnp.float32, 'lnm_b': _jnp.float32, 'ffn2_w_in': _jnp.float32, 'ffn2_w_out': _jnp.float32, 'ln2_g': _jnp.float32, 'ln2_b': _jnp.float32, 'fox_w_in': _jnp.float32, 'fox_b_f': _jnp.float32, 'fox_w_o': _jnp.float32, 's5_a_re': _jnp.float32, 's5_a_im': _jnp.float32, 's5_log_dt': _jnp.float32, 's5_b_re': _jnp.float32, 's5_b_im': _jnp.float32, 's5_c_re': _jnp.float32, 's5_c_im': _jnp.float32, 's5_d': _jnp.float32, 's5_w_out': _jnp.float32}
MOMENT_SCALE = {'ffn1_w_in': 8.170821e-03, 'ffn1_w_out': 3.173540e-02, 'ln1_g': 1.207806e+00, 'ln1_b': 5.591849e-01, 'lnm_g': 1.215339e+00, 'lnm_b': 5.570332e-01, 'ffn2_w_in': 8.161408e-03, 'ffn2_w_out': 3.171535e-02, 'ln2_g': 1.615271e+01, 'ln2_b': 1.441108e+00, 'fox_w_in': 9.129353e-03, 'fox_b_f': 2.572587e-02, 'fox_w_o': 1.444380e-02, 's5_a_re': 9.450437e-04, 's5_a_im': 9.264944e-04, 's5_log_dt': 9.353399e-01, 's5_b_re': 5.713001e-04, 's5_b_im': 5.764562e-04, 's5_c_re': 1.146907e-03, 's5_c_im': 1.160994e-03, 's5_d': 2.606371e-02, 's5_w_out': 4.084557e-02}


def _to_microbatches(a, axis):
    t = _jnp.moveaxis(a, axis, 0)
    t = t.reshape((N_MICROBATCH, t.shape[0] // N_MICROBATCH) + t.shape[1:])
    return _jnp.moveaxis(t, 1, axis + 1)


def setup_inputs(seed: int = 0) -> dict:
    inp = _fwd_setup_inputs(seed)
    key = _jax.random.fold_in(_jax.random.key(seed), 7919)
    shape, _ = _output_shape()
    out = dict(inp)
    out["loss_target"] = _jax.random.normal(_jax.random.fold_in(key, 0), shape, _jnp.float32)
    for i, name in enumerate(TWIN_WEIGHTS):
        w = inp[name].astype(_jnp.float32)
        if MOMENT_SCALE is None:
            s = _jnp.sqrt(_jnp.mean(_jnp.square(w)) + 1e-30)
        else:
            s = MOMENT_SCALE[name]
        km, kv = _jax.random.split(_jax.random.fold_in(key, i + 1))
        out[name] = w
        out["m_" + name] = s * _jax.random.normal(km, w.shape, _jnp.float32)
        out["v_" + name] = (s * s) * _jax.random.uniform(kv, w.shape, _jnp.float32, 0.5, 1.5)
    if N_MICROBATCH > 1:
        for name, axis in PER_EXAMPLE_BATCH_AXIS.items():
            out[name] = _to_microbatches(out[name], axis)
    return {'x': out['x'], 'ffn1_w_in': out['ffn1_w_in'], 'ffn1_w_out': out['ffn1_w_out'], 'ln1_g': out['ln1_g'], 'ln1_b': out['ln1_b'], 'lnm_g': out['lnm_g'], 'lnm_b': out['lnm_b'], 'ffn2_w_in': out['ffn2_w_in'], 'ffn2_w_out': out['ffn2_w_out'], 'ln2_g': out['ln2_g'], 'ln2_b': out['ln2_b'], 'fox_w_in': out['fox_w_in'], 'fox_b_f': out['fox_b_f'], 'fox_w_o': out['fox_w_o'], 's5_a_re': out['s5_a_re'], 's5_a_im': out['s5_a_im'], 's5_log_dt': out['s5_log_dt'], 's5_b_re': out['s5_b_re'], 's5_b_im': out['s5_b_im'], 's5_c_re': out['s5_c_re'], 's5_c_im': out['s5_c_im'], 's5_d': out['s5_d'], 's5_w_out': out['s5_w_out'], 'loss_target': out['loss_target'], 'm_ffn1_w_in': out['m_ffn1_w_in'], 'm_ffn1_w_out': out['m_ffn1_w_out'], 'm_ln1_g': out['m_ln1_g'], 'm_ln1_b': out['m_ln1_b'], 'm_lnm_g': out['m_lnm_g'], 'm_lnm_b': out['m_lnm_b'], 'm_ffn2_w_in': out['m_ffn2_w_in'], 'm_ffn2_w_out': out['m_ffn2_w_out'], 'm_ln2_g': out['m_ln2_g'], 'm_ln2_b': out['m_ln2_b'], 'm_fox_w_in': out['m_fox_w_in'], 'm_fox_b_f': out['m_fox_b_f'], 'm_fox_w_o': out['m_fox_w_o'], 'm_s5_a_re': out['m_s5_a_re'], 'm_s5_a_im': out['m_s5_a_im'], 'm_s5_log_dt': out['m_s5_log_dt'], 'm_s5_b_re': out['m_s5_b_re'], 'm_s5_b_im': out['m_s5_b_im'], 'm_s5_c_re': out['m_s5_c_re'], 'm_s5_c_im': out['m_s5_c_im'], 'm_s5_d': out['m_s5_d'], 'm_s5_w_out': out['m_s5_w_out'], 'v_ffn1_w_in': out['v_ffn1_w_in'], 'v_ffn1_w_out': out['v_ffn1_w_out'], 'v_ln1_g': out['v_ln1_g'], 'v_ln1_b': out['v_ln1_b'], 'v_lnm_g': out['v_lnm_g'], 'v_lnm_b': out['v_lnm_b'], 'v_ffn2_w_in': out['v_ffn2_w_in'], 'v_ffn2_w_out': out['v_ffn2_w_out'], 'v_ln2_g': out['v_ln2_g'], 'v_ln2_b': out['v_ln2_b'], 'v_fox_w_in': out['v_fox_w_in'], 'v_fox_b_f': out['v_fox_b_f'], 'v_fox_w_o': out['v_fox_w_o'], 'v_s5_a_re': out['v_s5_a_re'], 'v_s5_a_im': out['v_s5_a_im'], 'v_s5_log_dt': out['v_s5_log_dt'], 'v_s5_b_re': out['v_s5_b_re'], 'v_s5_b_im': out['v_s5_b_im'], 'v_s5_c_re': out['v_s5_c_re'], 'v_s5_c_im': out['v_s5_c_im'], 'v_s5_d': out['v_s5_d'], 'v_s5_w_out': out['v_s5_w_out']}


def _loss(weights, diff, rest, loss_target):
    with _jax.named_scope("forward"):
        args = {**rest, TWIN_DIFF_INPUT: diff, **{k: w.astype(_WEIGHT_DTYPES[k]) for k, w in weights.items()}}
        y = _forward(args)
    with _jax.named_scope("loss_head"):
        err = _jnp.square(y.astype(_jnp.float32) - loss_target)
        return 0.5 * _jnp.sum(_jnp.mean(err, axis=-1)) if err.ndim else 0.5 * err


def _adamw(w, g, m, v):
    m = ADAM_B1 * m + (1.0 - ADAM_B1) * g
    v = ADAM_B2 * v + (1.0 - ADAM_B2) * _jnp.square(g)
    m_hat = m / (1.0 - ADAM_B1 ** ADAM_STEP)
    v_hat = v / (1.0 - ADAM_B2 ** ADAM_STEP)
    delta = -ADAM_LR * (m_hat / (_jnp.sqrt(v_hat) + ADAM_EPS) + ADAM_WD * w)
    return delta, m, v


def reference(x, ffn1_w_in, ffn1_w_out, ln1_g, ln1_b, lnm_g, lnm_b, ffn2_w_in, ffn2_w_out, ln2_g, ln2_b, fox_w_in, fox_b_f, fox_w_o, s5_a_re, s5_a_im, s5_log_dt, s5_b_re, s5_b_im, s5_c_re, s5_c_im, s5_d, s5_w_out, loss_target, m_ffn1_w_in, m_ffn1_w_out, m_ln1_g, m_ln1_b, m_lnm_g, m_lnm_b, m_ffn2_w_in, m_ffn2_w_out, m_ln2_g, m_ln2_b, m_fox_w_in, m_fox_b_f, m_fox_w_o, m_s5_a_re, m_s5_a_im, m_s5_log_dt, m_s5_b_re, m_s5_b_im, m_s5_c_re, m_s5_c_im, m_s5_d, m_s5_w_out, v_ffn1_w_in, v_ffn1_w_out, v_ln1_g, v_ln1_b, v_lnm_g, v_lnm_b, v_ffn2_w_in, v_ffn2_w_out, v_ln2_g, v_ln2_b, v_fox_w_in, v_fox_b_f, v_fox_w_o, v_s5_a_re, v_s5_a_im, v_s5_log_dt, v_s5_b_re, v_s5_b_im, v_s5_c_re, v_s5_c_im, v_s5_d, v_s5_w_out):
    given = dict(x=x, ffn1_w_in=ffn1_w_in, ffn1_w_out=ffn1_w_out, ln1_g=ln1_g, ln1_b=ln1_b, lnm_g=lnm_g, lnm_b=lnm_b, ffn2_w_in=ffn2_w_in, ffn2_w_out=ffn2_w_out, ln2_g=ln2_g, ln2_b=ln2_b, fox_w_in=fox_w_in, fox_b_f=fox_b_f, fox_w_o=fox_w_o, s5_a_re=s5_a_re, s5_a_im=s5_a_im, s5_log_dt=s5_log_dt, s5_b_re=s5_b_re, s5_b_im=s5_b_im, s5_c_re=s5_c_re, s5_c_im=s5_c_im, s5_d=s5_d, s5_w_out=s5_w_out, loss_target=loss_target, m_ffn1_w_in=m_ffn1_w_in, m_ffn1_w_out=m_ffn1_w_out, m_ln1_g=m_ln1_g, m_ln1_b=m_ln1_b, m_lnm_g=m_lnm_g, m_lnm_b=m_lnm_b, m_ffn2_w_in=m_ffn2_w_in, m_ffn2_w_out=m_ffn2_w_out, m_ln2_g=m_ln2_g, m_ln2_b=m_ln2_b, m_fox_w_in=m_fox_w_in, m_fox_b_f=m_fox_b_f, m_fox_w_o=m_fox_w_o, m_s5_a_re=m_s5_a_re, m_s5_a_im=m_s5_a_im, m_s5_log_dt=m_s5_log_dt, m_s5_b_re=m_s5_b_re, m_s5_b_im=m_s5_b_im, m_s5_c_re=m_s5_c_re, m_s5_c_im=m_s5_c_im, m_s5_d=m_s5_d, m_s5_w_out=m_s5_w_out, v_ffn1_w_in=v_ffn1_w_in, v_ffn1_w_out=v_ffn1_w_out, v_ln1_g=v_ln1_g, v_ln1_b=v_ln1_b, v_lnm_g=v_lnm_g, v_lnm_b=v_lnm_b, v_ffn2_w_in=v_ffn2_w_in, v_ffn2_w_out=v_ffn2_w_out, v_ln2_g=v_ln2_g, v_ln2_b=v_ln2_b, v_fox_w_in=v_fox_w_in, v_fox_b_f=v_fox_b_f, v_fox_w_o=v_fox_w_o, v_s5_a_re=v_s5_a_re, v_s5_a_im=v_s5_a_im, v_s5_log_dt=v_s5_log_dt, v_s5_b_re=v_s5_b_re, v_s5_b_im=v_s5_b_im, v_s5_c_re=v_s5_c_re, v_s5_c_im=v_s5_c_im, v_s5_d=v_s5_d, v_s5_w_out=v_s5_w_out)
    weights = {n: given[n] for n in TWIN_WEIGHTS}
    shared = {n: given[n] for n in SHARED_INPUTS}
    per_example = {n: given[n] for n in ['x']}
    grad_fn = _jax.value_and_grad(_loss, argnums=(0, 1))

    def one_microbatch(ex, loss_target):
        ex = dict(ex)
        diff = ex.pop(TWIN_DIFF_INPUT)
        return grad_fn(weights, diff, {**shared, **ex}, loss_target)

    if N_MICROBATCH == 1:
        loss, (grad_w, grad_x) = one_microbatch(per_example, given["loss_target"])
    else:
        def body(carry, xs):
            loss_sum, grad_sum = carry
            l_k, (gw_k, gx_k) = one_microbatch(xs[0], xs[1])
            with _jax.named_scope("update"):
                return (loss_sum + l_k, _jax.tree.map(_jnp.add, grad_sum, gw_k)), gx_k

        init = (_jnp.zeros((), _jnp.float32), _jax.tree.map(_jnp.zeros_like, weights))
        (loss, grad_w), grad_x = _jax.lax.scan(body, init, (per_example, given["loss_target"]))
    with _jax.named_scope("update"):
        delta_w, new_m, new_v = {}, {}, {}
        for n in TWIN_WEIGHTS:
            delta_w[n], new_m[n], new_v[n] = _adamw(weights[n], grad_w[n], given["m_" + n], given["v_" + n])
    return (loss, grad_x, *[grad_w[n] for n in TWIN_WEIGHTS], *[delta_w[n] for n in TWIN_WEIGHTS],
            *[new_m[n] for n in TWIN_WEIGHTS], *[new_v[n] for n in TWIN_WEIGHTS])
```

```python
import functools
import math

import jax
import jax.numpy as jnp
from jax import lax
from jax.experimental import pallas as pl
from jax.experimental.pallas import tpu as pltpu

F32 = jnp.float32
BF16 = jnp.bfloat16

N_DEV = 8
HEAD_DIM = 64
S5_GROUP = 16
S5_STATE = 64
LANES = 128
SUBLANES = 8
S5_PART = 256
LN_EPS = 1e-5
NEG_INF = -1e30
ADAM_LR, ADAM_B1, ADAM_B2, ADAM_EPS, ADAM_WD, ADAM_STEP = 0.001, 0.9, 0.999, 1e-08, 0.01, 10
VMEM_LIMIT = 48 * 1024 * 1024
PACK_COLS = 1024
PACK_ROW_ALIGN = 512

MESH = pl.DeviceIdType.MESH


def _tile(dim, pref, align=LANES):
    if dim <= pref:
        return dim
    t = (pref // align) * align
    while t >= align:
        if dim % t == 0:
            return t
        t -= align
    return dim


def _params(sem):
    return pltpu.CompilerParams(dimension_semantics=sem, vmem_limit_bytes=VMEM_LIMIT)


def _mm(a, b, *, ta=False, tb=False, out_dtype=F32, scale=None, add=None, add_scale=1.0,
        tm=512, tn=1408, tk=1408, name="mm"):
    if ta:
        K, M = a.shape
    else:
        M, K = a.shape
    if tb:
        N, K2 = b.shape
    else:
        K2, N = b.shape
    assert K == K2, (a.shape, b.shape, ta, tb)
    tm, tn, tk = _tile(M, tm), _tile(N, tn), _tile(K, tk)
    nk = K // tk

    def body(*refs):
        if add is None:
            a_ref, b_ref, o_ref, acc = refs
            add_ref = None
        else:
            a_ref, b_ref, add_ref, o_ref, acc = refs
        k = pl.program_id(2)

        @pl.when(k == 0)
        def _():
            acc[...] = jnp.zeros_like(acc)

        dims = (((0 if ta else 1,), (1 if tb else 0,)), ((), ()))
        acc[...] += lax.dot_general(a_ref[...].astype(BF16), b_ref[...].astype(BF16), dims,
                                    preferred_element_type=F32)

        @pl.when(k == nk - 1)
        def _():
            r = acc[...]
            if scale is not None:
                r = r * scale
            if add_ref is not None:
                r = r + add_scale * add_ref[...]
            o_ref[...] = r.astype(out_dtype)

    a_spec = pl.BlockSpec((tk, tm), lambda i, j, k: (k, i)) if ta else pl.BlockSpec((tm, tk), lambda i, j, k: (i, k))
    b_spec = pl.BlockSpec((tn, tk), lambda i, j, k: (j, k)) if tb else pl.BlockSpec((tk, tn), lambda i, j, k: (k, j))
    o_spec = pl.BlockSpec((tm, tn), lambda i, j, k: (i, j))
    in_specs = [a_spec, b_spec]
    args = [a, b]
    if add is not None:
        in_specs.append(o_spec)
        args.append(add)
    return pl.pallas_call(
        body, name=name, out_shape=jax.ShapeDtypeStruct((M, N), out_dtype),
        grid=(M // tm, N // tn, nk), in_specs=in_specs, out_specs=o_spec,
        scratch_shapes=[pltpu.VMEM((tm, tn), F32)],
        compiler_params=_params(("parallel", "parallel", "arbitrary")),
    )(*args)


def _res_ln_fwd(x, y, g, b, alpha, s, name):
    S, D = x.shape
    tm = _tile(S, 256, SUBLANES)

    def body(x_ref, y_ref, g_ref, b_ref, o_ref, xh_ref, r_ref):
        z = alpha * x_ref[...] + s * y_ref[...]
        mu = jnp.mean(z, axis=-1, keepdims=True)
        zc = z - mu
        var = jnp.mean(zc * zc, axis=-1, keepdims=True)
        rstd = lax.rsqrt(var + LN_EPS)
        xh = zc * rstd
        xh_ref[...] = xh
        r_ref[...] = rstd
        o_ref[...] = xh * g_ref[...] + b_ref[...]

    row = pl.BlockSpec((tm, D), lambda i: (i, 0))
    vec = pl.BlockSpec((1, D), lambda i: (0, 0))
    return pl.pallas_call(
        body, name=name,
        out_shape=(jax.ShapeDtypeStruct((S, D), F32), jax.ShapeDtypeStruct((S, D), F32),
                   jax.ShapeDtypeStruct((S, 1), F32)),
        grid=(S // tm,), in_specs=[row, row, vec, vec],
        out_specs=(row, row, pl.BlockSpec((tm, 1), lambda i: (i, 0))),
        compiler_params=_params(("parallel",)),
    )(x, y, g.reshape(1, D), b.reshape(1, D))


def _ln_bwd(dout, xh, rstd, g, name):
    S, D = dout.shape
    tm = _tile(S, 256, SUBLANES)

    def body(d_ref, xh_ref, r_ref, g_ref, dz_ref, dg_ref, db_ref):
        i = pl.program_id(0)

        @pl.when(i == 0)
        def _():
            dg_ref[...] = jnp.zeros_like(dg_ref)
            db_ref[...] = jnp.zeros_like(db_ref)

        d = d_ref[...]
        xhv = xh_ref[...]
        dxh = d * g_ref[...]
        m1 = jnp.mean(dxh, axis=-1, keepdims=True)
        m2 = jnp.mean(dxh * xhv, axis=-1, keepdims=True)
        dz_ref[...] = r_ref[...] * (dxh - m1 - xhv * m2)
        dg_ref[...] += jnp.sum(d * xhv, axis=0, keepdims=True)
        db_ref[...] += jnp.sum(d, axis=0, keepdims=True)

    row = pl.BlockSpec((tm, D), lambda i: (i, 0))
    vec = pl.BlockSpec((1, D), lambda i: (0, 0))
    dz, dg, db = pl.pallas_call(
        body, name=name,
        out_shape=(jax.ShapeDtypeStruct((S, D), F32), jax.ShapeDtypeStruct((1, D), F32),
                   jax.ShapeDtypeStruct((1, D), F32)),
        grid=(S // tm,), in_specs=[row, row, pl.BlockSpec((tm, 1), lambda i: (i, 0)), vec],
        out_specs=(row, vec, vec),
        compiler_params=_params(("arbitrary",)),
    )(dout, xh, rstd, g.reshape(1, D))
    return dz, dg[0], db[0]


def _sigmoid(x):
    e = jnp.exp(-jnp.abs(x))
    return jnp.where(x >= 0, 1.0 / (1.0 + e), e / (1.0 + e))


def _swiglu_fwd(h, name):
    S, F2 = h.shape
    F = F2 // 2
    tm = _tile(S, 512, SUBLANES)
    tf = _tile(F, 1408)
    nf = F // tf

    def body(g_ref, u_ref, o_ref):
        g = g_ref[...].astype(F32)
        u = u_ref[...].astype(F32)
        o_ref[...] = (g * _sigmoid(g) * u).astype(BF16)

    return pl.pallas_call(
        body, name=name, out_shape=jax.ShapeDtypeStruct((S, F), BF16),
        grid=(S // tm, nf),
        in_specs=[pl.BlockSpec((tm, tf), lambda i, j: (i, j)),
                  pl.BlockSpec((tm, tf), lambda i, j: (i, j + nf))],
        out_specs=pl.BlockSpec((tm, tf), lambda i, j: (i, j)),
        compiler_params=_params(("parallel", "parallel")),
    )(h, h)


def _swiglu_bwd(h, da, name):
    S, F2 = h.shape
    F = F2 // 2
    tm = _tile(S, 512, SUBLANES)
    tf = _tile(F, 1408)
    nf = F // tf

    def body(g_ref, u_ref, da_ref, dg_ref, du_ref):
        g = g_ref[...].astype(F32)
        u = u_ref[...].astype(F32)
        da_v = da_ref[...]
        sg = _sigmoid(g)
        silu = g * sg
        dg_ref[...] = (da_v * u * (sg + silu * (1.0 - sg))).astype(BF16)
        du_ref[...] = (da_v * silu).astype(BF16)

    blk = pl.BlockSpec((tm, tf), lambda i, j: (i, j))
    blk_up = pl.BlockSpec((tm, tf), lambda i, j: (i, j + nf))
    dgate, dup = pl.pallas_call(
        body, name=name,
        out_shape=(jax.ShapeDtypeStruct((S, F), BF16), jax.ShapeDtypeStruct((S, F), BF16)),
        grid=(S // tm, nf), in_specs=[blk, blk_up, blk], out_specs=(blk, blk),
        compiler_params=_params(("parallel", "parallel")),
    )(h, h, da)
    return jnp.concatenate([dgate, dup], axis=1)


def _glu_fwd(vg, name):
    S, D2 = vg.shape
    D = D2 // 2
    tm = _tile(S, 512, SUBLANES)

    def body(v_ref, g_ref, o_ref):
        o_ref[...] = v_ref[...] * _sigmoid(g_ref[...])

    return pl.pallas_call(
        body, name=name, out_shape=jax.ShapeDtypeStruct((S, D), F32), grid=(S // tm,),
        in_specs=[pl.BlockSpec((tm, D), lambda i: (i, 0)), pl.BlockSpec((tm, D), lambda i: (i, 1))],
        out_specs=pl.BlockSpec((tm, D), lambda i: (i, 0)),
        compiler_params=_params(("parallel",)),
    )(vg, vg)


def _glu_bwd(vg, dm, name):
    S, D2 = vg.shape
    D = D2 // 2
    tm = _tile(S, 512, SUBLANES)

    def body(v_ref, g_ref, dm_ref, dv_ref, dg_ref):
        sg = _sigmoid(g_ref[...])
        d = dm_ref[...]
        dv_ref[...] = (d * sg).astype(BF16)
        dg_ref[...] = (d * v_ref[...] * sg * (1.0 - sg)).astype(BF16)

    blk = pl.BlockSpec((tm, D), lambda i: (i, 0))
    dv, dg = pl.pallas_call(
        body, name=name,
        out_shape=(jax.ShapeDtypeStruct((S, D), BF16), jax.ShapeDtypeStruct((S, D), BF16)),
        grid=(S // tm,), in_specs=[blk, pl.BlockSpec((tm, D), lambda i: (i, 1)), blk],
        out_specs=(blk, blk), compiler_params=_params(("parallel",)),
    )(vg, vg, dm)
    return jnp.concatenate([dv, dg], axis=1)


def _loss_fwd_bwd(y, target):
    S, D = y.shape
    tm = _tile(S, 256, SUBLANES)

    def body(y_ref, t_ref, dy_ref, l_ref):
        i = pl.program_id(0)

        @pl.when(i == 0)
        def _():
            l_ref[...] = jnp.zeros_like(l_ref)

        e = y_ref[...] - t_ref[...]
        dy_ref[...] = e * (1.0 / D)
        l_ref[...] += jnp.sum(e * e, axis=0, keepdims=True) * (0.5 / D)

    row = pl.BlockSpec((tm, D), lambda i: (i, 0))
    dy, part = pl.pallas_call(
        body, name="loss", out_shape=(jax.ShapeDtypeStruct((S, D), F32), jax.ShapeDtypeStruct((1, D), F32)),
        grid=(S // tm,), in_specs=[row, row], out_specs=(row, pl.BlockSpec((1, D), lambda i: (0, 0))),
        compiler_params=_params(("arbitrary",)),
    )(y, target)
    return dy, jnp.sum(part)


def _tri(n, lower):
    r = lax.broadcasted_iota(jnp.int32, (n, n), 0)
    c = lax.broadcasted_iota(jnp.int32, (n, n), 1)
    return jnp.where((c <= r) if lower else (c >= r), 1.0, 0.0)


def _fox_gate_fwd(fl, bf):
    S, W = fl.shape
    tm = _tile(S, 256, SUBLANES)

    def body(fl_ref, b_ref, c_ref, carry):
        i = pl.program_id(0)

        @pl.when(i == 0)
        def _():
            carry[...] = jnp.zeros_like(carry)

        x = fl_ref[...] + b_ref[...]
        lf = jnp.minimum(x, 0.0) - jnp.log(1.0 + jnp.exp(-jnp.abs(x)))
        c_ref[...] = jnp.dot(_tri(tm, True), lf, precision=lax.Precision.HIGHEST,
                             preferred_element_type=F32) + carry[...]
        carry[...] += jnp.sum(lf, axis=0, keepdims=True)

    blk = pl.BlockSpec((tm, W), lambda i: (i, 0))
    return pl.pallas_call(
        body, name="fox_gate_fwd", out_shape=jax.ShapeDtypeStruct((S, W), F32), grid=(S // tm,),
        in_specs=[blk, pl.BlockSpec((1, W), lambda i: (0, 0))], out_specs=blk,
        scratch_shapes=[pltpu.VMEM((1, W), F32)], compiler_params=_params(("arbitrary",)),
    )(fl, bf)


def _fox_gate_bwd(dcum, fl, bf):
    S, W = fl.shape
    tm = _tile(S, 256, SUBLANES)
    nb = S // tm

    def body(dc_ref, fl_ref, b_ref, dfl_ref, db_ref, carry):
        i = pl.program_id(0)

        @pl.when(i == 0)
        def _():
            carry[...] = jnp.zeros_like(carry)
            db_ref[...] = jnp.zeros_like(db_ref)

        dc = dc_ref[...]
        r = jnp.dot(_tri(tm, False), dc, precision=lax.Precision.HIGHEST, preferred_element_type=F32) + carry[...]
        carry[...] += jnp.sum(dc, axis=0, keepdims=True)
        x = fl_ref[...] + b_ref[...]
        dfl = r * (1.0 - _sigmoid(x))
        dfl_ref[...] = dfl
        db_ref[...] += jnp.sum(dfl, axis=0, keepdims=True)

    blk = pl.BlockSpec((tm, W), lambda i: (nb - 1 - i, 0))
    vec = pl.BlockSpec((1, W), lambda i: (0, 0))
    return pl.pallas_call(
        body, name="fox_gate_bwd",
        out_shape=(jax.ShapeDtypeStruct((S, W), F32), jax.ShapeDtypeStruct((1, W), F32)),
        grid=(nb,), in_specs=[blk, blk, vec], out_specs=(blk, vec),
        scratch_shapes=[pltpu.VMEM((1, W), F32)], compiler_params=_params(("arbitrary",)),
    )(dcum, fl, bf)


def _scores(q_ref, k_ref, cq_ref, ck_ref, qi, ki, tq, tk, scale):
    s = lax.dot_general(q_ref[0], k_ref[0], (((1,), (1,)), ((), ())), preferred_element_type=F32) * scale
    s = s + (cq_ref[0] - ck_ref[0])
    row = qi * tq + lax.broadcasted_iota(jnp.int32, (tq, tk), 0)
    col = ki * tk + lax.broadcasted_iota(jnp.int32, (tq, tk), 1)
    return jnp.where(col <= row, s, NEG_INF)


def _flash_fwd(q, k, v, cq, ck):
    H, S, hd = q.shape
    tq = tk = _tile(S, 512)
    nq, nk = S // tq, S // tk
    scale = 1.0 / math.sqrt(hd)

    def body(q_ref, k_ref, v_ref, cq_ref, ck_ref, o_ref, lse_ref, m_s, l_s, acc):
        qi, ki = pl.program_id(1), pl.program_id(2)

        @pl.when(ki == 0)
        def _():
            m_s[...] = jnp.full_like(m_s, NEG_INF)
            l_s[...] = jnp.zeros_like(l_s)
            acc[...] = jnp.zeros_like(acc)

        @pl.when(ki <= qi)
        def _():
            s = _scores(q_ref, k_ref, cq_ref, ck_ref, qi, ki, tq, tk, scale)
            m_old = m_s[...]
            m_new = jnp.maximum(m_old, jnp.max(s, axis=1, keepdims=True))
            p = jnp.exp(s - m_new)
            corr = jnp.exp(m_old - m_new)
            l_s[...] = corr * l_s[...] + jnp.sum(p, axis=1, keepdims=True)
            acc[...] = corr * acc[...] + jnp.dot(p.astype(BF16), v_ref[0], preferred_element_type=F32)
            m_s[...] = m_new

        @pl.when(ki == nk - 1)
        def _():
            o_ref[0] = acc[...] / l_s[...]
            lse_ref[0] = m_s[...] + jnp.log(l_s[...])

    qs = pl.BlockSpec((1, tq, hd), lambda h, i, j: (h, i, 0))
    ks = pl.BlockSpec((1, tk, hd), lambda h, i, j: (h, jnp.minimum(j, i), 0))
    return pl.pallas_call(
        body, name="fox_attn_fwd",
        out_shape=(jax.ShapeDtypeStruct((H, S, hd), F32), jax.ShapeDtypeStruct((H, S, 1), F32)),
        grid=(H, nq, nk),
        in_specs=[qs, ks, ks, pl.BlockSpec((1, tq, 1), lambda h, i, j: (h, i, 0)),
                  pl.BlockSpec((1, 1, tk), lambda h, i, j: (h, 0, jnp.minimum(j, i)))],
        out_specs=(qs, pl.BlockSpec((1, tq, 1), lambda h, i, j: (h, i, 0))),
        scratch_shapes=[pltpu.VMEM((tq, 1), F32), pltpu.VMEM((tq, 1), F32), pltpu.VMEM((tq, hd), F32)],
        compiler_params=_params(("parallel", "parallel", "arbitrary")),
    )(q, k, v, cq, ck)


def _flash_bwd_dq(q, k, v, cq, ck, o, lse, do):
    H, S, hd = q.shape
    tq = tk = _tile(S, 512)
    nq, nk = S // tq, S // tk
    scale = 1.0 / math.sqrt(hd)

    def body(q_ref, k_ref, v_ref, cq_ref, ck_ref, o_ref, lse_ref, do_ref, dq_ref, dl_ref, dc_ref, acc, dc_acc):
        qi, ki = pl.program_id(1), pl.program_id(2)

        @pl.when(ki == 0)
        def _():
            acc[...] = jnp.zeros_like(acc)
            dc_acc[...] = jnp.zeros_like(dc_acc)
            dl_ref[0] = jnp.sum(do_ref[0] * o_ref[0], axis=1, keepdims=True)

        @pl.when(ki <= qi)
        def _():
            s = _scores(q_ref, k_ref, cq_ref, ck_ref, qi, ki, tq, tk, scale)
            p = jnp.exp(s - lse_ref[0])
            dp = lax.dot_general(do_ref[0].astype(BF16), v_ref[0], (((1,), (1,)), ((), ())),
                                 preferred_element_type=F32)
            ds = p * (dp - dl_ref[0])
            acc[...] += jnp.dot(ds.astype(BF16), k_ref[0], preferred_element_type=F32)
            dc_acc[...] += jnp.sum(ds, axis=1, keepdims=True)

        @pl.when(ki == nk - 1)
        def _():
            dq_ref[0] = acc[...] * scale
            dc_ref[0] = dc_acc[...]

    qs = pl.BlockSpec((1, tq, hd), lambda h, i, j: (h, i, 0))
    ks = pl.BlockSpec((1, tk, hd), lambda h, i, j: (h, jnp.minimum(j, i), 0))
    q1 = pl.BlockSpec((1, tq, 1), lambda h, i, j: (h, i, 0))
    return pl.pallas_call(
        body, name="fox_attn_bwd_dq",
        out_shape=(jax.ShapeDtypeStruct((H, S, hd), F32), jax.ShapeDtypeStruct((H, S, 1), F32),
                   jax.ShapeDtypeStruct((H, S, 1), F32)),
        grid=(H, nq, nk),
        in_specs=[qs, ks, ks, q1, pl.BlockSpec((1, 1, tk), lambda h, i, j: (h, 0, jnp.minimum(j, i))),
                  qs, q1, qs],
        out_specs=(qs, q1, q1), scratch_shapes=[pltpu.VMEM((tq, hd), F32), pltpu.VMEM((tq, 1), F32)],
        compiler_params=_params(("parallel", "parallel", "arbitrary")),
    )(q, k, v, cq, ck, o, lse, do)


def _flash_bwd_dkv(q, k, v, cq, ck, lse, do, delta):
    H, S, hd = q.shape
    tq = tk = _tile(S, 512)
    nq, nk = S // tq, S // tk
    scale = 1.0 / math.sqrt(hd)

    def body(q_ref, k_ref, v_ref, cq_ref, ck_ref, lse_ref, do_ref, dl_ref, dk_ref, dv_ref, dc_ref,
             dk_acc, dv_acc, dc_acc):
        ki, qi = pl.program_id(1), pl.program_id(2)

        @pl.when(qi == 0)
        def _():
            dk_acc[...] = jnp.zeros_like(dk_acc)
            dv_acc[...] = jnp.zeros_like(dv_acc)
            dc_acc[...] = jnp.zeros_like(dc_acc)

        @pl.when(qi >= ki)
        def _():
            s = _scores(q_ref, k_ref, cq_ref, ck_ref, qi, ki, tq, tk, scale)
            p = jnp.exp(s - lse_ref[0])
            dob = do_ref[0].astype(BF16)
            dv_acc[...] += lax.dot_general(p.astype(BF16), dob, (((0,), (0,)), ((), ())),
                                           preferred_element_type=F32)
            dp = lax.dot_general(dob, v_ref[0], (((1,), (1,)), ((), ())), preferred_element_type=F32)
            ds = p * (dp - dl_ref[0])
            dk_acc[...] += lax.dot_general(ds.astype(BF16), q_ref[0], (((0,), (0,)), ((), ())),
                                           preferred_element_type=F32)
            dc_acc[...] -= jnp.sum(ds, axis=0, keepdims=True)

        @pl.when(qi == nq - 1)
        def _():
            dk_ref[0] = dk_acc[...] * scale
            dv_ref[0] = dv_acc[...]
            dc_ref[0] = dc_acc[...]

    qs = pl.BlockSpec((1, tq, hd), lambda h, j, i: (h, jnp.maximum(i, j), 0))
    q1 = pl.BlockSpec((1, tq, 1), lambda h, j, i: (h, jnp.maximum(i, j), 0))
    ks = pl.BlockSpec((1, tk, hd), lambda h, j, i: (h, j, 0))
    k1 = pl.BlockSpec((1, 1, tk), lambda h, j, i: (h, 0, j))
    return pl.pallas_call(
        body, name="fox_attn_bwd_dkv",
        out_shape=(jax.ShapeDtypeStruct((H, S, hd), F32), jax.ShapeDtypeStruct((H, S, hd), F32),
                   jax.ShapeDtypeStruct((H, 1, S), F32)),
        grid=(H, nk, nq), in_specs=[qs, ks, ks, q1, k1, q1, qs, q1], out_specs=(ks, ks, k1),
        scratch_shapes=[pltpu.VMEM((tk, hd), F32), pltpu.VMEM((tk, hd), F32), pltpu.VMEM((1, tk), F32)],
        compiler_params=_params(("parallel", "parallel", "arbitrary")),
    )(q, k, v, cq, ck, lse, do, delta)


def _s5_consts(T):
    rows = T * SUBLANES
    rr = lax.broadcasted_iota(jnp.int32, (rows, T), 0)
    tt = lax.broadcasted_iota(jnp.int32, (rows, T), 1)
    rep = jnp.where(rr // SUBLANES == tt, 1.0, 0.0).astype(BF16)
    r2 = lax.broadcasted_iota(jnp.int32, (rows, S5_PART), 0)
    c2 = lax.broadcasted_iota(jnp.int32, (rows, S5_PART), 1)
    mask = (c2 // (S5_PART // SUBLANES)) == (r2 % SUBLANES)
    return rep, mask


def _gelu(y):
    c = math.sqrt(2.0 / math.pi)
    return 0.5 * y * (1.0 + jnp.tanh(c * (y + 0.044715 * y * y * y)))


def _gelu_grad(y):
    c = math.sqrt(2.0 / math.pi)
    th = jnp.tanh(c * (y + 0.044715 * y * y * y))
    return 0.5 * (1.0 + th) + 0.5 * y * (1.0 - th * th) * c * (1.0 + 3.0 * 0.044715 * y * y)


def _s5_fwd(x, rmat, cmat, lam, dskip):
    S, D = x.shape
    NQ = D // S5_PART
    T = _tile(S, 128, SUBLANES)
    rows = T * SUBLANES

    def body(x_ref, r_ref, c_ref, lam_ref, d_ref, y_ref, yg_ref, h_ref, bu_s, carry):
        i = pl.program_id(1)

        @pl.when(i == 0)
        def _():
            carry[...] = jnp.zeros_like(carry)

        rep, mask = _s5_consts(T)
        xv = x_ref[...]
        xrep = jnp.dot(rep, xv.astype(BF16), preferred_element_type=F32)
        lx = jnp.where(mask, xrep, 0.0).astype(BF16)
        bu_s[...] = jnp.dot(lx, r_ref[0], preferred_element_type=F32)
        ar = lam_ref[0, :, 0:LANES]
        ai = lam_ref[0, :, LANES:2 * LANES]

        def step(t, c):
            hr, hi = c
            o = pl.multiple_of(t * SUBLANES, SUBLANES)
            sl = bu_s[pl.ds(o, SUBLANES), :]
            nhr = ar * hr - ai * hi + sl[:, 0:LANES]
            nhi = ar * hi + ai * hr + sl[:, LANES:2 * LANES]
            h_ref[0, pl.ds(o, SUBLANES), 0:LANES] = nhr
            h_ref[0, pl.ds(o, SUBLANES), LANES:2 * LANES] = nhi
            return nhr, nhi

        hr, hi = lax.fori_loop(0, T, step, (carry[:, 0:LANES], carry[:, LANES:2 * LANES]))
        carry[:, 0:LANES] = hr
        carry[:, LANES:2 * LANES] = hi
        z = jnp.dot(h_ref[0].astype(BF16), c_ref[0], preferred_element_type=F32)
        z = jnp.where(mask, z, 0.0)
        y = jnp.sum(z.reshape(T, SUBLANES, S5_PART), axis=1) + d_ref[...] * xv
        y_ref[...] = y
        yg_ref[...] = _gelu(y).astype(BF16)

    xs = pl.BlockSpec((T, S5_PART), lambda q, i: (i, q))
    ms = pl.BlockSpec((1, S5_PART, S5_PART), lambda q, i: (q, 0, 0))
    return pl.pallas_call(
        body, name="s5_scan_fwd",
        out_shape=(jax.ShapeDtypeStruct((S, D), F32), jax.ShapeDtypeStruct((S, D), BF16),
                   jax.ShapeDtypeStruct((NQ, S * SUBLANES, S5_PART), F32)),
        grid=(NQ, S // T),
        in_specs=[xs, ms, ms, pl.BlockSpec((1, SUBLANES, S5_PART), lambda q, i: (q, 0, 0)),
                  pl.BlockSpec((1, S5_PART), lambda q, i: (0, q))],
        out_specs=(xs, xs, pl.BlockSpec((1, rows, S5_PART), lambda q, i: (q, i, 0))),
        scratch_shapes=[pltpu.VMEM((rows, S5_PART), F32), pltpu.VMEM((SUBLANES, S5_PART), F32)],
        compiler_params=_params(("parallel", "arbitrary")),
    )(x, rmat, cmat, lam, dskip)


def _s5_bwd(x, y, dyg, hs, rmat, cmat, lam, dskip):
    S, D = x.shape
    NQ = D // S5_PART
    T = _tile(S, 128, SUBLANES)
    nb = S // T
    rows = T * SUBLANES

    def body(x_ref, y_ref, dyg_ref, h_ref, hp_ref, r_ref, c_ref, lam_ref, d_ref,
             dx_ref, dr_ref, dc_ref, dlam_ref, dd_ref, dh_s, g_s, hs_s, carry):
        i = pl.program_id(1)

        @pl.when(i == 0)
        def _():
            carry[...] = jnp.zeros_like(carry)
            dr_ref[...] = jnp.zeros_like(dr_ref)
            dc_ref[...] = jnp.zeros_like(dc_ref)
            dlam_ref[...] = jnp.zeros_like(dlam_ref)
            dd_ref[...] = jnp.zeros_like(dd_ref)

        rep, mask = _s5_consts(T)
        xv = x_ref[...]
        dy = dyg_ref[...] * _gelu_grad(y_ref[...])
        dyrep = jnp.dot(rep, dy.astype(BF16), preferred_element_type=F32)
        ldy = jnp.where(mask, dyrep, 0.0).astype(BF16)
        dh_s[...] = lax.dot_general(ldy, c_ref[0], (((1,), (1,)), ((), ())), preferred_element_type=F32)
        ar = lam_ref[0, :, 0:LANES]
        ai = lam_ref[0, :, LANES:2 * LANES]

        def step(n, c):
            gr, gi = c
            o = pl.multiple_of((T - 1 - n) * SUBLANES, SUBLANES)
            sl = dh_s[pl.ds(o, SUBLANES), :]
            ngr = sl[:, 0:LANES] + ar * gr + ai * gi
            ngi = sl[:, LANES:2 * LANES] - ai * gr + ar * gi
            g_s[pl.ds(o, SUBLANES), 0:LANES] = ngr
            g_s[pl.ds(o, SUBLANES), LANES:2 * LANES] = ngi
            return ngr, ngi

        gr, gi = lax.fori_loop(0, T, step, (carry[:, 0:LANES], carry[:, LANES:2 * LANES]))
        carry[:, 0:LANES] = gr
        carry[:, LANES:2 * LANES] = gi

        hv = h_ref[0]
        hs_s[0:SUBLANES, :] = jnp.where(i == nb - 1, 0.0, hp_ref[0])
        hs_s[SUBLANES:rows + SUBLANES, :] = hv
        hprev = hs_s[0:rows, :]
        gv = g_s[...]
        g_re, g_im = gv[:, 0:LANES], gv[:, LANES:2 * LANES]
        hp_re, hp_im = hprev[:, 0:LANES], hprev[:, LANES:2 * LANES]
        dar = jnp.sum((g_re * hp_re + g_im * hp_im).reshape(T, SUBLANES, LANES), axis=0)
        dai = jnp.sum((g_im * hp_re - g_re * hp_im).reshape(T, SUBLANES, LANES), axis=0)
        dlam_ref[0, :, 0:LANES] += dar
        dlam_ref[0, :, LANES:2 * LANES] += dai

        gb = gv.astype(BF16)
        xrep = jnp.dot(rep, xv.astype(BF16), preferred_element_type=F32)
        lx = jnp.where(mask, xrep, 0.0).astype(BF16)
        dr_ref[0] += lax.dot_general(lx, gb, (((0,), (0,)), ((), ())), preferred_element_type=F32)
        dc_ref[0] += lax.dot_general(hv.astype(BF16), ldy, (((0,), (0,)), ((), ())),
                                     preferred_element_type=F32)
        zx = lax.dot_general(gb, r_ref[0], (((1,), (1,)), ((), ())), preferred_element_type=F32)
        zx = jnp.where(mask, zx, 0.0)
        dx_ref[...] = jnp.sum(zx.reshape(T, SUBLANES, S5_PART), axis=1) + d_ref[...] * dy
        dd_ref[...] += jnp.sum(dy * xv, axis=0, keepdims=True)

    xs = pl.BlockSpec((T, S5_PART), lambda q, i: (nb - 1 - i, q))
    ms = pl.BlockSpec((1, S5_PART, S5_PART), lambda q, i: (q, 0, 0))
    ls = pl.BlockSpec((1, SUBLANES, S5_PART), lambda q, i: (q, 0, 0))
    ds_ = pl.BlockSpec((1, S5_PART), lambda q, i: (0, q))
    return pl.pallas_call(
        body, name="s5_scan_bwd",
        out_shape=(jax.ShapeDtypeStruct((S, D), F32), jax.ShapeDtypeStruct((NQ, S5_PART, S5_PART), F32),
                   jax.ShapeDtypeStruct((NQ, S5_PART, S5_PART), F32),
                   jax.ShapeDtypeStruct((NQ, SUBLANES, S5_PART), F32), jax.ShapeDtypeStruct((1, D), F32)),
        grid=(NQ, nb),
        in_specs=[xs, xs, xs, pl.BlockSpec((1, rows, S5_PART), lambda q, i: (q, nb - 1 - i, 0)),
                  pl.BlockSpec((1, SUBLANES, S5_PART), lambda q, i: (q, jnp.maximum((nb - 1 - i) * T - 1, 0), 0)),
                  ms, ms, ls, ds_],
        out_specs=(xs, ms, ms, ls, ds_),
        scratch_shapes=[pltpu.VMEM((rows, S5_PART), F32), pltpu.VMEM((rows, S5_PART), F32),
                        pltpu.VMEM((rows + SUBLANES, S5_PART), F32), pltpu.VMEM((SUBLANES, S5_PART), F32)],
        compiler_params=_params(("parallel", "arbitrary")),
    )(x, y, dyg, hs, hs, rmat, cmat, lam, dskip)


def _s5_discretise(a_re, a_im, log_dt, b_re, b_im):
    dt = jnp.exp(log_dt)[:, None]
    mag = jnp.exp(a_re * dt)
    ang = a_im * dt
    lb_re = mag * jnp.cos(ang)
    lb_im = mag * jnp.sin(ang)
    den = a_re * a_re + a_im * a_im
    nr = lb_re - 1.0
    ni = lb_im
    z_re = (nr * a_re + ni * a_im) / den
    z_im = (ni * a_re - nr * a_im) / den
    bb_re = z_re[..., None] * b_re - z_im[..., None] * b_im
    bb_im = z_re[..., None] * b_im + z_im[..., None] * b_re
    return lb_re, lb_im, bb_re, bb_im


def _s5_expand(w):
    G = w.shape[0]
    NQ = G // 16
    base = w.reshape(NQ, S5_PART, S5_STATE)
    half = (jnp.arange(S5_PART) // S5_GROUP) % 2
    sel = (half[:, None] == jnp.arange(2)[None, :]).astype(w.dtype)
    out = base[:, :, None, :] * sel[None, :, :, None]
    return out.reshape(NQ, S5_PART, 2 * S5_STATE)


def _s5_extract(m):
    NQ = m.shape[0]
    half = (jnp.arange(S5_PART) // S5_GROUP) % 2
    sel = (half[:, None] == jnp.arange(2)[None, :]).astype(m.dtype)
    base = jnp.sum(m.reshape(NQ, S5_PART, 2, S5_STATE) * sel[None, :, :, None], axis=2)
    return base.reshape(NQ * 16, S5_GROUP, S5_STATE)


def _s5_slab(v):
    return v.reshape(v.shape[0] // 16, SUBLANES, LANES)


def _adamw(w, g, m, v, name):
    R, C = w.shape
    tr = _tile(R, 256, SUBLANES)
    c1 = 1.0 / (1.0 - ADAM_B1 ** ADAM_STEP)
    c2 = 1.0 / (1.0 - ADAM_B2 ** ADAM_STEP)

    def body(w_ref, g_ref, m_ref, v_ref, d_ref, nm_ref, nv_ref):
        gv = g_ref[...]
        nm = ADAM_B1 * m_ref[...] + (1.0 - ADAM_B1) * gv
        nv = ADAM_B2 * v_ref[...] + (1.0 - ADAM_B2) * (gv * gv)
        nm_ref[...] = nm
        nv_ref[...] = nv
        d_ref[...] = -ADAM_LR * ((nm * c1) / (jnp.sqrt(nv * c2) + ADAM_EPS) + ADAM_WD * w_ref[...])

    blk = pl.BlockSpec((tr, C), lambda i: (i, 0))
    sh = jax.ShapeDtypeStruct((R, C), F32)
    return pl.pallas_call(
        body, name=name, out_shape=(sh, sh, sh), grid=(R // tr,), in_specs=[blk] * 4, out_specs=(blk,) * 3,
        compiler_params=_params(("parallel",)),
    )(w, g, m, v)


def _sum8(parts, name):
    _, R, C = parts.shape
    tr = _tile(R, 256, SUBLANES)

    def body(p_ref, o_ref):
        acc = p_ref[0].astype(F32)
        for k in range(1, N_DEV):
            acc = acc + p_ref[k].astype(F32)
        o_ref[...] = acc

    return pl.pallas_call(
        body, name=name, out_shape=jax.ShapeDtypeStruct((R, C), F32), grid=(R // tr,),
        in_specs=[pl.BlockSpec((N_DEV, tr, C), lambda i: (0, i, 0))],
        out_specs=pl.BlockSpec((tr, C), lambda i: (i, 0)), compiler_params=_params(("parallel",)),
    )(parts)


def _peers():
    x, y, c = lax.axis_index("x"), lax.axis_index("y"), lax.axis_index("c")
    me = 4 * x + 2 * y + c
    out = []
    for k in range(1, N_DEV):
        kx, ky, kc = (k >> 2) & 1, (k >> 1) & 1, k & 1
        px, py, pc = x ^ kx, y ^ ky, c ^ kc
        out.append(((px, py, pc), 4 * px + 2 * py + pc))
    return me, out


def _all_gather(x, name):
    R, C = x.shape

    def body(x_ref, o_ref, send_sems, recv_sems, local_sem):
        me, peers = _peers()
        mine = pltpu.make_async_copy(x_ref, o_ref.at[me], local_sem)
        mine.start()
        sends = []
        for k, (dev, _) in enumerate(peers):
            cp = pltpu.make_async_remote_copy(src_ref=x_ref, dst_ref=o_ref.at[me], send_sem=send_sems.at[k],
                                              recv_sem=recv_sems.at[k], device_id=dev, device_id_type=MESH)
            cp.start()
            sends.append(cp)
        for k, (dev, idx) in enumerate(peers):
            pltpu.make_async_remote_copy(src_ref=x_ref, dst_ref=o_ref.at[idx], send_sem=send_sems.at[k],
                                         recv_sem=recv_sems.at[k], device_id=dev, device_id_type=MESH).wait_recv()
        for cp in sends:
            cp.wait_send()
        mine.wait()

    return pl.pallas_call(
        body, name=name, out_shape=jax.ShapeDtypeStruct((N_DEV, R, C), x.dtype),
        in_specs=[pl.BlockSpec(memory_space=pltpu.HBM)], out_specs=pl.BlockSpec(memory_space=pltpu.HBM),
        scratch_shapes=[pltpu.SemaphoreType.DMA((N_DEV - 1,)), pltpu.SemaphoreType.DMA((N_DEV - 1,)),
                        pltpu.SemaphoreType.DMA],
    )(x)


def _all_to_all(x, name):
    _, R, C = x.shape

    def body(x_ref, o_ref, send_sems, recv_sems, local_sem):
        me, peers = _peers()
        mine = pltpu.make_async_copy(x_ref.at[me], o_ref.at[me], local_sem)
        mine.start()
        sends = []
        for k, (dev, idx) in enumerate(peers):
            cp = pltpu.make_async_remote_copy(src_ref=x_ref.at[idx], dst_ref=o_ref.at[me], send_sem=send_sems.at[k],
                                              recv_sem=recv_sems.at[k], device_id=dev, device_id_type=MESH)
            cp.start()
            sends.append(cp)
        for k, (dev, idx) in enumerate(peers):
            pltpu.make_async_remote_copy(src_ref=x_ref.at[idx], dst_ref=o_ref.at[idx], send_sem=send_sems.at[k],
                                         recv_sem=recv_sems.at[k], device_id=dev, device_id_type=MESH).wait_recv()
        for cp in sends:
            cp.wait_send()
        mine.wait()

    return pl.pallas_call(
        body, name=name, out_shape=jax.ShapeDtypeStruct((N_DEV, R, C), x.dtype),
        in_specs=[pl.BlockSpec(memory_space=pltpu.HBM)], out_specs=pl.BlockSpec(memory_space=pltpu.HBM),
        scratch_shapes=[pltpu.SemaphoreType.DMA((N_DEV - 1,)), pltpu.SemaphoreType.DMA((N_DEV - 1,)),
                        pltpu.SemaphoreType.DMA],
    )(x)


def _pack_rows(arrs, dtype, lead=()):
    flat = [a.astype(dtype).reshape(lead + (-1,)) for a in arrs]
    cat = jnp.concatenate(flat, axis=-1)
    n = cat.shape[-1]
    per = PACK_COLS * PACK_ROW_ALIGN
    tot = -(-n // per) * per
    cat = jnp.pad(cat, [(0, 0)] * len(lead) + [(0, tot - n)])
    return cat.reshape(lead + (tot // PACK_COLS, PACK_COLS))


def _unpack_rows(packed, shapes, lead=()):
    flat = packed.reshape(lead + (-1,))
    out, o = [], 0
    for s in shapes:
        n = math.prod(s)
        out.append(flat[..., o:o + n].reshape(lead + tuple(s)))
        o += n
    return out


def _ffn_fwd(x, w_in, w_out, g, b, alpha, tag):
    h = _mm(x, w_in, out_dtype=BF16, name=tag + "_h")
    a = _swiglu_fwd(h, tag + "_act")
    y = _mm(a, w_out, name=tag + "_y")
    out, xh, rstd = _res_ln_fwd(x, y, g, b, alpha, 0.5, tag + "_ln")
    return out, (x, h, a, xh, rstd)


def _ffn_bwd(dout, saved, w_in, w_out, g, alpha, tag):
    x, h, a, xh, rstd = saved
    dz, dg, db = _ln_bwd(dout, xh, rstd, g, tag + "_ln_bwd")
    dw_out = _mm(a, dz, ta=True, scale=0.5, name=tag + "_dwout")
    da = _mm(dz, w_out, tb=True, scale=0.5, name=tag + "_da")
    dh = _swiglu_bwd(h, da, tag + "_act_bwd")
    dw_in = _mm(x, dh, ta=True, name=tag + "_dwin")
    dx = _mm(dh, w_in, tb=True, add=dz, add_scale=alpha, name=tag + "_dx")
    return dx, dw_in, dw_out, dg, db


def _heads(t, H):
    S = t.shape[0]
    return t.reshape(S, H, HEAD_DIM).transpose(1, 0, 2)


def _fox_fwd(x, w_in_pad, b_f_pad, w_o, g, b, alpha, tag):
    S, D = x.shape
    H = D // HEAD_DIM
    proj = _mm(x, w_in_pad, name=tag + "_proj")
    q, k, v = (_heads(proj[:, i * D:(i + 1) * D].astype(BF16), H) for i in range(3))
    fl = proj[:, 3 * D:]
    cum = _fox_gate_fwd(fl, b_f_pad)
    ch = cum[:, :H].T
    cq, ck = ch[:, :, None], ch[:, None, :]
    o, lse = _flash_fwd(q, k, v, cq, ck)
    o2 = o.transpose(1, 0, 2).reshape(S, D)
    m = _mm(o2, w_o, name=tag + "_out")
    out, xh, rstd = _res_ln_fwd(x, m, g, b, alpha, 1.0, tag + "_ln")
    return out, (x, q, k, v, cq, ck, o, lse, o2, fl, xh, rstd)


def _fox_bwd(dout, saved, w_in_pad, b_f_pad, w_o, g, alpha, tag):
    x, q, k, v, cq, ck, o, lse, o2, fl, xh, rstd = saved
    S, D = x.shape
    H = D // HEAD_DIM
    dz, dg, db = _ln_bwd(dout, xh, rstd, g, tag + "_ln_bwd")
    dw_o = _mm(o2, dz, ta=True, name=tag + "_dwo")
    do = _heads(_mm(dz, w_o, tb=True, name=tag + "_do"), H)
    dq, delta, dcq = _flash_bwd_dq(q, k, v, cq, ck, o, lse, do)
    dk, dv, dck = _flash_bwd_dkv(q, k, v, cq, ck, lse, do, delta)
    dcum = jnp.pad((dcq[:, :, 0] + dck[:, 0, :]).T, ((0, 0), (0, LANES - H)))
    dfl, dbf = _fox_gate_bwd(dcum, fl, b_f_pad)
    flat = lambda t: t.transpose(1, 0, 2).reshape(S, D).astype(BF16)
    dproj = jnp.concatenate([flat(dq), flat(dk), flat(dv), dfl.astype(BF16)], axis=1)
    dw_in = _mm(x, dproj, ta=True, name=tag + "_dwin")
    dx = _mm(dproj, w_in_pad, tb=True, add=dz, add_scale=alpha, name=tag + "_dx")
    return dx, dw_in[:, :3 * D + H], dbf[0, :H], dw_o, dg, db


def _s5_mats(p):
    lb_re, lb_im, bb_re, bb_im = _s5_discretise(p["a_re"], p["a_im"], p["log_dt"], p["b_re"], p["b_im"])
    rmat = jnp.concatenate([_s5_expand(bb_re.transpose(0, 2, 1)), _s5_expand(bb_im.transpose(0, 2, 1))], axis=2)
    cmat = jnp.concatenate([_s5_expand(p["c_re"]).transpose(0, 2, 1), -_s5_expand(p["c_im"]).transpose(0, 2, 1)],
                           axis=1)
    lam = jnp.concatenate([_s5_slab(lb_re), _s5_slab(lb_im)], axis=2)
    return rmat.astype(BF16), cmat.astype(BF16), lam


def _s5_block_fwd(x, p, w_out, g, b, alpha, tag):
    S, D = x.shape
    rmat, cmat, lam = _s5_mats(p)
    dskip = p["d"].reshape(1, D)
    y, yg, hs = _s5_fwd(x, rmat, cmat, lam, dskip)
    vg = _mm(yg, w_out, name=tag + "_vg")
    m = _glu_fwd(vg, tag + "_glu")
    out, xh, rstd = _res_ln_fwd(x, m, g, b, alpha, 1.0, tag + "_ln")
    return out, (x, y, yg, hs, vg, rmat, cmat, lam, dskip, xh, rstd)


def _s5_block_bwd(dout, saved, p, w_out, g, alpha, tag):
    x, y, yg, hs, vg, rmat, cmat, lam, dskip, xh, rstd = saved
    S, D = x.shape
    G = D // S5_GROUP
    dz, dg, db = _ln_bwd(dout, xh, rstd, g, tag + "_ln_bwd")
    dvg = _glu_bwd(vg, dz, tag + "_glu_bwd")
    dw_out = _mm(yg, dvg, ta=True, name=tag + "_dwout")
    dyg = _mm(dvg, w_out, tb=True, name=tag + "_dyg")
    dxs, dr, dc, dlam, dd = _s5_bwd(x, y, dyg, hs, rmat, cmat, lam, dskip)
    dx = dxs + alpha * dz
    dbb_re = _s5_extract(dr[:, :, :LANES]).transpose(0, 2, 1)
    dbb_im = _s5_extract(dr[:, :, LANES:]).transpose(0, 2, 1)
    dc_re = _s5_extract(dc[:, :LANES, :].transpose(0, 2, 1))
    dc_im = -_s5_extract(dc[:, LANES:, :].transpose(0, 2, 1))
    dlb_re = dlam[:, :, :LANES].reshape(G, S5_STATE)
    dlb_im = dlam[:, :, LANES:].reshape(G, S5_STATE)
    _, vjp = jax.vjp(_s5_discretise, p["a_re"], p["a_im"], p["log_dt"], p["b_re"], p["b_im"])
    da_re, da_im, dlog_dt, db_re, db_im = vjp((dlb_re, dlb_im, dbb_re, dbb_im))
    grads = dict(a_re=da_re, a_im=da_im, log_dt=dlog_dt, b_re=db_re, b_im=db_im, c_re=dc_re, c_im=dc_im,
                 d=dd.reshape(G, S5_GROUP))
    return dx, grads, dw_out, dg, db


BIG = ("ffn1_w_in", "ffn1_w_out", "ffn2_w_in", "ffn2_w_out", "fox_w_in", "fox_w_o", "s5_w_out")
BIG_SPLIT_COLS = {"ffn1_w_in": True, "ffn1_w_out": False, "ffn2_w_in": True, "ffn2_w_out": False,
                  "fox_w_in": True, "fox_w_o": False, "s5_w_out": True}
SMALL = ("ln1_g", "ln1_b", "lnm_g", "lnm_b", "ln2_g", "ln2_b", "fox_b_f", "s5_a_re", "s5_a_im", "s5_log_dt",
         "s5_b_re", "s5_b_im", "s5_c_re", "s5_c_im", "s5_d")
WEIGHTS = ("ffn1_w_in", "ffn1_w_out", "ln1_g", "ln1_b", "lnm_g", "lnm_b", "ffn2_w_in", "ffn2_w_out", "ln2_g", "ln2_b",
           "fox_w_in", "fox_b_f", "fox_w_o", "s5_a_re", "s5_a_im", "s5_log_dt", "s5_b_re", "s5_b_im", "s5_c_re",
           "s5_c_im", "s5_d", "s5_w_out")


def _join(gathered, split_cols):
    n, L, r, c = gathered.shape
    if split_cols:
        return gathered.transpose(1, 2, 0, 3).reshape(L, r, n * c)
    return gathered.transpose(1, 0, 2, 3).reshape(L, n * r, c)


def _split(full, split_cols):
    L, R, C = full.shape
    if split_cols:
        return full.reshape(L, R, N_DEV, C // N_DEV).transpose(2, 0, 1, 3)
    return full.reshape(L, N_DEV, R // N_DEV, C).transpose(1, 0, 2, 3)


def _local_step(x, target, W, small):
    S, D = x.shape
    H = D // HEAD_DIM
    depth = small["ln1_g"].shape[0]
    alpha = (2.0 * depth) ** 0.25
    fox_in_pad = jnp.pad(W["fox_w_in"], ((0, 0), (0, 0), (0, LANES - H)))
    bf_pad = jnp.pad(small["fox_b_f"], ((0, 0), (0, LANES - H)))

    def s5_params(j):
        return {k: small["s5_" + k][j] for k in ("a_re", "a_im", "log_dt", "b_re", "b_im", "c_re", "c_im", "d")}

    saved = []
    h = x
    for i in range(depth):
        j = i // 2
        h, s1 = _ffn_fwd(h, W["ffn1_w_in"][i], W["ffn1_w_out"][i], small["ln1_g"][i], small["ln1_b"][i], alpha,
                         f"l{i}_ffn1")
        if i % 2 == 0:
            h, s2 = _fox_fwd(h, fox_in_pad[j], bf_pad[j:j + 1], W["fox_w_o"][j], small["lnm_g"][i],
                             small["lnm_b"][i], alpha, f"l{i}_fox")
        else:
            h, s2 = _s5_block_fwd(h, s5_params(j), W["s5_w_out"][j], small["lnm_g"][i], small["lnm_b"][i], alpha,
                                  f"l{i}_s5")
        h, s3 = _ffn_fwd(h, W["ffn2_w_in"][i], W["ffn2_w_out"][i], small["ln2_g"][i], small["ln2_b"][i], alpha,
                         f"l{i}_ffn2")
        saved.append((s1, s2, s3))

    dh, loss_part = _loss_fwd_bwd(h, target)

    gW = {k: [None] * W[k].shape[0] for k in BIG}
    gs = {k: [None] * small[k].shape[0] for k in SMALL}
    for i in reversed(range(depth)):
        j = i // 2
        s1, s2, s3 = saved[i]
        dh, gW["ffn2_w_in"][i], gW["ffn2_w_out"][i], gs["ln2_g"][i], gs["ln2_b"][i] = _ffn_bwd(
            dh, s3, W["ffn2_w_in"][i], W["ffn2_w_out"][i], small["ln2_g"][i], alpha, f"l{i}_ffn2")
        if i % 2 == 0:
            dh, gW["fox_w_in"][j], gs["fox_b_f"][j], gW["fox_w_o"][j], gs["lnm_g"][i], gs["lnm_b"][i] = _fox_bwd(
                dh, s2, fox_in_pad[j], bf_pad[j:j + 1], W["fox_w_o"][j], small["lnm_g"][i], alpha, f"l{i}_fox")
        else:
            dh, g5, gW["s5_w_out"][j], gs["lnm_g"][i], gs["lnm_b"][i] = _s5_block_bwd(
                dh, s2, s5_params(j), W["s5_w_out"][j], small["lnm_g"][i], alpha, f"l{i}_s5")
            for k, val in g5.items():
                gs["s5_" + k][j] = val
        dh, gW["ffn1_w_in"][i], gW["ffn1_w_out"][i], gs["ln1_g"][i], gs["ln1_b"][i] = _ffn_bwd(
            dh, s1, W["ffn1_w_in"][i], W["ffn1_w_out"][i], small["ln1_g"][i], alpha, f"l{i}_ffn1")
    gW = {k: jnp.stack(v) for k, v in gW.items()}
    gs = {k: jnp.stack(v) for k, v in gs.items()}
    return loss_part, dh, gW, gs


def kernel(x, ffn1_w_in, ffn1_w_out, ln1_g, ln1_b, lnm_g, lnm_b, ffn2_w_in, ffn2_w_out, ln2_g, ln2_b, fox_w_in, fox_b_f, fox_w_o, s5_a_re, s5_a_im, s5_log_dt, s5_b_re, s5_b_im, s5_c_re, s5_c_im, s5_d, s5_w_out, loss_target, m_ffn1_w_in, m_ffn1_w_out, m_ln1_g, m_ln1_b, m_lnm_g, m_lnm_b, m_ffn2_w_in, m_ffn2_w_out, m_ln2_g, m_ln2_b, m_fox_w_in, m_fox_b_f, m_fox_w_o, m_s5_a_re, m_s5_a_im, m_s5_log_dt, m_s5_b_re, m_s5_b_im, m_s5_c_re, m_s5_c_im, m_s5_d, m_s5_w_out, v_ffn1_w_in, v_ffn1_w_out, v_ln1_g, v_ln1_b, v_lnm_g, v_lnm_b, v_ffn2_w_in, v_ffn2_w_out, v_ln2_g, v_ln2_b, v_fox_w_in, v_fox_b_f, v_fox_w_o, v_s5_a_re, v_s5_a_im, v_s5_log_dt, v_s5_b_re, v_s5_b_im, v_s5_c_re, v_s5_c_im, v_s5_d, v_s5_w_out):
    args = dict(locals())
    w = {k: args[k] for k in WEIGHTS}
    m = {k: args["m_" + k] for k in WEIGHTS}
    v = {k: args["v_" + k] for k in WEIGHTS}
    small = {k: w[k] for k in SMALL}

    big_shapes = [w[k].shape for k in BIG]
    gathered = _all_gather(_pack_rows([w[k] for k in BIG], BF16), "gather_weights")
    parts = _unpack_rows(gathered, big_shapes, lead=(N_DEV,))
    W = {k: _join(p, BIG_SPLIT_COLS[k]) for k, p in zip(BIG, parts)}

    loss_part, dx, gW, gs = _local_step(x[0], loss_target[0], W, small)
    loss = lax.psum(loss_part, ("x", "y", "c"))

    send = _pack_rows([_split(gW[k], BIG_SPLIT_COLS[k]) for k in BIG], BF16, lead=(N_DEV,))
    recv = _all_to_all(send, "scatter_grads")
    g_big = dict(zip(BIG, _unpack_rows(_sum8(recv, "sum_grads"), big_shapes)))
    small_shapes = [w[k].shape for k in SMALL]
    g_small_all = _all_gather(_pack_rows([gs[k] for k in SMALL], F32), "gather_small_grads")
    g_small_flat = _sum8(g_small_all, "sum_small_grads")
    g_small = dict(zip(SMALL, _unpack_rows(g_small_flat, small_shapes)))

    delta, new_m, new_v = {}, {}, {}
    for k in BIG:
        sh = w[k].shape
        two = lambda a: a.reshape(-1, sh[-1])
        d_, m_, v_ = _adamw(two(w[k]), two(g_big[k]), two(m[k]), two(v[k]), "adamw_" + k)
        delta[k], new_m[k], new_v[k] = d_.reshape(sh), m_.reshape(sh), v_.reshape(sh)
    pk = lambda d: _pack_rows([d[k] for k in SMALL], F32)
    d_, m_, v_ = _adamw(pk(w), g_small_flat, pk(m), pk(v), "adamw_small")
    for dst, flat in ((delta, d_), (new_m, m_), (new_v, v_)):
        dst.update(zip(SMALL, _unpack_rows(flat, small_shapes)))
    grads = {**g_big, **g_small}

    return (loss, dx[None], *[grads[k] for k in WEIGHTS], *[delta[k] for k in WEIGHTS],
            *[new_m[k] for k in WEIGHTS], *[new_v[k] for k in WEIGHTS])
```

```python
import functools
import math

import jax
import jax.numpy as jnp
from jax import lax
from jax.experimental import pallas as pl
from jax.experimental.pallas import tpu as pltpu

F32 = jnp.float32
BF16 = jnp.bfloat16

N_DEV = 8
HEAD_DIM = 64
S5_GROUP = 16
S5_STATE = 64
LANES = 128
SUBLANES = 8
S5_PART = 256
LN_EPS = 1e-5
NEG_INF = -1e30
ADAM_LR, ADAM_B1, ADAM_B2, ADAM_EPS, ADAM_WD, ADAM_STEP = 0.001, 0.9, 0.999, 1e-08, 0.01, 10
VMEM_LIMIT = 48 * 1024 * 1024
PACK_COLS = 1024
PACK_ROW_ALIGN = 512

MESH = pl.DeviceIdType.MESH


def _tile(dim, pref, align=LANES):
    if dim <= pref:
        return dim
    t = (pref // align) * align
    while t >= align:
        if dim % t == 0:
            return t
        t -= align
    return dim


def _params(sem):
    return pltpu.CompilerParams(dimension_semantics=sem, vmem_limit_bytes=VMEM_LIMIT)


def _mm(a, b, *, ta=False, tb=False, out_dtype=F32, scale=None, add=None, add_scale=1.0,
        tm=512, tn=1408, tk=1408, name="mm"):
    if ta:
        K, M = a.shape
    else:
        M, K = a.shape
    if tb:
        N, K2 = b.shape
    else:
        K2, N = b.shape
    assert K == K2, (a.shape, b.shape, ta, tb)
    tm, tn, tk = _tile(M, tm), _tile(N, tn), _tile(K, tk)
    a_spec = pl.BlockSpec((tk, tm), lambda i, j, k: (k, i)) if ta else pl.BlockSpec((tm, tk), lambda i, j, k: (i, k))
    b_spec = pl.BlockSpec((tn, tk), lambda i, j, k: (j, k)) if tb else pl.BlockSpec((tk, tn), lambda i, j, k: (k, j))
    o_spec = pl.BlockSpec((tm, tn), lambda i, j, k: (i, j))
    return _mm_core(name, a, b, a_spec, b_spec, o_spec, jax.ShapeDtypeStruct((M, N), out_dtype),
                    (M // tm, N // tn, K // tk), (tm, tn), ta, tb, scale, add, add_scale)


def _mm_core(name, a, b, a_spec, b_spec, o_spec, out_shape, grid, acc_shape, ta, tb, scale=None, add=None,
             add_scale=1.0):
    nk = grid[2]

    def body(*refs):
        if add is None:
            a_ref, b_ref, o_ref, acc = refs
            add_ref = None
        else:
            a_ref, b_ref, add_ref, o_ref, acc = refs
        k = pl.program_id(2)

        @pl.when(k == 0)
        def _():
            acc[...] = jnp.zeros_like(acc)

        dims = (((0 if ta else 1,), (1 if tb else 0,)), ((), ()))
        acc[...] += lax.dot_general(a_ref[...].astype(BF16), b_ref[...].astype(BF16), dims,
                                    preferred_element_type=F32)

        @pl.when(k == nk - 1)
        def _():
            r = acc[...]
            if scale is not None:
                r = r * scale
            if add_ref is not None:
                r = r + add_scale * add_ref[...]
            o_ref[...] = r.astype(out_shape.dtype)

    in_specs = [a_spec, b_spec]
    args = [a, b]
    if add is not None:
        in_specs.append(o_spec)
        args.append(add)
    return pl.pallas_call(
        body, name=name, out_shape=out_shape, grid=grid, in_specs=in_specs, out_specs=o_spec,
        scratch_shapes=[pltpu.VMEM(acc_shape, F32)],
        compiler_params=_params(("parallel", "parallel", "arbitrary")),
    )(*args)


def _ffn_h(x, wg, l, name):
    S, D = x.shape
    n, _, _, c = wg.shape
    tm = _tile(S, 1024, SUBLANES)
    return _mm_core(name, x, wg, pl.BlockSpec((tm, D), lambda i, j, k: (i, 0)),
                    pl.BlockSpec((None, None, D, c), lambda i, j, k: (j, l, 0, 0)),
                    pl.BlockSpec((None, tm, c), lambda i, j, k: (j, i, 0)),
                    jax.ShapeDtypeStruct((n, S, c), BF16), (S // tm, n, 1), (tm, c), False, False)


def _ffn_y(a4, wo4, l, name):
    nb, S, c = a4.shape
    D = wo4.shape[-1]
    tm, tn = _tile(S, 1024, SUBLANES), _tile(D, 1024)
    return _mm_core(name, a4, wo4, pl.BlockSpec((None, tm, c), lambda i, j, k: (k, i, 0)),
                    pl.BlockSpec((None, None, c, tn), lambda i, j, k: (l, k, 0, j)),
                    pl.BlockSpec((tm, tn), lambda i, j, k: (i, j)),
                    jax.ShapeDtypeStruct((S, D), F32), (S // tm, D // tn, nb), (tm, tn), False, False)


def _ffn_dwout(a4, dz, scale, name):
    nb, S, c = a4.shape
    D = dz.shape[1]
    tk, tn = _tile(S, 1024, SUBLANES), _tile(D, 1024)
    return _mm_core(name, a4, dz, pl.BlockSpec((None, tk, c), lambda i, j, k: (i, k, 0)),
                    pl.BlockSpec((tk, tn), lambda i, j, k: (k, j)),
                    pl.BlockSpec((None, c, tn), lambda i, j, k: (i, 0, j)),
                    jax.ShapeDtypeStruct((nb, c, D), BF16), (nb, D // tn, S // tk), (c, tn), True, False, scale)


def _ffn_da(dz, wo4, l, scale, name):
    S, D = dz.shape
    nb, c = wo4.shape[1], wo4.shape[2]
    tm = _tile(S, 1024, SUBLANES)
    return _mm_core(name, dz, wo4, pl.BlockSpec((tm, D), lambda i, j, k: (i, 0)),
                    pl.BlockSpec((None, None, c, D), lambda i, j, k: (l, j, 0, 0)),
                    pl.BlockSpec((None, tm, c), lambda i, j, k: (j, i, 0)),
                    jax.ShapeDtypeStruct((nb, S, c), F32), (S // tm, nb, 1), (tm, c), False, True, scale)


def _ffn_dwin(x, dh8, name):
    S, D = x.shape
    n, _, c = dh8.shape
    tk, tm = _tile(S, 1024, SUBLANES), _tile(D, 1024)
    return _mm_core(name, x, dh8, pl.BlockSpec((tk, tm), lambda i, j, k: (k, j)),
                    pl.BlockSpec((None, tk, c), lambda i, j, k: (i, k, 0)),
                    pl.BlockSpec((None, tm, c), lambda i, j, k: (i, j, 0)),
                    jax.ShapeDtypeStruct((n, D, c), BF16), (n, D // tm, S // tk), (tm, c), True, False)


def _ffn_dx(dh8, wg, l, dz, alpha, name):
    n, S, c = dh8.shape
    D = wg.shape[2]
    tm, tn = _tile(S, 1024, SUBLANES), _tile(D, 1024)
    return _mm_core(name, dh8, wg, pl.BlockSpec((None, tm, c), lambda i, j, k: (k, i, 0)),
                    pl.BlockSpec((None, None, tn, c), lambda i, j, k: (k, l, j, 0)),
                    pl.BlockSpec((tm, tn), lambda i, j, k: (i, j)),
                    jax.ShapeDtypeStruct((S, D), F32), (S // tm, D // tn, n), (tm, tn), False, True,
                    None, dz, alpha)


def _res_ln_fwd(x, y, g, b, alpha, s, name):
    S, D = x.shape
    tm = _tile(S, 256, SUBLANES)

    def body(x_ref, y_ref, g_ref, b_ref, o_ref, xh_ref, r_ref):
        z = alpha * x_ref[...] + s * y_ref[...]
        mu = jnp.mean(z, axis=-1, keepdims=True)
        zc = z - mu
        var = jnp.mean(zc * zc, axis=-1, keepdims=True)
        rstd = lax.rsqrt(var + LN_EPS)
        xh = zc * rstd
        xh_ref[...] = xh
        r_ref[...] = rstd
        o_ref[...] = xh * g_ref[...] + b_ref[...]

    row = pl.BlockSpec((tm, D), lambda i: (i, 0))
    vec = pl.BlockSpec((1, D), lambda i: (0, 0))
    return pl.pallas_call(
        body, name=name,
        out_shape=(jax.ShapeDtypeStruct((S, D), F32), jax.ShapeDtypeStruct((S, D), F32),
                   jax.ShapeDtypeStruct((S, 1), F32)),
        grid=(S // tm,), in_specs=[row, row, vec, vec],
        out_specs=(row, row, pl.BlockSpec((tm, 1), lambda i: (i, 0))),
        compiler_params=_params(("parallel",)),
    )(x, y, g.reshape(1, D), b.reshape(1, D))


def _ln_bwd(dout, xh, rstd, g, name):
    S, D = dout.shape
    tm = _tile(S, 256, SUBLANES)

    def body(d_ref, xh_ref, r_ref, g_ref, dz_ref, dg_ref, db_ref):
        i = pl.program_id(0)

        @pl.when(i == 0)
        def _():
            dg_ref[...] = jnp.zeros_like(dg_ref)
            db_ref[...] = jnp.zeros_like(db_ref)

        d = d_ref[...]
        xhv = xh_ref[...]
        dxh = d * g_ref[...]
        m1 = jnp.mean(dxh, axis=-1, keepdims=True)
        m2 = jnp.mean(dxh * xhv, axis=-1, keepdims=True)
        dz_ref[...] = r_ref[...] * (dxh - m1 - xhv * m2)
        dg_ref[...] += jnp.sum(d * xhv, axis=0, keepdims=True)
        db_ref[...] += jnp.sum(d, axis=0, keepdims=True)

    row = pl.BlockSpec((tm, D), lambda i: (i, 0))
    vec = pl.BlockSpec((1, D), lambda i: (0, 0))
    dz, dg, db = pl.pallas_call(
        body, name=name,
        out_shape=(jax.ShapeDtypeStruct((S, D), F32), jax.ShapeDtypeStruct((1, D), F32),
                   jax.ShapeDtypeStruct((1, D), F32)),
        grid=(S // tm,), in_specs=[row, row, pl.BlockSpec((tm, 1), lambda i: (i, 0)), vec],
        out_specs=(row, vec, vec),
        compiler_params=_params(("arbitrary",)),
    )(dout, xh, rstd, g.reshape(1, D))
    return dz, dg[0], db[0]


def _sigmoid(x):
    e = jnp.exp(-jnp.abs(x))
    return jnp.where(x >= 0, 1.0 / (1.0 + e), e / (1.0 + e))


def _swiglu_fwd(h8, name):
    n, S, c = h8.shape
    nb = n // 2
    tm = _tile(S, 512, SUBLANES)

    def body(h_ref, o_ref):
        g = h_ref[0].astype(F32)
        u = h_ref[1].astype(F32)
        o_ref[...] = (g * _sigmoid(g) * u).astype(BF16)

    return pl.pallas_call(
        body, name=name, out_shape=jax.ShapeDtypeStruct((nb, S, c), BF16), grid=(nb, S // tm),
        in_specs=[pl.BlockSpec((2, None, tm, c), lambda k, i: (0, k, i, 0))],
        out_specs=pl.BlockSpec((None, tm, c), lambda k, i: (k, i, 0)),
        compiler_params=_params(("parallel", "parallel")),
    )(h8.reshape(2, nb, S, c))


def _swiglu_bwd(h8, da4, name):
    n, S, c = h8.shape
    nb = n // 2
    tm = _tile(S, 512, SUBLANES)

    def body(h_ref, da_ref, d_ref):
        g = h_ref[0].astype(F32)
        u = h_ref[1].astype(F32)
        da_v = da_ref[...]
        sg = _sigmoid(g)
        silu = g * sg
        d_ref[0] = (da_v * u * (sg + silu * (1.0 - sg))).astype(BF16)
        d_ref[1] = (da_v * silu).astype(BF16)

    pair = pl.BlockSpec((2, None, tm, c), lambda k, i: (0, k, i, 0))
    dh = pl.pallas_call(
        body, name=name, out_shape=jax.ShapeDtypeStruct((2, nb, S, c), BF16), grid=(nb, S // tm),
        in_specs=[pair, pl.BlockSpec((None, tm, c), lambda k, i: (k, i, 0))], out_specs=pair,
        compiler_params=_params(("parallel", "parallel")),
    )(h8.reshape(2, nb, S, c), da4)
    return dh.reshape(n, S, c)


def _glu_fwd(vg, name):
    S, D2 = vg.shape
    D = D2 // 2
    tm = _tile(S, 512, SUBLANES)

    def body(v_ref, g_ref, o_ref):
        o_ref[...] = v_ref[...] * _sigmoid(g_ref[...])

    return pl.pallas_call(
        body, name=name, out_shape=jax.ShapeDtypeStruct((S, D), F32), grid=(S // tm,),
        in_specs=[pl.BlockSpec((tm, D), lambda i: (i, 0)), pl.BlockSpec((tm, D), lambda i: (i, 1))],
        out_specs=pl.BlockSpec((tm, D), lambda i: (i, 0)),
        compiler_params=_params(("parallel",)),
    )(vg, vg)


def _glu_bwd(vg, dm, name):
    S, D2 = vg.shape
    D = D2 // 2
    tm = _tile(S, 512, SUBLANES)

    def body(v_ref, g_ref, dm_ref, dv_ref, dg_ref):
        sg = _sigmoid(g_ref[...])
        d = dm_ref[...]
        dv_ref[...] = (d * sg).astype(BF16)
        dg_ref[...] = (d * v_ref[...] * sg * (1.0 - sg)).astype(BF16)

    blk = pl.BlockSpec((tm, D), lambda i: (i, 0))
    dv, dg = pl.pallas_call(
        body, name=name,
        out_shape=(jax.ShapeDtypeStruct((S, D), BF16), jax.ShapeDtypeStruct((S, D), BF16)),
        grid=(S // tm,), in_specs=[blk, pl.BlockSpec((tm, D), lambda i: (i, 1)), blk],
        out_specs=(blk, blk), compiler_params=_params(("parallel",)),
    )(vg, vg, dm)
    return jnp.concatenate([dv, dg], axis=1)


def _loss_fwd_bwd(y, target):
    S, D = y.shape
    tm = _tile(S, 256, SUBLANES)

    def body(y_ref, t_ref, dy_ref, l_ref):
        i = pl.program_id(0)

        @pl.when(i == 0)
        def _():
            l_ref[...] = jnp.zeros_like(l_ref)

        e = y_ref[...] - t_ref[...]
        dy_ref[...] = e * (1.0 / D)
        l_ref[...] += jnp.sum(e * e, axis=0, keepdims=True) * (0.5 / D)

    row = pl.BlockSpec((tm, D), lambda i: (i, 0))
    dy, part = pl.pallas_call(
        body, name="loss", out_shape=(jax.ShapeDtypeStruct((S, D), F32), jax.ShapeDtypeStruct((1, D), F32)),
        grid=(S // tm,), in_specs=[row, row], out_specs=(row, pl.BlockSpec((1, D), lambda i: (0, 0))),
        compiler_params=_params(("arbitrary",)),
    )(y, target)
    return dy, jnp.sum(part)


def _tri(n, lower):
    r = lax.broadcasted_iota(jnp.int32, (n, n), 0)
    c = lax.broadcasted_iota(jnp.int32, (n, n), 1)
    return jnp.where((c <= r) if lower else (c >= r), 1.0, 0.0)


def _fox_gate_fwd(fl, bf):
    S, W = fl.shape
    tm = _tile(S, 256, SUBLANES)

    def body(fl_ref, b_ref, c_ref, carry):
        i = pl.program_id(0)

        @pl.when(i == 0)
        def _():
            carry[...] = jnp.zeros_like(carry)

        x = fl_ref[...] + b_ref[...]
        lf = jnp.minimum(x, 0.0) - jnp.log(1.0 + jnp.exp(-jnp.abs(x)))
        c_ref[...] = jnp.dot(_tri(tm, True), lf, precision=lax.Precision.HIGHEST,
                             preferred_element_type=F32) + carry[...]
        carry[...] += jnp.sum(lf, axis=0, keepdims=True)

    blk = pl.BlockSpec((tm, W), lambda i: (i, 0))
    return pl.pallas_call(
        body, name="fox_gate_fwd", out_shape=jax.ShapeDtypeStruct((S, W), F32), grid=(S // tm,),
        in_specs=[blk, pl.BlockSpec((1, W), lambda i: (0, 0))], out_specs=blk,
        scratch_shapes=[pltpu.VMEM((1, W), F32)], compiler_params=_params(("arbitrary",)),
    )(fl, bf)


def _fox_gate_bwd(dcum, fl, bf):
    S, W = fl.shape
    tm = _tile(S, 256, SUBLANES)
    nb = S // tm

    def body(dc_ref, fl_ref, b_ref, dfl_ref, db_ref, carry):
        i = pl.program_id(0)

        @pl.when(i == 0)
        def _():
            carry[...] = jnp.zeros_like(carry)
            db_ref[...] = jnp.zeros_like(db_ref)

        dc = dc_ref[...]
        r = jnp.dot(_tri(tm, False), dc, precision=lax.Precision.HIGHEST, preferred_element_type=F32) + carry[...]
        carry[...] += jnp.sum(dc, axis=0, keepdims=True)
        x = fl_ref[...] + b_ref[...]
        dfl = r * (1.0 - _sigmoid(x))
        dfl_ref[...] = dfl
        db_ref[...] += jnp.sum(dfl, axis=0, keepdims=True)

    blk = pl.BlockSpec((tm, W), lambda i: (nb - 1 - i, 0))
    vec = pl.BlockSpec((1, W), lambda i: (0, 0))
    return pl.pallas_call(
        body, name="fox_gate_bwd",
        out_shape=(jax.ShapeDtypeStruct((S, W), F32), jax.ShapeDtypeStruct((1, W), F32)),
        grid=(nb,), in_specs=[blk, blk, vec], out_specs=(blk, vec),
        scratch_shapes=[pltpu.VMEM((1, W), F32)], compiler_params=_params(("arbitrary",)),
    )(dcum, fl, bf)


def _causal(t):
    row = lax.broadcasted_iota(jnp.int32, (t, t), 0)
    col = lax.broadcasted_iota(jnp.int32, (t, t), 1)
    return col <= row


def _flash_fwd(q, k, v, cq, ck):
    H, S, hd = q.shape
    t = _tile(S, 512)
    nq = S // t

    def body(q_ref, k_ref, v_ref, cq_ref, ck_ref, o_ref, lse_ref):
        qi = pl.program_id(1)
        qv = q_ref[0]
        cqv = cq_ref[0]

        def block(ki, carry, masked):
            m_old, l_old, acc = carry
            o = pl.multiple_of(ki * t, t)
            s = lax.dot_general(qv, k_ref[0, pl.ds(o, t), :], (((1,), (1,)), ((), ())),
                                preferred_element_type=F32)
            s = s + (cqv - ck_ref[0, ki])
            if masked:
                s = jnp.where(_causal(t), s, NEG_INF)
            m_new = jnp.maximum(m_old, jnp.max(s, axis=1, keepdims=True))
            p = jnp.exp(s - m_new)
            corr = jnp.exp(m_old - m_new)
            l_new = corr * l_old + jnp.sum(p, axis=1, keepdims=True)
            acc = corr * acc + jnp.dot(p.astype(BF16), v_ref[0, pl.ds(o, t), :], preferred_element_type=F32)
            return m_new, l_new, acc

        init = (jnp.full((t, 1), NEG_INF, F32), jnp.zeros((t, 1), F32), jnp.zeros((t, hd), F32))
        carry = lax.fori_loop(0, qi, lambda ki, c: block(ki, c, False), init)
        m, l, acc = block(qi, carry, True)
        o_ref[0] = acc / l
        lse_ref[0] = m + jnp.log(l)

    qs = pl.BlockSpec((1, t, hd), lambda h, i: (h, i, 0))
    full = pl.BlockSpec((1, S, hd), lambda h, i: (h, 0, 0))
    q1 = pl.BlockSpec((1, t, 1), lambda h, i: (h, i, 0))
    return pl.pallas_call(
        body, name="fox_attn_fwd",
        out_shape=(jax.ShapeDtypeStruct((H, S, hd), F32), jax.ShapeDtypeStruct((H, S, 1), F32)),
        grid=(H, nq), in_specs=[qs, full, full, q1, pl.BlockSpec((1, nq, 1, t), lambda h, i: (h, 0, 0, 0))],
        out_specs=(qs, q1), compiler_params=_params(("parallel", "parallel")),
    )(q, k, v, cq, ck)


def _flash_bwd(q, k, v, rows, ck, o, do):
    H, S, hd = q.shape
    t = _tile(S, 512)
    nb = S // t
    scale = 1.0 / math.sqrt(hd)

    def body(q_ref, k_ref, v_ref, rows_ref, ck_ref, o_ref, do_ref, dq_ref, dk_ref, dv_ref, dcq_ref, dck_ref,
             dk_acc, dv_acc, dc_acc):
        kb = pl.program_id(1)

        @pl.when(kb == 0)
        def _():
            dq_ref[...] = jnp.zeros_like(dq_ref)
            dcq_ref[...] = jnp.zeros_like(dcq_ref)

        dk_acc[...] = jnp.zeros_like(dk_acc)
        dv_acc[...] = jnp.zeros_like(dv_acc)
        dc_acc[...] = jnp.zeros_like(dc_acc)
        kv = k_ref[0]
        vv = v_ref[0]
        ckv = ck_ref[0, 0]

        def block(qb, masked):
            sl = pl.ds(pl.multiple_of(qb * t, t), t)
            qv = q_ref[0, sl, :]
            dov = do_ref[0, sl, :]
            rv = rows_ref[0, sl, :]
            s = lax.dot_general(qv, kv, (((1,), (1,)), ((), ())), preferred_element_type=F32)
            s = s + (rv[:, 0:1] - ckv)
            if masked:
                s = jnp.where(_causal(t), s, NEG_INF)
            p = jnp.exp(s - rv[:, 1:2])
            dob = dov.astype(BF16)
            dv_acc[...] += lax.dot_general(p.astype(BF16), dob, (((0,), (0,)), ((), ())),
                                           preferred_element_type=F32)
            dp = lax.dot_general(dob, vv, (((1,), (1,)), ((), ())), preferred_element_type=F32)
            delta = jnp.sum(dov * o_ref[0, sl, :], axis=1, keepdims=True)
            ds = p * (dp - delta)
            dsb = ds.astype(BF16)
            dk_acc[...] += lax.dot_general(dsb, qv, (((0,), (0,)), ((), ())), preferred_element_type=F32)
            dq_ref[0, sl, :] += jnp.dot(dsb, kv, preferred_element_type=F32) * scale
            dcq_ref[0, sl, :] += jnp.sum(ds, axis=1, keepdims=True)
            dc_acc[...] -= jnp.sum(ds, axis=0, keepdims=True)

        block(kb, True)

        def rest(qb, c):
            block(qb, False)
            return c

        lax.fori_loop(kb + 1, nb, rest, 0)
        dk_ref[0] = dk_acc[...]
        dv_ref[0] = dv_acc[...]
        dck_ref[0, 0] = dc_acc[...]

    full = pl.BlockSpec((1, S, hd), lambda h, j: (h, 0, 0))
    blk = pl.BlockSpec((1, t, hd), lambda h, j: (h, j, 0))
    full1 = pl.BlockSpec((1, S, 1), lambda h, j: (h, 0, 0))
    ck_s = pl.BlockSpec((1, 1, 1, t), lambda h, j: (h, j, 0, 0))
    f32 = lambda *s: jax.ShapeDtypeStruct(s, F32)
    return pl.pallas_call(
        body, name="fox_attn_bwd",
        out_shape=(f32(H, S, hd), f32(H, S, hd), f32(H, S, hd), f32(H, S, 1), f32(H, nb, 1, t)),
        grid=(H, nb),
        in_specs=[full, blk, blk, pl.BlockSpec((1, S, 2), lambda h, j: (h, 0, 0)), ck_s, full, full],
        out_specs=(full, blk, blk, full1, ck_s),
        scratch_shapes=[pltpu.VMEM((t, hd), F32), pltpu.VMEM((t, hd), F32), pltpu.VMEM((1, t), F32)],
        compiler_params=_params(("parallel", "arbitrary")),
    )(q, k, v, rows, ck, o, do)


def _s5_consts(T):
    rows = T * SUBLANES
    rr = lax.broadcasted_iota(jnp.int32, (rows, T), 0)
    tt = lax.broadcasted_iota(jnp.int32, (rows, T), 1)
    rep = jnp.where(rr // SUBLANES == tt, 1.0, 0.0).astype(BF16)
    r2 = lax.broadcasted_iota(jnp.int32, (rows, S5_PART), 0)
    c2 = lax.broadcasted_iota(jnp.int32, (rows, S5_PART), 1)
    mask = (c2 // (S5_PART // SUBLANES)) == (r2 % SUBLANES)
    return rep, mask


def _gelu(y):
    c = math.sqrt(2.0 / math.pi)
    return 0.5 * y * (1.0 + jnp.tanh(c * (y + 0.044715 * y * y * y)))


def _gelu_grad(y):
    c = math.sqrt(2.0 / math.pi)
    th = jnp.tanh(c * (y + 0.044715 * y * y * y))
    return 0.5 * (1.0 + th) + 0.5 * y * (1.0 - th * th) * c * (1.0 + 3.0 * 0.044715 * y * y)


def _s5_fwd(x, rmat, cmat, lam, dskip):
    S, D = x.shape
    NQ = D // S5_PART
    T = _tile(S, 128, SUBLANES)
    rows = T * SUBLANES

    def body(x_ref, r_ref, c_ref, lam_ref, d_ref, y_ref, yg_ref, h_ref, bu_s, carry):
        i = pl.program_id(1)

        @pl.when(i == 0)
        def _():
            carry[...] = jnp.zeros_like(carry)

        rep, mask = _s5_consts(T)
        xv = x_ref[...]
        xrep = jnp.dot(rep, xv.astype(BF16), preferred_element_type=F32)
        lx = jnp.where(mask, xrep, 0.0).astype(BF16)
        bu_s[...] = jnp.dot(lx, r_ref[0], preferred_element_type=F32)
        ar = lam_ref[0, :, 0:LANES]
        ai = lam_ref[0, :, LANES:2 * LANES]

        def step(t, c):
            hr, hi = c
            o = pl.multiple_of(t * SUBLANES, SUBLANES)
            sl = bu_s[pl.ds(o, SUBLANES), :]
            nhr = ar * hr - ai * hi + sl[:, 0:LANES]
            nhi = ar * hi + ai * hr + sl[:, LANES:2 * LANES]
            h_ref[0, pl.ds(o, SUBLANES), 0:LANES] = nhr
            h_ref[0, pl.ds(o, SUBLANES), LANES:2 * LANES] = nhi
            return nhr, nhi

        hr, hi = lax.fori_loop(0, T, step, (carry[:, 0:LANES], carry[:, LANES:2 * LANES]))
        carry[:, 0:LANES] = hr
        carry[:, LANES:2 * LANES] = hi
        z = jnp.dot(h_ref[0].astype(BF16), c_ref[0], preferred_element_type=F32)
        z = jnp.where(mask, z, 0.0)
        y = jnp.sum(z.reshape(T, SUBLANES, S5_PART), axis=1) + d_ref[...] * xv
        y_ref[...] = y
        yg_ref[...] = _gelu(y).astype(BF16)

    xs = pl.BlockSpec((T, S5_PART), lambda q, i: (i, q))
    ms = pl.BlockSpec((1, S5_PART, S5_PART), lambda q, i: (q, 0, 0))
    return pl.pallas_call(
        body, name="s5_scan_fwd",
        out_shape=(jax.ShapeDtypeStruct((S, D), F32), jax.ShapeDtypeStruct((S, D), BF16),
                   jax.ShapeDtypeStruct((NQ, S * SUBLANES, S5_PART), F32)),
        grid=(NQ, S // T),
        in_specs=[xs, ms, ms, pl.BlockSpec((1, SUBLANES, S5_PART), lambda q, i: (q, 0, 0)),
                  pl.BlockSpec((1, S5_PART), lambda q, i: (0, q))],
        out_specs=(xs, xs, pl.BlockSpec((1, rows, S5_PART), lambda q, i: (q, i, 0))),
        scratch_shapes=[pltpu.VMEM((rows, S5_PART), F32), pltpu.VMEM((SUBLANES, S5_PART), F32)],
        compiler_params=_params(("parallel", "arbitrary")),
    )(x, rmat, cmat, lam, dskip)


def _s5_bwd(x, y, dyg, hs, rmat, cmat, lam, dskip, res, res_scale):
    S, D = x.shape
    NQ = D // S5_PART
    T = _tile(S, 128, SUBLANES)
    nb = S // T
    rows = T * SUBLANES

    def body(x_ref, y_ref, dyg_ref, res_ref, h_ref, hp_ref, r_ref, c_ref, lam_ref, d_ref,
             dx_ref, dr_ref, dc_ref, dlam_ref, dd_ref, dh_s, g_s, hs_s, carry):
        i = pl.program_id(1)

        @pl.when(i == 0)
        def _():
            carry[...] = jnp.zeros_like(carry)
            dr_ref[...] = jnp.zeros_like(dr_ref)
            dc_ref[...] = jnp.zeros_like(dc_ref)
            dlam_ref[...] = jnp.zeros_like(dlam_ref)
            dd_ref[...] = jnp.zeros_like(dd_ref)

        rep, mask = _s5_consts(T)
        xv = x_ref[...]
        dy = dyg_ref[...] * _gelu_grad(y_ref[...])
        dyrep = jnp.dot(rep, dy.astype(BF16), preferred_element_type=F32)
        ldy = jnp.where(mask, dyrep, 0.0).astype(BF16)
        dh_s[...] = lax.dot_general(ldy, c_ref[0], (((1,), (1,)), ((), ())), preferred_element_type=F32)
        ar = lam_ref[0, :, 0:LANES]
        ai = lam_ref[0, :, LANES:2 * LANES]

        def step(n, c):
            gr, gi = c
            o = pl.multiple_of((T - 1 - n) * SUBLANES, SUBLANES)
            sl = dh_s[pl.ds(o, SUBLANES), :]
            ngr = sl[:, 0:LANES] + ar * gr + ai * gi
            ngi = sl[:, LANES:2 * LANES] - ai * gr + ar * gi
            g_s[pl.ds(o, SUBLANES), 0:LANES] = ngr
            g_s[pl.ds(o, SUBLANES), LANES:2 * LANES] = ngi
            return ngr, ngi

        gr, gi = lax.fori_loop(0, T, step, (carry[:, 0:LANES], carry[:, LANES:2 * LANES]))
        carry[:, 0:LANES] = gr
        carry[:, LANES:2 * LANES] = gi

        hv = h_ref[0]
        hs_s[0:SUBLANES, :] = jnp.where(i == nb - 1, 0.0, hp_ref[0])
        hs_s[SUBLANES:rows + SUBLANES, :] = hv
        hprev = hs_s[0:rows, :]
        gv = g_s[...]
        g_re, g_im = gv[:, 0:LANES], gv[:, LANES:2 * LANES]
        hp_re, hp_im = hprev[:, 0:LANES], hprev[:, LANES:2 * LANES]
        dar = jnp.sum((g_re * hp_re + g_im * hp_im).reshape(T, SUBLANES, LANES), axis=0)
        dai = jnp.sum((g_im * hp_re - g_re * hp_im).reshape(T, SUBLANES, LANES), axis=0)
        dlam_ref[0, :, 0:LANES] += dar
        dlam_ref[0, :, LANES:2 * LANES] += dai

        gb = gv.astype(BF16)
        xrep = jnp.dot(rep, xv.astype(BF16), preferred_element_type=F32)
        lx = jnp.where(mask, xrep, 0.0).astype(BF16)
        dr_ref[0] += lax.dot_general(lx, gb, (((0,), (0,)), ((), ())), preferred_element_type=F32)
        dc_ref[0] += lax.dot_general(hv.astype(BF16), ldy, (((0,), (0,)), ((), ())),
                                     preferred_element_type=F32)
        zx = lax.dot_general(gb, r_ref[0], (((1,), (1,)), ((), ())), preferred_element_type=F32)
        zx = jnp.where(mask, zx, 0.0)
        dx_ref[...] = (jnp.sum(zx.reshape(T, SUBLANES, S5_PART), axis=1) + d_ref[...] * dy
                       + res_scale * res_ref[...])
        dd_ref[...] += jnp.sum(dy * xv, axis=0, keepdims=True)

    xs = pl.BlockSpec((T, S5_PART), lambda q, i: (nb - 1 - i, q))
    ms = pl.BlockSpec((1, S5_PART, S5_PART), lambda q, i: (q, 0, 0))
    ls = pl.BlockSpec((1, SUBLANES, S5_PART), lambda q, i: (q, 0, 0))
    ds_ = pl.BlockSpec((1, S5_PART), lambda q, i: (0, q))
    return pl.pallas_call(
        body, name="s5_scan_bwd",
        out_shape=(jax.ShapeDtypeStruct((S, D), F32), jax.ShapeDtypeStruct((NQ, S5_PART, S5_PART), F32),
                   jax.ShapeDtypeStruct((NQ, S5_PART, S5_PART), F32),
                   jax.ShapeDtypeStruct((NQ, SUBLANES, S5_PART), F32), jax.ShapeDtypeStruct((1, D), F32)),
        grid=(NQ, nb),
        in_specs=[xs, xs, xs, xs, pl.BlockSpec((1, rows, S5_PART), lambda q, i: (q, nb - 1 - i, 0)),
                  pl.BlockSpec((1, SUBLANES, S5_PART), lambda q, i: (q, jnp.maximum((nb - 1 - i) * T - 1, 0), 0)),
                  ms, ms, ls, ds_],
        out_specs=(xs, ms, ms, ls, ds_),
        scratch_shapes=[pltpu.VMEM((rows, S5_PART), F32), pltpu.VMEM((rows, S5_PART), F32),
                        pltpu.VMEM((rows + SUBLANES, S5_PART), F32), pltpu.VMEM((SUBLANES, S5_PART), F32)],
        compiler_params=_params(("parallel", "arbitrary")),
    )(x, y, dyg, res, hs, hs, rmat, cmat, lam, dskip)


def _s5_discretise(a_re, a_im, log_dt, b_re, b_im):
    dt = jnp.exp(log_dt)[:, None]
    mag = jnp.exp(a_re * dt)
    ang = a_im * dt
    lb_re = mag * jnp.cos(ang)
    lb_im = mag * jnp.sin(ang)
    den = a_re * a_re + a_im * a_im
    nr = lb_re - 1.0
    ni = lb_im
    z_re = (nr * a_re + ni * a_im) / den
    z_im = (ni * a_re - nr * a_im) / den
    bb_re = z_re[..., None] * b_re - z_im[..., None] * b_im
    bb_im = z_re[..., None] * b_im + z_im[..., None] * b_re
    return lb_re, lb_im, bb_re, bb_im


def _s5_expand(w):
    G = w.shape[0]
    NQ = G // 16
    base = w.reshape(NQ, S5_PART, S5_STATE)
    half = (jnp.arange(S5_PART) // S5_GROUP) % 2
    sel = (half[:, None] == jnp.arange(2)[None, :]).astype(w.dtype)
    out = base[:, :, None, :] * sel[None, :, :, None]
    return out.reshape(NQ, S5_PART, 2 * S5_STATE)


def _s5_extract(m):
    NQ = m.shape[0]
    half = (jnp.arange(S5_PART) // S5_GROUP) % 2
    sel = (half[:, None] == jnp.arange(2)[None, :]).astype(m.dtype)
    base = jnp.sum(m.reshape(NQ, S5_PART, 2, S5_STATE) * sel[None, :, :, None], axis=2)
    return base.reshape(NQ * 16, S5_GROUP, S5_STATE)


def _s5_slab(v):
    return v.reshape(v.shape[0] // 16, SUBLANES, LANES)


def _adamw(w, g, m, v, name):
    R, C = w.shape
    tr = _tile(R, 256, SUBLANES)
    c1 = 1.0 / (1.0 - ADAM_B1 ** ADAM_STEP)
    c2 = 1.0 / (1.0 - ADAM_B2 ** ADAM_STEP)

    def body(w_ref, g_ref, m_ref, v_ref, d_ref, nm_ref, nv_ref):
        gv = g_ref[...]
        nm = ADAM_B1 * m_ref[...] + (1.0 - ADAM_B1) * gv
        nv = ADAM_B2 * v_ref[...] + (1.0 - ADAM_B2) * (gv * gv)
        nm_ref[...] = nm
        nv_ref[...] = nv
        d_ref[...] = -ADAM_LR * ((nm * c1) / (jnp.sqrt(nv * c2) + ADAM_EPS) + ADAM_WD * w_ref[...])

    blk = pl.BlockSpec((tr, C), lambda i: (i, 0))
    sh = jax.ShapeDtypeStruct((R, C), F32)
    return pl.pallas_call(
        body, name=name, out_shape=(sh, sh, sh), grid=(R // tr,), in_specs=[blk] * 4, out_specs=(blk,) * 3,
        compiler_params=_params(("parallel",)),
    )(w, g, m, v)


def _adamw_recv(parts, w, m, v, name):
    R, C = w.shape
    tr = _tile(R, 256, SUBLANES)
    c1 = 1.0 / (1.0 - ADAM_B1 ** ADAM_STEP)
    c2 = 1.0 / (1.0 - ADAM_B2 ** ADAM_STEP)

    def body(p_ref, w_ref, m_ref, v_ref, g_ref, d_ref, nm_ref, nv_ref):
        gv = p_ref[0].astype(F32)
        for k in range(1, N_DEV):
            gv = gv + p_ref[k].astype(F32)
        g_ref[...] = gv
        nm = ADAM_B1 * m_ref[...] + (1.0 - ADAM_B1) * gv
        nv = ADAM_B2 * v_ref[...] + (1.0 - ADAM_B2) * (gv * gv)
        nm_ref[...] = nm
        nv_ref[...] = nv
        d_ref[...] = -ADAM_LR * ((nm * c1) / (jnp.sqrt(nv * c2) + ADAM_EPS) + ADAM_WD * w_ref[...])

    blk = pl.BlockSpec((tr, C), lambda i: (i, 0))
    sh = jax.ShapeDtypeStruct((R, C), F32)
    return pl.pallas_call(
        body, name=name, out_shape=(sh, sh, sh, sh), grid=(R // tr,),
        in_specs=[pl.BlockSpec((N_DEV, tr, C), lambda i: (0, i, 0)), blk, blk, blk], out_specs=(blk,) * 4,
        compiler_params=_params(("parallel",)),
    )(parts, w, m, v)


def _sum8(parts, name):
    _, R, C = parts.shape
    tr = _tile(R, 256, SUBLANES)

    def body(p_ref, o_ref):
        acc = p_ref[0].astype(F32)
        for k in range(1, N_DEV):
            acc = acc + p_ref[k].astype(F32)
        o_ref[...] = acc

    return pl.pallas_call(
        body, name=name, out_shape=jax.ShapeDtypeStruct((R, C), F32), grid=(R // tr,),
        in_specs=[pl.BlockSpec((N_DEV, tr, C), lambda i: (0, i, 0))],
        out_specs=pl.BlockSpec((tr, C), lambda i: (i, 0)), compiler_params=_params(("parallel",)),
    )(parts)


def _peers():
    x, y, c = lax.axis_index("x"), lax.axis_index("y"), lax.axis_index("c")
    me = 4 * x + 2 * y + c
    out = []
    for k in range(1, N_DEV):
        kx, ky, kc = (k >> 2) & 1, (k >> 1) & 1, k & 1
        px, py, pc = x ^ kx, y ^ ky, c ^ kc
        out.append(((px, py, pc), 4 * px + 2 * py + pc))
    return me, out


def _exchange(xs, scatter, name):
    n = len(xs)
    np_ = N_DEV - 1

    def body(*refs):
        x_refs, o_refs = refs[:n], refs[n:2 * n]
        send_sems, recv_sems, local_sems = refs[2 * n:]
        me, peers = _peers()
        owns, sends = [], []
        for i, (x_ref, o_ref) in enumerate(zip(x_refs, o_refs)):
            own = pltpu.make_async_copy(x_ref.at[me] if scatter else x_ref, o_ref.at[me], local_sems.at[i])
            own.start()
            owns.append(own)
            for k, (dev, idx) in enumerate(peers):
                cp = pltpu.make_async_remote_copy(
                    src_ref=x_ref.at[idx] if scatter else x_ref, dst_ref=o_ref.at[me],
                    send_sem=send_sems.at[i * np_ + k], recv_sem=recv_sems.at[i * np_ + k],
                    device_id=dev, device_id_type=MESH)
                cp.start()
                sends.append(cp)
        for i, (x_ref, o_ref) in enumerate(zip(x_refs, o_refs)):
            for k, (dev, idx) in enumerate(peers):
                pltpu.make_async_remote_copy(
                    src_ref=x_ref.at[idx] if scatter else x_ref, dst_ref=o_ref.at[idx],
                    send_sem=send_sems.at[i * np_ + k], recv_sem=recv_sems.at[i * np_ + k],
                    device_id=dev, device_id_type=MESH).wait_recv()
        for cp in sends:
            cp.wait_send()
        for own in owns:
            own.wait()

    hbm = pl.BlockSpec(memory_space=pltpu.HBM)
    out_shape = tuple(jax.ShapeDtypeStruct(x.shape if scatter else (N_DEV,) + x.shape, x.dtype) for x in xs)
    return pl.pallas_call(
        body, name=name, out_shape=out_shape, in_specs=[hbm] * n, out_specs=(hbm,) * n,
        scratch_shapes=[pltpu.SemaphoreType.DMA((n * np_,)), pltpu.SemaphoreType.DMA((n * np_,)),
                        pltpu.SemaphoreType.DMA((n,))],
    )(*xs)


def _pack_flat(arrs):
    cat = jnp.concatenate([a.reshape(-1) for a in arrs])
    per = PACK_COLS * SUBLANES
    tot = -(-cat.shape[0] // per) * per
    return jnp.pad(cat, (0, tot - cat.shape[0])).reshape(tot // PACK_COLS, PACK_COLS)


def _unpack_flat(packed, shapes):
    flat = packed.reshape(-1)
    out, o = [], 0
    for s in shapes:
        n = math.prod(s)
        out.append(flat[o:o + n].reshape(s))
        o += n
    return out


def _ffn_fwd(x, wg, wo4, l, g, b, alpha, tag):
    h8 = _ffn_h(x, wg, l, tag + "_h")
    a4 = _swiglu_fwd(h8, tag + "_act")
    y = _ffn_y(a4, wo4, l, tag + "_y")
    out, xh, rstd = _res_ln_fwd(x, y, g, b, alpha, 0.5, tag + "_ln")
    return out, (x, h8, a4, xh, rstd)


def _ffn_bwd(dout, saved, wg, wo4, l, g, alpha, tag):
    x, h8, a4, xh, rstd = saved
    dz, dg, db = _ln_bwd(dout, xh, rstd, g, tag + "_ln_bwd")
    dw_out = _ffn_dwout(a4, dz, 0.5, tag + "_dwout")
    da4 = _ffn_da(dz, wo4, l, 0.5, tag + "_da")
    dh8 = _swiglu_bwd(h8, da4, tag + "_act_bwd")
    dw_in = _ffn_dwin(x, dh8, tag + "_dwin")
    dx = _ffn_dx(dh8, wg, l, dz, alpha, tag + "_dx")
    return dx, dw_in, dw_out.reshape(N_DEV, -1, dw_out.shape[-1]), dg, db


def _heads(t, H):
    S = t.shape[0]
    return t.reshape(S, H, HEAD_DIM).transpose(1, 0, 2)


def _fox_fwd(x, w_in_pad, b_f_pad, w_o, g, b, alpha, tag):
    S, D = x.shape
    H = D // HEAD_DIM
    proj = _mm(x, w_in_pad, name=tag + "_proj")
    q = _heads((proj[:, :D] * (1.0 / math.sqrt(HEAD_DIM))).astype(BF16), H)
    k, v = (_heads(proj[:, i * D:(i + 1) * D].astype(BF16), H) for i in (1, 2))
    fl = proj[:, 3 * D:]
    cum = _fox_gate_fwd(fl, b_f_pad)
    ch = cum[:, :H].T
    t = _tile(S, 512)
    cq, ck = ch[:, :, None], ch.reshape(H, S // t, 1, t)
    o, lse = _flash_fwd(q, k, v, cq, ck)
    o2 = o.transpose(1, 0, 2).reshape(S, D)
    m = _mm(o2, w_o, name=tag + "_out")
    out, xh, rstd = _res_ln_fwd(x, m, g, b, alpha, 1.0, tag + "_ln")
    return out, (x, q, k, v, jnp.concatenate([cq, lse], axis=2), ck, o, o2, fl, xh, rstd)


def _fox_bwd(dout, saved, w_in_pad, b_f_pad, w_o, g, alpha, tag):
    x, q, k, v, rows, ck, o, o2, fl, xh, rstd = saved
    S, D = x.shape
    H = D // HEAD_DIM
    dz, dg, db = _ln_bwd(dout, xh, rstd, g, tag + "_ln_bwd")
    dw_o = _mm(o2, dz, ta=True, name=tag + "_dwo")
    do = _heads(_mm(dz, w_o, tb=True, name=tag + "_do"), H)
    dq, dk, dv, dcq, dck = _flash_bwd(q, k, v, rows, ck, o, do)
    dcum = jnp.pad((dcq[:, :, 0] + dck.reshape(H, S)).T, ((0, 0), (0, LANES - H)))
    dfl, dbf = _fox_gate_bwd(dcum, fl, b_f_pad)
    flat = lambda t: t.transpose(1, 0, 2).reshape(S, D).astype(BF16)
    dproj = jnp.concatenate([flat(dq), flat(dk), flat(dv), dfl.astype(BF16)], axis=1)
    dw_in = _mm(x, dproj, ta=True, name=tag + "_dwin")
    dx = _mm(dproj, w_in_pad, tb=True, add=dz, add_scale=alpha, name=tag + "_dx")
    return dx, dw_in[:, :3 * D + H], dbf[0, :H], dw_o, dg, db


def _s5_mats(p):
    lb_re, lb_im, bb_re, bb_im = _s5_discretise(p["a_re"], p["a_im"], p["log_dt"], p["b_re"], p["b_im"])
    rmat = jnp.concatenate([_s5_expand(bb_re.transpose(0, 2, 1)), _s5_expand(bb_im.transpose(0, 2, 1))], axis=2)
    cmat = jnp.concatenate([_s5_expand(p["c_re"]).transpose(0, 2, 1), -_s5_expand(p["c_im"]).transpose(0, 2, 1)],
                           axis=1)
    lam = jnp.concatenate([_s5_slab(lb_re), _s5_slab(lb_im)], axis=2)
    return rmat.astype(BF16), cmat.astype(BF16), lam


def _s5_block_fwd(x, p, w_out, g, b, alpha, tag):
    S, D = x.shape
    rmat, cmat, lam = _s5_mats(p)
    dskip = p["d"].reshape(1, D)
    y, yg, hs = _s5_fwd(x, rmat, cmat, lam, dskip)
    vg = _mm(yg, w_out, name=tag + "_vg")
    m = _glu_fwd(vg, tag + "_glu")
    out, xh, rstd = _res_ln_fwd(x, m, g, b, alpha, 1.0, tag + "_ln")
    return out, (x, y, yg, hs, vg, rmat, cmat, lam, dskip, xh, rstd)


def _s5_block_bwd(dout, saved, p, w_out, g, alpha, tag):
    x, y, yg, hs, vg, rmat, cmat, lam, dskip, xh, rstd = saved
    S, D = x.shape
    G = D // S5_GROUP
    dz, dg, db = _ln_bwd(dout, xh, rstd, g, tag + "_ln_bwd")
    dvg = _glu_bwd(vg, dz, tag + "_glu_bwd")
    dw_out = _mm(yg, dvg, ta=True, name=tag + "_dwout")
    dyg = _mm(dvg, w_out, tb=True, name=tag + "_dyg")
    dx, dr, dc, dlam, dd = _s5_bwd(x, y, dyg, hs, rmat, cmat, lam, dskip, dz, alpha)
    dbb_re = _s5_extract(dr[:, :, :LANES]).transpose(0, 2, 1)
    dbb_im = _s5_extract(dr[:, :, LANES:]).transpose(0, 2, 1)
    dc_re = _s5_extract(dc[:, :LANES, :].transpose(0, 2, 1))
    dc_im = -_s5_extract(dc[:, LANES:, :].transpose(0, 2, 1))
    dlb_re = dlam[:, :, :LANES].reshape(G, S5_STATE)
    dlb_im = dlam[:, :, LANES:].reshape(G, S5_STATE)
    _, vjp = jax.vjp(_s5_discretise, p["a_re"], p["a_im"], p["log_dt"], p["b_re"], p["b_im"])
    da_re, da_im, dlog_dt, db_re, db_im = vjp((dlb_re, dlb_im, dbb_re, dbb_im))
    grads = dict(a_re=da_re, a_im=da_im, log_dt=dlog_dt, b_re=db_re, b_im=db_im, c_re=dc_re, c_im=dc_im,
                 d=dd.reshape(G, S5_GROUP))
    return dx, grads, dw_out, dg, db


BIG = ("ffn1_w_in", "ffn1_w_out", "ffn2_w_in", "ffn2_w_out", "fox_w_in", "fox_w_o", "s5_w_out")
BIG_SPLIT_COLS = {"ffn1_w_in": True, "ffn1_w_out": False, "ffn2_w_in": True, "ffn2_w_out": False,
                  "fox_w_in": True, "fox_w_o": False, "s5_w_out": True}
SMALL = ("ln1_g", "ln1_b", "lnm_g", "lnm_b", "ln2_g", "ln2_b", "fox_b_f", "s5_a_re", "s5_a_im", "s5_log_dt",
         "s5_b_re", "s5_b_im", "s5_c_re", "s5_c_im", "s5_d")
WEIGHTS = ("ffn1_w_in", "ffn1_w_out", "ln1_g", "ln1_b", "lnm_g", "lnm_b", "ffn2_w_in", "ffn2_w_out", "ln2_g", "ln2_b",
           "fox_w_in", "fox_b_f", "fox_w_o", "s5_a_re", "s5_a_im", "s5_log_dt", "s5_b_re", "s5_b_im", "s5_c_re",
           "s5_c_im", "s5_d", "s5_w_out")


def _join(gathered, split_cols):
    n, L, r, c = gathered.shape
    if split_cols:
        return gathered.transpose(1, 2, 0, 3).reshape(L, r, n * c)
    return gathered.transpose(1, 0, 2, 3).reshape(L, n * r, c)


def _split(full, split_cols):
    L, R, C = full.shape
    if split_cols:
        return full.reshape(L, R, N_DEV, C // N_DEV).transpose(2, 0, 1, 3)
    return full.reshape(L, N_DEV, R // N_DEV, C).transpose(1, 0, 2, 3)


def _local_step(x, target, W, small):
    S, D = x.shape
    H = D // HEAD_DIM
    depth = small["ln1_g"].shape[0]
    alpha = (2.0 * depth) ** 0.25
    fox_in_pad = jnp.pad(W["fox_w_in"], ((0, 0), (0, 0), (0, LANES - H)))
    bf_pad = jnp.pad(small["fox_b_f"], ((0, 0), (0, LANES - H)))

    def s5_params(j):
        return {k: small["s5_" + k][j] for k in ("a_re", "a_im", "log_dt", "b_re", "b_im", "c_re", "c_im", "d")}

    saved = []
    h = x
    for i in range(depth):
        j = i // 2
        h, s1 = _ffn_fwd(h, W["ffn1_w_in"], W["ffn1_w_out"], i, small["ln1_g"][i], small["ln1_b"][i], alpha,
                         f"l{i}_ffn1")
        if i % 2 == 0:
            h, s2 = _fox_fwd(h, fox_in_pad[j], bf_pad[j:j + 1], W["fox_w_o"][j], small["lnm_g"][i],
                             small["lnm_b"][i], alpha, f"l{i}_fox")
        else:
            h, s2 = _s5_block_fwd(h, s5_params(j), W["s5_w_out"][j], small["lnm_g"][i], small["lnm_b"][i], alpha,
                                  f"l{i}_s5")
        h, s3 = _ffn_fwd(h, W["ffn2_w_in"], W["ffn2_w_out"], i, small["ln2_g"][i], small["ln2_b"][i], alpha,
                         f"l{i}_ffn2")
        saved.append((s1, s2, s3))

    dh, loss_part = _loss_fwd_bwd(h, target)

    gW = {k: [None] * (depth if k.startswith("ffn") else W[k].shape[0]) for k in BIG}
    gs = {k: [None] * small[k].shape[0] for k in SMALL}
    for i in reversed(range(depth)):
        j = i // 2
        s1, s2, s3 = saved[i]
        dh, gW["ffn2_w_in"][i], gW["ffn2_w_out"][i], gs["ln2_g"][i], gs["ln2_b"][i] = _ffn_bwd(
            dh, s3, W["ffn2_w_in"], W["ffn2_w_out"], i, small["ln2_g"][i], alpha, f"l{i}_ffn2")
        if i % 2 == 0:
            dh, gW["fox_w_in"][j], gs["fox_b_f"][j], gW["fox_w_o"][j], gs["lnm_g"][i], gs["lnm_b"][i] = _fox_bwd(
                dh, s2, fox_in_pad[j], bf_pad[j:j + 1], W["fox_w_o"][j], small["lnm_g"][i], alpha, f"l{i}_fox")
        else:
            dh, g5, gW["s5_w_out"][j], gs["lnm_g"][i], gs["lnm_b"][i] = _s5_block_bwd(
                dh, s2, s5_params(j), W["s5_w_out"][j], small["lnm_g"][i], alpha, f"l{i}_s5")
            for k, val in g5.items():
                gs["s5_" + k][j] = val
        dh, gW["ffn1_w_in"][i], gW["ffn1_w_out"][i], gs["ln1_g"][i], gs["ln1_b"][i] = _ffn_bwd(
            dh, s1, W["ffn1_w_in"], W["ffn1_w_out"], i, small["ln1_g"][i], alpha, f"l{i}_ffn1")
    shards = {}
    for k in BIG:
        if k.startswith("ffn"):
            shards[k] = jnp.stack(gW[k], axis=1)
        else:
            shards[k] = _split(jnp.stack(gW[k]), BIG_SPLIT_COLS[k]).astype(BF16)
    gs = {k: jnp.stack(v) for k, v in gs.items()}
    return loss_part, dh, shards, gs


def kernel(x, ffn1_w_in, ffn1_w_out, ln1_g, ln1_b, lnm_g, lnm_b, ffn2_w_in, ffn2_w_out, ln2_g, ln2_b, fox_w_in, fox_b_f, fox_w_o, s5_a_re, s5_a_im, s5_log_dt, s5_b_re, s5_b_im, s5_c_re, s5_c_im, s5_d, s5_w_out, loss_target, m_ffn1_w_in, m_ffn1_w_out, m_ln1_g, m_ln1_b, m_lnm_g, m_lnm_b, m_ffn2_w_in, m_ffn2_w_out, m_ln2_g, m_ln2_b, m_fox_w_in, m_fox_b_f, m_fox_w_o, m_s5_a_re, m_s5_a_im, m_s5_log_dt, m_s5_b_re, m_s5_b_im, m_s5_c_re, m_s5_c_im, m_s5_d, m_s5_w_out, v_ffn1_w_in, v_ffn1_w_out, v_ln1_g, v_ln1_b, v_lnm_g, v_lnm_b, v_ffn2_w_in, v_ffn2_w_out, v_ln2_g, v_ln2_b, v_fox_w_in, v_fox_b_f, v_fox_w_o, v_s5_a_re, v_s5_a_im, v_s5_log_dt, v_s5_b_re, v_s5_b_im, v_s5_c_re, v_s5_c_im, v_s5_d, v_s5_w_out):
    args = dict(locals())
    w = {k: args[k] for k in WEIGHTS}
    m = {k: args["m_" + k] for k in WEIGHTS}
    v = {k: args["v_" + k] for k in WEIGHTS}
    small = {k: w[k] for k in SMALL}

    gathered = dict(zip(BIG, _exchange([w[k].astype(BF16) for k in BIG], False, "gather_weights")))
    W = {}
    for k in BIG:
        g8 = gathered[k]
        if k.endswith("ffn1_w_in") or k.endswith("ffn2_w_in"):
            W[k] = g8
        elif k.startswith("ffn"):
            n, L, r, c = g8.shape
            W[k] = g8.transpose(1, 0, 2, 3).reshape(L, n // 2, 2 * r, c)
        else:
            W[k] = _join(g8, BIG_SPLIT_COLS[k])

    loss_part, dx, shards, gs = _local_step(x[0], loss_target[0], W, small)
    loss = lax.psum(loss_part, ("x", "y", "c"))

    recv = dict(zip(BIG, _exchange([shards[k] for k in BIG], True, "scatter_grads")))
    small_shapes = [w[k].shape for k in SMALL]
    (g_small_all,) = _exchange([_pack_flat([gs[k] for k in SMALL])], False, "gather_small_grads")
    g_small_flat = _sum8(g_small_all, "sum_small_grads")

    grads, delta, new_m, new_v = {}, {}, {}, {}
    for k in BIG:
        sh = w[k].shape
        two = lambda a: a.reshape(-1, sh[-1])
        outs = _adamw_recv(recv[k].reshape(N_DEV, -1, sh[-1]), two(w[k]), two(m[k]), two(v[k]), "adamw_" + k)
        grads[k], delta[k], new_m[k], new_v[k] = (o.reshape(sh) for o in outs)
    pk = lambda d: _pack_flat([d[k] for k in SMALL])
    d_, m_, v_ = _adamw(pk(w), g_small_flat, pk(m), pk(v), "adamw_small")
    for dst, flat in ((grads, g_small_flat), (delta, d_), (new_m, m_), (new_v, v_)):
        dst.update(zip(SMALL, _unpack_flat(flat, small_shapes)))

    return (loss, dx[None], *[grads[k] for k in WEIGHTS], *[delta[k] for k in WEIGHTS],
            *[new_m[k] for k in WEIGHTS], *[new_v[k] for k in WEIGHTS])
```

```python
import functools
import math

import jax
import jax.numpy as jnp
from jax import lax
from jax.experimental import pallas as pl
from jax.experimental.pallas import tpu as pltpu

F32 = jnp.float32
BF16 = jnp.bfloat16

N_DEV = 8
HEAD_DIM = 64
S5_GROUP = 16
S5_STATE = 64
LANES = 128
SUBLANES = 8
S5_PART = 256
LN_EPS = 1e-5
NEG_INF = -1e30
ADAM_LR, ADAM_B1, ADAM_B2, ADAM_EPS, ADAM_WD, ADAM_STEP = 0.001, 0.9, 0.999, 1e-08, 0.01, 10
VMEM_LIMIT = 48 * 1024 * 1024
PACK_COLS = 1024
PACK_ROW_ALIGN = 512

MESH = pl.DeviceIdType.MESH


def _tile(dim, pref, align=LANES):
    if dim <= pref:
        return dim
    t = (pref // align) * align
    while t >= align:
        if dim % t == 0:
            return t
        t -= align
    return dim


def _params(sem):
    return pltpu.CompilerParams(dimension_semantics=sem, vmem_limit_bytes=VMEM_LIMIT)


def _mm(a, b, *, ta=False, tb=False, out_dtype=F32, scale=None, add=None, add_scale=1.0,
        tm=512, tn=1408, tk=1408, name="mm"):
    if ta:
        K, M = a.shape
    else:
        M, K = a.shape
    if tb:
        N, K2 = b.shape
    else:
        K2, N = b.shape
    assert K == K2, (a.shape, b.shape, ta, tb)
    tm, tn, tk = _tile(M, tm), _tile(N, tn), _tile(K, tk)
    a_spec = pl.BlockSpec((tk, tm), lambda i, j, k: (k, i)) if ta else pl.BlockSpec((tm, tk), lambda i, j, k: (i, k))
    b_spec = pl.BlockSpec((tn, tk), lambda i, j, k: (j, k)) if tb else pl.BlockSpec((tk, tn), lambda i, j, k: (k, j))
    o_spec = pl.BlockSpec((tm, tn), lambda i, j, k: (i, j))
    return _mm_core(name, a, b, a_spec, b_spec, o_spec, jax.ShapeDtypeStruct((M, N), out_dtype),
                    (M // tm, N // tn, K // tk), (tm, tn), ta, tb, scale, add, add_scale)


def _mm_core(name, a, b, a_spec, b_spec, o_spec, out_shape, grid, acc_shape, ta, tb, scale=None, add=None,
             add_scale=1.0):
    nk = grid[2]

    def body(*refs):
        if add is None:
            a_ref, b_ref, o_ref, acc = refs
            add_ref = None
        else:
            a_ref, b_ref, add_ref, o_ref, acc = refs
        k = pl.program_id(2)

        @pl.when(k == 0)
        def _():
            acc[...] = jnp.zeros_like(acc)

        dims = (((0 if ta else 1,), (1 if tb else 0,)), ((), ()))
        acc[...] += lax.dot_general(a_ref[...].astype(BF16), b_ref[...].astype(BF16), dims,
                                    preferred_element_type=F32)

        @pl.when(k == nk - 1)
        def _():
            r = acc[...]
            if scale is not None:
                r = r * scale
            if add_ref is not None:
                r = r + add_scale * add_ref[...]
            o_ref[...] = r.astype(out_shape.dtype)

    in_specs = [a_spec, b_spec]
    args = [a, b]
    if add is not None:
        in_specs.append(o_spec)
        args.append(add)
    return pl.pallas_call(
        body, name=name, out_shape=out_shape, grid=grid, in_specs=in_specs, out_specs=o_spec,
        scratch_shapes=[pltpu.VMEM(acc_shape, F32)],
        compiler_params=_params(("parallel", "parallel", "arbitrary")),
    )(*args)


def _ffn_h(x, wg, name):
    S, D = x.shape
    n, _, c = wg.shape
    tm = _tile(S, 1024, SUBLANES)
    return _mm_core(name, x, wg, pl.BlockSpec((tm, D), lambda i, j, k: (i, 0)),
                    pl.BlockSpec((None, D, c), lambda i, j, k: (j, 0, 0)),
                    pl.BlockSpec((None, tm, c), lambda i, j, k: (j, i, 0)),
                    jax.ShapeDtypeStruct((n, S, c), BF16), (S // tm, n, 1), (tm, c), False, False)


def _ffn_y(a4, wo4, name):
    nb, S, c = a4.shape
    D = wo4.shape[-1]
    tm, tn = _tile(S, 1024, SUBLANES), _tile(D, 1024)
    return _mm_core(name, a4, wo4, pl.BlockSpec((None, tm, c), lambda i, j, k: (k, i, 0)),
                    pl.BlockSpec((None, c, tn), lambda i, j, k: (k, 0, j)),
                    pl.BlockSpec((tm, tn), lambda i, j, k: (i, j)),
                    jax.ShapeDtypeStruct((S, D), F32), (S // tm, D // tn, nb), (tm, tn), False, False)


def _ffn_dwout(a4, dz, scale, name):
    nb, S, c = a4.shape
    D = dz.shape[1]
    tk, tn = _tile(S, 1024, SUBLANES), _tile(D, 1024)
    return _mm_core(name, a4, dz, pl.BlockSpec((None, tk, c), lambda i, j, k: (i, k, 0)),
                    pl.BlockSpec((tk, tn), lambda i, j, k: (k, j)),
                    pl.BlockSpec((None, c, tn), lambda i, j, k: (i, 0, j)),
                    jax.ShapeDtypeStruct((nb, c, D), BF16), (nb, D // tn, S // tk), (c, tn), True, False, scale)


def _ffn_da(dz, wo4, scale, name):
    S, D = dz.shape
    nb, c = wo4.shape[0], wo4.shape[1]
    tm = _tile(S, 1024, SUBLANES)
    return _mm_core(name, dz, wo4, pl.BlockSpec((tm, D), lambda i, j, k: (i, 0)),
                    pl.BlockSpec((None, c, D), lambda i, j, k: (j, 0, 0)),
                    pl.BlockSpec((None, tm, c), lambda i, j, k: (j, i, 0)),
                    jax.ShapeDtypeStruct((nb, S, c), F32), (S // tm, nb, 1), (tm, c), False, True, scale)


def _ffn_dwin(x, dh8, name):
    S, D = x.shape
    n, _, c = dh8.shape
    tk, tm = _tile(S, 1024, SUBLANES), _tile(D, 1024)
    return _mm_core(name, x, dh8, pl.BlockSpec((tk, tm), lambda i, j, k: (k, j)),
                    pl.BlockSpec((None, tk, c), lambda i, j, k: (i, k, 0)),
                    pl.BlockSpec((None, tm, c), lambda i, j, k: (i, j, 0)),
                    jax.ShapeDtypeStruct((n, D, c), BF16), (n, D // tm, S // tk), (tm, c), True, False)


def _ffn_dx(dh8, wg, dz, alpha, name):
    n, S, c = dh8.shape
    D = wg.shape[1]
    tm, tn = _tile(S, 1024, SUBLANES), _tile(D, 1024)
    return _mm_core(name, dh8, wg, pl.BlockSpec((None, tm, c), lambda i, j, k: (k, i, 0)),
                    pl.BlockSpec((None, tn, c), lambda i, j, k: (k, j, 0)),
                    pl.BlockSpec((tm, tn), lambda i, j, k: (i, j)),
                    jax.ShapeDtypeStruct((S, D), F32), (S // tm, D // tn, n), (tm, tn), False, True,
                    None, dz, alpha)


def _res_ln_fwd(x, y, g, b, alpha, s, name):
    S, D = x.shape
    tm = _tile(S, 256, SUBLANES)

    def body(x_ref, y_ref, g_ref, b_ref, o_ref, xh_ref, r_ref):
        z = alpha * x_ref[...] + s * y_ref[...]
        mu = jnp.mean(z, axis=-1, keepdims=True)
        zc = z - mu
        var = jnp.mean(zc * zc, axis=-1, keepdims=True)
        rstd = lax.rsqrt(var + LN_EPS)
        xh = zc * rstd
        xh_ref[...] = xh
        r_ref[...] = rstd
        o_ref[...] = xh * g_ref[...] + b_ref[...]

    row = pl.BlockSpec((tm, D), lambda i: (i, 0))
    vec = pl.BlockSpec((1, D), lambda i: (0, 0))
    return pl.pallas_call(
        body, name=name,
        out_shape=(jax.ShapeDtypeStruct((S, D), F32), jax.ShapeDtypeStruct((S, D), F32),
                   jax.ShapeDtypeStruct((S, 1), F32)),
        grid=(S // tm,), in_specs=[row, row, vec, vec],
        out_specs=(row, row, pl.BlockSpec((tm, 1), lambda i: (i, 0))),
        compiler_params=_params(("parallel",)),
    )(x, y, g.reshape(1, D), b.reshape(1, D))


def _ln_bwd(dout, xh, rstd, g, name):
    S, D = dout.shape
    tm = _tile(S, 256, SUBLANES)

    def body(d_ref, xh_ref, r_ref, g_ref, dz_ref, dg_ref, db_ref):
        i = pl.program_id(0)

        @pl.when(i == 0)
        def _():
            dg_ref[...] = jnp.zeros_like(dg_ref)
            db_ref[...] = jnp.zeros_like(db_ref)

        d = d_ref[...]
        xhv = xh_ref[...]
        dxh = d * g_ref[...]
        m1 = jnp.mean(dxh, axis=-1, keepdims=True)
        m2 = jnp.mean(dxh * xhv, axis=-1, keepdims=True)
        dz_ref[...] = r_ref[...] * (dxh - m1 - xhv * m2)
        dg_ref[...] += jnp.sum(d * xhv, axis=0, keepdims=True)
        db_ref[...] += jnp.sum(d, axis=0, keepdims=True)

    row = pl.BlockSpec((tm, D), lambda i: (i, 0))
    vec = pl.BlockSpec((1, D), lambda i: (0, 0))
    dz, dg, db = pl.pallas_call(
        body, name=name,
        out_shape=(jax.ShapeDtypeStruct((S, D), F32), jax.ShapeDtypeStruct((1, D), F32),
                   jax.ShapeDtypeStruct((1, D), F32)),
        grid=(S // tm,), in_specs=[row, row, pl.BlockSpec((tm, 1), lambda i: (i, 0)), vec],
        out_specs=(row, vec, vec),
        compiler_params=_params(("arbitrary",)),
    )(dout, xh, rstd, g.reshape(1, D))
    return dz, dg[0], db[0]


def _sigmoid(x):
    e = jnp.exp(-jnp.abs(x))
    return jnp.where(x >= 0, 1.0 / (1.0 + e), e / (1.0 + e))


def _swiglu_fwd(h8, name):
    n, S, c = h8.shape
    nb = n // 2
    tm = _tile(S, 512, SUBLANES)

    def body(h_ref, o_ref):
        g = h_ref[0].astype(F32)
        u = h_ref[1].astype(F32)
        o_ref[...] = (g * _sigmoid(g) * u).astype(BF16)

    return pl.pallas_call(
        body, name=name, out_shape=jax.ShapeDtypeStruct((nb, S, c), BF16), grid=(nb, S // tm),
        in_specs=[pl.BlockSpec((2, None, tm, c), lambda k, i: (0, k, i, 0))],
        out_specs=pl.BlockSpec((None, tm, c), lambda k, i: (k, i, 0)),
        compiler_params=_params(("parallel", "parallel")),
    )(h8.reshape(2, nb, S, c))


def _swiglu_bwd(h8, da4, name):
    n, S, c = h8.shape
    nb = n // 2
    tm = _tile(S, 512, SUBLANES)

    def body(h_ref, da_ref, d_ref):
        g = h_ref[0].astype(F32)
        u = h_ref[1].astype(F32)
        da_v = da_ref[...]
        sg = _sigmoid(g)
        silu = g * sg
        d_ref[0] = (da_v * u * (sg + silu * (1.0 - sg))).astype(BF16)
        d_ref[1] = (da_v * silu).astype(BF16)

    pair = pl.BlockSpec((2, None, tm, c), lambda k, i: (0, k, i, 0))
    dh = pl.pallas_call(
        body, name=name, out_shape=jax.ShapeDtypeStruct((2, nb, S, c), BF16), grid=(nb, S // tm),
        in_specs=[pair, pl.BlockSpec((None, tm, c), lambda k, i: (k, i, 0))], out_specs=pair,
        compiler_params=_params(("parallel", "parallel")),
    )(h8.reshape(2, nb, S, c), da4)
    return dh.reshape(n, S, c)


def _glu_fwd(vg, name):
    S, D2 = vg.shape
    D = D2 // 2
    tm = _tile(S, 512, SUBLANES)

    def body(v_ref, g_ref, o_ref):
        o_ref[...] = v_ref[...] * _sigmoid(g_ref[...])

    return pl.pallas_call(
        body, name=name, out_shape=jax.ShapeDtypeStruct((S, D), F32), grid=(S // tm,),
        in_specs=[pl.BlockSpec((tm, D), lambda i: (i, 0)), pl.BlockSpec((tm, D), lambda i: (i, 1))],
        out_specs=pl.BlockSpec((tm, D), lambda i: (i, 0)),
        compiler_params=_params(("parallel",)),
    )(vg, vg)


def _glu_bwd(vg, dm, name):
    S, D2 = vg.shape
    D = D2 // 2
    tm = _tile(S, 512, SUBLANES)

    def body(v_ref, g_ref, dm_ref, dv_ref, dg_ref):
        sg = _sigmoid(g_ref[...])
        d = dm_ref[...]
        dv_ref[...] = (d * sg).astype(BF16)
        dg_ref[...] = (d * v_ref[...] * sg * (1.0 - sg)).astype(BF16)

    blk = pl.BlockSpec((tm, D), lambda i: (i, 0))
    dv, dg = pl.pallas_call(
        body, name=name,
        out_shape=(jax.ShapeDtypeStruct((S, D), BF16), jax.ShapeDtypeStruct((S, D), BF16)),
        grid=(S // tm,), in_specs=[blk, pl.BlockSpec((tm, D), lambda i: (i, 1)), blk],
        out_specs=(blk, blk), compiler_params=_params(("parallel",)),
    )(vg, vg, dm)
    return jnp.concatenate([dv, dg], axis=1)


def _loss_fwd_bwd(y, target):
    S, D = y.shape
    tm = _tile(S, 256, SUBLANES)

    def body(y_ref, t_ref, dy_ref, l_ref):
        i = pl.program_id(0)

        @pl.when(i == 0)
        def _():
            l_ref[...] = jnp.zeros_like(l_ref)

        e = y_ref[...] - t_ref[...]
        dy_ref[...] = e * (1.0 / D)
        l_ref[...] += jnp.sum(e * e, axis=0, keepdims=True) * (0.5 / D)

    row = pl.BlockSpec((tm, D), lambda i: (i, 0))
    dy, part = pl.pallas_call(
        body, name="loss", out_shape=(jax.ShapeDtypeStruct((S, D), F32), jax.ShapeDtypeStruct((1, D), F32)),
        grid=(S // tm,), in_specs=[row, row], out_specs=(row, pl.BlockSpec((1, D), lambda i: (0, 0))),
        compiler_params=_params(("arbitrary",)),
    )(y, target)
    return dy, jnp.sum(part)


def _tri(n, lower):
    r = lax.broadcasted_iota(jnp.int32, (n, n), 0)
    c = lax.broadcasted_iota(jnp.int32, (n, n), 1)
    return jnp.where((c <= r) if lower else (c >= r), 1.0, 0.0)


def _fox_gate_fwd(fl, bf):
    S, W = fl.shape
    tm = _tile(S, 256, SUBLANES)

    def body(fl_ref, b_ref, c_ref, carry):
        i = pl.program_id(0)

        @pl.when(i == 0)
        def _():
            carry[...] = jnp.zeros_like(carry)

        x = fl_ref[...] + b_ref[...]
        lf = jnp.minimum(x, 0.0) - jnp.log(1.0 + jnp.exp(-jnp.abs(x)))
        c_ref[...] = jnp.dot(_tri(tm, True), lf, precision=lax.Precision.HIGHEST,
                             preferred_element_type=F32) + carry[...]
        carry[...] += jnp.sum(lf, axis=0, keepdims=True)

    blk = pl.BlockSpec((tm, W), lambda i: (i, 0))
    return pl.pallas_call(
        body, name="fox_gate_fwd", out_shape=jax.ShapeDtypeStruct((S, W), F32), grid=(S // tm,),
        in_specs=[blk, pl.BlockSpec((1, W), lambda i: (0, 0))], out_specs=blk,
        scratch_shapes=[pltpu.VMEM((1, W), F32)], compiler_params=_params(("arbitrary",)),
    )(fl, bf)


def _fox_gate_bwd(dcum, fl, bf):
    S, W = fl.shape
    tm = _tile(S, 256, SUBLANES)
    nb = S // tm

    def body(dc_ref, fl_ref, b_ref, dfl_ref, db_ref, carry):
        i = pl.program_id(0)

        @pl.when(i == 0)
        def _():
            carry[...] = jnp.zeros_like(carry)
            db_ref[...] = jnp.zeros_like(db_ref)

        dc = dc_ref[...]
        r = jnp.dot(_tri(tm, False), dc, precision=lax.Precision.HIGHEST, preferred_element_type=F32) + carry[...]
        carry[...] += jnp.sum(dc, axis=0, keepdims=True)
        x = fl_ref[...] + b_ref[...]
        dfl = r * (1.0 - _sigmoid(x))
        dfl_ref[...] = dfl
        db_ref[...] += jnp.sum(dfl, axis=0, keepdims=True)

    blk = pl.BlockSpec((tm, W), lambda i: (nb - 1 - i, 0))
    vec = pl.BlockSpec((1, W), lambda i: (0, 0))
    return pl.pallas_call(
        body, name="fox_gate_bwd",
        out_shape=(jax.ShapeDtypeStruct((S, W), F32), jax.ShapeDtypeStruct((1, W), F32)),
        grid=(nb,), in_specs=[blk, blk, vec], out_specs=(blk, vec),
        scratch_shapes=[pltpu.VMEM((1, W), F32)], compiler_params=_params(("arbitrary",)),
    )(dcum, fl, bf)


def _causal(t):
    row = lax.broadcasted_iota(jnp.int32, (t, t), 0)
    col = lax.broadcasted_iota(jnp.int32, (t, t), 1)
    return col <= row


def _flash_fwd(q, k, v, cq, ck, ex=None):
    H, S, hd = q.shape
    t = _tile(S, 512)
    nq = S // t

    def body(q_ref, k_ref, v_ref, cq_ref, ck_ref, o_ref, lse_ref):
        qi = pl.program_id(1)
        qv = q_ref[0]
        cqv = cq_ref[0]

        def block(ki, carry, masked):
            m_old, l_old, acc = carry
            o = pl.multiple_of(ki * t, t)
            s = lax.dot_general(qv, k_ref[0, pl.ds(o, t), :], (((1,), (1,)), ((), ())),
                                preferred_element_type=F32)
            s = s + (cqv - ck_ref[0, ki])
            if masked:
                s = jnp.where(_causal(t), s, NEG_INF)
            m_new = jnp.maximum(m_old, jnp.max(s, axis=1, keepdims=True))
            p = jnp.exp(s - m_new)
            corr = jnp.exp(m_old - m_new)
            l_new = corr * l_old + jnp.sum(p, axis=1, keepdims=True)
            acc = corr * acc + jnp.dot(p.astype(BF16), v_ref[0, pl.ds(o, t), :], preferred_element_type=F32)
            return m_new, l_new, acc

        init = (jnp.full((t, 1), NEG_INF, F32), jnp.zeros((t, 1), F32), jnp.zeros((t, hd), F32))
        carry = lax.fori_loop(0, qi, lambda ki, c: block(ki, c, False), init)
        m, l, acc = block(qi, carry, True)
        o_ref[0] = acc / l
        lse_ref[0] = m + jnp.log(l)

    qs = pl.BlockSpec((1, t, hd), lambda h, i: (h, i, 0))
    full = pl.BlockSpec((1, S, hd), lambda h, i: (h, 0, 0))
    q1 = pl.BlockSpec((1, t, 1), lambda h, i: (h, i, 0))
    (o, lse), extra = _call(
        body, name="fox_attn_fwd",
        out_shape=(jax.ShapeDtypeStruct((H, S, hd), F32), jax.ShapeDtypeStruct((H, S, 1), F32)),
        grid=(H, nq), in_specs=[qs, full, full, q1, pl.BlockSpec((1, nq, 1, t), lambda h, i: (h, 0, 0, 0))],
        out_specs=(qs, q1), scratch_shapes=[], args=(q, k, v, cq, ck), semantics=("parallel", "parallel"), ex=ex)
    return o, lse, extra


def _flash_bwd(q, k, v, rows, ck, o, do, ex=None):
    H, S, hd = q.shape
    t = _tile(S, 512)
    nb = S // t
    scale = 1.0 / math.sqrt(hd)

    def body(q_ref, k_ref, v_ref, rows_ref, ck_ref, o_ref, do_ref, dq_ref, dk_ref, dv_ref, dcq_ref, dck_ref,
             dk_acc, dv_acc, dc_acc):
        kb = pl.program_id(1)

        @pl.when(kb == 0)
        def _():
            dq_ref[...] = jnp.zeros_like(dq_ref)
            dcq_ref[...] = jnp.zeros_like(dcq_ref)

        dk_acc[...] = jnp.zeros_like(dk_acc)
        dv_acc[...] = jnp.zeros_like(dv_acc)
        dc_acc[...] = jnp.zeros_like(dc_acc)
        kv = k_ref[0]
        vv = v_ref[0]
        ckv = ck_ref[0, 0]

        def block(qb, masked):
            sl = pl.ds(pl.multiple_of(qb * t, t), t)
            qv = q_ref[0, sl, :]
            dov = do_ref[0, sl, :]
            rv = rows_ref[0, sl, :]
            s = lax.dot_general(qv, kv, (((1,), (1,)), ((), ())), preferred_element_type=F32)
            s = s + (rv[:, 0:1] - ckv)
            if masked:
                s = jnp.where(_causal(t), s, NEG_INF)
            p = jnp.exp(s - rv[:, 1:2])
            dob = dov.astype(BF16)
            dv_acc[...] += lax.dot_general(p.astype(BF16), dob, (((0,), (0,)), ((), ())),
                                           preferred_element_type=F32)
            dp = lax.dot_general(dob, vv, (((1,), (1,)), ((), ())), preferred_element_type=F32)
            delta = jnp.sum(dov * o_ref[0, sl, :], axis=1, keepdims=True)
            ds = p * (dp - delta)
            dsb = ds.astype(BF16)
            dk_acc[...] += lax.dot_general(dsb, qv, (((0,), (0,)), ((), ())), preferred_element_type=F32)
            dq_ref[0, sl, :] += jnp.dot(dsb, kv, preferred_element_type=F32) * scale
            dcq_ref[0, sl, :] += jnp.sum(ds, axis=1, keepdims=True)
            dc_acc[...] -= jnp.sum(ds, axis=0, keepdims=True)

        block(kb, True)

        def rest(qb, c):
            block(qb, False)
            return c

        lax.fori_loop(kb + 1, nb, rest, 0)
        dk_ref[0] = dk_acc[...]
        dv_ref[0] = dv_acc[...]
        dck_ref[0, 0] = dc_acc[...]

    full = pl.BlockSpec((1, S, hd), lambda h, j: (h, 0, 0))
    blk = pl.BlockSpec((1, t, hd), lambda h, j: (h, j, 0))
    full1 = pl.BlockSpec((1, S, 1), lambda h, j: (h, 0, 0))
    ck_s = pl.BlockSpec((1, 1, 1, t), lambda h, j: (h, j, 0, 0))
    f32 = lambda *s: jax.ShapeDtypeStruct(s, F32)
    outs, extra = _call(
        body, name="fox_attn_bwd",
        out_shape=(f32(H, S, hd), f32(H, S, hd), f32(H, S, hd), f32(H, S, 1), f32(H, nb, 1, t)),
        grid=(H, nb),
        in_specs=[full, blk, blk, pl.BlockSpec((1, S, 2), lambda h, j: (h, 0, 0)), ck_s, full, full],
        out_specs=(full, blk, blk, full1, ck_s),
        scratch_shapes=[pltpu.VMEM((t, hd), F32), pltpu.VMEM((t, hd), F32), pltpu.VMEM((1, t), F32)],
        args=(q, k, v, rows, ck, o, do), semantics=("parallel", "arbitrary"), ex=ex)
    return (*outs, extra)


def _s5_consts(T):
    rows = T * SUBLANES
    rr = lax.broadcasted_iota(jnp.int32, (rows, T), 0)
    tt = lax.broadcasted_iota(jnp.int32, (rows, T), 1)
    rep = jnp.where(rr // SUBLANES == tt, 1.0, 0.0).astype(BF16)
    r2 = lax.broadcasted_iota(jnp.int32, (rows, S5_PART), 0)
    c2 = lax.broadcasted_iota(jnp.int32, (rows, S5_PART), 1)
    mask = (c2 // (S5_PART // SUBLANES)) == (r2 % SUBLANES)
    return rep, mask


def _gelu(y):
    c = math.sqrt(2.0 / math.pi)
    return 0.5 * y * (1.0 + jnp.tanh(c * (y + 0.044715 * y * y * y)))


def _gelu_grad(y):
    c = math.sqrt(2.0 / math.pi)
    th = jnp.tanh(c * (y + 0.044715 * y * y * y))
    return 0.5 * (1.0 + th) + 0.5 * y * (1.0 - th * th) * c * (1.0 + 3.0 * 0.044715 * y * y)


def _s5_fwd(x, rmat, cmat, lam, dskip, ex=None):
    S, D = x.shape
    NQ = D // S5_PART
    T = _tile(S, 128, SUBLANES)
    rows = T * SUBLANES

    def body(x_ref, r_ref, c_ref, lam_ref, d_ref, y_ref, yg_ref, h_ref, bu_s, carry):
        i = pl.program_id(1)

        @pl.when(i == 0)
        def _():
            carry[...] = jnp.zeros_like(carry)

        rep, mask = _s5_consts(T)
        xv = x_ref[...]
        xrep = jnp.dot(rep, xv.astype(BF16), preferred_element_type=F32)
        lx = jnp.where(mask, xrep, 0.0).astype(BF16)
        bu_s[...] = jnp.dot(lx, r_ref[0], preferred_element_type=F32)
        ar = lam_ref[0, :, 0:LANES]
        ai = lam_ref[0, :, LANES:2 * LANES]

        def step(t, c):
            hr, hi = c
            o = pl.multiple_of(t * SUBLANES, SUBLANES)
            sl = bu_s[pl.ds(o, SUBLANES), :]
            nhr = ar * hr - ai * hi + sl[:, 0:LANES]
            nhi = ar * hi + ai * hr + sl[:, LANES:2 * LANES]
            h_ref[0, pl.ds(o, SUBLANES), 0:LANES] = nhr
            h_ref[0, pl.ds(o, SUBLANES), LANES:2 * LANES] = nhi
            return nhr, nhi

        hr, hi = lax.fori_loop(0, T, step, (carry[:, 0:LANES], carry[:, LANES:2 * LANES]))
        carry[:, 0:LANES] = hr
        carry[:, LANES:2 * LANES] = hi
        z = jnp.dot(h_ref[0].astype(BF16), c_ref[0], preferred_element_type=F32)
        z = jnp.where(mask, z, 0.0)
        y = jnp.sum(z.reshape(T, SUBLANES, S5_PART), axis=1) + d_ref[...] * xv
        y_ref[...] = y
        yg_ref[...] = _gelu(y).astype(BF16)

    xs = pl.BlockSpec((T, S5_PART), lambda q, i: (i, q))
    ms = pl.BlockSpec((1, S5_PART, S5_PART), lambda q, i: (q, 0, 0))
    outs, extra = _call(
        body, name="s5_scan_fwd",
        out_shape=(jax.ShapeDtypeStruct((S, D), F32), jax.ShapeDtypeStruct((S, D), BF16),
                   jax.ShapeDtypeStruct((NQ, S * SUBLANES, S5_PART), F32)),
        grid=(NQ, S // T),
        in_specs=[xs, ms, ms, pl.BlockSpec((1, SUBLANES, S5_PART), lambda q, i: (q, 0, 0)),
                  pl.BlockSpec((1, S5_PART), lambda q, i: (0, q))],
        out_specs=(xs, xs, pl.BlockSpec((1, rows, S5_PART), lambda q, i: (q, i, 0))),
        scratch_shapes=[pltpu.VMEM((rows, S5_PART), F32), pltpu.VMEM((SUBLANES, S5_PART), F32)],
        args=(x, rmat, cmat, lam, dskip), semantics=("parallel", "arbitrary"), ex=ex)
    return (*outs, extra)


def _s5_bwd(x, y, dyg, hs, rmat, cmat, lam, dskip, res, res_scale, ex=None):
    S, D = x.shape
    NQ = D // S5_PART
    T = _tile(S, 128, SUBLANES)
    nb = S // T
    rows = T * SUBLANES

    def body(x_ref, y_ref, dyg_ref, res_ref, h_ref, hp_ref, r_ref, c_ref, lam_ref, d_ref,
             dx_ref, dr_ref, dc_ref, dlam_ref, dd_ref, dh_s, g_s, hs_s, carry):
        i = pl.program_id(1)

        @pl.when(i == 0)
        def _():
            carry[...] = jnp.zeros_like(carry)
            dr_ref[...] = jnp.zeros_like(dr_ref)
            dc_ref[...] = jnp.zeros_like(dc_ref)
            dlam_ref[...] = jnp.zeros_like(dlam_ref)
            dd_ref[...] = jnp.zeros_like(dd_ref)

        rep, mask = _s5_consts(T)
        xv = x_ref[...]
        dy = dyg_ref[...] * _gelu_grad(y_ref[...])
        dyrep = jnp.dot(rep, dy.astype(BF16), preferred_element_type=F32)
        ldy = jnp.where(mask, dyrep, 0.0).astype(BF16)
        dh_s[...] = lax.dot_general(ldy, c_ref[0], (((1,), (1,)), ((), ())), preferred_element_type=F32)
        ar = lam_ref[0, :, 0:LANES]
        ai = lam_ref[0, :, LANES:2 * LANES]

        def step(n, c):
            gr, gi = c
            o = pl.multiple_of((T - 1 - n) * SUBLANES, SUBLANES)
            sl = dh_s[pl.ds(o, SUBLANES), :]
            ngr = sl[:, 0:LANES] + ar * gr + ai * gi
            ngi = sl[:, LANES:2 * LANES] - ai * gr + ar * gi
            g_s[pl.ds(o, SUBLANES), 0:LANES] = ngr
            g_s[pl.ds(o, SUBLANES), LANES:2 * LANES] = ngi
            return ngr, ngi

        gr, gi = lax.fori_loop(0, T, step, (carry[:, 0:LANES], carry[:, LANES:2 * LANES]))
        carry[:, 0:LANES] = gr
        carry[:, LANES:2 * LANES] = gi

        hv = h_ref[0]
        hs_s[0:SUBLANES, :] = jnp.where(i == nb - 1, 0.0, hp_ref[0])
        hs_s[SUBLANES:rows + SUBLANES, :] = hv
        hprev = hs_s[0:rows, :]
        gv = g_s[...]
        g_re, g_im = gv[:, 0:LANES], gv[:, LANES:2 * LANES]
        hp_re, hp_im = hprev[:, 0:LANES], hprev[:, LANES:2 * LANES]
        dar = jnp.sum((g_re * hp_re + g_im * hp_im).reshape(T, SUBLANES, LANES), axis=0)
        dai = jnp.sum((g_im * hp_re - g_re * hp_im).reshape(T, SUBLANES, LANES), axis=0)
        dlam_ref[0, :, 0:LANES] += dar
        dlam_ref[0, :, LANES:2 * LANES] += dai

        gb = gv.astype(BF16)
        xrep = jnp.dot(rep, xv.astype(BF16), preferred_element_type=F32)
        lx = jnp.where(mask, xrep, 0.0).astype(BF16)
        dr_ref[0] += lax.dot_general(lx, gb, (((0,), (0,)), ((), ())), preferred_element_type=F32)
        dc_ref[0] += lax.dot_general(hv.astype(BF16), ldy, (((0,), (0,)), ((), ())),
                                     preferred_element_type=F32)
        zx = lax.dot_general(gb, r_ref[0], (((1,), (1,)), ((), ())), preferred_element_type=F32)
        zx = jnp.where(mask, zx, 0.0)
        dx_ref[...] = (jnp.sum(zx.reshape(T, SUBLANES, S5_PART), axis=1) + d_ref[...] * dy
                       + res_scale * res_ref[...])
        dd_ref[...] += jnp.sum(dy * xv, axis=0, keepdims=True)

    xs = pl.BlockSpec((T, S5_PART), lambda q, i: (nb - 1 - i, q))
    ms = pl.BlockSpec((1, S5_PART, S5_PART), lambda q, i: (q, 0, 0))
    ls = pl.BlockSpec((1, SUBLANES, S5_PART), lambda q, i: (q, 0, 0))
    ds_ = pl.BlockSpec((1, S5_PART), lambda q, i: (0, q))
    outs, extra = _call(
        body, name="s5_scan_bwd",
        out_shape=(jax.ShapeDtypeStruct((S, D), F32), jax.ShapeDtypeStruct((NQ, S5_PART, S5_PART), F32),
                   jax.ShapeDtypeStruct((NQ, S5_PART, S5_PART), F32),
                   jax.ShapeDtypeStruct((NQ, SUBLANES, S5_PART), F32), jax.ShapeDtypeStruct((1, D), F32)),
        grid=(NQ, nb),
        in_specs=[xs, xs, xs, xs, pl.BlockSpec((1, rows, S5_PART), lambda q, i: (q, nb - 1 - i, 0)),
                  pl.BlockSpec((1, SUBLANES, S5_PART), lambda q, i: (q, jnp.maximum((nb - 1 - i) * T - 1, 0), 0)),
                  ms, ms, ls, ds_],
        out_specs=(xs, ms, ms, ls, ds_),
        scratch_shapes=[pltpu.VMEM((rows, S5_PART), F32), pltpu.VMEM((rows, S5_PART), F32),
                        pltpu.VMEM((rows + SUBLANES, S5_PART), F32), pltpu.VMEM((SUBLANES, S5_PART), F32)],
        args=(x, y, dyg, res, hs, hs, rmat, cmat, lam, dskip), semantics=("parallel", "arbitrary"), ex=ex)
    return (*outs, extra)


def _s5_discretise(a_re, a_im, log_dt, b_re, b_im):
    dt = jnp.exp(log_dt)[:, None]
    mag = jnp.exp(a_re * dt)
    ang = a_im * dt
    lb_re = mag * jnp.cos(ang)
    lb_im = mag * jnp.sin(ang)
    den = a_re * a_re + a_im * a_im
    nr = lb_re - 1.0
    ni = lb_im
    z_re = (nr * a_re + ni * a_im) / den
    z_im = (ni * a_re - nr * a_im) / den
    bb_re = z_re[..., None] * b_re - z_im[..., None] * b_im
    bb_im = z_re[..., None] * b_im + z_im[..., None] * b_re
    return lb_re, lb_im, bb_re, bb_im


def _s5_expand(w):
    G = w.shape[0]
    NQ = G // 16
    base = w.reshape(NQ, S5_PART, S5_STATE)
    half = (jnp.arange(S5_PART) // S5_GROUP) % 2
    sel = (half[:, None] == jnp.arange(2)[None, :]).astype(w.dtype)
    out = base[:, :, None, :] * sel[None, :, :, None]
    return out.reshape(NQ, S5_PART, 2 * S5_STATE)


def _s5_extract(m):
    NQ = m.shape[0]
    half = (jnp.arange(S5_PART) // S5_GROUP) % 2
    sel = (half[:, None] == jnp.arange(2)[None, :]).astype(m.dtype)
    base = jnp.sum(m.reshape(NQ, S5_PART, 2, S5_STATE) * sel[None, :, :, None], axis=2)
    return base.reshape(NQ * 16, S5_GROUP, S5_STATE)


def _s5_slab(v):
    return v.reshape(v.shape[0] // 16, SUBLANES, LANES)


def _adamw(w, g, m, v, name):
    R, C = w.shape
    tr = _tile(R, 256, SUBLANES)
    c1 = 1.0 / (1.0 - ADAM_B1 ** ADAM_STEP)
    c2 = 1.0 / (1.0 - ADAM_B2 ** ADAM_STEP)

    def body(w_ref, g_ref, m_ref, v_ref, d_ref, nm_ref, nv_ref):
        gv = g_ref[...]
        nm = ADAM_B1 * m_ref[...] + (1.0 - ADAM_B1) * gv
        nv = ADAM_B2 * v_ref[...] + (1.0 - ADAM_B2) * (gv * gv)
        nm_ref[...] = nm
        nv_ref[...] = nv
        d_ref[...] = -ADAM_LR * ((nm * c1) / (jnp.sqrt(nv * c2) + ADAM_EPS) + ADAM_WD * w_ref[...])

    blk = pl.BlockSpec((tr, C), lambda i: (i, 0))
    sh = jax.ShapeDtypeStruct((R, C), F32)
    return pl.pallas_call(
        body, name=name, out_shape=(sh, sh, sh), grid=(R // tr,), in_specs=[blk] * 4, out_specs=(blk,) * 3,
        compiler_params=_params(("parallel",)),
    )(w, g, m, v)


def _adamw_recv(parts, w, m, v, name):
    L, R, C = w.shape
    tr = _tile(R, 256, SUBLANES)
    c1 = 1.0 / (1.0 - ADAM_B1 ** ADAM_STEP)
    c2 = 1.0 / (1.0 - ADAM_B2 ** ADAM_STEP)

    def body(*refs):
        p_refs = refs[:L]
        w_ref, m_ref, v_ref, g_ref, d_ref, nm_ref, nv_ref = refs[L:]
        li = pl.program_id(0)
        for l in range(L):
            @pl.when(li == l)
            def _(p_ref=p_refs[l]):
                gv = p_ref[0].astype(F32)
                for k in range(1, N_DEV):
                    gv = gv + p_ref[k].astype(F32)
                g_ref[...] = gv
                nm = ADAM_B1 * m_ref[...] + (1.0 - ADAM_B1) * gv
                nv = ADAM_B2 * v_ref[...] + (1.0 - ADAM_B2) * (gv * gv)
                nm_ref[...] = nm
                nv_ref[...] = nv
                d_ref[...] = -ADAM_LR * ((nm * c1) / (jnp.sqrt(nv * c2) + ADAM_EPS) + ADAM_WD * w_ref[...])

    p_specs = [pl.BlockSpec((N_DEV, tr, C), lambda li, i, l=l: (0, jnp.where(li == l, i, 0), 0)) for l in range(L)]
    blk = pl.BlockSpec((None, tr, C), lambda li, i: (li, i, 0))
    sh = jax.ShapeDtypeStruct((L, R, C), F32)
    return pl.pallas_call(
        body, name=name, out_shape=(sh, sh, sh, sh), grid=(L, R // tr),
        in_specs=p_specs + [blk, blk, blk], out_specs=(blk,) * 4,
        compiler_params=_params(("parallel", "parallel")),
    )(*parts, w, m, v)


def _sum8(parts, name):
    _, R, C = parts.shape
    tr = _tile(R, 256, SUBLANES)

    def body(p_ref, o_ref):
        acc = p_ref[0].astype(F32)
        for k in range(1, N_DEV):
            acc = acc + p_ref[k].astype(F32)
        o_ref[...] = acc

    return pl.pallas_call(
        body, name=name, out_shape=jax.ShapeDtypeStruct((R, C), F32), grid=(R // tr,),
        in_specs=[pl.BlockSpec((N_DEV, tr, C), lambda i: (0, i, 0))],
        out_specs=pl.BlockSpec((tr, C), lambda i: (i, 0)), compiler_params=_params(("parallel",)),
    )(parts)


def _peers():
    x, y, c = lax.axis_index("x"), lax.axis_index("y"), lax.axis_index("c")
    me = 4 * x + 2 * y + c
    out = []
    for k in range(1, N_DEV):
        kx, ky, kc = (k >> 2) & 1, (k >> 1) & 1, k & 1
        px, py, pc = x ^ kx, y ^ ky, c ^ kc
        out.append(((px, py, pc), 4 * px + 2 * py + pc))
    return me, out


class Exchange:
    def __init__(self, xs, scatter):
        self.xs = list(xs)
        self.scatter = list(scatter)
        self.n = len(self.xs)

    def out_shapes(self):
        return tuple(jax.ShapeDtypeStruct(x.shape if sc else (N_DEV,) + x.shape, x.dtype)
                     for x, sc in zip(self.xs, self.scatter))

    def sems(self):
        np_ = N_DEV - 1
        return [pltpu.SemaphoreType.DMA((self.n * np_,)), pltpu.SemaphoreType.DMA((self.n * np_,)),
                pltpu.SemaphoreType.DMA((self.n,))]

    def _copies(self, x_refs, o_refs, send_sems, recv_sems, local_sems, landing):
        np_ = N_DEV - 1
        me, peers = _peers()
        owns, remote = [], []
        for i, (x_ref, o_ref, sc) in enumerate(zip(x_refs, o_refs, self.scatter)):
            owns.append(pltpu.make_async_copy(x_ref.at[me] if sc else x_ref, o_ref.at[me], local_sems.at[i]))
            for k, (dev, idx) in enumerate(peers):
                remote.append(pltpu.make_async_remote_copy(
                    src_ref=x_ref.at[idx] if sc else x_ref, dst_ref=o_ref.at[idx if landing else me],
                    send_sem=send_sems.at[i * np_ + k], recv_sem=recv_sems.at[i * np_ + k],
                    device_id=dev, device_id_type=MESH))
        return owns, remote

    def start(self, *refs):
        owns, sends = self._copies(*refs, landing=False)
        for cp in owns + sends:
            cp.start()

    def wait(self, *refs):
        owns, lands = self._copies(*refs, landing=True)
        for cp in lands:
            cp.wait_recv()
        for cp in lands:
            cp.wait_send()
        for cp in owns:
            cp.wait()


def _exchange(ex, name):
    n = ex.n

    def body(*refs):
        x_refs, o_refs, sems = refs[:n], refs[n:2 * n], refs[2 * n:]
        ex.start(x_refs, o_refs, *sems)
        ex.wait(x_refs, o_refs, *sems)

    hbm = pl.BlockSpec(memory_space=pltpu.HBM)
    return pl.pallas_call(body, name=name, out_shape=ex.out_shapes(), in_specs=[hbm] * n, out_specs=(hbm,) * n,
                          scratch_shapes=ex.sems())(*ex.xs)


def _call(body, *, name, out_shape, grid, in_specs, out_specs, scratch_shapes, args, semantics, ex=None):
    if ex is None:
        return pl.pallas_call(body, name=name, out_shape=out_shape, grid=grid, in_specs=in_specs, out_specs=out_specs,
                              scratch_shapes=scratch_shapes, compiler_params=_params(semantics))(*args), ()
    n, ni, no, ns = ex.n, len(args), len(out_shape), len(scratch_shapes)

    def wrapped(*refs):
        ins, cx = refs[:ni], refs[ni:ni + n]
        outs, co = refs[ni + n:ni + n + no], refs[ni + n + no:ni + 2 * n + no]
        scratch, sems = refs[ni + 2 * n + no:ni + 2 * n + no + ns], refs[ni + 2 * n + no + ns:]
        ids = [pl.program_id(a) for a in range(len(grid))]
        first = functools.reduce(jnp.logical_and, [i == 0 for i in ids])
        last = functools.reduce(jnp.logical_and, [i == g - 1 for i, g in zip(ids, grid)])

        @pl.when(first)
        def _():
            ex.start(cx, co, *sems)

        body(*ins, *outs, *scratch)

        @pl.when(last)
        def _():
            ex.wait(cx, co, *sems)

    hbm = pl.BlockSpec(memory_space=pltpu.HBM)
    res = pl.pallas_call(
        wrapped, name=name, out_shape=tuple(out_shape) + ex.out_shapes(), grid=grid,
        in_specs=list(in_specs) + [hbm] * n, out_specs=tuple(out_specs) + (hbm,) * n,
        scratch_shapes=list(scratch_shapes) + ex.sems(),
        compiler_params=_params(("arbitrary",) * len(grid)))(*args, *ex.xs)
    return res[:no], res[no:]


def _pack_flat(arrs):
    cat = jnp.concatenate([a.reshape(-1) for a in arrs])
    per = PACK_COLS * SUBLANES
    tot = -(-cat.shape[0] // per) * per
    return jnp.pad(cat, (0, tot - cat.shape[0])).reshape(tot // PACK_COLS, PACK_COLS)


def _unpack_flat(packed, shapes):
    flat = packed.reshape(-1)
    out, o = [], 0
    for s in shapes:
        n = math.prod(s)
        out.append(flat[o:o + n].reshape(s))
        o += n
    return out


def _ffn_fwd(x, wg, wo4, g, b, alpha, tag):
    h8 = _ffn_h(x, wg, tag + "_h")
    a4 = _swiglu_fwd(h8, tag + "_act")
    y = _ffn_y(a4, wo4, tag + "_y")
    out, xh, rstd = _res_ln_fwd(x, y, g, b, alpha, 0.5, tag + "_ln")
    return out, (x, h8, a4, xh, rstd)


def _ffn_bwd(dout, saved, wg, wo4, g, alpha, tag):
    x, h8, a4, xh, rstd = saved
    dz, dg, db = _ln_bwd(dout, xh, rstd, g, tag + "_ln_bwd")
    dw_out = _ffn_dwout(a4, dz, 0.5, tag + "_dwout")
    da4 = _ffn_da(dz, wo4, 0.5, tag + "_da")
    dh8 = _swiglu_bwd(h8, da4, tag + "_act_bwd")
    dw_in = _ffn_dwin(x, dh8, tag + "_dwin")
    dx = _ffn_dx(dh8, wg, dz, alpha, tag + "_dx")
    return dx, dw_in, dw_out.reshape(N_DEV, -1, dw_out.shape[-1]), dg, db


def _heads(t, H):
    S = t.shape[0]
    return t.reshape(S, H, HEAD_DIM).transpose(1, 0, 2)


def _fox_fwd(x, w_in_pad, b_f_pad, w_o, g, b, alpha, tag, ex=None):
    S, D = x.shape
    H = D // HEAD_DIM
    proj = _mm(x, w_in_pad, name=tag + "_proj")
    q = _heads((proj[:, :D] * (1.0 / math.sqrt(HEAD_DIM))).astype(BF16), H)
    k, v = (_heads(proj[:, i * D:(i + 1) * D].astype(BF16), H) for i in (1, 2))
    fl = proj[:, 3 * D:]
    cum = _fox_gate_fwd(fl, b_f_pad)
    ch = cum[:, :H].T
    t = _tile(S, 512)
    cq, ck = ch[:, :, None], ch.reshape(H, S // t, 1, t)
    o, lse, extra = _flash_fwd(q, k, v, cq, ck, ex)
    o2 = o.transpose(1, 0, 2).reshape(S, D)
    m = _mm(o2, w_o, name=tag + "_out")
    out, xh, rstd = _res_ln_fwd(x, m, g, b, alpha, 1.0, tag + "_ln")
    return out, (x, q, k, v, jnp.concatenate([cq, lse], axis=2), ck, o, o2, fl, xh, rstd), extra


def _fox_bwd(dout, saved, w_in_pad, b_f_pad, w_o, g, alpha, tag, ex=None):
    x, q, k, v, rows, ck, o, o2, fl, xh, rstd = saved
    S, D = x.shape
    H = D // HEAD_DIM
    dz, dg, db = _ln_bwd(dout, xh, rstd, g, tag + "_ln_bwd")
    dw_o = _mm(o2, dz, ta=True, name=tag + "_dwo")
    do = _heads(_mm(dz, w_o, tb=True, name=tag + "_do"), H)
    dq, dk, dv, dcq, dck, extra = _flash_bwd(q, k, v, rows, ck, o, do, ex)
    dcum = jnp.pad((dcq[:, :, 0] + dck.reshape(H, S)).T, ((0, 0), (0, LANES - H)))
    dfl, dbf = _fox_gate_bwd(dcum, fl, b_f_pad)
    flat = lambda t: t.transpose(1, 0, 2).reshape(S, D).astype(BF16)
    dproj = jnp.concatenate([flat(dq), flat(dk), flat(dv), dfl.astype(BF16)], axis=1)
    dw_in = _mm(x, dproj, ta=True, name=tag + "_dwin")
    dx = _mm(dproj, w_in_pad, tb=True, add=dz, add_scale=alpha, name=tag + "_dx")
    shards = {"fox_w_in": _split(dw_in[None, :, :3 * D + H], True)[:, 0].astype(BF16),
              "fox_w_o": _split(dw_o[None], False)[:, 0].astype(BF16)}
    return dx, shards, {"fox_b_f": dbf[0, :H], "lnm_g": dg, "lnm_b": db}, extra


def _s5_mats(p):
    lb_re, lb_im, bb_re, bb_im = _s5_discretise(p["a_re"], p["a_im"], p["log_dt"], p["b_re"], p["b_im"])
    rmat = jnp.concatenate([_s5_expand(bb_re.transpose(0, 2, 1)), _s5_expand(bb_im.transpose(0, 2, 1))], axis=2)
    cmat = jnp.concatenate([_s5_expand(p["c_re"]).transpose(0, 2, 1), -_s5_expand(p["c_im"]).transpose(0, 2, 1)],
                           axis=1)
    lam = jnp.concatenate([_s5_slab(lb_re), _s5_slab(lb_im)], axis=2)
    return rmat.astype(BF16), cmat.astype(BF16), lam


def _s5_block_fwd(x, p, w_out, g, b, alpha, tag, ex=None):
    S, D = x.shape
    rmat, cmat, lam = _s5_mats(p)
    dskip = p["d"].reshape(1, D)
    y, yg, hs, extra = _s5_fwd(x, rmat, cmat, lam, dskip, ex)
    vg = _mm(yg, w_out, name=tag + "_vg")
    m = _glu_fwd(vg, tag + "_glu")
    out, xh, rstd = _res_ln_fwd(x, m, g, b, alpha, 1.0, tag + "_ln")
    return out, (x, y, yg, hs, vg, rmat, cmat, lam, dskip, xh, rstd), extra


def _s5_block_bwd(dout, saved, p, w_out, g, alpha, tag, ex=None):
    x, y, yg, hs, vg, rmat, cmat, lam, dskip, xh, rstd = saved
    S, D = x.shape
    G = D // S5_GROUP
    dz, dg, db = _ln_bwd(dout, xh, rstd, g, tag + "_ln_bwd")
    dvg = _glu_bwd(vg, dz, tag + "_glu_bwd")
    dw_out = _mm(yg, dvg, ta=True, name=tag + "_dwout")
    dyg = _mm(dvg, w_out, tb=True, name=tag + "_dyg")
    dx, dr, dc, dlam, dd, extra = _s5_bwd(x, y, dyg, hs, rmat, cmat, lam, dskip, dz, alpha, ex)
    dbb_re = _s5_extract(dr[:, :, :LANES]).transpose(0, 2, 1)
    dbb_im = _s5_extract(dr[:, :, LANES:]).transpose(0, 2, 1)
    dc_re = _s5_extract(dc[:, :LANES, :].transpose(0, 2, 1))
    dc_im = -_s5_extract(dc[:, LANES:, :].transpose(0, 2, 1))
    dlb_re = dlam[:, :, :LANES].reshape(G, S5_STATE)
    dlb_im = dlam[:, :, LANES:].reshape(G, S5_STATE)
    _, vjp = jax.vjp(_s5_discretise, p["a_re"], p["a_im"], p["log_dt"], p["b_re"], p["b_im"])
    da_re, da_im, dlog_dt, db_re, db_im = vjp((dlb_re, dlb_im, dbb_re, dbb_im))
    small = dict(s5_a_re=da_re, s5_a_im=da_im, s5_log_dt=dlog_dt, s5_b_re=db_re, s5_b_im=db_im, s5_c_re=dc_re,
                 s5_c_im=dc_im, s5_d=dd.reshape(G, S5_GROUP), lnm_g=dg, lnm_b=db)
    return dx, {"s5_w_out": _split(dw_out[None], True)[:, 0].astype(BF16)}, small, extra


FFN_NAMES = ("ffn1_w_in", "ffn1_w_out", "ffn2_w_in", "ffn2_w_out")
BIG = FFN_NAMES + ("fox_w_in", "fox_w_o", "s5_w_out")
BIG_SPLIT_COLS = {"ffn1_w_in": True, "ffn1_w_out": False, "ffn2_w_in": True, "ffn2_w_out": False,
                  "fox_w_in": True, "fox_w_o": False, "s5_w_out": True}
SMALL = ("ln1_g", "ln1_b", "lnm_g", "lnm_b", "ln2_g", "ln2_b", "fox_b_f", "s5_a_re", "s5_a_im", "s5_log_dt",
         "s5_b_re", "s5_b_im", "s5_c_re", "s5_c_im", "s5_d")
WEIGHTS = ("ffn1_w_in", "ffn1_w_out", "ln1_g", "ln1_b", "lnm_g", "lnm_b", "ffn2_w_in", "ffn2_w_out", "ln2_g", "ln2_b",
           "fox_w_in", "fox_b_f", "fox_w_o", "s5_a_re", "s5_a_im", "s5_log_dt", "s5_b_re", "s5_b_im", "s5_c_re",
           "s5_c_im", "s5_d", "s5_w_out")


def _join(gathered, split_cols):
    n, L, r, c = gathered.shape
    if split_cols:
        return gathered.transpose(1, 2, 0, 3).reshape(L, r, n * c)
    return gathered.transpose(1, 0, 2, 3).reshape(L, n * r, c)


def _split(full, split_cols):
    L, R, C = full.shape
    if split_cols:
        return full.reshape(L, R, N_DEV, C // N_DEV).transpose(2, 0, 1, 3)
    return full.reshape(L, N_DEV, R // N_DEV, C).transpose(1, 0, 2, 3)


def _layer_names(i):
    return FFN_NAMES + (("fox_w_in", "fox_w_o") if i % 2 == 0 else ("s5_w_out",))


def _layer_index(name, i):
    return i if name in FFN_NAMES else i // 2


def _prepare(gathered):
    out = {}
    for k, g8 in gathered.items():
        if k in ("ffn1_w_in", "ffn2_w_in"):
            out[k] = g8
        elif k in FFN_NAMES:
            n, r, c = g8.shape
            out[k] = g8.reshape(n // 2, 2 * r, c)
        else:
            out[k] = _join(g8[:, None], BIG_SPLIT_COLS[k])[0]
    return out


def _local_step(x, target, small, shard_of=None, pregathered=None):
    S, D = x.shape
    H = D // HEAD_DIM
    depth = small["ln1_g"].shape[0]
    alpha = (2.0 * depth) ** 0.25
    local = pregathered is not None
    bf_pad = jnp.pad(small["fox_b_f"], ((0, 0), (0, LANES - H)))

    def s5_params(j):
        return {k: small["s5_" + k][j] for k in ("a_re", "a_im", "log_dt", "b_re", "b_im", "c_re", "c_im", "d")}

    def gather_of(i):
        names = _layer_names(i)
        return names, Exchange([shard_of(i)[k] for k in names], [False] * len(names))

    if local:
        Wl = _prepare(pregathered[0])
    else:
        names, ex = gather_of(0)
        Wl = _prepare(dict(zip(names, _exchange(ex, "gather_layer0"))))
    saved, Ws = [], []
    h = x
    for i in range(depth):
        j = i // 2
        names, ex = gather_of(i + 1) if (not local and i + 1 < depth) else ((), None)
        h, s1 = _ffn_fwd(h, Wl["ffn1_w_in"], Wl["ffn1_w_out"], small["ln1_g"][i], small["ln1_b"][i], alpha,
                         f"l{i}_ffn1")
        if i % 2 == 0:
            Wl["fox_w_in"] = jnp.pad(Wl["fox_w_in"], ((0, 0), (0, LANES - H)))
            h, s2, extra = _fox_fwd(h, Wl["fox_w_in"], bf_pad[j:j + 1], Wl["fox_w_o"], small["lnm_g"][i],
                                    small["lnm_b"][i], alpha, f"l{i}_fox", ex)
        else:
            h, s2, extra = _s5_block_fwd(h, s5_params(j), Wl["s5_w_out"], small["lnm_g"][i], small["lnm_b"][i],
                                         alpha, f"l{i}_s5", ex)
        h, s3 = _ffn_fwd(h, Wl["ffn2_w_in"], Wl["ffn2_w_out"], small["ln2_g"][i], small["ln2_b"][i], alpha,
                         f"l{i}_ffn2")
        saved.append((s1, s2, s3))
        Ws.append(Wl)
        if i + 1 < depth:
            Wl = _prepare(pregathered[i + 1] if local else dict(zip(names, extra)))

    dh, loss_part = _loss_fwd_bwd(h, target)

    arrived = {}
    pending = None
    gs = {k: [None] * small[k].shape[0] for k in SMALL}
    for i in reversed(range(depth)):
        j = i // 2
        s1, s2, s3 = saved[i]
        Wl = Ws[i]
        mine = {}
        dh, mine["ffn2_w_in"], mine["ffn2_w_out"], gs["ln2_g"][i], gs["ln2_b"][i] = _ffn_bwd(
            dh, s3, Wl["ffn2_w_in"], Wl["ffn2_w_out"], small["ln2_g"][i], alpha, f"l{i}_ffn2")
        ex = None
        if pending is not None and not local:
            pi, pshards = pending
            ex = Exchange(list(pshards.values()), [True] * len(pshards))
        if i % 2 == 0:
            dh, mix, sg, extra = _fox_bwd(dh, s2, Wl["fox_w_in"], bf_pad[j:j + 1], Wl["fox_w_o"], small["lnm_g"][i],
                                          alpha, f"l{i}_fox", ex)
        else:
            dh, mix, sg, extra = _s5_block_bwd(dh, s2, s5_params(j), Wl["s5_w_out"], small["lnm_g"][i], alpha,
                                               f"l{i}_s5", ex)
        if pending is not None:
            pi, pshards = pending
            for k, got in zip(pshards, pshards.values() if local else extra):
                arrived[(k, _layer_index(k, pi))] = got
        mine.update(mix)
        for k, val in sg.items():
            gs[k][i if k in ("lnm_g", "lnm_b") else j] = val
        dh, mine["ffn1_w_in"], mine["ffn1_w_out"], gs["ln1_g"][i], gs["ln1_b"][i] = _ffn_bwd(
            dh, s1, Wl["ffn1_w_in"], Wl["ffn1_w_out"], small["ln1_g"][i], alpha, f"l{i}_ffn1")
        pending = (i, mine)
    gs = {k: jnp.stack(v) for k, v in gs.items()}
    return loss_part, dh, arrived, pending[1], gs


def kernel(x, ffn1_w_in, ffn1_w_out, ln1_g, ln1_b, lnm_g, lnm_b, ffn2_w_in, ffn2_w_out, ln2_g, ln2_b, fox_w_in, fox_b_f, fox_w_o, s5_a_re, s5_a_im, s5_log_dt, s5_b_re, s5_b_im, s5_c_re, s5_c_im, s5_d, s5_w_out, loss_target, m_ffn1_w_in, m_ffn1_w_out, m_ln1_g, m_ln1_b, m_lnm_g, m_lnm_b, m_ffn2_w_in, m_ffn2_w_out, m_ln2_g, m_ln2_b, m_fox_w_in, m_fox_b_f, m_fox_w_o, m_s5_a_re, m_s5_a_im, m_s5_log_dt, m_s5_b_re, m_s5_b_im, m_s5_c_re, m_s5_c_im, m_s5_d, m_s5_w_out, v_ffn1_w_in, v_ffn1_w_out, v_ln1_g, v_ln1_b, v_lnm_g, v_lnm_b, v_ffn2_w_in, v_ffn2_w_out, v_ln2_g, v_ln2_b, v_fox_w_in, v_fox_b_f, v_fox_w_o, v_s5_a_re, v_s5_a_im, v_s5_log_dt, v_s5_b_re, v_s5_b_im, v_s5_c_re, v_s5_c_im, v_s5_d, v_s5_w_out):
    args = dict(locals())
    w = {k: args[k] for k in WEIGHTS}
    m = {k: args["m_" + k] for k in WEIGHTS}
    v = {k: args["v_" + k] for k in WEIGHTS}
    small = {k: w[k] for k in SMALL}

    wb = {k: w[k].astype(BF16) for k in BIG}
    shard_of = lambda i: {k: wb[k][_layer_index(k, i)] for k in _layer_names(i)}
    loss_part, dx, arrived, last, gs = _local_step(x[0], loss_target[0], small, shard_of)
    loss = lax.psum(loss_part, ("x", "y", "c"))

    small_shapes = [w[k].shape for k in SMALL]
    ex = Exchange(list(last.values()) + [_pack_flat([gs[k] for k in SMALL])], [True] * len(last) + [False])
    *got, g_small_all = _exchange(ex, "scatter_layer0_gather_small")
    arrived.update({(k, 0): g for k, g in zip(last, got)})
    g_small_flat = _sum8(g_small_all, "sum_small_grads")

    grads, delta, new_m, new_v = {}, {}, {}, {}
    for k in BIG:
        parts = [arrived[(k, l)] for l in range(w[k].shape[0])]
        grads[k], delta[k], new_m[k], new_v[k] = _adamw_recv(parts, w[k], m[k], v[k], "adamw_" + k)
    pk = lambda d: _pack_flat([d[k] for k in SMALL])
    d_, m_, v_ = _adamw(pk(w), g_small_flat, pk(m), pk(v), "adamw_small")
    for dst, flat in ((grads, g_small_flat), (delta, d_), (new_m, m_), (new_v, v_)):
        dst.update(zip(SMALL, _unpack_flat(flat, small_shapes)))

    return (loss, dx[None], *[grads[k] for k in WEIGHTS], *[delta[k] for k in WEIGHTS],
            *[new_m[k] for k in WEIGHTS], *[new_v[k] for k in WEIGHTS])
```

```python
import functools
import math

import jax
import jax.numpy as jnp
from jax import lax
from jax.experimental import pallas as pl
from jax.experimental.pallas import tpu as pltpu

F32 = jnp.float32
BF16 = jnp.bfloat16

N_DEV = 8
HEAD_DIM = 64
S5_GROUP = 16
S5_STATE = 64
LANES = 128
SUBLANES = 8
S5_PART = 256
LN_EPS = 1e-5
NEG_INF = -1e30
ADAM_LR, ADAM_B1, ADAM_B2, ADAM_EPS, ADAM_WD, ADAM_STEP = 0.001, 0.9, 0.999, 1e-08, 0.01, 10
VMEM_LIMIT = 48 * 1024 * 1024
PACK_COLS = 1024
PACK_ROW_ALIGN = 512

MESH = pl.DeviceIdType.MESH


def _tile(dim, pref, align=LANES):
    if dim <= pref:
        return dim
    t = (pref // align) * align
    while t >= align:
        if dim % t == 0:
            return t
        t -= align
    return dim


def _params(sem):
    return pltpu.CompilerParams(dimension_semantics=sem, vmem_limit_bytes=VMEM_LIMIT)


def _mm(a, b, *, ta=False, tb=False, out_dtype=F32, scale=None, add=None, add_scale=1.0,
        tm=512, tn=1408, tk=1408, name="mm"):
    if ta:
        K, M = a.shape
    else:
        M, K = a.shape
    if tb:
        N, K2 = b.shape
    else:
        K2, N = b.shape
    assert K == K2, (a.shape, b.shape, ta, tb)
    tm, tn, tk = _tile(M, tm), _tile(N, tn), _tile(K, tk)
    a_spec = pl.BlockSpec((tk, tm), lambda i, j, k: (k, i)) if ta else pl.BlockSpec((tm, tk), lambda i, j, k: (i, k))
    b_spec = pl.BlockSpec((tn, tk), lambda i, j, k: (j, k)) if tb else pl.BlockSpec((tk, tn), lambda i, j, k: (k, j))
    o_spec = pl.BlockSpec((tm, tn), lambda i, j, k: (i, j))
    return _mm_core(name, a, b, a_spec, b_spec, o_spec, jax.ShapeDtypeStruct((M, N), out_dtype),
                    (M // tm, N // tn, K // tk), (tm, tn), ta, tb, scale, add, add_scale)


def _mm_core(name, a, b, a_spec, b_spec, o_spec, out_shape, grid, acc_shape, ta, tb, scale=None, add=None,
             add_scale=1.0):
    nk = grid[2]

    def body(*refs):
        if add is None:
            a_ref, b_ref, o_ref, acc = refs
            add_ref = None
        else:
            a_ref, b_ref, add_ref, o_ref, acc = refs
        k = pl.program_id(2)

        @pl.when(k == 0)
        def _():
            acc[...] = jnp.zeros_like(acc)

        dims = (((0 if ta else 1,), (1 if tb else 0,)), ((), ()))
        acc[...] += lax.dot_general(a_ref[...].astype(BF16), b_ref[...].astype(BF16), dims,
                                    preferred_element_type=F32)

        @pl.when(k == nk - 1)
        def _():
            r = acc[...]
            if scale is not None:
                r = r * scale
            if add_ref is not None:
                r = r + add_scale * add_ref[...]
            o_ref[...] = r.astype(out_shape.dtype)

    in_specs = [a_spec, b_spec]
    args = [a, b]
    if add is not None:
        in_specs.append(o_spec)
        args.append(add)
    return pl.pallas_call(
        body, name=name, out_shape=out_shape, grid=grid, in_specs=in_specs, out_specs=o_spec,
        scratch_shapes=[pltpu.VMEM(acc_shape, F32)],
        compiler_params=_params(("parallel", "parallel", "arbitrary")),
    )(*args)


def _ffn_h(x, wg, name):
    S, D = x.shape
    n, _, c = wg.shape
    tm = _tile(S, 1024, SUBLANES)
    return _mm_core(name, x, wg, pl.BlockSpec((tm, D), lambda i, j, k: (i, 0)),
                    pl.BlockSpec((None, D, c), lambda i, j, k: (j, 0, 0)),
                    pl.BlockSpec((None, tm, c), lambda i, j, k: (j, i, 0)),
                    jax.ShapeDtypeStruct((n, S, c), BF16), (S // tm, n, 1), (tm, c), False, False)


def _ffn_y(a4, wo4, name):
    nb, S, c = a4.shape
    D = wo4.shape[-1]
    tm, tn = _tile(S, 1024, SUBLANES), _tile(D, 1024)
    return _mm_core(name, a4, wo4, pl.BlockSpec((None, tm, c), lambda i, j, k: (k, i, 0)),
                    pl.BlockSpec((None, c, tn), lambda i, j, k: (k, 0, j)),
                    pl.BlockSpec((tm, tn), lambda i, j, k: (i, j)),
                    jax.ShapeDtypeStruct((S, D), F32), (S // tm, D // tn, nb), (tm, tn), False, False)


def _ffn_dwout(a4, dz, scale, name):
    nb, S, c = a4.shape
    D = dz.shape[1]
    tk, tn = _tile(S, 1024, SUBLANES), _tile(D, 1024)
    return _mm_core(name, a4, dz, pl.BlockSpec((None, tk, c), lambda i, j, k: (i, k, 0)),
                    pl.BlockSpec((tk, tn), lambda i, j, k: (k, j)),
                    pl.BlockSpec((None, c, tn), lambda i, j, k: (i, 0, j)),
                    jax.ShapeDtypeStruct((nb, c, D), BF16), (nb, D // tn, S // tk), (c, tn), True, False, scale)


def _ffn_da(dz, wo4, scale, name):
    S, D = dz.shape
    nb, c = wo4.shape[0], wo4.shape[1]
    tm = _tile(S, 1024, SUBLANES)
    return _mm_core(name, dz, wo4, pl.BlockSpec((tm, D), lambda i, j, k: (i, 0)),
                    pl.BlockSpec((None, c, D), lambda i, j, k: (j, 0, 0)),
                    pl.BlockSpec((None, tm, c), lambda i, j, k: (j, i, 0)),
                    jax.ShapeDtypeStruct((nb, S, c), BF16), (S // tm, nb, 1), (tm, c), False, True, scale)


def _ffn_dwin(x, dh8, name):
    S, D = x.shape
    n, _, c = dh8.shape
    tk, tm = _tile(S, 1024, SUBLANES), _tile(D, 1024)
    return _mm_core(name, x, dh8, pl.BlockSpec((tk, tm), lambda i, j, k: (k, j)),
                    pl.BlockSpec((None, tk, c), lambda i, j, k: (i, k, 0)),
                    pl.BlockSpec((None, tm, c), lambda i, j, k: (i, j, 0)),
                    jax.ShapeDtypeStruct((n, D, c), BF16), (n, D // tm, S // tk), (tm, c), True, False)


def _ffn_dx(dh8, wg, dz, alpha, name):
    n, S, c = dh8.shape
    D = wg.shape[1]
    tm, tn = _tile(S, 1024, SUBLANES), _tile(D, 1024)
    return _mm_core(name, dh8, wg, pl.BlockSpec((None, tm, c), lambda i, j, k: (k, i, 0)),
                    pl.BlockSpec((None, tn, c), lambda i, j, k: (k, j, 0)),
                    pl.BlockSpec((tm, tn), lambda i, j, k: (i, j)),
                    jax.ShapeDtypeStruct((S, D), F32), (S // tm, D // tn, n), (tm, tn), False, True,
                    None, dz, alpha)


def _res_ln_fwd(x, y, g, b, alpha, s, name):
    S, D = x.shape
    tm = _tile(S, 256, SUBLANES)

    def body(x_ref, y_ref, g_ref, b_ref, o_ref, xh_ref, r_ref):
        z = alpha * x_ref[...] + s * y_ref[...]
        mu = jnp.mean(z, axis=-1, keepdims=True)
        zc = z - mu
        var = jnp.mean(zc * zc, axis=-1, keepdims=True)
        rstd = lax.rsqrt(var + LN_EPS)
        xh = zc * rstd
        xh_ref[...] = xh
        r_ref[...] = rstd
        o_ref[...] = xh * g_ref[...] + b_ref[...]

    row = pl.BlockSpec((tm, D), lambda i: (i, 0))
    vec = pl.BlockSpec((1, D), lambda i: (0, 0))
    return pl.pallas_call(
        body, name=name,
        out_shape=(jax.ShapeDtypeStruct((S, D), F32), jax.ShapeDtypeStruct((S, D), F32),
                   jax.ShapeDtypeStruct((S, 1), F32)),
        grid=(S // tm,), in_specs=[row, row, vec, vec],
        out_specs=(row, row, pl.BlockSpec((tm, 1), lambda i: (i, 0))),
        compiler_params=_params(("parallel",)),
    )(x, y, g.reshape(1, D), b.reshape(1, D))


def _ln_bwd(dout, xh, rstd, g, name):
    S, D = dout.shape
    tm = _tile(S, 256, SUBLANES)

    def body(d_ref, xh_ref, r_ref, g_ref, dz_ref, dg_ref, db_ref):
        i = pl.program_id(0)

        @pl.when(i == 0)
        def _():
            dg_ref[...] = jnp.zeros_like(dg_ref)
            db_ref[...] = jnp.zeros_like(db_ref)

        d = d_ref[...]
        xhv = xh_ref[...]
        dxh = d * g_ref[...]
        m1 = jnp.mean(dxh, axis=-1, keepdims=True)
        m2 = jnp.mean(dxh * xhv, axis=-1, keepdims=True)
        dz_ref[...] = r_ref[...] * (dxh - m1 - xhv * m2)
        dg_ref[...] += jnp.sum(d * xhv, axis=0, keepdims=True)
        db_ref[...] += jnp.sum(d, axis=0, keepdims=True)

    row = pl.BlockSpec((tm, D), lambda i: (i, 0))
    vec = pl.BlockSpec((1, D), lambda i: (0, 0))
    dz, dg, db = pl.pallas_call(
        body, name=name,
        out_shape=(jax.ShapeDtypeStruct((S, D), F32), jax.ShapeDtypeStruct((1, D), F32),
                   jax.ShapeDtypeStruct((1, D), F32)),
        grid=(S // tm,), in_specs=[row, row, pl.BlockSpec((tm, 1), lambda i: (i, 0)), vec],
        out_specs=(row, vec, vec),
        compiler_params=_params(("arbitrary",)),
    )(dout, xh, rstd, g.reshape(1, D))
    return dz, dg[0], db[0]


def _sigmoid(x):
    e = jnp.exp(-jnp.abs(x))
    return jnp.where(x >= 0, 1.0 / (1.0 + e), e / (1.0 + e))


def _swiglu_fwd(h8, name):
    n, S, c = h8.shape
    nb = n // 2
    tm = _tile(S, 512, SUBLANES)

    def body(h_ref, o_ref):
        g = h_ref[0].astype(F32)
        u = h_ref[1].astype(F32)
        o_ref[...] = (g * _sigmoid(g) * u).astype(BF16)

    return pl.pallas_call(
        body, name=name, out_shape=jax.ShapeDtypeStruct((nb, S, c), BF16), grid=(nb, S // tm),
        in_specs=[pl.BlockSpec((2, None, tm, c), lambda k, i: (0, k, i, 0))],
        out_specs=pl.BlockSpec((None, tm, c), lambda k, i: (k, i, 0)),
        compiler_params=_params(("parallel", "parallel")),
    )(h8.reshape(2, nb, S, c))


def _swiglu_bwd(h8, da4, name):
    n, S, c = h8.shape
    nb = n // 2
    tm = _tile(S, 512, SUBLANES)

    def body(h_ref, da_ref, d_ref):
        g = h_ref[0].astype(F32)
        u = h_ref[1].astype(F32)
        da_v = da_ref[...].astype(F32)
        sg = _sigmoid(g)
        silu = g * sg
        d_ref[0] = (da_v * u * (sg + silu * (1.0 - sg))).astype(BF16)
        d_ref[1] = (da_v * silu).astype(BF16)

    pair = pl.BlockSpec((2, None, tm, c), lambda k, i: (0, k, i, 0))
    dh = pl.pallas_call(
        body, name=name, out_shape=jax.ShapeDtypeStruct((2, nb, S, c), BF16), grid=(nb, S // tm),
        in_specs=[pair, pl.BlockSpec((None, tm, c), lambda k, i: (k, i, 0))], out_specs=pair,
        compiler_params=_params(("parallel", "parallel")),
    )(h8.reshape(2, nb, S, c), da4)
    return dh.reshape(n, S, c)


def _glu_fwd(vg, name):
    S, D2 = vg.shape
    D = D2 // 2
    tm = _tile(S, 512, SUBLANES)

    def body(v_ref, g_ref, o_ref):
        o_ref[...] = v_ref[...] * _sigmoid(g_ref[...])

    return pl.pallas_call(
        body, name=name, out_shape=jax.ShapeDtypeStruct((S, D), F32), grid=(S // tm,),
        in_specs=[pl.BlockSpec((tm, D), lambda i: (i, 0)), pl.BlockSpec((tm, D), lambda i: (i, 1))],
        out_specs=pl.BlockSpec((tm, D), lambda i: (i, 0)),
        compiler_params=_params(("parallel",)),
    )(vg, vg)


def _glu_bwd(vg, dm, name):
    S, D2 = vg.shape
    D = D2 // 2
    tm = _tile(S, 512, SUBLANES)

    def body(v_ref, g_ref, dm_ref, dv_ref, dg_ref):
        sg = _sigmoid(g_ref[...])
        d = dm_ref[...]
        dv_ref[...] = (d * sg).astype(BF16)
        dg_ref[...] = (d * v_ref[...] * sg * (1.0 - sg)).astype(BF16)

    blk = pl.BlockSpec((tm, D), lambda i: (i, 0))
    dv, dg = pl.pallas_call(
        body, name=name,
        out_shape=(jax.ShapeDtypeStruct((S, D), BF16), jax.ShapeDtypeStruct((S, D), BF16)),
        grid=(S // tm,), in_specs=[blk, pl.BlockSpec((tm, D), lambda i: (i, 1)), blk],
        out_specs=(blk, blk), compiler_params=_params(("parallel",)),
    )(vg, vg, dm)
    return jnp.concatenate([dv, dg], axis=1)


def _loss_fwd_bwd(y, target):
    S, D = y.shape
    tm = _tile(S, 256, SUBLANES)

    def body(y_ref, t_ref, dy_ref, l_ref):
        i = pl.program_id(0)

        @pl.when(i == 0)
        def _():
            l_ref[...] = jnp.zeros_like(l_ref)

        e = y_ref[...] - t_ref[...]
        dy_ref[...] = e * (1.0 / D)
        l_ref[...] += jnp.sum(e * e, axis=0, keepdims=True) * (0.5 / D)

    row = pl.BlockSpec((tm, D), lambda i: (i, 0))
    dy, part = pl.pallas_call(
        body, name="loss", out_shape=(jax.ShapeDtypeStruct((S, D), F32), jax.ShapeDtypeStruct((1, D), F32)),
        grid=(S // tm,), in_specs=[row, row], out_specs=(row, pl.BlockSpec((1, D), lambda i: (0, 0))),
        compiler_params=_params(("arbitrary",)),
    )(y, target)
    return dy, jnp.sum(part)


def _tri(n, lower):
    r = lax.broadcasted_iota(jnp.int32, (n, n), 0)
    c = lax.broadcasted_iota(jnp.int32, (n, n), 1)
    return jnp.where((c <= r) if lower else (c >= r), 1.0, 0.0)


def _fox_gate_fwd(fl, bf):
    S, W = fl.shape
    tm = _tile(S, 256, SUBLANES)

    def body(fl_ref, b_ref, c_ref, carry):
        i = pl.program_id(0)

        @pl.when(i == 0)
        def _():
            carry[...] = jnp.zeros_like(carry)

        x = fl_ref[...] + b_ref[...]
        lf = jnp.minimum(x, 0.0) - jnp.log(1.0 + jnp.exp(-jnp.abs(x)))
        c_ref[...] = jnp.dot(_tri(tm, True), lf, precision=lax.Precision.HIGHEST,
                             preferred_element_type=F32) + carry[...]
        carry[...] += jnp.sum(lf, axis=0, keepdims=True)

    blk = pl.BlockSpec((tm, W), lambda i: (i, 0))
    return pl.pallas_call(
        body, name="fox_gate_fwd", out_shape=jax.ShapeDtypeStruct((S, W), F32), grid=(S // tm,),
        in_specs=[blk, pl.BlockSpec((1, W), lambda i: (0, 0))], out_specs=blk,
        scratch_shapes=[pltpu.VMEM((1, W), F32)], compiler_params=_params(("arbitrary",)),
    )(fl, bf)


def _fox_gate_bwd(dcum, fl, bf):
    S, W = fl.shape
    tm = _tile(S, 256, SUBLANES)
    nb = S // tm

    def body(dc_ref, fl_ref, b_ref, dfl_ref, db_ref, carry):
        i = pl.program_id(0)

        @pl.when(i == 0)
        def _():
            carry[...] = jnp.zeros_like(carry)
            db_ref[...] = jnp.zeros_like(db_ref)

        dc = dc_ref[...]
        r = jnp.dot(_tri(tm, False), dc, precision=lax.Precision.HIGHEST, preferred_element_type=F32) + carry[...]
        carry[...] += jnp.sum(dc, axis=0, keepdims=True)
        x = fl_ref[...] + b_ref[...]
        dfl = r * (1.0 - _sigmoid(x))
        dfl_ref[...] = dfl
        db_ref[...] += jnp.sum(dfl, axis=0, keepdims=True)

    blk = pl.BlockSpec((tm, W), lambda i: (nb - 1 - i, 0))
    vec = pl.BlockSpec((1, W), lambda i: (0, 0))
    return pl.pallas_call(
        body, name="fox_gate_bwd",
        out_shape=(jax.ShapeDtypeStruct((S, W), F32), jax.ShapeDtypeStruct((1, W), F32)),
        grid=(nb,), in_specs=[blk, blk, vec], out_specs=(blk, vec),
        scratch_shapes=[pltpu.VMEM((1, W), F32)], compiler_params=_params(("arbitrary",)),
    )(dcum, fl, bf)


def _causal(t):
    row = lax.broadcasted_iota(jnp.int32, (t, t), 0)
    col = lax.broadcasted_iota(jnp.int32, (t, t), 1)
    return col <= row


def _first_head(shape):
    return lax.broadcasted_iota(jnp.int32, shape, len(shape) - 1) < HEAD_DIM


def _flash_fwd(proj, cq, ck, ex=None):
    S = proj.shape[0]
    D = (proj.shape[1] - LANES) // 3
    HP = D // LANES
    t = _tile(S, 512)
    nq = S // t
    scale = 1.0 / math.sqrt(HEAD_DIM)

    def body(q_ref, k_ref, v_ref, cq_ref, ck_ref, o_ref, lse_ref, kb, vb):
        qi = pl.program_id(1)

        @pl.when(qi == 0)
        def _():
            kb[...] = k_ref[...].astype(BF16)
            vb[...] = v_ref[...].astype(BF16)

        first = _first_head((t, LANES))
        qf = q_ref[...] * scale
        qs_ = (jnp.where(first, qf, 0.0).astype(BF16), jnp.where(first, 0.0, qf).astype(BF16))
        cqv = cq_ref[0]

        def block(ki, carry, masked):
            m_old, l_old, acc = carry
            sl = pl.ds(pl.multiple_of(ki * t, t), t)
            kk, vv = kb[sl, :], vb[sl, :]
            m_new, l_new, corr, pv = [], [], [], []
            for a in range(2):
                s = lax.dot_general(qs_[a], kk, (((1,), (1,)), ((), ())), preferred_element_type=F32)
                s = s + (cqv[:, a:a + 1] - ck_ref[a, ki])
                if masked:
                    s = jnp.where(_causal(t), s, NEG_INF)
                m_a = jnp.maximum(m_old[a], jnp.max(s, axis=1, keepdims=True))
                p = jnp.exp(s - m_a)
                c_a = jnp.exp(m_old[a] - m_a)
                m_new.append(m_a)
                corr.append(c_a)
                l_new.append(c_a * l_old[a] + jnp.sum(p, axis=1, keepdims=True))
                pv.append(jnp.dot(p.astype(BF16), vv, preferred_element_type=F32))
            acc = jnp.where(first, corr[0] * acc + pv[0], corr[1] * acc + pv[1])
            return tuple(m_new), tuple(l_new), acc

        neg = jnp.full((t, 1), NEG_INF, F32)
        zero = jnp.zeros((t, 1), F32)
        carry = lax.fori_loop(0, qi, lambda ki, c: block(ki, c, False),
                              ((neg, neg), (zero, zero), jnp.zeros((t, LANES), F32)))
        m, l, acc = block(qi, carry, True)
        o_ref[...] = acc / jnp.where(first, l[0], l[1])
        lse_ref[0, :, 0:1] = m[0] + jnp.log(l[0])
        lse_ref[0, :, 1:2] = m[1] + jnp.log(l[1])

    qblk = pl.BlockSpec((t, LANES), lambda h, i: (i, h))
    r2 = pl.BlockSpec((1, t, 2), lambda h, i: (h, i, 0))
    (o, lse), extra = _call(
        body, name="fox_attn_fwd",
        out_shape=(jax.ShapeDtypeStruct((S, D), F32), jax.ShapeDtypeStruct((HP, S, 2), F32)),
        grid=(HP, nq),
        in_specs=[qblk, pl.BlockSpec((S, LANES), lambda h, i: (0, HP + h)),
                  pl.BlockSpec((S, LANES), lambda h, i: (0, 2 * HP + h)), r2,
                  pl.BlockSpec((2, nq, 1, t), lambda h, i: (h, 0, 0, 0))],
        out_specs=(qblk, r2), scratch_shapes=[pltpu.VMEM((S, LANES), BF16), pltpu.VMEM((S, LANES), BF16)],
        args=(proj, proj, proj, cq, ck), semantics=("parallel", "arbitrary"), ex=ex)
    return o, lse, extra


def _flash_bwd(proj, rows, ck, o, do, ex=None):
    S = proj.shape[0]
    D = (proj.shape[1] - LANES) // 3
    HP = D // LANES
    t = _tile(S, 512)
    nb = S // t
    scale = 1.0 / math.sqrt(HEAD_DIM)

    def body(q_ref, k_ref, v_ref, rows_ref, ck_ref, o_ref, do_ref, dq_ref, dk_ref, dv_ref, dcq_ref, dck_ref,
             q_s, do_s, dl_s, dk_acc, dv_acc, dc_acc):
        kb = pl.program_id(1)

        @pl.when(kb == 0)
        def _():
            dq_ref[...] = jnp.zeros_like(dq_ref)
            dcq_ref[...] = jnp.zeros_like(dcq_ref)

            def prep(qb, c):
                sl = pl.ds(pl.multiple_of(qb * t, t), t)
                first = _first_head((t, LANES))
                qf = q_ref[sl, :] * scale
                dof = do_ref[sl, :]
                prod = dof * o_ref[sl, :]
                for a, keep in enumerate((first, jnp.logical_not(first))):
                    q_s[a, sl, :] = jnp.where(keep, qf, 0.0).astype(BF16)
                    do_s[a, sl, :] = jnp.where(keep, dof, 0.0).astype(BF16)
                    dl_s[sl, a:a + 1] = jnp.sum(jnp.where(keep, prod, 0.0), axis=1, keepdims=True)
                return c

            lax.fori_loop(0, nb, prep, 0)

        dk_acc[...] = jnp.zeros_like(dk_acc)
        dv_acc[...] = jnp.zeros_like(dv_acc)
        dc_acc[...] = jnp.zeros_like(dc_acc)
        first = _first_head((t, LANES))
        kf = k_ref[...]
        kk = kf.astype(BF16)
        k_own = (jnp.where(first, kf, 0.0).astype(BF16), jnp.where(first, 0.0, kf).astype(BF16))
        vv = v_ref[...].astype(BF16)

        def block(qb, masked):
            sl = pl.ds(pl.multiple_of(qb * t, t), t)
            rv = rows_ref[0, sl, :]
            dlv = dl_s[sl, :]
            dq_new = dq_ref[sl, :]
            for a in range(2):
                qv = q_s[a, sl, :]
                dob = do_s[a, sl, :]
                s = lax.dot_general(qv, kk, (((1,), (1,)), ((), ())), preferred_element_type=F32)
                s = s + (rv[:, a:a + 1] - ck_ref[a, 0])
                if masked:
                    s = jnp.where(_causal(t), s, NEG_INF)
                p = jnp.exp(s - rv[:, 2 + a:3 + a])
                dv_acc[...] += lax.dot_general(p.astype(BF16), dob, (((0,), (0,)), ((), ())),
                                               preferred_element_type=F32)
                dp = lax.dot_general(dob, vv, (((1,), (1,)), ((), ())), preferred_element_type=F32)
                ds = p * (dp - dlv[:, a:a + 1])
                dsb = ds.astype(BF16)
                dk_acc[...] += lax.dot_general(dsb, qv, (((0,), (0,)), ((), ())), preferred_element_type=F32)
                dq_new = dq_new + jnp.dot(dsb, k_own[a], preferred_element_type=F32) * scale
                dcq_ref[0, sl, a:a + 1] += jnp.sum(ds, axis=1, keepdims=True)
                dc_acc[a:a + 1, :] -= jnp.sum(ds, axis=0, keepdims=True)
            dq_ref[sl, :] = dq_new

        block(kb, True)

        def rest(qb, c):
            block(qb, False)
            return c

        lax.fori_loop(kb + 1, nb, rest, 0)
        dk_ref[...] = dk_acc[...]
        dv_ref[...] = dv_acc[...]
        dck_ref[0, 0] = dc_acc[0:1, :]
        dck_ref[1, 0] = dc_acc[1:2, :]

    full = lambda c0: pl.BlockSpec((S, LANES), lambda h, j, c0=c0: (0, c0 + h))
    blk = lambda c0: pl.BlockSpec((t, LANES), lambda h, j, c0=c0: (j, c0 + h))
    ck_s = pl.BlockSpec((2, 1, 1, t), lambda h, j: (h, j, 0, 0))
    f32 = lambda *s: jax.ShapeDtypeStruct(s, F32)
    outs, extra = _call(
        body, name="fox_attn_bwd",
        out_shape=(f32(S, D), f32(S, D), f32(S, D), f32(HP, S, 2), f32(2 * HP, nb, 1, t)),
        grid=(HP, nb),
        in_specs=[full(0), blk(HP), blk(2 * HP), pl.BlockSpec((1, S, 4), lambda h, j: (h, 0, 0)), ck_s,
                  full(0), full(0)],
        out_specs=(full(0), blk(0), blk(0), pl.BlockSpec((1, S, 2), lambda h, j: (h, 0, 0)), ck_s),
        scratch_shapes=[pltpu.VMEM((2, S, LANES), BF16), pltpu.VMEM((2, S, LANES), BF16),
                        pltpu.VMEM((S, 2), F32), pltpu.VMEM((t, LANES), F32), pltpu.VMEM((t, LANES), F32),
                        pltpu.VMEM((2, t), F32)],
        args=(proj, proj, proj, rows, ck, o, do), semantics=("parallel", "arbitrary"), ex=ex)
    return (*outs, extra)


def _s5_consts(T):
    rows = T * SUBLANES
    rr = lax.broadcasted_iota(jnp.int32, (rows, T), 0)
    tt = lax.broadcasted_iota(jnp.int32, (rows, T), 1)
    rep = jnp.where(rr // SUBLANES == tt, 1.0, 0.0).astype(BF16)
    r2 = lax.broadcasted_iota(jnp.int32, (rows, S5_PART), 0)
    c2 = lax.broadcasted_iota(jnp.int32, (rows, S5_PART), 1)
    mask = (c2 // (S5_PART // SUBLANES)) == (r2 % SUBLANES)
    return rep, mask


def _gelu(y):
    c = math.sqrt(2.0 / math.pi)
    return 0.5 * y * (1.0 + jnp.tanh(c * (y + 0.044715 * y * y * y)))


def _gelu_grad(y):
    c = math.sqrt(2.0 / math.pi)
    th = jnp.tanh(c * (y + 0.044715 * y * y * y))
    return 0.5 * (1.0 + th) + 0.5 * y * (1.0 - th * th) * c * (1.0 + 3.0 * 0.044715 * y * y)


def _s5_fwd(x, rmat, cmat, lam, dskip, ex=None):
    S, D = x.shape
    NQ = D // S5_PART
    T = _tile(S, 128, SUBLANES)
    rows = T * SUBLANES

    def body(x_ref, r_ref, c_ref, lam_ref, d_ref, y_ref, yg_ref, h_ref, bu_s, carry):
        i = pl.program_id(0)

        @pl.when(i == 0)
        def _():
            carry[...] = jnp.zeros_like(carry)

        rep, mask = _s5_consts(T)
        cols = [pl.ds(q * S5_PART, S5_PART) for q in range(NQ)]
        for q in range(NQ):
            xrep = jnp.dot(rep, x_ref[:, cols[q]].astype(BF16), preferred_element_type=F32)
            lx = jnp.where(mask, xrep, 0.0).astype(BF16)
            bu_s[q] = jnp.dot(lx, r_ref[q], preferred_element_type=F32)
        lam_v = [(lam_ref[q, :, 0:LANES], lam_ref[q, :, LANES:2 * LANES]) for q in range(NQ)]

        def step(t, c):
            o = pl.multiple_of(t * SUBLANES, SUBLANES)
            new = []
            for q in range(NQ):
                hr, hi = c[q]
                ar, ai = lam_v[q]
                sl = bu_s[q, pl.ds(o, SUBLANES), :]
                nhr = ar * hr - ai * hi + sl[:, 0:LANES]
                nhi = ar * hi + ai * hr + sl[:, LANES:2 * LANES]
                h_ref[q, pl.ds(o, SUBLANES), 0:LANES] = nhr
                h_ref[q, pl.ds(o, SUBLANES), LANES:2 * LANES] = nhi
                new.append((nhr, nhi))
            return tuple(new)

        fin = lax.fori_loop(0, T, step,
                            tuple((carry[q, :, 0:LANES], carry[q, :, LANES:2 * LANES]) for q in range(NQ)))
        for q in range(NQ):
            carry[q, :, 0:LANES] = fin[q][0]
            carry[q, :, LANES:2 * LANES] = fin[q][1]
            z = jnp.dot(h_ref[q].astype(BF16), c_ref[q], preferred_element_type=F32)
            z = jnp.where(mask, z, 0.0)
            y = jnp.sum(z.reshape(T, SUBLANES, S5_PART), axis=1) + d_ref[:, cols[q]] * x_ref[:, cols[q]]
            y_ref[:, cols[q]] = y
            yg_ref[:, cols[q]] = _gelu(y).astype(BF16)

    xs = pl.BlockSpec((T, D), lambda i: (i, 0))
    ms = pl.BlockSpec((NQ, S5_PART, S5_PART), lambda i: (0, 0, 0))
    outs, extra = _call(
        body, name="s5_scan_fwd",
        out_shape=(jax.ShapeDtypeStruct((S, D), F32), jax.ShapeDtypeStruct((S, D), BF16),
                   jax.ShapeDtypeStruct((NQ, S * SUBLANES, S5_PART), F32)),
        grid=(S // T,),
        in_specs=[xs, ms, ms, pl.BlockSpec((NQ, SUBLANES, S5_PART), lambda i: (0, 0, 0)),
                  pl.BlockSpec((1, D), lambda i: (0, 0))],
        out_specs=(xs, xs, pl.BlockSpec((NQ, rows, S5_PART), lambda i: (0, i, 0))),
        scratch_shapes=[pltpu.VMEM((NQ, rows, S5_PART), F32), pltpu.VMEM((NQ, SUBLANES, S5_PART), F32)],
        args=(x, rmat, cmat, lam, dskip), semantics=("arbitrary",), ex=ex)
    return (*outs, extra)


def _s5_bwd(x, y, dyg, hs, rmat, cmat, lam, dskip, res, res_scale, ex=None):
    S, D = x.shape
    NQ = D // S5_PART
    T = _tile(S, 128, SUBLANES)
    nb = S // T
    rows = T * SUBLANES

    def body(x_ref, y_ref, dyg_ref, res_ref, h_ref, hp_ref, r_ref, c_ref, lam_ref, d_ref,
             dx_ref, dr_ref, dc_ref, dlam_ref, dd_ref, dh_s, g_s, hs_s, carry):
        i = pl.program_id(0)

        @pl.when(i == 0)
        def _():
            carry[...] = jnp.zeros_like(carry)
            dr_ref[...] = jnp.zeros_like(dr_ref)
            dc_ref[...] = jnp.zeros_like(dc_ref)
            dlam_ref[...] = jnp.zeros_like(dlam_ref)
            dd_ref[...] = jnp.zeros_like(dd_ref)

        rep, mask = _s5_consts(T)
        cols = [pl.ds(q * S5_PART, S5_PART) for q in range(NQ)]
        dys, ldys = [], []
        for q in range(NQ):
            dy = dyg_ref[:, cols[q]] * _gelu_grad(y_ref[:, cols[q]])
            dyrep = jnp.dot(rep, dy.astype(BF16), preferred_element_type=F32)
            ldy = jnp.where(mask, dyrep, 0.0).astype(BF16)
            dh_s[q] = lax.dot_general(ldy, c_ref[q], (((1,), (1,)), ((), ())), preferred_element_type=F32)
            dys.append(dy)
            ldys.append(ldy)
        lam_v = [(lam_ref[q, :, 0:LANES], lam_ref[q, :, LANES:2 * LANES]) for q in range(NQ)]

        def step(n, c):
            o = pl.multiple_of((T - 1 - n) * SUBLANES, SUBLANES)
            new = []
            for q in range(NQ):
                gr, gi = c[q]
                ar, ai = lam_v[q]
                sl = dh_s[q, pl.ds(o, SUBLANES), :]
                ngr = sl[:, 0:LANES] + ar * gr + ai * gi
                ngi = sl[:, LANES:2 * LANES] - ai * gr + ar * gi
                g_s[q, pl.ds(o, SUBLANES), 0:LANES] = ngr
                g_s[q, pl.ds(o, SUBLANES), LANES:2 * LANES] = ngi
                new.append((ngr, ngi))
            return tuple(new)

        fin = lax.fori_loop(0, T, step,
                            tuple((carry[q, :, 0:LANES], carry[q, :, LANES:2 * LANES]) for q in range(NQ)))
        for q in range(NQ):
            carry[q, :, 0:LANES] = fin[q][0]
            carry[q, :, LANES:2 * LANES] = fin[q][1]
            xv = x_ref[:, cols[q]]
            hv = h_ref[q]
            hs_s[0:SUBLANES, :] = jnp.where(i == nb - 1, 0.0, hp_ref[q])
            hs_s[SUBLANES:rows + SUBLANES, :] = hv
            hprev = hs_s[0:rows, :]
            gv = g_s[q]
            g_re, g_im = gv[:, 0:LANES], gv[:, LANES:2 * LANES]
            hp_re, hp_im = hprev[:, 0:LANES], hprev[:, LANES:2 * LANES]
            dar = jnp.sum((g_re * hp_re + g_im * hp_im).reshape(T, SUBLANES, LANES), axis=0)
            dai = jnp.sum((g_im * hp_re - g_re * hp_im).reshape(T, SUBLANES, LANES), axis=0)
            dlam_ref[q, :, 0:LANES] += dar
            dlam_ref[q, :, LANES:2 * LANES] += dai

            gb = gv.astype(BF16)
            xrep = jnp.dot(rep, xv.astype(BF16), preferred_element_type=F32)
            lx = jnp.where(mask, xrep, 0.0).astype(BF16)
            dr_ref[q] += lax.dot_general(lx, gb, (((0,), (0,)), ((), ())), preferred_element_type=F32)
            dc_ref[q] += lax.dot_general(hv.astype(BF16), ldys[q], (((0,), (0,)), ((), ())),
                                         preferred_element_type=F32)
            zx = lax.dot_general(gb, r_ref[q], (((1,), (1,)), ((), ())), preferred_element_type=F32)
            zx = jnp.where(mask, zx, 0.0)
            dx_ref[:, cols[q]] = (jnp.sum(zx.reshape(T, SUBLANES, S5_PART), axis=1) + d_ref[:, cols[q]] * dys[q]
                                  + res_scale * res_ref[:, cols[q]])
            dd_ref[:, cols[q]] += jnp.sum(dys[q] * xv, axis=0, keepdims=True)

    xs = pl.BlockSpec((T, D), lambda i: (nb - 1 - i, 0))
    ms = pl.BlockSpec((NQ, S5_PART, S5_PART), lambda i: (0, 0, 0))
    ls = pl.BlockSpec((NQ, SUBLANES, S5_PART), lambda i: (0, 0, 0))
    ds_ = pl.BlockSpec((1, D), lambda i: (0, 0))
    outs, extra = _call(
        body, name="s5_scan_bwd",
        out_shape=(jax.ShapeDtypeStruct((S, D), F32), jax.ShapeDtypeStruct((NQ, S5_PART, S5_PART), F32),
                   jax.ShapeDtypeStruct((NQ, S5_PART, S5_PART), F32),
                   jax.ShapeDtypeStruct((NQ, SUBLANES, S5_PART), F32), jax.ShapeDtypeStruct((1, D), F32)),
        grid=(nb,),
        in_specs=[xs, xs, xs, xs, pl.BlockSpec((NQ, rows, S5_PART), lambda i: (0, nb - 1 - i, 0)),
                  pl.BlockSpec((NQ, SUBLANES, S5_PART), lambda i: (0, jnp.maximum((nb - 1 - i) * T - 1, 0), 0)),
                  ms, ms, ls, ds_],
        out_specs=(xs, ms, ms, ls, ds_),
        scratch_shapes=[pltpu.VMEM((NQ, rows, S5_PART), F32), pltpu.VMEM((NQ, rows, S5_PART), F32),
                        pltpu.VMEM((rows + SUBLANES, S5_PART), F32), pltpu.VMEM((NQ, SUBLANES, S5_PART), F32)],
        args=(x, y, dyg, res, hs, hs, rmat, cmat, lam, dskip), semantics=("arbitrary",), ex=ex)
    return (*outs, extra)


def _s5_discretise(a_re, a_im, log_dt, b_re, b_im):
    dt = jnp.exp(log_dt)[:, None]
    mag = jnp.exp(a_re * dt)
    ang = a_im * dt
    lb_re = mag * jnp.cos(ang)
    lb_im = mag * jnp.sin(ang)
    den = a_re * a_re + a_im * a_im
    nr = lb_re - 1.0
    ni = lb_im
    z_re = (nr * a_re + ni * a_im) / den
    z_im = (ni * a_re - nr * a_im) / den
    bb_re = z_re[..., None] * b_re - z_im[..., None] * b_im
    bb_im = z_re[..., None] * b_im + z_im[..., None] * b_re
    return lb_re, lb_im, bb_re, bb_im


def _s5_expand(w):
    G = w.shape[0]
    NQ = G // 16
    base = w.reshape(NQ, S5_PART, S5_STATE)
    half = (jnp.arange(S5_PART) // S5_GROUP) % 2
    sel = (half[:, None] == jnp.arange(2)[None, :]).astype(w.dtype)
    out = base[:, :, None, :] * sel[None, :, :, None]
    return out.reshape(NQ, S5_PART, 2 * S5_STATE)


def _s5_extract(m):
    NQ = m.shape[0]
    half = (jnp.arange(S5_PART) // S5_GROUP) % 2
    sel = (half[:, None] == jnp.arange(2)[None, :]).astype(m.dtype)
    base = jnp.sum(m.reshape(NQ, S5_PART, 2, S5_STATE) * sel[None, :, :, None], axis=2)
    return base.reshape(NQ * 16, S5_GROUP, S5_STATE)


def _s5_slab(v):
    return v.reshape(v.shape[0] // 16, SUBLANES, LANES)


def _adamw(w, g, m, v, name):
    R, C = w.shape
    tr = _tile(R, 256, SUBLANES)
    c1 = 1.0 / (1.0 - ADAM_B1 ** ADAM_STEP)
    c2 = 1.0 / (1.0 - ADAM_B2 ** ADAM_STEP)

    def body(w_ref, g_ref, m_ref, v_ref, d_ref, nm_ref, nv_ref):
        gv = g_ref[...]
        nm = ADAM_B1 * m_ref[...] + (1.0 - ADAM_B1) * gv
        nv = ADAM_B2 * v_ref[...] + (1.0 - ADAM_B2) * (gv * gv)
        nm_ref[...] = nm
        nv_ref[...] = nv
        d_ref[...] = -ADAM_LR * ((nm * c1) / (jnp.sqrt(nv * c2) + ADAM_EPS) + ADAM_WD * w_ref[...])

    blk = pl.BlockSpec((tr, C), lambda i: (i, 0))
    sh = jax.ShapeDtypeStruct((R, C), F32)
    return pl.pallas_call(
        body, name=name, out_shape=(sh, sh, sh), grid=(R // tr,), in_specs=[blk] * 4, out_specs=(blk,) * 3,
        compiler_params=_params(("parallel",)),
    )(w, g, m, v)


def _adamw_recv(parts, w, m, v, name):
    L, R, C = w.shape
    tr = _tile(R, 256, SUBLANES)
    c1 = 1.0 / (1.0 - ADAM_B1 ** ADAM_STEP)
    c2 = 1.0 / (1.0 - ADAM_B2 ** ADAM_STEP)

    def body(*refs):
        p_refs = refs[:L]
        w_ref, m_ref, v_ref, g_ref, d_ref, nm_ref, nv_ref = refs[L:]
        li = pl.program_id(0)
        for l in range(L):
            @pl.when(li == l)
            def _(p_ref=p_refs[l]):
                gv = p_ref[0].astype(F32)
                for k in range(1, N_DEV):
                    gv = gv + p_ref[k].astype(F32)
                g_ref[...] = gv
                nm = ADAM_B1 * m_ref[...] + (1.0 - ADAM_B1) * gv
                nv = ADAM_B2 * v_ref[...] + (1.0 - ADAM_B2) * (gv * gv)
                nm_ref[...] = nm
                nv_ref[...] = nv
                d_ref[...] = -ADAM_LR * ((nm * c1) / (jnp.sqrt(nv * c2) + ADAM_EPS) + ADAM_WD * w_ref[...])

    p_specs = [pl.BlockSpec((N_DEV, tr, C), lambda li, i, l=l: (0, jnp.where(li == l, i, 0), 0)) for l in range(L)]
    blk = pl.BlockSpec((None, tr, C), lambda li, i: (li, i, 0))
    sh = jax.ShapeDtypeStruct((L, R, C), F32)
    return pl.pallas_call(
        body, name=name, out_shape=(sh, sh, sh, sh), grid=(L, R // tr),
        in_specs=p_specs + [blk, blk, blk], out_specs=(blk,) * 4,
        compiler_params=_params(("parallel", "parallel")),
    )(*parts, w, m, v)


def _sum8(parts, name):
    _, R, C = parts.shape
    tr = _tile(R, 256, SUBLANES)

    def body(p_ref, o_ref):
        acc = p_ref[0].astype(F32)
        for k in range(1, N_DEV):
            acc = acc + p_ref[k].astype(F32)
        o_ref[...] = acc

    return pl.pallas_call(
        body, name=name, out_shape=jax.ShapeDtypeStruct((R, C), F32), grid=(R // tr,),
        in_specs=[pl.BlockSpec((N_DEV, tr, C), lambda i: (0, i, 0))],
        out_specs=pl.BlockSpec((tr, C), lambda i: (i, 0)), compiler_params=_params(("parallel",)),
    )(parts)


def _peers():
    x, y, c = lax.axis_index("x"), lax.axis_index("y"), lax.axis_index("c")
    me = 4 * x + 2 * y + c
    out = []
    for k in range(1, N_DEV):
        kx, ky, kc = (k >> 2) & 1, (k >> 1) & 1, k & 1
        px, py, pc = x ^ kx, y ^ ky, c ^ kc
        out.append(((px, py, pc), 4 * px + 2 * py + pc))
    return me, out


class Exchange:
    def __init__(self, xs, scatter):
        self.xs = list(xs)
        self.scatter = list(scatter)
        self.n = len(self.xs)

    def out_shapes(self):
        return tuple(jax.ShapeDtypeStruct(x.shape if sc else (N_DEV,) + x.shape, x.dtype)
                     for x, sc in zip(self.xs, self.scatter))

    def sems(self):
        np_ = N_DEV - 1
        return [pltpu.SemaphoreType.DMA((self.n * np_,)), pltpu.SemaphoreType.DMA((self.n * np_,)),
                pltpu.SemaphoreType.DMA((self.n,))]

    def _copies(self, x_refs, o_refs, send_sems, recv_sems, local_sems, landing):
        np_ = N_DEV - 1
        me, peers = _peers()
        owns, remote = [], []
        for i, (x_ref, o_ref, sc) in enumerate(zip(x_refs, o_refs, self.scatter)):
            owns.append(pltpu.make_async_copy(x_ref.at[me] if sc else x_ref, o_ref.at[me], local_sems.at[i]))
            for k, (dev, idx) in enumerate(peers):
                remote.append(pltpu.make_async_remote_copy(
                    src_ref=x_ref.at[idx] if sc else x_ref, dst_ref=o_ref.at[idx if landing else me],
                    send_sem=send_sems.at[i * np_ + k], recv_sem=recv_sems.at[i * np_ + k],
                    device_id=dev, device_id_type=MESH))
        return owns, remote

    def start(self, *refs):
        owns, sends = self._copies(*refs, landing=False)
        for cp in owns + sends:
            cp.start()

    def wait(self, *refs):
        owns, lands = self._copies(*refs, landing=True)
        for cp in lands:
            cp.wait_recv()
        for cp in lands:
            cp.wait_send()
        for cp in owns:
            cp.wait()


def _exchange(ex, name):
    n = ex.n

    def body(*refs):
        x_refs, o_refs, sems = refs[:n], refs[n:2 * n], refs[2 * n:]
        ex.start(x_refs, o_refs, *sems)
        ex.wait(x_refs, o_refs, *sems)

    hbm = pl.BlockSpec(memory_space=pltpu.HBM)
    return pl.pallas_call(body, name=name, out_shape=ex.out_shapes(), in_specs=[hbm] * n, out_specs=(hbm,) * n,
                          scratch_shapes=ex.sems())(*ex.xs)


def _call(body, *, name, out_shape, grid, in_specs, out_specs, scratch_shapes, args, semantics, ex=None):
    if ex is None:
        return pl.pallas_call(body, name=name, out_shape=out_shape, grid=grid, in_specs=in_specs, out_specs=out_specs,
                              scratch_shapes=scratch_shapes, compiler_params=_params(semantics))(*args), ()
    n, ni, no, ns = ex.n, len(args), len(out_shape), len(scratch_shapes)

    def wrapped(*refs):
        ins, cx = refs[:ni], refs[ni:ni + n]
        outs, co = refs[ni + n:ni + n + no], refs[ni + n + no:ni + 2 * n + no]
        scratch, sems = refs[ni + 2 * n + no:ni + 2 * n + no + ns], refs[ni + 2 * n + no + ns:]
        ids = [pl.program_id(a) for a in range(len(grid))]
        first = functools.reduce(jnp.logical_and, [i == 0 for i in ids])
        last = functools.reduce(jnp.logical_and, [i == g - 1 for i, g in zip(ids, grid)])

        @pl.when(first)
        def _():
            ex.start(cx, co, *sems)

        body(*ins, *outs, *scratch)

        @pl.when(last)
        def _():
            ex.wait(cx, co, *sems)

    hbm = pl.BlockSpec(memory_space=pltpu.HBM)
    res = pl.pallas_call(
        wrapped, name=name, out_shape=tuple(out_shape) + ex.out_shapes(), grid=grid,
        in_specs=list(in_specs) + [hbm] * n, out_specs=tuple(out_specs) + (hbm,) * n,
        scratch_shapes=list(scratch_shapes) + ex.sems(),
        compiler_params=_params(("arbitrary",) * len(grid)))(*args, *ex.xs)
    return res[:no], res[no:]


def _pack_flat(arrs):
    cat = jnp.concatenate([a.reshape(-1) for a in arrs])
    per = PACK_COLS * 2 * SUBLANES
    tot = -(-cat.shape[0] // per) * per
    return jnp.pad(cat, (0, tot - cat.shape[0])).reshape(tot // PACK_COLS, PACK_COLS)


def _unpack_flat(packed, shapes):
    flat = packed.reshape(-1)
    out, o = [], 0
    for s in shapes:
        n = math.prod(s)
        out.append(flat[o:o + n].reshape(s))
        o += n
    return out


def _ffn_fwd(x, wg, wo4, g, b, alpha, tag):
    h8 = _ffn_h(x, wg, tag + "_h")
    a4 = _swiglu_fwd(h8, tag + "_act")
    y = _ffn_y(a4, wo4, tag + "_y")
    out, xh, rstd = _res_ln_fwd(x, y, g, b, alpha, 0.5, tag + "_ln")
    return out, (x, h8, a4, xh, rstd)


def _ffn_bwd(dout, saved, wg, wo4, g, alpha, tag):
    x, h8, a4, xh, rstd = saved
    dz, dg, db = _ln_bwd(dout, xh, rstd, g, tag + "_ln_bwd")
    dw_out = _ffn_dwout(a4, dz, 0.5, tag + "_dwout")
    da4 = _ffn_da(dz, wo4, 0.5, tag + "_da")
    dh8 = _swiglu_bwd(h8, da4, tag + "_act_bwd")
    dw_in = _ffn_dwin(x, dh8, tag + "_dwin")
    dx = _ffn_dx(dh8, wg, dz, alpha, tag + "_dx")
    return dx, dw_in, dw_out.reshape(N_DEV, -1, dw_out.shape[-1]), dg, db


def _fox_fwd(x, w_in_pad, b_f_pad, w_o, g, b, alpha, tag, ex=None):
    S, D = x.shape
    H = D // HEAD_DIM
    proj = _mm(x, w_in_pad, name=tag + "_proj")
    fl = proj[:, 3 * D:]
    cum = _fox_gate_fwd(fl, b_f_pad)
    ch = cum[:, :H].T
    t = _tile(S, 512)
    cq = ch.reshape(H // 2, 2, S).transpose(0, 2, 1)
    ck = ch.reshape(H, S // t, 1, t)
    o, lse, extra = _flash_fwd(proj, cq, ck, ex)
    m = _mm(o, w_o, name=tag + "_out")
    out, xh, rstd = _res_ln_fwd(x, m, g, b, alpha, 1.0, tag + "_ln")
    return out, (x, proj, jnp.concatenate([cq, lse], axis=2), ck, o, fl, xh, rstd), extra


def _fox_bwd(dout, saved, w_in_pad, b_f_pad, w_o, g, alpha, tag, ex=None):
    x, proj, rows, ck, o, fl, xh, rstd = saved
    S, D = x.shape
    H = D // HEAD_DIM
    dz, dg, db = _ln_bwd(dout, xh, rstd, g, tag + "_ln_bwd")
    dw_o = _mm(o, dz, ta=True, name=tag + "_dwo")
    do = _mm(dz, w_o, tb=True, name=tag + "_do")
    dq, dk, dv, dcq, dck, extra = _flash_bwd(proj, rows, ck, o, do, ex)
    dcq = dcq.transpose(0, 2, 1).reshape(H, S)
    dcum = jnp.pad((dcq + dck.reshape(H, S)).T, ((0, 0), (0, LANES - H)))
    dfl, dbf = _fox_gate_bwd(dcum, fl, b_f_pad)
    dproj = jnp.concatenate([dq.astype(BF16), dk.astype(BF16), dv.astype(BF16), dfl.astype(BF16)], axis=1)
    dw_in = _mm(x, dproj, ta=True, name=tag + "_dwin")
    dx = _mm(dproj, w_in_pad, tb=True, add=dz, add_scale=alpha, name=tag + "_dx")
    shards = {"fox_w_in": _split(dw_in[None, :, :3 * D + H], True)[:, 0].astype(BF16),
              "fox_w_o": _split(dw_o[None], False)[:, 0].astype(BF16)}
    return dx, shards, {"fox_b_f": dbf[0, :H], "lnm_g": dg, "lnm_b": db}, extra


def _s5_mats(p):
    lb_re, lb_im, bb_re, bb_im = _s5_discretise(p["a_re"], p["a_im"], p["log_dt"], p["b_re"], p["b_im"])
    rmat = jnp.concatenate([_s5_expand(bb_re.transpose(0, 2, 1)), _s5_expand(bb_im.transpose(0, 2, 1))], axis=2)
    cmat = jnp.concatenate([_s5_expand(p["c_re"]).transpose(0, 2, 1), -_s5_expand(p["c_im"]).transpose(0, 2, 1)],
                           axis=1)
    lam = jnp.concatenate([_s5_slab(lb_re), _s5_slab(lb_im)], axis=2)
    return rmat.astype(BF16), cmat.astype(BF16), lam


def _s5_block_fwd(x, p, w_out, g, b, alpha, tag, ex=None):
    S, D = x.shape
    rmat, cmat, lam = _s5_mats(p)
    dskip = p["d"].reshape(1, D)
    y, yg, hs, extra = _s5_fwd(x, rmat, cmat, lam, dskip, ex)
    vg = _mm(yg, w_out, name=tag + "_vg")
    m = _glu_fwd(vg, tag + "_glu")
    out, xh, rstd = _res_ln_fwd(x, m, g, b, alpha, 1.0, tag + "_ln")
    return out, (x, y, yg, hs, vg, rmat, cmat, lam, dskip, xh, rstd), extra


def _s5_block_bwd(dout, saved, p, w_out, g, alpha, tag, ex=None):
    x, y, yg, hs, vg, rmat, cmat, lam, dskip, xh, rstd = saved
    S, D = x.shape
    G = D // S5_GROUP
    dz, dg, db = _ln_bwd(dout, xh, rstd, g, tag + "_ln_bwd")
    dvg = _glu_bwd(vg, dz, tag + "_glu_bwd")
    dw_out = _mm(yg, dvg, ta=True, name=tag + "_dwout")
    dyg = _mm(dvg, w_out, tb=True, name=tag + "_dyg")
    dx, dr, dc, dlam, dd, extra = _s5_bwd(x, y, dyg, hs, rmat, cmat, lam, dskip, dz, alpha, ex)
    dbb_re = _s5_extract(dr[:, :, :LANES]).transpose(0, 2, 1)
    dbb_im = _s5_extract(dr[:, :, LANES:]).transpose(0, 2, 1)
    dc_re = _s5_extract(dc[:, :LANES, :].transpose(0, 2, 1))
    dc_im = -_s5_extract(dc[:, LANES:, :].transpose(0, 2, 1))
    dlb_re = dlam[:, :, :LANES].reshape(G, S5_STATE)
    dlb_im = dlam[:, :, LANES:].reshape(G, S5_STATE)
    _, vjp = jax.vjp(_s5_discretise, p["a_re"], p["a_im"], p["log_dt"], p["b_re"], p["b_im"])
    da_re, da_im, dlog_dt, db_re, db_im = vjp((dlb_re, dlb_im, dbb_re, dbb_im))
    small = dict(s5_a_re=da_re, s5_a_im=da_im, s5_log_dt=dlog_dt, s5_b_re=db_re, s5_b_im=db_im, s5_c_re=dc_re,
                 s5_c_im=dc_im, s5_d=dd.reshape(G, S5_GROUP), lnm_g=dg, lnm_b=db)
    return dx, {"s5_w_out": _split(dw_out[None], True)[:, 0].astype(BF16)}, small, extra


FFN_NAMES = ("ffn1_w_in", "ffn1_w_out", "ffn2_w_in", "ffn2_w_out")
BIG = FFN_NAMES + ("fox_w_in", "fox_w_o", "s5_w_out")
BIG_SPLIT_COLS = {"ffn1_w_in": True, "ffn1_w_out": False, "ffn2_w_in": True, "ffn2_w_out": False,
                  "fox_w_in": True, "fox_w_o": False, "s5_w_out": True}
SMALL = ("ln1_g", "ln1_b", "lnm_g", "lnm_b", "ln2_g", "ln2_b", "fox_b_f", "s5_a_re", "s5_a_im", "s5_log_dt",
         "s5_b_re", "s5_b_im", "s5_c_re", "s5_c_im", "s5_d")
WEIGHTS = ("ffn1_w_in", "ffn1_w_out", "ln1_g", "ln1_b", "lnm_g", "lnm_b", "ffn2_w_in", "ffn2_w_out", "ln2_g", "ln2_b",
           "fox_w_in", "fox_b_f", "fox_w_o", "s5_a_re", "s5_a_im", "s5_log_dt", "s5_b_re", "s5_b_im", "s5_c_re",
           "s5_c_im", "s5_d", "s5_w_out")


def _join(gathered, split_cols):
    n, L, r, c = gathered.shape
    if split_cols:
        return gathered.transpose(1, 2, 0, 3).reshape(L, r, n * c)
    return gathered.transpose(1, 0, 2, 3).reshape(L, n * r, c)


def _split(full, split_cols):
    L, R, C = full.shape
    if split_cols:
        return full.reshape(L, R, N_DEV, C // N_DEV).transpose(2, 0, 1, 3)
    return full.reshape(L, N_DEV, R // N_DEV, C).transpose(1, 0, 2, 3)


def _group(i, part):
    if part == "b":
        return (("ffn2_w_in", i), ("ffn2_w_out", i))
    mixer = (("fox_w_in", i // 2), ("fox_w_o", i // 2)) if i % 2 == 0 else (("s5_w_out", i // 2),)
    return (("ffn1_w_in", i), ("ffn1_w_out", i)) + mixer


def _prepare(name, g8):
    if name in ("ffn1_w_in", "ffn2_w_in"):
        return g8
    if name in FFN_NAMES:
        n, r, c = g8.shape
        return g8.reshape(n // 2, 2 * r, c)
    full = _join(g8[:, None], BIG_SPLIT_COLS[name])[0]
    if name == "fox_w_in":
        full = jnp.pad(full, ((0, 0), (0, LANES - full.shape[0] // HEAD_DIM)))
    return full


def _local_step(x, target, small, shard_of=None, pregathered=None):
    S, D = x.shape
    H = D // HEAD_DIM
    depth = small["ln1_g"].shape[0]
    alpha = (2.0 * depth) ** 0.25
    local = pregathered is not None
    bf_pad = jnp.pad(small["fox_b_f"], ((0, 0), (0, LANES - H)))

    def s5_params(j):
        return {k: small["s5_" + k][j] for k in ("a_re", "a_im", "log_dt", "b_re", "b_im", "c_re", "c_im", "d")}

    if local:
        W = {k: _prepare(k[0], g8) for k, g8 in pregathered.items()}
    else:
        keys = _group(0, "a")
        got = _exchange(Exchange([shard_of(*k) for k in keys], [False] * len(keys)), "gather_first")
        W = {k: _prepare(k[0], g8) for k, g8 in zip(keys, got)}
    saved = []
    h = x
    for i in range(depth):
        j = i // 2
        keys = _group(i, "b") + (_group(i + 1, "a") if i + 1 < depth else ())
        ex = None if local else Exchange([shard_of(*k) for k in keys], [False] * len(keys))
        h, s1 = _ffn_fwd(h, W[("ffn1_w_in", i)], W[("ffn1_w_out", i)], small["ln1_g"][i], small["ln1_b"][i], alpha,
                         f"l{i}_ffn1")
        if i % 2 == 0:
            h, s2, extra = _fox_fwd(h, W[("fox_w_in", j)], bf_pad[j:j + 1], W[("fox_w_o", j)], small["lnm_g"][i],
                                    small["lnm_b"][i], alpha, f"l{i}_fox", ex)
        else:
            h, s2, extra = _s5_block_fwd(h, s5_params(j), W[("s5_w_out", j)], small["lnm_g"][i],
                                         small["lnm_b"][i], alpha, f"l{i}_s5", ex)
        if not local:
            W.update({k: _prepare(k[0], g8) for k, g8 in zip(keys, extra)})
        h, s3 = _ffn_fwd(h, W[("ffn2_w_in", i)], W[("ffn2_w_out", i)], small["ln2_g"][i], small["ln2_b"][i], alpha,
                         f"l{i}_ffn2")
        saved.append((s1, s2, s3))

    dh, loss_part = _loss_fwd_bwd(h, target)

    arrived = {}
    pending = {}
    gs = {k: [None] * small[k].shape[0] for k in SMALL}
    for i in reversed(range(depth)):
        j = i // 2
        s1, s2, s3 = saved[i]
        dh, pending[("ffn2_w_in", i)], pending[("ffn2_w_out", i)], gs["ln2_g"][i], gs["ln2_b"][i] = _ffn_bwd(
            dh, s3, W[("ffn2_w_in", i)], W[("ffn2_w_out", i)], small["ln2_g"][i], alpha, f"l{i}_ffn2")
        keys = list(pending)
        ex = None if local else Exchange([pending[k] for k in keys], [True] * len(keys))
        if i % 2 == 0:
            dh, mix, sg, extra = _fox_bwd(dh, s2, W[("fox_w_in", j)], bf_pad[j:j + 1], W[("fox_w_o", j)],
                                          small["lnm_g"][i], alpha, f"l{i}_fox", ex)
        else:
            dh, mix, sg, extra = _s5_block_bwd(dh, s2, s5_params(j), W[("s5_w_out", j)], small["lnm_g"][i], alpha,
                                               f"l{i}_s5", ex)
        arrived.update(zip(keys, [pending[k] for k in keys] if local else extra))
        pending = {(k, j): val for k, val in mix.items()}
        for k, val in sg.items():
            gs[k][i if k in ("lnm_g", "lnm_b") else j] = val
        dh, pending[("ffn1_w_in", i)], pending[("ffn1_w_out", i)], gs["ln1_g"][i], gs["ln1_b"][i] = _ffn_bwd(
            dh, s1, W[("ffn1_w_in", i)], W[("ffn1_w_out", i)], small["ln1_g"][i], alpha, f"l{i}_ffn1")
    gs = {k: jnp.stack(v) for k, v in gs.items()}
    return loss_part, dh, arrived, pending, gs


def kernel(x, ffn1_w_in, ffn1_w_out, ln1_g, ln1_b, lnm_g, lnm_b, ffn2_w_in, ffn2_w_out, ln2_g, ln2_b, fox_w_in, fox_b_f, fox_w_o, s5_a_re, s5_a_im, s5_log_dt, s5_b_re, s5_b_im, s5_c_re, s5_c_im, s5_d, s5_w_out, loss_target, m_ffn1_w_in, m_ffn1_w_out, m_ln1_g, m_ln1_b, m_lnm_g, m_lnm_b, m_ffn2_w_in, m_ffn2_w_out, m_ln2_g, m_ln2_b, m_fox_w_in, m_fox_b_f, m_fox_w_o, m_s5_a_re, m_s5_a_im, m_s5_log_dt, m_s5_b_re, m_s5_b_im, m_s5_c_re, m_s5_c_im, m_s5_d, m_s5_w_out, v_ffn1_w_in, v_ffn1_w_out, v_ln1_g, v_ln1_b, v_lnm_g, v_lnm_b, v_ffn2_w_in, v_ffn2_w_out, v_ln2_g, v_ln2_b, v_fox_w_in, v_fox_b_f, v_fox_w_o, v_s5_a_re, v_s5_a_im, v_s5_log_dt, v_s5_b_re, v_s5_b_im, v_s5_c_re, v_s5_c_im, v_s5_d, v_s5_w_out):
    args = dict(locals())
    w = {k: args[k] for k in WEIGHTS}
    m = {k: args["m_" + k] for k in WEIGHTS}
    v = {k: args["v_" + k] for k in WEIGHTS}
    small = {k: w[k] for k in SMALL}

    wb = {k: w[k].astype(BF16) for k in BIG}
    loss_part, dx, arrived, last, gs = _local_step(x[0], loss_target[0], small, lambda k, idx: wb[k][idx])
    loss = lax.psum(loss_part, ("x", "y", "c"))

    small_shapes = [w[k].shape for k in SMALL]
    ex = Exchange(list(last.values()) + [_pack_flat([gs[k] for k in SMALL]).astype(BF16)],
                  [True] * len(last) + [False])
    *got, g_small_all = _exchange(ex, "scatter_last_gather_small")
    arrived.update(zip(last, got))
    g_small_flat = _sum8(g_small_all, "sum_small_grads")

    grads, delta, new_m, new_v = {}, {}, {}, {}
    for k in BIG:
        parts = [arrived[(k, l)] for l in range(w[k].shape[0])]
        grads[k], delta[k], new_m[k], new_v[k] = _adamw_recv(parts, w[k], m[k], v[k], "adamw_" + k)
    pk = lambda d: _pack_flat([d[k] for k in SMALL])
    d_, m_, v_ = _adamw(pk(w), g_small_flat, pk(m), pk(v), "adamw_small")
    for dst, flat in ((grads, g_small_flat), (delta, d_), (new_m, m_), (new_v, v_)):
        dst.update(zip(SMALL, _unpack_flat(flat, small_shapes)))

    return (loss, dx[None], *[grads[k] for k in WEIGHTS], *[delta[k] for k in WEIGHTS],
            *[new_m[k] for k in WEIGHTS], *[new_v[k] for k in WEIGHTS])
```

```python
import functools
import math

import jax
import jax.numpy as jnp
from jax import lax
from jax.experimental import pallas as pl
from jax.experimental.pallas import tpu as pltpu

F32 = jnp.float32
BF16 = jnp.bfloat16

N_DEV = 8
HEAD_DIM = 64
S5_GROUP = 16
S5_STATE = 64
LANES = 128
SUBLANES = 8
S5_PART = 256
LN_EPS = 1e-5
NEG_INF = -1e30
ADAM_LR, ADAM_B1, ADAM_B2, ADAM_EPS, ADAM_WD, ADAM_STEP = 0.001, 0.9, 0.999, 1e-08, 0.01, 10
VMEM_LIMIT = 48 * 1024 * 1024
PACK_COLS = 1024
PACK_ROW_ALIGN = 512

MESH = pl.DeviceIdType.MESH


def _tile(dim, pref, align=LANES):
    if dim <= pref:
        return dim
    t = (pref // align) * align
    while t >= align:
        if dim % t == 0:
            return t
        t -= align
    return dim


def _params(sem):
    return pltpu.CompilerParams(dimension_semantics=sem, vmem_limit_bytes=VMEM_LIMIT)


def _mm(a, b, *, ta=False, tb=False, out_dtype=F32, scale=None, add=None, add_scale=1.0,
        tm=512, tn=1408, tk=1408, name="mm"):
    if ta:
        K, M = a.shape
    else:
        M, K = a.shape
    if tb:
        N, K2 = b.shape
    else:
        K2, N = b.shape
    assert K == K2, (a.shape, b.shape, ta, tb)
    tm, tn, tk = _tile(M, tm), _tile(N, tn), _tile(K, tk)
    a_spec = pl.BlockSpec((tk, tm), lambda i, j, k: (k, i)) if ta else pl.BlockSpec((tm, tk), lambda i, j, k: (i, k))
    b_spec = pl.BlockSpec((tn, tk), lambda i, j, k: (j, k)) if tb else pl.BlockSpec((tk, tn), lambda i, j, k: (k, j))
    o_spec = pl.BlockSpec((tm, tn), lambda i, j, k: (i, j))
    return _mm_core(name, a, b, a_spec, b_spec, o_spec, jax.ShapeDtypeStruct((M, N), out_dtype),
                    (M // tm, N // tn, K // tk), (tm, tn), ta, tb, scale, add, add_scale)


def _mm_core(name, a, b, a_spec, b_spec, o_spec, out_shape, grid, acc_shape, ta, tb, scale=None, add=None,
             add_scale=1.0, ex=None):
    nk = grid[2]

    def body(*refs):
        if add is None:
            a_ref, b_ref, o_ref, acc = refs
            add_ref = None
        else:
            a_ref, b_ref, add_ref, o_ref, acc = refs
        k = pl.program_id(2)

        @pl.when(k == 0)
        def _():
            acc[...] = jnp.zeros_like(acc)

        dims = (((0 if ta else 1,), (1 if tb else 0,)), ((), ()))
        acc[...] += lax.dot_general(a_ref[...].astype(BF16), b_ref[...].astype(BF16), dims,
                                    preferred_element_type=F32)

        @pl.when(k == nk - 1)
        def _():
            r = acc[...]
            if scale is not None:
                r = r * scale
            if add_ref is not None:
                r = r + add_scale * add_ref[...]
            o_ref[...] = r.astype(out_shape.dtype)

    in_specs = [a_spec, b_spec]
    args = [a, b]
    if add is not None:
        in_specs.append(o_spec)
        args.append(add)
    (res,), extra = _call(body, name=name, out_shape=(out_shape,), grid=grid, in_specs=in_specs, out_specs=(o_spec,),
                          scratch_shapes=[pltpu.VMEM(acc_shape, F32)], args=args,
                          semantics=("parallel", "parallel", "arbitrary"), ex=ex)
    return res if ex is None else (res, extra)


def _ffn_h(x, wg, name):
    S, D = x.shape
    n, _, c = wg.shape
    tm = _tile(S, 1024, SUBLANES)
    return _mm_core(name, x, wg, pl.BlockSpec((tm, D), lambda i, j, k: (i, 0)),
                    pl.BlockSpec((None, D, c), lambda i, j, k: (j, 0, 0)),
                    pl.BlockSpec((None, tm, c), lambda i, j, k: (j, i, 0)),
                    jax.ShapeDtypeStruct((n, S, c), BF16), (S // tm, n, 1), (tm, c), False, False)


def _ffn_y(a4, wo4, name):
    nb, S, c = a4.shape
    D = wo4.shape[-1]
    tm, tn = _tile(S, 1024, SUBLANES), _tile(D, 1024)
    return _mm_core(name, a4, wo4, pl.BlockSpec((None, tm, c), lambda i, j, k: (k, i, 0)),
                    pl.BlockSpec((None, c, tn), lambda i, j, k: (k, 0, j)),
                    pl.BlockSpec((tm, tn), lambda i, j, k: (i, j)),
                    jax.ShapeDtypeStruct((S, D), F32), (S // tm, D // tn, nb), (tm, tn), False, False)


def _ffn_dwout(a4, dz, scale, name, ex=None):
    nb, S, c = a4.shape
    D = dz.shape[1]
    tk, tn = _tile(S, 1024, SUBLANES), _tile(D, 1024)
    return _mm_core(name, a4, dz, pl.BlockSpec((None, tk, c), lambda i, j, k: (i, k, 0)),
                    pl.BlockSpec((tk, tn), lambda i, j, k: (k, j)),
                    pl.BlockSpec((None, c, tn), lambda i, j, k: (i, 0, j)),
                    jax.ShapeDtypeStruct((nb, c, D), BF16), (nb, D // tn, S // tk), (c, tn), True, False, scale,
                    ex=ex)


def _ffn_da(dz, wo4, scale, name, ex=None):
    S, D = dz.shape
    nb, c = wo4.shape[0], wo4.shape[1]
    tm = _tile(S, 1024, SUBLANES)
    return _mm_core(name, dz, wo4, pl.BlockSpec((tm, D), lambda i, j, k: (i, 0)),
                    pl.BlockSpec((None, c, D), lambda i, j, k: (j, 0, 0)),
                    pl.BlockSpec((None, tm, c), lambda i, j, k: (j, i, 0)),
                    jax.ShapeDtypeStruct((nb, S, c), BF16), (S // tm, nb, 1), (tm, c), False, True, scale, ex=ex)


def _ffn_dwin(x, dh8, name, ex=None):
    S, D = x.shape
    n, _, c = dh8.shape
    tk, tm = _tile(S, 1024, SUBLANES), _tile(D, 1024)
    return _mm_core(name, x, dh8, pl.BlockSpec((tk, tm), lambda i, j, k: (k, j)),
                    pl.BlockSpec((None, tk, c), lambda i, j, k: (i, k, 0)),
                    pl.BlockSpec((None, tm, c), lambda i, j, k: (i, j, 0)),
                    jax.ShapeDtypeStruct((n, D, c), BF16), (n, D // tm, S // tk), (tm, c), True, False, ex=ex)


def _ffn_dx(dh8, wg, dz, alpha, name, ex=None):
    n, S, c = dh8.shape
    D = wg.shape[1]
    tm, tn = _tile(S, 1024, SUBLANES), _tile(D, 1024)
    return _mm_core(name, dh8, wg, pl.BlockSpec((None, tm, c), lambda i, j, k: (k, i, 0)),
                    pl.BlockSpec((None, tn, c), lambda i, j, k: (k, j, 0)),
                    pl.BlockSpec((tm, tn), lambda i, j, k: (i, j)),
                    jax.ShapeDtypeStruct((S, D), F32), (S // tm, D // tn, n), (tm, tn), False, True,
                    None, dz, alpha, ex=ex)


def _res_ln_fwd(x, y, g, b, alpha, s, name):
    S, D = x.shape
    tm = _tile(S, 256, SUBLANES)

    def body(x_ref, y_ref, g_ref, b_ref, o_ref, xh_ref, r_ref):
        z = alpha * x_ref[...] + s * y_ref[...]
        mu = jnp.mean(z, axis=-1, keepdims=True)
        zc = z - mu
        var = jnp.mean(zc * zc, axis=-1, keepdims=True)
        rstd = lax.rsqrt(var + LN_EPS)
        xh = zc * rstd
        xh_ref[...] = xh
        r_ref[...] = rstd
        o_ref[...] = xh * g_ref[...] + b_ref[...]

    row = pl.BlockSpec((tm, D), lambda i: (i, 0))
    vec = pl.BlockSpec((1, D), lambda i: (0, 0))
    return pl.pallas_call(
        body, name=name,
        out_shape=(jax.ShapeDtypeStruct((S, D), F32), jax.ShapeDtypeStruct((S, D), F32),
                   jax.ShapeDtypeStruct((S, 1), F32)),
        grid=(S // tm,), in_specs=[row, row, vec, vec],
        out_specs=(row, row, pl.BlockSpec((tm, 1), lambda i: (i, 0))),
        compiler_params=_params(("parallel",)),
    )(x, y, g.reshape(1, D), b.reshape(1, D))


def _ln_bwd(dout, xh, rstd, g, name):
    S, D = dout.shape
    tm = _tile(S, 256, SUBLANES)

    def body(d_ref, xh_ref, r_ref, g_ref, dz_ref, dg_ref, db_ref):
        i = pl.program_id(0)

        @pl.when(i == 0)
        def _():
            dg_ref[...] = jnp.zeros_like(dg_ref)
            db_ref[...] = jnp.zeros_like(db_ref)

        d = d_ref[...]
        xhv = xh_ref[...]
        dxh = d * g_ref[...]
        m1 = jnp.mean(dxh, axis=-1, keepdims=True)
        m2 = jnp.mean(dxh * xhv, axis=-1, keepdims=True)
        dz_ref[...] = r_ref[...] * (dxh - m1 - xhv * m2)
        dg_ref[...] += jnp.sum(d * xhv, axis=0, keepdims=True)
        db_ref[...] += jnp.sum(d, axis=0, keepdims=True)

    row = pl.BlockSpec((tm, D), lambda i: (i, 0))
    vec = pl.BlockSpec((1, D), lambda i: (0, 0))
    dz, dg, db = pl.pallas_call(
        body, name=name,
        out_shape=(jax.ShapeDtypeStruct((S, D), F32), jax.ShapeDtypeStruct((1, D), F32),
                   jax.ShapeDtypeStruct((1, D), F32)),
        grid=(S // tm,), in_specs=[row, row, pl.BlockSpec((tm, 1), lambda i: (i, 0)), vec],
        out_specs=(row, vec, vec),
        compiler_params=_params(("arbitrary",)),
    )(dout, xh, rstd, g.reshape(1, D))
    return dz, dg[0], db[0]


def _sigmoid(x):
    e = jnp.exp(-jnp.abs(x))
    return jnp.where(x >= 0, 1.0 / (1.0 + e), e / (1.0 + e))


def _swiglu_fwd(h8, name):
    n, S, c = h8.shape
    nb = n // 2
    tm = _tile(S, 512, SUBLANES)

    def body(h_ref, o_ref):
        g = h_ref[0].astype(F32)
        u = h_ref[1].astype(F32)
        o_ref[...] = (g * _sigmoid(g) * u).astype(BF16)

    return pl.pallas_call(
        body, name=name, out_shape=jax.ShapeDtypeStruct((nb, S, c), BF16), grid=(nb, S // tm),
        in_specs=[pl.BlockSpec((2, None, tm, c), lambda k, i: (0, k, i, 0))],
        out_specs=pl.BlockSpec((None, tm, c), lambda k, i: (k, i, 0)),
        compiler_params=_params(("parallel", "parallel")),
    )(h8.reshape(2, nb, S, c))


def _swiglu_bwd(h8, da4, name):
    n, S, c = h8.shape
    nb = n // 2
    tm = _tile(S, 512, SUBLANES)

    def body(h_ref, da_ref, d_ref):
        g = h_ref[0].astype(F32)
        u = h_ref[1].astype(F32)
        da_v = da_ref[...].astype(F32)
        sg = _sigmoid(g)
        silu = g * sg
        d_ref[0] = (da_v * u * (sg + silu * (1.0 - sg))).astype(BF16)
        d_ref[1] = (da_v * silu).astype(BF16)

    pair = pl.BlockSpec((2, None, tm, c), lambda k, i: (0, k, i, 0))
    dh = pl.pallas_call(
        body, name=name, out_shape=jax.ShapeDtypeStruct((2, nb, S, c), BF16), grid=(nb, S // tm),
        in_specs=[pair, pl.BlockSpec((None, tm, c), lambda k, i: (k, i, 0))], out_specs=pair,
        compiler_params=_params(("parallel", "parallel")),
    )(h8.reshape(2, nb, S, c), da4)
    return dh.reshape(n, S, c)


def _glu_fwd(vg, name):
    S, D2 = vg.shape
    D = D2 // 2
    tm = _tile(S, 512, SUBLANES)

    def body(v_ref, g_ref, o_ref):
        o_ref[...] = v_ref[...] * _sigmoid(g_ref[...])

    return pl.pallas_call(
        body, name=name, out_shape=jax.ShapeDtypeStruct((S, D), F32), grid=(S // tm,),
        in_specs=[pl.BlockSpec((tm, D), lambda i: (i, 0)), pl.BlockSpec((tm, D), lambda i: (i, 1))],
        out_specs=pl.BlockSpec((tm, D), lambda i: (i, 0)),
        compiler_params=_params(("parallel",)),
    )(vg, vg)


def _glu_bwd(vg, dm, name):
    S, D2 = vg.shape
    D = D2 // 2
    tm = _tile(S, 512, SUBLANES)

    def body(v_ref, g_ref, dm_ref, dv_ref, dg_ref):
        sg = _sigmoid(g_ref[...])
        d = dm_ref[...]
        dv_ref[...] = (d * sg).astype(BF16)
        dg_ref[...] = (d * v_ref[...] * sg * (1.0 - sg)).astype(BF16)

    blk = pl.BlockSpec((tm, D), lambda i: (i, 0))
    dv, dg = pl.pallas_call(
        body, name=name,
        out_shape=(jax.ShapeDtypeStruct((S, D), BF16), jax.ShapeDtypeStruct((S, D), BF16)),
        grid=(S // tm,), in_specs=[blk, pl.BlockSpec((tm, D), lambda i: (i, 1)), blk],
        out_specs=(blk, blk), compiler_params=_params(("parallel",)),
    )(vg, vg, dm)
    return jnp.concatenate([dv, dg], axis=1)


def _loss_fwd_bwd(y, target):
    S, D = y.shape
    tm = _tile(S, 256, SUBLANES)

    def body(y_ref, t_ref, dy_ref, l_ref):
        i = pl.program_id(0)

        @pl.when(i == 0)
        def _():
            l_ref[...] = jnp.zeros_like(l_ref)

        e = y_ref[...] - t_ref[...]
        dy_ref[...] = e * (1.0 / D)
        l_ref[...] += jnp.sum(e * e, axis=0, keepdims=True) * (0.5 / D)

    row = pl.BlockSpec((tm, D), lambda i: (i, 0))
    dy, part = pl.pallas_call(
        body, name="loss", out_shape=(jax.ShapeDtypeStruct((S, D), F32), jax.ShapeDtypeStruct((1, D), F32)),
        grid=(S // tm,), in_specs=[row, row], out_specs=(row, pl.BlockSpec((1, D), lambda i: (0, 0))),
        compiler_params=_params(("arbitrary",)),
    )(y, target)
    return dy, jnp.sum(part)


def _tri(n, lower):
    r = lax.broadcasted_iota(jnp.int32, (n, n), 0)
    c = lax.broadcasted_iota(jnp.int32, (n, n), 1)
    return jnp.where((c <= r) if lower else (c >= r), 1.0, 0.0)


def _fox_gate_fwd(fl, bf):
    S, W = fl.shape
    tm = _tile(S, 256, SUBLANES)

    def body(fl_ref, b_ref, c_ref, carry):
        i = pl.program_id(0)

        @pl.when(i == 0)
        def _():
            carry[...] = jnp.zeros_like(carry)

        x = fl_ref[...] + b_ref[...]
        lf = jnp.minimum(x, 0.0) - jnp.log(1.0 + jnp.exp(-jnp.abs(x)))
        c_ref[...] = jnp.dot(_tri(tm, True), lf, precision=lax.Precision.HIGHEST,
                             preferred_element_type=F32) + carry[...]
        carry[...] += jnp.sum(lf, axis=0, keepdims=True)

    blk = pl.BlockSpec((tm, W), lambda i: (i, 0))
    return pl.pallas_call(
        body, name="fox_gate_fwd", out_shape=jax.ShapeDtypeStruct((S, W), F32), grid=(S // tm,),
        in_specs=[blk, pl.BlockSpec((1, W), lambda i: (0, 0))], out_specs=blk,
        scratch_shapes=[pltpu.VMEM((1, W), F32)], compiler_params=_params(("arbitrary",)),
    )(fl, bf)


def _fox_gate_bwd(dcum, fl, bf):
    S, W = fl.shape
    tm = _tile(S, 256, SUBLANES)
    nb = S // tm

    def body(dc_ref, fl_ref, b_ref, dfl_ref, db_ref, carry):
        i = pl.program_id(0)

        @pl.when(i == 0)
        def _():
            carry[...] = jnp.zeros_like(carry)
            db_ref[...] = jnp.zeros_like(db_ref)

        dc = dc_ref[...]
        r = jnp.dot(_tri(tm, False), dc, precision=lax.Precision.HIGHEST, preferred_element_type=F32) + carry[...]
        carry[...] += jnp.sum(dc, axis=0, keepdims=True)
        x = fl_ref[...] + b_ref[...]
        dfl = r * (1.0 - _sigmoid(x))
        dfl_ref[...] = dfl
        db_ref[...] += jnp.sum(dfl, axis=0, keepdims=True)

    blk = pl.BlockSpec((tm, W), lambda i: (nb - 1 - i, 0))
    vec = pl.BlockSpec((1, W), lambda i: (0, 0))
    return pl.pallas_call(
        body, name="fox_gate_bwd",
        out_shape=(jax.ShapeDtypeStruct((S, W), F32), jax.ShapeDtypeStruct((1, W), F32)),
        grid=(nb,), in_specs=[blk, blk, vec], out_specs=(blk, vec),
        scratch_shapes=[pltpu.VMEM((1, W), F32)], compiler_params=_params(("arbitrary",)),
    )(dcum, fl, bf)


def _causal(t):
    row = lax.broadcasted_iota(jnp.int32, (t, t), 0)
    col = lax.broadcasted_iota(jnp.int32, (t, t), 1)
    return col <= row


def _first_head(shape):
    return lax.broadcasted_iota(jnp.int32, shape, len(shape) - 1) < HEAD_DIM


def _flash_fwd(proj, cq, ck, ex=None):
    S = proj.shape[0]
    D = (proj.shape[1] - LANES) // 3
    HP = D // LANES
    t = _tile(S, 512)
    nq = S // t
    scale = 1.0 / math.sqrt(HEAD_DIM)

    def body(q_ref, k_ref, v_ref, cq_ref, ck_ref, o_ref, lse_ref, kb, vb):
        qi = pl.program_id(1)

        @pl.when(qi == 0)
        def _():
            kb[...] = k_ref[...].astype(BF16)
            vb[...] = v_ref[...].astype(BF16)

        first = _first_head((t, LANES))
        qf = q_ref[...] * scale
        qs_ = (jnp.where(first, qf, 0.0).astype(BF16), jnp.where(first, 0.0, qf).astype(BF16))
        cqv = cq_ref[0]

        def block(ki, carry, masked):
            m_old, l_old, acc = carry
            sl = pl.ds(pl.multiple_of(ki * t, t), t)
            kk, vv = kb[sl, :], vb[sl, :]
            m_new, l_new, corr, pv = [], [], [], []
            for a in range(2):
                s = lax.dot_general(qs_[a], kk, (((1,), (1,)), ((), ())), preferred_element_type=F32)
                s = s + (cqv[:, a:a + 1] - ck_ref[a, ki])
                if masked:
                    s = jnp.where(_causal(t), s, NEG_INF)
                m_a = jnp.maximum(m_old[a], jnp.max(s, axis=1, keepdims=True))
                p = jnp.exp(s - m_a)
                c_a = jnp.exp(m_old[a] - m_a)
                m_new.append(m_a)
                corr.append(c_a)
                l_new.append(c_a * l_old[a] + jnp.sum(p, axis=1, keepdims=True))
                pv.append(jnp.dot(p.astype(BF16), vv, preferred_element_type=F32))
            acc = jnp.where(first, corr[0] * acc + pv[0], corr[1] * acc + pv[1])
            return tuple(m_new), tuple(l_new), acc

        neg = jnp.full((t, 1), NEG_INF, F32)
        zero = jnp.zeros((t, 1), F32)
        carry = lax.fori_loop(0, qi, lambda ki, c: block(ki, c, False),
                              ((neg, neg), (zero, zero), jnp.zeros((t, LANES), F32)))
        m, l, acc = block(qi, carry, True)
        o_ref[...] = acc / jnp.where(first, l[0], l[1])
        lse_ref[0, :, 0:1] = m[0] + jnp.log(l[0])
        lse_ref[0, :, 1:2] = m[1] + jnp.log(l[1])

    qblk = pl.BlockSpec((t, LANES), lambda h, i: (i, h))
    r2 = pl.BlockSpec((1, t, 2), lambda h, i: (h, i, 0))
    (o, lse), extra = _call(
        body, name="fox_attn_fwd",
        out_shape=(jax.ShapeDtypeStruct((S, D), F32), jax.ShapeDtypeStruct((HP, S, 2), F32)),
        grid=(HP, nq),
        in_specs=[qblk, pl.BlockSpec((S, LANES), lambda h, i: (0, HP + h)),
                  pl.BlockSpec((S, LANES), lambda h, i: (0, 2 * HP + h)), r2,
                  pl.BlockSpec((2, nq, 1, t), lambda h, i: (h, 0, 0, 0))],
        out_specs=(qblk, r2), scratch_shapes=[pltpu.VMEM((S, LANES), BF16), pltpu.VMEM((S, LANES), BF16)],
        args=(proj, proj, proj, cq, ck), semantics=("parallel", "arbitrary"), ex=ex)
    return o, lse, extra


def _flash_bwd(proj, rows, ck, o, do, ex=None):
    S = proj.shape[0]
    D = (proj.shape[1] - LANES) // 3
    HP = D // LANES
    t = _tile(S, 512)
    nb = S // t
    scale = 1.0 / math.sqrt(HEAD_DIM)

    def body(q_ref, k_ref, v_ref, rows_ref, ck_ref, o_ref, do_ref, dq_ref, dk_ref, dv_ref, dcq_ref, dck_ref,
             q_s, do_s, dl_s, dk_acc, dv_acc, dc_acc):
        kb = pl.program_id(1)

        @pl.when(kb == 0)
        def _():
            dq_ref[...] = jnp.zeros_like(dq_ref)
            dcq_ref[...] = jnp.zeros_like(dcq_ref)

            def prep(qb, c):
                sl = pl.ds(pl.multiple_of(qb * t, t), t)
                first = _first_head((t, LANES))
                qf = q_ref[sl, :] * scale
                dof = do_ref[sl, :]
                prod = dof * o_ref[sl, :]
                for a, keep in enumerate((first, jnp.logical_not(first))):
                    q_s[a, sl, :] = jnp.where(keep, qf, 0.0).astype(BF16)
                    do_s[a, sl, :] = jnp.where(keep, dof, 0.0).astype(BF16)
                    dl_s[sl, a:a + 1] = jnp.sum(jnp.where(keep, prod, 0.0), axis=1, keepdims=True)
                return c

            lax.fori_loop(0, nb, prep, 0)

        dk_acc[...] = jnp.zeros_like(dk_acc)
        dv_acc[...] = jnp.zeros_like(dv_acc)
        dc_acc[...] = jnp.zeros_like(dc_acc)
        first = _first_head((t, LANES))
        kf = k_ref[...]
        kk = kf.astype(BF16)
        k_own = (jnp.where(first, kf, 0.0).astype(BF16), jnp.where(first, 0.0, kf).astype(BF16))
        vv = v_ref[...].astype(BF16)

        def block(qb, masked):
            sl = pl.ds(pl.multiple_of(qb * t, t), t)
            rv = rows_ref[0, sl, :]
            dlv = dl_s[sl, :]
            dq_new = dq_ref[sl, :]
            for a in range(2):
                qv = q_s[a, sl, :]
                dob = do_s[a, sl, :]
                s = lax.dot_general(qv, kk, (((1,), (1,)), ((), ())), preferred_element_type=F32)
                s = s + (rv[:, a:a + 1] - ck_ref[a, 0])
                if masked:
                    s = jnp.where(_causal(t), s, NEG_INF)
                p = jnp.exp(s - rv[:, 2 + a:3 + a])
                dv_acc[...] += lax.dot_general(p.astype(BF16), dob, (((0,), (0,)), ((), ())),
                                               preferred_element_type=F32)
                dp = lax.dot_general(dob, vv, (((1,), (1,)), ((), ())), preferred_element_type=F32)
                ds = p * (dp - dlv[:, a:a + 1])
                dsb = ds.astype(BF16)
                dk_acc[...] += lax.dot_general(dsb, qv, (((0,), (0,)), ((), ())), preferred_element_type=F32)
                dq_new = dq_new + jnp.dot(dsb, k_own[a], preferred_element_type=F32) * scale
                dcq_ref[0, sl, a:a + 1] += jnp.sum(ds, axis=1, keepdims=True)
                dc_acc[a:a + 1, :] -= jnp.sum(ds, axis=0, keepdims=True)
            dq_ref[sl, :] = dq_new

        block(kb, True)

        def rest(qb, c):
            block(qb, False)
            return c

        lax.fori_loop(kb + 1, nb, rest, 0)
        dk_ref[...] = dk_acc[...]
        dv_ref[...] = dv_acc[...]
        dck_ref[0, 0] = dc_acc[0:1, :]
        dck_ref[1, 0] = dc_acc[1:2, :]

    full = lambda c0: pl.BlockSpec((S, LANES), lambda h, j, c0=c0: (0, c0 + h))
    blk = lambda c0: pl.BlockSpec((t, LANES), lambda h, j, c0=c0: (j, c0 + h))
    ck_s = pl.BlockSpec((2, 1, 1, t), lambda h, j: (h, j, 0, 0))
    f32 = lambda *s: jax.ShapeDtypeStruct(s, F32)
    outs, extra = _call(
        body, name="fox_attn_bwd",
        out_shape=(f32(S, D), f32(S, D), f32(S, D), f32(HP, S, 2), f32(2 * HP, nb, 1, t)),
        grid=(HP, nb),
        in_specs=[full(0), blk(HP), blk(2 * HP), pl.BlockSpec((1, S, 4), lambda h, j: (h, 0, 0)), ck_s,
                  full(0), full(0)],
        out_specs=(full(0), blk(0), blk(0), pl.BlockSpec((1, S, 2), lambda h, j: (h, 0, 0)), ck_s),
        scratch_shapes=[pltpu.VMEM((2, S, LANES), BF16), pltpu.VMEM((2, S, LANES), BF16),
                        pltpu.VMEM((S, 2), F32), pltpu.VMEM((t, LANES), F32), pltpu.VMEM((t, LANES), F32),
                        pltpu.VMEM((2, t), F32)],
        args=(proj, proj, proj, rows, ck, o, do), semantics=("parallel", "arbitrary"), ex=ex)
    return (*outs, extra)


def _s5_consts(T):
    rows = T * SUBLANES
    rr = lax.broadcasted_iota(jnp.int32, (rows, T), 0)
    tt = lax.broadcasted_iota(jnp.int32, (rows, T), 1)
    rep = jnp.where(rr // SUBLANES == tt, 1.0, 0.0).astype(BF16)
    r2 = lax.broadcasted_iota(jnp.int32, (rows, S5_PART), 0)
    c2 = lax.broadcasted_iota(jnp.int32, (rows, S5_PART), 1)
    mask = (c2 // (S5_PART // SUBLANES)) == (r2 % SUBLANES)
    return rep, mask


def _gelu(y):
    c = math.sqrt(2.0 / math.pi)
    return 0.5 * y * (1.0 + jnp.tanh(c * (y + 0.044715 * y * y * y)))


def _gelu_grad(y):
    c = math.sqrt(2.0 / math.pi)
    th = jnp.tanh(c * (y + 0.044715 * y * y * y))
    return 0.5 * (1.0 + th) + 0.5 * y * (1.0 - th * th) * c * (1.0 + 3.0 * 0.044715 * y * y)


def _s5_fwd(x, rmat, cmat, lam, dskip, ex=None):
    S, D = x.shape
    NQ = D // S5_PART
    T = _tile(S, 128, SUBLANES)
    rows = T * SUBLANES

    def body(x_ref, r_ref, c_ref, lam_ref, d_ref, y_ref, yg_ref, h_ref, bu_s, carry):
        i = pl.program_id(0)

        @pl.when(i == 0)
        def _():
            carry[...] = jnp.zeros_like(carry)

        rep, mask = _s5_consts(T)
        cols = [pl.ds(q * S5_PART, S5_PART) for q in range(NQ)]
        for q in range(NQ):
            xrep = jnp.dot(rep, x_ref[:, cols[q]].astype(BF16), preferred_element_type=F32)
            lx = jnp.where(mask, xrep, 0.0).astype(BF16)
            bu_s[q] = jnp.dot(lx, r_ref[q], preferred_element_type=F32)
        lam_v = [(lam_ref[q, :, 0:LANES], lam_ref[q, :, LANES:2 * LANES]) for q in range(NQ)]

        def step(t, c):
            o = pl.multiple_of(t * SUBLANES, SUBLANES)
            new = []
            for q in range(NQ):
                hr, hi = c[q]
                ar, ai = lam_v[q]
                sl = bu_s[q, pl.ds(o, SUBLANES), :]
                nhr = ar * hr - ai * hi + sl[:, 0:LANES]
                nhi = ar * hi + ai * hr + sl[:, LANES:2 * LANES]
                h_ref[q, pl.ds(o, SUBLANES), 0:LANES] = nhr
                h_ref[q, pl.ds(o, SUBLANES), LANES:2 * LANES] = nhi
                new.append((nhr, nhi))
            return tuple(new)

        fin = lax.fori_loop(0, T, step,
                            tuple((carry[q, :, 0:LANES], carry[q, :, LANES:2 * LANES]) for q in range(NQ)))
        for q in range(NQ):
            carry[q, :, 0:LANES] = fin[q][0]
            carry[q, :, LANES:2 * LANES] = fin[q][1]
            z = jnp.dot(h_ref[q].astype(BF16), c_ref[q], preferred_element_type=F32)
            z = jnp.where(mask, z, 0.0)
            y = jnp.sum(z.reshape(T, SUBLANES, S5_PART), axis=1) + d_ref[:, cols[q]] * x_ref[:, cols[q]]
            y_ref[:, cols[q]] = y
            yg_ref[:, cols[q]] = _gelu(y).astype(BF16)

    xs = pl.BlockSpec((T, D), lambda i: (i, 0))
    ms = pl.BlockSpec((NQ, S5_PART, S5_PART), lambda i: (0, 0, 0))
    outs, extra = _call(
        body, name="s5_scan_fwd",
        out_shape=(jax.ShapeDtypeStruct((S, D), F32), jax.ShapeDtypeStruct((S, D), BF16),
                   jax.ShapeDtypeStruct((NQ, S * SUBLANES, S5_PART), F32)),
        grid=(S // T,),
        in_specs=[xs, ms, ms, pl.BlockSpec((NQ, SUBLANES, S5_PART), lambda i: (0, 0, 0)),
                  pl.BlockSpec((1, D), lambda i: (0, 0))],
        out_specs=(xs, xs, pl.BlockSpec((NQ, rows, S5_PART), lambda i: (0, i, 0))),
        scratch_shapes=[pltpu.VMEM((NQ, rows, S5_PART), F32), pltpu.VMEM((NQ, SUBLANES, S5_PART), F32)],
        args=(x, rmat, cmat, lam, dskip), semantics=("arbitrary",), ex=ex)
    return (*outs, extra)


def _s5_bwd(x, y, dyg, hs, rmat, cmat, lam, dskip, res, res_scale, ex=None):
    S, D = x.shape
    NQ = D // S5_PART
    T = _tile(S, 128, SUBLANES)
    nb = S // T
    rows = T * SUBLANES

    def body(x_ref, y_ref, dyg_ref, res_ref, h_ref, hp_ref, r_ref, c_ref, lam_ref, d_ref,
             dx_ref, dr_ref, dc_ref, dlam_ref, dd_ref, dh_s, g_s, hs_s, carry):
        i = pl.program_id(0)

        @pl.when(i == 0)
        def _():
            carry[...] = jnp.zeros_like(carry)
            dr_ref[...] = jnp.zeros_like(dr_ref)
            dc_ref[...] = jnp.zeros_like(dc_ref)
            dlam_ref[...] = jnp.zeros_like(dlam_ref)
            dd_ref[...] = jnp.zeros_like(dd_ref)

        rep, mask = _s5_consts(T)
        cols = [pl.ds(q * S5_PART, S5_PART) for q in range(NQ)]
        dys, ldys = [], []
        for q in range(NQ):
            dy = dyg_ref[:, cols[q]] * _gelu_grad(y_ref[:, cols[q]])
            dyrep = jnp.dot(rep, dy.astype(BF16), preferred_element_type=F32)
            ldy = jnp.where(mask, dyrep, 0.0).astype(BF16)
            dh_s[q] = lax.dot_general(ldy, c_ref[q], (((1,), (1,)), ((), ())), preferred_element_type=F32)
            dys.append(dy)
            ldys.append(ldy)
        lam_v = [(lam_ref[q, :, 0:LANES], lam_ref[q, :, LANES:2 * LANES]) for q in range(NQ)]

        def step(n, c):
            o = pl.multiple_of((T - 1 - n) * SUBLANES, SUBLANES)
            new = []
            for q in range(NQ):
                gr, gi = c[q]
                ar, ai = lam_v[q]
                sl = dh_s[q, pl.ds(o, SUBLANES), :]
                ngr = sl[:, 0:LANES] + ar * gr + ai * gi
                ngi = sl[:, LANES:2 * LANES] - ai * gr + ar * gi
                g_s[q, pl.ds(o, SUBLANES), 0:LANES] = ngr
                g_s[q, pl.ds(o, SUBLANES), LANES:2 * LANES] = ngi
                new.append((ngr, ngi))
            return tuple(new)

        fin = lax.fori_loop(0, T, step,
                            tuple((carry[q, :, 0:LANES], carry[q, :, LANES:2 * LANES]) for q in range(NQ)))
        for q in range(NQ):
            carry[q, :, 0:LANES] = fin[q][0]
            carry[q, :, LANES:2 * LANES] = fin[q][1]
            xv = x_ref[:, cols[q]]
            hv = h_ref[q]
            hs_s[0:SUBLANES, :] = jnp.where(i == nb - 1, 0.0, hp_ref[q])
            hs_s[SUBLANES:rows + SUBLANES, :] = hv
            hprev = hs_s[0:rows, :]
            gv = g_s[q]
            g_re, g_im = gv[:, 0:LANES], gv[:, LANES:2 * LANES]
            hp_re, hp_im = hprev[:, 0:LANES], hprev[:, LANES:2 * LANES]
            dar = jnp.sum((g_re * hp_re + g_im * hp_im).reshape(T, SUBLANES, LANES), axis=0)
            dai = jnp.sum((g_im * hp_re - g_re * hp_im).reshape(T, SUBLANES, LANES), axis=0)
            dlam_ref[q, :, 0:LANES] += dar
            dlam_ref[q, :, LANES:2 * LANES] += dai

            gb = gv.astype(BF16)
            xrep = jnp.dot(rep, xv.astype(BF16), preferred_element_type=F32)
            lx = jnp.where(mask, xrep, 0.0).astype(BF16)
            dr_ref[q] += lax.dot_general(lx, gb, (((0,), (0,)), ((), ())), preferred_element_type=F32)
            dc_ref[q] += lax.dot_general(hv.astype(BF16), ldys[q], (((0,), (0,)), ((), ())),
                                         preferred_element_type=F32)
            zx = lax.dot_general(gb, r_ref[q], (((1,), (1,)), ((), ())), preferred_element_type=F32)
            zx = jnp.where(mask, zx, 0.0)
            dx_ref[:, cols[q]] = (jnp.sum(zx.reshape(T, SUBLANES, S5_PART), axis=1) + d_ref[:, cols[q]] * dys[q]
                                  + res_scale * res_ref[:, cols[q]])
            dd_ref[:, cols[q]] += jnp.sum(dys[q] * xv, axis=0, keepdims=True)

    xs = pl.BlockSpec((T, D), lambda i: (nb - 1 - i, 0))
    ms = pl.BlockSpec((NQ, S5_PART, S5_PART), lambda i: (0, 0, 0))
    ls = pl.BlockSpec((NQ, SUBLANES, S5_PART), lambda i: (0, 0, 0))
    ds_ = pl.BlockSpec((1, D), lambda i: (0, 0))
    outs, extra = _call(
        body, name="s5_scan_bwd",
        out_shape=(jax.ShapeDtypeStruct((S, D), F32), jax.ShapeDtypeStruct((NQ, S5_PART, S5_PART), F32),
                   jax.ShapeDtypeStruct((NQ, S5_PART, S5_PART), F32),
                   jax.ShapeDtypeStruct((NQ, SUBLANES, S5_PART), F32), jax.ShapeDtypeStruct((1, D), F32)),
        grid=(nb,),
        in_specs=[xs, xs, xs, xs, pl.BlockSpec((NQ, rows, S5_PART), lambda i: (0, nb - 1 - i, 0)),
                  pl.BlockSpec((NQ, SUBLANES, S5_PART), lambda i: (0, jnp.maximum((nb - 1 - i) * T - 1, 0), 0)),
                  ms, ms, ls, ds_],
        out_specs=(xs, ms, ms, ls, ds_),
        scratch_shapes=[pltpu.VMEM((NQ, rows, S5_PART), F32), pltpu.VMEM((NQ, rows, S5_PART), F32),
                        pltpu.VMEM((rows + SUBLANES, S5_PART), F32), pltpu.VMEM((NQ, SUBLANES, S5_PART), F32)],
        args=(x, y, dyg, res, hs, hs, rmat, cmat, lam, dskip), semantics=("arbitrary",), ex=ex)
    return (*outs, extra)


def _s5_discretise(a_re, a_im, log_dt, b_re, b_im):
    dt = jnp.exp(log_dt)[:, None]
    mag = jnp.exp(a_re * dt)
    ang = a_im * dt
    lb_re = mag * jnp.cos(ang)
    lb_im = mag * jnp.sin(ang)
    den = a_re * a_re + a_im * a_im
    nr = lb_re - 1.0
    ni = lb_im
    z_re = (nr * a_re + ni * a_im) / den
    z_im = (ni * a_re - nr * a_im) / den
    bb_re = z_re[..., None] * b_re - z_im[..., None] * b_im
    bb_im = z_re[..., None] * b_im + z_im[..., None] * b_re
    return lb_re, lb_im, bb_re, bb_im


def _s5_expand(w):
    G = w.shape[0]
    NQ = G // 16
    base = w.reshape(NQ, S5_PART, S5_STATE)
    half = (jnp.arange(S5_PART) // S5_GROUP) % 2
    sel = (half[:, None] == jnp.arange(2)[None, :]).astype(w.dtype)
    out = base[:, :, None, :] * sel[None, :, :, None]
    return out.reshape(NQ, S5_PART, 2 * S5_STATE)


def _s5_extract(m):
    NQ = m.shape[0]
    half = (jnp.arange(S5_PART) // S5_GROUP) % 2
    sel = (half[:, None] == jnp.arange(2)[None, :]).astype(m.dtype)
    base = jnp.sum(m.reshape(NQ, S5_PART, 2, S5_STATE) * sel[None, :, :, None], axis=2)
    return base.reshape(NQ * 16, S5_GROUP, S5_STATE)


def _s5_slab(v):
    return v.reshape(v.shape[0] // 16, SUBLANES, LANES)


def _adamw(w, g, m, v, name):
    R, C = w.shape
    tr = _tile(R, 256, SUBLANES)
    c1 = 1.0 / (1.0 - ADAM_B1 ** ADAM_STEP)
    c2 = 1.0 / (1.0 - ADAM_B2 ** ADAM_STEP)

    def body(w_ref, g_ref, m_ref, v_ref, d_ref, nm_ref, nv_ref):
        gv = g_ref[...]
        nm = ADAM_B1 * m_ref[...] + (1.0 - ADAM_B1) * gv
        nv = ADAM_B2 * v_ref[...] + (1.0 - ADAM_B2) * (gv * gv)
        nm_ref[...] = nm
        nv_ref[...] = nv
        d_ref[...] = -ADAM_LR * ((nm * c1) / (jnp.sqrt(nv * c2) + ADAM_EPS) + ADAM_WD * w_ref[...])

    blk = pl.BlockSpec((tr, C), lambda i: (i, 0))
    sh = jax.ShapeDtypeStruct((R, C), F32)
    return pl.pallas_call(
        body, name=name, out_shape=(sh, sh, sh), grid=(R // tr,), in_specs=[blk] * 4, out_specs=(blk,) * 3,
        compiler_params=_params(("parallel",)),
    )(w, g, m, v)


def _adamw_recv(parts, w, m, v, name):
    L, R, C = w.shape
    tr = _tile(R, 256, SUBLANES)
    c1 = 1.0 / (1.0 - ADAM_B1 ** ADAM_STEP)
    c2 = 1.0 / (1.0 - ADAM_B2 ** ADAM_STEP)

    def body(*refs):
        p_refs = refs[:L]
        w_ref, m_ref, v_ref, g_ref, d_ref, nm_ref, nv_ref = refs[L:]
        li = pl.program_id(0)
        for l in range(L):
            @pl.when(li == l)
            def _(p_ref=p_refs[l]):
                gv = p_ref[0].astype(F32)
                for k in range(1, N_DEV):
                    gv = gv + p_ref[k].astype(F32)
                g_ref[...] = gv
                nm = ADAM_B1 * m_ref[...] + (1.0 - ADAM_B1) * gv
                nv = ADAM_B2 * v_ref[...] + (1.0 - ADAM_B2) * (gv * gv)
                nm_ref[...] = nm
                nv_ref[...] = nv
                d_ref[...] = -ADAM_LR * ((nm * c1) / (jnp.sqrt(nv * c2) + ADAM_EPS) + ADAM_WD * w_ref[...])

    p_specs = [pl.BlockSpec((N_DEV, tr, C), lambda li, i, l=l: (0, jnp.where(li == l, i, 0), 0)) for l in range(L)]
    blk = pl.BlockSpec((None, tr, C), lambda li, i: (li, i, 0))
    sh = jax.ShapeDtypeStruct((L, R, C), F32)
    return pl.pallas_call(
        body, name=name, out_shape=(sh, sh, sh, sh), grid=(L, R // tr),
        in_specs=p_specs + [blk, blk, blk], out_specs=(blk,) * 4,
        compiler_params=_params(("parallel", "parallel")),
    )(*parts, w, m, v)


def _sum8(parts, name):
    _, R, C = parts.shape
    tr = _tile(R, 256, SUBLANES)

    def body(p_ref, o_ref):
        acc = p_ref[0].astype(F32)
        for k in range(1, N_DEV):
            acc = acc + p_ref[k].astype(F32)
        o_ref[...] = acc

    return pl.pallas_call(
        body, name=name, out_shape=jax.ShapeDtypeStruct((R, C), F32), grid=(R // tr,),
        in_specs=[pl.BlockSpec((N_DEV, tr, C), lambda i: (0, i, 0))],
        out_specs=pl.BlockSpec((tr, C), lambda i: (i, 0)), compiler_params=_params(("parallel",)),
    )(parts)


def _peers():
    x, y, c = lax.axis_index("x"), lax.axis_index("y"), lax.axis_index("c")
    me = 4 * x + 2 * y + c
    out = []
    for k in range(1, N_DEV):
        kx, ky, kc = (k >> 2) & 1, (k >> 1) & 1, k & 1
        px, py, pc = x ^ kx, y ^ ky, c ^ kc
        out.append(((px, py, pc), 4 * px + 2 * py + pc))
    return me, out


SIBLING = 1
SAME_CORE = (2, 4, 6)


class Exchange:
    def __init__(self, xs, scatter):
        self.xs = list(xs)
        self.scatter = list(scatter)
        self.n = len(self.xs)

    def out_shapes(self):
        return tuple(jax.ShapeDtypeStruct(x.shape if sc else (N_DEV,) + x.shape, x.dtype)
                     for x, sc in zip(self.xs, self.scatter))

    def sems(self):
        return [pltpu.SemaphoreType.DMA((self.n * N_DEV,)), pltpu.SemaphoreType.DMA((self.n * N_DEV,)),
                pltpu.SemaphoreType.DMA((self.n,))]

    def _copy(self, i, m, src, dst, to, send_sems, recv_sems):
        return pltpu.make_async_remote_copy(src_ref=src, dst_ref=dst, send_sem=send_sems.at[i * N_DEV + m],
                                            recv_sem=recv_sems.at[i * N_DEV + m], device_id=to, device_id_type=MESH)

    def start(self, x_refs, o_refs, send_sems, recv_sems, local_sems):
        me, peers = _peers()
        for i, (x_ref, o_ref, sc) in enumerate(zip(x_refs, o_refs, self.scatter)):
            pltpu.make_async_copy(x_ref.at[me] if sc else x_ref, o_ref.at[me], local_sems.at[i]).start()
            for m in (range(1, N_DEV) if sc else (SIBLING,) + SAME_CORE):
                dev, idx = peers[m - 1]
                self._copy(i, m, x_ref.at[idx] if sc else x_ref, o_ref.at[me], dev, send_sems, recv_sems).start()

    def middle(self, x_refs, o_refs, send_sems, recv_sems, local_sems):
        me, peers = _peers()
        sibling = peers[SIBLING - 1][0]
        for i, (x_ref, o_ref, sc) in enumerate(zip(x_refs, o_refs, self.scatter)):
            if sc:
                continue
            for m in SAME_CORE:
                dev, idx = peers[m - 1]
                self._copy(i, m, x_ref, o_ref.at[idx], dev, send_sems, recv_sems).wait_recv()
                self._copy(i, m ^ 1, o_ref.at[idx], o_ref.at[idx], sibling, send_sems, recv_sems).start()

    def wait(self, x_refs, o_refs, send_sems, recv_sems, local_sems):
        me, peers = _peers()
        for i, (x_ref, o_ref, sc) in enumerate(zip(x_refs, o_refs, self.scatter)):
            for m in range(1, N_DEV):
                dev, idx = peers[m - 1]
                cp = self._copy(i, m, x_ref.at[idx] if sc else x_ref, o_ref.at[idx], dev, send_sems, recv_sems)
                if sc or m not in SAME_CORE:
                    cp.wait_recv()
                cp.wait_send()
            pltpu.make_async_copy(x_ref.at[me] if sc else x_ref, o_ref.at[me], local_sems.at[i]).wait()


def _exchange(ex, name):
    n = ex.n

    def body(*refs):
        x_refs, o_refs, sems = refs[:n], refs[n:2 * n], refs[2 * n:]
        ex.start(x_refs, o_refs, *sems)
        ex.middle(x_refs, o_refs, *sems)
        ex.wait(x_refs, o_refs, *sems)

    hbm = pl.BlockSpec(memory_space=pltpu.HBM)
    return pl.pallas_call(body, name=name, out_shape=ex.out_shapes(), in_specs=[hbm] * n, out_specs=(hbm,) * n,
                          scratch_shapes=ex.sems())(*ex.xs)


def _call(body, *, name, out_shape, grid, in_specs, out_specs, scratch_shapes, args, semantics, ex=None):
    if ex is None:
        return pl.pallas_call(body, name=name, out_shape=out_shape, grid=grid, in_specs=in_specs, out_specs=out_specs,
                              scratch_shapes=scratch_shapes, compiler_params=_params(semantics))(*args), ()
    n, ni, no, ns = ex.n, len(args), len(out_shape), len(scratch_shapes)

    def wrapped(*refs):
        ins, cx = refs[:ni], refs[ni:ni + n]
        outs, co = refs[ni + n:ni + n + no], refs[ni + n + no:ni + 2 * n + no]
        scratch, sems = refs[ni + 2 * n + no:ni + 2 * n + no + ns], refs[ni + 2 * n + no + ns:]
        step = functools.reduce(lambda acc, a: acc * grid[a] + pl.program_id(a), range(len(grid)), 0)
        steps = math.prod(grid)

        @pl.when(step == 0)
        def _():
            ex.start(cx, co, *sems)

        body(*ins, *outs, *scratch)

        @pl.when(step == (steps * 3) // 4 - (steps > 1))
        def _():
            ex.middle(cx, co, *sems)

        @pl.when(step == steps - 1)
        def _():
            ex.wait(cx, co, *sems)

    hbm = pl.BlockSpec(memory_space=pltpu.HBM)
    res = pl.pallas_call(
        wrapped, name=name, out_shape=tuple(out_shape) + ex.out_shapes(), grid=grid,
        in_specs=list(in_specs) + [hbm] * n, out_specs=tuple(out_specs) + (hbm,) * n,
        scratch_shapes=list(scratch_shapes) + ex.sems(),
        compiler_params=_params(("arbitrary",) * len(grid)))(*args, *ex.xs)
    return res[:no], res[no:]


def _pack_flat(arrs):
    cat = jnp.concatenate([a.reshape(-1) for a in arrs])
    per = PACK_COLS * 2 * SUBLANES
    tot = -(-cat.shape[0] // per) * per
    return jnp.pad(cat, (0, tot - cat.shape[0])).reshape(tot // PACK_COLS, PACK_COLS)


def _unpack_flat(packed, shapes):
    flat = packed.reshape(-1)
    out, o = [], 0
    for s in shapes:
        n = math.prod(s)
        out.append(flat[o:o + n].reshape(s))
        o += n
    return out


def _ffn_fwd(x, wg, wo4, g, b, alpha, tag):
    h8 = _ffn_h(x, wg, tag + "_h")
    a4 = _swiglu_fwd(h8, tag + "_act")
    y = _ffn_y(a4, wo4, tag + "_y")
    out, xh, rstd = _res_ln_fwd(x, y, g, b, alpha, 0.5, tag + "_ln")
    return out, (x, h8, a4, xh, rstd)


def _ffn_bwd(dout, saved, wg, wo4, g, alpha, tag, carry=None):
    x, h8, a4, xh, rstd = saved
    jobs = sorted(carry or {}, key=lambda k: -carry[k].size)
    slots = [[], [], [], []]
    for n, k in enumerate(jobs):
        slots[min(n, 3)].append(k)
    exs = [Exchange([carry[k] for k in ks], [True] * len(ks)) if ks else None for ks in slots]
    arrived = {}

    def run(fn, slot, *args):
        if exs[slot] is None:
            return fn(*args)
        res, extra = fn(*args, ex=exs[slot])
        arrived.update(zip(slots[slot], extra))
        return res

    dz, dg, db = _ln_bwd(dout, xh, rstd, g, tag + "_ln_bwd")
    dw_out = run(_ffn_dwout, 3, a4, dz, 0.5, tag + "_dwout")
    da4 = run(_ffn_da, 2, dz, wo4, 0.5, tag + "_da")
    dh8 = _swiglu_bwd(h8, da4, tag + "_act_bwd")
    dw_in = run(_ffn_dwin, 0, x, dh8, tag + "_dwin")
    dx = run(_ffn_dx, 1, dh8, wg, dz, alpha, tag + "_dx")
    return dx, dw_in, dw_out.reshape(N_DEV, -1, dw_out.shape[-1]), dg, db, arrived


def _fox_fwd(x, w_in_pad, b_f_pad, w_o, g, b, alpha, tag, ex=None):
    S, D = x.shape
    H = D // HEAD_DIM
    proj = _mm(x, w_in_pad, name=tag + "_proj")
    fl = proj[:, 3 * D:]
    cum = _fox_gate_fwd(fl, b_f_pad)
    ch = cum[:, :H].T
    t = _tile(S, 512)
    cq = ch.reshape(H // 2, 2, S).transpose(0, 2, 1)
    ck = ch.reshape(H, S // t, 1, t)
    o, lse, extra = _flash_fwd(proj, cq, ck, ex)
    m = _mm(o, w_o, name=tag + "_out")
    out, xh, rstd = _res_ln_fwd(x, m, g, b, alpha, 1.0, tag + "_ln")
    return out, (x, proj, jnp.concatenate([cq, lse], axis=2), ck, o, fl, xh, rstd), extra


def _fox_bwd(dout, saved, w_in_pad, b_f_pad, w_o, g, alpha, tag, ex=None):
    x, proj, rows, ck, o, fl, xh, rstd = saved
    S, D = x.shape
    H = D // HEAD_DIM
    dz, dg, db = _ln_bwd(dout, xh, rstd, g, tag + "_ln_bwd")
    dw_o = _mm(o, dz, ta=True, name=tag + "_dwo")
    do = _mm(dz, w_o, tb=True, name=tag + "_do")
    dq, dk, dv, dcq, dck, extra = _flash_bwd(proj, rows, ck, o, do, ex)
    dcq = dcq.transpose(0, 2, 1).reshape(H, S)
    dcum = jnp.pad((dcq + dck.reshape(H, S)).T, ((0, 0), (0, LANES - H)))
    dfl, dbf = _fox_gate_bwd(dcum, fl, b_f_pad)
    dproj = jnp.concatenate([dq.astype(BF16), dk.astype(BF16), dv.astype(BF16), dfl.astype(BF16)], axis=1)
    dw_in = _mm(x, dproj, ta=True, name=tag + "_dwin")
    dx = _mm(dproj, w_in_pad, tb=True, add=dz, add_scale=alpha, name=tag + "_dx")
    shards = {"fox_w_in": _split(dw_in[None, :, :3 * D + H], True)[:, 0].astype(BF16),
              "fox_w_o": _split(dw_o[None], False)[:, 0].astype(BF16)}
    return dx, shards, {"fox_b_f": dbf[0, :H], "lnm_g": dg, "lnm_b": db}, extra


def _s5_mats(p):
    lb_re, lb_im, bb_re, bb_im = _s5_discretise(p["a_re"], p["a_im"], p["log_dt"], p["b_re"], p["b_im"])
    rmat = jnp.concatenate([_s5_expand(bb_re.transpose(0, 2, 1)), _s5_expand(bb_im.transpose(0, 2, 1))], axis=2)
    cmat = jnp.concatenate([_s5_expand(p["c_re"]).transpose(0, 2, 1), -_s5_expand(p["c_im"]).transpose(0, 2, 1)],
                           axis=1)
    lam = jnp.concatenate([_s5_slab(lb_re), _s5_slab(lb_im)], axis=2)
    return rmat.astype(BF16), cmat.astype(BF16), lam


def _s5_block_fwd(x, p, w_out, g, b, alpha, tag, ex=None):
    S, D = x.shape
    rmat, cmat, lam = _s5_mats(p)
    dskip = p["d"].reshape(1, D)
    y, yg, hs, extra = _s5_fwd(x, rmat, cmat, lam, dskip, ex)
    vg = _mm(yg, w_out, name=tag + "_vg")
    m = _glu_fwd(vg, tag + "_glu")
    out, xh, rstd = _res_ln_fwd(x, m, g, b, alpha, 1.0, tag + "_ln")
    return out, (x, y, yg, hs, vg, rmat, cmat, lam, dskip, xh, rstd), extra


def _s5_block_bwd(dout, saved, p, w_out, g, alpha, tag, ex=None):
    x, y, yg, hs, vg, rmat, cmat, lam, dskip, xh, rstd = saved
    S, D = x.shape
    G = D // S5_GROUP
    dz, dg, db = _ln_bwd(dout, xh, rstd, g, tag + "_ln_bwd")
    dvg = _glu_bwd(vg, dz, tag + "_glu_bwd")
    dw_out = _mm(yg, dvg, ta=True, name=tag + "_dwout")
    dyg = _mm(dvg, w_out, tb=True, name=tag + "_dyg")
    dx, dr, dc, dlam, dd, extra = _s5_bwd(x, y, dyg, hs, rmat, cmat, lam, dskip, dz, alpha, ex)
    dbb_re = _s5_extract(dr[:, :, :LANES]).transpose(0, 2, 1)
    dbb_im = _s5_extract(dr[:, :, LANES:]).transpose(0, 2, 1)
    dc_re = _s5_extract(dc[:, :LANES, :].transpose(0, 2, 1))
    dc_im = -_s5_extract(dc[:, LANES:, :].transpose(0, 2, 1))
    dlb_re = dlam[:, :, :LANES].reshape(G, S5_STATE)
    dlb_im = dlam[:, :, LANES:].reshape(G, S5_STATE)
    _, vjp = jax.vjp(_s5_discretise, p["a_re"], p["a_im"], p["log_dt"], p["b_re"], p["b_im"])
    da_re, da_im, dlog_dt, db_re, db_im = vjp((dlb_re, dlb_im, dbb_re, dbb_im))
    small = dict(s5_a_re=da_re, s5_a_im=da_im, s5_log_dt=dlog_dt, s5_b_re=db_re, s5_b_im=db_im, s5_c_re=dc_re,
                 s5_c_im=dc_im, s5_d=dd.reshape(G, S5_GROUP), lnm_g=dg, lnm_b=db)
    return dx, {"s5_w_out": _split(dw_out[None], True)[:, 0].astype(BF16)}, small, extra


FFN_NAMES = ("ffn1_w_in", "ffn1_w_out", "ffn2_w_in", "ffn2_w_out")
BIG = FFN_NAMES + ("fox_w_in", "fox_w_o", "s5_w_out")
BIG_SPLIT_COLS = {"ffn1_w_in": True, "ffn1_w_out": False, "ffn2_w_in": True, "ffn2_w_out": False,
                  "fox_w_in": True, "fox_w_o": False, "s5_w_out": True}
SMALL = ("ln1_g", "ln1_b", "lnm_g", "lnm_b", "ln2_g", "ln2_b", "fox_b_f", "s5_a_re", "s5_a_im", "s5_log_dt",
         "s5_b_re", "s5_b_im", "s5_c_re", "s5_c_im", "s5_d")
WEIGHTS = ("ffn1_w_in", "ffn1_w_out", "ln1_g", "ln1_b", "lnm_g", "lnm_b", "ffn2_w_in", "ffn2_w_out", "ln2_g", "ln2_b",
           "fox_w_in", "fox_b_f", "fox_w_o", "s5_a_re", "s5_a_im", "s5_log_dt", "s5_b_re", "s5_b_im", "s5_c_re",
           "s5_c_im", "s5_d", "s5_w_out")


def _join(gathered, split_cols):
    n, L, r, c = gathered.shape
    if split_cols:
        return gathered.transpose(1, 2, 0, 3).reshape(L, r, n * c)
    return gathered.transpose(1, 0, 2, 3).reshape(L, n * r, c)


def _split(full, split_cols):
    L, R, C = full.shape
    if split_cols:
        return full.reshape(L, R, N_DEV, C // N_DEV).transpose(2, 0, 1, 3)
    return full.reshape(L, N_DEV, R // N_DEV, C).transpose(1, 0, 2, 3)


def _group(i, part):
    if part == "b":
        return (("ffn2_w_in", i), ("ffn2_w_out", i))
    mixer = (("fox_w_in", i // 2), ("fox_w_o", i // 2)) if i % 2 == 0 else (("s5_w_out", i // 2),)
    return (("ffn1_w_in", i), ("ffn1_w_out", i)) + mixer


def _prepare(name, g8):
    if name in ("ffn1_w_in", "ffn2_w_in"):
        return g8
    if name in FFN_NAMES:
        n, r, c = g8.shape
        return g8.reshape(n // 2, 2 * r, c)
    full = _join(g8[:, None], BIG_SPLIT_COLS[name])[0]
    if name == "fox_w_in":
        full = jnp.pad(full, ((0, 0), (0, LANES - full.shape[0] // HEAD_DIM)))
    return full


def _local_step(x, target, small, shard_of=None, pregathered=None):
    S, D = x.shape
    H = D // HEAD_DIM
    depth = small["ln1_g"].shape[0]
    alpha = (2.0 * depth) ** 0.25
    local = pregathered is not None
    bf_pad = jnp.pad(small["fox_b_f"], ((0, 0), (0, LANES - H)))

    def s5_params(j):
        return {k: small["s5_" + k][j] for k in ("a_re", "a_im", "log_dt", "b_re", "b_im", "c_re", "c_im", "d")}

    if local:
        W = {k: _prepare(k[0], g8) for k, g8 in pregathered.items()}
    else:
        keys = _group(0, "a")
        got = _exchange(Exchange([shard_of(*k) for k in keys], [False] * len(keys)), "gather_first")
        W = {k: _prepare(k[0], g8) for k, g8 in zip(keys, got)}
    saved = []
    h = x
    for i in range(depth):
        j = i // 2
        keys = _group(i, "b") + (_group(i + 1, "a") if i + 1 < depth else ())
        ex = None if local else Exchange([shard_of(*k) for k in keys], [False] * len(keys))
        h, s1 = _ffn_fwd(h, W[("ffn1_w_in", i)], W[("ffn1_w_out", i)], small["ln1_g"][i], small["ln1_b"][i], alpha,
                         f"l{i}_ffn1")
        if i % 2 == 0:
            h, s2, extra = _fox_fwd(h, W[("fox_w_in", j)], bf_pad[j:j + 1], W[("fox_w_o", j)], small["lnm_g"][i],
                                    small["lnm_b"][i], alpha, f"l{i}_fox", ex)
        else:
            h, s2, extra = _s5_block_fwd(h, s5_params(j), W[("s5_w_out", j)], small["lnm_g"][i],
                                         small["lnm_b"][i], alpha, f"l{i}_s5", ex)
        if not local:
            W.update({k: _prepare(k[0], g8) for k, g8 in zip(keys, extra)})
        h, s3 = _ffn_fwd(h, W[("ffn2_w_in", i)], W[("ffn2_w_out", i)], small["ln2_g"][i], small["ln2_b"][i], alpha,
                         f"l{i}_ffn2")
        saved.append((s1, s2, s3))

    dh, loss_part = _loss_fwd_bwd(h, target)

    arrived = {}
    pending = {}
    gs = {k: [None] * small[k].shape[0] for k in SMALL}
    for i in reversed(range(depth)):
        j = i // 2
        s1, s2, s3 = saved[i]
        dh, dw_in, dw_out, gs["ln2_g"][i], gs["ln2_b"][i], got = _ffn_bwd(
            dh, s3, W[("ffn2_w_in", i)], W[("ffn2_w_out", i)], small["ln2_g"][i], alpha, f"l{i}_ffn2",
            None if local else pending)
        arrived.update(pending if local else got)
        pending = {("ffn2_w_in", i): dw_in, ("ffn2_w_out", i): dw_out}
        keys = list(pending)
        ex = None if local else Exchange([pending[k] for k in keys], [True] * len(keys))
        if i % 2 == 0:
            dh, mix, sg, extra = _fox_bwd(dh, s2, W[("fox_w_in", j)], bf_pad[j:j + 1], W[("fox_w_o", j)],
                                          small["lnm_g"][i], alpha, f"l{i}_fox", ex)
        else:
            dh, mix, sg, extra = _s5_block_bwd(dh, s2, s5_params(j), W[("s5_w_out", j)], small["lnm_g"][i], alpha,
                                               f"l{i}_s5", ex)
        arrived.update(zip(keys, [pending[k] for k in keys] if local else extra))
        pending = {(k, j): val for k, val in mix.items()}
        for k, val in sg.items():
            gs[k][i if k in ("lnm_g", "lnm_b") else j] = val
        dh, pending[("ffn1_w_in", i)], pending[("ffn1_w_out", i)], gs["ln1_g"][i], gs["ln1_b"][i], _ = _ffn_bwd(
            dh, s1, W[("ffn1_w_in", i)], W[("ffn1_w_out", i)], small["ln1_g"][i], alpha, f"l{i}_ffn1")
    gs = {k: jnp.stack(v) for k, v in gs.items()}
    return loss_part, dh, arrived, pending, gs


def kernel(x, ffn1_w_in, ffn1_w_out, ln1_g, ln1_b, lnm_g, lnm_b, ffn2_w_in, ffn2_w_out, ln2_g, ln2_b, fox_w_in, fox_b_f, fox_w_o, s5_a_re, s5_a_im, s5_log_dt, s5_b_re, s5_b_im, s5_c_re, s5_c_im, s5_d, s5_w_out, loss_target, m_ffn1_w_in, m_ffn1_w_out, m_ln1_g, m_ln1_b, m_lnm_g, m_lnm_b, m_ffn2_w_in, m_ffn2_w_out, m_ln2_g, m_ln2_b, m_fox_w_in, m_fox_b_f, m_fox_w_o, m_s5_a_re, m_s5_a_im, m_s5_log_dt, m_s5_b_re, m_s5_b_im, m_s5_c_re, m_s5_c_im, m_s5_d, m_s5_w_out, v_ffn1_w_in, v_ffn1_w_out, v_ln1_g, v_ln1_b, v_lnm_g, v_lnm_b, v_ffn2_w_in, v_ffn2_w_out, v_ln2_g, v_ln2_b, v_fox_w_in, v_fox_b_f, v_fox_w_o, v_s5_a_re, v_s5_a_im, v_s5_log_dt, v_s5_b_re, v_s5_b_im, v_s5_c_re, v_s5_c_im, v_s5_d, v_s5_w_out):
    args = dict(locals())
    w = {k: args[k] for k in WEIGHTS}
    m = {k: args["m_" + k] for k in WEIGHTS}
    v = {k: args["v_" + k] for k in WEIGHTS}
    small = {k: w[k] for k in SMALL}

    wb = {k: w[k].astype(BF16) for k in BIG}
    loss_part, dx, arrived, last, gs = _local_step(x[0], loss_target[0], small, lambda k, idx: wb[k][idx])
    loss = lax.psum(loss_part, ("x", "y", "c"))

    small_shapes = [w[k].shape for k in SMALL]
    ex = Exchange(list(last.values()) + [_pack_flat([gs[k] for k in SMALL]).astype(BF16)],
                  [True] * len(last) + [False])
    *got, g_small_all = _exchange(ex, "scatter_last_gather_small")
    arrived.update(zip(last, got))
    g_small_flat = _sum8(g_small_all, "sum_small_grads")

    grads, delta, new_m, new_v = {}, {}, {}, {}
    for k in BIG:
        parts = [arrived[(k, l)] for l in range(w[k].shape[0])]
        grads[k], delta[k], new_m[k], new_v[k] = _adamw_recv(parts, w[k], m[k], v[k], "adamw_" + k)
    pk = lambda d: _pack_flat([d[k] for k in SMALL])
    d_, m_, v_ = _adamw(pk(w), g_small_flat, pk(m), pk(v), "adamw_small")
    for dst, flat in ((grads, g_small_flat), (delta, d_), (new_m, m_), (new_v, v_)):
        dst.update(zip(SMALL, _unpack_flat(flat, small_shapes)))

    return (loss, dx[None], *[grads[k] for k in WEIGHTS], *[delta[k] for k in WEIGHTS],
            *[new_m[k] for k in WEIGHTS], *[new_v[k] for k in WEIGHTS])
```

```python
import functools
import math

import jax
import jax.numpy as jnp
from jax import lax
from jax.experimental import pallas as pl
from jax.experimental.pallas import tpu as pltpu

F32 = jnp.float32
BF16 = jnp.bfloat16

N_DEV = 8
HEAD_DIM = 64
S5_GROUP = 16
S5_STATE = 64
LANES = 128
SUBLANES = 8
S5_PART = 256
ATTN_TILE = 512
LN_EPS = 1e-5
NEG_INF = -1e30
ADAM_LR, ADAM_B1, ADAM_B2, ADAM_EPS, ADAM_WD, ADAM_STEP = 0.001, 0.9, 0.999, 1e-08, 0.01, 10
VMEM_LIMIT = 48 * 1024 * 1024
PACK_COLS = 1024
PACK_ROW_ALIGN = 512

MESH = pl.DeviceIdType.MESH


def _tile(dim, pref, align=LANES):
    if dim <= pref:
        return dim
    t = (pref // align) * align
    while t >= align:
        if dim % t == 0:
            return t
        t -= align
    return dim


def _params(sem):
    return pltpu.CompilerParams(dimension_semantics=sem, vmem_limit_bytes=VMEM_LIMIT)


def _mm(a, b, *, ta=False, tb=False, out_dtype=F32, scale=None, add=None, add_scale=1.0,
        tm=512, tn=1408, tk=1408, name="mm"):
    if ta:
        K, M = a.shape
    else:
        M, K = a.shape
    if tb:
        N, K2 = b.shape
    else:
        K2, N = b.shape
    assert K == K2, (a.shape, b.shape, ta, tb)
    tm, tn, tk = _tile(M, tm), _tile(N, tn), _tile(K, tk)
    a_spec = pl.BlockSpec((tk, tm), lambda i, j, k: (k, i)) if ta else pl.BlockSpec((tm, tk), lambda i, j, k: (i, k))
    b_spec = pl.BlockSpec((tn, tk), lambda i, j, k: (j, k)) if tb else pl.BlockSpec((tk, tn), lambda i, j, k: (k, j))
    o_spec = pl.BlockSpec((tm, tn), lambda i, j, k: (i, j))
    return _mm_core(name, a, b, a_spec, b_spec, o_spec, jax.ShapeDtypeStruct((M, N), out_dtype),
                    (M // tm, N // tn, K // tk), (tm, tn), ta, tb, scale, add, add_scale)


def _mm_core(name, a, b, a_spec, b_spec, o_spec, out_shape, grid, acc_shape, ta, tb, scale=None, add=None,
             add_scale=1.0, ex=None):
    nk = grid[2]

    def body(*refs):
        if add is None:
            a_ref, b_ref, o_ref, acc = refs
            add_ref = None
        else:
            a_ref, b_ref, add_ref, o_ref, acc = refs
        k = pl.program_id(2)

        @pl.when(k == 0)
        def _():
            acc[...] = jnp.zeros_like(acc)

        dims = (((0 if ta else 1,), (1 if tb else 0,)), ((), ()))
        acc[...] += lax.dot_general(a_ref[...].astype(BF16), b_ref[...].astype(BF16), dims,
                                    preferred_element_type=F32)

        @pl.when(k == nk - 1)
        def _():
            r = acc[...]
            if scale is not None:
                r = r * scale
            if add_ref is not None:
                r = r + add_scale * add_ref[...]
            o_ref[...] = r.astype(out_shape.dtype)

    in_specs = [a_spec, b_spec]
    args = [a, b]
    if add is not None:
        in_specs.append(o_spec)
        args.append(add)
    (res,), extra = _call(body, name=name, out_shape=(out_shape,), grid=grid, in_specs=in_specs, out_specs=(o_spec,),
                          scratch_shapes=[pltpu.VMEM(acc_shape, F32)], args=args,
                          semantics=("parallel", "parallel", "arbitrary"), ex=ex)
    return res if ex is None else (res, extra)


def _ffn_h(x, wt, name):
    S, D = x.shape
    n, c, _ = wt.shape
    nb = n // 2
    tm = _tile(S, 1024, SUBLANES)
    dims = (((1,), (1,)), ((), ()))

    def body(x_ref, wg_ref, wu_ref, h_ref, a_ref):
        xb = x_ref[...].astype(BF16)
        g = lax.dot_general(xb, wg_ref[...], dims, preferred_element_type=F32).astype(BF16)
        u = lax.dot_general(xb, wu_ref[...], dims, preferred_element_type=F32).astype(BF16)
        h_ref[0] = g
        h_ref[1] = u
        g = g.astype(F32)
        a_ref[...] = (g * _sigmoid(g) * u.astype(F32)).astype(BF16)

    h, a4 = pl.pallas_call(
        body, name=name,
        out_shape=(jax.ShapeDtypeStruct((2, nb, S, c), BF16), jax.ShapeDtypeStruct((nb, S, c), BF16)),
        grid=(S // tm, nb),
        in_specs=[pl.BlockSpec((tm, D), lambda i, j: (i, 0)), pl.BlockSpec((None, c, D), lambda i, j: (j, 0, 0)),
                  pl.BlockSpec((None, c, D), lambda i, j: (j + nb, 0, 0))],
        out_specs=(pl.BlockSpec((2, None, tm, c), lambda i, j: (0, j, i, 0)),
                   pl.BlockSpec((None, tm, c), lambda i, j: (j, i, 0))),
        compiler_params=_params(("parallel", "parallel")),
    )(x, wt, wt)
    return h.reshape(n, S, c), a4


def _ffn_y(a4, wo4, name):
    nb, S, c = a4.shape
    D = wo4.shape[-1]
    tm, tn = _tile(S, 1024, SUBLANES), _tile(D, 1024)
    return _mm_core(name, a4, wo4, pl.BlockSpec((None, tm, c), lambda i, j, k: (k, i, 0)),
                    pl.BlockSpec((None, c, tn), lambda i, j, k: (k, 0, j)),
                    pl.BlockSpec((tm, tn), lambda i, j, k: (i, j)),
                    jax.ShapeDtypeStruct((S, D), F32), (S // tm, D // tn, nb), (tm, tn), False, False)


def _ffn_dwout(a4, dz, scale, name, ex=None):
    nb, S, c = a4.shape
    D = dz.shape[1]
    tk, tn = _tile(S, 1024, SUBLANES), _tile(D, 1024)
    return _mm_core(name, a4, dz, pl.BlockSpec((None, tk, c), lambda i, j, k: (i, k, 0)),
                    pl.BlockSpec((tk, tn), lambda i, j, k: (k, j)),
                    pl.BlockSpec((None, c, tn), lambda i, j, k: (i, 0, j)),
                    jax.ShapeDtypeStruct((nb, c, D), BF16), (nb, D // tn, S // tk), (c, tn), True, False, scale,
                    ex=ex)


def _ffn_dh(dz, wo4, h8, scale, name, ex=None):
    S, D = dz.shape
    nb, c = wo4.shape[0], wo4.shape[1]
    tm = _tile(S, 512, SUBLANES)

    def body(dz_ref, w_ref, h_ref, d_ref):
        da = lax.dot_general(dz_ref[...].astype(BF16), w_ref[...], (((1,), (1,)), ((), ())),
                             preferred_element_type=F32) * scale
        g = h_ref[0].astype(F32)
        u = h_ref[1].astype(F32)
        sg = _sigmoid(g)
        silu = g * sg
        d_ref[0] = (da * u * (sg + silu * (1.0 - sg))).astype(BF16)
        d_ref[1] = (da * silu).astype(BF16)

    pair = pl.BlockSpec((2, None, tm, c), lambda i, j: (0, j, i, 0))
    (dh,), extra = _call(
        body, name=name, out_shape=(jax.ShapeDtypeStruct((2, nb, S, c), BF16),), grid=(S // tm, nb),
        in_specs=[pl.BlockSpec((tm, D), lambda i, j: (i, 0)), pl.BlockSpec((None, c, D), lambda i, j: (j, 0, 0)),
                  pair],
        out_specs=(pair,), scratch_shapes=[], args=(dz, wo4, h8.reshape(2, nb, S, c)),
        semantics=("parallel", "parallel"), ex=ex)
    dh = dh.reshape(2 * nb, S, c)
    return dh if ex is None else (dh, extra)


def _ffn_dwin(x, dh8, name, ex=None):
    S, D = x.shape
    n, _, c = dh8.shape
    tk, tn = _tile(S, 1024, SUBLANES), _tile(D, 1024)
    return _mm_core(name, dh8, x, pl.BlockSpec((None, tk, c), lambda i, j, k: (i, k, 0)),
                    pl.BlockSpec((tk, tn), lambda i, j, k: (k, j)),
                    pl.BlockSpec((None, c, tn), lambda i, j, k: (i, 0, j)),
                    jax.ShapeDtypeStruct((n, c, D), BF16), (n, D // tn, S // tk), (c, tn), True, False, ex=ex)


def _ffn_dx(dh8, wt, dz, alpha, name, ex=None):
    n, S, c = dh8.shape
    D = wt.shape[2]
    tm, tn = _tile(S, 1024, SUBLANES), _tile(D, 1024)
    return _mm_core(name, dh8, wt, pl.BlockSpec((None, tm, c), lambda i, j, k: (k, i, 0)),
                    pl.BlockSpec((None, c, tn), lambda i, j, k: (k, 0, j)),
                    pl.BlockSpec((tm, tn), lambda i, j, k: (i, j)),
                    jax.ShapeDtypeStruct((S, D), F32), (S // tm, D // tn, n), (tm, tn), False, False,
                    None, dz, alpha, ex=ex)


def _res_ln_fwd(x, y, g, b, alpha, s, name):
    S, D = x.shape
    tm = _tile(S, 256, SUBLANES)

    def body(x_ref, y_ref, g_ref, b_ref, o_ref, xh_ref, r_ref):
        z = alpha * x_ref[...] + s * y_ref[...]
        mu = jnp.mean(z, axis=-1, keepdims=True)
        zc = z - mu
        var = jnp.mean(zc * zc, axis=-1, keepdims=True)
        rstd = lax.rsqrt(var + LN_EPS)
        xh = zc * rstd
        xh_ref[...] = xh
        r_ref[...] = rstd
        o_ref[...] = xh * g_ref[...] + b_ref[...]

    row = pl.BlockSpec((tm, D), lambda i: (i, 0))
    vec = pl.BlockSpec((1, D), lambda i: (0, 0))
    return pl.pallas_call(
        body, name=name,
        out_shape=(jax.ShapeDtypeStruct((S, D), F32), jax.ShapeDtypeStruct((S, D), F32),
                   jax.ShapeDtypeStruct((S, 1), F32)),
        grid=(S // tm,), in_specs=[row, row, vec, vec],
        out_specs=(row, row, pl.BlockSpec((tm, 1), lambda i: (i, 0))),
        compiler_params=_params(("parallel",)),
    )(x, y, g.reshape(1, D), b.reshape(1, D))


def _ln_bwd(dout, xh, rstd, g, name):
    S, D = dout.shape
    tm = _tile(S, 256, SUBLANES)

    def body(d_ref, xh_ref, r_ref, g_ref, dz_ref, dg_ref, db_ref):
        i = pl.program_id(0)

        @pl.when(i == 0)
        def _():
            dg_ref[...] = jnp.zeros_like(dg_ref)
            db_ref[...] = jnp.zeros_like(db_ref)

        d = d_ref[...]
        xhv = xh_ref[...]
        dxh = d * g_ref[...]
        m1 = jnp.mean(dxh, axis=-1, keepdims=True)
        m2 = jnp.mean(dxh * xhv, axis=-1, keepdims=True)
        dz_ref[...] = r_ref[...] * (dxh - m1 - xhv * m2)
        dg_ref[...] += jnp.sum(d * xhv, axis=0, keepdims=True)
        db_ref[...] += jnp.sum(d, axis=0, keepdims=True)

    row = pl.BlockSpec((tm, D), lambda i: (i, 0))
    vec = pl.BlockSpec((1, D), lambda i: (0, 0))
    dz, dg, db = pl.pallas_call(
        body, name=name,
        out_shape=(jax.ShapeDtypeStruct((S, D), F32), jax.ShapeDtypeStruct((1, D), F32),
                   jax.ShapeDtypeStruct((1, D), F32)),
        grid=(S // tm,), in_specs=[row, row, pl.BlockSpec((tm, 1), lambda i: (i, 0)), vec],
        out_specs=(row, vec, vec),
        compiler_params=_params(("arbitrary",)),
    )(dout, xh, rstd, g.reshape(1, D))
    return dz, dg[0], db[0]


def _sigmoid(x):
    e = jnp.exp(-jnp.abs(x))
    return jnp.where(x >= 0, 1.0 / (1.0 + e), e / (1.0 + e))


def _glu_fwd(vg, name):
    S, D2 = vg.shape
    D = D2 // 2
    tm = _tile(S, 512, SUBLANES)

    def body(v_ref, g_ref, o_ref):
        o_ref[...] = v_ref[...] * _sigmoid(g_ref[...])

    return pl.pallas_call(
        body, name=name, out_shape=jax.ShapeDtypeStruct((S, D), F32), grid=(S // tm,),
        in_specs=[pl.BlockSpec((tm, D), lambda i: (i, 0)), pl.BlockSpec((tm, D), lambda i: (i, 1))],
        out_specs=pl.BlockSpec((tm, D), lambda i: (i, 0)),
        compiler_params=_params(("parallel",)),
    )(vg, vg)


def _glu_bwd(vg, dm, name):
    S, D2 = vg.shape
    D = D2 // 2
    tm = _tile(S, 512, SUBLANES)

    def body(v_ref, g_ref, dm_ref, dv_ref, dg_ref):
        sg = _sigmoid(g_ref[...])
        d = dm_ref[...]
        dv_ref[...] = (d * sg).astype(BF16)
        dg_ref[...] = (d * v_ref[...] * sg * (1.0 - sg)).astype(BF16)

    blk = pl.BlockSpec((tm, D), lambda i: (i, 0))
    dv, dg = pl.pallas_call(
        body, name=name,
        out_shape=(jax.ShapeDtypeStruct((S, D), BF16), jax.ShapeDtypeStruct((S, D), BF16)),
        grid=(S // tm,), in_specs=[blk, pl.BlockSpec((tm, D), lambda i: (i, 1)), blk],
        out_specs=(blk, blk), compiler_params=_params(("parallel",)),
    )(vg, vg, dm)
    return jnp.concatenate([dv, dg], axis=1)


def _loss_fwd_bwd(y, target):
    S, D = y.shape
    tm = _tile(S, 256, SUBLANES)

    def body(y_ref, t_ref, dy_ref, l_ref):
        i = pl.program_id(0)

        @pl.when(i == 0)
        def _():
            l_ref[...] = jnp.zeros_like(l_ref)

        e = y_ref[...] - t_ref[...]
        dy_ref[...] = e * (1.0 / D)
        l_ref[...] += jnp.sum(e * e, axis=0, keepdims=True) * (0.5 / D)

    row = pl.BlockSpec((tm, D), lambda i: (i, 0))
    dy, part = pl.pallas_call(
        body, name="loss", out_shape=(jax.ShapeDtypeStruct((S, D), F32), jax.ShapeDtypeStruct((1, D), F32)),
        grid=(S // tm,), in_specs=[row, row], out_specs=(row, pl.BlockSpec((1, D), lambda i: (0, 0))),
        compiler_params=_params(("arbitrary",)),
    )(y, target)
    return dy, jnp.sum(part)


def _tri(n, lower):
    r = lax.broadcasted_iota(jnp.int32, (n, n), 0)
    c = lax.broadcasted_iota(jnp.int32, (n, n), 1)
    return jnp.where((c <= r) if lower else (c >= r), 1.0, 0.0)


def _fox_gate_fwd(fl, bf):
    S, W = fl.shape
    tm = _tile(S, 256, SUBLANES)

    def body(fl_ref, b_ref, c_ref, carry):
        i = pl.program_id(0)

        @pl.when(i == 0)
        def _():
            carry[...] = jnp.zeros_like(carry)

        x = fl_ref[...] + b_ref[...]
        lf = jnp.minimum(x, 0.0) - jnp.log(1.0 + jnp.exp(-jnp.abs(x)))
        c_ref[...] = jnp.dot(_tri(tm, True), lf, precision=lax.Precision.HIGHEST,
                             preferred_element_type=F32) + carry[...]
        carry[...] += jnp.sum(lf, axis=0, keepdims=True)

    blk = pl.BlockSpec((tm, W), lambda i: (i, 0))
    return pl.pallas_call(
        body, name="fox_gate_fwd", out_shape=jax.ShapeDtypeStruct((S, W), F32), grid=(S // tm,),
        in_specs=[blk, pl.BlockSpec((1, W), lambda i: (0, 0))], out_specs=blk,
        scratch_shapes=[pltpu.VMEM((1, W), F32)], compiler_params=_params(("arbitrary",)),
    )(fl, bf)


def _fox_gate_bwd(dcum, fl, bf):
    S, W = fl.shape
    tm = _tile(S, 256, SUBLANES)
    nb = S // tm

    def body(dc_ref, fl_ref, b_ref, dfl_ref, db_ref, carry):
        i = pl.program_id(0)

        @pl.when(i == 0)
        def _():
            carry[...] = jnp.zeros_like(carry)
            db_ref[...] = jnp.zeros_like(db_ref)

        dc = dc_ref[...]
        r = jnp.dot(_tri(tm, False), dc, precision=lax.Precision.HIGHEST, preferred_element_type=F32) + carry[...]
        carry[...] += jnp.sum(dc, axis=0, keepdims=True)
        x = fl_ref[...] + b_ref[...]
        dfl = r * (1.0 - _sigmoid(x))
        dfl_ref[...] = dfl
        db_ref[...] += jnp.sum(dfl, axis=0, keepdims=True)

    blk = pl.BlockSpec((tm, W), lambda i: (nb - 1 - i, 0))
    vec = pl.BlockSpec((1, W), lambda i: (0, 0))
    return pl.pallas_call(
        body, name="fox_gate_bwd",
        out_shape=(jax.ShapeDtypeStruct((S, W), F32), jax.ShapeDtypeStruct((1, W), F32)),
        grid=(nb,), in_specs=[blk, blk, vec], out_specs=(blk, vec),
        scratch_shapes=[pltpu.VMEM((1, W), F32)], compiler_params=_params(("arbitrary",)),
    )(dcum, fl, bf)


def _causal(t):
    row = lax.broadcasted_iota(jnp.int32, (t, t), 0)
    col = lax.broadcasted_iota(jnp.int32, (t, t), 1)
    return col <= row


def _first_head(shape):
    return lax.broadcasted_iota(jnp.int32, shape, len(shape) - 1) < HEAD_DIM


def _flash_fwd(proj, cq, ck, ex=None):
    S = proj.shape[0]
    D = (proj.shape[1] - LANES) // 3
    HP = D // LANES
    t = _tile(S, ATTN_TILE)
    nq = S // t
    scale = 1.0 / math.sqrt(HEAD_DIM)

    def body(q_ref, k_ref, v_ref, cq_ref, ck_ref, o_ref, lse_ref, kb, vb):
        qi = pl.program_id(1)

        @pl.when(qi == 0)
        def _():
            kb[...] = k_ref[...].astype(BF16)
            vb[...] = v_ref[...].astype(BF16)

        first = _first_head((t, LANES))
        qf = q_ref[...] * scale
        qs_ = (jnp.where(first, qf, 0.0).astype(BF16), jnp.where(first, 0.0, qf).astype(BF16))
        cqv = cq_ref[0]

        def block(ki, carry, masked):
            m_old, l_old, acc = carry
            sl = pl.ds(pl.multiple_of(ki * t, t), t)
            kk, vv = kb[sl, :], vb[sl, :]
            m_new, l_new, corr, pv = [], [], [], []
            for a in range(2):
                s = lax.dot_general(qs_[a], kk, (((1,), (1,)), ((), ())), preferred_element_type=F32)
                s = s + (cqv[:, a:a + 1] - ck_ref[a, ki])
                if masked:
                    s = jnp.where(_causal(t), s, NEG_INF)
                m_a = jnp.maximum(m_old[a], jnp.max(s, axis=1, keepdims=True))
                p = jnp.exp(s - m_a)
                c_a = jnp.exp(m_old[a] - m_a)
                m_new.append(m_a)
                corr.append(c_a)
                l_new.append(c_a * l_old[a] + jnp.sum(p, axis=1, keepdims=True))
                pv.append(jnp.dot(p.astype(BF16), vv, preferred_element_type=F32))
            acc = jnp.where(first, corr[0] * acc + pv[0], corr[1] * acc + pv[1])
            return tuple(m_new), tuple(l_new), acc

        neg = jnp.full((t, 1), NEG_INF, F32)
        zero = jnp.zeros((t, 1), F32)
        carry = lax.fori_loop(0, qi, lambda ki, c: block(ki, c, False),
                              ((neg, neg), (zero, zero), jnp.zeros((t, LANES), F32)))
        m, l, acc = block(qi, carry, True)
        o_ref[...] = acc / jnp.where(first, l[0], l[1])
        lse_ref[0, :, 0:1] = m[0] + jnp.log(l[0])
        lse_ref[0, :, 1:2] = m[1] + jnp.log(l[1])

    qblk = pl.BlockSpec((t, LANES), lambda h, i: (i, h))
    r2 = pl.BlockSpec((1, t, 2), lambda h, i: (h, i, 0))
    (o, lse), extra = _call(
        body, name="fox_attn_fwd",
        out_shape=(jax.ShapeDtypeStruct((S, D), F32), jax.ShapeDtypeStruct((HP, S, 2), F32)),
        grid=(HP, nq),
        in_specs=[qblk, pl.BlockSpec((S, LANES), lambda h, i: (0, HP + h)),
                  pl.BlockSpec((S, LANES), lambda h, i: (0, 2 * HP + h)), r2,
                  pl.BlockSpec((2, nq, 1, t), lambda h, i: (h, 0, 0, 0))],
        out_specs=(qblk, r2), scratch_shapes=[pltpu.VMEM((S, LANES), BF16), pltpu.VMEM((S, LANES), BF16)],
        args=(proj, proj, proj, cq, ck), semantics=("parallel", "arbitrary"), ex=ex)
    return o, lse, extra


def _flash_bwd(proj, rows, ck, o, do, ex=None):
    S = proj.shape[0]
    D = (proj.shape[1] - LANES) // 3
    HP = D // LANES
    t = _tile(S, ATTN_TILE)
    nb = S // t
    scale = 1.0 / math.sqrt(HEAD_DIM)

    def body(q_ref, k_ref, v_ref, rows_ref, ck_ref, o_ref, do_ref, dq_ref, dk_ref, dv_ref, dcq_ref, dck_ref,
             q_s, do_s, dl_s, dk_acc, dv_acc, dc_acc):
        kb = pl.program_id(1)

        @pl.when(kb == 0)
        def _():
            dq_ref[...] = jnp.zeros_like(dq_ref)
            dcq_ref[...] = jnp.zeros_like(dcq_ref)

            def prep(qb, c):
                sl = pl.ds(pl.multiple_of(qb * t, t), t)
                first = _first_head((t, LANES))
                qf = q_ref[sl, :] * scale
                dof = do_ref[sl, :]
                prod = dof * o_ref[sl, :]
                for a, keep in enumerate((first, jnp.logical_not(first))):
                    q_s[a, sl, :] = jnp.where(keep, qf, 0.0).astype(BF16)
                    do_s[a, sl, :] = jnp.where(keep, dof, 0.0).astype(BF16)
                    dl_s[sl, a:a + 1] = jnp.sum(jnp.where(keep, prod, 0.0), axis=1, keepdims=True)
                return c

            lax.fori_loop(0, nb, prep, 0)

        dk_acc[...] = jnp.zeros_like(dk_acc)
        dv_acc[...] = jnp.zeros_like(dv_acc)
        dc_acc[...] = jnp.zeros_like(dc_acc)
        first = _first_head((t, LANES))
        kf = k_ref[...]
        kk = kf.astype(BF16)
        k_own = (jnp.where(first, kf, 0.0).astype(BF16), jnp.where(first, 0.0, kf).astype(BF16))
        vv = v_ref[...].astype(BF16)

        def block(qb, masked):
            sl = pl.ds(pl.multiple_of(qb * t, t), t)
            rv = rows_ref[0, sl, :]
            dlv = dl_s[sl, :]
            dq_new = dq_ref[sl, :]
            for a in range(2):
                qv = q_s[a, sl, :]
                dob = do_s[a, sl, :]
                s = lax.dot_general(qv, kk, (((1,), (1,)), ((), ())), preferred_element_type=F32)
                s = s + (rv[:, a:a + 1] - ck_ref[a, 0])
                if masked:
                    s = jnp.where(_causal(t), s, NEG_INF)
                p = jnp.exp(s - rv[:, 2 + a:3 + a])
                dv_acc[...] += lax.dot_general(p.astype(BF16), dob, (((0,), (0,)), ((), ())),
                                               preferred_element_type=F32)
                dp = lax.dot_general(dob, vv, (((1,), (1,)), ((), ())), preferred_element_type=F32)
                ds = p * (dp - dlv[:, a:a + 1])
                dsb = ds.astype(BF16)
                dk_acc[...] += lax.dot_general(dsb, qv, (((0,), (0,)), ((), ())), preferred_element_type=F32)
                dq_new = dq_new + jnp.dot(dsb, k_own[a], preferred_element_type=F32) * scale
                dcq_ref[0, sl, a:a + 1] += jnp.sum(ds, axis=1, keepdims=True)
                dc_acc[a:a + 1, :] -= jnp.sum(ds, axis=0, keepdims=True)
            dq_ref[sl, :] = dq_new

        block(kb, True)

        def rest(qb, c):
            block(qb, False)
            return c

        lax.fori_loop(kb + 1, nb, rest, 0)
        dk_ref[...] = dk_acc[...]
        dv_ref[...] = dv_acc[...]
        dck_ref[0, 0] = dc_acc[0:1, :]
        dck_ref[1, 0] = dc_acc[1:2, :]

    full = lambda c0: pl.BlockSpec((S, LANES), lambda h, j, c0=c0: (0, c0 + h))
    blk = lambda c0: pl.BlockSpec((t, LANES), lambda h, j, c0=c0: (j, c0 + h))
    ck_s = pl.BlockSpec((2, 1, 1, t), lambda h, j: (h, j, 0, 0))
    f32 = lambda *s: jax.ShapeDtypeStruct(s, F32)
    outs, extra = _call(
        body, name="fox_attn_bwd",
        out_shape=(f32(S, D), f32(S, D), f32(S, D), f32(HP, S, 2), f32(2 * HP, nb, 1, t)),
        grid=(HP, nb),
        in_specs=[full(0), blk(HP), blk(2 * HP), pl.BlockSpec((1, S, 4), lambda h, j: (h, 0, 0)), ck_s,
                  full(0), full(0)],
        out_specs=(full(0), blk(0), blk(0), pl.BlockSpec((1, S, 2), lambda h, j: (h, 0, 0)), ck_s),
        scratch_shapes=[pltpu.VMEM((2, S, LANES), BF16), pltpu.VMEM((2, S, LANES), BF16),
                        pltpu.VMEM((S, 2), F32), pltpu.VMEM((t, LANES), F32), pltpu.VMEM((t, LANES), F32),
                        pltpu.VMEM((2, t), F32)],
        args=(proj, proj, proj, rows, ck, o, do), semantics=("parallel", "arbitrary"), ex=ex)
    return (*outs, extra)


def _s5_consts(T):
    rows = T * SUBLANES
    rr = lax.broadcasted_iota(jnp.int32, (rows, T), 0)
    tt = lax.broadcasted_iota(jnp.int32, (rows, T), 1)
    rep = jnp.where(rr // SUBLANES == tt, 1.0, 0.0).astype(BF16)
    r2 = lax.broadcasted_iota(jnp.int32, (rows, S5_PART), 0)
    c2 = lax.broadcasted_iota(jnp.int32, (rows, S5_PART), 1)
    mask = (c2 // (S5_PART // SUBLANES)) == (r2 % SUBLANES)
    return rep, mask


def _gelu(y):
    c = math.sqrt(2.0 / math.pi)
    return 0.5 * y * (1.0 + jnp.tanh(c * (y + 0.044715 * y * y * y)))


def _gelu_grad(y):
    c = math.sqrt(2.0 / math.pi)
    th = jnp.tanh(c * (y + 0.044715 * y * y * y))
    return 0.5 * (1.0 + th) + 0.5 * y * (1.0 - th * th) * c * (1.0 + 3.0 * 0.044715 * y * y)


def _s5_fwd(x, rmat, cmat, lam, dskip, ex=None):
    S, D = x.shape
    NQ = D // S5_PART
    T = _tile(S, 128, SUBLANES)
    rows = T * SUBLANES

    def body(x_ref, r_ref, c_ref, lam_ref, d_ref, y_ref, yg_ref, h_ref, bu_s, carry):
        i = pl.program_id(0)

        @pl.when(i == 0)
        def _():
            carry[...] = jnp.zeros_like(carry)

        rep, mask = _s5_consts(T)
        cols = [pl.ds(q * S5_PART, S5_PART) for q in range(NQ)]
        for q in range(NQ):
            xrep = jnp.dot(rep, x_ref[:, cols[q]].astype(BF16), preferred_element_type=F32)
            lx = jnp.where(mask, xrep, 0.0).astype(BF16)
            bu_s[q] = jnp.dot(lx, r_ref[q], preferred_element_type=F32)
        lam_v = [(lam_ref[q, :, 0:LANES], lam_ref[q, :, LANES:2 * LANES]) for q in range(NQ)]

        def step(t, c):
            o = pl.multiple_of(t * SUBLANES, SUBLANES)
            new = []
            for q in range(NQ):
                hr, hi = c[q]
                ar, ai = lam_v[q]
                sl = bu_s[q, pl.ds(o, SUBLANES), :]
                nhr = ar * hr - ai * hi + sl[:, 0:LANES]
                nhi = ar * hi + ai * hr + sl[:, LANES:2 * LANES]
                h_ref[q, pl.ds(o, SUBLANES), 0:LANES] = nhr
                h_ref[q, pl.ds(o, SUBLANES), LANES:2 * LANES] = nhi
                new.append((nhr, nhi))
            return tuple(new)

        fin = lax.fori_loop(0, T, step,
                            tuple((carry[q, :, 0:LANES], carry[q, :, LANES:2 * LANES]) for q in range(NQ)))
        for q in range(NQ):
            carry[q, :, 0:LANES] = fin[q][0]
            carry[q, :, LANES:2 * LANES] = fin[q][1]
            z = jnp.dot(h_ref[q].astype(BF16), c_ref[q], preferred_element_type=F32)
            z = jnp.where(mask, z, 0.0)
            y = jnp.sum(z.reshape(T, SUBLANES, S5_PART), axis=1) + d_ref[:, cols[q]] * x_ref[:, cols[q]]
            y_ref[:, cols[q]] = y
            yg_ref[:, cols[q]] = _gelu(y).astype(BF16)

    xs = pl.BlockSpec((T, D), lambda i: (i, 0))
    ms = pl.BlockSpec((NQ, S5_PART, S5_PART), lambda i: (0, 0, 0))
    outs, extra = _call(
        body, name="s5_scan_fwd",
        out_shape=(jax.ShapeDtypeStruct((S, D), F32), jax.ShapeDtypeStruct((S, D), BF16),
                   jax.ShapeDtypeStruct((NQ, S * SUBLANES, S5_PART), F32)),
        grid=(S // T,),
        in_specs=[xs, ms, ms, pl.BlockSpec((NQ, SUBLANES, S5_PART), lambda i: (0, 0, 0)),
                  pl.BlockSpec((1, D), lambda i: (0, 0))],
        out_specs=(xs, xs, pl.BlockSpec((NQ, rows, S5_PART), lambda i: (0, i, 0))),
        scratch_shapes=[pltpu.VMEM((NQ, rows, S5_PART), F32), pltpu.VMEM((NQ, SUBLANES, S5_PART), F32)],
        args=(x, rmat, cmat, lam, dskip), semantics=("arbitrary",), ex=ex)
    return (*outs, extra)


def _s5_bwd(x, y, dyg, hs, rmat, cmat, lam, dskip, res, res_scale, ex=None):
    S, D = x.shape
    NQ = D // S5_PART
    T = _tile(S, 128, SUBLANES)
    nb = S // T
    rows = T * SUBLANES

    def body(x_ref, y_ref, dyg_ref, res_ref, h_ref, hp_ref, r_ref, c_ref, lam_ref, d_ref,
             dx_ref, dr_ref, dc_ref, dlam_ref, dd_ref, dh_s, g_s, hs_s, carry):
        i = pl.program_id(0)

        @pl.when(i == 0)
        def _():
            carry[...] = jnp.zeros_like(carry)
            dr_ref[...] = jnp.zeros_like(dr_ref)
            dc_ref[...] = jnp.zeros_like(dc_ref)
            dlam_ref[...] = jnp.zeros_like(dlam_ref)
            dd_ref[...] = jnp.zeros_like(dd_ref)

        rep, mask = _s5_consts(T)
        cols = [pl.ds(q * S5_PART, S5_PART) for q in range(NQ)]
        dys, ldys = [], []
        for q in range(NQ):
            dy = dyg_ref[:, cols[q]] * _gelu_grad(y_ref[:, cols[q]])
            dyrep = jnp.dot(rep, dy.astype(BF16), preferred_element_type=F32)
            ldy = jnp.where(mask, dyrep, 0.0).astype(BF16)
            dh_s[q] = lax.dot_general(ldy, c_ref[q], (((1,), (1,)), ((), ())), preferred_element_type=F32)
            dys.append(dy)
            ldys.append(ldy)
        lam_v = [(lam_ref[q, :, 0:LANES], lam_ref[q, :, LANES:2 * LANES]) for q in range(NQ)]

        def step(n, c):
            o = pl.multiple_of((T - 1 - n) * SUBLANES, SUBLANES)
            new = []
            for q in range(NQ):
                gr, gi = c[q]
                ar, ai = lam_v[q]
                sl = dh_s[q, pl.ds(o, SUBLANES), :]
                ngr = sl[:, 0:LANES] + ar * gr + ai * gi
                ngi = sl[:, LANES:2 * LANES] - ai * gr + ar * gi
                g_s[q, pl.ds(o, SUBLANES), 0:LANES] = ngr
                g_s[q, pl.ds(o, SUBLANES), LANES:2 * LANES] = ngi
                new.append((ngr, ngi))
            return tuple(new)

        fin = lax.fori_loop(0, T, step,
                            tuple((carry[q, :, 0:LANES], carry[q, :, LANES:2 * LANES]) for q in range(NQ)))
        for q in range(NQ):
            carry[q, :, 0:LANES] = fin[q][0]
            carry[q, :, LANES:2 * LANES] = fin[q][1]
            xv = x_ref[:, cols[q]]
            hv = h_ref[q]
            hs_s[0:SUBLANES, :] = jnp.where(i == nb - 1, 0.0, hp_ref[q])
            hs_s[SUBLANES:rows + SUBLANES, :] = hv
            hprev = hs_s[0:rows, :]
            gv = g_s[q]
            g_re, g_im = gv[:, 0:LANES], gv[:, LANES:2 * LANES]
            hp_re, hp_im = hprev[:, 0:LANES], hprev[:, LANES:2 * LANES]
            dar = jnp.sum((g_re * hp_re + g_im * hp_im).reshape(T, SUBLANES, LANES), axis=0)
            dai = jnp.sum((g_im * hp_re - g_re * hp_im).reshape(T, SUBLANES, LANES), axis=0)
            dlam_ref[q, :, 0:LANES] += dar
            dlam_ref[q, :, LANES:2 * LANES] += dai

            gb = gv.astype(BF16)
            xrep = jnp.dot(rep, xv.astype(BF16), preferred_element_type=F32)
            lx = jnp.where(mask, xrep, 0.0).astype(BF16)
            dr_ref[q] += lax.dot_general(lx, gb, (((0,), (0,)), ((), ())), preferred_element_type=F32)
            dc_ref[q] += lax.dot_general(hv.astype(BF16), ldys[q], (((0,), (0,)), ((), ())),
                                         preferred_element_type=F32)
            zx = lax.dot_general(gb, r_ref[q], (((1,), (1,)), ((), ())), preferred_element_type=F32)
            zx = jnp.where(mask, zx, 0.0)
            dx_ref[:, cols[q]] = (jnp.sum(zx.reshape(T, SUBLANES, S5_PART), axis=1) + d_ref[:, cols[q]] * dys[q]
                                  + res_scale * res_ref[:, cols[q]])
            dd_ref[:, cols[q]] += jnp.sum(dys[q] * xv, axis=0, keepdims=True)

    xs = pl.BlockSpec((T, D), lambda i: (nb - 1 - i, 0))
    ms = pl.BlockSpec((NQ, S5_PART, S5_PART), lambda i: (0, 0, 0))
    ls = pl.BlockSpec((NQ, SUBLANES, S5_PART), lambda i: (0, 0, 0))
    ds_ = pl.BlockSpec((1, D), lambda i: (0, 0))
    outs, extra = _call(
        body, name="s5_scan_bwd",
        out_shape=(jax.ShapeDtypeStruct((S, D), F32), jax.ShapeDtypeStruct((NQ, S5_PART, S5_PART), F32),
                   jax.ShapeDtypeStruct((NQ, S5_PART, S5_PART), F32),
                   jax.ShapeDtypeStruct((NQ, SUBLANES, S5_PART), F32), jax.ShapeDtypeStruct((1, D), F32)),
        grid=(nb,),
        in_specs=[xs, xs, xs, xs, pl.BlockSpec((NQ, rows, S5_PART), lambda i: (0, nb - 1 - i, 0)),
                  pl.BlockSpec((NQ, SUBLANES, S5_PART), lambda i: (0, jnp.maximum((nb - 1 - i) * T - 1, 0), 0)),
                  ms, ms, ls, ds_],
        out_specs=(xs, ms, ms, ls, ds_),
        scratch_shapes=[pltpu.VMEM((NQ, rows, S5_PART), F32), pltpu.VMEM((NQ, rows, S5_PART), F32),
                        pltpu.VMEM((rows + SUBLANES, S5_PART), F32), pltpu.VMEM((NQ, SUBLANES, S5_PART), F32)],
        args=(x, y, dyg, res, hs, hs, rmat, cmat, lam, dskip), semantics=("arbitrary",), ex=ex)
    return (*outs, extra)


def _s5_discretise(a_re, a_im, log_dt, b_re, b_im):
    dt = jnp.exp(log_dt)[:, None]
    mag = jnp.exp(a_re * dt)
    ang = a_im * dt
    lb_re = mag * jnp.cos(ang)
    lb_im = mag * jnp.sin(ang)
    den = a_re * a_re + a_im * a_im
    nr = lb_re - 1.0
    ni = lb_im
    z_re = (nr * a_re + ni * a_im) / den
    z_im = (ni * a_re - nr * a_im) / den
    bb_re = z_re[..., None] * b_re - z_im[..., None] * b_im
    bb_im = z_re[..., None] * b_im + z_im[..., None] * b_re
    return lb_re, lb_im, bb_re, bb_im


def _s5_expand(w):
    G = w.shape[0]
    NQ = G // 16
    base = w.reshape(NQ, S5_PART, S5_STATE)
    half = (jnp.arange(S5_PART) // S5_GROUP) % 2
    sel = (half[:, None] == jnp.arange(2)[None, :]).astype(w.dtype)
    out = base[:, :, None, :] * sel[None, :, :, None]
    return out.reshape(NQ, S5_PART, 2 * S5_STATE)


def _s5_extract(m):
    NQ = m.shape[0]
    half = (jnp.arange(S5_PART) // S5_GROUP) % 2
    sel = (half[:, None] == jnp.arange(2)[None, :]).astype(m.dtype)
    base = jnp.sum(m.reshape(NQ, S5_PART, 2, S5_STATE) * sel[None, :, :, None], axis=2)
    return base.reshape(NQ * 16, S5_GROUP, S5_STATE)


def _s5_slab(v):
    return v.reshape(v.shape[0] // 16, SUBLANES, LANES)


def _adamw(w, g, m, v, name):
    R, C = w.shape
    tr = _tile(R, 256, SUBLANES)
    c1 = 1.0 / (1.0 - ADAM_B1 ** ADAM_STEP)
    c2 = 1.0 / (1.0 - ADAM_B2 ** ADAM_STEP)

    def body(w_ref, g_ref, m_ref, v_ref, d_ref, nm_ref, nv_ref):
        gv = g_ref[...]
        nm = ADAM_B1 * m_ref[...] + (1.0 - ADAM_B1) * gv
        nv = ADAM_B2 * v_ref[...] + (1.0 - ADAM_B2) * (gv * gv)
        nm_ref[...] = nm
        nv_ref[...] = nv
        d_ref[...] = -ADAM_LR * ((nm * c1) / (jnp.sqrt(nv * c2) + ADAM_EPS) + ADAM_WD * w_ref[...])

    blk = pl.BlockSpec((tr, C), lambda i: (i, 0))
    sh = jax.ShapeDtypeStruct((R, C), F32)
    return pl.pallas_call(
        body, name=name, out_shape=(sh, sh, sh), grid=(R // tr,), in_specs=[blk] * 4, out_specs=(blk,) * 3,
        compiler_params=_params(("parallel",)),
    )(w, g, m, v)


def _adamw_recv(parts, w, m, v, name):
    L, R, C = w.shape
    tr = _tile(R, 256, SUBLANES)
    c1 = 1.0 / (1.0 - ADAM_B1 ** ADAM_STEP)
    c2 = 1.0 / (1.0 - ADAM_B2 ** ADAM_STEP)

    def body(*refs):
        p_refs = refs[:L]
        w_ref, m_ref, v_ref, g_ref, d_ref, nm_ref, nv_ref = refs[L:]
        li = pl.program_id(0)
        for l in range(L):
            @pl.when(li == l)
            def _(p_ref=p_refs[l]):
                gv = p_ref[0].astype(F32)
                for k in range(1, N_DEV):
                    gv = gv + p_ref[k].astype(F32)
                g_ref[...] = gv
                nm = ADAM_B1 * m_ref[...] + (1.0 - ADAM_B1) * gv
                nv = ADAM_B2 * v_ref[...] + (1.0 - ADAM_B2) * (gv * gv)
                nm_ref[...] = nm
                nv_ref[...] = nv
                d_ref[...] = -ADAM_LR * ((nm * c1) / (jnp.sqrt(nv * c2) + ADAM_EPS) + ADAM_WD * w_ref[...])

    p_specs = [pl.BlockSpec((N_DEV, tr, C), lambda li, i, l=l: (0, jnp.where(li == l, i, 0), 0)) for l in range(L)]
    blk = pl.BlockSpec((None, tr, C), lambda li, i: (li, i, 0))
    sh = jax.ShapeDtypeStruct((L, R, C), F32)
    return pl.pallas_call(
        body, name=name, out_shape=(sh, sh, sh, sh), grid=(L, R // tr),
        in_specs=p_specs + [blk, blk, blk], out_specs=(blk,) * 4,
        compiler_params=_params(("parallel", "parallel")),
    )(*parts, w, m, v)


def _sum8(parts, name):
    _, R, C = parts.shape
    tr = _tile(R, 256, SUBLANES)

    def body(p_ref, o_ref):
        acc = p_ref[0].astype(F32)
        for k in range(1, N_DEV):
            acc = acc + p_ref[k].astype(F32)
        o_ref[...] = acc

    return pl.pallas_call(
        body, name=name, out_shape=jax.ShapeDtypeStruct((R, C), F32), grid=(R // tr,),
        in_specs=[pl.BlockSpec((N_DEV, tr, C), lambda i: (0, i, 0))],
        out_specs=pl.BlockSpec((tr, C), lambda i: (i, 0)), compiler_params=_params(("parallel",)),
    )(parts)


def _peers():
    x, y, c = lax.axis_index("x"), lax.axis_index("y"), lax.axis_index("c")
    me = 4 * x + 2 * y + c
    out = []
    for k in range(1, N_DEV):
        kx, ky, kc = (k >> 2) & 1, (k >> 1) & 1, k & 1
        px, py, pc = x ^ kx, y ^ ky, c ^ kc
        out.append(((px, py, pc), 4 * px + 2 * py + pc))
    return me, out


SIBLING = 1
SAME_CORE = (2, 4, 6)


class Exchange:
    def __init__(self, xs, scatter):
        self.xs = list(xs)
        self.scatter = list(scatter)
        self.n = len(self.xs)

    def out_shapes(self):
        return tuple(jax.ShapeDtypeStruct(x.shape if sc else (N_DEV,) + x.shape, x.dtype)
                     for x, sc in zip(self.xs, self.scatter))

    def sems(self):
        return [pltpu.SemaphoreType.DMA((self.n * N_DEV,)), pltpu.SemaphoreType.DMA((self.n * N_DEV,)),
                pltpu.SemaphoreType.DMA((self.n,))]

    def _copy(self, i, m, src, dst, to, send_sems, recv_sems):
        return pltpu.make_async_remote_copy(src_ref=src, dst_ref=dst, send_sem=send_sems.at[i * N_DEV + m],
                                            recv_sem=recv_sems.at[i * N_DEV + m], device_id=to, device_id_type=MESH)

    def start(self, x_refs, o_refs, send_sems, recv_sems, local_sems):
        me, peers = _peers()
        for i, (x_ref, o_ref, sc) in enumerate(zip(x_refs, o_refs, self.scatter)):
            pltpu.make_async_copy(x_ref.at[me] if sc else x_ref, o_ref.at[me], local_sems.at[i]).start()
            for m in (range(1, N_DEV) if sc else (SIBLING,) + SAME_CORE):
                dev, idx = peers[m - 1]
                self._copy(i, m, x_ref.at[idx] if sc else x_ref, o_ref.at[me], dev, send_sems, recv_sems).start()

    def middle(self, x_refs, o_refs, send_sems, recv_sems, local_sems):
        me, peers = _peers()
        sibling = peers[SIBLING - 1][0]
        for i, (x_ref, o_ref, sc) in enumerate(zip(x_refs, o_refs, self.scatter)):
            if sc:
                continue
            for m in SAME_CORE:
                dev, idx = peers[m - 1]
                self._copy(i, m, x_ref, o_ref.at[idx], dev, send_sems, recv_sems).wait_recv()
                self._copy(i, m ^ 1, o_ref.at[idx], o_ref.at[idx], sibling, send_sems, recv_sems).start()

    def wait(self, x_refs, o_refs, send_sems, recv_sems, local_sems):
        me, peers = _peers()
        for i, (x_ref, o_ref, sc) in enumerate(zip(x_refs, o_refs, self.scatter)):
            for m in range(1, N_DEV):
                dev, idx = peers[m - 1]
                cp = self._copy(i, m, x_ref.at[idx] if sc else x_ref, o_ref.at[idx], dev, send_sems, recv_sems)
                if sc or m not in SAME_CORE:
                    cp.wait_recv()
                cp.wait_send()
            pltpu.make_async_copy(x_ref.at[me] if sc else x_ref, o_ref.at[me], local_sems.at[i]).wait()


def _exchange(ex, name):
    n = ex.n

    def body(*refs):
        x_refs, o_refs, sems = refs[:n], refs[n:2 * n], refs[2 * n:]
        ex.start(x_refs, o_refs, *sems)
        ex.middle(x_refs, o_refs, *sems)
        ex.wait(x_refs, o_refs, *sems)

    hbm = pl.BlockSpec(memory_space=pltpu.HBM)
    return pl.pallas_call(body, name=name, out_shape=ex.out_shapes(), in_specs=[hbm] * n, out_specs=(hbm,) * n,
                          scratch_shapes=ex.sems())(*ex.xs)


def _call(body, *, name, out_shape, grid, in_specs, out_specs, scratch_shapes, args, semantics, ex=None):
    if ex is None:
        return pl.pallas_call(body, name=name, out_shape=out_shape, grid=grid, in_specs=in_specs, out_specs=out_specs,
                              scratch_shapes=scratch_shapes, compiler_params=_params(semantics))(*args), ()
    n, ni, no, ns = ex.n, len(args), len(out_shape), len(scratch_shapes)

    def wrapped(*refs):
        ins, cx = refs[:ni], refs[ni:ni + n]
        outs, co = refs[ni + n:ni + n + no], refs[ni + n + no:ni + 2 * n + no]
        scratch, sems = refs[ni + 2 * n + no:ni + 2 * n + no + ns], refs[ni + 2 * n + no + ns:]
        step = functools.reduce(lambda acc, a: acc * grid[a] + pl.program_id(a), range(len(grid)), 0)
        steps = math.prod(grid)

        @pl.when(step == 0)
        def _():
            ex.start(cx, co, *sems)

        body(*ins, *outs, *scratch)

        @pl.when(step == (steps * 3) // 4 - (steps > 1))
        def _():
            ex.middle(cx, co, *sems)

        @pl.when(step == steps - 1)
        def _():
            ex.wait(cx, co, *sems)

    hbm = pl.BlockSpec(memory_space=pltpu.HBM)
    res = pl.pallas_call(
        wrapped, name=name, out_shape=tuple(out_shape) + ex.out_shapes(), grid=grid,
        in_specs=list(in_specs) + [hbm] * n, out_specs=tuple(out_specs) + (hbm,) * n,
        scratch_shapes=list(scratch_shapes) + ex.sems(),
        compiler_params=_params(("arbitrary",) * len(grid)))(*args, *ex.xs)
    return res[:no], res[no:]


def _pack_flat(arrs):
    cat = jnp.concatenate([a.reshape(-1) for a in arrs])
    per = PACK_COLS * 2 * SUBLANES
    tot = -(-cat.shape[0] // per) * per
    return jnp.pad(cat, (0, tot - cat.shape[0])).reshape(tot // PACK_COLS, PACK_COLS)


def _unpack_flat(packed, shapes):
    flat = packed.reshape(-1)
    out, o = [], 0
    for s in shapes:
        n = math.prod(s)
        out.append(flat[o:o + n].reshape(s))
        o += n
    return out


def _ffn_fwd(x, wt, wo4, g, b, alpha, tag):
    h8, a4 = _ffn_h(x, wt, tag + "_h")
    y = _ffn_y(a4, wo4, tag + "_y")
    out, xh, rstd = _res_ln_fwd(x, y, g, b, alpha, 0.5, tag + "_ln")
    return out, (x, h8, a4, xh, rstd)


def _ffn_bwd(dout, saved, wt, wo4, g, alpha, tag, carry=None, own_key=None):
    x, h8, a4, xh, rstd = saved
    load = dict(carry or {})
    slots = [[], [], [], []]
    for n, k in enumerate(sorted(load, key=lambda k: -load[k].size)):
        slots[min(n, 3)].append(k)
    arrived = {}

    def run(fn, slot, *args):
        keys = slots[slot]
        if not keys:
            return fn(*args)
        res, extra = fn(*args, ex=Exchange([load[k] for k in keys], [True] * len(keys)))
        arrived.update(zip(keys, extra))
        return res

    dz, dg, db = _ln_bwd(dout, xh, rstd, g, tag + "_ln_bwd")
    dw_out = run(_ffn_dwout, 3, a4, dz, 0.5, tag + "_dwout")
    dw_out = dw_out.reshape(N_DEV, -1, dw_out.shape[-1])
    if own_key is not None:
        load[own_key] = dw_out
        slots[1].append(own_key)
    dh8 = run(_ffn_dh, 2, dz, wo4, h8, 0.5, tag + "_dh")
    dw_in = run(_ffn_dwin, 0, x, dh8, tag + "_dwin")
    dx = run(_ffn_dx, 1, dh8, wt, dz, alpha, tag + "_dx")
    return dx, dw_in, dw_out, dg, db, arrived


def _fox_fwd(x, w_in_pad, b_f_pad, w_o, g, b, alpha, tag, ex=None):
    S, D = x.shape
    H = D // HEAD_DIM
    proj = _mm(x, w_in_pad, name=tag + "_proj")
    fl = proj[:, 3 * D:]
    cum = _fox_gate_fwd(fl, b_f_pad)
    ch = cum[:, :H].T
    t = _tile(S, ATTN_TILE)
    cq = ch.reshape(H // 2, 2, S).transpose(0, 2, 1)
    ck = ch.reshape(H, S // t, 1, t)
    o, lse, extra = _flash_fwd(proj, cq, ck, ex)
    m = _mm(o, w_o, name=tag + "_out")
    out, xh, rstd = _res_ln_fwd(x, m, g, b, alpha, 1.0, tag + "_ln")
    return out, (x, proj, jnp.concatenate([cq, lse], axis=2), ck, o, fl, xh, rstd), extra


def _fox_bwd(dout, saved, w_in_pad, b_f_pad, w_o, g, alpha, tag, ex=None):
    x, proj, rows, ck, o, fl, xh, rstd = saved
    S, D = x.shape
    H = D // HEAD_DIM
    dz, dg, db = _ln_bwd(dout, xh, rstd, g, tag + "_ln_bwd")
    dw_o = _mm(o, dz, ta=True, name=tag + "_dwo")
    do = _mm(dz, w_o, tb=True, name=tag + "_do")
    dq, dk, dv, dcq, dck, extra = _flash_bwd(proj, rows, ck, o, do, ex)
    dcq = dcq.transpose(0, 2, 1).reshape(H, S)
    dcum = jnp.pad((dcq + dck.reshape(H, S)).T, ((0, 0), (0, LANES - H)))
    dfl, dbf = _fox_gate_bwd(dcum, fl, b_f_pad)
    dproj = jnp.concatenate([dq.astype(BF16), dk.astype(BF16), dv.astype(BF16), dfl.astype(BF16)], axis=1)
    dw_in = _mm(x, dproj, ta=True, name=tag + "_dwin")
    dx = _mm(dproj, w_in_pad, tb=True, add=dz, add_scale=alpha, name=tag + "_dx")
    shards = {"fox_w_in": _split(dw_in[None, :, :3 * D + H], True)[:, 0].astype(BF16),
              "fox_w_o": _split(dw_o[None], False)[:, 0].astype(BF16)}
    return dx, shards, {"fox_b_f": dbf[0, :H], "lnm_g": dg, "lnm_b": db}, extra


def _s5_mats(p):
    lb_re, lb_im, bb_re, bb_im = _s5_discretise(p["a_re"], p["a_im"], p["log_dt"], p["b_re"], p["b_im"])
    rmat = jnp.concatenate([_s5_expand(bb_re.transpose(0, 2, 1)), _s5_expand(bb_im.transpose(0, 2, 1))], axis=2)
    cmat = jnp.concatenate([_s5_expand(p["c_re"]).transpose(0, 2, 1), -_s5_expand(p["c_im"]).transpose(0, 2, 1)],
                           axis=1)
    lam = jnp.concatenate([_s5_slab(lb_re), _s5_slab(lb_im)], axis=2)
    return rmat.astype(BF16), cmat.astype(BF16), lam


def _s5_block_fwd(x, p, w_out, g, b, alpha, tag, ex=None):
    S, D = x.shape
    rmat, cmat, lam = _s5_mats(p)
    dskip = p["d"].reshape(1, D)
    y, yg, hs, extra = _s5_fwd(x, rmat, cmat, lam, dskip, ex)
    vg = _mm(yg, w_out, name=tag + "_vg")
    m = _glu_fwd(vg, tag + "_glu")
    out, xh, rstd = _res_ln_fwd(x, m, g, b, alpha, 1.0, tag + "_ln")
    return out, (x, y, yg, hs, vg, rmat, cmat, lam, dskip, xh, rstd), extra


def _s5_block_bwd(dout, saved, p, w_out, g, alpha, tag, ex=None):
    x, y, yg, hs, vg, rmat, cmat, lam, dskip, xh, rstd = saved
    S, D = x.shape
    G = D // S5_GROUP
    dz, dg, db = _ln_bwd(dout, xh, rstd, g, tag + "_ln_bwd")
    dvg = _glu_bwd(vg, dz, tag + "_glu_bwd")
    dw_out = _mm(yg, dvg, ta=True, name=tag + "_dwout")
    dyg = _mm(dvg, w_out, tb=True, name=tag + "_dyg")
    dx, dr, dc, dlam, dd, extra = _s5_bwd(x, y, dyg, hs, rmat, cmat, lam, dskip, dz, alpha, ex)
    dbb_re = _s5_extract(dr[:, :, :LANES]).transpose(0, 2, 1)
    dbb_im = _s5_extract(dr[:, :, LANES:]).transpose(0, 2, 1)
    dc_re = _s5_extract(dc[:, :LANES, :].transpose(0, 2, 1))
    dc_im = -_s5_extract(dc[:, LANES:, :].transpose(0, 2, 1))
    dlb_re = dlam[:, :, :LANES].reshape(G, S5_STATE)
    dlb_im = dlam[:, :, LANES:].reshape(G, S5_STATE)
    _, vjp = jax.vjp(_s5_discretise, p["a_re"], p["a_im"], p["log_dt"], p["b_re"], p["b_im"])
    da_re, da_im, dlog_dt, db_re, db_im = vjp((dlb_re, dlb_im, dbb_re, dbb_im))
    small = dict(s5_a_re=da_re, s5_a_im=da_im, s5_log_dt=dlog_dt, s5_b_re=db_re, s5_b_im=db_im, s5_c_re=dc_re,
                 s5_c_im=dc_im, s5_d=dd.reshape(G, S5_GROUP), lnm_g=dg, lnm_b=db)
    return dx, {"s5_w_out": _split(dw_out[None], True)[:, 0].astype(BF16)}, small, extra


FFN_NAMES = ("ffn1_w_in", "ffn1_w_out", "ffn2_w_in", "ffn2_w_out")
BIG = FFN_NAMES + ("fox_w_in", "fox_w_o", "s5_w_out")
BIG_SPLIT_COLS = {"ffn1_w_in": True, "ffn1_w_out": False, "ffn2_w_in": True, "ffn2_w_out": False,
                  "fox_w_in": True, "fox_w_o": False, "s5_w_out": True}
SMALL = ("ln1_g", "ln1_b", "lnm_g", "lnm_b", "ln2_g", "ln2_b", "fox_b_f", "s5_a_re", "s5_a_im", "s5_log_dt",
         "s5_b_re", "s5_b_im", "s5_c_re", "s5_c_im", "s5_d")
WEIGHTS = ("ffn1_w_in", "ffn1_w_out", "ln1_g", "ln1_b", "lnm_g", "lnm_b", "ffn2_w_in", "ffn2_w_out", "ln2_g", "ln2_b",
           "fox_w_in", "fox_b_f", "fox_w_o", "s5_a_re", "s5_a_im", "s5_log_dt", "s5_b_re", "s5_b_im", "s5_c_re",
           "s5_c_im", "s5_d", "s5_w_out")


def _join(gathered, split_cols):
    n, L, r, c = gathered.shape
    if split_cols:
        return gathered.transpose(1, 2, 0, 3).reshape(L, r, n * c)
    return gathered.transpose(1, 0, 2, 3).reshape(L, n * r, c)


def _split(full, split_cols):
    L, R, C = full.shape
    if split_cols:
        return full.reshape(L, R, N_DEV, C // N_DEV).transpose(2, 0, 1, 3)
    return full.reshape(L, N_DEV, R // N_DEV, C).transpose(1, 0, 2, 3)


def _group(i, part):
    if part == "b":
        return (("ffn2_w_in", i), ("ffn2_w_out", i))
    mixer = (("fox_w_in", i // 2), ("fox_w_o", i // 2)) if i % 2 == 0 else (("s5_w_out", i // 2),)
    return (("ffn1_w_in", i), ("ffn1_w_out", i)) + mixer


def _prepare(name, g8):
    if name in ("ffn1_w_in", "ffn2_w_in"):
        return g8
    if name in FFN_NAMES:
        n, r, c = g8.shape
        return g8.reshape(n // 2, 2 * r, c)
    full = _join(g8[:, None], BIG_SPLIT_COLS[name])[0]
    if name == "fox_w_in":
        full = jnp.pad(full, ((0, 0), (0, LANES - full.shape[0] // HEAD_DIM)))
    return full


def _local_step(x, target, small, shard_of=None, pregathered=None):
    S, D = x.shape
    H = D // HEAD_DIM
    depth = small["ln1_g"].shape[0]
    alpha = (2.0 * depth) ** 0.25
    local = pregathered is not None
    bf_pad = jnp.pad(small["fox_b_f"], ((0, 0), (0, LANES - H)))

    def s5_params(j):
        return {k: small["s5_" + k][j] for k in ("a_re", "a_im", "log_dt", "b_re", "b_im", "c_re", "c_im", "d")}

    if local:
        W = {k: _prepare(k[0], g8) for k, g8 in pregathered.items()}
    else:
        keys = _group(0, "a")
        got = _exchange(Exchange([shard_of(*k) for k in keys], [False] * len(keys)), "gather_first")
        W = {k: _prepare(k[0], g8) for k, g8 in zip(keys, got)}
    saved = []
    h = x
    for i in range(depth):
        j = i // 2
        keys = _group(i, "b") + (_group(i + 1, "a") if i + 1 < depth else ())
        ex = None if local else Exchange([shard_of(*k) for k in keys], [False] * len(keys))
        h, s1 = _ffn_fwd(h, W[("ffn1_w_in", i)], W[("ffn1_w_out", i)], small["ln1_g"][i], small["ln1_b"][i], alpha,
                         f"l{i}_ffn1")
        if i % 2 == 0:
            h, s2, extra = _fox_fwd(h, W[("fox_w_in", j)], bf_pad[j:j + 1], W[("fox_w_o", j)], small["lnm_g"][i],
                                    small["lnm_b"][i], alpha, f"l{i}_fox", ex)
        else:
            h, s2, extra = _s5_block_fwd(h, s5_params(j), W[("s5_w_out", j)], small["lnm_g"][i],
                                         small["lnm_b"][i], alpha, f"l{i}_s5", ex)
        if not local:
            W.update({k: _prepare(k[0], g8) for k, g8 in zip(keys, extra)})
        h, s3 = _ffn_fwd(h, W[("ffn2_w_in", i)], W[("ffn2_w_out", i)], small["ln2_g"][i], small["ln2_b"][i], alpha,
                         f"l{i}_ffn2")
        saved.append((s1, s2, s3))

    dh, loss_part = _loss_fwd_bwd(h, target)

    arrived = {}
    pending = {}
    gs = {k: [None] * small[k].shape[0] for k in SMALL}
    for i in reversed(range(depth)):
        j = i // 2
        s1, s2, s3 = saved[i]
        dh, dw_in, dw_out, gs["ln2_g"][i], gs["ln2_b"][i], got = _ffn_bwd(
            dh, s3, W[("ffn2_w_in", i)], W[("ffn2_w_out", i)], small["ln2_g"][i], alpha, f"l{i}_ffn2",
            None if local else pending)
        arrived.update(pending if local else got)
        pending = {("ffn2_w_in", i): dw_in, ("ffn2_w_out", i): dw_out}
        keys = list(pending)
        ex = None if local else Exchange([pending[k] for k in keys], [True] * len(keys))
        if i % 2 == 0:
            dh, mix, sg, extra = _fox_bwd(dh, s2, W[("fox_w_in", j)], bf_pad[j:j + 1], W[("fox_w_o", j)],
                                          small["lnm_g"][i], alpha, f"l{i}_fox", ex)
        else:
            dh, mix, sg, extra = _s5_block_bwd(dh, s2, s5_params(j), W[("s5_w_out", j)], small["lnm_g"][i], alpha,
                                               f"l{i}_s5", ex)
        arrived.update(zip(keys, [pending[k] for k in keys] if local else extra))
        pending = {(k, j): val for k, val in mix.items()}
        for k, val in sg.items():
            gs[k][i if k in ("lnm_g", "lnm_b") else j] = val
        own = ("ffn1_w_out", i) if (i == 0 and not local) else None
        dh, dw_in, dw_out, gs["ln1_g"][i], gs["ln1_b"][i], got = _ffn_bwd(
            dh, s1, W[("ffn1_w_in", i)], W[("ffn1_w_out", i)], small["ln1_g"][i], alpha, f"l{i}_ffn1",
            None if local else pending, own)
        arrived.update(pending if local else got)
        pending = {("ffn1_w_in", i): dw_in}
        if own is None:
            pending[("ffn1_w_out", i)] = dw_out
    gs = {k: jnp.stack(v) for k, v in gs.items()}
    return loss_part, dh, arrived, pending, gs


def kernel(x, ffn1_w_in, ffn1_w_out, ln1_g, ln1_b, lnm_g, lnm_b, ffn2_w_in, ffn2_w_out, ln2_g, ln2_b, fox_w_in, fox_b_f, fox_w_o, s5_a_re, s5_a_im, s5_log_dt, s5_b_re, s5_b_im, s5_c_re, s5_c_im, s5_d, s5_w_out, loss_target, m_ffn1_w_in, m_ffn1_w_out, m_ln1_g, m_ln1_b, m_lnm_g, m_lnm_b, m_ffn2_w_in, m_ffn2_w_out, m_ln2_g, m_ln2_b, m_fox_w_in, m_fox_b_f, m_fox_w_o, m_s5_a_re, m_s5_a_im, m_s5_log_dt, m_s5_b_re, m_s5_b_im, m_s5_c_re, m_s5_c_im, m_s5_d, m_s5_w_out, v_ffn1_w_in, v_ffn1_w_out, v_ln1_g, v_ln1_b, v_lnm_g, v_lnm_b, v_ffn2_w_in, v_ffn2_w_out, v_ln2_g, v_ln2_b, v_fox_w_in, v_fox_b_f, v_fox_w_o, v_s5_a_re, v_s5_a_im, v_s5_log_dt, v_s5_b_re, v_s5_b_im, v_s5_c_re, v_s5_c_im, v_s5_d, v_s5_w_out):
    args = dict(locals())
    w = {k: args[k] for k in WEIGHTS}
    m = {k: args["m_" + k] for k in WEIGHTS}
    v = {k: args["v_" + k] for k in WEIGHTS}
    small = {k: w[k] for k in SMALL}

    turned = lambda d: {k: jnp.swapaxes(a, 1, 2) if k in ("ffn1_w_in", "ffn2_w_in") else a for k, a in d.items()}
    w, m, v = turned(w), turned(m), turned(v)
    wb = {k: w[k].astype(BF16) for k in BIG}
    loss_part, dx, arrived, last, gs = _local_step(x[0], loss_target[0], small, lambda k, idx: wb[k][idx])
    loss = lax.psum(loss_part, ("x", "y", "c"))

    small_shapes = [w[k].shape for k in SMALL]
    ex = Exchange(list(last.values()) + [_pack_flat([gs[k] for k in SMALL]).astype(BF16)],
                  [True] * len(last) + [False])
    *got, g_small_all = _exchange(ex, "scatter_last_gather_small")
    arrived.update(zip(last, got))
    g_small_flat = _sum8(g_small_all, "sum_small_grads")

    grads, delta, new_m, new_v = {}, {}, {}, {}
    for k in BIG:
        parts = [arrived[(k, l)] for l in range(w[k].shape[0])]
        grads[k], delta[k], new_m[k], new_v[k] = _adamw_recv(parts, w[k], m[k], v[k], "adamw_" + k)
    pk = lambda d: _pack_flat([d[k] for k in SMALL])
    d_, m_, v_ = _adamw(pk(w), g_small_flat, pk(m), pk(v), "adamw_small")
    for dst, flat in ((grads, g_small_flat), (delta, d_), (new_m, m_), (new_v, v_)):
        dst.update(zip(SMALL, _unpack_flat(flat, small_shapes)))
    grads, delta, new_m, new_v = turned(grads), turned(delta), turned(new_m), turned(new_v)

    return (loss, dx[None], *[grads[k] for k in WEIGHTS], *[delta[k] for k in WEIGHTS],
            *[new_m[k] for k in WEIGHTS], *[new_v[k] for k in WEIGHTS])
```

```python
import functools
import math

import jax
import jax.numpy as jnp
from jax import lax
from jax.experimental import pallas as pl
from jax.experimental.pallas import tpu as pltpu

F32 = jnp.float32
BF16 = jnp.bfloat16

N_DEV = 8
HEAD_DIM = 64
S5_GROUP = 16
S5_STATE = 64
LANES = 128
SUBLANES = 8
S5_PART = 256
ATTN_TILE = 512
LN_EPS = 1e-5
NEG_INF = -1e30
LOG2E = 1.4426950408889634
ADAM_LR, ADAM_B1, ADAM_B2, ADAM_EPS, ADAM_WD, ADAM_STEP = 0.001, 0.9, 0.999, 1e-08, 0.01, 10
VMEM_LIMIT = 48 * 1024 * 1024
PACK_COLS = 1024
PACK_ROW_ALIGN = 512

MESH = pl.DeviceIdType.MESH


def _tile(dim, pref, align=LANES):
    if dim <= pref:
        return dim
    t = (pref // align) * align
    while t >= align:
        if dim % t == 0:
            return t
        t -= align
    return dim


def _params(sem):
    return pltpu.CompilerParams(dimension_semantics=sem, vmem_limit_bytes=VMEM_LIMIT)


def _mm(a, b, *, ta=False, tb=False, out_dtype=F32, scale=None, add=None, add_scale=1.0,
        tm=512, tn=1408, tk=1408, name="mm"):
    if ta:
        K, M = a.shape
    else:
        M, K = a.shape
    if tb:
        N, K2 = b.shape
    else:
        K2, N = b.shape
    assert K == K2, (a.shape, b.shape, ta, tb)
    tm, tn, tk = _tile(M, tm), _tile(N, tn), _tile(K, tk)
    a_spec = pl.BlockSpec((tk, tm), lambda i, j, k: (k, i)) if ta else pl.BlockSpec((tm, tk), lambda i, j, k: (i, k))
    b_spec = pl.BlockSpec((tn, tk), lambda i, j, k: (j, k)) if tb else pl.BlockSpec((tk, tn), lambda i, j, k: (k, j))
    o_spec = pl.BlockSpec((tm, tn), lambda i, j, k: (i, j))
    return _mm_core(name, a, b, a_spec, b_spec, o_spec, jax.ShapeDtypeStruct((M, N), out_dtype),
                    (M // tm, N // tn, K // tk), (tm, tn), ta, tb, scale, add, add_scale)


def _mm_core(name, a, b, a_spec, b_spec, o_spec, out_shape, grid, acc_shape, ta, tb, scale=None, add=None,
             add_scale=1.0, ex=None):
    nk = grid[2]

    def body(*refs):
        if add is None:
            a_ref, b_ref, o_ref, acc = refs
            add_ref = None
        else:
            a_ref, b_ref, add_ref, o_ref, acc = refs
        k = pl.program_id(2)

        @pl.when(k == 0)
        def _():
            acc[...] = jnp.zeros_like(acc)

        dims = (((0 if ta else 1,), (1 if tb else 0,)), ((), ()))
        acc[...] += lax.dot_general(a_ref[...].astype(BF16), b_ref[...].astype(BF16), dims,
                                    preferred_element_type=F32)

        @pl.when(k == nk - 1)
        def _():
            r = acc[...]
            if scale is not None:
                r = r * scale
            if add_ref is not None:
                r = r + add_scale * add_ref[...]
            o_ref[...] = r.astype(out_shape.dtype)

    in_specs = [a_spec, b_spec]
    args = [a, b]
    if add is not None:
        in_specs.append(o_spec)
        args.append(add)
    (res,), extra = _call(body, name=name, out_shape=(out_shape,), grid=grid, in_specs=in_specs, out_specs=(o_spec,),
                          scratch_shapes=[pltpu.VMEM(acc_shape, F32)], args=args,
                          semantics=("parallel", "parallel", "arbitrary"), ex=ex)
    return res if ex is None else (res, extra)


def _ffn_h(x, wt, name):
    S, D = x.shape
    n, c, _ = wt.shape
    nb = n // 2
    tm = _tile(S, 1024, SUBLANES)
    dims = (((1,), (1,)), ((), ()))

    def body(x_ref, wg_ref, wu_ref, h_ref, a_ref):
        xb = x_ref[...].astype(BF16)
        g = lax.dot_general(xb, wg_ref[...], dims, preferred_element_type=F32).astype(BF16)
        u = lax.dot_general(xb, wu_ref[...], dims, preferred_element_type=F32).astype(BF16)
        h_ref[0] = g
        h_ref[1] = u
        g = g.astype(F32)
        a_ref[...] = (g * _sigmoid(g) * u.astype(F32)).astype(BF16)

    h, a4 = pl.pallas_call(
        body, name=name,
        out_shape=(jax.ShapeDtypeStruct((2, nb, S, c), BF16), jax.ShapeDtypeStruct((nb, S, c), BF16)),
        grid=(S // tm, nb),
        in_specs=[pl.BlockSpec((tm, D), lambda i, j: (i, 0)), pl.BlockSpec((None, c, D), lambda i, j: (j, 0, 0)),
                  pl.BlockSpec((None, c, D), lambda i, j: (j + nb, 0, 0))],
        out_specs=(pl.BlockSpec((2, None, tm, c), lambda i, j: (0, j, i, 0)),
                   pl.BlockSpec((None, tm, c), lambda i, j: (j, i, 0))),
        compiler_params=_params(("parallel", "parallel")),
    )(x, wt, wt)
    return h.reshape(n, S, c), a4


def _ffn_y_ln(a4, wo4, x, g, b, alpha, s, name):
    nb, S, c = a4.shape
    D = wo4.shape[-1]
    tm = _tile(S, 512, SUBLANES)

    def body(a_ref, w_ref, x_ref, g_ref, b_ref, o_ref, xh_ref, r_ref, acc):
        k = pl.program_id(1)

        @pl.when(k == 0)
        def _():
            acc[...] = jnp.zeros_like(acc)

        acc[...] += jnp.dot(a_ref[...], w_ref[...], preferred_element_type=F32)

        @pl.when(k == nb - 1)
        def _():
            z = alpha * x_ref[...] + s * acc[...]
            mu = jnp.mean(z, axis=-1, keepdims=True)
            zc = z - mu
            rstd = lax.rsqrt(jnp.mean(zc * zc, axis=-1, keepdims=True) + LN_EPS)
            xh = zc * rstd
            xh_ref[...] = xh
            r_ref[...] = rstd
            o_ref[...] = xh * g_ref[...] + b_ref[...]

    row = pl.BlockSpec((tm, D), lambda i, k: (i, 0))
    vec = pl.BlockSpec((1, D), lambda i, k: (0, 0))
    return pl.pallas_call(
        body, name=name,
        out_shape=(jax.ShapeDtypeStruct((S, D), F32), jax.ShapeDtypeStruct((S, D), F32),
                   jax.ShapeDtypeStruct((S, 1), F32)),
        grid=(S // tm, nb),
        in_specs=[pl.BlockSpec((None, tm, c), lambda i, k: (k, i, 0)), pl.BlockSpec((None, c, D), lambda i, k: (k, 0, 0)),
                  row, vec, vec],
        out_specs=(row, row, pl.BlockSpec((tm, 1), lambda i, k: (i, 0))),
        scratch_shapes=[pltpu.VMEM((tm, D), F32)], compiler_params=_params(("parallel", "arbitrary")),
    )(a4, wo4, x, g.reshape(1, D), b.reshape(1, D))


def _ffn_dwout(a4, dz, scale, name, ex=None):
    nb, S, c = a4.shape
    D = dz.shape[1]
    tk, tn = _tile(S, 1024, SUBLANES), _tile(D, 1024)
    return _mm_core(name, a4, dz, pl.BlockSpec((None, tk, c), lambda i, j, k: (i, k, 0)),
                    pl.BlockSpec((tk, tn), lambda i, j, k: (k, j)),
                    pl.BlockSpec((None, c, tn), lambda i, j, k: (i, 0, j)),
                    jax.ShapeDtypeStruct((nb, c, D), BF16), (nb, D // tn, S // tk), (c, tn), True, False, scale,
                    ex=ex)


def _ffn_dh(dz, wo4, h8, scale, name, ex=None):
    S, D = dz.shape
    nb, c = wo4.shape[0], wo4.shape[1]
    tm = _tile(S, 512, SUBLANES)

    def body(dz_ref, w_ref, h_ref, d_ref):
        da = lax.dot_general(dz_ref[...].astype(BF16), w_ref[...], (((1,), (1,)), ((), ())),
                             preferred_element_type=F32) * scale
        g = h_ref[0].astype(F32)
        u = h_ref[1].astype(F32)
        sg = _sigmoid(g)
        silu = g * sg
        d_ref[0] = (da * u * (sg + silu * (1.0 - sg))).astype(BF16)
        d_ref[1] = (da * silu).astype(BF16)

    pair = pl.BlockSpec((2, None, tm, c), lambda i, j: (0, j, i, 0))
    (dh,), extra = _call(
        body, name=name, out_shape=(jax.ShapeDtypeStruct((2, nb, S, c), BF16),), grid=(S // tm, nb),
        in_specs=[pl.BlockSpec((tm, D), lambda i, j: (i, 0)), pl.BlockSpec((None, c, D), lambda i, j: (j, 0, 0)),
                  pair],
        out_specs=(pair,), scratch_shapes=[], args=(dz, wo4, h8.reshape(2, nb, S, c)),
        semantics=("parallel", "parallel"), ex=ex)
    dh = dh.reshape(2 * nb, S, c)
    return dh if ex is None else (dh, extra)


def _ffn_dwin(x, dh8, name, ex=None):
    S, D = x.shape
    n, _, c = dh8.shape
    tk, tn = _tile(S, 1024, SUBLANES), _tile(D, 1024)
    return _mm_core(name, dh8, x, pl.BlockSpec((None, tk, c), lambda i, j, k: (i, k, 0)),
                    pl.BlockSpec((tk, tn), lambda i, j, k: (k, j)),
                    pl.BlockSpec((None, c, tn), lambda i, j, k: (i, 0, j)),
                    jax.ShapeDtypeStruct((n, c, D), BF16), (n, D // tn, S // tk), (c, tn), True, False, ex=ex)


def _ffn_dx(dh8, wt, dz, alpha, name, ex=None):
    n, S, c = dh8.shape
    D = wt.shape[2]
    tm, tn = _tile(S, 1024, SUBLANES), _tile(D, 1024)
    return _mm_core(name, dh8, wt, pl.BlockSpec((None, tm, c), lambda i, j, k: (k, i, 0)),
                    pl.BlockSpec((None, c, tn), lambda i, j, k: (k, 0, j)),
                    pl.BlockSpec((tm, tn), lambda i, j, k: (i, j)),
                    jax.ShapeDtypeStruct((S, D), F32), (S // tm, D // tn, n), (tm, tn), False, False,
                    None, dz, alpha, ex=ex)


def _res_ln_fwd(x, y, g, b, alpha, s, name):
    S, D = x.shape
    tm = _tile(S, 256, SUBLANES)

    def body(x_ref, y_ref, g_ref, b_ref, o_ref, xh_ref, r_ref):
        z = alpha * x_ref[...] + s * y_ref[...]
        mu = jnp.mean(z, axis=-1, keepdims=True)
        zc = z - mu
        var = jnp.mean(zc * zc, axis=-1, keepdims=True)
        rstd = lax.rsqrt(var + LN_EPS)
        xh = zc * rstd
        xh_ref[...] = xh
        r_ref[...] = rstd
        o_ref[...] = xh * g_ref[...] + b_ref[...]

    row = pl.BlockSpec((tm, D), lambda i: (i, 0))
    vec = pl.BlockSpec((1, D), lambda i: (0, 0))
    return pl.pallas_call(
        body, name=name,
        out_shape=(jax.ShapeDtypeStruct((S, D), F32), jax.ShapeDtypeStruct((S, D), F32),
                   jax.ShapeDtypeStruct((S, 1), F32)),
        grid=(S // tm,), in_specs=[row, row, vec, vec],
        out_specs=(row, row, pl.BlockSpec((tm, 1), lambda i: (i, 0))),
        compiler_params=_params(("parallel",)),
    )(x, y, g.reshape(1, D), b.reshape(1, D))


def _ln_bwd(dout, xh, rstd, g, name):
    S, D = dout.shape
    tm = _tile(S, 256, SUBLANES)

    def body(d_ref, xh_ref, r_ref, g_ref, dz_ref, dg_ref, db_ref):
        i = pl.program_id(0)

        @pl.when(i == 0)
        def _():
            dg_ref[...] = jnp.zeros_like(dg_ref)
            db_ref[...] = jnp.zeros_like(db_ref)

        d = d_ref[...]
        xhv = xh_ref[...]
        dxh = d * g_ref[...]
        m1 = jnp.mean(dxh, axis=-1, keepdims=True)
        m2 = jnp.mean(dxh * xhv, axis=-1, keepdims=True)
        dz_ref[...] = r_ref[...] * (dxh - m1 - xhv * m2)
        dg_ref[...] += jnp.sum(d * xhv, axis=0, keepdims=True)
        db_ref[...] += jnp.sum(d, axis=0, keepdims=True)

    row = pl.BlockSpec((tm, D), lambda i: (i, 0))
    vec = pl.BlockSpec((1, D), lambda i: (0, 0))
    dz, dg, db = pl.pallas_call(
        body, name=name,
        out_shape=(jax.ShapeDtypeStruct((S, D), F32), jax.ShapeDtypeStruct((1, D), F32),
                   jax.ShapeDtypeStruct((1, D), F32)),
        grid=(S // tm,), in_specs=[row, row, pl.BlockSpec((tm, 1), lambda i: (i, 0)), vec],
        out_specs=(row, vec, vec),
        compiler_params=_params(("arbitrary",)),
    )(dout, xh, rstd, g.reshape(1, D))
    return dz, dg[0], db[0]


def _sigmoid(x):
    e = jnp.exp(-jnp.abs(x))
    return jnp.where(x >= 0, 1.0 / (1.0 + e), e / (1.0 + e))


def _glu_fwd(vg, name):
    S, D2 = vg.shape
    D = D2 // 2
    tm = _tile(S, 512, SUBLANES)

    def body(v_ref, g_ref, o_ref):
        o_ref[...] = v_ref[...] * _sigmoid(g_ref[...])

    return pl.pallas_call(
        body, name=name, out_shape=jax.ShapeDtypeStruct((S, D), F32), grid=(S // tm,),
        in_specs=[pl.BlockSpec((tm, D), lambda i: (i, 0)), pl.BlockSpec((tm, D), lambda i: (i, 1))],
        out_specs=pl.BlockSpec((tm, D), lambda i: (i, 0)),
        compiler_params=_params(("parallel",)),
    )(vg, vg)


def _glu_bwd(vg, dm, name):
    S, D2 = vg.shape
    D = D2 // 2
    tm = _tile(S, 512, SUBLANES)

    def body(v_ref, g_ref, dm_ref, dv_ref, dg_ref):
        sg = _sigmoid(g_ref[...])
        d = dm_ref[...]
        dv_ref[...] = (d * sg).astype(BF16)
        dg_ref[...] = (d * v_ref[...] * sg * (1.0 - sg)).astype(BF16)

    blk = pl.BlockSpec((tm, D), lambda i: (i, 0))
    dv, dg = pl.pallas_call(
        body, name=name,
        out_shape=(jax.ShapeDtypeStruct((S, D), BF16), jax.ShapeDtypeStruct((S, D), BF16)),
        grid=(S // tm,), in_specs=[blk, pl.BlockSpec((tm, D), lambda i: (i, 1)), blk],
        out_specs=(blk, blk), compiler_params=_params(("parallel",)),
    )(vg, vg, dm)
    return jnp.concatenate([dv, dg], axis=1)


def _loss_fwd_bwd(y, target):
    S, D = y.shape
    tm = _tile(S, 256, SUBLANES)

    def body(y_ref, t_ref, dy_ref, l_ref):
        i = pl.program_id(0)

        @pl.when(i == 0)
        def _():
            l_ref[...] = jnp.zeros_like(l_ref)

        e = y_ref[...] - t_ref[...]
        dy_ref[...] = e * (1.0 / D)
        l_ref[...] += jnp.sum(e * e, axis=0, keepdims=True) * (0.5 / D)

    row = pl.BlockSpec((tm, D), lambda i: (i, 0))
    dy, part = pl.pallas_call(
        body, name="loss", out_shape=(jax.ShapeDtypeStruct((S, D), F32), jax.ShapeDtypeStruct((1, D), F32)),
        grid=(S // tm,), in_specs=[row, row], out_specs=(row, pl.BlockSpec((1, D), lambda i: (0, 0))),
        compiler_params=_params(("arbitrary",)),
    )(y, target)
    return dy, jnp.sum(part)


def _tri(n, lower):
    r = lax.broadcasted_iota(jnp.int32, (n, n), 0)
    c = lax.broadcasted_iota(jnp.int32, (n, n), 1)
    return jnp.where((c <= r) if lower else (c >= r), 1.0, 0.0)


def _fox_gate_fwd(fl, bf):
    S, W = fl.shape
    tm = _tile(S, 256, SUBLANES)

    def body(fl_ref, b_ref, c_ref, carry):
        i = pl.program_id(0)

        @pl.when(i == 0)
        def _():
            carry[...] = jnp.zeros_like(carry)

        x = fl_ref[...] + b_ref[...]
        lf = jnp.minimum(x, 0.0) - jnp.log(1.0 + jnp.exp(-jnp.abs(x)))
        c_ref[...] = jnp.dot(_tri(tm, True), lf, precision=lax.Precision.HIGHEST,
                             preferred_element_type=F32) + carry[...]
        carry[...] += jnp.sum(lf, axis=0, keepdims=True)

    blk = pl.BlockSpec((tm, W), lambda i: (i, 0))
    return pl.pallas_call(
        body, name="fox_gate_fwd", out_shape=jax.ShapeDtypeStruct((S, W), F32), grid=(S // tm,),
        in_specs=[blk, pl.BlockSpec((1, W), lambda i: (0, 0))], out_specs=blk,
        scratch_shapes=[pltpu.VMEM((1, W), F32)], compiler_params=_params(("arbitrary",)),
    )(fl, bf)


def _fox_gate_bwd(dcum, fl, bf):
    S, W = fl.shape
    tm = _tile(S, 256, SUBLANES)
    nb = S // tm

    def body(dc_ref, fl_ref, b_ref, dfl_ref, db_ref, carry):
        i = pl.program_id(0)

        @pl.when(i == 0)
        def _():
            carry[...] = jnp.zeros_like(carry)
            db_ref[...] = jnp.zeros_like(db_ref)

        dc = dc_ref[...]
        r = jnp.dot(_tri(tm, False), dc, precision=lax.Precision.HIGHEST, preferred_element_type=F32) + carry[...]
        carry[...] += jnp.sum(dc, axis=0, keepdims=True)
        x = fl_ref[...] + b_ref[...]
        dfl = r * (1.0 - _sigmoid(x))
        dfl_ref[...] = dfl
        db_ref[...] += jnp.sum(dfl, axis=0, keepdims=True)

    blk = pl.BlockSpec((tm, W), lambda i: (nb - 1 - i, 0))
    vec = pl.BlockSpec((1, W), lambda i: (0, 0))
    return pl.pallas_call(
        body, name="fox_gate_bwd",
        out_shape=(jax.ShapeDtypeStruct((S, W), F32), jax.ShapeDtypeStruct((1, W), F32)),
        grid=(nb,), in_specs=[blk, blk, vec], out_specs=(blk, vec),
        scratch_shapes=[pltpu.VMEM((1, W), F32)], compiler_params=_params(("arbitrary",)),
    )(dcum, fl, bf)


def _causal(t):
    row = lax.broadcasted_iota(jnp.int32, (t, t), 0)
    col = lax.broadcasted_iota(jnp.int32, (t, t), 1)
    return col <= row


def _first_head(shape):
    return lax.broadcasted_iota(jnp.int32, shape, len(shape) - 1) < HEAD_DIM


def _split3(x):
    hi = x.astype(BF16).astype(F32)
    r = x - hi
    mid = r.astype(BF16).astype(F32)
    return hi, mid, (r - mid).astype(BF16).astype(F32)


def _bias_lanes(c, query):
    lane = lax.broadcasted_iota(jnp.int32, (c.shape[0], LANES), 1)
    hi, mid, lo = _split3(c)
    if query:
        out = jnp.where(lane == 0, hi, jnp.where(lane == 1, mid, jnp.where(lane == 2, lo,
                                                                          jnp.where(lane < 6, 1.0, 0.0))))
    else:
        out = jnp.where(lane < 3, 1.0, jnp.where(lane == 3, -hi, jnp.where(lane == 4, -mid,
                                                                          jnp.where(lane == 5, -lo, 0.0))))
    return out.astype(BF16)


def _flash_fwd(proj, cum2, ex=None):
    S = proj.shape[0]
    D = (proj.shape[1] - LANES) // 3
    HP = D // LANES
    t = _tile(S, ATTN_TILE)
    nq = S // t
    scale = 1.0 / math.sqrt(HEAD_DIM)

    def body(q_ref, k_ref, v_ref, cq_ref, ck_ref, o_ref, lse_ref, kx, vb):
        qi = pl.program_id(1)

        @pl.when(qi == 0)
        def _():
            def prep(kb, c):
                sl = pl.ds(pl.multiple_of(kb * t, t), t)
                k16 = k_ref[sl, :].astype(BF16)
                ckv = ck_ref[0, sl, :]
                for a in range(2):
                    kx[a, sl, 0:LANES] = k16
                    kx[a, sl, LANES:2 * LANES] = _bias_lanes(ckv[:, a:a + 1], False)
                vb[sl, :] = v_ref[sl, :].astype(BF16)
                return c

            lax.fori_loop(0, nq, prep, 0)

        first = _first_head((t, LANES))
        qf = q_ref[...] * (scale * LOG2E)
        cqv = cq_ref[0]
        qx = [jnp.concatenate([jnp.where(keep, qf, 0.0).astype(BF16), _bias_lanes(cqv[:, a:a + 1], True)], axis=1)
              for a, keep in enumerate((first, jnp.logical_not(first)))]

        def block(ki, carry, masked):
            m_old, l_old, acc = carry
            sl = pl.ds(pl.multiple_of(ki * t, t), t)
            vv = vb[sl, :]
            m_new, l_new, corr, pv = [], [], [], []
            for a in range(2):
                s = lax.dot_general(qx[a], kx[a, sl, :], (((1,), (1,)), ((), ())), preferred_element_type=F32)
                if masked:
                    s = jnp.where(_causal(t), s, NEG_INF)
                m_a = jnp.maximum(m_old[a], jnp.max(s, axis=1, keepdims=True))
                p = jnp.exp2(s - m_a)
                c_a = jnp.exp2(m_old[a] - m_a)
                m_new.append(m_a)
                corr.append(c_a)
                l_new.append(c_a * l_old[a] + jnp.sum(p, axis=1, keepdims=True))
                pv.append(jnp.dot(p.astype(BF16), vv, preferred_element_type=F32))
            acc = jnp.where(first, corr[0] * acc + pv[0], corr[1] * acc + pv[1])
            return tuple(m_new), tuple(l_new), acc

        neg = jnp.full((t, 1), NEG_INF, F32)
        zero = jnp.zeros((t, 1), F32)
        carry = lax.fori_loop(0, qi, lambda ki, c: block(ki, c, False),
                              ((neg, neg), (zero, zero), jnp.zeros((t, LANES), F32)))
        m, l, acc = block(qi, carry, True)
        o_ref[...] = acc / jnp.where(first, l[0], l[1])
        lse_ref[0, :, 0:1] = m[0] + jnp.log2(l[0])
        lse_ref[0, :, 1:2] = m[1] + jnp.log2(l[1])

    qblk = pl.BlockSpec((t, LANES), lambda h, i: (i, h))
    r2 = pl.BlockSpec((1, t, 2), lambda h, i: (h, i, 0))
    (o, lse), extra = _call(
        body, name="fox_attn_fwd",
        out_shape=(jax.ShapeDtypeStruct((S, D), F32), jax.ShapeDtypeStruct((HP, S, 2), F32)),
        grid=(HP, nq),
        in_specs=[qblk, pl.BlockSpec((S, LANES), lambda h, i: (0, HP + h)),
                  pl.BlockSpec((S, LANES), lambda h, i: (0, 2 * HP + h)), r2,
                  pl.BlockSpec((1, S, 2), lambda h, i: (h, 0, 0))],
        out_specs=(qblk, r2),
        scratch_shapes=[pltpu.VMEM((2, S, 2 * LANES), BF16), pltpu.VMEM((S, LANES), BF16)],
        args=(proj, proj, proj, cum2, cum2), semantics=("parallel", "arbitrary"), ex=ex)
    return o, lse, extra


def _flash_bwd(proj, rows, o, do, ex=None):
    S = proj.shape[0]
    D = (proj.shape[1] - LANES) // 3
    HP = D // LANES
    t = _tile(S, ATTN_TILE)
    nb = S // t
    scale = 1.0 / math.sqrt(HEAD_DIM)

    def body(q_ref, k_ref, v_ref, rows_ref, o_ref, do_ref, dq_ref, dk_ref, dv_ref, dcq_ref, dck_ref,
             q_s, do_s, dl_s, dk_acc, dv_acc, dc_acc):
        kb = pl.program_id(1)

        @pl.when(kb == 0)
        def _():
            dq_ref[...] = jnp.zeros_like(dq_ref)
            dcq_ref[...] = jnp.zeros_like(dcq_ref)

            def prep(qb, c):
                sl = pl.ds(pl.multiple_of(qb * t, t), t)
                first = _first_head((t, LANES))
                qf = q_ref[sl, :] * (scale * LOG2E)
                dof = do_ref[sl, :]
                prod = dof * o_ref[sl, :]
                rv = rows_ref[0, sl, :]
                for a, keep in enumerate((first, jnp.logical_not(first))):
                    q_s[a, sl, 0:LANES] = jnp.where(keep, qf, 0.0).astype(BF16)
                    q_s[a, sl, LANES:2 * LANES] = _bias_lanes(rv[:, a:a + 1], True)
                    do_s[a, sl, :] = jnp.where(keep, dof, 0.0).astype(BF16)
                    dl_s[sl, a:a + 1] = jnp.sum(jnp.where(keep, prod, 0.0), axis=1, keepdims=True)
                return c

            lax.fori_loop(0, nb, prep, 0)

        dk_acc[...] = jnp.zeros_like(dk_acc)
        dv_acc[...] = jnp.zeros_like(dv_acc)
        dc_acc[...] = jnp.zeros_like(dc_acc)
        first = _first_head((t, LANES))
        kf = k_ref[...]
        kk = kf.astype(BF16)
        k_own = (jnp.where(first, kf, 0.0).astype(BF16), jnp.where(first, 0.0, kf).astype(BF16))
        ckv = rows_ref[0, pl.ds(pl.multiple_of(kb * t, t), t), :]
        kx = [jnp.concatenate([kk, _bias_lanes(ckv[:, a:a + 1], False)], axis=1) for a in range(2)]
        vv = v_ref[...].astype(BF16)

        def block(qb, masked):
            sl = pl.ds(pl.multiple_of(qb * t, t), t)
            rv = rows_ref[0, sl, :]
            dlv = dl_s[sl, :]
            dq_new = dq_ref[sl, :]
            for a in range(2):
                dob = do_s[a, sl, :]
                s = lax.dot_general(q_s[a, sl, :], kx[a], (((1,), (1,)), ((), ())), preferred_element_type=F32)
                if masked:
                    s = jnp.where(_causal(t), s, NEG_INF)
                p = jnp.exp2(s - rv[:, 2 + a:3 + a])
                dv_acc[...] += lax.dot_general(p.astype(BF16), dob, (((0,), (0,)), ((), ())),
                                               preferred_element_type=F32)
                dp = lax.dot_general(dob, vv, (((1,), (1,)), ((), ())), preferred_element_type=F32)
                ds = p * (dp - dlv[:, a:a + 1])
                dsb = ds.astype(BF16)
                dk_acc[...] += lax.dot_general(dsb, q_s[a, sl, 0:LANES], (((0,), (0,)), ((), ())),
                                               preferred_element_type=F32)
                dq_new = dq_new + jnp.dot(dsb, k_own[a], preferred_element_type=F32) * scale
                dcq_ref[0, sl, a:a + 1] += jnp.sum(ds, axis=1, keepdims=True)
                dc_acc[a:a + 1, :] -= jnp.sum(ds, axis=0, keepdims=True)
            dq_ref[sl, :] = dq_new

        block(kb, True)

        def rest(qb, c):
            block(qb, False)
            return c

        lax.fori_loop(kb + 1, nb, rest, 0)
        dk_ref[...] = dk_acc[...] * (1.0 / LOG2E)
        dv_ref[...] = dv_acc[...]
        dck_ref[0, 0] = dc_acc[0:1, :]
        dck_ref[1, 0] = dc_acc[1:2, :]

    full = lambda c0: pl.BlockSpec((S, LANES), lambda h, j, c0=c0: (0, c0 + h))
    blk = lambda c0: pl.BlockSpec((t, LANES), lambda h, j, c0=c0: (j, c0 + h))
    f32 = lambda *s: jax.ShapeDtypeStruct(s, F32)
    outs, extra = _call(
        body, name="fox_attn_bwd",
        out_shape=(f32(S, D), f32(S, D), f32(S, D), f32(HP, S, 2), f32(2 * HP, nb, 1, t)),
        grid=(HP, nb),
        in_specs=[full(0), blk(HP), blk(2 * HP), pl.BlockSpec((1, S, 4), lambda h, j: (h, 0, 0)), full(0), full(0)],
        out_specs=(full(0), blk(0), blk(0), pl.BlockSpec((1, S, 2), lambda h, j: (h, 0, 0)),
                   pl.BlockSpec((2, 1, 1, t), lambda h, j: (h, j, 0, 0))),
        scratch_shapes=[pltpu.VMEM((2, S, 2 * LANES), BF16), pltpu.VMEM((2, S, LANES), BF16),
                        pltpu.VMEM((S, 2), F32), pltpu.VMEM((t, LANES), F32), pltpu.VMEM((t, LANES), F32),
                        pltpu.VMEM((2, t), F32)],
        args=(proj, proj, proj, rows, o, do), semantics=("parallel", "arbitrary"), ex=ex)
    return (*outs, extra)


def _s5_consts(T):
    rows = T * SUBLANES
    rr = lax.broadcasted_iota(jnp.int32, (rows, T), 0)
    tt = lax.broadcasted_iota(jnp.int32, (rows, T), 1)
    rep = jnp.where(rr // SUBLANES == tt, 1.0, 0.0).astype(BF16)
    r2 = lax.broadcasted_iota(jnp.int32, (rows, S5_PART), 0)
    c2 = lax.broadcasted_iota(jnp.int32, (rows, S5_PART), 1)
    mask = (c2 // (S5_PART // SUBLANES)) == (r2 % SUBLANES)
    return rep, mask


def _gelu(y):
    c = math.sqrt(2.0 / math.pi)
    return 0.5 * y * (1.0 + jnp.tanh(c * (y + 0.044715 * y * y * y)))


def _gelu_grad(y):
    c = math.sqrt(2.0 / math.pi)
    th = jnp.tanh(c * (y + 0.044715 * y * y * y))
    return 0.5 * (1.0 + th) + 0.5 * y * (1.0 - th * th) * c * (1.0 + 3.0 * 0.044715 * y * y)


def _s5_fwd(x, rmat, cmat, lam, dskip, ex=None):
    S, D = x.shape
    NQ = D // S5_PART
    T = _tile(S, 128, SUBLANES)
    rows = T * SUBLANES

    def body(x_ref, r_ref, c_ref, lam_ref, d_ref, y_ref, yg_ref, h_ref, bu_s, carry):
        i = pl.program_id(0)

        @pl.when(i == 0)
        def _():
            carry[...] = jnp.zeros_like(carry)

        rep, mask = _s5_consts(T)
        cols = [pl.ds(q * S5_PART, S5_PART) for q in range(NQ)]
        for q in range(NQ):
            xrep = jnp.dot(rep, x_ref[:, cols[q]].astype(BF16), preferred_element_type=F32)
            lx = jnp.where(mask, xrep, 0.0).astype(BF16)
            bu_s[q] = jnp.dot(lx, r_ref[q], preferred_element_type=F32)
        lam_v = [(lam_ref[q, :, 0:LANES], lam_ref[q, :, LANES:2 * LANES]) for q in range(NQ)]

        def step(t, c):
            o = pl.multiple_of(t * SUBLANES, SUBLANES)
            new = []
            for q in range(NQ):
                hr, hi = c[q]
                ar, ai = lam_v[q]
                sl = bu_s[q, pl.ds(o, SUBLANES), :]
                nhr = ar * hr - ai * hi + sl[:, 0:LANES]
                nhi = ar * hi + ai * hr + sl[:, LANES:2 * LANES]
                h_ref[q, pl.ds(o, SUBLANES), 0:LANES] = nhr
                h_ref[q, pl.ds(o, SUBLANES), LANES:2 * LANES] = nhi
                new.append((nhr, nhi))
            return tuple(new)

        fin = lax.fori_loop(0, T, step,
                            tuple((carry[q, :, 0:LANES], carry[q, :, LANES:2 * LANES]) for q in range(NQ)))
        for q in range(NQ):
            carry[q, :, 0:LANES] = fin[q][0]
            carry[q, :, LANES:2 * LANES] = fin[q][1]
            z = jnp.dot(h_ref[q].astype(BF16), c_ref[q], preferred_element_type=F32)
            z = jnp.where(mask, z, 0.0)
            y = jnp.sum(z.reshape(T, SUBLANES, S5_PART), axis=1) + d_ref[:, cols[q]] * x_ref[:, cols[q]]
            y_ref[:, cols[q]] = y
            yg_ref[:, cols[q]] = _gelu(y).astype(BF16)

    xs = pl.BlockSpec((T, D), lambda i: (i, 0))
    ms = pl.BlockSpec((NQ, S5_PART, S5_PART), lambda i: (0, 0, 0))
    outs, extra = _call(
        body, name="s5_scan_fwd",
        out_shape=(jax.ShapeDtypeStruct((S, D), F32), jax.ShapeDtypeStruct((S, D), BF16),
                   jax.ShapeDtypeStruct((NQ, S * SUBLANES, S5_PART), F32)),
        grid=(S // T,),
        in_specs=[xs, ms, ms, pl.BlockSpec((NQ, SUBLANES, S5_PART), lambda i: (0, 0, 0)),
                  pl.BlockSpec((1, D), lambda i: (0, 0))],
        out_specs=(xs, xs, pl.BlockSpec((NQ, rows, S5_PART), lambda i: (0, i, 0))),
        scratch_shapes=[pltpu.VMEM((NQ, rows, S5_PART), F32), pltpu.VMEM((NQ, SUBLANES, S5_PART), F32)],
        args=(x, rmat, cmat, lam, dskip), semantics=("arbitrary",), ex=ex)
    return (*outs, extra)


def _s5_bwd(x, y, dyg, hs, rmat, cmat, lam, dskip, res, res_scale, ex=None):
    S, D = x.shape
    NQ = D // S5_PART
    T = _tile(S, 128, SUBLANES)
    nb = S // T
    rows = T * SUBLANES

    def body(x_ref, y_ref, dyg_ref, res_ref, h_ref, hp_ref, r_ref, c_ref, lam_ref, d_ref,
             dx_ref, dr_ref, dc_ref, dlam_ref, dd_ref, dh_s, g_s, hs_s, carry):
        i = pl.program_id(0)

        @pl.when(i == 0)
        def _():
            carry[...] = jnp.zeros_like(carry)
            dr_ref[...] = jnp.zeros_like(dr_ref)
            dc_ref[...] = jnp.zeros_like(dc_ref)
            dlam_ref[...] = jnp.zeros_like(dlam_ref)
            dd_ref[...] = jnp.zeros_like(dd_ref)

        rep, mask = _s5_consts(T)
        cols = [pl.ds(q * S5_PART, S5_PART) for q in range(NQ)]
        dys, ldys = [], []
        for q in range(NQ):
            dy = dyg_ref[:, cols[q]] * _gelu_grad(y_ref[:, cols[q]])
            dyrep = jnp.dot(rep, dy.astype(BF16), preferred_element_type=F32)
            ldy = jnp.where(mask, dyrep, 0.0).astype(BF16)
            dh_s[q] = lax.dot_general(ldy, c_ref[q], (((1,), (1,)), ((), ())), preferred_element_type=F32)
            dys.append(dy)
            ldys.append(ldy)
        lam_v = [(lam_ref[q, :, 0:LANES], lam_ref[q, :, LANES:2 * LANES]) for q in range(NQ)]

        def step(n, c):
            o = pl.multiple_of((T - 1 - n) * SUBLANES, SUBLANES)
            new = []
            for q in range(NQ):
                gr, gi = c[q]
                ar, ai = lam_v[q]
                sl = dh_s[q, pl.ds(o, SUBLANES), :]
                ngr = sl[:, 0:LANES] + ar * gr + ai * gi
                ngi = sl[:, LANES:2 * LANES] - ai * gr + ar * gi
                g_s[q, pl.ds(o, SUBLANES), 0:LANES] = ngr
                g_s[q, pl.ds(o, SUBLANES), LANES:2 * LANES] = ngi
                new.append((ngr, ngi))
            return tuple(new)

        fin = lax.fori_loop(0, T, step,
                            tuple((carry[q, :, 0:LANES], carry[q, :, LANES:2 * LANES]) for q in range(NQ)))
        for q in range(NQ):
            carry[q, :, 0:LANES] = fin[q][0]
            carry[q, :, LANES:2 * LANES] = fin[q][1]
            xv = x_ref[:, cols[q]]
            hv = h_ref[q]
            hs_s[0:SUBLANES, :] = jnp.where(i == nb - 1, 0.0, hp_ref[q])
            hs_s[SUBLANES:rows + SUBLANES, :] = hv
            hprev = hs_s[0:rows, :]
            gv = g_s[q]
            g_re, g_im = gv[:, 0:LANES], gv[:, LANES:2 * LANES]
            hp_re, hp_im = hprev[:, 0:LANES], hprev[:, LANES:2 * LANES]
            dar = jnp.sum((g_re * hp_re + g_im * hp_im).reshape(T, SUBLANES, LANES), axis=0)
            dai = jnp.sum((g_im * hp_re - g_re * hp_im).reshape(T, SUBLANES, LANES), axis=0)
            dlam_ref[q, :, 0:LANES] += dar
            dlam_ref[q, :, LANES:2 * LANES] += dai

            gb = gv.astype(BF16)
            xrep = jnp.dot(rep, xv.astype(BF16), preferred_element_type=F32)
            lx = jnp.where(mask, xrep, 0.0).astype(BF16)
            dr_ref[q] += lax.dot_general(lx, gb, (((0,), (0,)), ((), ())), preferred_element_type=F32)
            dc_ref[q] += lax.dot_general(hv.astype(BF16), ldys[q], (((0,), (0,)), ((), ())),
                                         preferred_element_type=F32)
            zx = lax.dot_general(gb, r_ref[q], (((1,), (1,)), ((), ())), preferred_element_type=F32)
            zx = jnp.where(mask, zx, 0.0)
            dx_ref[:, cols[q]] = (jnp.sum(zx.reshape(T, SUBLANES, S5_PART), axis=1) + d_ref[:, cols[q]] * dys[q]
                                  + res_scale * res_ref[:, cols[q]])
            dd_ref[:, cols[q]] += jnp.sum(dys[q] * xv, axis=0, keepdims=True)

    xs = pl.BlockSpec((T, D), lambda i: (nb - 1 - i, 0))
    ms = pl.BlockSpec((NQ, S5_PART, S5_PART), lambda i: (0, 0, 0))
    ls = pl.BlockSpec((NQ, SUBLANES, S5_PART), lambda i: (0, 0, 0))
    ds_ = pl.BlockSpec((1, D), lambda i: (0, 0))
    outs, extra = _call(
        body, name="s5_scan_bwd",
        out_shape=(jax.ShapeDtypeStruct((S, D), F32), jax.ShapeDtypeStruct((NQ, S5_PART, S5_PART), F32),
                   jax.ShapeDtypeStruct((NQ, S5_PART, S5_PART), F32),
                   jax.ShapeDtypeStruct((NQ, SUBLANES, S5_PART), F32), jax.ShapeDtypeStruct((1, D), F32)),
        grid=(nb,),
        in_specs=[xs, xs, xs, xs, pl.BlockSpec((NQ, rows, S5_PART), lambda i: (0, nb - 1 - i, 0)),
                  pl.BlockSpec((NQ, SUBLANES, S5_PART), lambda i: (0, jnp.maximum((nb - 1 - i) * T - 1, 0), 0)),
                  ms, ms, ls, ds_],
        out_specs=(xs, ms, ms, ls, ds_),
        scratch_shapes=[pltpu.VMEM((NQ, rows, S5_PART), F32), pltpu.VMEM((NQ, rows, S5_PART), F32),
                        pltpu.VMEM((rows + SUBLANES, S5_PART), F32), pltpu.VMEM((NQ, SUBLANES, S5_PART), F32)],
        args=(x, y, dyg, res, hs, hs, rmat, cmat, lam, dskip), semantics=("arbitrary",), ex=ex)
    return (*outs, extra)


def _s5_discretise(a_re, a_im, log_dt, b_re, b_im):
    dt = jnp.exp(log_dt)[:, None]
    mag = jnp.exp(a_re * dt)
    ang = a_im * dt
    lb_re = mag * jnp.cos(ang)
    lb_im = mag * jnp.sin(ang)
    den = a_re * a_re + a_im * a_im
    nr = lb_re - 1.0
    ni = lb_im
    z_re = (nr * a_re + ni * a_im) / den
    z_im = (ni * a_re - nr * a_im) / den
    bb_re = z_re[..., None] * b_re - z_im[..., None] * b_im
    bb_im = z_re[..., None] * b_im + z_im[..., None] * b_re
    return lb_re, lb_im, bb_re, bb_im


def _s5_expand(w):
    G = w.shape[0]
    NQ = G // 16
    base = w.reshape(NQ, S5_PART, S5_STATE)
    half = (jnp.arange(S5_PART) // S5_GROUP) % 2
    sel = (half[:, None] == jnp.arange(2)[None, :]).astype(w.dtype)
    out = base[:, :, None, :] * sel[None, :, :, None]
    return out.reshape(NQ, S5_PART, 2 * S5_STATE)


def _s5_extract(m):
    NQ = m.shape[0]
    half = (jnp.arange(S5_PART) // S5_GROUP) % 2
    sel = (half[:, None] == jnp.arange(2)[None, :]).astype(m.dtype)
    base = jnp.sum(m.reshape(NQ, S5_PART, 2, S5_STATE) * sel[None, :, :, None], axis=2)
    return base.reshape(NQ * 16, S5_GROUP, S5_STATE)


def _s5_slab(v):
    return v.reshape(v.shape[0] // 16, SUBLANES, LANES)


def _adamw(w, g, m, v, name):
    R, C = w.shape
    tr = _tile(R, 256, SUBLANES)
    c1 = 1.0 / (1.0 - ADAM_B1 ** ADAM_STEP)
    c2 = 1.0 / (1.0 - ADAM_B2 ** ADAM_STEP)

    def body(w_ref, g_ref, m_ref, v_ref, d_ref, nm_ref, nv_ref):
        gv = g_ref[...]
        nm = ADAM_B1 * m_ref[...] + (1.0 - ADAM_B1) * gv
        nv = ADAM_B2 * v_ref[...] + (1.0 - ADAM_B2) * (gv * gv)
        nm_ref[...] = nm
        nv_ref[...] = nv
        d_ref[...] = -ADAM_LR * ((nm * c1) / (jnp.sqrt(nv * c2) + ADAM_EPS) + ADAM_WD * w_ref[...])

    blk = pl.BlockSpec((tr, C), lambda i: (i, 0))
    sh = jax.ShapeDtypeStruct((R, C), F32)
    return pl.pallas_call(
        body, name=name, out_shape=(sh, sh, sh), grid=(R // tr,), in_specs=[blk] * 4, out_specs=(blk,) * 3,
        compiler_params=_params(("parallel",)),
    )(w, g, m, v)


def _adamw_recv(parts, w, m, v, name):
    L, R, C = w.shape
    tr = _tile(R, 256, SUBLANES)
    c1 = 1.0 / (1.0 - ADAM_B1 ** ADAM_STEP)
    c2 = 1.0 / (1.0 - ADAM_B2 ** ADAM_STEP)

    def body(*refs):
        p_refs = refs[:L]
        w_ref, m_ref, v_ref, g_ref, d_ref, nm_ref, nv_ref = refs[L:]
        li = pl.program_id(0)
        for l in range(L):
            @pl.when(li == l)
            def _(p_ref=p_refs[l]):
                gv = p_ref[0].astype(F32)
                for k in range(1, N_DEV):
                    gv = gv + p_ref[k].astype(F32)
                g_ref[...] = gv
                nm = ADAM_B1 * m_ref[...] + (1.0 - ADAM_B1) * gv
                nv = ADAM_B2 * v_ref[...] + (1.0 - ADAM_B2) * (gv * gv)
                nm_ref[...] = nm
                nv_ref[...] = nv
                d_ref[...] = -ADAM_LR * ((nm * c1) / (jnp.sqrt(nv * c2) + ADAM_EPS) + ADAM_WD * w_ref[...])

    p_specs = [pl.BlockSpec((N_DEV, tr, C), lambda li, i, l=l: (0, jnp.where(li == l, i, 0), 0)) for l in range(L)]
    blk = pl.BlockSpec((None, tr, C), lambda li, i: (li, i, 0))
    sh = jax.ShapeDtypeStruct((L, R, C), F32)
    return pl.pallas_call(
        body, name=name, out_shape=(sh, sh, sh, sh), grid=(L, R // tr),
        in_specs=p_specs + [blk, blk, blk], out_specs=(blk,) * 4,
        compiler_params=_params(("parallel", "parallel")),
    )(*parts, w, m, v)


def _sum8(parts, name):
    _, R, C = parts.shape
    tr = _tile(R, 256, SUBLANES)

    def body(p_ref, o_ref):
        acc = p_ref[0].astype(F32)
        for k in range(1, N_DEV):
            acc = acc + p_ref[k].astype(F32)
        o_ref[...] = acc

    return pl.pallas_call(
        body, name=name, out_shape=jax.ShapeDtypeStruct((R, C), F32), grid=(R // tr,),
        in_specs=[pl.BlockSpec((N_DEV, tr, C), lambda i: (0, i, 0))],
        out_specs=pl.BlockSpec((tr, C), lambda i: (i, 0)), compiler_params=_params(("parallel",)),
    )(parts)


def _peers():
    x, y, c = lax.axis_index("x"), lax.axis_index("y"), lax.axis_index("c")
    me = 4 * x + 2 * y + c
    out = []
    for k in range(1, N_DEV):
        kx, ky, kc = (k >> 2) & 1, (k >> 1) & 1, k & 1
        px, py, pc = x ^ kx, y ^ ky, c ^ kc
        out.append(((px, py, pc), 4 * px + 2 * py + pc))
    return me, out


SIBLING = 1
SAME_CORE = (2, 4, 6)


class Exchange:
    def __init__(self, xs, scatter):
        self.xs = list(xs)
        self.scatter = list(scatter)
        self.n = len(self.xs)

    def out_shapes(self):
        return tuple(jax.ShapeDtypeStruct(x.shape if sc else (N_DEV,) + x.shape, x.dtype)
                     for x, sc in zip(self.xs, self.scatter))

    def sems(self):
        return [pltpu.SemaphoreType.DMA((self.n * N_DEV,)), pltpu.SemaphoreType.DMA((self.n * N_DEV,)),
                pltpu.SemaphoreType.DMA((self.n,))]

    def _copy(self, i, m, src, dst, to, send_sems, recv_sems):
        return pltpu.make_async_remote_copy(src_ref=src, dst_ref=dst, send_sem=send_sems.at[i * N_DEV + m],
                                            recv_sem=recv_sems.at[i * N_DEV + m], device_id=to, device_id_type=MESH)

    def start(self, x_refs, o_refs, send_sems, recv_sems, local_sems):
        me, peers = _peers()
        for i, (x_ref, o_ref, sc) in enumerate(zip(x_refs, o_refs, self.scatter)):
            pltpu.make_async_copy(x_ref.at[me] if sc else x_ref, o_ref.at[me], local_sems.at[i]).start()
            for m in (range(1, N_DEV) if sc else (SIBLING,) + SAME_CORE):
                dev, idx = peers[m - 1]
                self._copy(i, m, x_ref.at[idx] if sc else x_ref, o_ref.at[me], dev, send_sems, recv_sems).start()

    def middle(self, x_refs, o_refs, send_sems, recv_sems, local_sems):
        me, peers = _peers()
        sibling = peers[SIBLING - 1][0]
        for i, (x_ref, o_ref, sc) in enumerate(zip(x_refs, o_refs, self.scatter)):
            if sc:
                continue
            for m in SAME_CORE:
                dev, idx = peers[m - 1]
                self._copy(i, m, x_ref, o_ref.at[idx], dev, send_sems, recv_sems).wait_recv()
                self._copy(i, m ^ 1, o_ref.at[idx], o_ref.at[idx], sibling, send_sems, recv_sems).start()

    def wait(self, x_refs, o_refs, send_sems, recv_sems, local_sems):
        me, peers = _peers()
        for i, (x_ref, o_ref, sc) in enumerate(zip(x_refs, o_refs, self.scatter)):
            for m in range(1, N_DEV):
                dev, idx = peers[m - 1]
                cp = self._copy(i, m, x_ref.at[idx] if sc else x_ref, o_ref.at[idx], dev, send_sems, recv_sems)
                if sc or m not in SAME_CORE:
                    cp.wait_recv()
                cp.wait_send()
            pltpu.make_async_copy(x_ref.at[me] if sc else x_ref, o_ref.at[me], local_sems.at[i]).wait()


def _exchange(ex, name):
    n = ex.n

    def body(*refs):
        x_refs, o_refs, sems = refs[:n], refs[n:2 * n], refs[2 * n:]
        ex.start(x_refs, o_refs, *sems)
        ex.middle(x_refs, o_refs, *sems)
        ex.wait(x_refs, o_refs, *sems)

    hbm = pl.BlockSpec(memory_space=pltpu.HBM)
    return pl.pallas_call(body, name=name, out_shape=ex.out_shapes(), in_specs=[hbm] * n, out_specs=(hbm,) * n,
                          scratch_shapes=ex.sems())(*ex.xs)


def _call(body, *, name, out_shape, grid, in_specs, out_specs, scratch_shapes, args, semantics, ex=None):
    if ex is None:
        return pl.pallas_call(body, name=name, out_shape=out_shape, grid=grid, in_specs=in_specs, out_specs=out_specs,
                              scratch_shapes=scratch_shapes, compiler_params=_params(semantics))(*args), ()
    n, ni, no, ns = ex.n, len(args), len(out_shape), len(scratch_shapes)

    def wrapped(*refs):
        ins, cx = refs[:ni], refs[ni:ni + n]
        outs, co = refs[ni + n:ni + n + no], refs[ni + n + no:ni + 2 * n + no]
        scratch, sems = refs[ni + 2 * n + no:ni + 2 * n + no + ns], refs[ni + 2 * n + no + ns:]
        step = functools.reduce(lambda acc, a: acc * grid[a] + pl.program_id(a), range(len(grid)), 0)
        steps = math.prod(grid)

        @pl.when(step == 0)
        def _():
            ex.start(cx, co, *sems)

        body(*ins, *outs, *scratch)

        @pl.when(step == (steps * 3) // 4 - (steps > 1))
        def _():
            ex.middle(cx, co, *sems)

        @pl.when(step == steps - 1)
        def _():
            ex.wait(cx, co, *sems)

    hbm = pl.BlockSpec(memory_space=pltpu.HBM)
    res = pl.pallas_call(
        wrapped, name=name, out_shape=tuple(out_shape) + ex.out_shapes(), grid=grid,
        in_specs=list(in_specs) + [hbm] * n, out_specs=tuple(out_specs) + (hbm,) * n,
        scratch_shapes=list(scratch_shapes) + ex.sems(),
        compiler_params=_params(("arbitrary",) * len(grid)))(*args, *ex.xs)
    return res[:no], res[no:]


def _pack_flat(arrs):
    cat = jnp.concatenate([a.reshape(-1) for a in arrs])
    per = PACK_COLS * 2 * SUBLANES
    tot = -(-cat.shape[0] // per) * per
    return jnp.pad(cat, (0, tot - cat.shape[0])).reshape(tot // PACK_COLS, PACK_COLS)


def _unpack_flat(packed, shapes):
    flat = packed.reshape(-1)
    out, o = [], 0
    for s in shapes:
        n = math.prod(s)
        out.append(flat[o:o + n].reshape(s))
        o += n
    return out


def _ffn_fwd(x, wt, wo4, g, b, alpha, tag):
    h8, a4 = _ffn_h(x, wt, tag + "_h")
    out, xh, rstd = _ffn_y_ln(a4, wo4, x, g, b, alpha, 0.5, tag + "_y_ln")
    return out, (x, h8, a4, xh, rstd)


def _ffn_bwd(dout, saved, wt, wo4, g, alpha, tag, carry=None, own_key=None):
    x, h8, a4, xh, rstd = saved
    load = dict(carry or {})
    slots = [[], [], [], []]
    for n, k in enumerate(sorted(load, key=lambda k: -load[k].size)):
        slots[min(n, 3)].append(k)
    arrived = {}

    def run(fn, slot, *args):
        keys = slots[slot]
        if not keys:
            return fn(*args)
        res, extra = fn(*args, ex=Exchange([load[k] for k in keys], [True] * len(keys)))
        arrived.update(zip(keys, extra))
        return res

    dz, dg, db = _ln_bwd(dout, xh, rstd, g, tag + "_ln_bwd")
    dw_out = run(_ffn_dwout, 3, a4, dz, 0.5, tag + "_dwout")
    dw_out = dw_out.reshape(N_DEV, -1, dw_out.shape[-1])
    if own_key is not None:
        load[own_key] = dw_out
        slots[1].append(own_key)
    dh8 = run(_ffn_dh, 2, dz, wo4, h8, 0.5, tag + "_dh")
    dw_in = run(_ffn_dwin, 0, x, dh8, tag + "_dwin")
    dx = run(_ffn_dx, 1, dh8, wt, dz, alpha, tag + "_dx")
    return dx, dw_in, dw_out, dg, db, arrived


def _fox_fwd(x, w_in_pad, b_f_pad, w_o, g, b, alpha, tag, ex=None):
    S, D = x.shape
    H = D // HEAD_DIM
    proj = _mm(x, w_in_pad, name=tag + "_proj")
    fl = proj[:, 3 * D:]
    cum = _fox_gate_fwd(fl, b_f_pad)
    cum2 = (cum[:, :H].T * LOG2E).reshape(H // 2, 2, S).transpose(0, 2, 1)
    o, lse, extra = _flash_fwd(proj, cum2, ex)
    m = _mm(o, w_o, name=tag + "_out")
    out, xh, rstd = _res_ln_fwd(x, m, g, b, alpha, 1.0, tag + "_ln")
    return out, (x, proj, jnp.concatenate([cum2, lse], axis=2), o, fl, xh, rstd), extra


def _fox_bwd(dout, saved, w_in_pad, b_f_pad, w_o, g, alpha, tag, ex=None):
    x, proj, rows, o, fl, xh, rstd = saved
    S, D = x.shape
    H = D // HEAD_DIM
    dz, dg, db = _ln_bwd(dout, xh, rstd, g, tag + "_ln_bwd")
    dw_o = _mm(o, dz, ta=True, name=tag + "_dwo")
    do = _mm(dz, w_o, tb=True, name=tag + "_do")
    dq, dk, dv, dcq, dck, extra = _flash_bwd(proj, rows, o, do, ex)
    dcq = dcq.transpose(0, 2, 1).reshape(H, S)
    dcum = jnp.pad((dcq + dck.reshape(H, S)).T, ((0, 0), (0, LANES - H)))
    dfl, dbf = _fox_gate_bwd(dcum, fl, b_f_pad)
    dproj = jnp.concatenate([dq.astype(BF16), dk.astype(BF16), dv.astype(BF16), dfl.astype(BF16)], axis=1)
    dw_in = _mm(x, dproj, ta=True, name=tag + "_dwin")
    dx = _mm(dproj, w_in_pad, tb=True, add=dz, add_scale=alpha, name=tag + "_dx")
    shards = {"fox_w_in": _split(dw_in[None, :, :3 * D + H], True)[:, 0].astype(BF16),
              "fox_w_o": _split(dw_o[None], False)[:, 0].astype(BF16)}
    return dx, shards, {"fox_b_f": dbf[0, :H], "lnm_g": dg, "lnm_b": db}, extra


def _s5_mats(p):
    lb_re, lb_im, bb_re, bb_im = _s5_discretise(p["a_re"], p["a_im"], p["log_dt"], p["b_re"], p["b_im"])
    rmat = jnp.concatenate([_s5_expand(bb_re.transpose(0, 2, 1)), _s5_expand(bb_im.transpose(0, 2, 1))], axis=2)
    cmat = jnp.concatenate([_s5_expand(p["c_re"]).transpose(0, 2, 1), -_s5_expand(p["c_im"]).transpose(0, 2, 1)],
                           axis=1)
    lam = jnp.concatenate([_s5_slab(lb_re), _s5_slab(lb_im)], axis=2)
    return rmat.astype(BF16), cmat.astype(BF16), lam


def _s5_block_fwd(x, p, w_out, g, b, alpha, tag, ex=None):
    S, D = x.shape
    rmat, cmat, lam = _s5_mats(p)
    dskip = p["d"].reshape(1, D)
    y, yg, hs, extra = _s5_fwd(x, rmat, cmat, lam, dskip, ex)
    vg = _mm(yg, w_out, name=tag + "_vg")
    m = _glu_fwd(vg, tag + "_glu")
    out, xh, rstd = _res_ln_fwd(x, m, g, b, alpha, 1.0, tag + "_ln")
    return out, (x, y, yg, hs, vg, rmat, cmat, lam, dskip, xh, rstd), extra


def _s5_block_bwd(dout, saved, p, w_out, g, alpha, tag, ex=None):
    x, y, yg, hs, vg, rmat, cmat, lam, dskip, xh, rstd = saved
    S, D = x.shape
    G = D // S5_GROUP
    dz, dg, db = _ln_bwd(dout, xh, rstd, g, tag + "_ln_bwd")
    dvg = _glu_bwd(vg, dz, tag + "_glu_bwd")
    dw_out = _mm(yg, dvg, ta=True, name=tag + "_dwout")
    dyg = _mm(dvg, w_out, tb=True, name=tag + "_dyg")
    dx, dr, dc, dlam, dd, extra = _s5_bwd(x, y, dyg, hs, rmat, cmat, lam, dskip, dz, alpha, ex)
    dbb_re = _s5_extract(dr[:, :, :LANES]).transpose(0, 2, 1)
    dbb_im = _s5_extract(dr[:, :, LANES:]).transpose(0, 2, 1)
    dc_re = _s5_extract(dc[:, :LANES, :].transpose(0, 2, 1))
    dc_im = -_s5_extract(dc[:, LANES:, :].transpose(0, 2, 1))
    dlb_re = dlam[:, :, :LANES].reshape(G, S5_STATE)
    dlb_im = dlam[:, :, LANES:].reshape(G, S5_STATE)
    _, vjp = jax.vjp(_s5_discretise, p["a_re"], p["a_im"], p["log_dt"], p["b_re"], p["b_im"])
    da_re, da_im, dlog_dt, db_re, db_im = vjp((dlb_re, dlb_im, dbb_re, dbb_im))
    small = dict(s5_a_re=da_re, s5_a_im=da_im, s5_log_dt=dlog_dt, s5_b_re=db_re, s5_b_im=db_im, s5_c_re=dc_re,
                 s5_c_im=dc_im, s5_d=dd.reshape(G, S5_GROUP), lnm_g=dg, lnm_b=db)
    return dx, {"s5_w_out": _split(dw_out[None], True)[:, 0].astype(BF16)}, small, extra


FFN_NAMES = ("ffn1_w_in", "ffn1_w_out", "ffn2_w_in", "ffn2_w_out")
BIG = FFN_NAMES + ("fox_w_in", "fox_w_o", "s5_w_out")
BIG_SPLIT_COLS = {"ffn1_w_in": True, "ffn1_w_out": False, "ffn2_w_in": True, "ffn2_w_out": False,
                  "fox_w_in": True, "fox_w_o": False, "s5_w_out": True}
SMALL = ("ln1_g", "ln1_b", "lnm_g", "lnm_b", "ln2_g", "ln2_b", "fox_b_f", "s5_a_re", "s5_a_im", "s5_log_dt",
         "s5_b_re", "s5_b_im", "s5_c_re", "s5_c_im", "s5_d")
WEIGHTS = ("ffn1_w_in", "ffn1_w_out", "ln1_g", "ln1_b", "lnm_g", "lnm_b", "ffn2_w_in", "ffn2_w_out", "ln2_g", "ln2_b",
           "fox_w_in", "fox_b_f", "fox_w_o", "s5_a_re", "s5_a_im", "s5_log_dt", "s5_b_re", "s5_b_im", "s5_c_re",
           "s5_c_im", "s5_d", "s5_w_out")


def _join(gathered, split_cols):
    n, L, r, c = gathered.shape
    if split_cols:
        return gathered.transpose(1, 2, 0, 3).reshape(L, r, n * c)
    return gathered.transpose(1, 0, 2, 3).reshape(L, n * r, c)


def _split(full, split_cols):
    L, R, C = full.shape
    if split_cols:
        return full.reshape(L, R, N_DEV, C // N_DEV).transpose(2, 0, 1, 3)
    return full.reshape(L, N_DEV, R // N_DEV, C).transpose(1, 0, 2, 3)


def _group(i, part):
    if part == "b":
        return (("ffn2_w_in", i), ("ffn2_w_out", i))
    mixer = (("fox_w_in", i // 2), ("fox_w_o", i // 2)) if i % 2 == 0 else (("s5_w_out", i // 2),)
    return (("ffn1_w_in", i), ("ffn1_w_out", i)) + mixer


def _prepare(name, g8):
    if name in ("ffn1_w_in", "ffn2_w_in"):
        return g8
    if name in FFN_NAMES:
        n, r, c = g8.shape
        return g8.reshape(n // 2, 2 * r, c)
    full = _join(g8[:, None], BIG_SPLIT_COLS[name])[0]
    if name == "fox_w_in":
        full = jnp.pad(full, ((0, 0), (0, LANES - full.shape[0] // HEAD_DIM)))
    return full


def _local_step(x, target, small, shard_of=None, pregathered=None):
    S, D = x.shape
    H = D // HEAD_DIM
    depth = small["ln1_g"].shape[0]
    alpha = (2.0 * depth) ** 0.25
    local = pregathered is not None
    bf_pad = jnp.pad(small["fox_b_f"], ((0, 0), (0, LANES - H)))

    def s5_params(j):
        return {k: small["s5_" + k][j] for k in ("a_re", "a_im", "log_dt", "b_re", "b_im", "c_re", "c_im", "d")}

    if local:
        W = {k: _prepare(k[0], g8) for k, g8 in pregathered.items()}
    else:
        keys = _group(0, "a")
        got = _exchange(Exchange([shard_of(*k) for k in keys], [False] * len(keys)), "gather_first")
        W = {k: _prepare(k[0], g8) for k, g8 in zip(keys, got)}
    saved = []
    h = x
    for i in range(depth):
        j = i // 2
        keys = _group(i, "b") + (_group(i + 1, "a") if i + 1 < depth else ())
        ex = None if local else Exchange([shard_of(*k) for k in keys], [False] * len(keys))
        h, s1 = _ffn_fwd(h, W[("ffn1_w_in", i)], W[("ffn1_w_out", i)], small["ln1_g"][i], small["ln1_b"][i], alpha,
                         f"l{i}_ffn1")
        if i % 2 == 0:
            h, s2, extra = _fox_fwd(h, W[("fox_w_in", j)], bf_pad[j:j + 1], W[("fox_w_o", j)], small["lnm_g"][i],
                                    small["lnm_b"][i], alpha, f"l{i}_fox", ex)
        else:
            h, s2, extra = _s5_block_fwd(h, s5_params(j), W[("s5_w_out", j)], small["lnm_g"][i],
                                         small["lnm_b"][i], alpha, f"l{i}_s5", ex)
        if not local:
            W.update({k: _prepare(k[0], g8) for k, g8 in zip(keys, extra)})
        h, s3 = _ffn_fwd(h, W[("ffn2_w_in", i)], W[("ffn2_w_out", i)], small["ln2_g"][i], small["ln2_b"][i], alpha,
                         f"l{i}_ffn2")
        saved.append((s1, s2, s3))

    dh, loss_part = _loss_fwd_bwd(h, target)

    arrived = {}
    pending = {}
    gs = {k: [None] * small[k].shape[0] for k in SMALL}
    for i in reversed(range(depth)):
        j = i // 2
        s1, s2, s3 = saved[i]
        dh, dw_in, dw_out, gs["ln2_g"][i], gs["ln2_b"][i], got = _ffn_bwd(
            dh, s3, W[("ffn2_w_in", i)], W[("ffn2_w_out", i)], small["ln2_g"][i], alpha, f"l{i}_ffn2",
            None if local else pending)
        arrived.update(pending if local else got)
        pending = {("ffn2_w_in", i): dw_in, ("ffn2_w_out", i): dw_out}
        keys = list(pending)
        ex = None if local else Exchange([pending[k] for k in keys], [True] * len(keys))
        if i % 2 == 0:
            dh, mix, sg, extra = _fox_bwd(dh, s2, W[("fox_w_in", j)], bf_pad[j:j + 1], W[("fox_w_o", j)],
                                          small["lnm_g"][i], alpha, f"l{i}_fox", ex)
        else:
            dh, mix, sg, extra = _s5_block_bwd(dh, s2, s5_params(j), W[("s5_w_out", j)], small["lnm_g"][i], alpha,
                                               f"l{i}_s5", ex)
        arrived.update(zip(keys, [pending[k] for k in keys] if local else extra))
        pending = {(k, j): val for k, val in mix.items()}
        for k, val in sg.items():
            gs[k][i if k in ("lnm_g", "lnm_b") else j] = val
        own = ("ffn1_w_out", i) if (i == 0 and not local) else None
        dh, dw_in, dw_out, gs["ln1_g"][i], gs["ln1_b"][i], got = _ffn_bwd(
            dh, s1, W[("ffn1_w_in", i)], W[("ffn1_w_out", i)], small["ln1_g"][i], alpha, f"l{i}_ffn1",
            None if local else pending, own)
        arrived.update(pending if local else got)
        pending = {("ffn1_w_in", i): dw_in}
        if own is None:
            pending[("ffn1_w_out", i)] = dw_out
    gs = {k: jnp.stack(v) for k, v in gs.items()}
    return loss_part, dh, arrived, pending, gs


def kernel(x, ffn1_w_in, ffn1_w_out, ln1_g, ln1_b, lnm_g, lnm_b, ffn2_w_in, ffn2_w_out, ln2_g, ln2_b, fox_w_in, fox_b_f, fox_w_o, s5_a_re, s5_a_im, s5_log_dt, s5_b_re, s5_b_im, s5_c_re, s5_c_im, s5_d, s5_w_out, loss_target, m_ffn1_w_in, m_ffn1_w_out, m_ln1_g, m_ln1_b, m_lnm_g, m_lnm_b, m_ffn2_w_in, m_ffn2_w_out, m_ln2_g, m_ln2_b, m_fox_w_in, m_fox_b_f, m_fox_w_o, m_s5_a_re, m_s5_a_im, m_s5_log_dt, m_s5_b_re, m_s5_b_im, m_s5_c_re, m_s5_c_im, m_s5_d, m_s5_w_out, v_ffn1_w_in, v_ffn1_w_out, v_ln1_g, v_ln1_b, v_lnm_g, v_lnm_b, v_ffn2_w_in, v_ffn2_w_out, v_ln2_g, v_ln2_b, v_fox_w_in, v_fox_b_f, v_fox_w_o, v_s5_a_re, v_s5_a_im, v_s5_log_dt, v_s5_b_re, v_s5_b_im, v_s5_c_re, v_s5_c_im, v_s5_d, v_s5_w_out):
    args = dict(locals())
    w = {k: args[k] for k in WEIGHTS}
    m = {k: args["m_" + k] for k in WEIGHTS}
    v = {k: args["v_" + k] for k in WEIGHTS}
    small = {k: w[k] for k in SMALL}

    turned = lambda d: {k: jnp.swapaxes(a, 1, 2) if k in ("ffn1_w_in", "ffn2_w_in") else a for k, a in d.items()}
    w, m, v = turned(w), turned(m), turned(v)
    wb = {k: w[k].astype(BF16) for k in BIG}
    loss_part, dx, arrived, last, gs = _local_step(x[0], loss_target[0], small, lambda k, idx: wb[k][idx])
    loss = lax.psum(loss_part, ("x", "y", "c"))

    small_shapes = [w[k].shape for k in SMALL]
    ex = Exchange(list(last.values()) + [_pack_flat([gs[k] for k in SMALL]).astype(BF16)],
                  [True] * len(last) + [False])
    *got, g_small_all = _exchange(ex, "scatter_last_gather_small")
    arrived.update(zip(last, got))
    g_small_flat = _sum8(g_small_all, "sum_small_grads")

    grads, delta, new_m, new_v = {}, {}, {}, {}
    for k in BIG:
        parts = [arrived[(k, l)] for l in range(w[k].shape[0])]
        grads[k], delta[k], new_m[k], new_v[k] = _adamw_recv(parts, w[k], m[k], v[k], "adamw_" + k)
    pk = lambda d: _pack_flat([d[k] for k in SMALL])
    d_, m_, v_ = _adamw(pk(w), g_small_flat, pk(m), pk(v), "adamw_small")
    for dst, flat in ((grads, g_small_flat), (delta, d_), (new_m, m_), (new_v, v_)):
        dst.update(zip(SMALL, _unpack_flat(flat, small_shapes)))
    grads, delta, new_m, new_v = turned(grads), turned(delta), turned(new_m), turned(new_v)

    return (loss, dx[None], *[grads[k] for k in WEIGHTS], *[delta[k] for k in WEIGHTS],
            *[new_m[k] for k in WEIGHTS], *[new_v[k] for k in WEIGHTS])
```

```python
import functools
import math

import jax
import jax.numpy as jnp
from jax import lax
from jax.experimental import pallas as pl
from jax.experimental.pallas import tpu as pltpu

F32 = jnp.float32
BF16 = jnp.bfloat16

N_DEV = 8
HEAD_DIM = 64
S5_GROUP = 16
S5_STATE = 64
LANES = 128
SUBLANES = 8
S5_PART = 256
ATTN_TILE = 512
LN_EPS = 1e-5
NEG_INF = -1e30
LOG2E = 1.4426950408889634
ADAM_LR, ADAM_B1, ADAM_B2, ADAM_EPS, ADAM_WD, ADAM_STEP = 0.001, 0.9, 0.999, 1e-08, 0.01, 10
VMEM_LIMIT = 48 * 1024 * 1024
PACK_COLS = 1024
PACK_ROW_ALIGN = 512

MESH = pl.DeviceIdType.MESH


def _tile(dim, pref, align=LANES):
    if dim <= pref:
        return dim
    t = (pref // align) * align
    while t >= align:
        if dim % t == 0:
            return t
        t -= align
    return dim


def _params(sem):
    return pltpu.CompilerParams(dimension_semantics=sem, vmem_limit_bytes=VMEM_LIMIT)


def _mm(a, b, *, ta=False, tb=False, out_dtype=F32, scale=None, add=None, add_scale=1.0,
        tm=512, tn=1408, tk=1408, name="mm"):
    if ta:
        K, M = a.shape
    else:
        M, K = a.shape
    if tb:
        N, K2 = b.shape
    else:
        K2, N = b.shape
    assert K == K2, (a.shape, b.shape, ta, tb)
    tm, tn, tk = _tile(M, tm), _tile(N, tn), _tile(K, tk)
    a_spec = pl.BlockSpec((tk, tm), lambda i, j, k: (k, i)) if ta else pl.BlockSpec((tm, tk), lambda i, j, k: (i, k))
    b_spec = pl.BlockSpec((tn, tk), lambda i, j, k: (j, k)) if tb else pl.BlockSpec((tk, tn), lambda i, j, k: (k, j))
    o_spec = pl.BlockSpec((tm, tn), lambda i, j, k: (i, j))
    return _mm_core(name, a, b, a_spec, b_spec, o_spec, jax.ShapeDtypeStruct((M, N), out_dtype),
                    (M // tm, N // tn, K // tk), (tm, tn), ta, tb, scale, add, add_scale)


def _mm_core(name, a, b, a_spec, b_spec, o_spec, out_shape, grid, acc_shape, ta, tb, scale=None, add=None,
             add_scale=1.0, ex=None):
    nk = grid[2]

    def body(*refs):
        if add is None:
            a_ref, b_ref, o_ref, acc = refs
            add_ref = None
        else:
            a_ref, b_ref, add_ref, o_ref, acc = refs
        k = pl.program_id(2)

        @pl.when(k == 0)
        def _():
            acc[...] = jnp.zeros_like(acc)

        dims = (((0 if ta else 1,), (1 if tb else 0,)), ((), ()))
        acc[...] += lax.dot_general(a_ref[...].astype(BF16), b_ref[...].astype(BF16), dims,
                                    preferred_element_type=F32)

        @pl.when(k == nk - 1)
        def _():
            r = acc[...]
            if scale is not None:
                r = r * scale
            if add_ref is not None:
                r = r + add_scale * add_ref[...]
            o_ref[...] = r.astype(out_shape.dtype)

    in_specs = [a_spec, b_spec]
    args = [a, b]
    if add is not None:
        in_specs.append(o_spec)
        args.append(add)
    (res,), extra = _call(body, name=name, out_shape=(out_shape,), grid=grid, in_specs=in_specs, out_specs=(o_spec,),
                          scratch_shapes=[pltpu.VMEM(acc_shape, F32)], args=args,
                          semantics=("parallel", "parallel", "arbitrary"), ex=ex)
    return res if ex is None else (res, extra)


def _ffn_h(x, wt, name):
    S, D = x.shape
    n, c, _ = wt.shape
    nb = n // 2
    tm = _tile(S, 1024, SUBLANES)
    dims = (((1,), (1,)), ((), ()))

    def body(x_ref, wg_ref, wu_ref, h_ref, a_ref):
        xb = x_ref[...].astype(BF16)
        g = lax.dot_general(xb, wg_ref[...], dims, preferred_element_type=F32).astype(BF16)
        u = lax.dot_general(xb, wu_ref[...], dims, preferred_element_type=F32).astype(BF16)
        h_ref[0] = g
        h_ref[1] = u
        g = g.astype(F32)
        a_ref[...] = (g * _sigmoid(g) * u.astype(F32)).astype(BF16)

    h, a4 = pl.pallas_call(
        body, name=name,
        out_shape=(jax.ShapeDtypeStruct((2, nb, S, c), BF16), jax.ShapeDtypeStruct((nb, S, c), BF16)),
        grid=(S // tm, nb),
        in_specs=[pl.BlockSpec((tm, D), lambda i, j: (i, 0)), pl.BlockSpec((None, c, D), lambda i, j: (j, 0, 0)),
                  pl.BlockSpec((None, c, D), lambda i, j: (j + nb, 0, 0))],
        out_specs=(pl.BlockSpec((2, None, tm, c), lambda i, j: (0, j, i, 0)),
                   pl.BlockSpec((None, tm, c), lambda i, j: (j, i, 0))),
        compiler_params=_params(("parallel", "parallel")),
    )(x, wt, wt)
    return h.reshape(n, S, c), a4


def _ffn_y_ln(a4, wo4, x, g, b, alpha, s, name):
    nb, S, c = a4.shape
    D = wo4.shape[-1]
    tm = _tile(S, 512, SUBLANES)

    def body(a_ref, w_ref, x_ref, g_ref, b_ref, o_ref, ob_ref, xh_ref, r_ref, acc):
        k = pl.program_id(1)

        @pl.when(k == 0)
        def _():
            acc[...] = jnp.zeros_like(acc)

        acc[...] += jnp.dot(a_ref[...], w_ref[...], preferred_element_type=F32)

        @pl.when(k == nb - 1)
        def _():
            z = alpha * x_ref[...] + s * acc[...]
            mu = jnp.mean(z, axis=-1, keepdims=True)
            zc = z - mu
            rstd = lax.rsqrt(jnp.mean(zc * zc, axis=-1, keepdims=True) + LN_EPS)
            xh = zc * rstd
            xh_ref[...] = xh
            r_ref[...] = rstd
            out = xh * g_ref[...] + b_ref[...]
            o_ref[...] = out
            ob_ref[...] = out.astype(BF16)

    row = pl.BlockSpec((tm, D), lambda i, k: (i, 0))
    vec = pl.BlockSpec((1, D), lambda i, k: (0, 0))
    return pl.pallas_call(
        body, name=name,
        out_shape=(jax.ShapeDtypeStruct((S, D), F32), jax.ShapeDtypeStruct((S, D), BF16),
                   jax.ShapeDtypeStruct((S, D), F32), jax.ShapeDtypeStruct((S, 1), F32)),
        grid=(S // tm, nb),
        in_specs=[pl.BlockSpec((None, tm, c), lambda i, k: (k, i, 0)), pl.BlockSpec((None, c, D), lambda i, k: (k, 0, 0)),
                  row, vec, vec],
        out_specs=(row, row, row, pl.BlockSpec((tm, 1), lambda i, k: (i, 0))),
        scratch_shapes=[pltpu.VMEM((tm, D), F32)], compiler_params=_params(("parallel", "arbitrary")),
    )(a4, wo4, x, g.reshape(1, D), b.reshape(1, D))


def _ffn_dwout(a4, dz, scale, name, ex=None):
    nb, S, c = a4.shape
    D = dz.shape[1]
    tk, tn = _tile(S, 1024, SUBLANES), _tile(D, 1024)
    return _mm_core(name, a4, dz, pl.BlockSpec((None, tk, c), lambda i, j, k: (i, k, 0)),
                    pl.BlockSpec((tk, tn), lambda i, j, k: (k, j)),
                    pl.BlockSpec((None, c, tn), lambda i, j, k: (i, 0, j)),
                    jax.ShapeDtypeStruct((nb, c, D), BF16), (nb, D // tn, S // tk), (c, tn), True, False, scale,
                    ex=ex)


def _ffn_dh(dz, wo4, h8, scale, name, ex=None):
    S, D = dz.shape
    nb, c = wo4.shape[0], wo4.shape[1]
    tm = _tile(S, 512, SUBLANES)

    def body(dz_ref, w_ref, h_ref, d_ref):
        da = lax.dot_general(dz_ref[...].astype(BF16), w_ref[...], (((1,), (1,)), ((), ())),
                             preferred_element_type=F32) * scale
        g = h_ref[0].astype(F32)
        u = h_ref[1].astype(F32)
        sg = _sigmoid(g)
        silu = g * sg
        d_ref[0] = (da * u * (sg + silu * (1.0 - sg))).astype(BF16)
        d_ref[1] = (da * silu).astype(BF16)

    pair = pl.BlockSpec((2, None, tm, c), lambda i, j: (0, j, i, 0))
    (dh,), extra = _call(
        body, name=name, out_shape=(jax.ShapeDtypeStruct((2, nb, S, c), BF16),), grid=(S // tm, nb),
        in_specs=[pl.BlockSpec((tm, D), lambda i, j: (i, 0)), pl.BlockSpec((None, c, D), lambda i, j: (j, 0, 0)),
                  pair],
        out_specs=(pair,), scratch_shapes=[], args=(dz, wo4, h8.reshape(2, nb, S, c)),
        semantics=("parallel", "parallel"), ex=ex)
    dh = dh.reshape(2 * nb, S, c)
    return dh if ex is None else (dh, extra)


def _ffn_dwin(x, dh8, name, ex=None):
    S, D = x.shape
    n, _, c = dh8.shape
    return _mm_core(name, dh8, x, pl.BlockSpec((None, S, c), lambda i, j, k: (i, 0, 0)),
                    pl.BlockSpec((S, D), lambda i, j, k: (0, 0)),
                    pl.BlockSpec((None, c, D), lambda i, j, k: (i, 0, 0)),
                    jax.ShapeDtypeStruct((n, c, D), BF16), (n, 1, 1), (c, D), True, False, ex=ex)


def _ffn_dx(dh8, wt, dz, alpha, name, ex=None):
    n, S, c = dh8.shape
    D = wt.shape[2]
    tm, tn = _tile(S, 1024, SUBLANES), _tile(D, 1024)
    return _mm_core(name, dh8, wt, pl.BlockSpec((None, tm, c), lambda i, j, k: (k, i, 0)),
                    pl.BlockSpec((None, c, tn), lambda i, j, k: (k, 0, j)),
                    pl.BlockSpec((tm, tn), lambda i, j, k: (i, j)),
                    jax.ShapeDtypeStruct((S, D), F32), (S // tm, D // tn, n), (tm, tn), False, False,
                    None, dz, alpha, ex=ex)


def _res_ln_fwd(x, y, g, b, alpha, s, name):
    S, D = x.shape
    tm = _tile(S, 256, SUBLANES)

    def body(x_ref, y_ref, g_ref, b_ref, o_ref, ob_ref, xh_ref, r_ref):
        z = alpha * x_ref[...] + s * y_ref[...]
        mu = jnp.mean(z, axis=-1, keepdims=True)
        zc = z - mu
        var = jnp.mean(zc * zc, axis=-1, keepdims=True)
        rstd = lax.rsqrt(var + LN_EPS)
        xh = zc * rstd
        xh_ref[...] = xh
        r_ref[...] = rstd
        out = xh * g_ref[...] + b_ref[...]
        o_ref[...] = out
        ob_ref[...] = out.astype(BF16)

    row = pl.BlockSpec((tm, D), lambda i: (i, 0))
    vec = pl.BlockSpec((1, D), lambda i: (0, 0))
    return pl.pallas_call(
        body, name=name,
        out_shape=(jax.ShapeDtypeStruct((S, D), F32), jax.ShapeDtypeStruct((S, D), BF16),
                   jax.ShapeDtypeStruct((S, D), F32), jax.ShapeDtypeStruct((S, 1), F32)),
        grid=(S // tm,), in_specs=[row, row, vec, vec],
        out_specs=(row, row, row, pl.BlockSpec((tm, 1), lambda i: (i, 0))),
        compiler_params=_params(("parallel",)),
    )(x, y, g.reshape(1, D), b.reshape(1, D))


def _ln_bwd(dout, xh, rstd, g, name):
    S, D = dout.shape
    tm = _tile(S, 256, SUBLANES)

    def body(d_ref, xh_ref, r_ref, g_ref, dz_ref, dg_ref, db_ref):
        i = pl.program_id(0)

        @pl.when(i == 0)
        def _():
            dg_ref[...] = jnp.zeros_like(dg_ref)
            db_ref[...] = jnp.zeros_like(db_ref)

        d = d_ref[...]
        xhv = xh_ref[...]
        dxh = d * g_ref[...]
        m1 = jnp.mean(dxh, axis=-1, keepdims=True)
        m2 = jnp.mean(dxh * xhv, axis=-1, keepdims=True)
        dz_ref[...] = r_ref[...] * (dxh - m1 - xhv * m2)
        dg_ref[...] += jnp.sum(d * xhv, axis=0, keepdims=True)
        db_ref[...] += jnp.sum(d, axis=0, keepdims=True)

    row = pl.BlockSpec((tm, D), lambda i: (i, 0))
    vec = pl.BlockSpec((1, D), lambda i: (0, 0))
    dz, dg, db = pl.pallas_call(
        body, name=name,
        out_shape=(jax.ShapeDtypeStruct((S, D), F32), jax.ShapeDtypeStruct((1, D), F32),
                   jax.ShapeDtypeStruct((1, D), F32)),
        grid=(S // tm,), in_specs=[row, row, pl.BlockSpec((tm, 1), lambda i: (i, 0)), vec],
        out_specs=(row, vec, vec),
        compiler_params=_params(("arbitrary",)),
    )(dout, xh, rstd, g.reshape(1, D))
    return dz, dg[0], db[0]


def _sigmoid(x):
    e = jnp.exp(-jnp.abs(x))
    r = 1.0 / (1.0 + e)
    return jnp.where(x >= 0, r, e * r)


def _glu_fwd(vg, name):
    S, D2 = vg.shape
    D = D2 // 2
    tm = _tile(S, 512, SUBLANES)

    def body(v_ref, g_ref, o_ref):
        o_ref[...] = v_ref[...] * _sigmoid(g_ref[...])

    return pl.pallas_call(
        body, name=name, out_shape=jax.ShapeDtypeStruct((S, D), F32), grid=(S // tm,),
        in_specs=[pl.BlockSpec((tm, D), lambda i: (i, 0)), pl.BlockSpec((tm, D), lambda i: (i, 1))],
        out_specs=pl.BlockSpec((tm, D), lambda i: (i, 0)),
        compiler_params=_params(("parallel",)),
    )(vg, vg)


def _glu_bwd(vg, dm, name):
    S, D2 = vg.shape
    D = D2 // 2
    tm = _tile(S, 512, SUBLANES)

    def body(v_ref, g_ref, dm_ref, dv_ref, dg_ref):
        sg = _sigmoid(g_ref[...])
        d = dm_ref[...]
        dv_ref[...] = (d * sg).astype(BF16)
        dg_ref[...] = (d * v_ref[...] * sg * (1.0 - sg)).astype(BF16)

    blk = pl.BlockSpec((tm, D), lambda i: (i, 0))
    dv, dg = pl.pallas_call(
        body, name=name,
        out_shape=(jax.ShapeDtypeStruct((S, D), BF16), jax.ShapeDtypeStruct((S, D), BF16)),
        grid=(S // tm,), in_specs=[blk, pl.BlockSpec((tm, D), lambda i: (i, 1)), blk],
        out_specs=(blk, blk), compiler_params=_params(("parallel",)),
    )(vg, vg, dm)
    return jnp.concatenate([dv, dg], axis=1)


def _loss_fwd_bwd(y, target):
    S, D = y.shape
    tm = _tile(S, 256, SUBLANES)

    def body(y_ref, t_ref, dy_ref, l_ref):
        i = pl.program_id(0)

        @pl.when(i == 0)
        def _():
            l_ref[...] = jnp.zeros_like(l_ref)

        e = y_ref[...] - t_ref[...]
        dy_ref[...] = e * (1.0 / D)
        l_ref[...] += jnp.sum(e * e, axis=0, keepdims=True) * (0.5 / D)

    row = pl.BlockSpec((tm, D), lambda i: (i, 0))
    dy, part = pl.pallas_call(
        body, name="loss", out_shape=(jax.ShapeDtypeStruct((S, D), F32), jax.ShapeDtypeStruct((1, D), F32)),
        grid=(S // tm,), in_specs=[row, row], out_specs=(row, pl.BlockSpec((1, D), lambda i: (0, 0))),
        compiler_params=_params(("arbitrary",)),
    )(y, target)
    return dy, jnp.sum(part)


def _tri(n, lower):
    r = lax.broadcasted_iota(jnp.int32, (n, n), 0)
    c = lax.broadcasted_iota(jnp.int32, (n, n), 1)
    return jnp.where((c <= r) if lower else (c >= r), 1.0, 0.0)


def _fox_gate_fwd(fl, bf):
    S, W = fl.shape
    tm = _tile(S, 256, SUBLANES)

    def body(fl_ref, b_ref, c_ref, carry):
        i = pl.program_id(0)

        @pl.when(i == 0)
        def _():
            carry[...] = jnp.zeros_like(carry)

        x = fl_ref[...] + b_ref[...]
        lf = jnp.minimum(x, 0.0) - jnp.log(1.0 + jnp.exp(-jnp.abs(x)))
        c_ref[...] = jnp.dot(_tri(tm, True), lf, precision=lax.Precision.HIGHEST,
                             preferred_element_type=F32) + carry[...]
        carry[...] += jnp.sum(lf, axis=0, keepdims=True)

    blk = pl.BlockSpec((tm, W), lambda i: (i, 0))
    return pl.pallas_call(
        body, name="fox_gate_fwd", out_shape=jax.ShapeDtypeStruct((S, W), F32), grid=(S // tm,),
        in_specs=[blk, pl.BlockSpec((1, W), lambda i: (0, 0))], out_specs=blk,
        scratch_shapes=[pltpu.VMEM((1, W), F32)], compiler_params=_params(("arbitrary",)),
    )(fl, bf)


def _fox_gate_bwd(dcum, fl, bf):
    S, W = fl.shape
    tm = _tile(S, 256, SUBLANES)
    nb = S // tm

    def body(dc_ref, fl_ref, b_ref, dfl_ref, db_ref, carry):
        i = pl.program_id(0)

        @pl.when(i == 0)
        def _():
            carry[...] = jnp.zeros_like(carry)
            db_ref[...] = jnp.zeros_like(db_ref)

        dc = dc_ref[...]
        r = jnp.dot(_tri(tm, False), dc, precision=lax.Precision.HIGHEST, preferred_element_type=F32) + carry[...]
        carry[...] += jnp.sum(dc, axis=0, keepdims=True)
        x = fl_ref[...] + b_ref[...]
        dfl = r * (1.0 - _sigmoid(x))
        dfl_ref[...] = dfl
        db_ref[...] += jnp.sum(dfl, axis=0, keepdims=True)

    blk = pl.BlockSpec((tm, W), lambda i: (nb - 1 - i, 0))
    vec = pl.BlockSpec((1, W), lambda i: (0, 0))
    return pl.pallas_call(
        body, name="fox_gate_bwd",
        out_shape=(jax.ShapeDtypeStruct((S, W), F32), jax.ShapeDtypeStruct((1, W), F32)),
        grid=(nb,), in_specs=[blk, blk, vec], out_specs=(blk, vec),
        scratch_shapes=[pltpu.VMEM((1, W), F32)], compiler_params=_params(("arbitrary",)),
    )(dcum, fl, bf)


def _causal(t):
    row = lax.broadcasted_iota(jnp.int32, (t, t), 0)
    col = lax.broadcasted_iota(jnp.int32, (t, t), 1)
    return col <= row


def _first_head(shape):
    return lax.broadcasted_iota(jnp.int32, shape, len(shape) - 1) < HEAD_DIM


def _split3(x):
    hi = x.astype(BF16).astype(F32)
    r = x - hi
    mid = r.astype(BF16).astype(F32)
    return hi, mid, (r - mid).astype(BF16).astype(F32)


def _bias_lanes(c, query):
    lane = lax.broadcasted_iota(jnp.int32, (c.shape[0], LANES), 1)
    hi, mid, lo = _split3(c)
    if query:
        out = jnp.where(lane == 0, hi, jnp.where(lane == 1, mid, jnp.where(lane == 2, lo,
                                                                          jnp.where(lane < 6, 1.0, 0.0))))
    else:
        out = jnp.where(lane < 3, 1.0, jnp.where(lane == 3, -hi, jnp.where(lane == 4, -mid,
                                                                          jnp.where(lane == 5, -lo, 0.0))))
    return out.astype(BF16)


def _flash_fwd(proj, cum2, ex=None):
    S = proj.shape[0]
    D = (proj.shape[1] - LANES) // 3
    HP = D // LANES
    t = _tile(S, ATTN_TILE)
    nq = S // t
    scale = 1.0 / math.sqrt(HEAD_DIM)

    def body(q_ref, k_ref, v_ref, cq_ref, ck_ref, o_ref, lse_ref, kx, vb):
        qi = pl.program_id(1)

        @pl.when(qi == 0)
        def _():
            def prep(kb, c):
                sl = pl.ds(pl.multiple_of(kb * t, t), t)
                k16 = k_ref[sl, :].astype(BF16)
                ckv = ck_ref[0, sl, :]
                for a in range(2):
                    kx[a, sl, 0:LANES] = k16
                    kx[a, sl, LANES:2 * LANES] = _bias_lanes(ckv[:, a:a + 1], False)
                vb[sl, :] = v_ref[sl, :].astype(BF16)
                return c

            lax.fori_loop(0, nq, prep, 0)

        first = _first_head((t, LANES))
        qf = q_ref[...] * (scale * LOG2E)
        cqv = cq_ref[0]
        qx = [jnp.concatenate([jnp.where(keep, qf, 0.0).astype(BF16), _bias_lanes(cqv[:, a:a + 1], True)], axis=1)
              for a, keep in enumerate((first, jnp.logical_not(first)))]

        def block(ki, carry, masked):
            m_old, l_old, acc = carry
            sl = pl.ds(pl.multiple_of(ki * t, t), t)
            vv = vb[sl, :]
            m_new, l_new, corr, pv = [], [], [], []
            for a in range(2):
                s = lax.dot_general(qx[a], kx[a, sl, :], (((1,), (1,)), ((), ())), preferred_element_type=F32)
                if masked:
                    s = jnp.where(_causal(t), s, NEG_INF)
                m_a = jnp.maximum(m_old[a], jnp.max(s, axis=1, keepdims=True))
                p = jnp.exp2(s - m_a)
                c_a = jnp.exp2(m_old[a] - m_a)
                m_new.append(m_a)
                corr.append(c_a)
                l_new.append(c_a * l_old[a] + jnp.sum(p, axis=1, keepdims=True))
                pv.append(jnp.dot(p.astype(BF16), vv, preferred_element_type=F32))
            acc = jnp.where(first, corr[0] * acc + pv[0], corr[1] * acc + pv[1])
            return tuple(m_new), tuple(l_new), acc

        neg = jnp.full((t, 1), NEG_INF, F32)
        zero = jnp.zeros((t, 1), F32)
        carry = lax.fori_loop(0, qi, lambda ki, c: block(ki, c, False),
                              ((neg, neg), (zero, zero), jnp.zeros((t, LANES), F32)))
        m, l, acc = block(qi, carry, True)
        o_ref[...] = acc / jnp.where(first, l[0], l[1])
        lse_ref[0, :, 0:1] = m[0] + jnp.log2(l[0])
        lse_ref[0, :, 1:2] = m[1] + jnp.log2(l[1])

    qblk = pl.BlockSpec((t, LANES), lambda h, i: (i, h))
    r2 = pl.BlockSpec((1, t, 2), lambda h, i: (h, i, 0))
    (o, lse), extra = _call(
        body, name="fox_attn_fwd",
        out_shape=(jax.ShapeDtypeStruct((S, D), F32), jax.ShapeDtypeStruct((HP, S, 2), F32)),
        grid=(HP, nq),
        in_specs=[qblk, pl.BlockSpec((S, LANES), lambda h, i: (0, HP + h)),
                  pl.BlockSpec((S, LANES), lambda h, i: (0, 2 * HP + h)), r2,
                  pl.BlockSpec((1, S, 2), lambda h, i: (h, 0, 0))],
        out_specs=(qblk, r2),
        scratch_shapes=[pltpu.VMEM((2, S, 2 * LANES), BF16), pltpu.VMEM((S, LANES), BF16)],
        args=(proj, proj, proj, cum2, cum2), semantics=("parallel", "arbitrary"), ex=ex)
    return o, lse, extra


def _flash_bwd(proj, rows, o, do, ex=None):
    S = proj.shape[0]
    D = (proj.shape[1] - LANES) // 3
    HP = D // LANES
    t = _tile(S, ATTN_TILE)
    nb = S // t
    scale = 1.0 / math.sqrt(HEAD_DIM)

    def body(q_ref, k_ref, v_ref, rows_ref, o_ref, do_ref, dq_ref, dk_ref, dv_ref, dcq_ref, dck_ref,
             q_s, do_s, dl_s, dk_acc, dv_acc, dc_acc):
        kb = pl.program_id(1)

        @pl.when(kb == 0)
        def _():
            dq_ref[...] = jnp.zeros_like(dq_ref)
            dcq_ref[...] = jnp.zeros_like(dcq_ref)

            def prep(qb, c):
                sl = pl.ds(pl.multiple_of(qb * t, t), t)
                first = _first_head((t, LANES))
                qf = q_ref[sl, :] * (scale * LOG2E)
                dof = do_ref[sl, :]
                prod = dof * o_ref[sl, :]
                rv = rows_ref[0, sl, :]
                for a, keep in enumerate((first, jnp.logical_not(first))):
                    q_s[a, sl, 0:LANES] = jnp.where(keep, qf, 0.0).astype(BF16)
                    q_s[a, sl, LANES:2 * LANES] = _bias_lanes(rv[:, a:a + 1], True)
                    do_s[a, sl, :] = jnp.where(keep, dof, 0.0).astype(BF16)
                    dl_s[sl, a:a + 1] = jnp.sum(jnp.where(keep, prod, 0.0), axis=1, keepdims=True)
                return c

            lax.fori_loop(0, nb, prep, 0)

        dk_acc[...] = jnp.zeros_like(dk_acc)
        dv_acc[...] = jnp.zeros_like(dv_acc)
        dc_acc[...] = jnp.zeros_like(dc_acc)
        first = _first_head((t, LANES))
        kf = k_ref[...]
        kk = kf.astype(BF16)
        k_own = (jnp.where(first, kf, 0.0).astype(BF16), jnp.where(first, 0.0, kf).astype(BF16))
        ckv = rows_ref[0, pl.ds(pl.multiple_of(kb * t, t), t), :]
        kx = [jnp.concatenate([kk, _bias_lanes(ckv[:, a:a + 1], False)], axis=1) for a in range(2)]
        vv = v_ref[...].astype(BF16)

        def block(qb, masked):
            sl = pl.ds(pl.multiple_of(qb * t, t), t)
            rv = rows_ref[0, sl, :]
            dlv = dl_s[sl, :]
            dq_new = dq_ref[sl, :]
            for a in range(2):
                dob = do_s[a, sl, :]
                s = lax.dot_general(q_s[a, sl, :], kx[a], (((1,), (1,)), ((), ())), preferred_element_type=F32)
                if masked:
                    s = jnp.where(_causal(t), s, NEG_INF)
                p = jnp.exp2(s - rv[:, 2 + a:3 + a])
                dv_acc[...] += lax.dot_general(p.astype(BF16), dob, (((0,), (0,)), ((), ())),
                                               preferred_element_type=F32)
                dp = lax.dot_general(dob, vv, (((1,), (1,)), ((), ())), preferred_element_type=F32)
                ds = p * (dp - dlv[:, a:a + 1])
                dsb = ds.astype(BF16)
                dk_acc[...] += lax.dot_general(dsb, q_s[a, sl, 0:LANES], (((0,), (0,)), ((), ())),
                                               preferred_element_type=F32)
                dq_new = dq_new + jnp.dot(dsb, k_own[a], preferred_element_type=F32) * scale
                dcq_ref[0, sl, a:a + 1] += jnp.sum(ds, axis=1, keepdims=True)
                dc_acc[a:a + 1, :] -= jnp.sum(ds, axis=0, keepdims=True)
            dq_ref[sl, :] = dq_new

        block(kb, True)

        def rest(qb, c):
            block(qb, False)
            return c

        lax.fori_loop(kb + 1, nb, rest, 0)
        dk_ref[...] = dk_acc[...] * (1.0 / LOG2E)
        dv_ref[...] = dv_acc[...]
        dck_ref[0, 0] = dc_acc[0:1, :]
        dck_ref[1, 0] = dc_acc[1:2, :]

    full = lambda c0: pl.BlockSpec((S, LANES), lambda h, j, c0=c0: (0, c0 + h))
    blk = lambda c0: pl.BlockSpec((t, LANES), lambda h, j, c0=c0: (j, c0 + h))
    f32 = lambda *s: jax.ShapeDtypeStruct(s, F32)
    outs, extra = _call(
        body, name="fox_attn_bwd",
        out_shape=(f32(S, D), f32(S, D), f32(S, D), f32(HP, S, 2), f32(2 * HP, nb, 1, t)),
        grid=(HP, nb),
        in_specs=[full(0), blk(HP), blk(2 * HP), pl.BlockSpec((1, S, 4), lambda h, j: (h, 0, 0)), full(0), full(0)],
        out_specs=(full(0), blk(0), blk(0), pl.BlockSpec((1, S, 2), lambda h, j: (h, 0, 0)),
                   pl.BlockSpec((2, 1, 1, t), lambda h, j: (h, j, 0, 0))),
        scratch_shapes=[pltpu.VMEM((2, S, 2 * LANES), BF16), pltpu.VMEM((2, S, LANES), BF16),
                        pltpu.VMEM((S, 2), F32), pltpu.VMEM((t, LANES), F32), pltpu.VMEM((t, LANES), F32),
                        pltpu.VMEM((2, t), F32)],
        args=(proj, proj, proj, rows, o, do), semantics=("parallel", "arbitrary"), ex=ex)
    return (*outs, extra)


def _s5_consts(T):
    rows = T * SUBLANES
    rr = lax.broadcasted_iota(jnp.int32, (rows, T), 0)
    tt = lax.broadcasted_iota(jnp.int32, (rows, T), 1)
    rep = jnp.where(rr // SUBLANES == tt, 1.0, 0.0).astype(BF16)
    r2 = lax.broadcasted_iota(jnp.int32, (rows, S5_PART), 0)
    c2 = lax.broadcasted_iota(jnp.int32, (rows, S5_PART), 1)
    mask = (c2 // (S5_PART // SUBLANES)) == (r2 % SUBLANES)
    return rep, mask


def _gelu(y):
    c = math.sqrt(2.0 / math.pi)
    return 0.5 * y * (1.0 + jnp.tanh(c * (y + 0.044715 * y * y * y)))


def _gelu_grad(y):
    c = math.sqrt(2.0 / math.pi)
    th = jnp.tanh(c * (y + 0.044715 * y * y * y))
    return 0.5 * (1.0 + th) + 0.5 * y * (1.0 - th * th) * c * (1.0 + 3.0 * 0.044715 * y * y)


def _s5_fwd(x, rmat, cmat, lam, dskip, ex=None):
    S, D = x.shape
    NQ = D // S5_PART
    T = _tile(S, 128, SUBLANES)
    rows = T * SUBLANES

    def body(x_ref, r_ref, c_ref, lam_ref, d_ref, y_ref, yg_ref, h_ref, bu_s, carry):
        i = pl.program_id(0)

        @pl.when(i == 0)
        def _():
            carry[...] = jnp.zeros_like(carry)

        rep, mask = _s5_consts(T)
        cols = [pl.ds(q * S5_PART, S5_PART) for q in range(NQ)]
        for q in range(NQ):
            xrep = jnp.dot(rep, x_ref[:, cols[q]].astype(BF16), preferred_element_type=F32)
            lx = jnp.where(mask, xrep, 0.0).astype(BF16)
            bu_s[q] = jnp.dot(lx, r_ref[q], preferred_element_type=F32)
        lam_v = [(lam_ref[q, :, 0:LANES], lam_ref[q, :, LANES:2 * LANES]) for q in range(NQ)]

        def step(t, c):
            o = pl.multiple_of(t * SUBLANES, SUBLANES)
            new = []
            for q in range(NQ):
                hr, hi = c[q]
                ar, ai = lam_v[q]
                sl = bu_s[q, pl.ds(o, SUBLANES), :]
                nhr = ar * hr - ai * hi + sl[:, 0:LANES]
                nhi = ar * hi + ai * hr + sl[:, LANES:2 * LANES]
                h_ref[q, pl.ds(o, SUBLANES), 0:LANES] = nhr
                h_ref[q, pl.ds(o, SUBLANES), LANES:2 * LANES] = nhi
                new.append((nhr, nhi))
            return tuple(new)

        fin = lax.fori_loop(0, T, step,
                            tuple((carry[q, :, 0:LANES], carry[q, :, LANES:2 * LANES]) for q in range(NQ)))
        for q in range(NQ):
            carry[q, :, 0:LANES] = fin[q][0]
            carry[q, :, LANES:2 * LANES] = fin[q][1]
            z = jnp.dot(h_ref[q].astype(BF16), c_ref[q], preferred_element_type=F32)
            z = jnp.where(mask, z, 0.0)
            y = jnp.sum(z.reshape(T, SUBLANES, S5_PART), axis=1) + d_ref[:, cols[q]] * x_ref[:, cols[q]]
            y_ref[:, cols[q]] = y
            yg_ref[:, cols[q]] = _gelu(y).astype(BF16)

    xs = pl.BlockSpec((T, D), lambda i: (i, 0))
    ms = pl.BlockSpec((NQ, S5_PART, S5_PART), lambda i: (0, 0, 0))
    outs, extra = _call(
        body, name="s5_scan_fwd",
        out_shape=(jax.ShapeDtypeStruct((S, D), F32), jax.ShapeDtypeStruct((S, D), BF16),
                   jax.ShapeDtypeStruct((NQ, S * SUBLANES, S5_PART), F32)),
        grid=(S // T,),
        in_specs=[xs, ms, ms, pl.BlockSpec((NQ, SUBLANES, S5_PART), lambda i: (0, 0, 0)),
                  pl.BlockSpec((1, D), lambda i: (0, 0))],
        out_specs=(xs, xs, pl.BlockSpec((NQ, rows, S5_PART), lambda i: (0, i, 0))),
        scratch_shapes=[pltpu.VMEM((NQ, rows, S5_PART), F32), pltpu.VMEM((NQ, SUBLANES, S5_PART), F32)],
        args=(x, rmat, cmat, lam, dskip), semantics=("arbitrary",), ex=ex)
    return (*outs, extra)


def _s5_bwd(x, y, dyg, hs, rmat, cmat, lam, dskip, res, res_scale, ex=None):
    S, D = x.shape
    NQ = D // S5_PART
    T = _tile(S, 128, SUBLANES)
    nb = S // T
    rows = T * SUBLANES

    def body(x_ref, y_ref, dyg_ref, res_ref, h_ref, hp_ref, r_ref, c_ref, lam_ref, d_ref,
             dx_ref, dr_ref, dc_ref, dlam_ref, dd_ref, dh_s, g_s, hs_s, carry):
        i = pl.program_id(0)

        @pl.when(i == 0)
        def _():
            carry[...] = jnp.zeros_like(carry)
            dr_ref[...] = jnp.zeros_like(dr_ref)
            dc_ref[...] = jnp.zeros_like(dc_ref)
            dlam_ref[...] = jnp.zeros_like(dlam_ref)
            dd_ref[...] = jnp.zeros_like(dd_ref)

        rep, mask = _s5_consts(T)
        cols = [pl.ds(q * S5_PART, S5_PART) for q in range(NQ)]
        dys, ldys = [], []
        for q in range(NQ):
            dy = dyg_ref[:, cols[q]] * _gelu_grad(y_ref[:, cols[q]])
            dyrep = jnp.dot(rep, dy.astype(BF16), preferred_element_type=F32)
            ldy = jnp.where(mask, dyrep, 0.0).astype(BF16)
            dh_s[q] = lax.dot_general(ldy, c_ref[q], (((1,), (1,)), ((), ())), preferred_element_type=F32)
            dys.append(dy)
            ldys.append(ldy)
        lam_v = [(lam_ref[q, :, 0:LANES], lam_ref[q, :, LANES:2 * LANES]) for q in range(NQ)]

        def step(n, c):
            o = pl.multiple_of((T - 1 - n) * SUBLANES, SUBLANES)
            new = []
            for q in range(NQ):
                gr, gi = c[q]
                ar, ai = lam_v[q]
                sl = dh_s[q, pl.ds(o, SUBLANES), :]
                ngr = sl[:, 0:LANES] + ar * gr + ai * gi
                ngi = sl[:, LANES:2 * LANES] - ai * gr + ar * gi
                g_s[q, pl.ds(o, SUBLANES), 0:LANES] = ngr
                g_s[q, pl.ds(o, SUBLANES), LANES:2 * LANES] = ngi
                new.append((ngr, ngi))
            return tuple(new)

        fin = lax.fori_loop(0, T, step,
                            tuple((carry[q, :, 0:LANES], carry[q, :, LANES:2 * LANES]) for q in range(NQ)))
        for q in range(NQ):
            carry[q, :, 0:LANES] = fin[q][0]
            carry[q, :, LANES:2 * LANES] = fin[q][1]
            xv = x_ref[:, cols[q]]
            hv = h_ref[q]
            hs_s[0:SUBLANES, :] = jnp.where(i == nb - 1, 0.0, hp_ref[q])
            hs_s[SUBLANES:rows + SUBLANES, :] = hv
            hprev = hs_s[0:rows, :]
            gv = g_s[q]
            g_re, g_im = gv[:, 0:LANES], gv[:, LANES:2 * LANES]
            hp_re, hp_im = hprev[:, 0:LANES], hprev[:, LANES:2 * LANES]
            dar = jnp.sum((g_re * hp_re + g_im * hp_im).reshape(T, SUBLANES, LANES), axis=0)
            dai = jnp.sum((g_im * hp_re - g_re * hp_im).reshape(T, SUBLANES, LANES), axis=0)
            dlam_ref[q, :, 0:LANES] += dar
            dlam_ref[q, :, LANES:2 * LANES] += dai

            gb = gv.astype(BF16)
            xrep = jnp.dot(rep, xv.astype(BF16), preferred_element_type=F32)
            lx = jnp.where(mask, xrep, 0.0).astype(BF16)
            dr_ref[q] += lax.dot_general(lx, gb, (((0,), (0,)), ((), ())), preferred_element_type=F32)
            dc_ref[q] += lax.dot_general(hv.astype(BF16), ldys[q], (((0,), (0,)), ((), ())),
                                         preferred_element_type=F32)
            zx = lax.dot_general(gb, r_ref[q], (((1,), (1,)), ((), ())), preferred_element_type=F32)
            zx = jnp.where(mask, zx, 0.0)
            dx_ref[:, cols[q]] = (jnp.sum(zx.reshape(T, SUBLANES, S5_PART), axis=1) + d_ref[:, cols[q]] * dys[q]
                                  + res_scale * res_ref[:, cols[q]])
            dd_ref[:, cols[q]] += jnp.sum(dys[q] * xv, axis=0, keepdims=True)

    xs = pl.BlockSpec((T, D), lambda i: (nb - 1 - i, 0))
    ms = pl.BlockSpec((NQ, S5_PART, S5_PART), lambda i: (0, 0, 0))
    ls = pl.BlockSpec((NQ, SUBLANES, S5_PART), lambda i: (0, 0, 0))
    ds_ = pl.BlockSpec((1, D), lambda i: (0, 0))
    outs, extra = _call(
        body, name="s5_scan_bwd",
        out_shape=(jax.ShapeDtypeStruct((S, D), F32), jax.ShapeDtypeStruct((NQ, S5_PART, S5_PART), F32),
                   jax.ShapeDtypeStruct((NQ, S5_PART, S5_PART), F32),
                   jax.ShapeDtypeStruct((NQ, SUBLANES, S5_PART), F32), jax.ShapeDtypeStruct((1, D), F32)),
        grid=(nb,),
        in_specs=[xs, xs, xs, xs, pl.BlockSpec((NQ, rows, S5_PART), lambda i: (0, nb - 1 - i, 0)),
                  pl.BlockSpec((NQ, SUBLANES, S5_PART), lambda i: (0, jnp.maximum((nb - 1 - i) * T - 1, 0), 0)),
                  ms, ms, ls, ds_],
        out_specs=(xs, ms, ms, ls, ds_),
        scratch_shapes=[pltpu.VMEM((NQ, rows, S5_PART), F32), pltpu.VMEM((NQ, rows, S5_PART), F32),
                        pltpu.VMEM((rows + SUBLANES, S5_PART), F32), pltpu.VMEM((NQ, SUBLANES, S5_PART), F32)],
        args=(x, y, dyg, res, hs, hs, rmat, cmat, lam, dskip), semantics=("arbitrary",), ex=ex)
    return (*outs, extra)


def _s5_discretise(a_re, a_im, log_dt, b_re, b_im):
    dt = jnp.exp(log_dt)[:, None]
    mag = jnp.exp(a_re * dt)
    ang = a_im * dt
    lb_re = mag * jnp.cos(ang)
    lb_im = mag * jnp.sin(ang)
    den = a_re * a_re + a_im * a_im
    nr = lb_re - 1.0
    ni = lb_im
    z_re = (nr * a_re + ni * a_im) / den
    z_im = (ni * a_re - nr * a_im) / den
    bb_re = z_re[..., None] * b_re - z_im[..., None] * b_im
    bb_im = z_re[..., None] * b_im + z_im[..., None] * b_re
    return lb_re, lb_im, bb_re, bb_im


def _s5_expand(w):
    G = w.shape[0]
    NQ = G // 16
    base = w.reshape(NQ, S5_PART, S5_STATE)
    half = (jnp.arange(S5_PART) // S5_GROUP) % 2
    sel = (half[:, None] == jnp.arange(2)[None, :]).astype(w.dtype)
    out = base[:, :, None, :] * sel[None, :, :, None]
    return out.reshape(NQ, S5_PART, 2 * S5_STATE)


def _s5_extract(m):
    NQ = m.shape[0]
    half = (jnp.arange(S5_PART) // S5_GROUP) % 2
    sel = (half[:, None] == jnp.arange(2)[None, :]).astype(m.dtype)
    base = jnp.sum(m.reshape(NQ, S5_PART, 2, S5_STATE) * sel[None, :, :, None], axis=2)
    return base.reshape(NQ * 16, S5_GROUP, S5_STATE)


def _s5_slab(v):
    return v.reshape(v.shape[0] // 16, SUBLANES, LANES)


def _adamw(w, g, m, v, name):
    R, C = w.shape
    tr = _tile(R, 256, SUBLANES)
    c1 = 1.0 / (1.0 - ADAM_B1 ** ADAM_STEP)
    c2 = 1.0 / (1.0 - ADAM_B2 ** ADAM_STEP)

    def body(w_ref, g_ref, m_ref, v_ref, d_ref, nm_ref, nv_ref):
        gv = g_ref[...]
        nm = ADAM_B1 * m_ref[...] + (1.0 - ADAM_B1) * gv
        nv = ADAM_B2 * v_ref[...] + (1.0 - ADAM_B2) * (gv * gv)
        nm_ref[...] = nm
        nv_ref[...] = nv
        d_ref[...] = -ADAM_LR * ((nm * c1) / (jnp.sqrt(nv * c2) + ADAM_EPS) + ADAM_WD * w_ref[...])

    blk = pl.BlockSpec((tr, C), lambda i: (i, 0))
    sh = jax.ShapeDtypeStruct((R, C), F32)
    return pl.pallas_call(
        body, name=name, out_shape=(sh, sh, sh), grid=(R // tr,), in_specs=[blk] * 4, out_specs=(blk,) * 3,
        compiler_params=_params(("parallel",)),
    )(w, g, m, v)


def _adamw_recv(parts, w, m, v, name):
    L, R, C = w.shape
    tr = _tile(R, 256, SUBLANES)
    c1 = 1.0 / (1.0 - ADAM_B1 ** ADAM_STEP)
    c2 = 1.0 / (1.0 - ADAM_B2 ** ADAM_STEP)

    def body(*refs):
        p_refs = refs[:L]
        w_ref, m_ref, v_ref, g_ref, d_ref, nm_ref, nv_ref = refs[L:]
        li = pl.program_id(0)
        for l in range(L):
            @pl.when(li == l)
            def _(p_ref=p_refs[l]):
                gv = p_ref[0].astype(F32)
                for k in range(1, N_DEV):
                    gv = gv + p_ref[k].astype(F32)
                g_ref[...] = gv
                nm = ADAM_B1 * m_ref[...] + (1.0 - ADAM_B1) * gv
                nv = ADAM_B2 * v_ref[...] + (1.0 - ADAM_B2) * (gv * gv)
                nm_ref[...] = nm
                nv_ref[...] = nv
                d_ref[...] = -ADAM_LR * ((nm * c1) / (jnp.sqrt(nv * c2) + ADAM_EPS) + ADAM_WD * w_ref[...])

    p_specs = [pl.BlockSpec((N_DEV, tr, C), lambda li, i, l=l: (0, jnp.where(li == l, i, 0), 0)) for l in range(L)]
    blk = pl.BlockSpec((None, tr, C), lambda li, i: (li, i, 0))
    sh = jax.ShapeDtypeStruct((L, R, C), F32)
    return pl.pallas_call(
        body, name=name, out_shape=(sh, sh, sh, sh), grid=(L, R // tr),
        in_specs=p_specs + [blk, blk, blk], out_specs=(blk,) * 4,
        compiler_params=_params(("parallel", "parallel")),
    )(*parts, w, m, v)


def _sum8(parts, name):
    _, R, C = parts.shape
    tr = _tile(R, 256, SUBLANES)

    def body(p_ref, o_ref):
        acc = p_ref[0].astype(F32)
        for k in range(1, N_DEV):
            acc = acc + p_ref[k].astype(F32)
        o_ref[...] = acc

    return pl.pallas_call(
        body, name=name, out_shape=jax.ShapeDtypeStruct((R, C), F32), grid=(R // tr,),
        in_specs=[pl.BlockSpec((N_DEV, tr, C), lambda i: (0, i, 0))],
        out_specs=pl.BlockSpec((tr, C), lambda i: (i, 0)), compiler_params=_params(("parallel",)),
    )(parts)


def _peers():
    x, y, c = lax.axis_index("x"), lax.axis_index("y"), lax.axis_index("c")
    me = 4 * x + 2 * y + c
    out = []
    for k in range(1, N_DEV):
        kx, ky, kc = (k >> 2) & 1, (k >> 1) & 1, k & 1
        px, py, pc = x ^ kx, y ^ ky, c ^ kc
        out.append(((px, py, pc), 4 * px + 2 * py + pc))
    return me, out


SIBLING = 1
SAME_CORE = (2, 4, 6)


class Exchange:
    def __init__(self, xs, scatter):
        self.xs = list(xs)
        self.scatter = list(scatter)
        self.n = len(self.xs)

    def out_shapes(self):
        return tuple(jax.ShapeDtypeStruct(x.shape if sc else (N_DEV,) + x.shape, x.dtype)
                     for x, sc in zip(self.xs, self.scatter))

    def sems(self):
        return [pltpu.SemaphoreType.DMA((self.n * N_DEV,)), pltpu.SemaphoreType.DMA((self.n * N_DEV,)),
                pltpu.SemaphoreType.DMA((self.n,))]

    def _copy(self, i, m, src, dst, to, send_sems, recv_sems):
        return pltpu.make_async_remote_copy(src_ref=src, dst_ref=dst, send_sem=send_sems.at[i * N_DEV + m],
                                            recv_sem=recv_sems.at[i * N_DEV + m], device_id=to, device_id_type=MESH)

    def start(self, x_refs, o_refs, send_sems, recv_sems, local_sems):
        me, peers = _peers()
        for i, (x_ref, o_ref, sc) in enumerate(zip(x_refs, o_refs, self.scatter)):
            pltpu.make_async_copy(x_ref.at[me] if sc else x_ref, o_ref.at[me], local_sems.at[i]).start()
            for m in (range(1, N_DEV) if sc else (SIBLING,) + SAME_CORE):
                dev, idx = peers[m - 1]
                self._copy(i, m, x_ref.at[idx] if sc else x_ref, o_ref.at[me], dev, send_sems, recv_sems).start()

    def middle(self, x_refs, o_refs, send_sems, recv_sems, local_sems):
        me, peers = _peers()
        sibling = peers[SIBLING - 1][0]
        for i, (x_ref, o_ref, sc) in enumerate(zip(x_refs, o_refs, self.scatter)):
            if sc:
                continue
            for m in SAME_CORE:
                dev, idx = peers[m - 1]
                self._copy(i, m, x_ref, o_ref.at[idx], dev, send_sems, recv_sems).wait_recv()
                self._copy(i, m ^ 1, o_ref.at[idx], o_ref.at[idx], sibling, send_sems, recv_sems).start()

    def wait(self, x_refs, o_refs, send_sems, recv_sems, local_sems):
        me, peers = _peers()
        for i, (x_ref, o_ref, sc) in enumerate(zip(x_refs, o_refs, self.scatter)):
            for m in range(1, N_DEV):
                dev, idx = peers[m - 1]
                cp = self._copy(i, m, x_ref.at[idx] if sc else x_ref, o_ref.at[idx], dev, send_sems, recv_sems)
                if sc or m not in SAME_CORE:
                    cp.wait_recv()
                cp.wait_send()
            pltpu.make_async_copy(x_ref.at[me] if sc else x_ref, o_ref.at[me], local_sems.at[i]).wait()


def _exchange(ex, name):
    n = ex.n

    def body(*refs):
        x_refs, o_refs, sems = refs[:n], refs[n:2 * n], refs[2 * n:]
        ex.start(x_refs, o_refs, *sems)
        ex.middle(x_refs, o_refs, *sems)
        ex.wait(x_refs, o_refs, *sems)

    hbm = pl.BlockSpec(memory_space=pltpu.HBM)
    return pl.pallas_call(body, name=name, out_shape=ex.out_shapes(), in_specs=[hbm] * n, out_specs=(hbm,) * n,
                          scratch_shapes=ex.sems())(*ex.xs)


def _call(body, *, name, out_shape, grid, in_specs, out_specs, scratch_shapes, args, semantics, ex=None):
    if ex is None:
        return pl.pallas_call(body, name=name, out_shape=out_shape, grid=grid, in_specs=in_specs, out_specs=out_specs,
                              scratch_shapes=scratch_shapes, compiler_params=_params(semantics))(*args), ()
    n, ni, no, ns = ex.n, len(args), len(out_shape), len(scratch_shapes)

    def wrapped(*refs):
        ins, cx = refs[:ni], refs[ni:ni + n]
        outs, co = refs[ni + n:ni + n + no], refs[ni + n + no:ni + 2 * n + no]
        scratch, sems = refs[ni + 2 * n + no:ni + 2 * n + no + ns], refs[ni + 2 * n + no + ns:]
        step = functools.reduce(lambda acc, a: acc * grid[a] + pl.program_id(a), range(len(grid)), 0)
        steps = math.prod(grid)

        @pl.when(step == 0)
        def _():
            ex.start(cx, co, *sems)

        body(*ins, *outs, *scratch)

        @pl.when(step == (steps * 3) // 4 - (steps > 1))
        def _():
            ex.middle(cx, co, *sems)

        @pl.when(step == steps - 1)
        def _():
            ex.wait(cx, co, *sems)

    hbm = pl.BlockSpec(memory_space=pltpu.HBM)
    res = pl.pallas_call(
        wrapped, name=name, out_shape=tuple(out_shape) + ex.out_shapes(), grid=grid,
        in_specs=list(in_specs) + [hbm] * n, out_specs=tuple(out_specs) + (hbm,) * n,
        scratch_shapes=list(scratch_shapes) + ex.sems(),
        compiler_params=_params(("arbitrary",) * len(grid)))(*args, *ex.xs)
    return res[:no], res[no:]


def _pack_flat(arrs):
    cat = jnp.concatenate([a.reshape(-1) for a in arrs])
    per = PACK_COLS * 2 * SUBLANES
    tot = -(-cat.shape[0] // per) * per
    return jnp.pad(cat, (0, tot - cat.shape[0])).reshape(tot // PACK_COLS, PACK_COLS)


def _unpack_flat(packed, shapes):
    flat = packed.reshape(-1)
    out, o = [], 0
    for s in shapes:
        n = math.prod(s)
        out.append(flat[o:o + n].reshape(s))
        o += n
    return out


def _ffn_fwd(x, xb, wt, wo4, g, b, alpha, tag):
    h8, a4 = _ffn_h(xb, wt, tag + "_h")
    out, outb, xh, rstd = _ffn_y_ln(a4, wo4, x, g, b, alpha, 0.5, tag + "_y_ln")
    return out, outb, (xb, h8, a4, xh, rstd)


def _ffn_bwd(dout, saved, wt, wo4, g, alpha, tag, carry=None, own_key=None):
    x, h8, a4, xh, rstd = saved
    load = dict(carry or {})
    slots = [[], [], [], []]
    for n, k in enumerate(sorted(load, key=lambda k: -load[k].size)):
        slots[min(n, 3)].append(k)
    arrived = {}

    def run(fn, slot, *args):
        keys = slots[slot]
        if not keys:
            return fn(*args)
        res, extra = fn(*args, ex=Exchange([load[k] for k in keys], [True] * len(keys)))
        arrived.update(zip(keys, extra))
        return res

    dz, dg, db = _ln_bwd(dout, xh, rstd, g, tag + "_ln_bwd")
    dw_out = run(_ffn_dwout, 3, a4, dz, 0.5, tag + "_dwout")
    dw_out = dw_out.reshape(N_DEV, -1, dw_out.shape[-1])
    if own_key is not None:
        load[own_key] = dw_out
        slots[1].append(own_key)
    dh8 = run(_ffn_dh, 2, dz, wo4, h8, 0.5, tag + "_dh")
    dw_in = run(_ffn_dwin, 0, x, dh8, tag + "_dwin")
    dx = run(_ffn_dx, 1, dh8, wt, dz, alpha, tag + "_dx")
    return dx, dw_in, dw_out, dg, db, arrived


def _fox_fwd(x, xb, w_in_pad, b_f_pad, w_o, g, b, alpha, tag, ex=None):
    S, D = x.shape
    H = D // HEAD_DIM
    proj = _mm(xb, w_in_pad, name=tag + "_proj")
    fl = proj[:, 3 * D:]
    cum = _fox_gate_fwd(fl, b_f_pad)
    cum2 = (cum[:, :H].T * LOG2E).reshape(H // 2, 2, S).transpose(0, 2, 1)
    o, lse, extra = _flash_fwd(proj, cum2, ex)
    m = _mm(o, w_o, name=tag + "_out")
    out, outb, xh, rstd = _res_ln_fwd(x, m, g, b, alpha, 1.0, tag + "_ln")
    return out, outb, (xb, proj, jnp.concatenate([cum2, lse], axis=2), o, fl, xh, rstd), extra


def _fox_bwd(dout, saved, w_in_pad, b_f_pad, w_o, g, alpha, tag, ex=None):
    x, proj, rows, o, fl, xh, rstd = saved
    S, D = x.shape
    H = D // HEAD_DIM
    dz, dg, db = _ln_bwd(dout, xh, rstd, g, tag + "_ln_bwd")
    dw_o = _mm(o, dz, ta=True, name=tag + "_dwo")
    do = _mm(dz, w_o, tb=True, name=tag + "_do")
    dq, dk, dv, dcq, dck, extra = _flash_bwd(proj, rows, o, do, ex)
    dcq = dcq.transpose(0, 2, 1).reshape(H, S)
    dcum = jnp.pad((dcq + dck.reshape(H, S)).T, ((0, 0), (0, LANES - H)))
    dfl, dbf = _fox_gate_bwd(dcum, fl, b_f_pad)
    dproj = jnp.concatenate([dq.astype(BF16), dk.astype(BF16), dv.astype(BF16), dfl.astype(BF16)], axis=1)
    dw_in = _mm(x, dproj, ta=True, name=tag + "_dwin")
    dx = _mm(dproj, w_in_pad, tb=True, add=dz, add_scale=alpha, name=tag + "_dx")
    shards = {"fox_w_in": _split(dw_in[None, :, :3 * D + H], True)[:, 0].astype(BF16),
              "fox_w_o": _split(dw_o[None], False)[:, 0].astype(BF16)}
    return dx, shards, {"fox_b_f": dbf[0, :H], "lnm_g": dg, "lnm_b": db}, extra


def _s5_mats(p):
    lb_re, lb_im, bb_re, bb_im = _s5_discretise(p["a_re"], p["a_im"], p["log_dt"], p["b_re"], p["b_im"])
    rmat = jnp.concatenate([_s5_expand(bb_re.transpose(0, 2, 1)), _s5_expand(bb_im.transpose(0, 2, 1))], axis=2)
    cmat = jnp.concatenate([_s5_expand(p["c_re"]).transpose(0, 2, 1), -_s5_expand(p["c_im"]).transpose(0, 2, 1)],
                           axis=1)
    lam = jnp.concatenate([_s5_slab(lb_re), _s5_slab(lb_im)], axis=2)
    return rmat.astype(BF16), cmat.astype(BF16), lam


def _s5_block_fwd(x, p, w_out, g, b, alpha, tag, ex=None):
    S, D = x.shape
    rmat, cmat, lam = _s5_mats(p)
    dskip = p["d"].reshape(1, D)
    y, yg, hs, extra = _s5_fwd(x, rmat, cmat, lam, dskip, ex)
    vg = _mm(yg, w_out, name=tag + "_vg")
    m = _glu_fwd(vg, tag + "_glu")
    out, outb, xh, rstd = _res_ln_fwd(x, m, g, b, alpha, 1.0, tag + "_ln")
    return out, outb, (x, y, yg, hs, vg, rmat, cmat, lam, dskip, xh, rstd), extra


def _s5_block_bwd(dout, saved, p, w_out, g, alpha, tag, ex=None):
    x, y, yg, hs, vg, rmat, cmat, lam, dskip, xh, rstd = saved
    S, D = x.shape
    G = D // S5_GROUP
    dz, dg, db = _ln_bwd(dout, xh, rstd, g, tag + "_ln_bwd")
    dvg = _glu_bwd(vg, dz, tag + "_glu_bwd")
    dw_out = _mm(yg, dvg, ta=True, name=tag + "_dwout")
    dyg = _mm(dvg, w_out, tb=True, name=tag + "_dyg")
    dx, dr, dc, dlam, dd, extra = _s5_bwd(x, y, dyg, hs, rmat, cmat, lam, dskip, dz, alpha, ex)
    dbb_re = _s5_extract(dr[:, :, :LANES]).transpose(0, 2, 1)
    dbb_im = _s5_extract(dr[:, :, LANES:]).transpose(0, 2, 1)
    dc_re = _s5_extract(dc[:, :LANES, :].transpose(0, 2, 1))
    dc_im = -_s5_extract(dc[:, LANES:, :].transpose(0, 2, 1))
    dlb_re = dlam[:, :, :LANES].reshape(G, S5_STATE)
    dlb_im = dlam[:, :, LANES:].reshape(G, S5_STATE)
    _, vjp = jax.vjp(_s5_discretise, p["a_re"], p["a_im"], p["log_dt"], p["b_re"], p["b_im"])
    da_re, da_im, dlog_dt, db_re, db_im = vjp((dlb_re, dlb_im, dbb_re, dbb_im))
    small = dict(s5_a_re=da_re, s5_a_im=da_im, s5_log_dt=dlog_dt, s5_b_re=db_re, s5_b_im=db_im, s5_c_re=dc_re,
                 s5_c_im=dc_im, s5_d=dd.reshape(G, S5_GROUP), lnm_g=dg, lnm_b=db)
    return dx, {"s5_w_out": _split(dw_out[None], True)[:, 0].astype(BF16)}, small, extra


FFN_NAMES = ("ffn1_w_in", "ffn1_w_out", "ffn2_w_in", "ffn2_w_out")
BIG = FFN_NAMES + ("fox_w_in", "fox_w_o", "s5_w_out")
BIG_SPLIT_COLS = {"ffn1_w_in": True, "ffn1_w_out": False, "ffn2_w_in": True, "ffn2_w_out": False,
                  "fox_w_in": True, "fox_w_o": False, "s5_w_out": True}
SMALL = ("ln1_g", "ln1_b", "lnm_g", "lnm_b", "ln2_g", "ln2_b", "fox_b_f", "s5_a_re", "s5_a_im", "s5_log_dt",
         "s5_b_re", "s5_b_im", "s5_c_re", "s5_c_im", "s5_d")
WEIGHTS = ("ffn1_w_in", "ffn1_w_out", "ln1_g", "ln1_b", "lnm_g", "lnm_b", "ffn2_w_in", "ffn2_w_out", "ln2_g", "ln2_b",
           "fox_w_in", "fox_b_f", "fox_w_o", "s5_a_re", "s5_a_im", "s5_log_dt", "s5_b_re", "s5_b_im", "s5_c_re",
           "s5_c_im", "s5_d", "s5_w_out")


def _join(gathered, split_cols):
    n, L, r, c = gathered.shape
    if split_cols:
        return gathered.transpose(1, 2, 0, 3).reshape(L, r, n * c)
    return gathered.transpose(1, 0, 2, 3).reshape(L, n * r, c)


def _split(full, split_cols):
    L, R, C = full.shape
    if split_cols:
        return full.reshape(L, R, N_DEV, C // N_DEV).transpose(2, 0, 1, 3)
    return full.reshape(L, N_DEV, R // N_DEV, C).transpose(1, 0, 2, 3)


def _group(i, part):
    if part == "b":
        return (("ffn2_w_in", i), ("ffn2_w_out", i))
    mixer = (("fox_w_in", i // 2), ("fox_w_o", i // 2)) if i % 2 == 0 else (("s5_w_out", i // 2),)
    return (("ffn1_w_in", i), ("ffn1_w_out", i)) + mixer


def _prepare(name, g8):
    if name in ("ffn1_w_in", "ffn2_w_in"):
        return g8
    if name in FFN_NAMES:
        n, r, c = g8.shape
        return g8.reshape(n // 2, 2 * r, c)
    full = _join(g8[:, None], BIG_SPLIT_COLS[name])[0]
    if name == "fox_w_in":
        full = jnp.pad(full, ((0, 0), (0, LANES - full.shape[0] // HEAD_DIM)))
    return full


def _local_step(x, target, small, shard_of=None, pregathered=None):
    S, D = x.shape
    H = D // HEAD_DIM
    depth = small["ln1_g"].shape[0]
    alpha = (2.0 * depth) ** 0.25
    local = pregathered is not None
    bf_pad = jnp.pad(small["fox_b_f"], ((0, 0), (0, LANES - H)))

    def s5_params(j):
        return {k: small["s5_" + k][j] for k in ("a_re", "a_im", "log_dt", "b_re", "b_im", "c_re", "c_im", "d")}

    if local:
        W = {k: _prepare(k[0], g8) for k, g8 in pregathered.items()}
    else:
        keys = _group(0, "a")
        got = _exchange(Exchange([shard_of(*k) for k in keys], [False] * len(keys)), "gather_first")
        W = {k: _prepare(k[0], g8) for k, g8 in zip(keys, got)}
    saved = []
    h, hb = x, x.astype(BF16)
    for i in range(depth):
        j = i // 2
        keys = _group(i, "b") + (_group(i + 1, "a") if i + 1 < depth else ())
        ex = None if local else Exchange([shard_of(*k) for k in keys], [False] * len(keys))
        h, hb, s1 = _ffn_fwd(h, hb, W[("ffn1_w_in", i)], W[("ffn1_w_out", i)], small["ln1_g"][i],
                             small["ln1_b"][i], alpha, f"l{i}_ffn1")
        if i % 2 == 0:
            h, hb, s2, extra = _fox_fwd(h, hb, W[("fox_w_in", j)], bf_pad[j:j + 1], W[("fox_w_o", j)],
                                        small["lnm_g"][i], small["lnm_b"][i], alpha, f"l{i}_fox", ex)
        else:
            h, hb, s2, extra = _s5_block_fwd(h, s5_params(j), W[("s5_w_out", j)], small["lnm_g"][i],
                                             small["lnm_b"][i], alpha, f"l{i}_s5", ex)
        if not local:
            W.update({k: _prepare(k[0], g8) for k, g8 in zip(keys, extra)})
        h, hb, s3 = _ffn_fwd(h, hb, W[("ffn2_w_in", i)], W[("ffn2_w_out", i)], small["ln2_g"][i],
                             small["ln2_b"][i], alpha, f"l{i}_ffn2")
        saved.append((s1, s2, s3))

    dh, loss_part = _loss_fwd_bwd(h, target)

    arrived = {}
    pending = {}
    gs = {k: [None] * small[k].shape[0] for k in SMALL}
    for i in reversed(range(depth)):
        j = i // 2
        s1, s2, s3 = saved[i]
        dh, dw_in, dw_out, gs["ln2_g"][i], gs["ln2_b"][i], got = _ffn_bwd(
            dh, s3, W[("ffn2_w_in", i)], W[("ffn2_w_out", i)], small["ln2_g"][i], alpha, f"l{i}_ffn2",
            None if local else pending)
        arrived.update(pending if local else got)
        pending = {("ffn2_w_in", i): dw_in, ("ffn2_w_out", i): dw_out}
        keys = list(pending)
        ex = None if local else Exchange([pending[k] for k in keys], [True] * len(keys))
        if i % 2 == 0:
            dh, mix, sg, extra = _fox_bwd(dh, s2, W[("fox_w_in", j)], bf_pad[j:j + 1], W[("fox_w_o", j)],
                                          small["lnm_g"][i], alpha, f"l{i}_fox", ex)
        else:
            dh, mix, sg, extra = _s5_block_bwd(dh, s2, s5_params(j), W[("s5_w_out", j)], small["lnm_g"][i], alpha,
                                               f"l{i}_s5", ex)
        arrived.update(zip(keys, [pending[k] for k in keys] if local else extra))
        pending = {(k, j): val for k, val in mix.items()}
        for k, val in sg.items():
            gs[k][i if k in ("lnm_g", "lnm_b") else j] = val
        own = ("ffn1_w_out", i) if (i == 0 and not local) else None
        dh, dw_in, dw_out, gs["ln1_g"][i], gs["ln1_b"][i], got = _ffn_bwd(
            dh, s1, W[("ffn1_w_in", i)], W[("ffn1_w_out", i)], small["ln1_g"][i], alpha, f"l{i}_ffn1",
            None if local else pending, own)
        arrived.update(pending if local else got)
        pending = {("ffn1_w_in", i): dw_in}
        if own is None:
            pending[("ffn1_w_out", i)] = dw_out
    gs = {k: jnp.stack(v) for k, v in gs.items()}
    return loss_part, dh, arrived, pending, gs


def kernel(x, ffn1_w_in, ffn1_w_out, ln1_g, ln1_b, lnm_g, lnm_b, ffn2_w_in, ffn2_w_out, ln2_g, ln2_b, fox_w_in, fox_b_f, fox_w_o, s5_a_re, s5_a_im, s5_log_dt, s5_b_re, s5_b_im, s5_c_re, s5_c_im, s5_d, s5_w_out, loss_target, m_ffn1_w_in, m_ffn1_w_out, m_ln1_g, m_ln1_b, m_lnm_g, m_lnm_b, m_ffn2_w_in, m_ffn2_w_out, m_ln2_g, m_ln2_b, m_fox_w_in, m_fox_b_f, m_fox_w_o, m_s5_a_re, m_s5_a_im, m_s5_log_dt, m_s5_b_re, m_s5_b_im, m_s5_c_re, m_s5_c_im, m_s5_d, m_s5_w_out, v_ffn1_w_in, v_ffn1_w_out, v_ln1_g, v_ln1_b, v_lnm_g, v_lnm_b, v_ffn2_w_in, v_ffn2_w_out, v_ln2_g, v_ln2_b, v_fox_w_in, v_fox_b_f, v_fox_w_o, v_s5_a_re, v_s5_a_im, v_s5_log_dt, v_s5_b_re, v_s5_b_im, v_s5_c_re, v_s5_c_im, v_s5_d, v_s5_w_out):
    args = dict(locals())
    w = {k: args[k] for k in WEIGHTS}
    m = {k: args["m_" + k] for k in WEIGHTS}
    v = {k: args["v_" + k] for k in WEIGHTS}
    small = {k: w[k] for k in SMALL}

    turned = lambda d: {k: jnp.swapaxes(a, 1, 2) if k in ("ffn1_w_in", "ffn2_w_in") else a for k, a in d.items()}
    w, m, v = turned(w), turned(m), turned(v)
    wb = {k: w[k].astype(BF16) for k in BIG}
    loss_part, dx, arrived, last, gs = _local_step(x[0], loss_target[0], small, lambda k, idx: wb[k][idx])
    loss = lax.psum(loss_part, ("x", "y", "c"))

    small_shapes = [w[k].shape for k in SMALL]
    ex = Exchange(list(last.values()) + [_pack_flat([gs[k] for k in SMALL]).astype(BF16)],
                  [True] * len(last) + [False])
    *got, g_small_all = _exchange(ex, "scatter_last_gather_small")
    arrived.update(zip(last, got))
    g_small_flat = _sum8(g_small_all, "sum_small_grads")

    grads, delta, new_m, new_v = {}, {}, {}, {}
    for k in BIG:
        parts = [arrived[(k, l)] for l in range(w[k].shape[0])]
        grads[k], delta[k], new_m[k], new_v[k] = _adamw_recv(parts, w[k], m[k], v[k], "adamw_" + k)
    pk = lambda d: _pack_flat([d[k] for k in SMALL])
    d_, m_, v_ = _adamw(pk(w), g_small_flat, pk(m), pk(v), "adamw_small")
    for dst, flat in ((grads, g_small_flat), (delta, d_), (new_m, m_), (new_v, v_)):
        dst.update(zip(SMALL, _unpack_flat(flat, small_shapes)))
    grads, delta, new_m, new_v = turned(grads), turned(delta), turned(new_m), turned(new_v)

    return (loss, dx[None], *[grads[k] for k in WEIGHTS], *[delta[k] for k in WEIGHTS],
            *[new_m[k] for k in WEIGHTS], *[new_v[k] for k in WEIGHTS])
```

```python
import functools
import math

import jax
import jax.numpy as jnp
from jax import lax
from jax.experimental import pallas as pl
from jax.experimental.pallas import tpu as pltpu

F32 = jnp.float32
BF16 = jnp.bfloat16

N_DEV = 8
HEAD_DIM = 64
S5_GROUP = 16
S5_STATE = 64
LANES = 128
SUBLANES = 8
S5_PART = 256
ATTN_TILE = 512
LN_EPS = 1e-5
NEG_INF = -1e30
LOG2E = 1.4426950408889634
ADAM_LR, ADAM_B1, ADAM_B2, ADAM_EPS, ADAM_WD, ADAM_STEP = 0.001, 0.9, 0.999, 1e-08, 0.01, 10
VMEM_LIMIT = 48 * 1024 * 1024
PACK_COLS = 1024

MESH = pl.DeviceIdType.MESH


def _tile(dim, pref, align=LANES):
    if dim <= pref:
        return dim
    t = (pref // align) * align
    while t >= align:
        if dim % t == 0:
            return t
        t -= align
    return dim


def _params(sem):
    return pltpu.CompilerParams(dimension_semantics=sem, vmem_limit_bytes=VMEM_LIMIT)


def _mm(a, b, *, ta=False, tb=False, out_dtype=F32, scale=None, add=None, add_scale=1.0,
        tm=512, tn=1408, tk=1408, name="mm"):
    if ta:
        K, M = a.shape
    else:
        M, K = a.shape
    if tb:
        N, K2 = b.shape
    else:
        K2, N = b.shape
    assert K == K2, (a.shape, b.shape, ta, tb)
    tm, tn, tk = _tile(M, tm), _tile(N, tn), _tile(K, tk)
    a_spec = pl.BlockSpec((tk, tm), lambda i, j, k: (k, i)) if ta else pl.BlockSpec((tm, tk), lambda i, j, k: (i, k))
    b_spec = pl.BlockSpec((tn, tk), lambda i, j, k: (j, k)) if tb else pl.BlockSpec((tk, tn), lambda i, j, k: (k, j))
    o_spec = pl.BlockSpec((tm, tn), lambda i, j, k: (i, j))
    return _mm_core(name, a, b, a_spec, b_spec, o_spec, jax.ShapeDtypeStruct((M, N), out_dtype),
                    (M // tm, N // tn, K // tk), (tm, tn), ta, tb, scale, add, add_scale)


def _mm_core(name, a, b, a_spec, b_spec, o_spec, out_shape, grid, acc_shape, ta, tb, scale=None, add=None,
             add_scale=1.0, ex=None):
    nk = grid[2]

    def body(*refs):
        if add is None:
            a_ref, b_ref, o_ref, acc = refs
            add_ref = None
        else:
            a_ref, b_ref, add_ref, o_ref, acc = refs
        k = pl.program_id(2)

        @pl.when(k == 0)
        def _():
            acc[...] = jnp.zeros_like(acc)

        dims = (((0 if ta else 1,), (1 if tb else 0,)), ((), ()))
        acc[...] += lax.dot_general(a_ref[...].astype(BF16), b_ref[...].astype(BF16), dims,
                                    preferred_element_type=F32)

        @pl.when(k == nk - 1)
        def _():
            r = acc[...]
            if scale is not None:
                r = r * scale
            if add_ref is not None:
                r = r + add_scale * add_ref[...]
            o_ref[...] = r.astype(out_shape.dtype)

    in_specs = [a_spec, b_spec]
    args = [a, b]
    if add is not None:
        in_specs.append(o_spec)
        args.append(add)
    (res,), extra = _call(body, name=name, out_shape=(out_shape,), grid=grid, in_specs=in_specs, out_specs=(o_spec,),
                          scratch_shapes=[pltpu.VMEM(acc_shape, F32)], args=args,
                          semantics=("parallel", "parallel", "arbitrary"), ex=ex)
    return res if ex is None else (res, extra)


def _ffn_h(x, wt, name, ex=None):
    S, D = x.shape
    n, c, _ = wt.shape
    nb = n // 2
    tm = _tile(S, 1024, SUBLANES)
    dims = (((1,), (1,)), ((), ()))

    def body(x_ref, wg_ref, wu_ref, h_ref, a_ref):
        xb = x_ref[...]
        g = lax.dot_general(xb, wg_ref[...], dims, preferred_element_type=F32).astype(BF16)
        u = lax.dot_general(xb, wu_ref[...], dims, preferred_element_type=F32).astype(BF16)
        h_ref[0] = g
        h_ref[1] = u
        g = g.astype(F32)
        a_ref[...] = (g * _sigmoid(g) * u.astype(F32)).astype(BF16)

    (h, a4), extra = _call(
        body, name=name,
        out_shape=(jax.ShapeDtypeStruct((2, nb, S, c), BF16), jax.ShapeDtypeStruct((nb, S, c), BF16)),
        grid=(S // tm, nb),
        in_specs=[pl.BlockSpec((tm, D), lambda i, j: (i, 0)), pl.BlockSpec((None, c, D), lambda i, j: (j, 0, 0)),
                  pl.BlockSpec((None, c, D), lambda i, j: (j + nb, 0, 0))],
        out_specs=(pl.BlockSpec((2, None, tm, c), lambda i, j: (0, j, i, 0)),
                   pl.BlockSpec((None, tm, c), lambda i, j: (j, i, 0))),
        scratch_shapes=[], args=(x, wt, wt), semantics=("parallel", "parallel"), ex=ex)
    return h.reshape(n, S, c), a4, extra


def _ffn_y_ln(a4, wo4, x, g, b, alpha, s, name):
    nb, S, c = a4.shape
    D = wo4.shape[-1]
    tm = _tile(S, 512, SUBLANES)

    def body(a_ref, w_ref, x_ref, g_ref, b_ref, o_ref, ob_ref, xh_ref, r_ref, acc):
        k = pl.program_id(1)

        @pl.when(k == 0)
        def _():
            acc[...] = jnp.zeros_like(acc)

        acc[...] += jnp.dot(a_ref[...], w_ref[...], preferred_element_type=F32)

        @pl.when(k == nb - 1)
        def _():
            z = alpha * x_ref[...] + s * acc[...]
            mu = jnp.mean(z, axis=-1, keepdims=True)
            zc = z - mu
            rstd = lax.rsqrt(jnp.mean(zc * zc, axis=-1, keepdims=True) + LN_EPS)
            xh = zc * rstd
            xh_ref[...] = xh
            r_ref[...] = rstd
            out = xh * g_ref[...] + b_ref[...]
            o_ref[...] = out
            ob_ref[...] = out.astype(BF16)

    row = pl.BlockSpec((tm, D), lambda i, k: (i, 0))
    vec = pl.BlockSpec((1, D), lambda i, k: (0, 0))
    return pl.pallas_call(
        body, name=name,
        out_shape=(jax.ShapeDtypeStruct((S, D), F32), jax.ShapeDtypeStruct((S, D), BF16),
                   jax.ShapeDtypeStruct((S, D), F32), jax.ShapeDtypeStruct((S, 1), F32)),
        grid=(S // tm, nb),
        in_specs=[pl.BlockSpec((None, tm, c), lambda i, k: (k, i, 0)), pl.BlockSpec((None, c, D), lambda i, k: (k, 0, 0)),
                  row, vec, vec],
        out_specs=(row, row, row, pl.BlockSpec((tm, 1), lambda i, k: (i, 0))),
        scratch_shapes=[pltpu.VMEM((tm, D), F32)], compiler_params=_params(("parallel", "arbitrary")),
    )(a4, wo4, x, g.reshape(1, D), b.reshape(1, D))


def _ffn_dwout(a4, dz, scale, name, ex=None):
    nb, S, c = a4.shape
    D = dz.shape[1]
    tk, tn = _tile(S, 1024, SUBLANES), _tile(D, 1024)
    return _mm_core(name, a4, dz, pl.BlockSpec((None, tk, c), lambda i, j, k: (i, k, 0)),
                    pl.BlockSpec((tk, tn), lambda i, j, k: (k, j)),
                    pl.BlockSpec((None, c, tn), lambda i, j, k: (i, 0, j)),
                    jax.ShapeDtypeStruct((nb, c, D), BF16), (nb, D // tn, S // tk), (c, tn), True, False, scale,
                    ex=ex)


def _ffn_dh(dz, wo4, h8, scale, name, ex=None):
    S, D = dz.shape
    nb, c = wo4.shape[0], wo4.shape[1]
    tm = _tile(S, 512, SUBLANES)

    def body(dz_ref, w_ref, h_ref, d_ref):
        da = lax.dot_general(dz_ref[...].astype(BF16), w_ref[...], (((1,), (1,)), ((), ())),
                             preferred_element_type=F32) * scale
        g = h_ref[0].astype(F32)
        u = h_ref[1].astype(F32)
        sg = _sigmoid(g)
        silu = g * sg
        d_ref[0] = (da * u * (sg + silu * (1.0 - sg))).astype(BF16)
        d_ref[1] = (da * silu).astype(BF16)

    pair = pl.BlockSpec((2, None, tm, c), lambda i, j: (0, j, i, 0))
    (dh,), extra = _call(
        body, name=name, out_shape=(jax.ShapeDtypeStruct((2, nb, S, c), BF16),), grid=(S // tm, nb),
        in_specs=[pl.BlockSpec((tm, D), lambda i, j: (i, 0)), pl.BlockSpec((None, c, D), lambda i, j: (j, 0, 0)),
                  pair],
        out_specs=(pair,), scratch_shapes=[], args=(dz, wo4, h8.reshape(2, nb, S, c)),
        semantics=("parallel", "parallel"), ex=ex)
    dh = dh.reshape(2 * nb, S, c)
    return dh if ex is None else (dh, extra)


def _ffn_dwin(x, dh8, name, ex=None):
    S, D = x.shape
    n, _, c = dh8.shape
    return _mm_core(name, dh8, x, pl.BlockSpec((None, S, c), lambda i, j, k: (i, 0, 0)),
                    pl.BlockSpec((S, D), lambda i, j, k: (0, 0)),
                    pl.BlockSpec((None, c, D), lambda i, j, k: (i, 0, 0)),
                    jax.ShapeDtypeStruct((n, c, D), BF16), (n, 1, 1), (c, D), True, False, ex=ex)


def _ffn_dx(dh8, wt, dz, alpha, name, ex=None):
    n, S, c = dh8.shape
    D = wt.shape[2]
    tm, tn = _tile(S, 1024, SUBLANES), _tile(D, 1024)
    return _mm_core(name, dh8, wt, pl.BlockSpec((None, tm, c), lambda i, j, k: (k, i, 0)),
                    pl.BlockSpec((None, c, tn), lambda i, j, k: (k, 0, j)),
                    pl.BlockSpec((tm, tn), lambda i, j, k: (i, j)),
                    jax.ShapeDtypeStruct((S, D), F32), (S // tm, D // tn, n), (tm, tn), False, False,
                    None, dz, alpha, ex=ex)


def _res_ln_fwd(x, y, g, b, alpha, s, name):
    S, D = x.shape
    tm = _tile(S, 256, SUBLANES)

    def body(x_ref, y_ref, g_ref, b_ref, o_ref, ob_ref, xh_ref, r_ref):
        z = alpha * x_ref[...] + s * y_ref[...]
        mu = jnp.mean(z, axis=-1, keepdims=True)
        zc = z - mu
        var = jnp.mean(zc * zc, axis=-1, keepdims=True)
        rstd = lax.rsqrt(var + LN_EPS)
        xh = zc * rstd
        xh_ref[...] = xh
        r_ref[...] = rstd
        out = xh * g_ref[...] + b_ref[...]
        o_ref[...] = out
        ob_ref[...] = out.astype(BF16)

    row = pl.BlockSpec((tm, D), lambda i: (i, 0))
    vec = pl.BlockSpec((1, D), lambda i: (0, 0))
    return pl.pallas_call(
        body, name=name,
        out_shape=(jax.ShapeDtypeStruct((S, D), F32), jax.ShapeDtypeStruct((S, D), BF16),
                   jax.ShapeDtypeStruct((S, D), F32), jax.ShapeDtypeStruct((S, 1), F32)),
        grid=(S // tm,), in_specs=[row, row, vec, vec],
        out_specs=(row, row, row, pl.BlockSpec((tm, 1), lambda i: (i, 0))),
        compiler_params=_params(("parallel",)),
    )(x, y, g.reshape(1, D), b.reshape(1, D))


def _ln_bwd(dout, xh, rstd, g, name):
    S, D = dout.shape
    tm = _tile(S, 256, SUBLANES)

    def body(d_ref, xh_ref, r_ref, g_ref, dz_ref, dg_ref, db_ref):
        i = pl.program_id(0)

        @pl.when(i == 0)
        def _():
            dg_ref[...] = jnp.zeros_like(dg_ref)
            db_ref[...] = jnp.zeros_like(db_ref)

        d = d_ref[...]
        xhv = xh_ref[...]
        dxh = d * g_ref[...]
        m1 = jnp.mean(dxh, axis=-1, keepdims=True)
        m2 = jnp.mean(dxh * xhv, axis=-1, keepdims=True)
        dz_ref[...] = r_ref[...] * (dxh - m1 - xhv * m2)
        dg_ref[...] += jnp.sum(d * xhv, axis=0, keepdims=True)
        db_ref[...] += jnp.sum(d, axis=0, keepdims=True)

    row = pl.BlockSpec((tm, D), lambda i: (i, 0))
    vec = pl.BlockSpec((1, D), lambda i: (0, 0))
    dz, dg, db = pl.pallas_call(
        body, name=name,
        out_shape=(jax.ShapeDtypeStruct((S, D), F32), jax.ShapeDtypeStruct((1, D), F32),
                   jax.ShapeDtypeStruct((1, D), F32)),
        grid=(S // tm,), in_specs=[row, row, pl.BlockSpec((tm, 1), lambda i: (i, 0)), vec],
        out_specs=(row, vec, vec),
        compiler_params=_params(("arbitrary",)),
    )(dout, xh, rstd, g.reshape(1, D))
    return dz, dg[0], db[0]


def _sigmoid(x):
    return 0.5 * jnp.tanh(0.5 * x) + 0.5


def _glu_fwd(vg, name):
    S, D2 = vg.shape
    D = D2 // 2
    tm = _tile(S, 512, SUBLANES)

    def body(v_ref, g_ref, o_ref):
        o_ref[...] = v_ref[...] * _sigmoid(g_ref[...])

    return pl.pallas_call(
        body, name=name, out_shape=jax.ShapeDtypeStruct((S, D), F32), grid=(S // tm,),
        in_specs=[pl.BlockSpec((tm, D), lambda i: (i, 0)), pl.BlockSpec((tm, D), lambda i: (i, 1))],
        out_specs=pl.BlockSpec((tm, D), lambda i: (i, 0)),
        compiler_params=_params(("parallel",)),
    )(vg, vg)


def _glu_bwd(vg, dm, name):
    S, D2 = vg.shape
    D = D2 // 2
    tm = _tile(S, 512, SUBLANES)

    def body(v_ref, g_ref, dm_ref, dv_ref, dg_ref):
        sg = _sigmoid(g_ref[...])
        d = dm_ref[...]
        dv_ref[...] = (d * sg).astype(BF16)
        dg_ref[...] = (d * v_ref[...] * sg * (1.0 - sg)).astype(BF16)

    blk = pl.BlockSpec((tm, D), lambda i: (i, 0))
    dv, dg = pl.pallas_call(
        body, name=name,
        out_shape=(jax.ShapeDtypeStruct((S, D), BF16), jax.ShapeDtypeStruct((S, D), BF16)),
        grid=(S // tm,), in_specs=[blk, pl.BlockSpec((tm, D), lambda i: (i, 1)), blk],
        out_specs=(blk, blk), compiler_params=_params(("parallel",)),
    )(vg, vg, dm)
    return jnp.concatenate([dv, dg], axis=1)


def _loss_fwd_bwd(y, target):
    S, D = y.shape
    tm = _tile(S, 256, SUBLANES)

    def body(y_ref, t_ref, dy_ref, l_ref):
        i = pl.program_id(0)

        @pl.when(i == 0)
        def _():
            l_ref[...] = jnp.zeros_like(l_ref)

        e = y_ref[...] - t_ref[...]
        dy_ref[...] = e * (1.0 / D)
        l_ref[...] += jnp.sum(e * e, axis=0, keepdims=True) * (0.5 / D)

    row = pl.BlockSpec((tm, D), lambda i: (i, 0))
    dy, part = pl.pallas_call(
        body, name="loss", out_shape=(jax.ShapeDtypeStruct((S, D), F32), jax.ShapeDtypeStruct((1, D), F32)),
        grid=(S // tm,), in_specs=[row, row], out_specs=(row, pl.BlockSpec((1, D), lambda i: (0, 0))),
        compiler_params=_params(("arbitrary",)),
    )(y, target)
    return dy, jnp.sum(part)


def _tri(n, lower):
    r = lax.broadcasted_iota(jnp.int32, (n, n), 0)
    c = lax.broadcasted_iota(jnp.int32, (n, n), 1)
    return jnp.where((c <= r) if lower else (c >= r), 1.0, 0.0)


def _fox_gate_fwd(fl, bf):
    S, W = fl.shape
    tm = _tile(S, 256, SUBLANES)

    def body(fl_ref, b_ref, c_ref, carry):
        i = pl.program_id(0)

        @pl.when(i == 0)
        def _():
            carry[...] = jnp.zeros_like(carry)

        x = fl_ref[...] + b_ref[...]
        lf = jnp.minimum(x, 0.0) - jnp.log(1.0 + jnp.exp(-jnp.abs(x)))
        c_ref[...] = jnp.dot(_tri(tm, True), lf, precision=lax.Precision.HIGHEST,
                             preferred_element_type=F32) + carry[...]
        carry[...] += jnp.sum(lf, axis=0, keepdims=True)

    blk = pl.BlockSpec((tm, W), lambda i: (i, 0))
    return pl.pallas_call(
        body, name="fox_gate_fwd", out_shape=jax.ShapeDtypeStruct((S, W), F32), grid=(S // tm,),
        in_specs=[blk, pl.BlockSpec((1, W), lambda i: (0, 0))], out_specs=blk,
        scratch_shapes=[pltpu.VMEM((1, W), F32)], compiler_params=_params(("arbitrary",)),
    )(fl, bf)


def _fox_gate_bwd(dcum, fl, bf):
    S, W = fl.shape
    tm = _tile(S, 256, SUBLANES)
    nb = S // tm

    def body(dc_ref, fl_ref, b_ref, dfl_ref, db_ref, carry):
        i = pl.program_id(0)

        @pl.when(i == 0)
        def _():
            carry[...] = jnp.zeros_like(carry)
            db_ref[...] = jnp.zeros_like(db_ref)

        dc = dc_ref[...]
        r = jnp.dot(_tri(tm, False), dc, precision=lax.Precision.HIGHEST, preferred_element_type=F32) + carry[...]
        carry[...] += jnp.sum(dc, axis=0, keepdims=True)
        x = fl_ref[...] + b_ref[...]
        e = jnp.exp(-jnp.abs(x))
        dfl = r * jnp.where(x >= 0, e, 1.0) / (1.0 + e)
        dfl_ref[...] = dfl
        db_ref[...] += jnp.sum(dfl, axis=0, keepdims=True)

    blk = pl.BlockSpec((tm, W), lambda i: (nb - 1 - i, 0))
    vec = pl.BlockSpec((1, W), lambda i: (0, 0))
    return pl.pallas_call(
        body, name="fox_gate_bwd",
        out_shape=(jax.ShapeDtypeStruct((S, W), F32), jax.ShapeDtypeStruct((1, W), F32)),
        grid=(nb,), in_specs=[blk, blk, vec], out_specs=(blk, vec),
        scratch_shapes=[pltpu.VMEM((1, W), F32)], compiler_params=_params(("arbitrary",)),
    )(dcum, fl, bf)


def _causal(t):
    row = lax.broadcasted_iota(jnp.int32, (t, t), 0)
    col = lax.broadcasted_iota(jnp.int32, (t, t), 1)
    return col <= row


def _first_head(shape):
    return lax.broadcasted_iota(jnp.int32, shape, len(shape) - 1) < HEAD_DIM


def _split3(x):
    hi = x.astype(BF16).astype(F32)
    r = x - hi
    mid = r.astype(BF16).astype(F32)
    return hi, mid, (r - mid).astype(BF16).astype(F32)


def _bias_lanes(c, query):
    lane = lax.broadcasted_iota(jnp.int32, (c.shape[0], LANES), 1)
    hi, mid, lo = _split3(c)
    if query:
        out = jnp.where(lane == 0, hi, jnp.where(lane == 1, mid, jnp.where(lane == 2, lo,
                                                                          jnp.where(lane < 6, 1.0, 0.0))))
    else:
        out = jnp.where(lane < 3, 1.0, jnp.where(lane == 3, -hi, jnp.where(lane == 4, -mid,
                                                                          jnp.where(lane == 5, -lo, 0.0))))
    return out.astype(BF16)


def _flash_fwd(proj, cum2, ex=None):
    S = proj.shape[0]
    D = (proj.shape[1] - LANES) // 3
    HP = D // LANES
    t = _tile(S, ATTN_TILE)
    nq = S // t
    scale = 1.0 / math.sqrt(HEAD_DIM)

    def body(q_ref, k_ref, v_ref, cq_ref, ck_ref, o_ref, lse_ref, kx, vb):
        qi = pl.program_id(1)

        @pl.when(qi == 0)
        def _():
            def prep(kb, c):
                sl = pl.ds(pl.multiple_of(kb * t, t), t)
                k16 = k_ref[sl, :].astype(BF16)
                ckv = ck_ref[0, sl, :]
                for a in range(2):
                    kx[a, sl, 0:LANES] = k16
                    kx[a, sl, LANES:2 * LANES] = _bias_lanes(ckv[:, a:a + 1], False)
                vb[sl, :] = v_ref[sl, :].astype(BF16)
                return c

            lax.fori_loop(0, nq, prep, 0)

        first = _first_head((t, LANES))
        qf = q_ref[...] * (scale * LOG2E)
        cqv = cq_ref[0]
        qx = [jnp.concatenate([jnp.where(keep, qf, 0.0).astype(BF16), _bias_lanes(cqv[:, a:a + 1], True)], axis=1)
              for a, keep in enumerate((first, jnp.logical_not(first)))]

        def block(ki, carry, masked):
            m_old, l_old, acc = carry
            sl = pl.ds(pl.multiple_of(ki * t, t), t)
            vv = vb[sl, :]
            m_new, l_new, corr, pv = [], [], [], []
            for a in range(2):
                s = lax.dot_general(qx[a], kx[a, sl, :], (((1,), (1,)), ((), ())), preferred_element_type=F32)
                if masked:
                    s = jnp.where(_causal(t), s, NEG_INF)
                m_a = jnp.maximum(m_old[a], jnp.max(s, axis=1, keepdims=True))
                p = jnp.exp2(s - m_a)
                c_a = jnp.exp2(m_old[a] - m_a)
                m_new.append(m_a)
                corr.append(c_a)
                l_new.append(c_a * l_old[a] + jnp.sum(p, axis=1, keepdims=True))
                pv.append(jnp.dot(p.astype(BF16), vv, preferred_element_type=F32))
            acc = jnp.where(first, corr[0] * acc + pv[0], corr[1] * acc + pv[1])
            return tuple(m_new), tuple(l_new), acc

        neg = jnp.full((t, 1), NEG_INF, F32)
        zero = jnp.zeros((t, 1), F32)
        carry = lax.fori_loop(0, qi, lambda ki, c: block(ki, c, False),
                              ((neg, neg), (zero, zero), jnp.zeros((t, LANES), F32)))
        m, l, acc = block(qi, carry, True)
        o_ref[...] = acc / jnp.where(first, l[0], l[1])
        lse_ref[0, :, 0:1] = m[0] + jnp.log2(l[0])
        lse_ref[0, :, 1:2] = m[1] + jnp.log2(l[1])

    qblk = pl.BlockSpec((t, LANES), lambda h, i: (i, h))
    r2 = pl.BlockSpec((1, t, 2), lambda h, i: (h, i, 0))
    (o, lse), extra = _call(
        body, name="fox_attn_fwd",
        out_shape=(jax.ShapeDtypeStruct((S, D), F32), jax.ShapeDtypeStruct((HP, S, 2), F32)),
        grid=(HP, nq),
        in_specs=[qblk, pl.BlockSpec((S, LANES), lambda h, i: (0, HP + h)),
                  pl.BlockSpec((S, LANES), lambda h, i: (0, 2 * HP + h)), r2,
                  pl.BlockSpec((1, S, 2), lambda h, i: (h, 0, 0))],
        out_specs=(qblk, r2),
        scratch_shapes=[pltpu.VMEM((2, S, 2 * LANES), BF16), pltpu.VMEM((S, LANES), BF16)],
        args=(proj, proj, proj, cum2, cum2), semantics=("parallel", "arbitrary"), ex=ex)
    return o, lse, extra


def _flash_bwd(proj, rows, o, do, ex=None):
    S = proj.shape[0]
    D = (proj.shape[1] - LANES) // 3
    HP = D // LANES
    t = _tile(S, ATTN_TILE)
    nb = S // t
    scale = 1.0 / math.sqrt(HEAD_DIM)

    def body(q_ref, k_ref, v_ref, rows_ref, o_ref, do_ref, dq_ref, dk_ref, dv_ref, dcq_ref, dck_ref,
             q_s, do_s, dl_s, dq_acc, dk_acc, dv_acc, dc_acc):
        kb = pl.program_id(1)

        @pl.when(kb == 0)
        def _():
            dq_acc[...] = jnp.zeros_like(dq_acc)
            dcq_ref[...] = jnp.zeros_like(dcq_ref)

            def prep(qb, c):
                sl = pl.ds(pl.multiple_of(qb * t, t), t)
                first = _first_head((t, LANES))
                qf = q_ref[sl, :] * (scale * LOG2E)
                dof = do_ref[sl, :]
                prod = dof * o_ref[sl, :]
                rv = rows_ref[0, sl, :]
                for a, keep in enumerate((first, jnp.logical_not(first))):
                    q_s[a, sl, 0:LANES] = jnp.where(keep, qf, 0.0).astype(BF16)
                    q_s[a, sl, LANES:2 * LANES] = _bias_lanes(rv[:, a:a + 1], True)
                    do_s[a, sl, :] = jnp.where(keep, dof, 0.0).astype(BF16)
                    dl_s[sl, a:a + 1] = jnp.sum(jnp.where(keep, prod, 0.0), axis=1, keepdims=True)
                return c

            lax.fori_loop(0, nb, prep, 0)

        dk_acc[...] = jnp.zeros_like(dk_acc)
        dv_acc[...] = jnp.zeros_like(dv_acc)
        dc_acc[...] = jnp.zeros_like(dc_acc)
        first = _first_head((t, LANES))
        kf = k_ref[...]
        kk = kf.astype(BF16)
        k_own = (jnp.where(first, kf, 0.0).astype(BF16), jnp.where(first, 0.0, kf).astype(BF16))
        ckv = rows_ref[0, pl.ds(pl.multiple_of(kb * t, t), t), :]
        kx = [jnp.concatenate([kk, _bias_lanes(ckv[:, a:a + 1], False)], axis=1) for a in range(2)]
        vv = v_ref[...].astype(BF16)

        def block(qb, masked):
            sl = pl.ds(pl.multiple_of(qb * t, t), t)
            rv = rows_ref[0, sl, :]
            dlv = dl_s[sl, :]
            dq_new = dq_acc[sl, :]
            for a in range(2):
                dob = do_s[a, sl, :]
                s = lax.dot_general(q_s[a, sl, :], kx[a], (((1,), (1,)), ((), ())), preferred_element_type=F32)
                if masked:
                    s = jnp.where(_causal(t), s, NEG_INF)
                p = jnp.exp2(s - rv[:, 2 + a:3 + a])
                dv_acc[...] += lax.dot_general(p.astype(BF16), dob, (((0,), (0,)), ((), ())),
                                               preferred_element_type=F32)
                dp = lax.dot_general(dob, vv, (((1,), (1,)), ((), ())), preferred_element_type=F32)
                ds = p * (dp - dlv[:, a:a + 1])
                dsb = ds.astype(BF16)
                dk_acc[...] += lax.dot_general(dsb, q_s[a, sl, 0:LANES], (((0,), (0,)), ((), ())),
                                               preferred_element_type=F32)
                dq_new = dq_new + jnp.dot(dsb, k_own[a], preferred_element_type=F32) * scale
                dcq_ref[0, sl, a:a + 1] += jnp.sum(ds, axis=1, keepdims=True)
                dc_acc[a:a + 1, :] -= jnp.sum(ds, axis=0, keepdims=True)
            dq_acc[sl, :] = dq_new

        block(kb, True)

        def rest(qb, c):
            block(qb, False)
            return c

        lax.fori_loop(kb + 1, nb, rest, 0)
        dk_ref[...] = (dk_acc[...] * (1.0 / LOG2E)).astype(BF16)
        dv_ref[...] = dv_acc[...].astype(BF16)
        dck_ref[0, 0] = dc_acc[0:1, :]
        dck_ref[1, 0] = dc_acc[1:2, :]

        @pl.when(kb == nb - 1)
        def _():
            dq_ref[...] = dq_acc[...].astype(BF16)

    full = lambda c0: pl.BlockSpec((S, LANES), lambda h, j, c0=c0: (0, c0 + h))
    blk = lambda c0: pl.BlockSpec((t, LANES), lambda h, j, c0=c0: (j, c0 + h))
    f32 = lambda *s: jax.ShapeDtypeStruct(s, F32)
    b16 = jax.ShapeDtypeStruct((S, D), BF16)
    outs, extra = _call(
        body, name="fox_attn_bwd",
        out_shape=(b16, b16, b16, f32(HP, S, 2), f32(2 * HP, nb, 1, t)),
        grid=(HP, nb),
        in_specs=[full(0), blk(HP), blk(2 * HP), pl.BlockSpec((1, S, 4), lambda h, j: (h, 0, 0)), full(0), full(0)],
        out_specs=(full(0), blk(0), blk(0), pl.BlockSpec((1, S, 2), lambda h, j: (h, 0, 0)),
                   pl.BlockSpec((2, 1, 1, t), lambda h, j: (h, j, 0, 0))),
        scratch_shapes=[pltpu.VMEM((2, S, 2 * LANES), BF16), pltpu.VMEM((2, S, LANES), BF16),
                        pltpu.VMEM((S, 2), F32), pltpu.VMEM((S, LANES), F32), pltpu.VMEM((t, LANES), F32),
                        pltpu.VMEM((t, LANES), F32), pltpu.VMEM((2, t), F32)],
        args=(proj, proj, proj, rows, o, do), semantics=("parallel", "arbitrary"), ex=ex)
    return (*outs, extra)


def _s5_consts(T):
    rows = T * SUBLANES
    rr = lax.broadcasted_iota(jnp.int32, (rows, T), 0)
    tt = lax.broadcasted_iota(jnp.int32, (rows, T), 1)
    rep = jnp.where(rr // SUBLANES == tt, 1.0, 0.0).astype(BF16)
    r2 = lax.broadcasted_iota(jnp.int32, (rows, S5_PART), 0)
    c2 = lax.broadcasted_iota(jnp.int32, (rows, S5_PART), 1)
    mask = (c2 // (S5_PART // SUBLANES)) == (r2 % SUBLANES)
    return rep, mask


def _gelu(y):
    c = math.sqrt(2.0 / math.pi)
    return 0.5 * y * (1.0 + jnp.tanh(c * (y + 0.044715 * y * y * y)))


def _gelu_grad(y):
    c = math.sqrt(2.0 / math.pi)
    th = jnp.tanh(c * (y + 0.044715 * y * y * y))
    return 0.5 * (1.0 + th) + 0.5 * y * (1.0 - th * th) * c * (1.0 + 3.0 * 0.044715 * y * y)


def _s5_fwd(x, rmat, cmat, lam, dskip, ex=None):
    S, D = x.shape
    NQ = D // S5_PART
    T = _tile(S, 128, SUBLANES)
    rows = T * SUBLANES

    def body(x_ref, r_ref, c_ref, lam_ref, d_ref, y_ref, yg_ref, h_ref, bu_s, carry):
        i = pl.program_id(0)

        @pl.when(i == 0)
        def _():
            carry[...] = jnp.zeros_like(carry)

        rep, mask = _s5_consts(T)
        cols = [pl.ds(q * S5_PART, S5_PART) for q in range(NQ)]
        for q in range(NQ):
            xrep = jnp.dot(rep, x_ref[:, cols[q]].astype(BF16), preferred_element_type=F32)
            lx = jnp.where(mask, xrep, 0.0).astype(BF16)
            bu_s[q] = jnp.dot(lx, r_ref[q], preferred_element_type=F32)
        lam_v = [(lam_ref[q, :, 0:LANES], lam_ref[q, :, LANES:2 * LANES]) for q in range(NQ)]

        def step(t, c):
            o = pl.multiple_of(t * SUBLANES, SUBLANES)
            new = []
            for q in range(NQ):
                hr, hi = c[q]
                ar, ai = lam_v[q]
                sl = bu_s[q, pl.ds(o, SUBLANES), :]
                nhr = ar * hr - ai * hi + sl[:, 0:LANES]
                nhi = ar * hi + ai * hr + sl[:, LANES:2 * LANES]
                h_ref[q, pl.ds(o, SUBLANES), 0:LANES] = nhr
                h_ref[q, pl.ds(o, SUBLANES), LANES:2 * LANES] = nhi
                new.append((nhr, nhi))
            return tuple(new)

        fin = lax.fori_loop(0, T, step,
                            tuple((carry[q, :, 0:LANES], carry[q, :, LANES:2 * LANES]) for q in range(NQ)))
        for q in range(NQ):
            carry[q, :, 0:LANES] = fin[q][0]
            carry[q, :, LANES:2 * LANES] = fin[q][1]
            z = jnp.dot(h_ref[q].astype(BF16), c_ref[q], preferred_element_type=F32)
            z = jnp.where(mask, z, 0.0)
            y = jnp.sum(z.reshape(T, SUBLANES, S5_PART), axis=1) + d_ref[:, cols[q]] * x_ref[:, cols[q]]
            y_ref[:, cols[q]] = y
            yg_ref[:, cols[q]] = _gelu(y).astype(BF16)

    xs = pl.BlockSpec((T, D), lambda i: (i, 0))
    ms = pl.BlockSpec((NQ, S5_PART, S5_PART), lambda i: (0, 0, 0))
    outs, extra = _call(
        body, name="s5_scan_fwd",
        out_shape=(jax.ShapeDtypeStruct((S, D), F32), jax.ShapeDtypeStruct((S, D), BF16),
                   jax.ShapeDtypeStruct((NQ, S * SUBLANES, S5_PART), F32)),
        grid=(S // T,),
        in_specs=[xs, ms, ms, pl.BlockSpec((NQ, SUBLANES, S5_PART), lambda i: (0, 0, 0)),
                  pl.BlockSpec((1, D), lambda i: (0, 0))],
        out_specs=(xs, xs, pl.BlockSpec((NQ, rows, S5_PART), lambda i: (0, i, 0))),
        scratch_shapes=[pltpu.VMEM((NQ, rows, S5_PART), F32), pltpu.VMEM((NQ, SUBLANES, S5_PART), F32)],
        args=(x, rmat, cmat, lam, dskip), semantics=("arbitrary",), ex=ex)
    return (*outs, extra)


def _s5_bwd(x, y, dyg, hs, rmat, cmat, lam, dskip, res, res_scale, ex=None):
    S, D = x.shape
    NQ = D // S5_PART
    T = _tile(S, 128, SUBLANES)
    nb = S // T
    rows = T * SUBLANES

    def body(x_ref, y_ref, dyg_ref, res_ref, h_ref, hp_ref, r_ref, c_ref, lam_ref, d_ref,
             dx_ref, dr_ref, dc_ref, dlam_ref, dd_ref, dh_s, g_s, hs_s, carry):
        i = pl.program_id(0)

        @pl.when(i == 0)
        def _():
            carry[...] = jnp.zeros_like(carry)
            dr_ref[...] = jnp.zeros_like(dr_ref)
            dc_ref[...] = jnp.zeros_like(dc_ref)
            dlam_ref[...] = jnp.zeros_like(dlam_ref)
            dd_ref[...] = jnp.zeros_like(dd_ref)

        rep, mask = _s5_consts(T)
        cols = [pl.ds(q * S5_PART, S5_PART) for q in range(NQ)]
        dys, ldys = [], []
        for q in range(NQ):
            dy = dyg_ref[:, cols[q]] * _gelu_grad(y_ref[:, cols[q]])
            dyrep = jnp.dot(rep, dy.astype(BF16), preferred_element_type=F32)
            ldy = jnp.where(mask, dyrep, 0.0).astype(BF16)
            dh_s[q] = lax.dot_general(ldy, c_ref[q], (((1,), (1,)), ((), ())), preferred_element_type=F32)
            dys.append(dy)
            ldys.append(ldy)
        lam_v = [(lam_ref[q, :, 0:LANES], lam_ref[q, :, LANES:2 * LANES]) for q in range(NQ)]

        def step(n, c):
            o = pl.multiple_of((T - 1 - n) * SUBLANES, SUBLANES)
            new = []
            for q in range(NQ):
                gr, gi = c[q]
                ar, ai = lam_v[q]
                sl = dh_s[q, pl.ds(o, SUBLANES), :]
                ngr = sl[:, 0:LANES] + ar * gr + ai * gi
                ngi = sl[:, LANES:2 * LANES] - ai * gr + ar * gi
                g_s[q, pl.ds(o, SUBLANES), 0:LANES] = ngr
                g_s[q, pl.ds(o, SUBLANES), LANES:2 * LANES] = ngi
                new.append((ngr, ngi))
            return tuple(new)

        fin = lax.fori_loop(0, T, step,
                            tuple((carry[q, :, 0:LANES], carry[q, :, LANES:2 * LANES]) for q in range(NQ)))
        for q in range(NQ):
            carry[q, :, 0:LANES] = fin[q][0]
            carry[q, :, LANES:2 * LANES] = fin[q][1]
            xv = x_ref[:, cols[q]]
            hv = h_ref[q]
            hs_s[0:SUBLANES, :] = jnp.where(i == nb - 1, 0.0, hp_ref[q])
            hs_s[SUBLANES:rows + SUBLANES, :] = hv
            hprev = hs_s[0:rows, :]
            gv = g_s[q]
            g_re, g_im = gv[:, 0:LANES], gv[:, LANES:2 * LANES]
            hp_re, hp_im = hprev[:, 0:LANES], hprev[:, LANES:2 * LANES]
            dar = jnp.sum((g_re * hp_re + g_im * hp_im).reshape(T, SUBLANES, LANES), axis=0)
            dai = jnp.sum((g_im * hp_re - g_re * hp_im).reshape(T, SUBLANES, LANES), axis=0)
            dlam_ref[q, :, 0:LANES] += dar
            dlam_ref[q, :, LANES:2 * LANES] += dai

            gb = gv.astype(BF16)
            xrep = jnp.dot(rep, xv.astype(BF16), preferred_element_type=F32)
            lx = jnp.where(mask, xrep, 0.0).astype(BF16)
            dr_ref[q] += lax.dot_general(lx, gb, (((0,), (0,)), ((), ())), preferred_element_type=F32)
            dc_ref[q] += lax.dot_general(hv.astype(BF16), ldys[q], (((0,), (0,)), ((), ())),
                                         preferred_element_type=F32)
            zx = lax.dot_general(gb, r_ref[q], (((1,), (1,)), ((), ())), preferred_element_type=F32)
            zx = jnp.where(mask, zx, 0.0)
            dx_ref[:, cols[q]] = (jnp.sum(zx.reshape(T, SUBLANES, S5_PART), axis=1) + d_ref[:, cols[q]] * dys[q]
                                  + res_scale * res_ref[:, cols[q]])
            dd_ref[:, cols[q]] += jnp.sum(dys[q] * xv, axis=0, keepdims=True)

    xs = pl.BlockSpec((T, D), lambda i: (nb - 1 - i, 0))
    ms = pl.BlockSpec((NQ, S5_PART, S5_PART), lambda i: (0, 0, 0))
    ls = pl.BlockSpec((NQ, SUBLANES, S5_PART), lambda i: (0, 0, 0))
    ds_ = pl.BlockSpec((1, D), lambda i: (0, 0))
    outs, extra = _call(
        body, name="s5_scan_bwd",
        out_shape=(jax.ShapeDtypeStruct((S, D), F32), jax.ShapeDtypeStruct((NQ, S5_PART, S5_PART), F32),
                   jax.ShapeDtypeStruct((NQ, S5_PART, S5_PART), F32),
                   jax.ShapeDtypeStruct((NQ, SUBLANES, S5_PART), F32), jax.ShapeDtypeStruct((1, D), F32)),
        grid=(nb,),
        in_specs=[xs, xs, xs, xs, pl.BlockSpec((NQ, rows, S5_PART), lambda i: (0, nb - 1 - i, 0)),
                  pl.BlockSpec((NQ, SUBLANES, S5_PART), lambda i: (0, jnp.maximum((nb - 1 - i) * T - 1, 0), 0)),
                  ms, ms, ls, ds_],
        out_specs=(xs, ms, ms, ls, ds_),
        scratch_shapes=[pltpu.VMEM((NQ, rows, S5_PART), F32), pltpu.VMEM((NQ, rows, S5_PART), F32),
                        pltpu.VMEM((rows + SUBLANES, S5_PART), F32), pltpu.VMEM((NQ, SUBLANES, S5_PART), F32)],
        args=(x, y, dyg, res, hs, hs, rmat, cmat, lam, dskip), semantics=("arbitrary",), ex=ex)
    return (*outs, extra)


def _s5_discretise(a_re, a_im, log_dt, b_re, b_im):
    dt = jnp.exp(log_dt)[:, None]
    mag = jnp.exp(a_re * dt)
    ang = a_im * dt
    lb_re = mag * jnp.cos(ang)
    lb_im = mag * jnp.sin(ang)
    den = a_re * a_re + a_im * a_im
    nr = lb_re - 1.0
    ni = lb_im
    z_re = (nr * a_re + ni * a_im) / den
    z_im = (ni * a_re - nr * a_im) / den
    bb_re = z_re[..., None] * b_re - z_im[..., None] * b_im
    bb_im = z_re[..., None] * b_im + z_im[..., None] * b_re
    return lb_re, lb_im, bb_re, bb_im


def _s5_expand(w):
    G = w.shape[0]
    NQ = G // 16
    base = w.reshape(NQ, S5_PART, S5_STATE)
    half = (jnp.arange(S5_PART) // S5_GROUP) % 2
    sel = (half[:, None] == jnp.arange(2)[None, :]).astype(w.dtype)
    out = base[:, :, None, :] * sel[None, :, :, None]
    return out.reshape(NQ, S5_PART, 2 * S5_STATE)


def _s5_extract(m):
    NQ = m.shape[0]
    half = (jnp.arange(S5_PART) // S5_GROUP) % 2
    sel = (half[:, None] == jnp.arange(2)[None, :]).astype(m.dtype)
    base = jnp.sum(m.reshape(NQ, S5_PART, 2, S5_STATE) * sel[None, :, :, None], axis=2)
    return base.reshape(NQ * 16, S5_GROUP, S5_STATE)


def _s5_slab(v):
    return v.reshape(v.shape[0] // 16, SUBLANES, LANES)


def _adamw(w, g, m, v, name):
    R, C = w.shape
    tr = _tile(R, 256, SUBLANES)
    c1 = 1.0 / (1.0 - ADAM_B1 ** ADAM_STEP)
    c2 = 1.0 / (1.0 - ADAM_B2 ** ADAM_STEP)

    def body(w_ref, g_ref, m_ref, v_ref, d_ref, nm_ref, nv_ref):
        gv = g_ref[...]
        nm = ADAM_B1 * m_ref[...] + (1.0 - ADAM_B1) * gv
        nv = ADAM_B2 * v_ref[...] + (1.0 - ADAM_B2) * (gv * gv)
        nm_ref[...] = nm
        nv_ref[...] = nv
        d_ref[...] = -ADAM_LR * ((nm * c1) / (jnp.sqrt(nv * c2) + ADAM_EPS) + ADAM_WD * w_ref[...])

    blk = pl.BlockSpec((tr, C), lambda i: (i, 0))
    sh = jax.ShapeDtypeStruct((R, C), F32)
    return pl.pallas_call(
        body, name=name, out_shape=(sh, sh, sh), grid=(R // tr,), in_specs=[blk] * 4, out_specs=(blk,) * 3,
        compiler_params=_params(("parallel",)),
    )(w, g, m, v)


def _adamw_recv(parts, w, m, v, name):
    L, R, C = w.shape
    tr = _tile(R, 256, SUBLANES)
    c1 = 1.0 / (1.0 - ADAM_B1 ** ADAM_STEP)
    c2 = 1.0 / (1.0 - ADAM_B2 ** ADAM_STEP)

    def body(*refs):
        p_refs = refs[:L]
        w_ref, m_ref, v_ref, g_ref, d_ref, nm_ref, nv_ref = refs[L:]
        li = pl.program_id(0)
        for l in range(L):
            @pl.when(li == l)
            def _(p_ref=p_refs[l]):
                gv = p_ref[0].astype(F32)
                for k in range(1, N_DEV):
                    gv = gv + p_ref[k].astype(F32)
                g_ref[...] = gv
                nm = ADAM_B1 * m_ref[...] + (1.0 - ADAM_B1) * gv
                nv = ADAM_B2 * v_ref[...] + (1.0 - ADAM_B2) * (gv * gv)
                nm_ref[...] = nm
                nv_ref[...] = nv
                d_ref[...] = -ADAM_LR * ((nm * c1) / (jnp.sqrt(nv * c2) + ADAM_EPS) + ADAM_WD * w_ref[...])

    p_specs = [pl.BlockSpec((N_DEV, tr, C), lambda li, i, l=l: (0, jnp.where(li == l, i, 0), 0)) for l in range(L)]
    blk = pl.BlockSpec((None, tr, C), lambda li, i: (li, i, 0))
    sh = jax.ShapeDtypeStruct((L, R, C), F32)
    return pl.pallas_call(
        body, name=name, out_shape=(sh, sh, sh, sh), grid=(L, R // tr),
        in_specs=p_specs + [blk, blk, blk], out_specs=(blk,) * 4,
        compiler_params=_params(("parallel", "parallel")),
    )(*parts, w, m, v)


def _sum8(parts, name):
    _, R, C = parts.shape
    tr = _tile(R, 256, SUBLANES)

    def body(p_ref, o_ref):
        acc = p_ref[0].astype(F32)
        for k in range(1, N_DEV):
            acc = acc + p_ref[k].astype(F32)
        o_ref[...] = acc

    return pl.pallas_call(
        body, name=name, out_shape=jax.ShapeDtypeStruct((R, C), F32), grid=(R // tr,),
        in_specs=[pl.BlockSpec((N_DEV, tr, C), lambda i: (0, i, 0))],
        out_specs=pl.BlockSpec((tr, C), lambda i: (i, 0)), compiler_params=_params(("parallel",)),
    )(parts)


def _peers():
    x, y, c = lax.axis_index("x"), lax.axis_index("y"), lax.axis_index("c")
    me = 4 * x + 2 * y + c
    out = []
    for k in range(1, N_DEV):
        kx, ky, kc = (k >> 2) & 1, (k >> 1) & 1, k & 1
        px, py, pc = x ^ kx, y ^ ky, c ^ kc
        out.append(((px, py, pc), 4 * px + 2 * py + pc))
    return me, out


SIBLING = 1
SAME_CORE = (2, 4, 6)


class Exchange:
    def __init__(self, xs, scatter):
        self.xs = list(xs)
        self.scatter = list(scatter)
        self.n = len(self.xs)

    def out_shapes(self):
        return tuple(jax.ShapeDtypeStruct(x.shape if sc else (N_DEV,) + x.shape, x.dtype)
                     for x, sc in zip(self.xs, self.scatter))

    def sems(self):
        return [pltpu.SemaphoreType.DMA((self.n * N_DEV,)), pltpu.SemaphoreType.DMA((self.n * N_DEV,)),
                pltpu.SemaphoreType.DMA((self.n,))]

    def _copy(self, i, m, src, dst, to, send_sems, recv_sems):
        return pltpu.make_async_remote_copy(src_ref=src, dst_ref=dst, send_sem=send_sems.at[i * N_DEV + m],
                                            recv_sem=recv_sems.at[i * N_DEV + m], device_id=to, device_id_type=MESH)

    def start(self, x_refs, o_refs, send_sems, recv_sems, local_sems):
        me, peers = _peers()
        for i, (x_ref, o_ref, sc) in enumerate(zip(x_refs, o_refs, self.scatter)):
            pltpu.make_async_copy(x_ref.at[me] if sc else x_ref, o_ref.at[me], local_sems.at[i]).start()
            for m in (range(1, N_DEV) if sc else (SIBLING,) + SAME_CORE):
                dev, idx = peers[m - 1]
                self._copy(i, m, x_ref.at[idx] if sc else x_ref, o_ref.at[me], dev, send_sems, recv_sems).start()

    def middle(self, x_refs, o_refs, send_sems, recv_sems, local_sems):
        me, peers = _peers()
        sibling = peers[SIBLING - 1][0]
        for i, (x_ref, o_ref, sc) in enumerate(zip(x_refs, o_refs, self.scatter)):
            if sc:
                continue
            for m in SAME_CORE:
                dev, idx = peers[m - 1]
                self._copy(i, m, x_ref, o_ref.at[idx], dev, send_sems, recv_sems).wait_recv()
                self._copy(i, m ^ 1, o_ref.at[idx], o_ref.at[idx], sibling, send_sems, recv_sems).start()

    def wait(self, x_refs, o_refs, send_sems, recv_sems, local_sems):
        me, peers = _peers()
        for i, (x_ref, o_ref, sc) in enumerate(zip(x_refs, o_refs, self.scatter)):
            for m in range(1, N_DEV):
                dev, idx = peers[m - 1]
                cp = self._copy(i, m, x_ref.at[idx] if sc else x_ref, o_ref.at[idx], dev, send_sems, recv_sems)
                if sc or m not in SAME_CORE:
                    cp.wait_recv()
                cp.wait_send()
            pltpu.make_async_copy(x_ref.at[me] if sc else x_ref, o_ref.at[me], local_sems.at[i]).wait()


def _exchange(ex, name):
    n = ex.n

    def body(*refs):
        x_refs, o_refs, sems = refs[:n], refs[n:2 * n], refs[2 * n:]
        ex.start(x_refs, o_refs, *sems)
        ex.middle(x_refs, o_refs, *sems)
        ex.wait(x_refs, o_refs, *sems)

    hbm = pl.BlockSpec(memory_space=pltpu.HBM)
    return pl.pallas_call(body, name=name, out_shape=ex.out_shapes(), in_specs=[hbm] * n, out_specs=(hbm,) * n,
                          scratch_shapes=ex.sems())(*ex.xs)


def _call(body, *, name, out_shape, grid, in_specs, out_specs, scratch_shapes, args, semantics, ex=None):
    if ex is None:
        return pl.pallas_call(body, name=name, out_shape=out_shape, grid=grid, in_specs=in_specs, out_specs=out_specs,
                              scratch_shapes=scratch_shapes, compiler_params=_params(semantics))(*args), ()
    n, ni, no, ns = ex.n, len(args), len(out_shape), len(scratch_shapes)

    def wrapped(*refs):
        ins, cx = refs[:ni], refs[ni:ni + n]
        outs, co = refs[ni + n:ni + n + no], refs[ni + n + no:ni + 2 * n + no]
        scratch, sems = refs[ni + 2 * n + no:ni + 2 * n + no + ns], refs[ni + 2 * n + no + ns:]
        step = functools.reduce(lambda acc, a: acc * grid[a] + pl.program_id(a), range(len(grid)), 0)
        steps = math.prod(grid)

        @pl.when(step == 0)
        def _():
            ex.start(cx, co, *sems)

        body(*ins, *outs, *scratch)

        @pl.when(step == (steps * 3) // 4 - (steps > 1))
        def _():
            ex.middle(cx, co, *sems)

        @pl.when(step == steps - 1)
        def _():
            ex.wait(cx, co, *sems)

    hbm = pl.BlockSpec(memory_space=pltpu.HBM)
    res = pl.pallas_call(
        wrapped, name=name, out_shape=tuple(out_shape) + ex.out_shapes(), grid=grid,
        in_specs=list(in_specs) + [hbm] * n, out_specs=tuple(out_specs) + (hbm,) * n,
        scratch_shapes=list(scratch_shapes) + ex.sems(),
        compiler_params=_params(("arbitrary",) * len(grid)))(*args, *ex.xs)
    return res[:no], res[no:]


def _pack_flat(arrs):
    cat = jnp.concatenate([a.reshape(-1) for a in arrs])
    per = PACK_COLS * 2 * SUBLANES
    tot = -(-cat.shape[0] // per) * per
    return jnp.pad(cat, (0, tot - cat.shape[0])).reshape(tot // PACK_COLS, PACK_COLS)


def _unpack_flat(packed, shapes):
    flat = packed.reshape(-1)
    out, o = [], 0
    for s in shapes:
        n = math.prod(s)
        out.append(flat[o:o + n].reshape(s))
        o += n
    return out


def _ffn_fwd(x, xb, wt, wo4, g, b, alpha, tag, ex=None):
    h8, a4, extra = _ffn_h(xb, wt, tag + "_h", ex)
    out, outb, xh, rstd = _ffn_y_ln(a4, wo4, x, g, b, alpha, 0.5, tag + "_y_ln")
    return out, outb, (xb, h8, a4, xh, rstd), extra


def _ffn_bwd(dout, saved, wt, wo4, g, alpha, tag, carry=None, own_key=None):
    x, h8, a4, xh, rstd = saved
    load = dict(carry or {})
    slots = [[], [], [], []]
    for n, k in enumerate(sorted(load, key=lambda k: -load[k].size)):
        slots[min(n, 3)].append(k)
    arrived = {}

    def run(fn, slot, *args):
        keys = slots[slot]
        if not keys:
            return fn(*args)
        res, extra = fn(*args, ex=Exchange([load[k] for k in keys], [True] * len(keys)))
        arrived.update(zip(keys, extra))
        return res

    dz, dg, db = _ln_bwd(dout, xh, rstd, g, tag + "_ln_bwd")
    dw_out = run(_ffn_dwout, 3, a4, dz, 0.5, tag + "_dwout")
    dw_out = dw_out.reshape(N_DEV, -1, dw_out.shape[-1])
    if own_key is not None:
        load[own_key] = dw_out
        slots[1].append(own_key)
    dh8 = run(_ffn_dh, 2, dz, wo4, h8, 0.5, tag + "_dh")
    dw_in = run(_ffn_dwin, 0, x, dh8, tag + "_dwin")
    dx = run(_ffn_dx, 1, dh8, wt, dz, alpha, tag + "_dx")
    return dx, dw_in, dw_out, dg, db, arrived


def _fox_fwd(x, xb, w_in_pad, b_f_pad, w_o, g, b, alpha, tag, ex=None):
    S, D = x.shape
    H = D // HEAD_DIM
    proj = _mm(xb, w_in_pad, name=tag + "_proj")
    fl = proj[:, 3 * D:]
    cum = _fox_gate_fwd(fl, b_f_pad)
    cum2 = (cum[:, :H].T * LOG2E).reshape(H // 2, 2, S).transpose(0, 2, 1)
    o, lse, extra = _flash_fwd(proj, cum2, ex)
    m = _mm(o, w_o, name=tag + "_out")
    out, outb, xh, rstd = _res_ln_fwd(x, m, g, b, alpha, 1.0, tag + "_ln")
    return out, outb, (xb, proj, jnp.concatenate([cum2, lse], axis=2), o, fl, xh, rstd), extra


def _fox_bwd(dout, saved, w_in_pad, b_f_pad, w_o, g, alpha, tag, ex=None):
    x, proj, rows, o, fl, xh, rstd = saved
    S, D = x.shape
    H = D // HEAD_DIM
    dz, dg, db = _ln_bwd(dout, xh, rstd, g, tag + "_ln_bwd")
    dw_o = _mm(o, dz, ta=True, name=tag + "_dwo")
    do = _mm(dz, w_o, tb=True, name=tag + "_do")
    dq, dk, dv, dcq, dck, extra = _flash_bwd(proj, rows, o, do, ex)
    dcq = dcq.transpose(0, 2, 1).reshape(H, S)
    dcum = jnp.pad((dcq + dck.reshape(H, S)).T, ((0, 0), (0, LANES - H)))
    dfl, dbf = _fox_gate_bwd(dcum, fl, b_f_pad)
    dproj = jnp.concatenate([dq, dk, dv, dfl.astype(BF16)], axis=1)
    dw_in = _mm(x, dproj, ta=True, name=tag + "_dwin")
    dx = _mm(dproj, w_in_pad, tb=True, add=dz, add_scale=alpha, name=tag + "_dx")
    shards = {"fox_w_in": _split(dw_in[None, :, :3 * D + H], True)[:, 0].astype(BF16),
              "fox_w_o": _split(dw_o[None], False)[:, 0].astype(BF16)}
    return dx, shards, {"fox_b_f": dbf[0, :H], "lnm_g": dg, "lnm_b": db}, extra


def _s5_mats(p):
    lb_re, lb_im, bb_re, bb_im = _s5_discretise(p["a_re"], p["a_im"], p["log_dt"], p["b_re"], p["b_im"])
    rmat = jnp.concatenate([_s5_expand(bb_re.transpose(0, 2, 1)), _s5_expand(bb_im.transpose(0, 2, 1))], axis=2)
    cmat = jnp.concatenate([_s5_expand(p["c_re"]).transpose(0, 2, 1), -_s5_expand(p["c_im"]).transpose(0, 2, 1)],
                           axis=1)
    lam = jnp.concatenate([_s5_slab(lb_re), _s5_slab(lb_im)], axis=2)
    return rmat.astype(BF16), cmat.astype(BF16), lam


def _s5_block_fwd(x, p, w_out, g, b, alpha, tag, ex=None):
    S, D = x.shape
    rmat, cmat, lam = _s5_mats(p)
    dskip = p["d"].reshape(1, D)
    y, yg, hs, extra = _s5_fwd(x, rmat, cmat, lam, dskip, ex)
    vg = _mm(yg, w_out, name=tag + "_vg")
    m = _glu_fwd(vg, tag + "_glu")
    out, outb, xh, rstd = _res_ln_fwd(x, m, g, b, alpha, 1.0, tag + "_ln")
    return out, outb, (x, y, yg, hs, vg, rmat, cmat, lam, dskip, xh, rstd), extra


def _s5_block_bwd(dout, saved, p, w_out, g, alpha, tag, ex=None):
    x, y, yg, hs, vg, rmat, cmat, lam, dskip, xh, rstd = saved
    S, D = x.shape
    G = D // S5_GROUP
    dz, dg, db = _ln_bwd(dout, xh, rstd, g, tag + "_ln_bwd")
    dvg = _glu_bwd(vg, dz, tag + "_glu_bwd")
    dw_out = _mm(yg, dvg, ta=True, name=tag + "_dwout")
    dyg = _mm(dvg, w_out, tb=True, name=tag + "_dyg")
    dx, dr, dc, dlam, dd, extra = _s5_bwd(x, y, dyg, hs, rmat, cmat, lam, dskip, dz, alpha, ex)
    dbb_re = _s5_extract(dr[:, :, :LANES]).transpose(0, 2, 1)
    dbb_im = _s5_extract(dr[:, :, LANES:]).transpose(0, 2, 1)
    dc_re = _s5_extract(dc[:, :LANES, :].transpose(0, 2, 1))
    dc_im = -_s5_extract(dc[:, LANES:, :].transpose(0, 2, 1))
    dlb_re = dlam[:, :, :LANES].reshape(G, S5_STATE)
    dlb_im = dlam[:, :, LANES:].reshape(G, S5_STATE)
    _, vjp = jax.vjp(_s5_discretise, p["a_re"], p["a_im"], p["log_dt"], p["b_re"], p["b_im"])
    da_re, da_im, dlog_dt, db_re, db_im = vjp((dlb_re, dlb_im, dbb_re, dbb_im))
    small = dict(s5_a_re=da_re, s5_a_im=da_im, s5_log_dt=dlog_dt, s5_b_re=db_re, s5_b_im=db_im, s5_c_re=dc_re,
                 s5_c_im=dc_im, s5_d=dd.reshape(G, S5_GROUP), lnm_g=dg, lnm_b=db)
    return dx, {"s5_w_out": _split(dw_out[None], True)[:, 0].astype(BF16)}, small, extra


FFN_NAMES = ("ffn1_w_in", "ffn1_w_out", "ffn2_w_in", "ffn2_w_out")
BIG = FFN_NAMES + ("fox_w_in", "fox_w_o", "s5_w_out")
BIG_SPLIT_COLS = {"ffn1_w_in": True, "ffn1_w_out": False, "ffn2_w_in": True, "ffn2_w_out": False,
                  "fox_w_in": True, "fox_w_o": False, "s5_w_out": True}
SMALL = ("ln1_g", "ln1_b", "lnm_g", "lnm_b", "ln2_g", "ln2_b", "fox_b_f", "s5_a_re", "s5_a_im", "s5_log_dt",
         "s5_b_re", "s5_b_im", "s5_c_re", "s5_c_im", "s5_d")
WEIGHTS = ("ffn1_w_in", "ffn1_w_out", "ln1_g", "ln1_b", "lnm_g", "lnm_b", "ffn2_w_in", "ffn2_w_out", "ln2_g", "ln2_b",
           "fox_w_in", "fox_b_f", "fox_w_o", "s5_a_re", "s5_a_im", "s5_log_dt", "s5_b_re", "s5_b_im", "s5_c_re",
           "s5_c_im", "s5_d", "s5_w_out")


def _join(gathered, split_cols):
    n, L, r, c = gathered.shape
    if split_cols:
        return gathered.transpose(1, 2, 0, 3).reshape(L, r, n * c)
    return gathered.transpose(1, 0, 2, 3).reshape(L, n * r, c)


def _split(full, split_cols):
    L, R, C = full.shape
    if split_cols:
        return full.reshape(L, R, N_DEV, C // N_DEV).transpose(2, 0, 1, 3)
    return full.reshape(L, N_DEV, R // N_DEV, C).transpose(1, 0, 2, 3)


def _group(i, part):
    if part == "mixer":
        return (("fox_w_in", i // 2), ("fox_w_o", i // 2)) if i % 2 == 0 else (("s5_w_out", i // 2),)
    return ((part + "_w_in", i), (part + "_w_out", i))


def _prepare(name, g8):
    if name in ("ffn1_w_in", "ffn2_w_in"):
        return g8
    if name in FFN_NAMES:
        n, r, c = g8.shape
        return g8.reshape(n // 2, 2 * r, c)
    full = _join(g8[:, None], BIG_SPLIT_COLS[name])[0]
    if name == "fox_w_in":
        full = jnp.pad(full, ((0, 0), (0, LANES - full.shape[0] // HEAD_DIM)))
    return full


def _local_step(x, target, small, shard_of=None, pregathered=None):
    S, D = x.shape
    H = D // HEAD_DIM
    depth = small["ln1_g"].shape[0]
    alpha = (2.0 * depth) ** 0.25
    local = pregathered is not None
    bf_pad = jnp.pad(small["fox_b_f"], ((0, 0), (0, LANES - H)))

    def s5_params(j):
        return {k: small["s5_" + k][j] for k in ("a_re", "a_im", "log_dt", "b_re", "b_im", "c_re", "c_im", "d")}

    if local:
        W = {k: _prepare(k[0], g8) for k, g8 in pregathered.items()}
    else:
        keys = _group(0, "ffn1")
        got = _exchange(Exchange([shard_of(*k) for k in keys], [False] * len(keys)), "gather_first")
        W = {k: _prepare(k[0], g8) for k, g8 in zip(keys, got)}

    def gather(keys):
        return None if local else Exchange([shard_of(*k) for k in keys], [False] * len(keys))

    def landed(keys, extra):
        if not local:
            W.update({k: _prepare(k[0], g8) for k, g8 in zip(keys, extra)})

    saved = []
    h, hb = x, x.astype(BF16)
    for i in range(depth):
        j = i // 2
        keys = _group(i, "mixer")
        h, hb, s1, extra = _ffn_fwd(h, hb, W[("ffn1_w_in", i)], W[("ffn1_w_out", i)], small["ln1_g"][i],
                                    small["ln1_b"][i], alpha, f"l{i}_ffn1", gather(keys))
        landed(keys, extra)
        keys = _group(i, "ffn2") + (_group(i + 1, "ffn1") if i + 1 < depth else ())
        ex = gather(keys)
        if i % 2 == 0:
            h, hb, s2, extra = _fox_fwd(h, hb, W[("fox_w_in", j)], bf_pad[j:j + 1], W[("fox_w_o", j)],
                                        small["lnm_g"][i], small["lnm_b"][i], alpha, f"l{i}_fox", ex)
        else:
            h, hb, s2, extra = _s5_block_fwd(h, s5_params(j), W[("s5_w_out", j)], small["lnm_g"][i],
                                             small["lnm_b"][i], alpha, f"l{i}_s5", ex)
        landed(keys, extra)
        h, hb, s3, _ = _ffn_fwd(h, hb, W[("ffn2_w_in", i)], W[("ffn2_w_out", i)], small["ln2_g"][i],
                                small["ln2_b"][i], alpha, f"l{i}_ffn2")
        saved.append((s1, s2, s3))

    dh, loss_part = _loss_fwd_bwd(h, target)

    arrived = {}
    pending = {}
    gs = {k: [None] * small[k].shape[0] for k in SMALL}
    for i in reversed(range(depth)):
        j = i // 2
        s1, s2, s3 = saved[i]
        dh, dw_in, dw_out, gs["ln2_g"][i], gs["ln2_b"][i], got = _ffn_bwd(
            dh, s3, W[("ffn2_w_in", i)], W[("ffn2_w_out", i)], small["ln2_g"][i], alpha, f"l{i}_ffn2",
            None if local else pending)
        arrived.update(pending if local else got)
        pending = {("ffn2_w_in", i): dw_in, ("ffn2_w_out", i): dw_out}
        keys = list(pending)
        ex = None if local else Exchange([pending[k] for k in keys], [True] * len(keys))
        if i % 2 == 0:
            dh, mix, sg, extra = _fox_bwd(dh, s2, W[("fox_w_in", j)], bf_pad[j:j + 1], W[("fox_w_o", j)],
                                          small["lnm_g"][i], alpha, f"l{i}_fox", ex)
        else:
            dh, mix, sg, extra = _s5_block_bwd(dh, s2, s5_params(j), W[("s5_w_out", j)], small["lnm_g"][i], alpha,
                                               f"l{i}_s5", ex)
        arrived.update(zip(keys, [pending[k] for k in keys] if local else extra))
        pending = {(k, j): val for k, val in mix.items()}
        for k, val in sg.items():
            gs[k][i if k in ("lnm_g", "lnm_b") else j] = val
        own = ("ffn1_w_out", i) if (i == 0 and not local) else None
        dh, dw_in, dw_out, gs["ln1_g"][i], gs["ln1_b"][i], got = _ffn_bwd(
            dh, s1, W[("ffn1_w_in", i)], W[("ffn1_w_out", i)], small["ln1_g"][i], alpha, f"l{i}_ffn1",
            None if local else pending, own)
        arrived.update(pending if local else got)
        pending = {("ffn1_w_in", i): dw_in}
        if own is None:
            pending[("ffn1_w_out", i)] = dw_out
    gs = {k: jnp.stack(v) for k, v in gs.items()}
    return loss_part, dh, arrived, pending, gs


def kernel(x, ffn1_w_in, ffn1_w_out, ln1_g, ln1_b, lnm_g, lnm_b, ffn2_w_in, ffn2_w_out, ln2_g, ln2_b, fox_w_in, fox_b_f, fox_w_o, s5_a_re, s5_a_im, s5_log_dt, s5_b_re, s5_b_im, s5_c_re, s5_c_im, s5_d, s5_w_out, loss_target, m_ffn1_w_in, m_ffn1_w_out, m_ln1_g, m_ln1_b, m_lnm_g, m_lnm_b, m_ffn2_w_in, m_ffn2_w_out, m_ln2_g, m_ln2_b, m_fox_w_in, m_fox_b_f, m_fox_w_o, m_s5_a_re, m_s5_a_im, m_s5_log_dt, m_s5_b_re, m_s5_b_im, m_s5_c_re, m_s5_c_im, m_s5_d, m_s5_w_out, v_ffn1_w_in, v_ffn1_w_out, v_ln1_g, v_ln1_b, v_lnm_g, v_lnm_b, v_ffn2_w_in, v_ffn2_w_out, v_ln2_g, v_ln2_b, v_fox_w_in, v_fox_b_f, v_fox_w_o, v_s5_a_re, v_s5_a_im, v_s5_log_dt, v_s5_b_re, v_s5_b_im, v_s5_c_re, v_s5_c_im, v_s5_d, v_s5_w_out):
    args = dict(locals())
    w = {k: args[k] for k in WEIGHTS}
    m = {k: args["m_" + k] for k in WEIGHTS}
    v = {k: args["v_" + k] for k in WEIGHTS}
    small = {k: w[k] for k in SMALL}

    turned = lambda d: {k: jnp.swapaxes(a, 1, 2) if k in ("ffn1_w_in", "ffn2_w_in") else a for k, a in d.items()}
    w, m, v = turned(w), turned(m), turned(v)
    wb = {k: w[k].astype(BF16) for k in BIG}
    loss_part, dx, arrived, last, gs = _local_step(x[0], loss_target[0], small, lambda k, idx: wb[k][idx])
    loss = lax.psum(loss_part, ("x", "y", "c"))

    small_shapes = [w[k].shape for k in SMALL]
    ex = Exchange(list(last.values()) + [_pack_flat([gs[k] for k in SMALL]).astype(BF16)],
                  [True] * len(last) + [False])
    *got, g_small_all = _exchange(ex, "scatter_last_gather_small")
    arrived.update(zip(last, got))
    g_small_flat = _sum8(g_small_all, "sum_small_grads")

    grads, delta, new_m, new_v = {}, {}, {}, {}
    for k in BIG:
        parts = [arrived[(k, l)] for l in range(w[k].shape[0])]
        grads[k], delta[k], new_m[k], new_v[k] = _adamw_recv(parts, w[k], m[k], v[k], "adamw_" + k)
    pk = lambda d: _pack_flat([d[k] for k in SMALL])
    d_, m_, v_ = _adamw(pk(w), g_small_flat, pk(m), pk(v), "adamw_small")
    for dst, flat in ((grads, g_small_flat), (delta, d_), (new_m, m_), (new_v, v_)):
        dst.update(zip(SMALL, _unpack_flat(flat, small_shapes)))
    grads, delta, new_m, new_v = turned(grads), turned(delta), turned(new_m), turned(new_v)

    return (loss, dx[None], *[grads[k] for k in WEIGHTS], *[delta[k] for k in WEIGHTS],
            *[new_m[k] for k in WEIGHTS], *[new_v[k] for k in WEIGHTS])
```

```python
import functools
import math

import jax
import jax.numpy as jnp
from jax import lax
from jax.experimental import pallas as pl
from jax.experimental.pallas import tpu as pltpu

F32 = jnp.float32
BF16 = jnp.bfloat16

N_DEV = 8
HEAD_DIM = 64
S5_GROUP = 16
S5_STATE = 64
LANES = 128
SUBLANES = 8
S5_PART = 256
ATTN_TILE = 512
LN_EPS = 1e-5
NEG_INF = -1e30
LOG2E = 1.4426950408889634
ADAM_LR, ADAM_B1, ADAM_B2, ADAM_EPS, ADAM_WD, ADAM_STEP = 0.001, 0.9, 0.999, 1e-08, 0.01, 10
VMEM_LIMIT = 48 * 1024 * 1024
PACK_COLS = 1024

MESH = pl.DeviceIdType.MESH


def _tile(dim, pref, align=LANES):
    if dim <= pref:
        return dim
    t = (pref // align) * align
    while t >= align:
        if dim % t == 0:
            return t
        t -= align
    return dim


def _params(sem):
    return pltpu.CompilerParams(dimension_semantics=sem, vmem_limit_bytes=VMEM_LIMIT)


def _mm(a, b, *, ta=False, tb=False, out_dtype=F32, scale=None, add=None, add_scale=1.0,
        tm=512, tn=1408, tk=1408, name="mm"):
    if ta:
        K, M = a.shape
    else:
        M, K = a.shape
    if tb:
        N, K2 = b.shape
    else:
        K2, N = b.shape
    assert K == K2, (a.shape, b.shape, ta, tb)
    tm, tn, tk = _tile(M, tm), _tile(N, tn), _tile(K, tk)
    a_spec = pl.BlockSpec((tk, tm), lambda i, j, k: (k, i)) if ta else pl.BlockSpec((tm, tk), lambda i, j, k: (i, k))
    b_spec = pl.BlockSpec((tn, tk), lambda i, j, k: (j, k)) if tb else pl.BlockSpec((tk, tn), lambda i, j, k: (k, j))
    o_spec = pl.BlockSpec((tm, tn), lambda i, j, k: (i, j))
    return _mm_core(name, a, b, a_spec, b_spec, o_spec, jax.ShapeDtypeStruct((M, N), out_dtype),
                    (M // tm, N // tn, K // tk), (tm, tn), ta, tb, scale, add, add_scale)


def _mm_core(name, a, b, a_spec, b_spec, o_spec, out_shape, grid, acc_shape, ta, tb, scale=None, add=None,
             add_scale=1.0, ex=None):
    nk = grid[2]

    def body(*refs):
        if add is None:
            a_ref, b_ref, o_ref, acc = refs
            add_ref = None
        else:
            a_ref, b_ref, add_ref, o_ref, acc = refs
        k = pl.program_id(2)

        @pl.when(k == 0)
        def _():
            acc[...] = jnp.zeros_like(acc)

        dims = (((0 if ta else 1,), (1 if tb else 0,)), ((), ()))
        acc[...] += lax.dot_general(a_ref[...].astype(BF16), b_ref[...].astype(BF16), dims,
                                    preferred_element_type=F32)

        @pl.when(k == nk - 1)
        def _():
            r = acc[...]
            if scale is not None:
                r = r * scale
            if add_ref is not None:
                r = r + add_scale * add_ref[...]
            o_ref[...] = r.astype(out_shape.dtype)

    in_specs = [a_spec, b_spec]
    args = [a, b]
    if add is not None:
        in_specs.append(o_spec)
        args.append(add)
    (res,), extra = _call(body, name=name, out_shape=(out_shape,), grid=grid, in_specs=in_specs, out_specs=(o_spec,),
                          scratch_shapes=[pltpu.VMEM(acc_shape, F32)], args=args,
                          semantics=("parallel", "parallel", "arbitrary"), ex=ex)
    return res if ex is None else (res, extra)


def _ffn_h(x, wt, name, ex=None):
    S, D = x.shape
    n, c, _ = wt.shape
    nb = n // 2
    tm = _tile(S, 1024, SUBLANES)
    dims = (((1,), (1,)), ((), ()))

    def body(x_ref, wg_ref, wu_ref, h_ref, a_ref):
        xb = x_ref[...]
        g = lax.dot_general(xb, wg_ref[...], dims, preferred_element_type=F32).astype(BF16)
        u = lax.dot_general(xb, wu_ref[...], dims, preferred_element_type=F32).astype(BF16)
        h_ref[0] = g
        h_ref[1] = u
        g = g.astype(F32)
        a_ref[...] = (g * _sigmoid(g) * u.astype(F32)).astype(BF16)

    (h, a4), extra = _call(
        body, name=name,
        out_shape=(jax.ShapeDtypeStruct((2, nb, S, c), BF16), jax.ShapeDtypeStruct((nb, S, c), BF16)),
        grid=(S // tm, nb),
        in_specs=[pl.BlockSpec((tm, D), lambda i, j: (i, 0)), pl.BlockSpec((None, c, D), lambda i, j: (j, 0, 0)),
                  pl.BlockSpec((None, c, D), lambda i, j: (j + nb, 0, 0))],
        out_specs=(pl.BlockSpec((2, None, tm, c), lambda i, j: (0, j, i, 0)),
                   pl.BlockSpec((None, tm, c), lambda i, j: (j, i, 0))),
        scratch_shapes=[], args=(x, wt, wt), semantics=("parallel", "parallel"), ex=ex)
    return h.reshape(n, S, c), a4, extra


def _ffn_y_ln(a4, wo4, x, g, b, alpha, s, name):
    nb, S, c = a4.shape
    D = wo4.shape[-1]
    tm = _tile(S, 512, SUBLANES)

    def body(a_ref, w_ref, x_ref, g_ref, b_ref, o_ref, ob_ref, xh_ref, r_ref, acc):
        k = pl.program_id(1)

        @pl.when(k == 0)
        def _():
            acc[...] = jnp.zeros_like(acc)

        acc[...] += jnp.dot(a_ref[...], w_ref[...], preferred_element_type=F32)

        @pl.when(k == nb - 1)
        def _():
            z = alpha * x_ref[...] + s * acc[...]
            mu = jnp.mean(z, axis=-1, keepdims=True)
            zc = z - mu
            rstd = lax.rsqrt(jnp.mean(zc * zc, axis=-1, keepdims=True) + LN_EPS)
            xh = zc * rstd
            xh_ref[...] = xh
            r_ref[...] = rstd
            out = xh * g_ref[...] + b_ref[...]
            o_ref[...] = out
            ob_ref[...] = out.astype(BF16)

    row = pl.BlockSpec((tm, D), lambda i, k: (i, 0))
    vec = pl.BlockSpec((1, D), lambda i, k: (0, 0))
    return pl.pallas_call(
        body, name=name,
        out_shape=(jax.ShapeDtypeStruct((S, D), F32), jax.ShapeDtypeStruct((S, D), BF16),
                   jax.ShapeDtypeStruct((S, D), F32), jax.ShapeDtypeStruct((S, 1), F32)),
        grid=(S // tm, nb),
        in_specs=[pl.BlockSpec((None, tm, c), lambda i, k: (k, i, 0)), pl.BlockSpec((None, c, D), lambda i, k: (k, 0, 0)),
                  row, vec, vec],
        out_specs=(row, row, row, pl.BlockSpec((tm, 1), lambda i, k: (i, 0))),
        scratch_shapes=[pltpu.VMEM((tm, D), F32)], compiler_params=_params(("parallel", "arbitrary")),
    )(a4, wo4, x, g.reshape(1, D), b.reshape(1, D))


def _ffn_dwout(a4, dz, scale, name, ex=None):
    nb, S, c = a4.shape
    D = dz.shape[1]
    tk, tn = _tile(S, 1024, SUBLANES), _tile(D, 1024)
    return _mm_core(name, a4, dz, pl.BlockSpec((None, tk, c), lambda i, j, k: (i, k, 0)),
                    pl.BlockSpec((tk, tn), lambda i, j, k: (k, j)),
                    pl.BlockSpec((None, c, tn), lambda i, j, k: (i, 0, j)),
                    jax.ShapeDtypeStruct((nb, c, D), BF16), (nb, D // tn, S // tk), (c, tn), True, False, scale,
                    ex=ex)


def _ffn_dh(dz, wo4, h8, scale, name, ex=None):
    S, D = dz.shape
    nb, c = wo4.shape[0], wo4.shape[1]
    tm = _tile(S, 512, SUBLANES)

    def body(dz_ref, w_ref, h_ref, d_ref):
        da = lax.dot_general(dz_ref[...].astype(BF16), w_ref[...], (((1,), (1,)), ((), ())),
                             preferred_element_type=F32) * scale
        g = h_ref[0].astype(F32)
        u = h_ref[1].astype(F32)
        sg = _sigmoid(g)
        silu = g * sg
        d_ref[0] = (da * u * (sg + silu * (1.0 - sg))).astype(BF16)
        d_ref[1] = (da * silu).astype(BF16)

    pair = pl.BlockSpec((2, None, tm, c), lambda i, j: (0, j, i, 0))
    (dh,), extra = _call(
        body, name=name, out_shape=(jax.ShapeDtypeStruct((2, nb, S, c), BF16),), grid=(S // tm, nb),
        in_specs=[pl.BlockSpec((tm, D), lambda i, j: (i, 0)), pl.BlockSpec((None, c, D), lambda i, j: (j, 0, 0)),
                  pair],
        out_specs=(pair,), scratch_shapes=[], args=(dz, wo4, h8.reshape(2, nb, S, c)),
        semantics=("parallel", "parallel"), ex=ex)
    dh = dh.reshape(2 * nb, S, c)
    return dh if ex is None else (dh, extra)


def _ffn_dwin(x, dh8, name, ex=None):
    S, D = x.shape
    n, _, c = dh8.shape
    return _mm_core(name, dh8, x, pl.BlockSpec((None, S, c), lambda i, j, k: (i, 0, 0)),
                    pl.BlockSpec((S, D), lambda i, j, k: (0, 0)),
                    pl.BlockSpec((None, c, D), lambda i, j, k: (i, 0, 0)),
                    jax.ShapeDtypeStruct((n, c, D), BF16), (n, 1, 1), (c, D), True, False, ex=ex)


def _ffn_dx(dh8, wt, dz, alpha, name, ex=None):
    n, S, c = dh8.shape
    D = wt.shape[2]
    tm, tn = _tile(S, 1024, SUBLANES), _tile(D, 1024)
    return _mm_core(name, dh8, wt, pl.BlockSpec((None, tm, c), lambda i, j, k: (k, i, 0)),
                    pl.BlockSpec((None, c, tn), lambda i, j, k: (k, 0, j)),
                    pl.BlockSpec((tm, tn), lambda i, j, k: (i, j)),
                    jax.ShapeDtypeStruct((S, D), F32), (S // tm, D // tn, n), (tm, tn), False, False,
                    None, dz, alpha, ex=ex)


def _res_ln_fwd(x, y, g, b, alpha, s, name):
    S, D = x.shape
    tm = _tile(S, 256, SUBLANES)

    def body(x_ref, y_ref, g_ref, b_ref, o_ref, ob_ref, xh_ref, r_ref):
        z = alpha * x_ref[...] + s * y_ref[...]
        mu = jnp.mean(z, axis=-1, keepdims=True)
        zc = z - mu
        var = jnp.mean(zc * zc, axis=-1, keepdims=True)
        rstd = lax.rsqrt(var + LN_EPS)
        xh = zc * rstd
        xh_ref[...] = xh
        r_ref[...] = rstd
        out = xh * g_ref[...] + b_ref[...]
        o_ref[...] = out
        ob_ref[...] = out.astype(BF16)

    row = pl.BlockSpec((tm, D), lambda i: (i, 0))
    vec = pl.BlockSpec((1, D), lambda i: (0, 0))
    return pl.pallas_call(
        body, name=name,
        out_shape=(jax.ShapeDtypeStruct((S, D), F32), jax.ShapeDtypeStruct((S, D), BF16),
                   jax.ShapeDtypeStruct((S, D), F32), jax.ShapeDtypeStruct((S, 1), F32)),
        grid=(S // tm,), in_specs=[row, row, vec, vec],
        out_specs=(row, row, row, pl.BlockSpec((tm, 1), lambda i: (i, 0))),
        compiler_params=_params(("parallel",)),
    )(x, y, g.reshape(1, D), b.reshape(1, D))


def _ln_bwd(dout, xh, rstd, g, name):
    S, D = dout.shape
    tm = _tile(S, 256, SUBLANES)

    def body(d_ref, xh_ref, r_ref, g_ref, dz_ref, dg_ref, db_ref):
        i = pl.program_id(0)

        @pl.when(i == 0)
        def _():
            dg_ref[...] = jnp.zeros_like(dg_ref)
            db_ref[...] = jnp.zeros_like(db_ref)

        d = d_ref[...]
        xhv = xh_ref[...]
        dxh = d * g_ref[...]
        m1 = jnp.mean(dxh, axis=-1, keepdims=True)
        m2 = jnp.mean(dxh * xhv, axis=-1, keepdims=True)
        dz_ref[...] = r_ref[...] * (dxh - m1 - xhv * m2)
        dg_ref[...] += jnp.sum(d * xhv, axis=0, keepdims=True)
        db_ref[...] += jnp.sum(d, axis=0, keepdims=True)

    row = pl.BlockSpec((tm, D), lambda i: (i, 0))
    vec = pl.BlockSpec((1, D), lambda i: (0, 0))
    dz, dg, db = pl.pallas_call(
        body, name=name,
        out_shape=(jax.ShapeDtypeStruct((S, D), F32), jax.ShapeDtypeStruct((1, D), F32),
                   jax.ShapeDtypeStruct((1, D), F32)),
        grid=(S // tm,), in_specs=[row, row, pl.BlockSpec((tm, 1), lambda i: (i, 0)), vec],
        out_specs=(row, vec, vec),
        compiler_params=_params(("arbitrary",)),
    )(dout, xh, rstd, g.reshape(1, D))
    return dz, dg[0], db[0]


def _sigmoid(x):
    return 0.5 * jnp.tanh(0.5 * x) + 0.5


def _glu_fwd(vg, name):
    S, D2 = vg.shape
    D = D2 // 2
    tm = _tile(S, 512, SUBLANES)

    def body(v_ref, g_ref, o_ref):
        o_ref[...] = v_ref[...] * _sigmoid(g_ref[...])

    return pl.pallas_call(
        body, name=name, out_shape=jax.ShapeDtypeStruct((S, D), F32), grid=(S // tm,),
        in_specs=[pl.BlockSpec((tm, D), lambda i: (i, 0)), pl.BlockSpec((tm, D), lambda i: (i, 1))],
        out_specs=pl.BlockSpec((tm, D), lambda i: (i, 0)),
        compiler_params=_params(("parallel",)),
    )(vg, vg)


def _glu_bwd(vg, dm, name):
    S, D2 = vg.shape
    D = D2 // 2
    tm = _tile(S, 512, SUBLANES)

    def body(v_ref, g_ref, dm_ref, dv_ref, dg_ref):
        sg = _sigmoid(g_ref[...])
        d = dm_ref[...]
        dv_ref[...] = (d * sg).astype(BF16)
        dg_ref[...] = (d * v_ref[...] * sg * (1.0 - sg)).astype(BF16)

    blk = pl.BlockSpec((tm, D), lambda i: (i, 0))
    dv, dg = pl.pallas_call(
        body, name=name,
        out_shape=(jax.ShapeDtypeStruct((S, D), BF16), jax.ShapeDtypeStruct((S, D), BF16)),
        grid=(S // tm,), in_specs=[blk, pl.BlockSpec((tm, D), lambda i: (i, 1)), blk],
        out_specs=(blk, blk), compiler_params=_params(("parallel",)),
    )(vg, vg, dm)
    return jnp.concatenate([dv, dg], axis=1)


def _loss_fwd_bwd(y, target):
    S, D = y.shape
    tm = _tile(S, 256, SUBLANES)

    def body(y_ref, t_ref, dy_ref, l_ref):
        i = pl.program_id(0)

        @pl.when(i == 0)
        def _():
            l_ref[...] = jnp.zeros_like(l_ref)

        e = y_ref[...] - t_ref[...]
        dy_ref[...] = e * (1.0 / D)
        l_ref[...] += jnp.sum(e * e, axis=0, keepdims=True) * (0.5 / D)

    row = pl.BlockSpec((tm, D), lambda i: (i, 0))
    dy, part = pl.pallas_call(
        body, name="loss", out_shape=(jax.ShapeDtypeStruct((S, D), F32), jax.ShapeDtypeStruct((1, D), F32)),
        grid=(S // tm,), in_specs=[row, row], out_specs=(row, pl.BlockSpec((1, D), lambda i: (0, 0))),
        compiler_params=_params(("arbitrary",)),
    )(y, target)
    return dy, jnp.sum(part)


def _tri(n, lower):
    r = lax.broadcasted_iota(jnp.int32, (n, n), 0)
    c = lax.broadcasted_iota(jnp.int32, (n, n), 1)
    return jnp.where((c <= r) if lower else (c >= r), 1.0, 0.0)


def _fox_gate_fwd(fl, bf):
    S, W = fl.shape
    tm = _tile(S, 256, SUBLANES)

    def body(fl_ref, b_ref, c_ref, carry):
        i = pl.program_id(0)

        @pl.when(i == 0)
        def _():
            carry[...] = jnp.zeros_like(carry)

        x = fl_ref[...] + b_ref[...]
        lf = jnp.minimum(x, 0.0) - jnp.log(1.0 + jnp.exp(-jnp.abs(x)))
        c_ref[...] = jnp.dot(_tri(tm, True), lf, precision=lax.Precision.HIGHEST,
                             preferred_element_type=F32) + carry[...]
        carry[...] += jnp.sum(lf, axis=0, keepdims=True)

    blk = pl.BlockSpec((tm, W), lambda i: (i, 0))
    return pl.pallas_call(
        body, name="fox_gate_fwd", out_shape=jax.ShapeDtypeStruct((S, W), F32), grid=(S // tm,),
        in_specs=[blk, pl.BlockSpec((1, W), lambda i: (0, 0))], out_specs=blk,
        scratch_shapes=[pltpu.VMEM((1, W), F32)], compiler_params=_params(("arbitrary",)),
    )(fl, bf)


def _fox_gate_bwd(dcum, fl, bf):
    S, W = fl.shape
    tm = _tile(S, 256, SUBLANES)
    nb = S // tm

    def body(dc_ref, fl_ref, b_ref, dfl_ref, db_ref, carry):
        i = pl.program_id(0)

        @pl.when(i == 0)
        def _():
            carry[...] = jnp.zeros_like(carry)
            db_ref[...] = jnp.zeros_like(db_ref)

        dc = dc_ref[...]
        r = jnp.dot(_tri(tm, False), dc, precision=lax.Precision.HIGHEST, preferred_element_type=F32) + carry[...]
        carry[...] += jnp.sum(dc, axis=0, keepdims=True)
        x = fl_ref[...] + b_ref[...]
        e = jnp.exp(-jnp.abs(x))
        dfl = r * jnp.where(x >= 0, e, 1.0) / (1.0 + e)
        dfl_ref[...] = dfl
        db_ref[...] += jnp.sum(dfl, axis=0, keepdims=True)

    blk = pl.BlockSpec((tm, W), lambda i: (nb - 1 - i, 0))
    vec = pl.BlockSpec((1, W), lambda i: (0, 0))
    return pl.pallas_call(
        body, name="fox_gate_bwd",
        out_shape=(jax.ShapeDtypeStruct((S, W), F32), jax.ShapeDtypeStruct((1, W), F32)),
        grid=(nb,), in_specs=[blk, blk, vec], out_specs=(blk, vec),
        scratch_shapes=[pltpu.VMEM((1, W), F32)], compiler_params=_params(("arbitrary",)),
    )(dcum, fl, bf)


def _causal(t):
    row = lax.broadcasted_iota(jnp.int32, (t, t), 0)
    col = lax.broadcasted_iota(jnp.int32, (t, t), 1)
    return col <= row


def _first_head(shape):
    return lax.broadcasted_iota(jnp.int32, shape, len(shape) - 1) < HEAD_DIM


def _split3(x):
    hi = x.astype(BF16).astype(F32)
    r = x - hi
    mid = r.astype(BF16).astype(F32)
    return hi, mid, (r - mid).astype(BF16).astype(F32)


def _bias_lanes(c, query):
    lane = lax.broadcasted_iota(jnp.int32, (c.shape[0], LANES), 1)
    hi, mid, lo = _split3(c)
    if query:
        out = jnp.where(lane == 0, hi, jnp.where(lane == 1, mid, jnp.where(lane == 2, lo,
                                                                          jnp.where(lane < 6, 1.0, 0.0))))
    else:
        out = jnp.where(lane < 3, 1.0, jnp.where(lane == 3, -hi, jnp.where(lane == 4, -mid,
                                                                          jnp.where(lane == 5, -lo, 0.0))))
    return out.astype(BF16)


def _flash_fwd(proj, cum2, ex=None):
    S = proj.shape[0]
    D = (proj.shape[1] - LANES) // 3
    HP = D // LANES
    t = _tile(S, ATTN_TILE)
    nq = S // t
    scale = 1.0 / math.sqrt(HEAD_DIM)

    def body(q_ref, k_ref, v_ref, cq_ref, ck_ref, o_ref, lse_ref, kx, vb):
        qi = pl.program_id(1)

        @pl.when(qi == 0)
        def _():
            def prep(kb, c):
                sl = pl.ds(pl.multiple_of(kb * t, t), t)
                k16 = k_ref[sl, :].astype(BF16)
                ckv = ck_ref[0, sl, :]
                for a in range(2):
                    kx[a, sl, 0:LANES] = k16
                    kx[a, sl, LANES:2 * LANES] = _bias_lanes(ckv[:, a:a + 1], False)
                vb[sl, :] = v_ref[sl, :].astype(BF16)
                return c

            lax.fori_loop(0, nq, prep, 0)

        first = _first_head((t, LANES))
        qf = q_ref[...] * (scale * LOG2E)
        cqv = cq_ref[0]
        qx = [jnp.concatenate([jnp.where(keep, qf, 0.0).astype(BF16), _bias_lanes(cqv[:, a:a + 1], True)], axis=1)
              for a, keep in enumerate((first, jnp.logical_not(first)))]

        def block(ki, carry, masked):
            m_old, l_old, acc = carry
            sl = pl.ds(pl.multiple_of(ki * t, t), t)
            vv = vb[sl, :]
            m_new, l_new, corr, pv = [], [], [], []
            for a in range(2):
                s = lax.dot_general(qx[a], kx[a, sl, :], (((1,), (1,)), ((), ())), preferred_element_type=F32)
                if masked:
                    s = jnp.where(_causal(t), s, NEG_INF)
                m_a = jnp.maximum(m_old[a], jnp.max(s, axis=1, keepdims=True))
                p = jnp.exp2(s - m_a)
                c_a = jnp.exp2(m_old[a] - m_a)
                m_new.append(m_a)
                corr.append(c_a)
                l_new.append(c_a * l_old[a] + jnp.sum(p, axis=1, keepdims=True))
                pv.append(jnp.dot(p.astype(BF16), vv, preferred_element_type=F32))
            acc = jnp.where(first, corr[0] * acc + pv[0], corr[1] * acc + pv[1])
            return tuple(m_new), tuple(l_new), acc

        neg = jnp.full((t, 1), NEG_INF, F32)
        zero = jnp.zeros((t, 1), F32)
        carry = lax.fori_loop(0, qi, lambda ki, c: block(ki, c, False),
                              ((neg, neg), (zero, zero), jnp.zeros((t, LANES), F32)))
        m, l, acc = block(qi, carry, True)
        o_ref[...] = acc / jnp.where(first, l[0], l[1])
        lse_ref[0, :, 0:1] = m[0] + jnp.log2(l[0])
        lse_ref[0, :, 1:2] = m[1] + jnp.log2(l[1])

    qblk = pl.BlockSpec((t, LANES), lambda h, i: (i, h))
    r2 = pl.BlockSpec((1, t, 2), lambda h, i: (h, i, 0))
    (o, lse), extra = _call(
        body, name="fox_attn_fwd",
        out_shape=(jax.ShapeDtypeStruct((S, D), F32), jax.ShapeDtypeStruct((HP, S, 2), F32)),
        grid=(HP, nq),
        in_specs=[qblk, pl.BlockSpec((S, LANES), lambda h, i: (0, HP + h)),
                  pl.BlockSpec((S, LANES), lambda h, i: (0, 2 * HP + h)), r2,
                  pl.BlockSpec((1, S, 2), lambda h, i: (h, 0, 0))],
        out_specs=(qblk, r2),
        scratch_shapes=[pltpu.VMEM((2, S, 2 * LANES), BF16), pltpu.VMEM((S, LANES), BF16)],
        args=(proj, proj, proj, cum2, cum2), semantics=("parallel", "arbitrary"), ex=ex)
    return o, lse, extra


def _flash_bwd(proj, rows, o, do, ex=None):
    S = proj.shape[0]
    D = (proj.shape[1] - LANES) // 3
    HP = D // LANES
    t = _tile(S, ATTN_TILE)
    nb = S // t
    scale = 1.0 / math.sqrt(HEAD_DIM)

    def body(q_ref, k_ref, v_ref, rows_ref, o_ref, do_ref, dq_ref, dk_ref, dv_ref, dcq_ref, dck_ref,
             q_s, do_s, dl_s, dq_acc, dk_acc, dv_acc, dc_acc):
        kb = pl.program_id(1)

        @pl.when(kb == 0)
        def _():
            dq_acc[...] = jnp.zeros_like(dq_acc)
            dcq_ref[...] = jnp.zeros_like(dcq_ref)

            def prep(qb, c):
                sl = pl.ds(pl.multiple_of(qb * t, t), t)
                first = _first_head((t, LANES))
                qf = q_ref[sl, :] * (scale * LOG2E)
                dof = do_ref[sl, :]
                prod = dof * o_ref[sl, :]
                rv = rows_ref[0, sl, :]
                for a, keep in enumerate((first, jnp.logical_not(first))):
                    q_s[a, sl, 0:LANES] = jnp.where(keep, qf, 0.0).astype(BF16)
                    q_s[a, sl, LANES:2 * LANES] = _bias_lanes(rv[:, a:a + 1], True)
                    do_s[a, sl, :] = jnp.where(keep, dof, 0.0).astype(BF16)
                    dl_s[sl, a:a + 1] = jnp.sum(jnp.where(keep, prod, 0.0), axis=1, keepdims=True)
                return c

            lax.fori_loop(0, nb, prep, 0)

        dk_acc[...] = jnp.zeros_like(dk_acc)
        dv_acc[...] = jnp.zeros_like(dv_acc)
        dc_acc[...] = jnp.zeros_like(dc_acc)
        first = _first_head((t, LANES))
        kf = k_ref[...]
        kk = kf.astype(BF16)
        k_own = (jnp.where(first, kf, 0.0).astype(BF16), jnp.where(first, 0.0, kf).astype(BF16))
        ckv = rows_ref[0, pl.ds(pl.multiple_of(kb * t, t), t), :]
        kx = [jnp.concatenate([kk, _bias_lanes(ckv[:, a:a + 1], False)], axis=1) for a in range(2)]
        vv = v_ref[...].astype(BF16)

        def block(qb, masked):
            sl = pl.ds(pl.multiple_of(qb * t, t), t)
            rv = rows_ref[0, sl, :]
            dlv = dl_s[sl, :]
            dq_new = dq_acc[sl, :]
            for a in range(2):
                dob = do_s[a, sl, :]
                s = lax.dot_general(q_s[a, sl, :], kx[a], (((1,), (1,)), ((), ())), preferred_element_type=F32)
                if masked:
                    s = jnp.where(_causal(t), s, NEG_INF)
                p = jnp.exp2(s - rv[:, 2 + a:3 + a])
                dv_acc[...] += lax.dot_general(p.astype(BF16), dob, (((0,), (0,)), ((), ())),
                                               preferred_element_type=F32)
                dp = lax.dot_general(dob, vv, (((1,), (1,)), ((), ())), preferred_element_type=F32)
                ds = p * (dp - dlv[:, a:a + 1])
                dsb = ds.astype(BF16)
                dk_acc[...] += lax.dot_general(dsb, q_s[a, sl, 0:LANES], (((0,), (0,)), ((), ())),
                                               preferred_element_type=F32)
                dq_new = dq_new + jnp.dot(dsb, k_own[a], preferred_element_type=F32) * scale
                dcq_ref[0, sl, a:a + 1] += jnp.sum(ds, axis=1, keepdims=True)
                dc_acc[a:a + 1, :] -= jnp.sum(ds, axis=0, keepdims=True)
            dq_acc[sl, :] = dq_new

        block(kb, True)

        def rest(qb, c):
            block(qb, False)
            return c

        lax.fori_loop(kb + 1, nb, rest, 0)
        dk_ref[...] = (dk_acc[...] * (1.0 / LOG2E)).astype(BF16)
        dv_ref[...] = dv_acc[...].astype(BF16)
        dck_ref[0, 0] = dc_acc[0:1, :]
        dck_ref[1, 0] = dc_acc[1:2, :]

        @pl.when(kb == nb - 1)
        def _():
            dq_ref[...] = dq_acc[...].astype(BF16)

    full = lambda c0: pl.BlockSpec((S, LANES), lambda h, j, c0=c0: (0, c0 + h))
    blk = lambda c0: pl.BlockSpec((t, LANES), lambda h, j, c0=c0: (j, c0 + h))
    f32 = lambda *s: jax.ShapeDtypeStruct(s, F32)
    b16 = jax.ShapeDtypeStruct((S, D), BF16)
    outs, extra = _call(
        body, name="fox_attn_bwd",
        out_shape=(b16, b16, b16, f32(HP, S, 2), f32(2 * HP, nb, 1, t)),
        grid=(HP, nb),
        in_specs=[full(0), blk(HP), blk(2 * HP), pl.BlockSpec((1, S, 4), lambda h, j: (h, 0, 0)), full(0), full(0)],
        out_specs=(full(0), blk(0), blk(0), pl.BlockSpec((1, S, 2), lambda h, j: (h, 0, 0)),
                   pl.BlockSpec((2, 1, 1, t), lambda h, j: (h, j, 0, 0))),
        scratch_shapes=[pltpu.VMEM((2, S, 2 * LANES), BF16), pltpu.VMEM((2, S, LANES), BF16),
                        pltpu.VMEM((S, 2), F32), pltpu.VMEM((S, LANES), F32), pltpu.VMEM((t, LANES), F32),
                        pltpu.VMEM((t, LANES), F32), pltpu.VMEM((2, t), F32)],
        args=(proj, proj, proj, rows, o, do), semantics=("parallel", "arbitrary"), ex=ex)
    return (*outs, extra)


def _s5_consts(T):
    rows = T * SUBLANES
    rr = lax.broadcasted_iota(jnp.int32, (rows, T), 0)
    tt = lax.broadcasted_iota(jnp.int32, (rows, T), 1)
    rep = jnp.where(rr // SUBLANES == tt, 1.0, 0.0).astype(BF16)
    r2 = lax.broadcasted_iota(jnp.int32, (rows, S5_PART), 0)
    c2 = lax.broadcasted_iota(jnp.int32, (rows, S5_PART), 1)
    mask = (c2 // (S5_PART // SUBLANES)) == (r2 % SUBLANES)
    return rep, mask


def _gelu(y):
    c = math.sqrt(2.0 / math.pi)
    return 0.5 * y * (1.0 + jnp.tanh(c * (y + 0.044715 * y * y * y)))


def _gelu_grad(y):
    c = math.sqrt(2.0 / math.pi)
    th = jnp.tanh(c * (y + 0.044715 * y * y * y))
    return 0.5 * (1.0 + th) + 0.5 * y * (1.0 - th * th) * c * (1.0 + 3.0 * 0.044715 * y * y)


def _s5_fwd(x, rmat, cmat, lam, dskip, ex=None):
    S, D = x.shape
    NQ = D // S5_PART
    T = _tile(S, 128, SUBLANES)
    rows = T * SUBLANES

    def body(x_ref, r_ref, c_ref, lam_ref, d_ref, y_ref, yg_ref, h_ref, bu_s, carry):
        i = pl.program_id(0)

        @pl.when(i == 0)
        def _():
            carry[...] = jnp.zeros_like(carry)

        rep, mask = _s5_consts(T)
        cols = [pl.ds(q * S5_PART, S5_PART) for q in range(NQ)]
        for q in range(NQ):
            xrep = jnp.dot(rep, x_ref[:, cols[q]].astype(BF16), preferred_element_type=F32)
            lx = jnp.where(mask, xrep, 0.0).astype(BF16)
            bu_s[q] = jnp.dot(lx, r_ref[q], preferred_element_type=F32)
        lam_v = [(lam_ref[q, :, 0:LANES], lam_ref[q, :, LANES:2 * LANES]) for q in range(NQ)]

        def step(t, c):
            o = pl.multiple_of(t * SUBLANES, SUBLANES)
            new = []
            for q in range(NQ):
                hr, hi = c[q]
                ar, ai = lam_v[q]
                sl = bu_s[q, pl.ds(o, SUBLANES), :]
                nhr = ar * hr - ai * hi + sl[:, 0:LANES]
                nhi = ar * hi + ai * hr + sl[:, LANES:2 * LANES]
                h_ref[q, pl.ds(o, SUBLANES), 0:LANES] = nhr
                h_ref[q, pl.ds(o, SUBLANES), LANES:2 * LANES] = nhi
                new.append((nhr, nhi))
            return tuple(new)

        fin = lax.fori_loop(0, T, step,
                            tuple((carry[q, :, 0:LANES], carry[q, :, LANES:2 * LANES]) for q in range(NQ)))
        for q in range(NQ):
            carry[q, :, 0:LANES] = fin[q][0]
            carry[q, :, LANES:2 * LANES] = fin[q][1]
            z = jnp.dot(h_ref[q].astype(BF16), c_ref[q], preferred_element_type=F32)
            z = jnp.where(mask, z, 0.0)
            y = jnp.sum(z.reshape(T, SUBLANES, S5_PART), axis=1) + d_ref[:, cols[q]] * x_ref[:, cols[q]]
            y_ref[:, cols[q]] = y
            yg_ref[:, cols[q]] = _gelu(y).astype(BF16)

    xs = pl.BlockSpec((T, D), lambda i: (i, 0))
    ms = pl.BlockSpec((NQ, S5_PART, S5_PART), lambda i: (0, 0, 0))
    outs, extra = _call(
        body, name="s5_scan_fwd",
        out_shape=(jax.ShapeDtypeStruct((S, D), F32), jax.ShapeDtypeStruct((S, D), BF16),
                   jax.ShapeDtypeStruct((NQ, S * SUBLANES, S5_PART), F32)),
        grid=(S // T,),
        in_specs=[xs, ms, ms, pl.BlockSpec((NQ, SUBLANES, S5_PART), lambda i: (0, 0, 0)),
                  pl.BlockSpec((1, D), lambda i: (0, 0))],
        out_specs=(xs, xs, pl.BlockSpec((NQ, rows, S5_PART), lambda i: (0, i, 0))),
        scratch_shapes=[pltpu.VMEM((NQ, rows, S5_PART), F32), pltpu.VMEM((NQ, SUBLANES, S5_PART), F32)],
        args=(x, rmat, cmat, lam, dskip), semantics=("arbitrary",), ex=ex)
    return (*outs, extra)


def _s5_bwd(x, y, dyg, hs, rmat, cmat, lam, dskip, res, res_scale, ex=None):
    S, D = x.shape
    NQ = D // S5_PART
    T = _tile(S, 128, SUBLANES)
    nb = S // T
    rows = T * SUBLANES

    def body(x_ref, y_ref, dyg_ref, res_ref, h_ref, hp_ref, r_ref, c_ref, lam_ref, d_ref,
             dx_ref, dr_ref, dc_ref, dlam_ref, dd_ref, dh_s, g_s, hs_s, carry):
        i = pl.program_id(0)

        @pl.when(i == 0)
        def _():
            carry[...] = jnp.zeros_like(carry)
            dr_ref[...] = jnp.zeros_like(dr_ref)
            dc_ref[...] = jnp.zeros_like(dc_ref)
            dlam_ref[...] = jnp.zeros_like(dlam_ref)
            dd_ref[...] = jnp.zeros_like(dd_ref)

        rep, mask = _s5_consts(T)
        cols = [pl.ds(q * S5_PART, S5_PART) for q in range(NQ)]
        dys, ldys = [], []
        for q in range(NQ):
            dy = dyg_ref[:, cols[q]] * _gelu_grad(y_ref[:, cols[q]])
            dyrep = jnp.dot(rep, dy.astype(BF16), preferred_element_type=F32)
            ldy = jnp.where(mask, dyrep, 0.0).astype(BF16)
            dh_s[q] = lax.dot_general(ldy, c_ref[q], (((1,), (1,)), ((), ())), preferred_element_type=F32)
            dys.append(dy)
            ldys.append(ldy)
        lam_v = [(lam_ref[q, :, 0:LANES], lam_ref[q, :, LANES:2 * LANES]) for q in range(NQ)]

        def step(n, c):
            o = pl.multiple_of((T - 1 - n) * SUBLANES, SUBLANES)
            new = []
            for q in range(NQ):
                gr, gi = c[q]
                ar, ai = lam_v[q]
                sl = dh_s[q, pl.ds(o, SUBLANES), :]
                ngr = sl[:, 0:LANES] + ar * gr + ai * gi
                ngi = sl[:, LANES:2 * LANES] - ai * gr + ar * gi
                g_s[q, pl.ds(o, SUBLANES), 0:LANES] = ngr
                g_s[q, pl.ds(o, SUBLANES), LANES:2 * LANES] = ngi
                new.append((ngr, ngi))
            return tuple(new)

        fin = lax.fori_loop(0, T, step,
                            tuple((carry[q, :, 0:LANES], carry[q, :, LANES:2 * LANES]) for q in range(NQ)))
        for q in range(NQ):
            carry[q, :, 0:LANES] = fin[q][0]
            carry[q, :, LANES:2 * LANES] = fin[q][1]
            xv = x_ref[:, cols[q]]
            hv = h_ref[q]
            hs_s[0:SUBLANES, :] = jnp.where(i == nb - 1, 0.0, hp_ref[q])
            hs_s[SUBLANES:rows + SUBLANES, :] = hv
            hprev = hs_s[0:rows, :]
            gv = g_s[q]
            g_re, g_im = gv[:, 0:LANES], gv[:, LANES:2 * LANES]
            hp_re, hp_im = hprev[:, 0:LANES], hprev[:, LANES:2 * LANES]
            dar = jnp.sum((g_re * hp_re + g_im * hp_im).reshape(T, SUBLANES, LANES), axis=0)
            dai = jnp.sum((g_im * hp_re - g_re * hp_im).reshape(T, SUBLANES, LANES), axis=0)
            dlam_ref[q, :, 0:LANES] += dar
            dlam_ref[q, :, LANES:2 * LANES] += dai

            gb = gv.astype(BF16)
            xrep = jnp.dot(rep, xv.astype(BF16), preferred_element_type=F32)
            lx = jnp.where(mask, xrep, 0.0).astype(BF16)
            dr_ref[q] += lax.dot_general(lx, gb, (((0,), (0,)), ((), ())), preferred_element_type=F32)
            dc_ref[q] += lax.dot_general(hv.astype(BF16), ldys[q], (((0,), (0,)), ((), ())),
                                         preferred_element_type=F32)
            zx = lax.dot_general(gb, r_ref[q], (((1,), (1,)), ((), ())), preferred_element_type=F32)
            zx = jnp.where(mask, zx, 0.0)
            dx_ref[:, cols[q]] = (jnp.sum(zx.reshape(T, SUBLANES, S5_PART), axis=1) + d_ref[:, cols[q]] * dys[q]
                                  + res_scale * res_ref[:, cols[q]])
            dd_ref[:, cols[q]] += jnp.sum(dys[q] * xv, axis=0, keepdims=True)

    xs = pl.BlockSpec((T, D), lambda i: (nb - 1 - i, 0))
    ms = pl.BlockSpec((NQ, S5_PART, S5_PART), lambda i: (0, 0, 0))
    ls = pl.BlockSpec((NQ, SUBLANES, S5_PART), lambda i: (0, 0, 0))
    ds_ = pl.BlockSpec((1, D), lambda i: (0, 0))
    outs, extra = _call(
        body, name="s5_scan_bwd",
        out_shape=(jax.ShapeDtypeStruct((S, D), F32), jax.ShapeDtypeStruct((NQ, S5_PART, S5_PART), F32),
                   jax.ShapeDtypeStruct((NQ, S5_PART, S5_PART), F32),
                   jax.ShapeDtypeStruct((NQ, SUBLANES, S5_PART), F32), jax.ShapeDtypeStruct((1, D), F32)),
        grid=(nb,),
        in_specs=[xs, xs, xs, xs, pl.BlockSpec((NQ, rows, S5_PART), lambda i: (0, nb - 1 - i, 0)),
                  pl.BlockSpec((NQ, SUBLANES, S5_PART), lambda i: (0, jnp.maximum((nb - 1 - i) * T - 1, 0), 0)),
                  ms, ms, ls, ds_],
        out_specs=(xs, ms, ms, ls, ds_),
        scratch_shapes=[pltpu.VMEM((NQ, rows, S5_PART), F32), pltpu.VMEM((NQ, rows, S5_PART), F32),
                        pltpu.VMEM((rows + SUBLANES, S5_PART), F32), pltpu.VMEM((NQ, SUBLANES, S5_PART), F32)],
        args=(x, y, dyg, res, hs, hs, rmat, cmat, lam, dskip), semantics=("arbitrary",), ex=ex)
    return (*outs, extra)


def _s5_discretise(a_re, a_im, log_dt, b_re, b_im):
    dt = jnp.exp(log_dt)[:, None]
    mag = jnp.exp(a_re * dt)
    ang = a_im * dt
    lb_re = mag * jnp.cos(ang)
    lb_im = mag * jnp.sin(ang)
    den = a_re * a_re + a_im * a_im
    nr = lb_re - 1.0
    ni = lb_im
    z_re = (nr * a_re + ni * a_im) / den
    z_im = (ni * a_re - nr * a_im) / den
    bb_re = z_re[..., None] * b_re - z_im[..., None] * b_im
    bb_im = z_re[..., None] * b_im + z_im[..., None] * b_re
    return lb_re, lb_im, bb_re, bb_im


def _s5_expand(w):
    G = w.shape[0]
    NQ = G // 16
    base = w.reshape(NQ, S5_PART, S5_STATE)
    half = (jnp.arange(S5_PART) // S5_GROUP) % 2
    sel = (half[:, None] == jnp.arange(2)[None, :]).astype(w.dtype)
    out = base[:, :, None, :] * sel[None, :, :, None]
    return out.reshape(NQ, S5_PART, 2 * S5_STATE)


def _s5_extract(m):
    NQ = m.shape[0]
    half = (jnp.arange(S5_PART) // S5_GROUP) % 2
    sel = (half[:, None] == jnp.arange(2)[None, :]).astype(m.dtype)
    base = jnp.sum(m.reshape(NQ, S5_PART, 2, S5_STATE) * sel[None, :, :, None], axis=2)
    return base.reshape(NQ * 16, S5_GROUP, S5_STATE)


def _s5_slab(v):
    return v.reshape(v.shape[0] // 16, SUBLANES, LANES)


def _adamw(w, g, m, v, name):
    R, C = w.shape
    tr = _tile(R, 256, SUBLANES)
    c1 = 1.0 / (1.0 - ADAM_B1 ** ADAM_STEP)
    c2 = 1.0 / (1.0 - ADAM_B2 ** ADAM_STEP)

    def body(w_ref, g_ref, m_ref, v_ref, d_ref, nm_ref, nv_ref):
        gv = g_ref[...]
        nm = ADAM_B1 * m_ref[...] + (1.0 - ADAM_B1) * gv
        nv = ADAM_B2 * v_ref[...] + (1.0 - ADAM_B2) * (gv * gv)
        nm_ref[...] = nm
        nv_ref[...] = nv
        d_ref[...] = -ADAM_LR * ((nm * c1) / (jnp.sqrt(nv * c2) + ADAM_EPS) + ADAM_WD * w_ref[...])

    blk = pl.BlockSpec((tr, C), lambda i: (i, 0))
    sh = jax.ShapeDtypeStruct((R, C), F32)
    return pl.pallas_call(
        body, name=name, out_shape=(sh, sh, sh), grid=(R // tr,), in_specs=[blk] * 4, out_specs=(blk,) * 3,
        compiler_params=_params(("parallel",)),
    )(w, g, m, v)


def _adamw_recv(parts, w, m, v, name):
    L, R, C = w.shape
    tr = _tile(R, 256, SUBLANES)
    c1 = 1.0 / (1.0 - ADAM_B1 ** ADAM_STEP)
    c2 = 1.0 / (1.0 - ADAM_B2 ** ADAM_STEP)

    def body(*refs):
        p_refs = refs[:L]
        w_ref, m_ref, v_ref, g_ref, d_ref, nm_ref, nv_ref = refs[L:]
        li = pl.program_id(0)
        for l in range(L):
            @pl.when(li == l)
            def _(p_ref=p_refs[l]):
                gv = p_ref[0].astype(F32)
                for k in range(1, N_DEV):
                    gv = gv + p_ref[k].astype(F32)
                g_ref[...] = gv
                nm = ADAM_B1 * m_ref[...] + (1.0 - ADAM_B1) * gv
                nv = ADAM_B2 * v_ref[...] + (1.0 - ADAM_B2) * (gv * gv)
                nm_ref[...] = nm
                nv_ref[...] = nv
                d_ref[...] = -ADAM_LR * ((nm * c1) / (jnp.sqrt(nv * c2) + ADAM_EPS) + ADAM_WD * w_ref[...])

    p_specs = [pl.BlockSpec((N_DEV, tr, C), lambda li, i, l=l: (0, jnp.where(li == l, i, 0), 0)) for l in range(L)]
    blk = pl.BlockSpec((None, tr, C), lambda li, i: (li, i, 0))
    sh = jax.ShapeDtypeStruct((L, R, C), F32)
    return pl.pallas_call(
        body, name=name, out_shape=(sh, sh, sh, sh), grid=(L, R // tr),
        in_specs=p_specs + [blk, blk, blk], out_specs=(blk,) * 4,
        compiler_params=_params(("parallel", "parallel")),
    )(*parts, w, m, v)


def _sum8(parts, name):
    _, R, C = parts.shape
    tr = _tile(R, 256, SUBLANES)

    def body(p_ref, o_ref):
        acc = p_ref[0].astype(F32)
        for k in range(1, N_DEV):
            acc = acc + p_ref[k].astype(F32)
        o_ref[...] = acc

    return pl.pallas_call(
        body, name=name, out_shape=jax.ShapeDtypeStruct((R, C), F32), grid=(R // tr,),
        in_specs=[pl.BlockSpec((N_DEV, tr, C), lambda i: (0, i, 0))],
        out_specs=pl.BlockSpec((tr, C), lambda i: (i, 0)), compiler_params=_params(("parallel",)),
    )(parts)


def _peers():
    x, y, c = lax.axis_index("x"), lax.axis_index("y"), lax.axis_index("c")
    me = 4 * x + 2 * y + c
    out = []
    for k in range(1, N_DEV):
        kx, ky, kc = (k >> 2) & 1, (k >> 1) & 1, k & 1
        px, py, pc = x ^ kx, y ^ ky, c ^ kc
        out.append(((px, py, pc), 4 * px + 2 * py + pc))
    return me, out


SIBLING = 1
SAME_CORE = (2, 4, 6)


class Exchange:
    def __init__(self, xs, scatter):
        self.xs = list(xs)
        self.scatter = list(scatter)
        self.n = len(self.xs)

    def out_shapes(self):
        return tuple(jax.ShapeDtypeStruct(x.shape if sc else (N_DEV,) + x.shape, x.dtype)
                     for x, sc in zip(self.xs, self.scatter))

    def sems(self):
        return [pltpu.SemaphoreType.DMA((self.n * N_DEV,)), pltpu.SemaphoreType.DMA((self.n * N_DEV,)),
                pltpu.SemaphoreType.DMA((self.n,))]

    def _copy(self, i, m, src, dst, to, send_sems, recv_sems):
        return pltpu.make_async_remote_copy(src_ref=src, dst_ref=dst, send_sem=send_sems.at[i * N_DEV + m],
                                            recv_sem=recv_sems.at[i * N_DEV + m], device_id=to, device_id_type=MESH)

    def start(self, x_refs, o_refs, send_sems, recv_sems, local_sems):
        me, peers = _peers()
        for i, (x_ref, o_ref, sc) in enumerate(zip(x_refs, o_refs, self.scatter)):
            pltpu.make_async_copy(x_ref.at[me] if sc else x_ref, o_ref.at[me], local_sems.at[i]).start()
            for m in (range(1, N_DEV) if sc else (SIBLING,) + SAME_CORE):
                dev, idx = peers[m - 1]
                self._copy(i, m, x_ref.at[idx] if sc else x_ref, o_ref.at[me], dev, send_sems, recv_sems).start()

    def middle(self, x_refs, o_refs, send_sems, recv_sems, local_sems):
        me, peers = _peers()
        sibling = peers[SIBLING - 1][0]
        for i, (x_ref, o_ref, sc) in enumerate(zip(x_refs, o_refs, self.scatter)):
            if sc:
                continue
            for m in SAME_CORE:
                dev, idx = peers[m - 1]
                self._copy(i, m, x_ref, o_ref.at[idx], dev, send_sems, recv_sems).wait_recv()
                self._copy(i, m ^ 1, o_ref.at[idx], o_ref.at[idx], sibling, send_sems, recv_sems).start()

    def wait(self, x_refs, o_refs, send_sems, recv_sems, local_sems):
        me, peers = _peers()
        for i, (x_ref, o_ref, sc) in enumerate(zip(x_refs, o_refs, self.scatter)):
            for m in range(1, N_DEV):
                dev, idx = peers[m - 1]
                cp = self._copy(i, m, x_ref.at[idx] if sc else x_ref, o_ref.at[idx], dev, send_sems, recv_sems)
                if sc or m not in SAME_CORE:
                    cp.wait_recv()
                cp.wait_send()
            pltpu.make_async_copy(x_ref.at[me] if sc else x_ref, o_ref.at[me], local_sems.at[i]).wait()


def _exchange(ex, name):
    n = ex.n

    def body(*refs):
        x_refs, o_refs, sems = refs[:n], refs[n:2 * n], refs[2 * n:]
        ex.start(x_refs, o_refs, *sems)
        ex.middle(x_refs, o_refs, *sems)
        ex.wait(x_refs, o_refs, *sems)

    hbm = pl.BlockSpec(memory_space=pltpu.HBM)
    return pl.pallas_call(body, name=name, out_shape=ex.out_shapes(), in_specs=[hbm] * n, out_specs=(hbm,) * n,
                          scratch_shapes=ex.sems())(*ex.xs)


def _call(body, *, name, out_shape, grid, in_specs, out_specs, scratch_shapes, args, semantics, ex=None):
    if ex is None:
        return pl.pallas_call(body, name=name, out_shape=out_shape, grid=grid, in_specs=in_specs, out_specs=out_specs,
                              scratch_shapes=scratch_shapes, compiler_params=_params(semantics))(*args), ()
    n, ni, no, ns = ex.n, len(args), len(out_shape), len(scratch_shapes)

    def wrapped(*refs):
        ins, cx = refs[:ni], refs[ni:ni + n]
        outs, co = refs[ni + n:ni + n + no], refs[ni + n + no:ni + 2 * n + no]
        scratch, sems = refs[ni + 2 * n + no:ni + 2 * n + no + ns], refs[ni + 2 * n + no + ns:]
        step = functools.reduce(lambda acc, a: acc * grid[a] + pl.program_id(a), range(len(grid)), 0)
        steps = math.prod(grid)

        @pl.when(step == 0)
        def _():
            ex.start(cx, co, *sems)

        body(*ins, *outs, *scratch)

        @pl.when(step == (steps * 3) // 4 - (steps > 1))
        def _():
            ex.middle(cx, co, *sems)

        @pl.when(step == steps - 1)
        def _():
            ex.wait(cx, co, *sems)

    hbm = pl.BlockSpec(memory_space=pltpu.HBM)
    res = pl.pallas_call(
        wrapped, name=name, out_shape=tuple(out_shape) + ex.out_shapes(), grid=grid,
        in_specs=list(in_specs) + [hbm] * n, out_specs=tuple(out_specs) + (hbm,) * n,
        scratch_shapes=list(scratch_shapes) + ex.sems(),
        compiler_params=_params(("arbitrary",) * len(grid)))(*args, *ex.xs)
    return res[:no], res[no:]


def _pack_flat(arrs):
    cat = jnp.concatenate([a.reshape(-1) for a in arrs])
    per = PACK_COLS * 2 * SUBLANES
    tot = -(-cat.shape[0] // per) * per
    return jnp.pad(cat, (0, tot - cat.shape[0])).reshape(tot // PACK_COLS, PACK_COLS)


def _unpack_flat(packed, shapes):
    flat = packed.reshape(-1)
    out, o = [], 0
    for s in shapes:
        n = math.prod(s)
        out.append(flat[o:o + n].reshape(s))
        o += n
    return out


def _ffn_fwd(x, xb, wt, wo4, g, b, alpha, tag, ex=None):
    h8, a4, extra = _ffn_h(xb, wt, tag + "_h", ex)
    out, outb, xh, rstd = _ffn_y_ln(a4, wo4, x, g, b, alpha, 0.5, tag + "_y_ln")
    return out, outb, (xb, h8, a4, xh, rstd), extra


def _ffn_bwd(dout, saved, wt, wo4, g, alpha, tag, carry=None, last=None):
    x, h8, a4, xh, rstd = saved
    load = dict(carry or {})
    slots = [[], [], [], []]
    for n, k in enumerate(sorted(load, key=lambda k: -load[k].size)):
        slots[min(n, 3)].append(k)
    arrived = {}

    def run(fn, slot, *args):
        keys = slots[slot]
        if not keys:
            return fn(*args)
        res, extra = fn(*args, ex=Exchange([load[k] for k in keys], [k != "small" for k in keys]))
        arrived.update(zip(keys, extra))
        return res

    dz, dg, db = _ln_bwd(dout, xh, rstd, g, tag + "_ln_bwd")
    if last is not None:
        (key_out, key_in), load["small"] = last(dg, db)
        slots[2].append("small")
    dw_out = run(_ffn_dwout, 3, a4, dz, 0.5, tag + "_dwout")
    dw_out = dw_out.reshape(N_DEV, -1, dw_out.shape[-1])
    if last is not None:
        load[key_out] = dw_out
        slots[0].append(key_out)
    dh8 = run(_ffn_dh, 2, dz, wo4, h8, 0.5, tag + "_dh")
    dw_in = run(_ffn_dwin, 0, x, dh8, tag + "_dwin")
    if last is not None:
        load[key_in] = dw_in
        slots[1].append(key_in)
    dx = run(_ffn_dx, 1, dh8, wt, dz, alpha, tag + "_dx")
    return dx, dw_in, dw_out, dg, db, arrived


def _fox_fwd(x, xb, w_in_pad, b_f_pad, w_o, g, b, alpha, tag, ex=None):
    S, D = x.shape
    H = D // HEAD_DIM
    proj = _mm(xb, w_in_pad, name=tag + "_proj")
    fl = proj[:, 3 * D:]
    cum = _fox_gate_fwd(fl, b_f_pad)
    cum2 = (cum[:, :H].T * LOG2E).reshape(H // 2, 2, S).transpose(0, 2, 1)
    o, lse, extra = _flash_fwd(proj, cum2, ex)
    m = _mm(o, w_o, name=tag + "_out")
    out, outb, xh, rstd = _res_ln_fwd(x, m, g, b, alpha, 1.0, tag + "_ln")
    return out, outb, (xb, proj, jnp.concatenate([cum2, lse], axis=2), o, fl, xh, rstd), extra


def _fox_bwd(dout, saved, w_in_pad, b_f_pad, w_o, g, alpha, tag, ex=None):
    x, proj, rows, o, fl, xh, rstd = saved
    S, D = x.shape
    H = D // HEAD_DIM
    dz, dg, db = _ln_bwd(dout, xh, rstd, g, tag + "_ln_bwd")
    dw_o = _mm(o, dz, ta=True, name=tag + "_dwo")
    do = _mm(dz, w_o, tb=True, name=tag + "_do")
    dq, dk, dv, dcq, dck, extra = _flash_bwd(proj, rows, o, do, ex)
    dcq = dcq.transpose(0, 2, 1).reshape(H, S)
    dcum = jnp.pad((dcq + dck.reshape(H, S)).T, ((0, 0), (0, LANES - H)))
    dfl, dbf = _fox_gate_bwd(dcum, fl, b_f_pad)
    dproj = jnp.concatenate([dq, dk, dv, dfl.astype(BF16)], axis=1)
    dw_in = _mm(x, dproj, ta=True, name=tag + "_dwin")
    dx = _mm(dproj, w_in_pad, tb=True, add=dz, add_scale=alpha, name=tag + "_dx")
    shards = {"fox_w_in": _split(dw_in[None, :, :3 * D + H], True)[:, 0].astype(BF16),
              "fox_w_o": _split(dw_o[None], False)[:, 0].astype(BF16)}
    return dx, shards, {"fox_b_f": dbf[0, :H], "lnm_g": dg, "lnm_b": db}, extra


def _s5_mats(p):
    lb_re, lb_im, bb_re, bb_im = _s5_discretise(p["a_re"], p["a_im"], p["log_dt"], p["b_re"], p["b_im"])
    rmat = jnp.concatenate([_s5_expand(bb_re.transpose(0, 2, 1)), _s5_expand(bb_im.transpose(0, 2, 1))], axis=2)
    cmat = jnp.concatenate([_s5_expand(p["c_re"]).transpose(0, 2, 1), -_s5_expand(p["c_im"]).transpose(0, 2, 1)],
                           axis=1)
    lam = jnp.concatenate([_s5_slab(lb_re), _s5_slab(lb_im)], axis=2)
    return rmat.astype(BF16), cmat.astype(BF16), lam


def _s5_block_fwd(x, p, w_out, g, b, alpha, tag, ex=None):
    S, D = x.shape
    rmat, cmat, lam = _s5_mats(p)
    dskip = p["d"].reshape(1, D)
    y, yg, hs, extra = _s5_fwd(x, rmat, cmat, lam, dskip, ex)
    vg = _mm(yg, w_out, name=tag + "_vg")
    m = _glu_fwd(vg, tag + "_glu")
    out, outb, xh, rstd = _res_ln_fwd(x, m, g, b, alpha, 1.0, tag + "_ln")
    return out, outb, (x, y, yg, hs, vg, rmat, cmat, lam, dskip, xh, rstd), extra


def _s5_block_bwd(dout, saved, p, w_out, g, alpha, tag, ex=None):
    x, y, yg, hs, vg, rmat, cmat, lam, dskip, xh, rstd = saved
    S, D = x.shape
    G = D // S5_GROUP
    dz, dg, db = _ln_bwd(dout, xh, rstd, g, tag + "_ln_bwd")
    dvg = _glu_bwd(vg, dz, tag + "_glu_bwd")
    dw_out = _mm(yg, dvg, ta=True, name=tag + "_dwout")
    dyg = _mm(dvg, w_out, tb=True, name=tag + "_dyg")
    dx, dr, dc, dlam, dd, extra = _s5_bwd(x, y, dyg, hs, rmat, cmat, lam, dskip, dz, alpha, ex)
    dbb_re = _s5_extract(dr[:, :, :LANES]).transpose(0, 2, 1)
    dbb_im = _s5_extract(dr[:, :, LANES:]).transpose(0, 2, 1)
    dc_re = _s5_extract(dc[:, :LANES, :].transpose(0, 2, 1))
    dc_im = -_s5_extract(dc[:, LANES:, :].transpose(0, 2, 1))
    dlb_re = dlam[:, :, :LANES].reshape(G, S5_STATE)
    dlb_im = dlam[:, :, LANES:].reshape(G, S5_STATE)
    _, vjp = jax.vjp(_s5_discretise, p["a_re"], p["a_im"], p["log_dt"], p["b_re"], p["b_im"])
    da_re, da_im, dlog_dt, db_re, db_im = vjp((dlb_re, dlb_im, dbb_re, dbb_im))
    small = dict(s5_a_re=da_re, s5_a_im=da_im, s5_log_dt=dlog_dt, s5_b_re=db_re, s5_b_im=db_im, s5_c_re=dc_re,
                 s5_c_im=dc_im, s5_d=dd.reshape(G, S5_GROUP), lnm_g=dg, lnm_b=db)
    return dx, {"s5_w_out": _split(dw_out[None], True)[:, 0].astype(BF16)}, small, extra


FFN_NAMES = ("ffn1_w_in", "ffn1_w_out", "ffn2_w_in", "ffn2_w_out")
BIG = FFN_NAMES + ("fox_w_in", "fox_w_o", "s5_w_out")
BIG_SPLIT_COLS = {"ffn1_w_in": True, "ffn1_w_out": False, "ffn2_w_in": True, "ffn2_w_out": False,
                  "fox_w_in": True, "fox_w_o": False, "s5_w_out": True}
SMALL = ("ln1_g", "ln1_b", "lnm_g", "lnm_b", "ln2_g", "ln2_b", "fox_b_f", "s5_a_re", "s5_a_im", "s5_log_dt",
         "s5_b_re", "s5_b_im", "s5_c_re", "s5_c_im", "s5_d")
WEIGHTS = ("ffn1_w_in", "ffn1_w_out", "ln1_g", "ln1_b", "lnm_g", "lnm_b", "ffn2_w_in", "ffn2_w_out", "ln2_g", "ln2_b",
           "fox_w_in", "fox_b_f", "fox_w_o", "s5_a_re", "s5_a_im", "s5_log_dt", "s5_b_re", "s5_b_im", "s5_c_re",
           "s5_c_im", "s5_d", "s5_w_out")


def _join(gathered, split_cols):
    n, L, r, c = gathered.shape
    if split_cols:
        return gathered.transpose(1, 2, 0, 3).reshape(L, r, n * c)
    return gathered.transpose(1, 0, 2, 3).reshape(L, n * r, c)


def _split(full, split_cols):
    L, R, C = full.shape
    if split_cols:
        return full.reshape(L, R, N_DEV, C // N_DEV).transpose(2, 0, 1, 3)
    return full.reshape(L, N_DEV, R // N_DEV, C).transpose(1, 0, 2, 3)


def _group(i, part):
    if part == "mixer":
        return (("fox_w_in", i // 2), ("fox_w_o", i // 2)) if i % 2 == 0 else (("s5_w_out", i // 2),)
    return ((part + "_w_in", i), (part + "_w_out", i))


def _prepare(name, g8):
    if name in ("ffn1_w_in", "ffn2_w_in"):
        return g8
    if name in FFN_NAMES:
        n, r, c = g8.shape
        return g8.reshape(n // 2, 2 * r, c)
    full = _join(g8[:, None], BIG_SPLIT_COLS[name])[0]
    if name == "fox_w_in":
        full = jnp.pad(full, ((0, 0), (0, LANES - full.shape[0] // HEAD_DIM)))
    return full


def _local_step(x, target, small, shard_of=None, pregathered=None):
    S, D = x.shape
    H = D // HEAD_DIM
    depth = small["ln1_g"].shape[0]
    alpha = (2.0 * depth) ** 0.25
    local = pregathered is not None
    bf_pad = jnp.pad(small["fox_b_f"], ((0, 0), (0, LANES - H)))

    def s5_params(j):
        return {k: small["s5_" + k][j] for k in ("a_re", "a_im", "log_dt", "b_re", "b_im", "c_re", "c_im", "d")}

    if local:
        W = {k: _prepare(k[0], g8) for k, g8 in pregathered.items()}
    else:
        keys = _group(0, "ffn1")
        got = _exchange(Exchange([shard_of(*k) for k in keys], [False] * len(keys)), "gather_first")
        W = {k: _prepare(k[0], g8) for k, g8 in zip(keys, got)}

    def gather(keys):
        return None if local else Exchange([shard_of(*k) for k in keys], [False] * len(keys))

    def landed(keys, extra):
        if not local:
            W.update({k: _prepare(k[0], g8) for k, g8 in zip(keys, extra)})

    saved = []
    h, hb = x, x.astype(BF16)
    for i in range(depth):
        j = i // 2
        keys = _group(i, "mixer")
        h, hb, s1, extra = _ffn_fwd(h, hb, W[("ffn1_w_in", i)], W[("ffn1_w_out", i)], small["ln1_g"][i],
                                    small["ln1_b"][i], alpha, f"l{i}_ffn1", gather(keys))
        landed(keys, extra)
        keys = _group(i, "ffn2") + (_group(i + 1, "ffn1") if i + 1 < depth else ())
        ex = gather(keys)
        if i % 2 == 0:
            h, hb, s2, extra = _fox_fwd(h, hb, W[("fox_w_in", j)], bf_pad[j:j + 1], W[("fox_w_o", j)],
                                        small["lnm_g"][i], small["lnm_b"][i], alpha, f"l{i}_fox", ex)
        else:
            h, hb, s2, extra = _s5_block_fwd(h, s5_params(j), W[("s5_w_out", j)], small["lnm_g"][i],
                                             small["lnm_b"][i], alpha, f"l{i}_s5", ex)
        landed(keys, extra)
        h, hb, s3, _ = _ffn_fwd(h, hb, W[("ffn2_w_in", i)], W[("ffn2_w_out", i)], small["ln2_g"][i],
                                small["ln2_b"][i], alpha, f"l{i}_ffn2")
        saved.append((s1, s2, s3))

    dh, loss_part = _loss_fwd_bwd(h, target)

    arrived = {}
    pending = {}
    gs = {k: [None] * small[k].shape[0] for k in SMALL}
    for i in reversed(range(depth)):
        j = i // 2
        s1, s2, s3 = saved[i]
        held = {k: pending.pop(k) for k in [("ffn1_w_in", i + 1)] if i % 2 == 0 and k in pending}
        dh, dw_in, dw_out, gs["ln2_g"][i], gs["ln2_b"][i], got = _ffn_bwd(
            dh, s3, W[("ffn2_w_in", i)], W[("ffn2_w_out", i)], small["ln2_g"][i], alpha, f"l{i}_ffn2",
            None if local else pending)
        arrived.update(pending if local else got)
        pending = {("ffn2_w_in", i): dw_in, ("ffn2_w_out", i): dw_out, **held}
        keys = list(pending)
        ex = None if local else Exchange([pending[k] for k in keys], [True] * len(keys))
        if i % 2 == 0:
            dh, mix, sg, extra = _fox_bwd(dh, s2, W[("fox_w_in", j)], bf_pad[j:j + 1], W[("fox_w_o", j)],
                                          small["lnm_g"][i], alpha, f"l{i}_fox", ex)
        else:
            dh, mix, sg, extra = _s5_block_bwd(dh, s2, s5_params(j), W[("s5_w_out", j)], small["lnm_g"][i], alpha,
                                               f"l{i}_s5", ex)
        arrived.update(zip(keys, [pending[k] for k in keys] if local else extra))
        pending = {(k, j): val for k, val in mix.items()}
        for k, val in sg.items():
            gs[k][i if k in ("lnm_g", "lnm_b") else j] = val
        def last(dg, db):
            gs["ln1_g"][0], gs["ln1_b"][0] = dg, db
            packed = _pack_flat([jnp.stack(gs[k]) for k in SMALL]).astype(BF16)
            return (("ffn1_w_out", 0), ("ffn1_w_in", 0)), packed

        dh, dw_in, dw_out, gs["ln1_g"][i], gs["ln1_b"][i], got = _ffn_bwd(
            dh, s1, W[("ffn1_w_in", i)], W[("ffn1_w_out", i)], small["ln1_g"][i], alpha, f"l{i}_ffn1",
            None if local else pending, last if (i == 0 and not local) else None)
        arrived.update(pending if local else got)
        pending = {("ffn1_w_in", i): dw_in, ("ffn1_w_out", i): dw_out}
    if local:
        arrived.update(pending)
    return loss_part, dh, arrived, {k: jnp.stack(v) for k, v in gs.items()}


def kernel(x, ffn1_w_in, ffn1_w_out, ln1_g, ln1_b, lnm_g, lnm_b, ffn2_w_in, ffn2_w_out, ln2_g, ln2_b, fox_w_in, fox_b_f, fox_w_o, s5_a_re, s5_a_im, s5_log_dt, s5_b_re, s5_b_im, s5_c_re, s5_c_im, s5_d, s5_w_out, loss_target, m_ffn1_w_in, m_ffn1_w_out, m_ln1_g, m_ln1_b, m_lnm_g, m_lnm_b, m_ffn2_w_in, m_ffn2_w_out, m_ln2_g, m_ln2_b, m_fox_w_in, m_fox_b_f, m_fox_w_o, m_s5_a_re, m_s5_a_im, m_s5_log_dt, m_s5_b_re, m_s5_b_im, m_s5_c_re, m_s5_c_im, m_s5_d, m_s5_w_out, v_ffn1_w_in, v_ffn1_w_out, v_ln1_g, v_ln1_b, v_lnm_g, v_lnm_b, v_ffn2_w_in, v_ffn2_w_out, v_ln2_g, v_ln2_b, v_fox_w_in, v_fox_b_f, v_fox_w_o, v_s5_a_re, v_s5_a_im, v_s5_log_dt, v_s5_b_re, v_s5_b_im, v_s5_c_re, v_s5_c_im, v_s5_d, v_s5_w_out):
    args = dict(locals())
    w = {k: args[k] for k in WEIGHTS}
    m = {k: args["m_" + k] for k in WEIGHTS}
    v = {k: args["v_" + k] for k in WEIGHTS}
    small = {k: w[k] for k in SMALL}

    turned = lambda d: {k: jnp.swapaxes(a, 1, 2) if k in ("ffn1_w_in", "ffn2_w_in") else a for k, a in d.items()}
    w, m, v = turned(w), turned(m), turned(v)
    wb = {k: w[k].astype(BF16) for k in BIG}
    loss_part, dx, arrived, _ = _local_step(x[0], loss_target[0], small, lambda k, idx: wb[k][idx])
    loss = lax.psum(loss_part, ("x", "y", "c"))
    small_shapes = [w[k].shape for k in SMALL]
    g_small_flat = _sum8(arrived["small"], "sum_small_grads")

    grads, delta, new_m, new_v = {}, {}, {}, {}
    for k in BIG:
        parts = [arrived[(k, l)] for l in range(w[k].shape[0])]
        grads[k], delta[k], new_m[k], new_v[k] = _adamw_recv(parts, w[k], m[k], v[k], "adamw_" + k)
    pk = lambda d: _pack_flat([d[k] for k in SMALL])
    d_, m_, v_ = _adamw(pk(w), g_small_flat, pk(m), pk(v), "adamw_small")
    for dst, flat in ((grads, g_small_flat), (delta, d_), (new_m, m_), (new_v, v_)):
        dst.update(zip(SMALL, _unpack_flat(flat, small_shapes)))
    grads, delta, new_m, new_v = turned(grads), turned(delta), turned(new_m), turned(new_v)

    return (loss, dx[None], *[grads[k] for k in WEIGHTS], *[delta[k] for k in WEIGHTS],
            *[new_m[k] for k in WEIGHTS], *[new_v[k] for k in WEIGHTS])
```

```python
import functools
import math

import jax
import jax.numpy as jnp
from jax import lax
from jax.experimental import pallas as pl
from jax.experimental.pallas import tpu as pltpu

F32 = jnp.float32
BF16 = jnp.bfloat16

N_DEV = 8
HEAD_DIM = 64
S5_GROUP = 16
S5_STATE = 64
LANES = 128
SUBLANES = 8
S5_PART = 256
ATTN_TILE = 512
LN_EPS = 1e-5
NEG_INF = -1e30
LOG2E = 1.4426950408889634
ADAM_LR, ADAM_B1, ADAM_B2, ADAM_EPS, ADAM_WD, ADAM_STEP = 0.001, 0.9, 0.999, 1e-08, 0.01, 10
VMEM_LIMIT = 48 * 1024 * 1024
PACK_COLS = 1024

MESH = pl.DeviceIdType.MESH


def _tile(dim, pref, align=LANES):
    if dim <= pref:
        return dim
    t = (pref // align) * align
    while t >= align:
        if dim % t == 0:
            return t
        t -= align
    return dim


def _params(sem):
    return pltpu.CompilerParams(dimension_semantics=sem, vmem_limit_bytes=VMEM_LIMIT)


def _mm(a, b, *, ta=False, tb=False, out_dtype=F32, scale=None, add=None, add_scale=1.0,
        tm=512, tn=1408, tk=1408, name="mm"):
    if ta:
        K, M = a.shape
    else:
        M, K = a.shape
    if tb:
        N, K2 = b.shape
    else:
        K2, N = b.shape
    assert K == K2, (a.shape, b.shape, ta, tb)
    tm, tn, tk = _tile(M, tm), _tile(N, tn), _tile(K, tk)
    a_spec = pl.BlockSpec((tk, tm), lambda i, j, k: (k, i)) if ta else pl.BlockSpec((tm, tk), lambda i, j, k: (i, k))
    b_spec = pl.BlockSpec((tn, tk), lambda i, j, k: (j, k)) if tb else pl.BlockSpec((tk, tn), lambda i, j, k: (k, j))
    o_spec = pl.BlockSpec((tm, tn), lambda i, j, k: (i, j))
    return _mm_core(name, a, b, a_spec, b_spec, o_spec, jax.ShapeDtypeStruct((M, N), out_dtype),
                    (M // tm, N // tn, K // tk), (tm, tn), ta, tb, scale, add, add_scale)


def _mm_core(name, a, b, a_spec, b_spec, o_spec, out_shape, grid, acc_shape, ta, tb, scale=None, add=None,
             add_scale=1.0, ex=None):
    nk = grid[2]

    def body(*refs):
        if add is None:
            a_ref, b_ref, o_ref, acc = refs
            add_ref = None
        else:
            a_ref, b_ref, add_ref, o_ref, acc = refs
        k = pl.program_id(2)

        @pl.when(k == 0)
        def _():
            acc[...] = jnp.zeros_like(acc)

        dims = (((0 if ta else 1,), (1 if tb else 0,)), ((), ()))
        acc[...] += lax.dot_general(a_ref[...].astype(BF16), b_ref[...].astype(BF16), dims,
                                    preferred_element_type=F32)

        @pl.when(k == nk - 1)
        def _():
            r = acc[...]
            if scale is not None:
                r = r * scale
            if add_ref is not None:
                r = r + add_scale * add_ref[...]
            o_ref[...] = r.astype(out_shape.dtype)

    in_specs = [a_spec, b_spec]
    args = [a, b]
    if add is not None:
        in_specs.append(o_spec)
        args.append(add)
    (res,), extra = _call(body, name=name, out_shape=(out_shape,), grid=grid, in_specs=in_specs, out_specs=(o_spec,),
                          scratch_shapes=[pltpu.VMEM(acc_shape, F32)], args=args,
                          semantics=("parallel", "parallel", "arbitrary"), ex=ex)
    return res if ex is None else (res, extra)


def _ffn_h(x, wt, name, ex=None):
    S, D = x.shape
    n, c, _ = wt.shape
    nb = n // 2
    tm = _tile(S, 1024, SUBLANES)
    dims = (((1,), (1,)), ((), ()))

    def body(x_ref, wg_ref, wu_ref, h_ref, a_ref):
        xb = x_ref[...]
        g = lax.dot_general(xb, wg_ref[...], dims, preferred_element_type=F32).astype(BF16)
        u = lax.dot_general(xb, wu_ref[...], dims, preferred_element_type=F32).astype(BF16)
        h_ref[0] = g
        h_ref[1] = u
        g = g.astype(F32)
        a_ref[...] = (g * _sigmoid(g) * u.astype(F32)).astype(BF16)

    (h, a4), extra = _call(
        body, name=name,
        out_shape=(jax.ShapeDtypeStruct((2, nb, S, c), BF16), jax.ShapeDtypeStruct((nb, S, c), BF16)),
        grid=(S // tm, nb),
        in_specs=[pl.BlockSpec((tm, D), lambda i, j: (i, 0)), pl.BlockSpec((None, c, D), lambda i, j: (j, 0, 0)),
                  pl.BlockSpec((None, c, D), lambda i, j: (j + nb, 0, 0))],
        out_specs=(pl.BlockSpec((2, None, tm, c), lambda i, j: (0, j, i, 0)),
                   pl.BlockSpec((None, tm, c), lambda i, j: (j, i, 0))),
        scratch_shapes=[], args=(x, wt, wt), semantics=("parallel", "parallel"), ex=ex)
    return h.reshape(n, S, c), a4, extra


def _ffn_y_ln(a4, wo4, x, g, b, alpha, s, name):
    nb, S, c = a4.shape
    D = wo4.shape[-1]
    tm = _tile(S, 512, SUBLANES)

    def body(a_ref, w_ref, x_ref, g_ref, b_ref, o_ref, ob_ref, xh_ref, r_ref, acc):
        k = pl.program_id(1)

        @pl.when(k == 0)
        def _():
            acc[...] = jnp.zeros_like(acc)

        acc[...] += jnp.dot(a_ref[...], w_ref[...], preferred_element_type=F32)

        @pl.when(k == nb - 1)
        def _():
            z = alpha * x_ref[...] + s * acc[...]
            mu = jnp.mean(z, axis=-1, keepdims=True)
            zc = z - mu
            rstd = lax.rsqrt(jnp.mean(zc * zc, axis=-1, keepdims=True) + LN_EPS)
            xh = zc * rstd
            xh_ref[...] = xh
            r_ref[...] = rstd
            out = xh * g_ref[...] + b_ref[...]
            o_ref[...] = out
            ob_ref[...] = out.astype(BF16)

    row = pl.BlockSpec((tm, D), lambda i, k: (i, 0))
    vec = pl.BlockSpec((1, D), lambda i, k: (0, 0))
    return pl.pallas_call(
        body, name=name,
        out_shape=(jax.ShapeDtypeStruct((S, D), F32), jax.ShapeDtypeStruct((S, D), BF16),
                   jax.ShapeDtypeStruct((S, D), F32), jax.ShapeDtypeStruct((S, 1), F32)),
        grid=(S // tm, nb),
        in_specs=[pl.BlockSpec((None, tm, c), lambda i, k: (k, i, 0)), pl.BlockSpec((None, c, D), lambda i, k: (k, 0, 0)),
                  row, vec, vec],
        out_specs=(row, row, row, pl.BlockSpec((tm, 1), lambda i, k: (i, 0))),
        scratch_shapes=[pltpu.VMEM((tm, D), F32)], compiler_params=_params(("parallel", "arbitrary")),
    )(a4, wo4, x, g.reshape(1, D), b.reshape(1, D))


def _ffn_dwout(a4, dz, scale, name, ex=None):
    nb, S, c = a4.shape
    D = dz.shape[1]
    tk, tn = _tile(S, 1024, SUBLANES), _tile(D, 1024)
    return _mm_core(name, a4, dz, pl.BlockSpec((None, tk, c), lambda i, j, k: (i, k, 0)),
                    pl.BlockSpec((tk, tn), lambda i, j, k: (k, j)),
                    pl.BlockSpec((None, c, tn), lambda i, j, k: (i, 0, j)),
                    jax.ShapeDtypeStruct((nb, c, D), BF16), (nb, D // tn, S // tk), (c, tn), True, False, scale,
                    ex=ex)


def _ffn_dh(dz, wo4, h8, scale, name, ex=None):
    S, D = dz.shape
    nb, c = wo4.shape[0], wo4.shape[1]
    tm = _tile(S, 512, SUBLANES)

    def body(dz_ref, w_ref, h_ref, d_ref):
        da = lax.dot_general(dz_ref[...].astype(BF16), w_ref[...], (((1,), (1,)), ((), ())),
                             preferred_element_type=F32) * scale
        g = h_ref[0].astype(F32)
        u = h_ref[1].astype(F32)
        sg = _sigmoid(g)
        silu = g * sg
        d_ref[0] = (da * u * (sg + silu * (1.0 - sg))).astype(BF16)
        d_ref[1] = (da * silu).astype(BF16)

    pair = pl.BlockSpec((2, None, tm, c), lambda i, j: (0, j, i, 0))
    (dh,), extra = _call(
        body, name=name, out_shape=(jax.ShapeDtypeStruct((2, nb, S, c), BF16),), grid=(S // tm, nb),
        in_specs=[pl.BlockSpec((tm, D), lambda i, j: (i, 0)), pl.BlockSpec((None, c, D), lambda i, j: (j, 0, 0)),
                  pair],
        out_specs=(pair,), scratch_shapes=[], args=(dz, wo4, h8.reshape(2, nb, S, c)),
        semantics=("parallel", "parallel"), ex=ex)
    dh = dh.reshape(2 * nb, S, c)
    return dh if ex is None else (dh, extra)


def _ffn_dwin(x, dh8, name, ex=None):
    S, D = x.shape
    n, _, c = dh8.shape
    return _mm_core(name, dh8, x, pl.BlockSpec((None, S, c), lambda i, j, k: (i, 0, 0)),
                    pl.BlockSpec((S, D), lambda i, j, k: (0, 0)),
                    pl.BlockSpec((None, c, D), lambda i, j, k: (i, 0, 0)),
                    jax.ShapeDtypeStruct((n, c, D), BF16), (n, 1, 1), (c, D), True, False, ex=ex)


def _ffn_dx(dh8, wt, dz, alpha, name, ex=None):
    n, S, c = dh8.shape
    D = wt.shape[2]
    tm, tn = _tile(S, 1024, SUBLANES), _tile(D, 1024)
    return _mm_core(name, dh8, wt, pl.BlockSpec((None, tm, c), lambda i, j, k: (k, i, 0)),
                    pl.BlockSpec((None, c, tn), lambda i, j, k: (k, 0, j)),
                    pl.BlockSpec((tm, tn), lambda i, j, k: (i, j)),
                    jax.ShapeDtypeStruct((S, D), F32), (S // tm, D // tn, n), (tm, tn), False, False,
                    None, dz, alpha, ex=ex)


def _res_ln_fwd(x, y, g, b, alpha, s, name):
    S, D = x.shape
    tm = _tile(S, 256, SUBLANES)

    def body(x_ref, y_ref, g_ref, b_ref, o_ref, ob_ref, xh_ref, r_ref):
        z = alpha * x_ref[...] + s * y_ref[...]
        mu = jnp.mean(z, axis=-1, keepdims=True)
        zc = z - mu
        var = jnp.mean(zc * zc, axis=-1, keepdims=True)
        rstd = lax.rsqrt(var + LN_EPS)
        xh = zc * rstd
        xh_ref[...] = xh
        r_ref[...] = rstd
        out = xh * g_ref[...] + b_ref[...]
        o_ref[...] = out
        ob_ref[...] = out.astype(BF16)

    row = pl.BlockSpec((tm, D), lambda i: (i, 0))
    vec = pl.BlockSpec((1, D), lambda i: (0, 0))
    return pl.pallas_call(
        body, name=name,
        out_shape=(jax.ShapeDtypeStruct((S, D), F32), jax.ShapeDtypeStruct((S, D), BF16),
                   jax.ShapeDtypeStruct((S, D), F32), jax.ShapeDtypeStruct((S, 1), F32)),
        grid=(S // tm,), in_specs=[row, row, vec, vec],
        out_specs=(row, row, row, pl.BlockSpec((tm, 1), lambda i: (i, 0))),
        compiler_params=_params(("parallel",)),
    )(x, y, g.reshape(1, D), b.reshape(1, D))


def _ln_bwd(dout, xh, rstd, g, name):
    S, D = dout.shape
    tm = _tile(S, 256, SUBLANES)

    def body(d_ref, xh_ref, r_ref, g_ref, dz_ref, dg_ref, db_ref):
        i = pl.program_id(0)

        @pl.when(i == 0)
        def _():
            dg_ref[...] = jnp.zeros_like(dg_ref)
            db_ref[...] = jnp.zeros_like(db_ref)

        d = d_ref[...]
        xhv = xh_ref[...]
        dxh = d * g_ref[...]
        m1 = jnp.mean(dxh, axis=-1, keepdims=True)
        m2 = jnp.mean(dxh * xhv, axis=-1, keepdims=True)
        dz_ref[...] = r_ref[...] * (dxh - m1 - xhv * m2)
        dg_ref[...] += jnp.sum(d * xhv, axis=0, keepdims=True)
        db_ref[...] += jnp.sum(d, axis=0, keepdims=True)

    row = pl.BlockSpec((tm, D), lambda i: (i, 0))
    vec = pl.BlockSpec((1, D), lambda i: (0, 0))
    dz, dg, db = pl.pallas_call(
        body, name=name,
        out_shape=(jax.ShapeDtypeStruct((S, D), F32), jax.ShapeDtypeStruct((1, D), F32),
                   jax.ShapeDtypeStruct((1, D), F32)),
        grid=(S // tm,), in_specs=[row, row, pl.BlockSpec((tm, 1), lambda i: (i, 0)), vec],
        out_specs=(row, vec, vec),
        compiler_params=_params(("arbitrary",)),
    )(dout, xh, rstd, g.reshape(1, D))
    return dz, dg[0], db[0]


def _sigmoid(x):
    return 0.5 * jnp.tanh(0.5 * x) + 0.5


def _glu_fwd(vg, name):
    S, D2 = vg.shape
    D = D2 // 2
    tm = _tile(S, 512, SUBLANES)

    def body(v_ref, g_ref, o_ref):
        o_ref[...] = v_ref[...] * _sigmoid(g_ref[...])

    return pl.pallas_call(
        body, name=name, out_shape=jax.ShapeDtypeStruct((S, D), F32), grid=(S // tm,),
        in_specs=[pl.BlockSpec((tm, D), lambda i: (i, 0)), pl.BlockSpec((tm, D), lambda i: (i, 1))],
        out_specs=pl.BlockSpec((tm, D), lambda i: (i, 0)),
        compiler_params=_params(("parallel",)),
    )(vg, vg)


def _glu_bwd(vg, dm, name):
    S, D2 = vg.shape
    D = D2 // 2
    tm = _tile(S, 512, SUBLANES)

    def body(v_ref, g_ref, dm_ref, dv_ref, dg_ref):
        sg = _sigmoid(g_ref[...])
        d = dm_ref[...]
        dv_ref[...] = (d * sg).astype(BF16)
        dg_ref[...] = (d * v_ref[...] * sg * (1.0 - sg)).astype(BF16)

    blk = pl.BlockSpec((tm, D), lambda i: (i, 0))
    dv, dg = pl.pallas_call(
        body, name=name,
        out_shape=(jax.ShapeDtypeStruct((S, D), BF16), jax.ShapeDtypeStruct((S, D), BF16)),
        grid=(S // tm,), in_specs=[blk, pl.BlockSpec((tm, D), lambda i: (i, 1)), blk],
        out_specs=(blk, blk), compiler_params=_params(("parallel",)),
    )(vg, vg, dm)
    return jnp.concatenate([dv, dg], axis=1)


def _loss_fwd_bwd(y, target):
    S, D = y.shape
    tm = _tile(S, 256, SUBLANES)

    def body(y_ref, t_ref, dy_ref, l_ref):
        i = pl.program_id(0)

        @pl.when(i == 0)
        def _():
            l_ref[...] = jnp.zeros_like(l_ref)

        e = y_ref[...] - t_ref[...]
        dy_ref[...] = e * (1.0 / D)
        l_ref[...] += jnp.sum(e * e, axis=0, keepdims=True) * (0.5 / D)

    row = pl.BlockSpec((tm, D), lambda i: (i, 0))
    dy, part = pl.pallas_call(
        body, name="loss", out_shape=(jax.ShapeDtypeStruct((S, D), F32), jax.ShapeDtypeStruct((1, D), F32)),
        grid=(S // tm,), in_specs=[row, row], out_specs=(row, pl.BlockSpec((1, D), lambda i: (0, 0))),
        compiler_params=_params(("arbitrary",)),
    )(y, target)
    return dy, jnp.sum(part)


def _tri(n, lower):
    r = lax.broadcasted_iota(jnp.int32, (n, n), 0)
    c = lax.broadcasted_iota(jnp.int32, (n, n), 1)
    return jnp.where((c <= r) if lower else (c >= r), 1.0, 0.0)


def _fox_gate_fwd(fl, bf):
    S, W = fl.shape
    tm = _tile(S, 256, SUBLANES)

    def body(fl_ref, b_ref, c_ref, carry):
        i = pl.program_id(0)

        @pl.when(i == 0)
        def _():
            carry[...] = jnp.zeros_like(carry)

        x = fl_ref[...] + b_ref[...]
        lf = jnp.minimum(x, 0.0) - jnp.log(1.0 + jnp.exp(-jnp.abs(x)))
        c_ref[...] = jnp.dot(_tri(tm, True), lf, precision=lax.Precision.HIGHEST,
                             preferred_element_type=F32) + carry[...]
        carry[...] += jnp.sum(lf, axis=0, keepdims=True)

    blk = pl.BlockSpec((tm, W), lambda i: (i, 0))
    return pl.pallas_call(
        body, name="fox_gate_fwd", out_shape=jax.ShapeDtypeStruct((S, W), F32), grid=(S // tm,),
        in_specs=[blk, pl.BlockSpec((1, W), lambda i: (0, 0))], out_specs=blk,
        scratch_shapes=[pltpu.VMEM((1, W), F32)], compiler_params=_params(("arbitrary",)),
    )(fl, bf)


def _fox_gate_bwd(dcum, fl, bf):
    S, W = fl.shape
    tm = _tile(S, 256, SUBLANES)
    nb = S // tm

    def body(dc_ref, fl_ref, b_ref, dfl_ref, db_ref, carry):
        i = pl.program_id(0)

        @pl.when(i == 0)
        def _():
            carry[...] = jnp.zeros_like(carry)
            db_ref[...] = jnp.zeros_like(db_ref)

        dc = dc_ref[...]
        r = jnp.dot(_tri(tm, False), dc, precision=lax.Precision.HIGHEST, preferred_element_type=F32) + carry[...]
        carry[...] += jnp.sum(dc, axis=0, keepdims=True)
        x = fl_ref[...] + b_ref[...]
        e = jnp.exp(-jnp.abs(x))
        dfl = r * jnp.where(x >= 0, e, 1.0) / (1.0 + e)
        dfl_ref[...] = dfl
        db_ref[...] += jnp.sum(dfl, axis=0, keepdims=True)

    blk = pl.BlockSpec((tm, W), lambda i: (nb - 1 - i, 0))
    vec = pl.BlockSpec((1, W), lambda i: (0, 0))
    return pl.pallas_call(
        body, name="fox_gate_bwd",
        out_shape=(jax.ShapeDtypeStruct((S, W), F32), jax.ShapeDtypeStruct((1, W), F32)),
        grid=(nb,), in_specs=[blk, blk, vec], out_specs=(blk, vec),
        scratch_shapes=[pltpu.VMEM((1, W), F32)], compiler_params=_params(("arbitrary",)),
    )(dcum, fl, bf)


def _causal(t):
    row = lax.broadcasted_iota(jnp.int32, (t, t), 0)
    col = lax.broadcasted_iota(jnp.int32, (t, t), 1)
    return col <= row


def _first_head(shape):
    return lax.broadcasted_iota(jnp.int32, shape, len(shape) - 1) < HEAD_DIM


def _split3(x):
    hi = x.astype(BF16).astype(F32)
    r = x - hi
    mid = r.astype(BF16).astype(F32)
    return hi, mid, (r - mid).astype(BF16).astype(F32)


def _bias_lanes(c, query):
    lane = lax.broadcasted_iota(jnp.int32, (c.shape[0], LANES), 1)
    hi, mid, lo = _split3(c)
    if query:
        out = jnp.where(lane == 0, hi, jnp.where(lane == 1, mid, jnp.where(lane == 2, lo,
                                                                          jnp.where(lane < 6, 1.0, 0.0))))
    else:
        out = jnp.where(lane < 3, 1.0, jnp.where(lane == 3, -hi, jnp.where(lane == 4, -mid,
                                                                          jnp.where(lane == 5, -lo, 0.0))))
    return out.astype(BF16)


def _flash_fwd(proj, cum2, ex=None):
    S = proj.shape[0]
    D = (proj.shape[1] - LANES) // 3
    HP = D // LANES
    t = _tile(S, ATTN_TILE)
    nq = S // t
    scale = 1.0 / math.sqrt(HEAD_DIM)

    def body(q_ref, k_ref, v_ref, cq_ref, ck_ref, o_ref, lse_ref, kx, vb):
        qi = pl.program_id(1)

        @pl.when(qi == 0)
        def _():
            def prep(kb, c):
                sl = pl.ds(pl.multiple_of(kb * t, t), t)
                k16 = k_ref[sl, :].astype(BF16)
                ckv = ck_ref[0, sl, :]
                for a in range(2):
                    kx[a, sl, 0:LANES] = k16
                    kx[a, sl, LANES:2 * LANES] = _bias_lanes(ckv[:, a:a + 1], False)
                vb[sl, :] = v_ref[sl, :].astype(BF16)
                return c

            lax.fori_loop(0, nq, prep, 0)

        first = _first_head((t, LANES))
        qf = q_ref[...] * (scale * LOG2E)
        cqv = cq_ref[0]
        qx = [jnp.concatenate([jnp.where(keep, qf, 0.0).astype(BF16), _bias_lanes(cqv[:, a:a + 1], True)], axis=1)
              for a, keep in enumerate((first, jnp.logical_not(first)))]

        def block(ki, carry, masked):
            m_old, l_old, acc = carry
            sl = pl.ds(pl.multiple_of(ki * t, t), t)
            vv = vb[sl, :]
            m_new, l_new, corr, pv = [], [], [], []
            for a in range(2):
                s = lax.dot_general(qx[a], kx[a, sl, :], (((1,), (1,)), ((), ())), preferred_element_type=F32)
                if masked:
                    s = jnp.where(_causal(t), s, NEG_INF)
                m_a = jnp.maximum(m_old[a], jnp.max(s, axis=1, keepdims=True))
                p = jnp.exp2(s - m_a)
                c_a = jnp.exp2(m_old[a] - m_a)
                m_new.append(m_a)
                corr.append(c_a)
                l_new.append(c_a * l_old[a] + jnp.sum(p, axis=1, keepdims=True))
                pv.append(jnp.dot(p.astype(BF16), vv, preferred_element_type=F32))
            acc = jnp.where(first, corr[0] * acc + pv[0], corr[1] * acc + pv[1])
            return tuple(m_new), tuple(l_new), acc

        neg = jnp.full((t, 1), NEG_INF, F32)
        zero = jnp.zeros((t, 1), F32)
        carry = lax.fori_loop(0, qi, lambda ki, c: block(ki, c, False),
                              ((neg, neg), (zero, zero), jnp.zeros((t, LANES), F32)))
        m, l, acc = block(qi, carry, True)
        o_ref[...] = acc / jnp.where(first, l[0], l[1])
        lse_ref[0, :, 0:1] = m[0] + jnp.log2(l[0])
        lse_ref[0, :, 1:2] = m[1] + jnp.log2(l[1])

    qblk = pl.BlockSpec((t, LANES), lambda h, i: (i, h))
    r2 = pl.BlockSpec((1, t, 2), lambda h, i: (h, i, 0))
    (o, lse), extra = _call(
        body, name="fox_attn_fwd",
        out_shape=(jax.ShapeDtypeStruct((S, D), F32), jax.ShapeDtypeStruct((HP, S, 2), F32)),
        grid=(HP, nq),
        in_specs=[qblk, pl.BlockSpec((S, LANES), lambda h, i: (0, HP + h)),
                  pl.BlockSpec((S, LANES), lambda h, i: (0, 2 * HP + h)), r2,
                  pl.BlockSpec((1, S, 2), lambda h, i: (h, 0, 0))],
        out_specs=(qblk, r2),
        scratch_shapes=[pltpu.VMEM((2, S, 2 * LANES), BF16), pltpu.VMEM((S, LANES), BF16)],
        args=(proj, proj, proj, cum2, cum2), semantics=("parallel", "arbitrary"), ex=ex)
    return o, lse, extra


def _flash_bwd(proj, rows, o, do, ex=None):
    S = proj.shape[0]
    D = (proj.shape[1] - LANES) // 3
    HP = D // LANES
    t = _tile(S, ATTN_TILE)
    nb = S // t
    scale = 1.0 / math.sqrt(HEAD_DIM)

    def body(q_ref, k_ref, v_ref, rows_ref, o_ref, do_ref, dq_ref, dk_ref, dv_ref, dcq_ref, dck_ref,
             q_s, do_s, dl_s, dq_acc, dk_acc, dv_acc, dc_acc):
        kb = pl.program_id(1)

        @pl.when(kb == 0)
        def _():
            dq_acc[...] = jnp.zeros_like(dq_acc)
            dcq_ref[...] = jnp.zeros_like(dcq_ref)

            def prep(qb, c):
                sl = pl.ds(pl.multiple_of(qb * t, t), t)
                first = _first_head((t, LANES))
                qf = q_ref[sl, :] * (scale * LOG2E)
                dof = do_ref[sl, :]
                prod = dof * o_ref[sl, :]
                rv = rows_ref[0, sl, :]
                for a, keep in enumerate((first, jnp.logical_not(first))):
                    q_s[a, sl, 0:LANES] = jnp.where(keep, qf, 0.0).astype(BF16)
                    q_s[a, sl, LANES:2 * LANES] = _bias_lanes(rv[:, a:a + 1], True)
                    do_s[a, sl, :] = jnp.where(keep, dof, 0.0).astype(BF16)
                    dl_s[sl, a:a + 1] = jnp.sum(jnp.where(keep, prod, 0.0), axis=1, keepdims=True)
                return c

            lax.fori_loop(0, nb, prep, 0)

        dk_acc[...] = jnp.zeros_like(dk_acc)
        dv_acc[...] = jnp.zeros_like(dv_acc)
        dc_acc[...] = jnp.zeros_like(dc_acc)
        first = _first_head((t, LANES))
        kf = k_ref[...]
        kk = kf.astype(BF16)
        k_own = (jnp.where(first, kf, 0.0).astype(BF16), jnp.where(first, 0.0, kf).astype(BF16))
        ckv = rows_ref[0, pl.ds(pl.multiple_of(kb * t, t), t), :]
        kx = [jnp.concatenate([kk, _bias_lanes(ckv[:, a:a + 1], False)], axis=1) for a in range(2)]
        vv = v_ref[...].astype(BF16)

        def block(qb, masked):
            sl = pl.ds(pl.multiple_of(qb * t, t), t)
            rv = rows_ref[0, sl, :]
            dlv = dl_s[sl, :]
            dq_new = dq_acc[sl, :]
            for a in range(2):
                dob = do_s[a, sl, :]
                s = lax.dot_general(q_s[a, sl, :], kx[a], (((1,), (1,)), ((), ())), preferred_element_type=F32)
                if masked:
                    s = jnp.where(_causal(t), s, NEG_INF)
                p = jnp.exp2(s - rv[:, 2 + a:3 + a])
                dv_acc[...] += lax.dot_general(p.astype(BF16), dob, (((0,), (0,)), ((), ())),
                                               preferred_element_type=F32)
                dp = lax.dot_general(dob, vv, (((1,), (1,)), ((), ())), preferred_element_type=F32)
                ds = p * (dp - dlv[:, a:a + 1])
                dsb = ds.astype(BF16)
                dk_acc[...] += lax.dot_general(dsb, q_s[a, sl, 0:LANES], (((0,), (0,)), ((), ())),
                                               preferred_element_type=F32)
                dq_new = dq_new + jnp.dot(dsb, k_own[a], preferred_element_type=F32) * scale
                dcq_ref[0, sl, a:a + 1] += jnp.sum(ds, axis=1, keepdims=True)
                dc_acc[a:a + 1, :] -= jnp.sum(ds, axis=0, keepdims=True)
            dq_acc[sl, :] = dq_new

        block(kb, True)

        def rest(qb, c):
            block(qb, False)
            return c

        lax.fori_loop(kb + 1, nb, rest, 0)
        dk_ref[...] = (dk_acc[...] * (1.0 / LOG2E)).astype(BF16)
        dv_ref[...] = dv_acc[...].astype(BF16)
        dck_ref[0, 0] = dc_acc[0:1, :]
        dck_ref[1, 0] = dc_acc[1:2, :]

        @pl.when(kb == nb - 1)
        def _():
            dq_ref[...] = dq_acc[...].astype(BF16)

    full = lambda c0: pl.BlockSpec((S, LANES), lambda h, j, c0=c0: (0, c0 + h))
    blk = lambda c0: pl.BlockSpec((t, LANES), lambda h, j, c0=c0: (j, c0 + h))
    f32 = lambda *s: jax.ShapeDtypeStruct(s, F32)
    b16 = jax.ShapeDtypeStruct((S, D), BF16)
    outs, extra = _call(
        body, name="fox_attn_bwd",
        out_shape=(b16, b16, b16, f32(HP, S, 2), f32(2 * HP, nb, 1, t)),
        grid=(HP, nb),
        in_specs=[full(0), blk(HP), blk(2 * HP), pl.BlockSpec((1, S, 4), lambda h, j: (h, 0, 0)), full(0), full(0)],
        out_specs=(full(0), blk(0), blk(0), pl.BlockSpec((1, S, 2), lambda h, j: (h, 0, 0)),
                   pl.BlockSpec((2, 1, 1, t), lambda h, j: (h, j, 0, 0))),
        scratch_shapes=[pltpu.VMEM((2, S, 2 * LANES), BF16), pltpu.VMEM((2, S, LANES), BF16),
                        pltpu.VMEM((S, 2), F32), pltpu.VMEM((S, LANES), F32), pltpu.VMEM((t, LANES), F32),
                        pltpu.VMEM((t, LANES), F32), pltpu.VMEM((2, t), F32)],
        args=(proj, proj, proj, rows, o, do), semantics=("parallel", "arbitrary"), ex=ex)
    return (*outs, extra)


def _s5_consts(T):
    rows = T * SUBLANES
    rr = lax.broadcasted_iota(jnp.int32, (rows, T), 0)
    tt = lax.broadcasted_iota(jnp.int32, (rows, T), 1)
    rep = jnp.where(rr // SUBLANES == tt, 1.0, 0.0).astype(BF16)
    r2 = lax.broadcasted_iota(jnp.int32, (rows, S5_PART), 0)
    c2 = lax.broadcasted_iota(jnp.int32, (rows, S5_PART), 1)
    mask = (c2 // (S5_PART // SUBLANES)) == (r2 % SUBLANES)
    return rep, mask


def _gelu(y):
    c = math.sqrt(2.0 / math.pi)
    return 0.5 * y * (1.0 + jnp.tanh(c * (y + 0.044715 * y * y * y)))


def _gelu_grad(y):
    c = math.sqrt(2.0 / math.pi)
    th = jnp.tanh(c * (y + 0.044715 * y * y * y))
    return 0.5 * (1.0 + th) + 0.5 * y * (1.0 - th * th) * c * (1.0 + 3.0 * 0.044715 * y * y)


def _s5_fwd(x, rmat, cmat, lam, dskip, ex=None):
    S, D = x.shape
    NQ = D // S5_PART
    T = _tile(S, 128, SUBLANES)
    rows = T * SUBLANES

    def body(x_ref, r_ref, c_ref, lam_ref, d_ref, y_ref, yg_ref, h_ref, bu_s, carry):
        i = pl.program_id(0)

        @pl.when(i == 0)
        def _():
            carry[...] = jnp.zeros_like(carry)

        rep, mask = _s5_consts(T)
        cols = [pl.ds(q * S5_PART, S5_PART) for q in range(NQ)]
        for q in range(NQ):
            xrep = jnp.dot(rep, x_ref[:, cols[q]].astype(BF16), preferred_element_type=F32)
            lx = jnp.where(mask, xrep, 0.0).astype(BF16)
            bu_s[q] = jnp.dot(lx, r_ref[q], preferred_element_type=F32)
        lam_v = [(lam_ref[q, :, 0:LANES], lam_ref[q, :, LANES:2 * LANES]) for q in range(NQ)]

        def step(t, c):
            o = pl.multiple_of(t * SUBLANES, SUBLANES)
            new = []
            for q in range(NQ):
                hr, hi = c[q]
                ar, ai = lam_v[q]
                sl = bu_s[q, pl.ds(o, SUBLANES), :]
                nhr = ar * hr - ai * hi + sl[:, 0:LANES]
                nhi = ar * hi + ai * hr + sl[:, LANES:2 * LANES]
                h_ref[q, pl.ds(o, SUBLANES), 0:LANES] = nhr
                h_ref[q, pl.ds(o, SUBLANES), LANES:2 * LANES] = nhi
                new.append((nhr, nhi))
            return tuple(new)

        fin = lax.fori_loop(0, T, step,
                            tuple((carry[q, :, 0:LANES], carry[q, :, LANES:2 * LANES]) for q in range(NQ)))
        for q in range(NQ):
            carry[q, :, 0:LANES] = fin[q][0]
            carry[q, :, LANES:2 * LANES] = fin[q][1]
            z = jnp.dot(h_ref[q].astype(BF16), c_ref[q], preferred_element_type=F32)
            z = jnp.where(mask, z, 0.0)
            y = jnp.sum(z.reshape(T, SUBLANES, S5_PART), axis=1) + d_ref[:, cols[q]] * x_ref[:, cols[q]]
            y_ref[:, cols[q]] = y
            yg_ref[:, cols[q]] = _gelu(y).astype(BF16)

    xs = pl.BlockSpec((T, D), lambda i: (i, 0))
    ms = pl.BlockSpec((NQ, S5_PART, S5_PART), lambda i: (0, 0, 0))
    outs, extra = _call(
        body, name="s5_scan_fwd",
        out_shape=(jax.ShapeDtypeStruct((S, D), F32), jax.ShapeDtypeStruct((S, D), BF16),
                   jax.ShapeDtypeStruct((NQ, S * SUBLANES, S5_PART), F32)),
        grid=(S // T,),
        in_specs=[xs, ms, ms, pl.BlockSpec((NQ, SUBLANES, S5_PART), lambda i: (0, 0, 0)),
                  pl.BlockSpec((1, D), lambda i: (0, 0))],
        out_specs=(xs, xs, pl.BlockSpec((NQ, rows, S5_PART), lambda i: (0, i, 0))),
        scratch_shapes=[pltpu.VMEM((NQ, rows, S5_PART), F32), pltpu.VMEM((NQ, SUBLANES, S5_PART), F32)],
        args=(x, rmat, cmat, lam, dskip), semantics=("arbitrary",), ex=ex)
    return (*outs, extra)


def _s5_bwd(x, y, dyg, hs, rmat, cmat, lam, dskip, res, res_scale, ex=None):
    S, D = x.shape
    NQ = D // S5_PART
    T = _tile(S, 128, SUBLANES)
    nb = S // T
    rows = T * SUBLANES

    def body(x_ref, y_ref, dyg_ref, res_ref, h_ref, hp_ref, r_ref, c_ref, lam_ref, d_ref,
             dx_ref, dr_ref, dc_ref, dlam_ref, dd_ref, dh_s, g_s, hs_s, carry):
        i = pl.program_id(0)

        @pl.when(i == 0)
        def _():
            carry[...] = jnp.zeros_like(carry)
            dr_ref[...] = jnp.zeros_like(dr_ref)
            dc_ref[...] = jnp.zeros_like(dc_ref)
            dlam_ref[...] = jnp.zeros_like(dlam_ref)
            dd_ref[...] = jnp.zeros_like(dd_ref)

        rep, mask = _s5_consts(T)
        cols = [pl.ds(q * S5_PART, S5_PART) for q in range(NQ)]
        dys, ldys = [], []
        for q in range(NQ):
            dy = dyg_ref[:, cols[q]] * _gelu_grad(y_ref[:, cols[q]])
            dyrep = jnp.dot(rep, dy.astype(BF16), preferred_element_type=F32)
            ldy = jnp.where(mask, dyrep, 0.0).astype(BF16)
            dh_s[q] = lax.dot_general(ldy, c_ref[q], (((1,), (1,)), ((), ())), preferred_element_type=F32)
            dys.append(dy)
            ldys.append(ldy)
        lam_v = [(lam_ref[q, :, 0:LANES], lam_ref[q, :, LANES:2 * LANES]) for q in range(NQ)]

        def step(n, c):
            o = pl.multiple_of((T - 1 - n) * SUBLANES, SUBLANES)
            new = []
            for q in range(NQ):
                gr, gi = c[q]
                ar, ai = lam_v[q]
                sl = dh_s[q, pl.ds(o, SUBLANES), :]
                ngr = sl[:, 0:LANES] + ar * gr + ai * gi
                ngi = sl[:, LANES:2 * LANES] - ai * gr + ar * gi
                g_s[q, pl.ds(o, SUBLANES), 0:LANES] = ngr
                g_s[q, pl.ds(o, SUBLANES), LANES:2 * LANES] = ngi
                new.append((ngr, ngi))
            return tuple(new)

        fin = lax.fori_loop(0, T, step,
                            tuple((carry[q, :, 0:LANES], carry[q, :, LANES:2 * LANES]) for q in range(NQ)))
        for q in range(NQ):
            carry[q, :, 0:LANES] = fin[q][0]
            carry[q, :, LANES:2 * LANES] = fin[q][1]
            xv = x_ref[:, cols[q]]
            hv = h_ref[q]
            hs_s[0:SUBLANES, :] = jnp.where(i == nb - 1, 0.0, hp_ref[q])
            hs_s[SUBLANES:rows + SUBLANES, :] = hv
            hprev = hs_s[0:rows, :]
            gv = g_s[q]
            g_re, g_im = gv[:, 0:LANES], gv[:, LANES:2 * LANES]
            hp_re, hp_im = hprev[:, 0:LANES], hprev[:, LANES:2 * LANES]
            dar = jnp.sum((g_re * hp_re + g_im * hp_im).reshape(T, SUBLANES, LANES), axis=0)
            dai = jnp.sum((g_im * hp_re - g_re * hp_im).reshape(T, SUBLANES, LANES), axis=0)
            dlam_ref[q, :, 0:LANES] += dar
            dlam_ref[q, :, LANES:2 * LANES] += dai

            gb = gv.astype(BF16)
            xrep = jnp.dot(rep, xv.astype(BF16), preferred_element_type=F32)
            lx = jnp.where(mask, xrep, 0.0).astype(BF16)
            dr_ref[q] += lax.dot_general(lx, gb, (((0,), (0,)), ((), ())), preferred_element_type=F32)
            dc_ref[q] += lax.dot_general(hv.astype(BF16), ldys[q], (((0,), (0,)), ((), ())),
                                         preferred_element_type=F32)
            zx = lax.dot_general(gb, r_ref[q], (((1,), (1,)), ((), ())), preferred_element_type=F32)
            zx = jnp.where(mask, zx, 0.0)
            dx_ref[:, cols[q]] = (jnp.sum(zx.reshape(T, SUBLANES, S5_PART), axis=1) + d_ref[:, cols[q]] * dys[q]
                                  + res_scale * res_ref[:, cols[q]])
            dd_ref[:, cols[q]] += jnp.sum(dys[q] * xv, axis=0, keepdims=True)

    xs = pl.BlockSpec((T, D), lambda i: (nb - 1 - i, 0))
    ms = pl.BlockSpec((NQ, S5_PART, S5_PART), lambda i: (0, 0, 0))
    ls = pl.BlockSpec((NQ, SUBLANES, S5_PART), lambda i: (0, 0, 0))
    ds_ = pl.BlockSpec((1, D), lambda i: (0, 0))
    outs, extra = _call(
        body, name="s5_scan_bwd",
        out_shape=(jax.ShapeDtypeStruct((S, D), F32), jax.ShapeDtypeStruct((NQ, S5_PART, S5_PART), F32),
                   jax.ShapeDtypeStruct((NQ, S5_PART, S5_PART), F32),
                   jax.ShapeDtypeStruct((NQ, SUBLANES, S5_PART), F32), jax.ShapeDtypeStruct((1, D), F32)),
        grid=(nb,),
        in_specs=[xs, xs, xs, xs, pl.BlockSpec((NQ, rows, S5_PART), lambda i: (0, nb - 1 - i, 0)),
                  pl.BlockSpec((NQ, SUBLANES, S5_PART), lambda i: (0, jnp.maximum((nb - 1 - i) * T - 1, 0), 0)),
                  ms, ms, ls, ds_],
        out_specs=(xs, ms, ms, ls, ds_),
        scratch_shapes=[pltpu.VMEM((NQ, rows, S5_PART), F32), pltpu.VMEM((NQ, rows, S5_PART), F32),
                        pltpu.VMEM((rows + SUBLANES, S5_PART), F32), pltpu.VMEM((NQ, SUBLANES, S5_PART), F32)],
        args=(x, y, dyg, res, hs, hs, rmat, cmat, lam, dskip), semantics=("arbitrary",), ex=ex)
    return (*outs, extra)


def _s5_discretise(a_re, a_im, log_dt, b_re, b_im):
    dt = jnp.exp(log_dt)[:, None]
    mag = jnp.exp(a_re * dt)
    ang = a_im * dt
    lb_re = mag * jnp.cos(ang)
    lb_im = mag * jnp.sin(ang)
    den = a_re * a_re + a_im * a_im
    nr = lb_re - 1.0
    ni = lb_im
    z_re = (nr * a_re + ni * a_im) / den
    z_im = (ni * a_re - nr * a_im) / den
    bb_re = z_re[..., None] * b_re - z_im[..., None] * b_im
    bb_im = z_re[..., None] * b_im + z_im[..., None] * b_re
    return lb_re, lb_im, bb_re, bb_im


def _s5_expand(w):
    G = w.shape[0]
    NQ = G // 16
    base = w.reshape(NQ, S5_PART, S5_STATE)
    half = (jnp.arange(S5_PART) // S5_GROUP) % 2
    sel = (half[:, None] == jnp.arange(2)[None, :]).astype(w.dtype)
    out = base[:, :, None, :] * sel[None, :, :, None]
    return out.reshape(NQ, S5_PART, 2 * S5_STATE)


def _s5_extract(m):
    NQ = m.shape[0]
    half = (jnp.arange(S5_PART) // S5_GROUP) % 2
    sel = (half[:, None] == jnp.arange(2)[None, :]).astype(m.dtype)
    base = jnp.sum(m.reshape(NQ, S5_PART, 2, S5_STATE) * sel[None, :, :, None], axis=2)
    return base.reshape(NQ * 16, S5_GROUP, S5_STATE)


def _s5_slab(v):
    return v.reshape(v.shape[0] // 16, SUBLANES, LANES)


def _adamw(w, g, m, v, name):
    R, C = w.shape
    tr = _tile(R, 256, SUBLANES)
    c1 = 1.0 / (1.0 - ADAM_B1 ** ADAM_STEP)
    c2 = 1.0 / (1.0 - ADAM_B2 ** ADAM_STEP)

    def body(w_ref, g_ref, m_ref, v_ref, d_ref, nm_ref, nv_ref):
        gv = g_ref[...]
        nm = ADAM_B1 * m_ref[...] + (1.0 - ADAM_B1) * gv
        nv = ADAM_B2 * v_ref[...] + (1.0 - ADAM_B2) * (gv * gv)
        nm_ref[...] = nm
        nv_ref[...] = nv
        d_ref[...] = -ADAM_LR * ((nm * c1) / (jnp.sqrt(nv * c2) + ADAM_EPS) + ADAM_WD * w_ref[...])

    blk = pl.BlockSpec((tr, C), lambda i: (i, 0))
    sh = jax.ShapeDtypeStruct((R, C), F32)
    return pl.pallas_call(
        body, name=name, out_shape=(sh, sh, sh), grid=(R // tr,), in_specs=[blk] * 4, out_specs=(blk,) * 3,
        compiler_params=_params(("parallel",)),
    )(w, g, m, v)


def _adamw_recv(parts, w, m, v, name):
    L, R, C = w.shape
    tr = _tile(R, 256, SUBLANES)
    c1 = 1.0 / (1.0 - ADAM_B1 ** ADAM_STEP)
    c2 = 1.0 / (1.0 - ADAM_B2 ** ADAM_STEP)

    def body(*refs):
        p_refs = refs[:L]
        w_ref, m_ref, v_ref, g_ref, d_ref, nm_ref, nv_ref = refs[L:]
        li = pl.program_id(0)
        for l in range(L):
            @pl.when(li == l)
            def _(p_ref=p_refs[l]):
                gv = p_ref[0].astype(F32)
                for k in range(1, N_DEV):
                    gv = gv + p_ref[k].astype(F32)
                g_ref[...] = gv
                nm = ADAM_B1 * m_ref[...] + (1.0 - ADAM_B1) * gv
                nv = ADAM_B2 * v_ref[...] + (1.0 - ADAM_B2) * (gv * gv)
                nm_ref[...] = nm
                nv_ref[...] = nv
                d_ref[...] = -ADAM_LR * ((nm * c1) / (jnp.sqrt(nv * c2) + ADAM_EPS) + ADAM_WD * w_ref[...])

    p_specs = [pl.BlockSpec((N_DEV, tr, C), lambda li, i, l=l: (0, jnp.where(li == l, i, 0), 0)) for l in range(L)]
    blk = pl.BlockSpec((None, tr, C), lambda li, i: (li, i, 0))
    sh = jax.ShapeDtypeStruct((L, R, C), F32)
    return pl.pallas_call(
        body, name=name, out_shape=(sh, sh, sh, sh), grid=(L, R // tr),
        in_specs=p_specs + [blk, blk, blk], out_specs=(blk,) * 4,
        compiler_params=_params(("parallel", "parallel")),
    )(*parts, w, m, v)


def _sum8(parts, name):
    _, R, C = parts.shape
    tr = _tile(R, 256, SUBLANES)

    def body(p_ref, o_ref):
        acc = p_ref[0].astype(F32)
        for k in range(1, N_DEV):
            acc = acc + p_ref[k].astype(F32)
        o_ref[...] = acc

    return pl.pallas_call(
        body, name=name, out_shape=jax.ShapeDtypeStruct((R, C), F32), grid=(R // tr,),
        in_specs=[pl.BlockSpec((N_DEV, tr, C), lambda i: (0, i, 0))],
        out_specs=pl.BlockSpec((tr, C), lambda i: (i, 0)), compiler_params=_params(("parallel",)),
    )(parts)


def _peers():
    x, y, c = lax.axis_index("x"), lax.axis_index("y"), lax.axis_index("c")
    me = 4 * x + 2 * y + c
    out = []
    for k in range(1, N_DEV):
        kx, ky, kc = (k >> 2) & 1, (k >> 1) & 1, k & 1
        px, py, pc = x ^ kx, y ^ ky, c ^ kc
        out.append(((px, py, pc), 4 * px + 2 * py + pc))
    return me, out


SIBLING = 1
SAME_CORE = (2, 4, 6)


class Exchange:
    def __init__(self, xs, scatter):
        self.xs = list(xs)
        self.scatter = list(scatter)
        self.n = len(self.xs)

    def out_shapes(self):
        return tuple(jax.ShapeDtypeStruct(x.shape if sc else (N_DEV,) + x.shape, x.dtype)
                     for x, sc in zip(self.xs, self.scatter))

    def sems(self):
        return [pltpu.SemaphoreType.DMA((self.n * N_DEV,)), pltpu.SemaphoreType.DMA((self.n * N_DEV,)),
                pltpu.SemaphoreType.DMA((self.n,))]

    def _copy(self, i, m, src, dst, to, send_sems, recv_sems):
        return pltpu.make_async_remote_copy(src_ref=src, dst_ref=dst, send_sem=send_sems.at[i * N_DEV + m],
                                            recv_sem=recv_sems.at[i * N_DEV + m], device_id=to, device_id_type=MESH)

    def start(self, x_refs, o_refs, send_sems, recv_sems, local_sems):
        me, peers = _peers()
        for i, (x_ref, o_ref, sc) in enumerate(zip(x_refs, o_refs, self.scatter)):
            pltpu.make_async_copy(x_ref.at[me] if sc else x_ref, o_ref.at[me], local_sems.at[i]).start()
            for m in (range(1, N_DEV) if sc else (SIBLING,) + SAME_CORE):
                dev, idx = peers[m - 1]
                self._copy(i, m, x_ref.at[idx] if sc else x_ref, o_ref.at[me], dev, send_sems, recv_sems).start()

    def middle(self, x_refs, o_refs, send_sems, recv_sems, local_sems):
        me, peers = _peers()
        sibling = peers[SIBLING - 1][0]
        for i, (x_ref, o_ref, sc) in enumerate(zip(x_refs, o_refs, self.scatter)):
            if sc:
                continue
            for m in SAME_CORE:
                dev, idx = peers[m - 1]
                self._copy(i, m, x_ref, o_ref.at[idx], dev, send_sems, recv_sems).wait_recv()
                self._copy(i, m ^ 1, o_ref.at[idx], o_ref.at[idx], sibling, send_sems, recv_sems).start()

    def wait(self, x_refs, o_refs, send_sems, recv_sems, local_sems):
        me, peers = _peers()
        for i, (x_ref, o_ref, sc) in enumerate(zip(x_refs, o_refs, self.scatter)):
            for m in range(1, N_DEV):
                dev, idx = peers[m - 1]
                cp = self._copy(i, m, x_ref.at[idx] if sc else x_ref, o_ref.at[idx], dev, send_sems, recv_sems)
                if sc or m not in SAME_CORE:
                    cp.wait_recv()
                cp.wait_send()
            pltpu.make_async_copy(x_ref.at[me] if sc else x_ref, o_ref.at[me], local_sems.at[i]).wait()


def _exchange(ex, name):
    n = ex.n

    def body(*refs):
        x_refs, o_refs, sems = refs[:n], refs[n:2 * n], refs[2 * n:]
        ex.start(x_refs, o_refs, *sems)
        ex.middle(x_refs, o_refs, *sems)
        ex.wait(x_refs, o_refs, *sems)

    hbm = pl.BlockSpec(memory_space=pltpu.HBM)
    return pl.pallas_call(body, name=name, out_shape=ex.out_shapes(), in_specs=[hbm] * n, out_specs=(hbm,) * n,
                          scratch_shapes=ex.sems())(*ex.xs)


def _call(body, *, name, out_shape, grid, in_specs, out_specs, scratch_shapes, args, semantics, ex=None):
    if ex is None:
        return pl.pallas_call(body, name=name, out_shape=out_shape, grid=grid, in_specs=in_specs, out_specs=out_specs,
                              scratch_shapes=scratch_shapes, compiler_params=_params(semantics))(*args), ()
    n, ni, no, ns = ex.n, len(args), len(out_shape), len(scratch_shapes)

    def wrapped(*refs):
        ins, cx = refs[:ni], refs[ni:ni + n]
        outs, co = refs[ni + n:ni + n + no], refs[ni + n + no:ni + 2 * n + no]
        scratch, sems = refs[ni + 2 * n + no:ni + 2 * n + no + ns], refs[ni + 2 * n + no + ns:]
        step = functools.reduce(lambda acc, a: acc * grid[a] + pl.program_id(a), range(len(grid)), 0)
        steps = math.prod(grid)

        @pl.when(step == 0)
        def _():
            ex.start(cx, co, *sems)

        body(*ins, *outs, *scratch)

        @pl.when(step == (steps * 3) // 4 - (steps > 1))
        def _():
            ex.middle(cx, co, *sems)

        @pl.when(step == steps - 1)
        def _():
            ex.wait(cx, co, *sems)

    hbm = pl.BlockSpec(memory_space=pltpu.HBM)
    res = pl.pallas_call(
        wrapped, name=name, out_shape=tuple(out_shape) + ex.out_shapes(), grid=grid,
        in_specs=list(in_specs) + [hbm] * n, out_specs=tuple(out_specs) + (hbm,) * n,
        scratch_shapes=list(scratch_shapes) + ex.sems(),
        compiler_params=_params(("arbitrary",) * len(grid)))(*args, *ex.xs)
    return res[:no], res[no:]


def _pack_flat(arrs):
    cat = jnp.concatenate([a.reshape(-1) for a in arrs])
    per = PACK_COLS * 2 * SUBLANES
    tot = -(-cat.shape[0] // per) * per
    return jnp.pad(cat, (0, tot - cat.shape[0])).reshape(tot // PACK_COLS, PACK_COLS)


def _unpack_flat(packed, shapes):
    flat = packed.reshape(-1)
    out, o = [], 0
    for s in shapes:
        n = math.prod(s)
        out.append(flat[o:o + n].reshape(s))
        o += n
    return out


def _ffn_fwd(x, xb, wt, wo4, g, b, alpha, tag, ex=None):
    h8, a4, extra = _ffn_h(xb, wt, tag + "_h", ex)
    out, outb, xh, rstd = _ffn_y_ln(a4, wo4, x, g, b, alpha, 0.5, tag + "_y_ln")
    return out, outb, (xb, h8, a4, xh, rstd), extra


def _ffn_bwd(dout, saved, wt, wo4, g, alpha, tag, carry=None, last=None):
    x, h8, a4, xh, rstd = saved
    load = dict(carry or {})
    slots = [[], [], [], []]
    for n, k in enumerate(sorted(load, key=lambda k: -load[k].size)):
        slots[min(n, 3)].append(k)
    arrived = {}

    def run(fn, slot, *args):
        keys = slots[slot]
        if not keys:
            return fn(*args)
        res, extra = fn(*args, ex=Exchange([load[k] for k in keys], [k != "small" for k in keys]))
        arrived.update(zip(keys, extra))
        return res

    dz, dg, db = _ln_bwd(dout, xh, rstd, g, tag + "_ln_bwd")
    if last is not None:
        (key_out, key_in), load["small"] = last(dg, db)
        slots[3].append("small")
    dw_out = run(_ffn_dwout, 3, a4, dz, 0.5, tag + "_dwout")
    dw_out = dw_out.reshape(N_DEV, -1, dw_out.shape[-1])
    if last is not None:
        load[key_out] = dw_out
        slots[2].append(key_out)
    dh8 = run(_ffn_dh, 2, dz, wo4, h8, 0.5, tag + "_dh")
    dw_in = run(_ffn_dwin, 0, x, dh8, tag + "_dwin")
    if last is not None:
        load[key_in] = dw_in
        slots[1].append(key_in)
    dx = run(_ffn_dx, 1, dh8, wt, dz, alpha, tag + "_dx")
    return dx, dw_in, dw_out, dg, db, arrived


def _fox_fwd(x, xb, w_in_pad, b_f_pad, w_o, g, b, alpha, tag, ex=None):
    S, D = x.shape
    H = D // HEAD_DIM
    proj = _mm(xb, w_in_pad, name=tag + "_proj")
    fl = proj[:, 3 * D:]
    cum = _fox_gate_fwd(fl, b_f_pad)
    cum2 = (cum[:, :H].T * LOG2E).reshape(H // 2, 2, S).transpose(0, 2, 1)
    o, lse, extra = _flash_fwd(proj, cum2, ex)
    m = _mm(o, w_o, name=tag + "_out")
    out, outb, xh, rstd = _res_ln_fwd(x, m, g, b, alpha, 1.0, tag + "_ln")
    return out, outb, (xb, proj, jnp.concatenate([cum2, lse], axis=2), o, fl, xh, rstd), extra


def _fox_bwd(dout, saved, w_in_pad, b_f_pad, w_o, g, alpha, tag, ex=None):
    x, proj, rows, o, fl, xh, rstd = saved
    S, D = x.shape
    H = D // HEAD_DIM
    dz, dg, db = _ln_bwd(dout, xh, rstd, g, tag + "_ln_bwd")
    dw_o = _mm(o, dz, ta=True, name=tag + "_dwo")
    do = _mm(dz, w_o, tb=True, name=tag + "_do")
    dq, dk, dv, dcq, dck, extra = _flash_bwd(proj, rows, o, do, ex)
    dcq = dcq.transpose(0, 2, 1).reshape(H, S)
    dcum = jnp.pad((dcq + dck.reshape(H, S)).T, ((0, 0), (0, LANES - H)))
    dfl, dbf = _fox_gate_bwd(dcum, fl, b_f_pad)
    dproj = jnp.concatenate([dq, dk, dv, dfl.astype(BF16)], axis=1)
    dw_in = _mm(x, dproj, ta=True, name=tag + "_dwin")
    dx = _mm(dproj, w_in_pad, tb=True, add=dz, add_scale=alpha, name=tag + "_dx")
    shards = {"fox_w_in": _split(dw_in[None, :, :3 * D + H], True)[:, 0].astype(BF16),
              "fox_w_o": _split(dw_o[None], False)[:, 0].astype(BF16)}
    return dx, shards, {"fox_b_f": dbf[0, :H], "lnm_g": dg, "lnm_b": db}, extra


def _s5_mats(p):
    lb_re, lb_im, bb_re, bb_im = _s5_discretise(p["a_re"], p["a_im"], p["log_dt"], p["b_re"], p["b_im"])
    rmat = jnp.concatenate([_s5_expand(bb_re.transpose(0, 2, 1)), _s5_expand(bb_im.transpose(0, 2, 1))], axis=2)
    cmat = jnp.concatenate([_s5_expand(p["c_re"]).transpose(0, 2, 1), -_s5_expand(p["c_im"]).transpose(0, 2, 1)],
                           axis=1)
    lam = jnp.concatenate([_s5_slab(lb_re), _s5_slab(lb_im)], axis=2)
    return rmat.astype(BF16), cmat.astype(BF16), lam


def _s5_block_fwd(x, p, w_out, g, b, alpha, tag, ex=None):
    S, D = x.shape
    rmat, cmat, lam = _s5_mats(p)
    dskip = p["d"].reshape(1, D)
    y, yg, hs, extra = _s5_fwd(x, rmat, cmat, lam, dskip, ex)
    vg = _mm(yg, w_out, name=tag + "_vg")
    m = _glu_fwd(vg, tag + "_glu")
    out, outb, xh, rstd = _res_ln_fwd(x, m, g, b, alpha, 1.0, tag + "_ln")
    return out, outb, (x, y, yg, hs, vg, rmat, cmat, lam, dskip, xh, rstd), extra


def _s5_block_bwd(dout, saved, p, w_out, g, alpha, tag, ex=None):
    x, y, yg, hs, vg, rmat, cmat, lam, dskip, xh, rstd = saved
    S, D = x.shape
    G = D // S5_GROUP
    dz, dg, db = _ln_bwd(dout, xh, rstd, g, tag + "_ln_bwd")
    dvg = _glu_bwd(vg, dz, tag + "_glu_bwd")
    dw_out = _mm(yg, dvg, ta=True, name=tag + "_dwout")
    dyg = _mm(dvg, w_out, tb=True, name=tag + "_dyg")
    dx, dr, dc, dlam, dd, extra = _s5_bwd(x, y, dyg, hs, rmat, cmat, lam, dskip, dz, alpha, ex)
    dbb_re = _s5_extract(dr[:, :, :LANES]).transpose(0, 2, 1)
    dbb_im = _s5_extract(dr[:, :, LANES:]).transpose(0, 2, 1)
    dc_re = _s5_extract(dc[:, :LANES, :].transpose(0, 2, 1))
    dc_im = -_s5_extract(dc[:, LANES:, :].transpose(0, 2, 1))
    dlb_re = dlam[:, :, :LANES].reshape(G, S5_STATE)
    dlb_im = dlam[:, :, LANES:].reshape(G, S5_STATE)
    _, vjp = jax.vjp(_s5_discretise, p["a_re"], p["a_im"], p["log_dt"], p["b_re"], p["b_im"])
    da_re, da_im, dlog_dt, db_re, db_im = vjp((dlb_re, dlb_im, dbb_re, dbb_im))
    small = dict(s5_a_re=da_re, s5_a_im=da_im, s5_log_dt=dlog_dt, s5_b_re=db_re, s5_b_im=db_im, s5_c_re=dc_re,
                 s5_c_im=dc_im, s5_d=dd.reshape(G, S5_GROUP), lnm_g=dg, lnm_b=db)
    return dx, {"s5_w_out": _split(dw_out[None], True)[:, 0].astype(BF16)}, small, extra


FFN_NAMES = ("ffn1_w_in", "ffn1_w_out", "ffn2_w_in", "ffn2_w_out")
BIG = FFN_NAMES + ("fox_w_in", "fox_w_o", "s5_w_out")
BIG_SPLIT_COLS = {"ffn1_w_in": True, "ffn1_w_out": False, "ffn2_w_in": True, "ffn2_w_out": False,
                  "fox_w_in": True, "fox_w_o": False, "s5_w_out": True}
SMALL = ("ln1_g", "ln1_b", "lnm_g", "lnm_b", "ln2_g", "ln2_b", "fox_b_f", "s5_a_re", "s5_a_im", "s5_log_dt",
         "s5_b_re", "s5_b_im", "s5_c_re", "s5_c_im", "s5_d")
WEIGHTS = ("ffn1_w_in", "ffn1_w_out", "ln1_g", "ln1_b", "lnm_g", "lnm_b", "ffn2_w_in", "ffn2_w_out", "ln2_g", "ln2_b",
           "fox_w_in", "fox_b_f", "fox_w_o", "s5_a_re", "s5_a_im", "s5_log_dt", "s5_b_re", "s5_b_im", "s5_c_re",
           "s5_c_im", "s5_d", "s5_w_out")


def _join(gathered, split_cols):
    n, L, r, c = gathered.shape
    if split_cols:
        return gathered.transpose(1, 2, 0, 3).reshape(L, r, n * c)
    return gathered.transpose(1, 0, 2, 3).reshape(L, n * r, c)


def _split(full, split_cols):
    L, R, C = full.shape
    if split_cols:
        return full.reshape(L, R, N_DEV, C // N_DEV).transpose(2, 0, 1, 3)
    return full.reshape(L, N_DEV, R // N_DEV, C).transpose(1, 0, 2, 3)


def _group(i, part):
    if part == "mixer":
        return (("fox_w_in", i // 2), ("fox_w_o", i // 2)) if i % 2 == 0 else (("s5_w_out", i // 2),)
    return ((part + "_w_in", i), (part + "_w_out", i))


def _prepare(name, g8):
    if name in ("ffn1_w_in", "ffn2_w_in"):
        return g8
    if name in FFN_NAMES:
        n, r, c = g8.shape
        return g8.reshape(n // 2, 2 * r, c)
    full = _join(g8[:, None], BIG_SPLIT_COLS[name])[0]
    if name == "fox_w_in":
        full = jnp.pad(full, ((0, 0), (0, LANES - full.shape[0] // HEAD_DIM)))
    return full


def _local_step(x, target, small, shard_of=None, pregathered=None):
    S, D = x.shape
    H = D // HEAD_DIM
    depth = small["ln1_g"].shape[0]
    alpha = (2.0 * depth) ** 0.25
    local = pregathered is not None
    bf_pad = jnp.pad(small["fox_b_f"], ((0, 0), (0, LANES - H)))

    def s5_params(j):
        return {k: small["s5_" + k][j] for k in ("a_re", "a_im", "log_dt", "b_re", "b_im", "c_re", "c_im", "d")}

    if local:
        W = {k: _prepare(k[0], g8) for k, g8 in pregathered.items()}
    else:
        keys = _group(0, "ffn1")
        got = _exchange(Exchange([shard_of(*k) for k in keys], [False] * len(keys)), "gather_first")
        W = {k: _prepare(k[0], g8) for k, g8 in zip(keys, got)}

    def gather(keys):
        return None if local else Exchange([shard_of(*k) for k in keys], [False] * len(keys))

    def landed(keys, extra):
        if not local:
            W.update({k: _prepare(k[0], g8) for k, g8 in zip(keys, extra)})

    saved = []
    h, hb = x, x.astype(BF16)
    for i in range(depth):
        j = i // 2
        keys = _group(i, "mixer")
        h, hb, s1, extra = _ffn_fwd(h, hb, W[("ffn1_w_in", i)], W[("ffn1_w_out", i)], small["ln1_g"][i],
                                    small["ln1_b"][i], alpha, f"l{i}_ffn1", gather(keys))
        landed(keys, extra)
        keys = _group(i, "ffn2") + (_group(i + 1, "ffn1") if i + 1 < depth else ())
        ex = gather(keys)
        if i % 2 == 0:
            h, hb, s2, extra = _fox_fwd(h, hb, W[("fox_w_in", j)], bf_pad[j:j + 1], W[("fox_w_o", j)],
                                        small["lnm_g"][i], small["lnm_b"][i], alpha, f"l{i}_fox", ex)
        else:
            h, hb, s2, extra = _s5_block_fwd(h, s5_params(j), W[("s5_w_out", j)], small["lnm_g"][i],
                                             small["lnm_b"][i], alpha, f"l{i}_s5", ex)
        landed(keys, extra)
        h, hb, s3, _ = _ffn_fwd(h, hb, W[("ffn2_w_in", i)], W[("ffn2_w_out", i)], small["ln2_g"][i],
                                small["ln2_b"][i], alpha, f"l{i}_ffn2")
        saved.append((s1, s2, s3))

    dh, loss_part = _loss_fwd_bwd(h, target)

    arrived = {}
    pending = {}
    gs = {k: [None] * small[k].shape[0] for k in SMALL}
    for i in reversed(range(depth)):
        j = i // 2
        s1, s2, s3 = saved[i]
        held = {k: pending.pop(k) for k in [("ffn1_w_in", i + 1)] if i % 2 == 0 and k in pending}
        dh, dw_in, dw_out, gs["ln2_g"][i], gs["ln2_b"][i], got = _ffn_bwd(
            dh, s3, W[("ffn2_w_in", i)], W[("ffn2_w_out", i)], small["ln2_g"][i], alpha, f"l{i}_ffn2",
            None if local else pending)
        arrived.update(pending if local else got)
        pending = {("ffn2_w_in", i): dw_in, ("ffn2_w_out", i): dw_out, **held}
        keys = list(pending)
        ex = None if local else Exchange([pending[k] for k in keys], [True] * len(keys))
        if i % 2 == 0:
            dh, mix, sg, extra = _fox_bwd(dh, s2, W[("fox_w_in", j)], bf_pad[j:j + 1], W[("fox_w_o", j)],
                                          small["lnm_g"][i], alpha, f"l{i}_fox", ex)
        else:
            dh, mix, sg, extra = _s5_block_bwd(dh, s2, s5_params(j), W[("s5_w_out", j)], small["lnm_g"][i], alpha,
                                               f"l{i}_s5", ex)
        arrived.update(zip(keys, [pending[k] for k in keys] if local else extra))
        pending = {(k, j): val for k, val in mix.items()}
        for k, val in sg.items():
            gs[k][i if k in ("lnm_g", "lnm_b") else j] = val
        def last(dg, db):
            gs["ln1_g"][0], gs["ln1_b"][0] = dg, db
            packed = _pack_flat([jnp.stack(gs[k]) for k in SMALL]).astype(BF16)
            return (("ffn1_w_out", 0), ("ffn1_w_in", 0)), packed

        dh, dw_in, dw_out, gs["ln1_g"][i], gs["ln1_b"][i], got = _ffn_bwd(
            dh, s1, W[("ffn1_w_in", i)], W[("ffn1_w_out", i)], small["ln1_g"][i], alpha, f"l{i}_ffn1",
            None if local else pending, last if (i == 0 and not local) else None)
        arrived.update(pending if local else got)
        pending = {("ffn1_w_in", i): dw_in, ("ffn1_w_out", i): dw_out}
    if local:
        arrived.update(pending)
    return loss_part, dh, arrived, {k: jnp.stack(v) for k, v in gs.items()}


def kernel(x, ffn1_w_in, ffn1_w_out, ln1_g, ln1_b, lnm_g, lnm_b, ffn2_w_in, ffn2_w_out, ln2_g, ln2_b, fox_w_in, fox_b_f, fox_w_o, s5_a_re, s5_a_im, s5_log_dt, s5_b_re, s5_b_im, s5_c_re, s5_c_im, s5_d, s5_w_out, loss_target, m_ffn1_w_in, m_ffn1_w_out, m_ln1_g, m_ln1_b, m_lnm_g, m_lnm_b, m_ffn2_w_in, m_ffn2_w_out, m_ln2_g, m_ln2_b, m_fox_w_in, m_fox_b_f, m_fox_w_o, m_s5_a_re, m_s5_a_im, m_s5_log_dt, m_s5_b_re, m_s5_b_im, m_s5_c_re, m_s5_c_im, m_s5_d, m_s5_w_out, v_ffn1_w_in, v_ffn1_w_out, v_ln1_g, v_ln1_b, v_lnm_g, v_lnm_b, v_ffn2_w_in, v_ffn2_w_out, v_ln2_g, v_ln2_b, v_fox_w_in, v_fox_b_f, v_fox_w_o, v_s5_a_re, v_s5_a_im, v_s5_log_dt, v_s5_b_re, v_s5_b_im, v_s5_c_re, v_s5_c_im, v_s5_d, v_s5_w_out):
    args = dict(locals())
    w = {k: args[k] for k in WEIGHTS}
    m = {k: args["m_" + k] for k in WEIGHTS}
    v = {k: args["v_" + k] for k in WEIGHTS}
    small = {k: w[k] for k in SMALL}

    turned = lambda d: {k: jnp.swapaxes(a, 1, 2) if k in ("ffn1_w_in", "ffn2_w_in") else a for k, a in d.items()}
    w, m, v = turned(w), turned(m), turned(v)
    wb = {k: w[k].astype(BF16) for k in BIG}
    loss_part, dx, arrived, _ = _local_step(x[0], loss_target[0], small, lambda k, idx: wb[k][idx])
    loss = lax.psum(loss_part, ("x", "y", "c"))
    small_shapes = [w[k].shape for k in SMALL]
    g_small_flat = _sum8(arrived["small"], "sum_small_grads")

    grads, delta, new_m, new_v = {}, {}, {}, {}
    for k in BIG:
        parts = [arrived[(k, l)] for l in range(w[k].shape[0])]
        grads[k], delta[k], new_m[k], new_v[k] = _adamw_recv(parts, w[k], m[k], v[k], "adamw_" + k)
    pk = lambda d: _pack_flat([d[k] for k in SMALL])
    d_, m_, v_ = _adamw(pk(w), g_small_flat, pk(m), pk(v), "adamw_small")
    for dst, flat in ((grads, g_small_flat), (delta, d_), (new_m, m_), (new_v, v_)):
        dst.update(zip(SMALL, _unpack_flat(flat, small_shapes)))
    grads, delta, new_m, new_v = turned(grads), turned(delta), turned(new_m), turned(new_v)

    return (loss, dx[None], *[grads[k] for k in WEIGHTS], *[delta[k] for k in WEIGHTS],
            *[new_m[k] for k in WEIGHTS], *[new_v[k] for k in WEIGHTS])
```

```python
import functools
import math

import jax
import jax.numpy as jnp
from jax import lax
from jax.experimental import pallas as pl
from jax.experimental.pallas import tpu as pltpu

F32 = jnp.float32
BF16 = jnp.bfloat16

N_DEV = 8
HEAD_DIM = 64
S5_GROUP = 16
S5_STATE = 64
LANES = 128
SUBLANES = 8
S5_PART = 256
ATTN_TILE = 512
LN_EPS = 1e-5
NEG_INF = -1e30
LOG2E = 1.4426950408889634
ADAM_LR, ADAM_B1, ADAM_B2, ADAM_EPS, ADAM_WD, ADAM_STEP = 0.001, 0.9, 0.999, 1e-08, 0.01, 10
VMEM_LIMIT = 48 * 1024 * 1024
PACK_COLS = 1024

MESH = pl.DeviceIdType.MESH


def _tile(dim, pref, align=LANES):
    if dim <= pref:
        return dim
    t = (pref // align) * align
    while t >= align:
        if dim % t == 0:
            return t
        t -= align
    return dim


def _params(sem):
    return pltpu.CompilerParams(dimension_semantics=sem, vmem_limit_bytes=VMEM_LIMIT)


def _mm(a, b, *, ta=False, tb=False, out_dtype=F32, scale=None, add=None, add_scale=1.0,
        tm=512, tn=1408, tk=1408, name="mm"):
    if ta:
        K, M = a.shape
    else:
        M, K = a.shape
    if tb:
        N, K2 = b.shape
    else:
        K2, N = b.shape
    assert K == K2, (a.shape, b.shape, ta, tb)
    tm, tn, tk = _tile(M, tm), _tile(N, tn), _tile(K, tk)
    a_spec = pl.BlockSpec((tk, tm), lambda i, j, k: (k, i)) if ta else pl.BlockSpec((tm, tk), lambda i, j, k: (i, k))
    b_spec = pl.BlockSpec((tn, tk), lambda i, j, k: (j, k)) if tb else pl.BlockSpec((tk, tn), lambda i, j, k: (k, j))
    o_spec = pl.BlockSpec((tm, tn), lambda i, j, k: (i, j))
    return _mm_core(name, a, b, a_spec, b_spec, o_spec, jax.ShapeDtypeStruct((M, N), out_dtype),
                    (M // tm, N // tn, K // tk), (tm, tn), ta, tb, scale, add, add_scale)


def _mm_core(name, a, b, a_spec, b_spec, o_spec, out_shape, grid, acc_shape, ta, tb, scale=None, add=None,
             add_scale=1.0, ex=None):
    nk = grid[2]

    def body(*refs):
        if add is None:
            a_ref, b_ref, o_ref, acc = refs
            add_ref = None
        else:
            a_ref, b_ref, add_ref, o_ref, acc = refs
        k = pl.program_id(2)

        @pl.when(k == 0)
        def _():
            acc[...] = jnp.zeros_like(acc)

        dims = (((0 if ta else 1,), (1 if tb else 0,)), ((), ()))
        if len(a_ref.shape) == 3:
            part = sum(lax.dot_general(a_ref[u].astype(BF16), b_ref[u].astype(BF16), dims, preferred_element_type=F32)
                       for u in range(a_ref.shape[0]))
        else:
            part = lax.dot_general(a_ref[...].astype(BF16), b_ref[...].astype(BF16), dims, preferred_element_type=F32)
        acc[...] += part

        @pl.when(k == nk - 1)
        def _():
            r = acc[...]
            if scale is not None:
                r = r * scale
            if add_ref is not None:
                r = r + add_scale * add_ref[...]
            o_ref[...] = r.astype(out_shape.dtype)

    in_specs = [a_spec, b_spec]
    args = [a, b]
    if add is not None:
        in_specs.append(o_spec)
        args.append(add)
    (res,), extra = _call(body, name=name, out_shape=(out_shape,), grid=grid, in_specs=in_specs, out_specs=(o_spec,),
                          scratch_shapes=[pltpu.VMEM(acc_shape, F32)], args=args,
                          semantics=("parallel", "parallel", "arbitrary"), ex=ex)
    return res if ex is None else (res, extra)


def _ffn_h(x, wt, name, ex=None):
    S, D = x.shape
    n, c, _ = wt.shape
    nb = n // 2
    tm = _tile(S, 1024, SUBLANES)
    dims = (((1,), (1,)), ((), ()))

    def body(x_ref, wg_ref, wu_ref, h_ref, a_ref):
        xb = x_ref[...]
        g = lax.dot_general(xb, wg_ref[...], dims, preferred_element_type=F32).astype(BF16)
        u = lax.dot_general(xb, wu_ref[...], dims, preferred_element_type=F32).astype(BF16)
        h_ref[0] = g
        h_ref[1] = u
        g = g.astype(F32)
        a_ref[...] = (g * _sigmoid(g) * u.astype(F32)).astype(BF16)

    (h, a4), extra = _call(
        body, name=name,
        out_shape=(jax.ShapeDtypeStruct((2, nb, S, c), BF16), jax.ShapeDtypeStruct((nb, S, c), BF16)),
        grid=(S // tm, nb),
        in_specs=[pl.BlockSpec((tm, D), lambda i, j: (i, 0)), pl.BlockSpec((None, c, D), lambda i, j: (j, 0, 0)),
                  pl.BlockSpec((None, c, D), lambda i, j: (j + nb, 0, 0))],
        out_specs=(pl.BlockSpec((2, None, tm, c), lambda i, j: (0, j, i, 0)),
                   pl.BlockSpec((None, tm, c), lambda i, j: (j, i, 0))),
        scratch_shapes=[], args=(x, wt, wt), semantics=("parallel", "parallel"), ex=ex)
    return h.reshape(n, S, c), a4, extra


def _ffn_y_ln(a4, wo4, x, g, b, alpha, s, name):
    nb, S, c = a4.shape
    D = wo4.shape[-1]
    tm = _tile(S, 512, SUBLANES)

    def body(a_ref, w_ref, x_ref, g_ref, b_ref, o_ref, ob_ref, xh_ref, r_ref):
        y = jnp.dot(a_ref[0], w_ref[0], preferred_element_type=F32)
        for k in range(1, nb):
            y = y + jnp.dot(a_ref[k], w_ref[k], preferred_element_type=F32)
        z = alpha * x_ref[...] + s * y
        mu = jnp.mean(z, axis=-1, keepdims=True)
        zc = z - mu
        rstd = lax.rsqrt(jnp.mean(zc * zc, axis=-1, keepdims=True) + LN_EPS)
        xh = zc * rstd
        xh_ref[...] = xh
        r_ref[...] = rstd
        out = xh * g_ref[...] + b_ref[...]
        o_ref[...] = out
        ob_ref[...] = out.astype(BF16)

    row = pl.BlockSpec((tm, D), lambda i: (i, 0))
    vec = pl.BlockSpec((1, D), lambda i: (0, 0))
    return pl.pallas_call(
        body, name=name,
        out_shape=(jax.ShapeDtypeStruct((S, D), F32), jax.ShapeDtypeStruct((S, D), BF16),
                   jax.ShapeDtypeStruct((S, D), F32), jax.ShapeDtypeStruct((S, 1), F32)),
        grid=(S // tm,),
        in_specs=[pl.BlockSpec((nb, tm, c), lambda i: (0, i, 0)), pl.BlockSpec((nb, c, D), lambda i: (0, 0, 0)),
                  row, vec, vec],
        out_specs=(row, row, row, pl.BlockSpec((tm, 1), lambda i: (i, 0))),
        compiler_params=_params(("parallel",)),
    )(a4, wo4, x, g.reshape(1, D), b.reshape(1, D))


def _ffn_dwout(a4, dz, scale, name, ex=None):
    nb, S, c = a4.shape
    D = dz.shape[1]
    tk, tn = _tile(S, 1024, SUBLANES), _tile(D, 1024)
    return _mm_core(name, a4, dz, pl.BlockSpec((None, tk, c), lambda i, j, k: (i, k, 0)),
                    pl.BlockSpec((tk, tn), lambda i, j, k: (k, j)),
                    pl.BlockSpec((None, c, tn), lambda i, j, k: (i, 0, j)),
                    jax.ShapeDtypeStruct((nb, c, D), BF16), (nb, D // tn, S // tk), (c, tn), True, False, scale,
                    ex=ex)


def _ffn_dh(dz, wo4, h8, scale, name, ex=None):
    S, D = dz.shape
    nb, c = wo4.shape[0], wo4.shape[1]
    tm = _tile(S, 512, SUBLANES)

    def body(dz_ref, w_ref, h_ref, d_ref):
        da = lax.dot_general(dz_ref[...].astype(BF16), w_ref[...], (((1,), (1,)), ((), ())),
                             preferred_element_type=F32) * scale
        g = h_ref[0].astype(F32)
        u = h_ref[1].astype(F32)
        sg = _sigmoid(g)
        silu = g * sg
        d_ref[0] = (da * u * (sg + silu * (1.0 - sg))).astype(BF16)
        d_ref[1] = (da * silu).astype(BF16)

    pair = pl.BlockSpec((2, None, tm, c), lambda i, j: (0, j, i, 0))
    (dh,), extra = _call(
        body, name=name, out_shape=(jax.ShapeDtypeStruct((2, nb, S, c), BF16),), grid=(S // tm, nb),
        in_specs=[pl.BlockSpec((tm, D), lambda i, j: (i, 0)), pl.BlockSpec((None, c, D), lambda i, j: (j, 0, 0)),
                  pair],
        out_specs=(pair,), scratch_shapes=[], args=(dz, wo4, h8.reshape(2, nb, S, c)),
        semantics=("parallel", "parallel"), ex=ex)
    dh = dh.reshape(2 * nb, S, c)
    return dh if ex is None else (dh, extra)


def _ffn_dwin(x, dh8, name, ex=None):
    S, D = x.shape
    n, _, c = dh8.shape
    return _mm_core(name, dh8, x, pl.BlockSpec((None, S, c), lambda i, j, k: (i, 0, 0)),
                    pl.BlockSpec((S, D), lambda i, j, k: (0, 0)),
                    pl.BlockSpec((None, c, D), lambda i, j, k: (i, 0, 0)),
                    jax.ShapeDtypeStruct((n, c, D), BF16), (n, 1, 1), (c, D), True, False, ex=ex)


def _ffn_dx(dh8, wt, dz, alpha, name, ex=None):
    n, S, c = dh8.shape
    D = wt.shape[2]
    tm, tn = _tile(S, 1024, SUBLANES), _tile(D, 1024)
    return _mm_core(name, dh8, wt, pl.BlockSpec((2, tm, c), lambda i, j, k: (k, i, 0)),
                    pl.BlockSpec((2, c, tn), lambda i, j, k: (k, 0, j)),
                    pl.BlockSpec((tm, tn), lambda i, j, k: (i, j)),
                    jax.ShapeDtypeStruct((S, D), F32), (S // tm, D // tn, n // 2), (tm, tn), False, False,
                    None, dz, alpha, ex=ex)


def _res_ln_fwd(x, y, g, b, alpha, s, name):
    S, D = x.shape
    tm = _tile(S, 256, SUBLANES)

    def body(x_ref, y_ref, g_ref, b_ref, o_ref, ob_ref, xh_ref, r_ref):
        z = alpha * x_ref[...] + s * y_ref[...]
        mu = jnp.mean(z, axis=-1, keepdims=True)
        zc = z - mu
        var = jnp.mean(zc * zc, axis=-1, keepdims=True)
        rstd = lax.rsqrt(var + LN_EPS)
        xh = zc * rstd
        xh_ref[...] = xh
        r_ref[...] = rstd
        out = xh * g_ref[...] + b_ref[...]
        o_ref[...] = out
        ob_ref[...] = out.astype(BF16)

    row = pl.BlockSpec((tm, D), lambda i: (i, 0))
    vec = pl.BlockSpec((1, D), lambda i: (0, 0))
    return pl.pallas_call(
        body, name=name,
        out_shape=(jax.ShapeDtypeStruct((S, D), F32), jax.ShapeDtypeStruct((S, D), BF16),
                   jax.ShapeDtypeStruct((S, D), F32), jax.ShapeDtypeStruct((S, 1), F32)),
        grid=(S // tm,), in_specs=[row, row, vec, vec],
        out_specs=(row, row, row, pl.BlockSpec((tm, 1), lambda i: (i, 0))),
        compiler_params=_params(("parallel",)),
    )(x, y, g.reshape(1, D), b.reshape(1, D))


def _ln_bwd(dout, xh, rstd, g, name):
    S, D = dout.shape
    tm = _tile(S, 256, SUBLANES)

    def body(d_ref, xh_ref, r_ref, g_ref, dz_ref, dg_ref, db_ref):
        i = pl.program_id(0)

        @pl.when(i == 0)
        def _():
            dg_ref[...] = jnp.zeros_like(dg_ref)
            db_ref[...] = jnp.zeros_like(db_ref)

        d = d_ref[...]
        xhv = xh_ref[...]
        dxh = d * g_ref[...]
        m1 = jnp.mean(dxh, axis=-1, keepdims=True)
        m2 = jnp.mean(dxh * xhv, axis=-1, keepdims=True)
        dz_ref[...] = r_ref[...] * (dxh - m1 - xhv * m2)
        dg_ref[...] += jnp.sum(d * xhv, axis=0, keepdims=True)
        db_ref[...] += jnp.sum(d, axis=0, keepdims=True)

    row = pl.BlockSpec((tm, D), lambda i: (i, 0))
    vec = pl.BlockSpec((1, D), lambda i: (0, 0))
    dz, dg, db = pl.pallas_call(
        body, name=name,
        out_shape=(jax.ShapeDtypeStruct((S, D), F32), jax.ShapeDtypeStruct((1, D), F32),
                   jax.ShapeDtypeStruct((1, D), F32)),
        grid=(S // tm,), in_specs=[row, row, pl.BlockSpec((tm, 1), lambda i: (i, 0)), vec],
        out_specs=(row, vec, vec),
        compiler_params=_params(("arbitrary",)),
    )(dout, xh, rstd, g.reshape(1, D))
    return dz, dg[0], db[0]


def _sigmoid(x):
    return 0.5 * jnp.tanh(0.5 * x) + 0.5


def _glu_fwd(vg, name):
    S, D2 = vg.shape
    D = D2 // 2
    tm = _tile(S, 512, SUBLANES)

    def body(v_ref, g_ref, o_ref):
        o_ref[...] = v_ref[...] * _sigmoid(g_ref[...])

    return pl.pallas_call(
        body, name=name, out_shape=jax.ShapeDtypeStruct((S, D), F32), grid=(S // tm,),
        in_specs=[pl.BlockSpec((tm, D), lambda i: (i, 0)), pl.BlockSpec((tm, D), lambda i: (i, 1))],
        out_specs=pl.BlockSpec((tm, D), lambda i: (i, 0)),
        compiler_params=_params(("parallel",)),
    )(vg, vg)


def _glu_bwd(vg, dm, name):
    S, D2 = vg.shape
    D = D2 // 2
    tm = _tile(S, 512, SUBLANES)

    def body(v_ref, g_ref, dm_ref, dv_ref, dg_ref):
        sg = _sigmoid(g_ref[...])
        d = dm_ref[...]
        dv_ref[...] = (d * sg).astype(BF16)
        dg_ref[...] = (d * v_ref[...] * sg * (1.0 - sg)).astype(BF16)

    blk = pl.BlockSpec((tm, D), lambda i: (i, 0))
    dv, dg = pl.pallas_call(
        body, name=name,
        out_shape=(jax.ShapeDtypeStruct((S, D), BF16), jax.ShapeDtypeStruct((S, D), BF16)),
        grid=(S // tm,), in_specs=[blk, pl.BlockSpec((tm, D), lambda i: (i, 1)), blk],
        out_specs=(blk, blk), compiler_params=_params(("parallel",)),
    )(vg, vg, dm)
    return jnp.concatenate([dv, dg], axis=1)


def _loss_fwd_bwd(y, target):
    S, D = y.shape
    tm = _tile(S, 256, SUBLANES)

    def body(y_ref, t_ref, dy_ref, l_ref):
        i = pl.program_id(0)

        @pl.when(i == 0)
        def _():
            l_ref[...] = jnp.zeros_like(l_ref)

        e = y_ref[...] - t_ref[...]
        dy_ref[...] = e * (1.0 / D)
        l_ref[...] += jnp.sum(e * e, axis=0, keepdims=True) * (0.5 / D)

    row = pl.BlockSpec((tm, D), lambda i: (i, 0))
    dy, part = pl.pallas_call(
        body, name="loss", out_shape=(jax.ShapeDtypeStruct((S, D), F32), jax.ShapeDtypeStruct((1, D), F32)),
        grid=(S // tm,), in_specs=[row, row], out_specs=(row, pl.BlockSpec((1, D), lambda i: (0, 0))),
        compiler_params=_params(("arbitrary",)),
    )(y, target)
    return dy, jnp.sum(part)


def _tri(n, lower):
    r = lax.broadcasted_iota(jnp.int32, (n, n), 0)
    c = lax.broadcasted_iota(jnp.int32, (n, n), 1)
    return jnp.where((c <= r) if lower else (c >= r), 1.0, 0.0)


def _fox_gate_fwd(fl, bf):
    S, W = fl.shape
    tm = _tile(S, 256, SUBLANES)

    def body(fl_ref, b_ref, c_ref, carry):
        i = pl.program_id(0)

        @pl.when(i == 0)
        def _():
            carry[...] = jnp.zeros_like(carry)

        x = fl_ref[...] + b_ref[...]
        lf = jnp.minimum(x, 0.0) - jnp.log(1.0 + jnp.exp(-jnp.abs(x)))
        c_ref[...] = jnp.dot(_tri(tm, True), lf, precision=lax.Precision.HIGHEST,
                             preferred_element_type=F32) + carry[...]
        carry[...] += jnp.sum(lf, axis=0, keepdims=True)

    blk = pl.BlockSpec((tm, W), lambda i: (i, 0))
    return pl.pallas_call(
        body, name="fox_gate_fwd", out_shape=jax.ShapeDtypeStruct((S, W), F32), grid=(S // tm,),
        in_specs=[blk, pl.BlockSpec((1, W), lambda i: (0, 0))], out_specs=blk,
        scratch_shapes=[pltpu.VMEM((1, W), F32)], compiler_params=_params(("arbitrary",)),
    )(fl, bf)


def _fox_gate_bwd(dcum, fl, bf):
    S, W = fl.shape
    tm = _tile(S, 256, SUBLANES)
    nb = S // tm

    def body(dc_ref, fl_ref, b_ref, dfl_ref, db_ref, carry):
        i = pl.program_id(0)

        @pl.when(i == 0)
        def _():
            carry[...] = jnp.zeros_like(carry)
            db_ref[...] = jnp.zeros_like(db_ref)

        dc = dc_ref[...]
        r = jnp.dot(_tri(tm, False), dc, precision=lax.Precision.HIGHEST, preferred_element_type=F32) + carry[...]
        carry[...] += jnp.sum(dc, axis=0, keepdims=True)
        x = fl_ref[...] + b_ref[...]
        e = jnp.exp(-jnp.abs(x))
        dfl = r * jnp.where(x >= 0, e, 1.0) / (1.0 + e)
        dfl_ref[...] = dfl
        db_ref[...] += jnp.sum(dfl, axis=0, keepdims=True)

    blk = pl.BlockSpec((tm, W), lambda i: (nb - 1 - i, 0))
    vec = pl.BlockSpec((1, W), lambda i: (0, 0))
    return pl.pallas_call(
        body, name="fox_gate_bwd",
        out_shape=(jax.ShapeDtypeStruct((S, W), F32), jax.ShapeDtypeStruct((1, W), F32)),
        grid=(nb,), in_specs=[blk, blk, vec], out_specs=(blk, vec),
        scratch_shapes=[pltpu.VMEM((1, W), F32)], compiler_params=_params(("arbitrary",)),
    )(dcum, fl, bf)


def _causal(t):
    row = lax.broadcasted_iota(jnp.int32, (t, t), 0)
    col = lax.broadcasted_iota(jnp.int32, (t, t), 1)
    return col <= row


def _first_head(shape):
    return lax.broadcasted_iota(jnp.int32, shape, len(shape) - 1) < HEAD_DIM


def _split3(x):
    hi = x.astype(BF16).astype(F32)
    r = x - hi
    mid = r.astype(BF16).astype(F32)
    return hi, mid, (r - mid).astype(BF16).astype(F32)


def _bias_lanes(c, query):
    lane = lax.broadcasted_iota(jnp.int32, (c.shape[0], LANES), 1)
    hi, mid, lo = _split3(c)
    if query:
        out = jnp.where(lane == 0, hi, jnp.where(lane == 1, mid, jnp.where(lane == 2, lo,
                                                                          jnp.where(lane < 6, 1.0, 0.0))))
    else:
        out = jnp.where(lane < 3, 1.0, jnp.where(lane == 3, -hi, jnp.where(lane == 4, -mid,
                                                                          jnp.where(lane == 5, -lo, 0.0))))
    return out.astype(BF16)


def _flash_fwd(proj, cum2, ex=None):
    S = proj.shape[0]
    D = (proj.shape[1] - LANES) // 3
    HP = D // LANES
    t = _tile(S, ATTN_TILE)
    nq = S // t
    scale = 1.0 / math.sqrt(HEAD_DIM)

    def body(q_ref, k_ref, v_ref, cq_ref, ck_ref, o_ref, lse_ref, kx, vb):
        qi = pl.program_id(1)

        @pl.when(qi == 0)
        def _():
            def prep(kb, c):
                sl = pl.ds(pl.multiple_of(kb * t, t), t)
                k16 = k_ref[sl, :].astype(BF16)
                ckv = ck_ref[0, sl, :]
                for a in range(2):
                    kx[a, sl, 0:LANES] = k16
                    kx[a, sl, LANES:2 * LANES] = _bias_lanes(ckv[:, a:a + 1], False)
                vb[sl, :] = v_ref[sl, :].astype(BF16)
                return c

            lax.fori_loop(0, nq, prep, 0)

        first = _first_head((t, LANES))
        qf = q_ref[...] * (scale * LOG2E)
        cqv = cq_ref[0]
        qx = [jnp.concatenate([jnp.where(keep, qf, 0.0).astype(BF16), _bias_lanes(cqv[:, a:a + 1], True)], axis=1)
              for a, keep in enumerate((first, jnp.logical_not(first)))]

        def block(ki, carry, masked):
            m_old, l_old, acc = carry
            sl = pl.ds(pl.multiple_of(ki * t, t), t)
            vv = vb[sl, :]
            m_new, l_new, corr, pv = [], [], [], []
            for a in range(2):
                s = lax.dot_general(qx[a], kx[a, sl, :], (((1,), (1,)), ((), ())), preferred_element_type=F32)
                if masked:
                    s = jnp.where(_causal(t), s, NEG_INF)
                m_a = jnp.maximum(m_old[a], jnp.max(s, axis=1, keepdims=True))
                p = jnp.exp2(s - m_a)
                c_a = jnp.exp2(m_old[a] - m_a)
                m_new.append(m_a)
                corr.append(c_a)
                l_new.append(c_a * l_old[a] + jnp.sum(p, axis=1, keepdims=True))
                pv.append(jnp.dot(p.astype(BF16), vv, preferred_element_type=F32))
            acc = jnp.where(first, corr[0] * acc + pv[0], corr[1] * acc + pv[1])
            return tuple(m_new), tuple(l_new), acc

        neg = jnp.full((t, 1), NEG_INF, F32)
        zero = jnp.zeros((t, 1), F32)
        carry = lax.fori_loop(0, qi, lambda ki, c: block(ki, c, False),
                              ((neg, neg), (zero, zero), jnp.zeros((t, LANES), F32)))
        m, l, acc = block(qi, carry, True)
        o_ref[...] = acc / jnp.where(first, l[0], l[1])
        lse_ref[0, :, 0:1] = m[0] + jnp.log2(l[0])
        lse_ref[0, :, 1:2] = m[1] + jnp.log2(l[1])

    qblk = pl.BlockSpec((t, LANES), lambda h, i: (i, h))
    r2 = pl.BlockSpec((1, t, 2), lambda h, i: (h, i, 0))
    (o, lse), extra = _call(
        body, name="fox_attn_fwd",
        out_shape=(jax.ShapeDtypeStruct((S, D), F32), jax.ShapeDtypeStruct((HP, S, 2), F32)),
        grid=(HP, nq),
        in_specs=[qblk, pl.BlockSpec((S, LANES), lambda h, i: (0, HP + h)),
                  pl.BlockSpec((S, LANES), lambda h, i: (0, 2 * HP + h)), r2,
                  pl.BlockSpec((1, S, 2), lambda h, i: (h, 0, 0))],
        out_specs=(qblk, r2),
        scratch_shapes=[pltpu.VMEM((2, S, 2 * LANES), BF16), pltpu.VMEM((S, LANES), BF16)],
        args=(proj, proj, proj, cum2, cum2), semantics=("parallel", "arbitrary"), ex=ex)
    return o, lse, extra


def _flash_bwd(proj, rows, o, do, ex=None):
    S = proj.shape[0]
    D = (proj.shape[1] - LANES) // 3
    HP = D // LANES
    t = _tile(S, ATTN_TILE)
    nb = S // t
    scale = 1.0 / math.sqrt(HEAD_DIM)

    def body(q_ref, k_ref, v_ref, rows_ref, o_ref, do_ref, dq_ref, dk_ref, dv_ref, dcq_ref, dck_ref,
             q_s, do_s, dl_s, dq_acc, dk_acc, dv_acc, dc_acc):
        kb = pl.program_id(1)

        @pl.when(kb == 0)
        def _():
            dq_acc[...] = jnp.zeros_like(dq_acc)
            dcq_ref[...] = jnp.zeros_like(dcq_ref)

            def prep(qb, c):
                sl = pl.ds(pl.multiple_of(qb * t, t), t)
                first = _first_head((t, LANES))
                qf = q_ref[sl, :] * (scale * LOG2E)
                dof = do_ref[sl, :]
                prod = dof * o_ref[sl, :]
                rv = rows_ref[0, sl, :]
                for a, keep in enumerate((first, jnp.logical_not(first))):
                    q_s[a, sl, 0:LANES] = jnp.where(keep, qf, 0.0).astype(BF16)
                    q_s[a, sl, LANES:2 * LANES] = _bias_lanes(rv[:, a:a + 1], True)
                    do_s[a, sl, :] = jnp.where(keep, dof, 0.0).astype(BF16)
                    dl_s[sl, a:a + 1] = jnp.sum(jnp.where(keep, prod, 0.0), axis=1, keepdims=True)
                return c

            lax.fori_loop(0, nb, prep, 0)

        dk_acc[...] = jnp.zeros_like(dk_acc)
        dv_acc[...] = jnp.zeros_like(dv_acc)
        dc_acc[...] = jnp.zeros_like(dc_acc)
        first = _first_head((t, LANES))
        kf = k_ref[...]
        kk = kf.astype(BF16)
        k_own = (jnp.where(first, kf, 0.0).astype(BF16), jnp.where(first, 0.0, kf).astype(BF16))
        ckv = rows_ref[0, pl.ds(pl.multiple_of(kb * t, t), t), :]
        kx = [jnp.concatenate([kk, _bias_lanes(ckv[:, a:a + 1], False)], axis=1) for a in range(2)]
        vv = v_ref[...].astype(BF16)

        def block(qb, masked):
            sl = pl.ds(pl.multiple_of(qb * t, t), t)
            rv = rows_ref[0, sl, :]
            dlv = dl_s[sl, :]
            dq_new = dq_acc[sl, :]
            for a in range(2):
                dob = do_s[a, sl, :]
                s = lax.dot_general(q_s[a, sl, :], kx[a], (((1,), (1,)), ((), ())), preferred_element_type=F32)
                if masked:
                    s = jnp.where(_causal(t), s, NEG_INF)
                p = jnp.exp2(s - rv[:, 2 + a:3 + a])
                dv_acc[...] += lax.dot_general(p.astype(BF16), dob, (((0,), (0,)), ((), ())),
                                               preferred_element_type=F32)
                dp = lax.dot_general(dob, vv, (((1,), (1,)), ((), ())), preferred_element_type=F32)
                ds = p * (dp - dlv[:, a:a + 1])
                dsb = ds.astype(BF16)
                dk_acc[...] += lax.dot_general(dsb, q_s[a, sl, 0:LANES], (((0,), (0,)), ((), ())),
                                               preferred_element_type=F32)
                dq_new = dq_new + jnp.dot(dsb, k_own[a], preferred_element_type=F32) * scale
                dcq_ref[0, sl, a:a + 1] += jnp.sum(ds, axis=1, keepdims=True)
                dc_acc[a:a + 1, :] -= jnp.sum(ds, axis=0, keepdims=True)
            dq_acc[sl, :] = dq_new

        block(kb, True)

        def rest(qb, c):
            block(qb, False)
            return c

        lax.fori_loop(kb + 1, nb, rest, 0)
        dk_ref[...] = (dk_acc[...] * (1.0 / LOG2E)).astype(BF16)
        dv_ref[...] = dv_acc[...].astype(BF16)
        dck_ref[0, 0] = dc_acc[0:1, :]
        dck_ref[1, 0] = dc_acc[1:2, :]

        @pl.when(kb == nb - 1)
        def _():
            dq_ref[...] = dq_acc[...].astype(BF16)

    full = lambda c0: pl.BlockSpec((S, LANES), lambda h, j, c0=c0: (0, c0 + h))
    blk = lambda c0: pl.BlockSpec((t, LANES), lambda h, j, c0=c0: (j, c0 + h))
    f32 = lambda *s: jax.ShapeDtypeStruct(s, F32)
    b16 = jax.ShapeDtypeStruct((S, D), BF16)
    outs, extra = _call(
        body, name="fox_attn_bwd",
        out_shape=(b16, b16, b16, f32(HP, S, 2), f32(2 * HP, nb, 1, t)),
        grid=(HP, nb),
        in_specs=[full(0), blk(HP), blk(2 * HP), pl.BlockSpec((1, S, 4), lambda h, j: (h, 0, 0)), full(0), full(0)],
        out_specs=(full(0), blk(0), blk(0), pl.BlockSpec((1, S, 2), lambda h, j: (h, 0, 0)),
                   pl.BlockSpec((2, 1, 1, t), lambda h, j: (h, j, 0, 0))),
        scratch_shapes=[pltpu.VMEM((2, S, 2 * LANES), BF16), pltpu.VMEM((2, S, LANES), BF16),
                        pltpu.VMEM((S, 2), F32), pltpu.VMEM((S, LANES), F32), pltpu.VMEM((t, LANES), F32),
                        pltpu.VMEM((t, LANES), F32), pltpu.VMEM((2, t), F32)],
        args=(proj, proj, proj, rows, o, do), semantics=("parallel", "arbitrary"), ex=ex)
    return (*outs, extra)


def _s5_consts(T):
    rows = T * SUBLANES
    rr = lax.broadcasted_iota(jnp.int32, (rows, T), 0)
    tt = lax.broadcasted_iota(jnp.int32, (rows, T), 1)
    rep = jnp.where(rr // SUBLANES == tt, 1.0, 0.0).astype(BF16)
    r2 = lax.broadcasted_iota(jnp.int32, (rows, S5_PART), 0)
    c2 = lax.broadcasted_iota(jnp.int32, (rows, S5_PART), 1)
    mask = (c2 // (S5_PART // SUBLANES)) == (r2 % SUBLANES)
    return rep, mask


def _gelu(y):
    c = math.sqrt(2.0 / math.pi)
    return 0.5 * y * (1.0 + jnp.tanh(c * (y + 0.044715 * y * y * y)))


def _gelu_grad(y):
    c = math.sqrt(2.0 / math.pi)
    th = jnp.tanh(c * (y + 0.044715 * y * y * y))
    return 0.5 * (1.0 + th) + 0.5 * y * (1.0 - th * th) * c * (1.0 + 3.0 * 0.044715 * y * y)


def _s5_fwd(x, rmat, cmat, lam, dskip, ex=None):
    S, D = x.shape
    NQ = D // S5_PART
    T = _tile(S, 128, SUBLANES)
    rows = T * SUBLANES

    def body(x_ref, r_ref, c_ref, lam_ref, d_ref, y_ref, yg_ref, h_ref, bu_s, carry):
        i = pl.program_id(0)

        @pl.when(i == 0)
        def _():
            carry[...] = jnp.zeros_like(carry)

        rep, mask = _s5_consts(T)
        cols = [pl.ds(q * S5_PART, S5_PART) for q in range(NQ)]
        for q in range(NQ):
            xrep = jnp.dot(rep, x_ref[:, cols[q]].astype(BF16), preferred_element_type=F32)
            lx = jnp.where(mask, xrep, 0.0).astype(BF16)
            bu_s[q] = jnp.dot(lx, r_ref[q], preferred_element_type=F32)
        lam_v = [(lam_ref[q, :, 0:LANES], lam_ref[q, :, LANES:2 * LANES]) for q in range(NQ)]

        def step(t, c):
            o = pl.multiple_of(t * SUBLANES, SUBLANES)
            new = []
            for q in range(NQ):
                hr, hi = c[q]
                ar, ai = lam_v[q]
                sl = bu_s[q, pl.ds(o, SUBLANES), :]
                nhr = ar * hr - ai * hi + sl[:, 0:LANES]
                nhi = ar * hi + ai * hr + sl[:, LANES:2 * LANES]
                h_ref[q, pl.ds(o, SUBLANES), 0:LANES] = nhr
                h_ref[q, pl.ds(o, SUBLANES), LANES:2 * LANES] = nhi
                new.append((nhr, nhi))
            return tuple(new)

        fin = lax.fori_loop(0, T, step,
                            tuple((carry[q, :, 0:LANES], carry[q, :, LANES:2 * LANES]) for q in range(NQ)))
        for q in range(NQ):
            carry[q, :, 0:LANES] = fin[q][0]
            carry[q, :, LANES:2 * LANES] = fin[q][1]
            z = jnp.dot(h_ref[q].astype(BF16), c_ref[q], preferred_element_type=F32)
            z = jnp.where(mask, z, 0.0)
            y = jnp.sum(z.reshape(T, SUBLANES, S5_PART), axis=1) + d_ref[:, cols[q]] * x_ref[:, cols[q]]
            y_ref[:, cols[q]] = y
            yg_ref[:, cols[q]] = _gelu(y).astype(BF16)

    xs = pl.BlockSpec((T, D), lambda i: (i, 0))
    ms = pl.BlockSpec((NQ, S5_PART, S5_PART), lambda i: (0, 0, 0))
    outs, extra = _call(
        body, name="s5_scan_fwd",
        out_shape=(jax.ShapeDtypeStruct((S, D), F32), jax.ShapeDtypeStruct((S, D), BF16),
                   jax.ShapeDtypeStruct((NQ, S * SUBLANES, S5_PART), F32)),
        grid=(S // T,),
        in_specs=[xs, ms, ms, pl.BlockSpec((NQ, SUBLANES, S5_PART), lambda i: (0, 0, 0)),
                  pl.BlockSpec((1, D), lambda i: (0, 0))],
        out_specs=(xs, xs, pl.BlockSpec((NQ, rows, S5_PART), lambda i: (0, i, 0))),
        scratch_shapes=[pltpu.VMEM((NQ, rows, S5_PART), F32), pltpu.VMEM((NQ, SUBLANES, S5_PART), F32)],
        args=(x, rmat, cmat, lam, dskip), semantics=("arbitrary",), ex=ex)
    return (*outs, extra)


def _s5_bwd(x, y, dyg, hs, rmat, cmat, lam, dskip, res, res_scale, ex=None):
    S, D = x.shape
    NQ = D // S5_PART
    T = _tile(S, 128, SUBLANES)
    nb = S // T
    rows = T * SUBLANES

    def body(x_ref, y_ref, dyg_ref, res_ref, h_ref, hp_ref, r_ref, c_ref, lam_ref, d_ref,
             dx_ref, dr_ref, dc_ref, dlam_ref, dd_ref, dh_s, g_s, hs_s, carry):
        i = pl.program_id(0)

        @pl.when(i == 0)
        def _():
            carry[...] = jnp.zeros_like(carry)
            dr_ref[...] = jnp.zeros_like(dr_ref)
            dc_ref[...] = jnp.zeros_like(dc_ref)
            dlam_ref[...] = jnp.zeros_like(dlam_ref)
            dd_ref[...] = jnp.zeros_like(dd_ref)

        rep, mask = _s5_consts(T)
        cols = [pl.ds(q * S5_PART, S5_PART) for q in range(NQ)]
        dys, ldys = [], []
        for q in range(NQ):
            dy = dyg_ref[:, cols[q]] * _gelu_grad(y_ref[:, cols[q]])
            dyrep = jnp.dot(rep, dy.astype(BF16), preferred_element_type=F32)
            ldy = jnp.where(mask, dyrep, 0.0).astype(BF16)
            dh_s[q] = lax.dot_general(ldy, c_ref[q], (((1,), (1,)), ((), ())), preferred_element_type=F32)
            dys.append(dy)
            ldys.append(ldy)
        lam_v = [(lam_ref[q, :, 0:LANES], lam_ref[q, :, LANES:2 * LANES]) for q in range(NQ)]

        def step(n, c):
            o = pl.multiple_of((T - 1 - n) * SUBLANES, SUBLANES)
            new = []
            for q in range(NQ):
                gr, gi = c[q]
                ar, ai = lam_v[q]
                sl = dh_s[q, pl.ds(o, SUBLANES), :]
                ngr = sl[:, 0:LANES] + ar * gr + ai * gi
                ngi = sl[:, LANES:2 * LANES] - ai * gr + ar * gi
                g_s[q, pl.ds(o, SUBLANES), 0:LANES] = ngr
                g_s[q, pl.ds(o, SUBLANES), LANES:2 * LANES] = ngi
                new.append((ngr, ngi))
            return tuple(new)

        fin = lax.fori_loop(0, T, step,
                            tuple((carry[q, :, 0:LANES], carry[q, :, LANES:2 * LANES]) for q in range(NQ)))
        for q in range(NQ):
            carry[q, :, 0:LANES] = fin[q][0]
            carry[q, :, LANES:2 * LANES] = fin[q][1]
            xv = x_ref[:, cols[q]]
            hv = h_ref[q]
            hs_s[0:SUBLANES, :] = jnp.where(i == nb - 1, 0.0, hp_ref[q])
            hs_s[SUBLANES:rows + SUBLANES, :] = hv
            hprev = hs_s[0:rows, :]
            gv = g_s[q]
            g_re, g_im = gv[:, 0:LANES], gv[:, LANES:2 * LANES]
            hp_re, hp_im = hprev[:, 0:LANES], hprev[:, LANES:2 * LANES]
            dar = jnp.sum((g_re * hp_re + g_im * hp_im).reshape(T, SUBLANES, LANES), axis=0)
            dai = jnp.sum((g_im * hp_re - g_re * hp_im).reshape(T, SUBLANES, LANES), axis=0)
            dlam_ref[q, :, 0:LANES] += dar
            dlam_ref[q, :, LANES:2 * LANES] += dai

            gb = gv.astype(BF16)
            xrep = jnp.dot(rep, xv.astype(BF16), preferred_element_type=F32)
            lx = jnp.where(mask, xrep, 0.0).astype(BF16)
            dr_ref[q] += lax.dot_general(lx, gb, (((0,), (0,)), ((), ())), preferred_element_type=F32)
            dc_ref[q] += lax.dot_general(hv.astype(BF16), ldys[q], (((0,), (0,)), ((), ())),
                                         preferred_element_type=F32)
            zx = lax.dot_general(gb, r_ref[q], (((1,), (1,)), ((), ())), preferred_element_type=F32)
            zx = jnp.where(mask, zx, 0.0)
            dx_ref[:, cols[q]] = (jnp.sum(zx.reshape(T, SUBLANES, S5_PART), axis=1) + d_ref[:, cols[q]] * dys[q]
                                  + res_scale * res_ref[:, cols[q]])
            dd_ref[:, cols[q]] += jnp.sum(dys[q] * xv, axis=0, keepdims=True)

    xs = pl.BlockSpec((T, D), lambda i: (nb - 1 - i, 0))
    ms = pl.BlockSpec((NQ, S5_PART, S5_PART), lambda i: (0, 0, 0))
    ls = pl.BlockSpec((NQ, SUBLANES, S5_PART), lambda i: (0, 0, 0))
    ds_ = pl.BlockSpec((1, D), lambda i: (0, 0))
    outs, extra = _call(
        body, name="s5_scan_bwd",
        out_shape=(jax.ShapeDtypeStruct((S, D), F32), jax.ShapeDtypeStruct((NQ, S5_PART, S5_PART), F32),
                   jax.ShapeDtypeStruct((NQ, S5_PART, S5_PART), F32),
                   jax.ShapeDtypeStruct((NQ, SUBLANES, S5_PART), F32), jax.ShapeDtypeStruct((1, D), F32)),
        grid=(nb,),
        in_specs=[xs, xs, xs, xs, pl.BlockSpec((NQ, rows, S5_PART), lambda i: (0, nb - 1 - i, 0)),
                  pl.BlockSpec((NQ, SUBLANES, S5_PART), lambda i: (0, jnp.maximum((nb - 1 - i) * T - 1, 0), 0)),
                  ms, ms, ls, ds_],
        out_specs=(xs, ms, ms, ls, ds_),
        scratch_shapes=[pltpu.VMEM((NQ, rows, S5_PART), F32), pltpu.VMEM((NQ, rows, S5_PART), F32),
                        pltpu.VMEM((rows + SUBLANES, S5_PART), F32), pltpu.VMEM((NQ, SUBLANES, S5_PART), F32)],
        args=(x, y, dyg, res, hs, hs, rmat, cmat, lam, dskip), semantics=("arbitrary",), ex=ex)
    return (*outs, extra)


def _s5_discretise(a_re, a_im, log_dt, b_re, b_im):
    dt = jnp.exp(log_dt)[:, None]
    mag = jnp.exp(a_re * dt)
    ang = a_im * dt
    lb_re = mag * jnp.cos(ang)
    lb_im = mag * jnp.sin(ang)
    den = a_re * a_re + a_im * a_im
    nr = lb_re - 1.0
    ni = lb_im
    z_re = (nr * a_re + ni * a_im) / den
    z_im = (ni * a_re - nr * a_im) / den
    bb_re = z_re[..., None] * b_re - z_im[..., None] * b_im
    bb_im = z_re[..., None] * b_im + z_im[..., None] * b_re
    return lb_re, lb_im, bb_re, bb_im


def _s5_expand(w):
    G = w.shape[0]
    NQ = G // 16
    base = w.reshape(NQ, S5_PART, S5_STATE)
    half = (jnp.arange(S5_PART) // S5_GROUP) % 2
    sel = (half[:, None] == jnp.arange(2)[None, :]).astype(w.dtype)
    out = base[:, :, None, :] * sel[None, :, :, None]
    return out.reshape(NQ, S5_PART, 2 * S5_STATE)


def _s5_extract(m):
    NQ = m.shape[0]
    half = (jnp.arange(S5_PART) // S5_GROUP) % 2
    sel = (half[:, None] == jnp.arange(2)[None, :]).astype(m.dtype)
    base = jnp.sum(m.reshape(NQ, S5_PART, 2, S5_STATE) * sel[None, :, :, None], axis=2)
    return base.reshape(NQ * 16, S5_GROUP, S5_STATE)


def _s5_slab(v):
    return v.reshape(v.shape[0] // 16, SUBLANES, LANES)


def _adamw(w, g, m, v, name):
    R, C = w.shape
    tr = _tile(R, 256, SUBLANES)
    c1 = 1.0 / (1.0 - ADAM_B1 ** ADAM_STEP)
    c2 = 1.0 / (1.0 - ADAM_B2 ** ADAM_STEP)

    def body(w_ref, g_ref, m_ref, v_ref, d_ref, nm_ref, nv_ref):
        gv = g_ref[...]
        nm = ADAM_B1 * m_ref[...] + (1.0 - ADAM_B1) * gv
        nv = ADAM_B2 * v_ref[...] + (1.0 - ADAM_B2) * (gv * gv)
        nm_ref[...] = nm
        nv_ref[...] = nv
        d_ref[...] = -ADAM_LR * ((nm * c1) / (jnp.sqrt(nv * c2) + ADAM_EPS) + ADAM_WD * w_ref[...])

    blk = pl.BlockSpec((tr, C), lambda i: (i, 0))
    sh = jax.ShapeDtypeStruct((R, C), F32)
    return pl.pallas_call(
        body, name=name, out_shape=(sh, sh, sh), grid=(R // tr,), in_specs=[blk] * 4, out_specs=(blk,) * 3,
        compiler_params=_params(("parallel",)),
    )(w, g, m, v)


def _adamw_recv(parts, w, m, v, name):
    L, R, C = w.shape
    tr = _tile(R, 256, SUBLANES)
    c1 = 1.0 / (1.0 - ADAM_B1 ** ADAM_STEP)
    c2 = 1.0 / (1.0 - ADAM_B2 ** ADAM_STEP)

    def body(*refs):
        p_refs = refs[:L]
        w_ref, m_ref, v_ref, g_ref, d_ref, nm_ref, nv_ref = refs[L:]
        li = pl.program_id(0)
        for l in range(L):
            @pl.when(li == l)
            def _(p_ref=p_refs[l]):
                gv = p_ref[0].astype(F32)
                for k in range(1, N_DEV):
                    gv = gv + p_ref[k].astype(F32)
                g_ref[...] = gv
                nm = ADAM_B1 * m_ref[...] + (1.0 - ADAM_B1) * gv
                nv = ADAM_B2 * v_ref[...] + (1.0 - ADAM_B2) * (gv * gv)
                nm_ref[...] = nm
                nv_ref[...] = nv
                d_ref[...] = -ADAM_LR * ((nm * c1) / (jnp.sqrt(nv * c2) + ADAM_EPS) + ADAM_WD * w_ref[...])

    p_specs = [pl.BlockSpec((N_DEV, tr, C), lambda li, i, l=l: (0, jnp.where(li == l, i, 0), 0)) for l in range(L)]
    blk = pl.BlockSpec((None, tr, C), lambda li, i: (li, i, 0))
    sh = jax.ShapeDtypeStruct((L, R, C), F32)
    return pl.pallas_call(
        body, name=name, out_shape=(sh, sh, sh, sh), grid=(L, R // tr),
        in_specs=p_specs + [blk, blk, blk], out_specs=(blk,) * 4,
        compiler_params=_params(("parallel", "parallel")),
    )(*parts, w, m, v)


def _sum8(parts, name):
    _, R, C = parts.shape
    tr = _tile(R, 256, SUBLANES)

    def body(p_ref, o_ref):
        acc = p_ref[0].astype(F32)
        for k in range(1, N_DEV):
            acc = acc + p_ref[k].astype(F32)
        o_ref[...] = acc

    return pl.pallas_call(
        body, name=name, out_shape=jax.ShapeDtypeStruct((R, C), F32), grid=(R // tr,),
        in_specs=[pl.BlockSpec((N_DEV, tr, C), lambda i: (0, i, 0))],
        out_specs=pl.BlockSpec((tr, C), lambda i: (i, 0)), compiler_params=_params(("parallel",)),
    )(parts)


def _peers():
    x, y, c = lax.axis_index("x"), lax.axis_index("y"), lax.axis_index("c")
    me = 4 * x + 2 * y + c
    out = []
    for k in range(1, N_DEV):
        kx, ky, kc = (k >> 2) & 1, (k >> 1) & 1, k & 1
        px, py, pc = x ^ kx, y ^ ky, c ^ kc
        out.append(((px, py, pc), 4 * px + 2 * py + pc))
    return me, out


SIBLING = 1
SAME_CORE = (2, 4, 6)


class Exchange:
    def __init__(self, xs, scatter):
        self.xs = list(xs)
        self.scatter = list(scatter)
        self.n = len(self.xs)

    def out_shapes(self):
        return tuple(jax.ShapeDtypeStruct(x.shape if sc else (N_DEV,) + x.shape, x.dtype)
                     for x, sc in zip(self.xs, self.scatter))

    def sems(self):
        return [pltpu.SemaphoreType.DMA((self.n * N_DEV,)), pltpu.SemaphoreType.DMA((self.n * N_DEV,)),
                pltpu.SemaphoreType.DMA((self.n,))]

    def _copy(self, i, m, src, dst, to, send_sems, recv_sems):
        return pltpu.make_async_remote_copy(src_ref=src, dst_ref=dst, send_sem=send_sems.at[i * N_DEV + m],
                                            recv_sem=recv_sems.at[i * N_DEV + m], device_id=to, device_id_type=MESH)

    def start(self, x_refs, o_refs, send_sems, recv_sems, local_sems):
        me, peers = _peers()
        for i, (x_ref, o_ref, sc) in enumerate(zip(x_refs, o_refs, self.scatter)):
            pltpu.make_async_copy(x_ref.at[me] if sc else x_ref, o_ref.at[me], local_sems.at[i]).start()
            for m in (range(1, N_DEV) if sc else (SIBLING,) + SAME_CORE):
                dev, idx = peers[m - 1]
                self._copy(i, m, x_ref.at[idx] if sc else x_ref, o_ref.at[me], dev, send_sems, recv_sems).start()

    def middle(self, x_refs, o_refs, send_sems, recv_sems, local_sems):
        me, peers = _peers()
        sibling = peers[SIBLING - 1][0]
        for i, (x_ref, o_ref, sc) in enumerate(zip(x_refs, o_refs, self.scatter)):
            if sc:
                continue
            for m in SAME_CORE:
                dev, idx = peers[m - 1]
                self._copy(i, m, x_ref, o_ref.at[idx], dev, send_sems, recv_sems).wait_recv()
                self._copy(i, m ^ 1, o_ref.at[idx], o_ref.at[idx], sibling, send_sems, recv_sems).start()

    def wait(self, x_refs, o_refs, send_sems, recv_sems, local_sems):
        me, peers = _peers()
        for i, (x_ref, o_ref, sc) in enumerate(zip(x_refs, o_refs, self.scatter)):
            for m in range(1, N_DEV):
                dev, idx = peers[m - 1]
                cp = self._copy(i, m, x_ref.at[idx] if sc else x_ref, o_ref.at[idx], dev, send_sems, recv_sems)
                if sc or m not in SAME_CORE:
                    cp.wait_recv()
                cp.wait_send()
            pltpu.make_async_copy(x_ref.at[me] if sc else x_ref, o_ref.at[me], local_sems.at[i]).wait()


def _exchange(ex, name):
    n = ex.n

    def body(*refs):
        x_refs, o_refs, sems = refs[:n], refs[n:2 * n], refs[2 * n:]
        ex.start(x_refs, o_refs, *sems)
        ex.middle(x_refs, o_refs, *sems)
        ex.wait(x_refs, o_refs, *sems)

    hbm = pl.BlockSpec(memory_space=pltpu.HBM)
    return pl.pallas_call(body, name=name, out_shape=ex.out_shapes(), in_specs=[hbm] * n, out_specs=(hbm,) * n,
                          scratch_shapes=ex.sems())(*ex.xs)


def _call(body, *, name, out_shape, grid, in_specs, out_specs, scratch_shapes, args, semantics, ex=None):
    if ex is None:
        return pl.pallas_call(body, name=name, out_shape=out_shape, grid=grid, in_specs=in_specs, out_specs=out_specs,
                              scratch_shapes=scratch_shapes, compiler_params=_params(semantics))(*args), ()
    n, ni, no, ns = ex.n, len(args), len(out_shape), len(scratch_shapes)

    def wrapped(*refs):
        ins, cx = refs[:ni], refs[ni:ni + n]
        outs, co = refs[ni + n:ni + n + no], refs[ni + n + no:ni + 2 * n + no]
        scratch, sems = refs[ni + 2 * n + no:ni + 2 * n + no + ns], refs[ni + 2 * n + no + ns:]
        step = functools.reduce(lambda acc, a: acc * grid[a] + pl.program_id(a), range(len(grid)), 0)
        steps = math.prod(grid)

        @pl.when(step == 0)
        def _():
            ex.start(cx, co, *sems)

        body(*ins, *outs, *scratch)

        @pl.when(step == (steps * 3) // 4 - (steps > 1))
        def _():
            ex.middle(cx, co, *sems)

        @pl.when(step == steps - 1)
        def _():
            ex.wait(cx, co, *sems)

    hbm = pl.BlockSpec(memory_space=pltpu.HBM)
    res = pl.pallas_call(
        wrapped, name=name, out_shape=tuple(out_shape) + ex.out_shapes(), grid=grid,
        in_specs=list(in_specs) + [hbm] * n, out_specs=tuple(out_specs) + (hbm,) * n,
        scratch_shapes=list(scratch_shapes) + ex.sems(),
        compiler_params=_params(("arbitrary",) * len(grid)))(*args, *ex.xs)
    return res[:no], res[no:]


def _pack_flat(arrs):
    cat = jnp.concatenate([a.reshape(-1) for a in arrs])
    per = PACK_COLS * 2 * SUBLANES
    tot = -(-cat.shape[0] // per) * per
    return jnp.pad(cat, (0, tot - cat.shape[0])).reshape(tot // PACK_COLS, PACK_COLS)


def _unpack_flat(packed, shapes):
    flat = packed.reshape(-1)
    out, o = [], 0
    for s in shapes:
        n = math.prod(s)
        out.append(flat[o:o + n].reshape(s))
        o += n
    return out


def _ffn_fwd(x, xb, wt, wo4, g, b, alpha, tag, ex=None):
    h8, a4, extra = _ffn_h(xb, wt, tag + "_h", ex)
    out, outb, xh, rstd = _ffn_y_ln(a4, wo4, x, g, b, alpha, 0.5, tag + "_y_ln")
    return out, outb, (xb, h8, a4, xh, rstd), extra


def _ffn_bwd(dout, saved, wt, wo4, g, alpha, tag, carry=None, last=None):
    x, h8, a4, xh, rstd = saved
    load = dict(carry or {})
    slots = [[], [], [], []]
    for n, k in enumerate(sorted(load, key=lambda k: -load[k].size)):
        slots[min(n, 3)].append(k)
    arrived = {}

    def run(fn, slot, *args):
        keys = slots[slot]
        if not keys:
            return fn(*args)
        res, extra = fn(*args, ex=Exchange([load[k] for k in keys], [k != "small" for k in keys]))
        arrived.update(zip(keys, extra))
        return res

    dz, dg, db = _ln_bwd(dout, xh, rstd, g, tag + "_ln_bwd")
    if last is not None:
        (key_out, key_in), load["small"] = last(dg, db)
        slots[3].append("small")
    dw_out = run(_ffn_dwout, 3, a4, dz, 0.5, tag + "_dwout")
    dw_out = dw_out.reshape(N_DEV, -1, dw_out.shape[-1])
    if last is not None:
        load[key_out] = dw_out
        slots[2].append(key_out)
    dh8 = run(_ffn_dh, 2, dz, wo4, h8, 0.5, tag + "_dh")
    dw_in = run(_ffn_dwin, 0, x, dh8, tag + "_dwin")
    if last is not None:
        load[key_in] = dw_in
        slots[1].append(key_in)
    dx = run(_ffn_dx, 1, dh8, wt, dz, alpha, tag + "_dx")
    return dx, dw_in, dw_out, dg, db, arrived


def _fox_fwd(x, xb, w_in_pad, b_f_pad, w_o, g, b, alpha, tag, ex=None):
    S, D = x.shape
    H = D // HEAD_DIM
    proj = _mm(xb, w_in_pad, name=tag + "_proj")
    fl = proj[:, 3 * D:]
    cum = _fox_gate_fwd(fl, b_f_pad)
    cum2 = (cum[:, :H].T * LOG2E).reshape(H // 2, 2, S).transpose(0, 2, 1)
    o, lse, extra = _flash_fwd(proj, cum2, ex)
    m = _mm(o, w_o, name=tag + "_out")
    out, outb, xh, rstd = _res_ln_fwd(x, m, g, b, alpha, 1.0, tag + "_ln")
    return out, outb, (xb, proj, jnp.concatenate([cum2, lse], axis=2), o, fl, xh, rstd), extra


def _fox_bwd(dout, saved, w_in_pad, b_f_pad, w_o, g, alpha, tag, ex=None):
    x, proj, rows, o, fl, xh, rstd = saved
    S, D = x.shape
    H = D // HEAD_DIM
    dz, dg, db = _ln_bwd(dout, xh, rstd, g, tag + "_ln_bwd")
    dw_o = _mm(o, dz, ta=True, name=tag + "_dwo")
    do = _mm(dz, w_o, tb=True, name=tag + "_do")
    dq, dk, dv, dcq, dck, extra = _flash_bwd(proj, rows, o, do, ex)
    dcq = dcq.transpose(0, 2, 1).reshape(H, S)
    dcum = jnp.pad((dcq + dck.reshape(H, S)).T, ((0, 0), (0, LANES - H)))
    dfl, dbf = _fox_gate_bwd(dcum, fl, b_f_pad)
    dproj = jnp.concatenate([dq, dk, dv, dfl.astype(BF16)], axis=1)
    dw_in = _mm(x, dproj, ta=True, name=tag + "_dwin")
    dx = _mm(dproj, w_in_pad, tb=True, add=dz, add_scale=alpha, name=tag + "_dx")
    shards = {"fox_w_in": _split(dw_in[None, :, :3 * D + H], True)[:, 0].astype(BF16),
              "fox_w_o": _split(dw_o[None], False)[:, 0].astype(BF16)}
    return dx, shards, {"fox_b_f": dbf[0, :H], "lnm_g": dg, "lnm_b": db}, extra


def _s5_mats(p):
    lb_re, lb_im, bb_re, bb_im = _s5_discretise(p["a_re"], p["a_im"], p["log_dt"], p["b_re"], p["b_im"])
    rmat = jnp.concatenate([_s5_expand(bb_re.transpose(0, 2, 1)), _s5_expand(bb_im.transpose(0, 2, 1))], axis=2)
    cmat = jnp.concatenate([_s5_expand(p["c_re"]).transpose(0, 2, 1), -_s5_expand(p["c_im"]).transpose(0, 2, 1)],
                           axis=1)
    lam = jnp.concatenate([_s5_slab(lb_re), _s5_slab(lb_im)], axis=2)
    return rmat.astype(BF16), cmat.astype(BF16), lam


def _s5_block_fwd(x, p, w_out, g, b, alpha, tag, ex=None):
    S, D = x.shape
    rmat, cmat, lam = _s5_mats(p)
    dskip = p["d"].reshape(1, D)
    y, yg, hs, extra = _s5_fwd(x, rmat, cmat, lam, dskip, ex)
    vg = _mm(yg, w_out, name=tag + "_vg")
    m = _glu_fwd(vg, tag + "_glu")
    out, outb, xh, rstd = _res_ln_fwd(x, m, g, b, alpha, 1.0, tag + "_ln")
    return out, outb, (x, y, yg, hs, vg, rmat, cmat, lam, dskip, xh, rstd), extra


def _s5_block_bwd(dout, saved, p, w_out, g, alpha, tag, ex=None):
    x, y, yg, hs, vg, rmat, cmat, lam, dskip, xh, rstd = saved
    S, D = x.shape
    G = D // S5_GROUP
    dz, dg, db = _ln_bwd(dout, xh, rstd, g, tag + "_ln_bwd")
    dvg = _glu_bwd(vg, dz, tag + "_glu_bwd")
    dw_out = _mm(yg, dvg, ta=True, name=tag + "_dwout")
    dyg = _mm(dvg, w_out, tb=True, name=tag + "_dyg")
    dx, dr, dc, dlam, dd, extra = _s5_bwd(x, y, dyg, hs, rmat, cmat, lam, dskip, dz, alpha, ex)
    dbb_re = _s5_extract(dr[:, :, :LANES]).transpose(0, 2, 1)
    dbb_im = _s5_extract(dr[:, :, LANES:]).transpose(0, 2, 1)
    dc_re = _s5_extract(dc[:, :LANES, :].transpose(0, 2, 1))
    dc_im = -_s5_extract(dc[:, LANES:, :].transpose(0, 2, 1))
    dlb_re = dlam[:, :, :LANES].reshape(G, S5_STATE)
    dlb_im = dlam[:, :, LANES:].reshape(G, S5_STATE)
    _, vjp = jax.vjp(_s5_discretise, p["a_re"], p["a_im"], p["log_dt"], p["b_re"], p["b_im"])
    da_re, da_im, dlog_dt, db_re, db_im = vjp((dlb_re, dlb_im, dbb_re, dbb_im))
    small = dict(s5_a_re=da_re, s5_a_im=da_im, s5_log_dt=dlog_dt, s5_b_re=db_re, s5_b_im=db_im, s5_c_re=dc_re,
                 s5_c_im=dc_im, s5_d=dd.reshape(G, S5_GROUP), lnm_g=dg, lnm_b=db)
    return dx, {"s5_w_out": _split(dw_out[None], True)[:, 0].astype(BF16)}, small, extra


FFN_NAMES = ("ffn1_w_in", "ffn1_w_out", "ffn2_w_in", "ffn2_w_out")
BIG = FFN_NAMES + ("fox_w_in", "fox_w_o", "s5_w_out")
BIG_SPLIT_COLS = {"ffn1_w_in": True, "ffn1_w_out": False, "ffn2_w_in": True, "ffn2_w_out": False,
                  "fox_w_in": True, "fox_w_o": False, "s5_w_out": True}
SMALL = ("ln1_g", "ln1_b", "lnm_g", "lnm_b", "ln2_g", "ln2_b", "fox_b_f", "s5_a_re", "s5_a_im", "s5_log_dt",
         "s5_b_re", "s5_b_im", "s5_c_re", "s5_c_im", "s5_d")
WEIGHTS = ("ffn1_w_in", "ffn1_w_out", "ln1_g", "ln1_b", "lnm_g", "lnm_b", "ffn2_w_in", "ffn2_w_out", "ln2_g", "ln2_b",
           "fox_w_in", "fox_b_f", "fox_w_o", "s5_a_re", "s5_a_im", "s5_log_dt", "s5_b_re", "s5_b_im", "s5_c_re",
           "s5_c_im", "s5_d", "s5_w_out")


def _join(gathered, split_cols):
    n, L, r, c = gathered.shape
    if split_cols:
        return gathered.transpose(1, 2, 0, 3).reshape(L, r, n * c)
    return gathered.transpose(1, 0, 2, 3).reshape(L, n * r, c)


def _split(full, split_cols):
    L, R, C = full.shape
    if split_cols:
        return full.reshape(L, R, N_DEV, C // N_DEV).transpose(2, 0, 1, 3)
    return full.reshape(L, N_DEV, R // N_DEV, C).transpose(1, 0, 2, 3)


def _group(i, part):
    if part == "mixer":
        return (("fox_w_in", i // 2), ("fox_w_o", i // 2)) if i % 2 == 0 else (("s5_w_out", i // 2),)
    return ((part + "_w_in", i), (part + "_w_out", i))


def _prepare(name, g8):
    if name in ("ffn1_w_in", "ffn2_w_in"):
        return g8
    if name in FFN_NAMES:
        n, r, c = g8.shape
        return g8.reshape(n // 2, 2 * r, c)
    full = _join(g8[:, None], BIG_SPLIT_COLS[name])[0]
    if name == "fox_w_in":
        full = jnp.pad(full, ((0, 0), (0, LANES - full.shape[0] // HEAD_DIM)))
    return full


def _local_step(x, target, small, shard_of=None, pregathered=None):
    S, D = x.shape
    H = D // HEAD_DIM
    depth = small["ln1_g"].shape[0]
    alpha = (2.0 * depth) ** 0.25
    local = pregathered is not None
    bf_pad = jnp.pad(small["fox_b_f"], ((0, 0), (0, LANES - H)))

    def s5_params(j):
        return {k: small["s5_" + k][j] for k in ("a_re", "a_im", "log_dt", "b_re", "b_im", "c_re", "c_im", "d")}

    if local:
        W = {k: _prepare(k[0], g8) for k, g8 in pregathered.items()}
    else:
        keys = _group(0, "ffn1")
        got = _exchange(Exchange([shard_of(*k) for k in keys], [False] * len(keys)), "gather_first")
        W = {k: _prepare(k[0], g8) for k, g8 in zip(keys, got)}

    def gather(keys):
        return None if local else Exchange([shard_of(*k) for k in keys], [False] * len(keys))

    def landed(keys, extra):
        if not local:
            W.update({k: _prepare(k[0], g8) for k, g8 in zip(keys, extra)})

    saved = []
    h, hb = x, x.astype(BF16)
    for i in range(depth):
        j = i // 2
        keys = _group(i, "mixer")
        h, hb, s1, extra = _ffn_fwd(h, hb, W[("ffn1_w_in", i)], W[("ffn1_w_out", i)], small["ln1_g"][i],
                                    small["ln1_b"][i], alpha, f"l{i}_ffn1", gather(keys))
        landed(keys, extra)
        keys = _group(i, "ffn2") + (_group(i + 1, "ffn1") if i + 1 < depth else ())
        ex = gather(keys)
        if i % 2 == 0:
            h, hb, s2, extra = _fox_fwd(h, hb, W[("fox_w_in", j)], bf_pad[j:j + 1], W[("fox_w_o", j)],
                                        small["lnm_g"][i], small["lnm_b"][i], alpha, f"l{i}_fox", ex)
        else:
            h, hb, s2, extra = _s5_block_fwd(h, s5_params(j), W[("s5_w_out", j)], small["lnm_g"][i],
                                             small["lnm_b"][i], alpha, f"l{i}_s5", ex)
        landed(keys, extra)
        h, hb, s3, _ = _ffn_fwd(h, hb, W[("ffn2_w_in", i)], W[("ffn2_w_out", i)], small["ln2_g"][i],
                                small["ln2_b"][i], alpha, f"l{i}_ffn2")
        saved.append((s1, s2, s3))

    dh, loss_part = _loss_fwd_bwd(h, target)

    arrived = {}
    pending = {}
    gs = {k: [None] * small[k].shape[0] for k in SMALL}
    for i in reversed(range(depth)):
        j = i // 2
        s1, s2, s3 = saved[i]
        held = {k: pending.pop(k) for k in [("ffn1_w_in", i + 1)] if i % 2 == 0 and k in pending}
        dh, dw_in, dw_out, gs["ln2_g"][i], gs["ln2_b"][i], got = _ffn_bwd(
            dh, s3, W[("ffn2_w_in", i)], W[("ffn2_w_out", i)], small["ln2_g"][i], alpha, f"l{i}_ffn2",
            None if local else pending)
        arrived.update(pending if local else got)
        pending = {("ffn2_w_in", i): dw_in, ("ffn2_w_out", i): dw_out, **held}
        keys = list(pending)
        ex = None if local else Exchange([pending[k] for k in keys], [True] * len(keys))
        if i % 2 == 0:
            dh, mix, sg, extra = _fox_bwd(dh, s2, W[("fox_w_in", j)], bf_pad[j:j + 1], W[("fox_w_o", j)],
                                          small["lnm_g"][i], alpha, f"l{i}_fox", ex)
        else:
            dh, mix, sg, extra = _s5_block_bwd(dh, s2, s5_params(j), W[("s5_w_out", j)], small["lnm_g"][i], alpha,
                                               f"l{i}_s5", ex)
        arrived.update(zip(keys, [pending[k] for k in keys] if local else extra))
        pending = {(k, j): val for k, val in mix.items()}
        for k, val in sg.items():
            gs[k][i if k in ("lnm_g", "lnm_b") else j] = val
        def last(dg, db):
            gs["ln1_g"][0], gs["ln1_b"][0] = dg, db
            packed = _pack_flat([jnp.stack(gs[k]) for k in SMALL]).astype(BF16)
            return (("ffn1_w_out", 0), ("ffn1_w_in", 0)), packed

        dh, dw_in, dw_out, gs["ln1_g"][i], gs["ln1_b"][i], got = _ffn_bwd(
            dh, s1, W[("ffn1_w_in", i)], W[("ffn1_w_out", i)], small["ln1_g"][i], alpha, f"l{i}_ffn1",
            None if local else pending, last if (i == 0 and not local) else None)
        arrived.update(pending if local else got)
        pending = {("ffn1_w_in", i): dw_in, ("ffn1_w_out", i): dw_out}
    if local:
        arrived.update(pending)
    return loss_part, dh, arrived, {k: jnp.stack(v) for k, v in gs.items()}


def kernel(x, ffn1_w_in, ffn1_w_out, ln1_g, ln1_b, lnm_g, lnm_b, ffn2_w_in, ffn2_w_out, ln2_g, ln2_b, fox_w_in, fox_b_f, fox_w_o, s5_a_re, s5_a_im, s5_log_dt, s5_b_re, s5_b_im, s5_c_re, s5_c_im, s5_d, s5_w_out, loss_target, m_ffn1_w_in, m_ffn1_w_out, m_ln1_g, m_ln1_b, m_lnm_g, m_lnm_b, m_ffn2_w_in, m_ffn2_w_out, m_ln2_g, m_ln2_b, m_fox_w_in, m_fox_b_f, m_fox_w_o, m_s5_a_re, m_s5_a_im, m_s5_log_dt, m_s5_b_re, m_s5_b_im, m_s5_c_re, m_s5_c_im, m_s5_d, m_s5_w_out, v_ffn1_w_in, v_ffn1_w_out, v_ln1_g, v_ln1_b, v_lnm_g, v_lnm_b, v_ffn2_w_in, v_ffn2_w_out, v_ln2_g, v_ln2_b, v_fox_w_in, v_fox_b_f, v_fox_w_o, v_s5_a_re, v_s5_a_im, v_s5_log_dt, v_s5_b_re, v_s5_b_im, v_s5_c_re, v_s5_c_im, v_s5_d, v_s5_w_out):
    args = dict(locals())
    w = {k: args[k] for k in WEIGHTS}
    m = {k: args["m_" + k] for k in WEIGHTS}
    v = {k: args["v_" + k] for k in WEIGHTS}
    small = {k: w[k] for k in SMALL}

    turned = lambda d: {k: jnp.swapaxes(a, 1, 2) if k in ("ffn1_w_in", "ffn2_w_in") else a for k, a in d.items()}
    w, m, v = turned(w), turned(m), turned(v)
    wb = {k: w[k].astype(BF16) for k in BIG}
    loss_part, dx, arrived, _ = _local_step(x[0], loss_target[0], small, lambda k, idx: wb[k][idx])
    loss = lax.psum(loss_part, ("x", "y", "c"))
    small_shapes = [w[k].shape for k in SMALL]
    g_small_flat = _sum8(arrived["small"], "sum_small_grads")

    grads, delta, new_m, new_v = {}, {}, {}, {}
    for k in BIG:
        parts = [arrived[(k, l)] for l in range(w[k].shape[0])]
        grads[k], delta[k], new_m[k], new_v[k] = _adamw_recv(parts, w[k], m[k], v[k], "adamw_" + k)
    pk = lambda d: _pack_flat([d[k] for k in SMALL])
    d_, m_, v_ = _adamw(pk(w), g_small_flat, pk(m), pk(v), "adamw_small")
    for dst, flat in ((grads, g_small_flat), (delta, d_), (new_m, m_), (new_v, v_)):
        dst.update(zip(SMALL, _unpack_flat(flat, small_shapes)))
    grads, delta, new_m, new_v = turned(grads), turned(delta), turned(new_m), turned(new_v)

    return (loss, dx[None], *[grads[k] for k in WEIGHTS], *[delta[k] for k in WEIGHTS],
            *[new_m[k] for k in WEIGHTS], *[new_v[k] for k in WEIGHTS])
```

```python
import functools
import math

import jax
import jax.numpy as jnp
from jax import lax
from jax.experimental import pallas as pl
from jax.experimental.pallas import tpu as pltpu

F32 = jnp.float32
BF16 = jnp.bfloat16

N_DEV = 8
HEAD_DIM = 64
S5_GROUP = 16
S5_STATE = 64
LANES = 128
SUBLANES = 8
S5_PART = 256
ATTN_TILE = 512
LN_EPS = 1e-5
NEG_INF = -1e30
LOG2E = 1.4426950408889634
ADAM_LR, ADAM_B1, ADAM_B2, ADAM_EPS, ADAM_WD, ADAM_STEP = 0.001, 0.9, 0.999, 1e-08, 0.01, 10
VMEM_LIMIT = 48 * 1024 * 1024
PACK_COLS = 1024

MESH = pl.DeviceIdType.MESH


def _tile(dim, pref, align=LANES):
    if dim <= pref:
        return dim
    t = (pref // align) * align
    while t >= align:
        if dim % t == 0:
            return t
        t -= align
    return dim


def _params(sem):
    return pltpu.CompilerParams(dimension_semantics=sem, vmem_limit_bytes=VMEM_LIMIT)


def _mm(a, b, *, ta=False, tb=False, out_dtype=F32, scale=None, add=None, add_scale=1.0,
        tm=512, tn=1408, tk=1408, name="mm"):
    if ta:
        K, M = a.shape
    else:
        M, K = a.shape
    if tb:
        N, K2 = b.shape
    else:
        K2, N = b.shape
    assert K == K2, (a.shape, b.shape, ta, tb)
    tm, tn, tk = _tile(M, tm), _tile(N, tn), _tile(K, tk)
    a_spec = pl.BlockSpec((tk, tm), lambda i, j, k: (k, i)) if ta else pl.BlockSpec((tm, tk), lambda i, j, k: (i, k))
    b_spec = pl.BlockSpec((tn, tk), lambda i, j, k: (j, k)) if tb else pl.BlockSpec((tk, tn), lambda i, j, k: (k, j))
    o_spec = pl.BlockSpec((tm, tn), lambda i, j, k: (i, j))
    return _mm_core(name, a, b, a_spec, b_spec, o_spec, jax.ShapeDtypeStruct((M, N), out_dtype),
                    (M // tm, N // tn, K // tk), (tm, tn), ta, tb, scale, add, add_scale)


def _mm_core(name, a, b, a_spec, b_spec, o_spec, out_shape, grid, acc_shape, ta, tb, scale=None, add=None,
             add_scale=1.0, ex=None):
    nk = grid[2]

    def body(*refs):
        if add is None:
            a_ref, b_ref, o_ref, acc = refs
            add_ref = None
        else:
            a_ref, b_ref, add_ref, o_ref, acc = refs
        k = pl.program_id(2)

        @pl.when(k == 0)
        def _():
            acc[...] = jnp.zeros_like(acc)

        dims = (((0 if ta else 1,), (1 if tb else 0,)), ((), ()))
        if len(a_ref.shape) == 3:
            part = sum(lax.dot_general(a_ref[u].astype(BF16), b_ref[u].astype(BF16), dims, preferred_element_type=F32)
                       for u in range(a_ref.shape[0]))
        else:
            part = lax.dot_general(a_ref[...].astype(BF16), b_ref[...].astype(BF16), dims, preferred_element_type=F32)
        acc[...] += part

        @pl.when(k == nk - 1)
        def _():
            r = acc[...]
            if scale is not None:
                r = r * scale
            if add_ref is not None:
                r = r + add_scale * add_ref[...]
            o_ref[...] = r.astype(out_shape.dtype)

    in_specs = [a_spec, b_spec]
    args = [a, b]
    if add is not None:
        in_specs.append(o_spec)
        args.append(add)
    (res,), extra = _call(body, name=name, out_shape=(out_shape,), grid=grid, in_specs=in_specs, out_specs=(o_spec,),
                          scratch_shapes=[pltpu.VMEM(acc_shape, F32)], args=args,
                          semantics=("parallel", "parallel", "arbitrary"), ex=ex)
    return res if ex is None else (res, extra)


def _ffn_h(x, wt, name, ex=None):
    S, D = x.shape
    n, c, _ = wt.shape
    nb = n // 2
    tm = _tile(S, 1024, SUBLANES)
    dims = (((1,), (1,)), ((), ()))

    def body(x_ref, wg_ref, wu_ref, h_ref, a_ref):
        xb = x_ref[...]
        g = lax.dot_general(xb, wg_ref[...], dims, preferred_element_type=F32).astype(BF16)
        u = lax.dot_general(xb, wu_ref[...], dims, preferred_element_type=F32).astype(BF16)
        h_ref[0] = g
        h_ref[1] = u
        g = g.astype(F32)
        a_ref[...] = (g * _sigmoid(g) * u.astype(F32)).astype(BF16)

    (h, a4), extra = _call(
        body, name=name,
        out_shape=(jax.ShapeDtypeStruct((2, nb, S, c), BF16), jax.ShapeDtypeStruct((nb, S, c), BF16)),
        grid=(S // tm, nb),
        in_specs=[pl.BlockSpec((tm, D), lambda i, j: (i, 0)), pl.BlockSpec((None, c, D), lambda i, j: (j, 0, 0)),
                  pl.BlockSpec((None, c, D), lambda i, j: (j + nb, 0, 0))],
        out_specs=(pl.BlockSpec((2, None, tm, c), lambda i, j: (0, j, i, 0)),
                   pl.BlockSpec((None, tm, c), lambda i, j: (j, i, 0))),
        scratch_shapes=[], args=(x, wt, wt), semantics=("parallel", "parallel"), ex=ex)
    return h.reshape(n, S, c), a4, extra


def _ffn_y_ln(a4, wo4, x, g, b, alpha, s, name):
    nb, S, c = a4.shape
    D = wo4.shape[-1]
    tm = _tile(S, 512, SUBLANES)

    def body(a_ref, w_ref, x_ref, g_ref, b_ref, o_ref, ob_ref, xh_ref, r_ref):
        y = jnp.dot(a_ref[0], w_ref[0], preferred_element_type=F32)
        for k in range(1, nb):
            y = y + jnp.dot(a_ref[k], w_ref[k], preferred_element_type=F32)
        z = alpha * x_ref[...] + s * y
        mu = jnp.mean(z, axis=-1, keepdims=True)
        zc = z - mu
        rstd = lax.rsqrt(jnp.mean(zc * zc, axis=-1, keepdims=True) + LN_EPS)
        xh = zc * rstd
        xh_ref[...] = xh
        r_ref[...] = rstd
        out = xh * g_ref[...] + b_ref[...]
        o_ref[...] = out
        ob_ref[...] = out.astype(BF16)

    row = pl.BlockSpec((tm, D), lambda i: (i, 0))
    vec = pl.BlockSpec((1, D), lambda i: (0, 0))
    return pl.pallas_call(
        body, name=name,
        out_shape=(jax.ShapeDtypeStruct((S, D), F32), jax.ShapeDtypeStruct((S, D), BF16),
                   jax.ShapeDtypeStruct((S, D), F32), jax.ShapeDtypeStruct((S, 1), F32)),
        grid=(S // tm,),
        in_specs=[pl.BlockSpec((nb, tm, c), lambda i: (0, i, 0)), pl.BlockSpec((nb, c, D), lambda i: (0, 0, 0)),
                  row, vec, vec],
        out_specs=(row, row, row, pl.BlockSpec((tm, 1), lambda i: (i, 0))),
        compiler_params=_params(("parallel",)),
    )(a4, wo4, x, g.reshape(1, D), b.reshape(1, D))


def _ffn_dwout(a4, dz, scale, name, ex=None):
    nb, S, c = a4.shape
    D = dz.shape[1]
    tk, tn = _tile(S, 2048, SUBLANES), _tile(D, 1024)
    return _mm_core(name, a4, dz, pl.BlockSpec((None, tk, c), lambda i, j, k: (i, k, 0)),
                    pl.BlockSpec((tk, tn), lambda i, j, k: (k, j)),
                    pl.BlockSpec((None, c, tn), lambda i, j, k: (i, 0, j)),
                    jax.ShapeDtypeStruct((nb, c, D), BF16), (nb, D // tn, S // tk), (c, tn), True, False, scale,
                    ex=ex)


def _ffn_dh(dz, wo4, h8, scale, name, ex=None):
    S, D = dz.shape
    nb, c = wo4.shape[0], wo4.shape[1]
    tm = _tile(S, 512, SUBLANES)

    def body(dz_ref, w_ref, h_ref, d_ref):
        da = lax.dot_general(dz_ref[...].astype(BF16), w_ref[...], (((1,), (1,)), ((), ())),
                             preferred_element_type=F32) * scale
        g = h_ref[0].astype(F32)
        u = h_ref[1].astype(F32)
        sg = _sigmoid(g)
        silu = g * sg
        d_ref[0] = (da * u * (sg + silu * (1.0 - sg))).astype(BF16)
        d_ref[1] = (da * silu).astype(BF16)

    pair = pl.BlockSpec((2, None, tm, c), lambda i, j: (0, j, i, 0))
    (dh,), extra = _call(
        body, name=name, out_shape=(jax.ShapeDtypeStruct((2, nb, S, c), BF16),), grid=(S // tm, nb),
        in_specs=[pl.BlockSpec((tm, D), lambda i, j: (i, 0)), pl.BlockSpec((None, c, D), lambda i, j: (j, 0, 0)),
                  pair],
        out_specs=(pair,), scratch_shapes=[], args=(dz, wo4, h8.reshape(2, nb, S, c)),
        semantics=("parallel", "parallel"), ex=ex)
    dh = dh.reshape(2 * nb, S, c)
    return dh if ex is None else (dh, extra)


def _ffn_dwin(x, dh8, name, ex=None):
    S, D = x.shape
    n, _, c = dh8.shape
    return _mm_core(name, dh8, x, pl.BlockSpec((None, S, c), lambda i, j, k: (i, 0, 0)),
                    pl.BlockSpec((S, D), lambda i, j, k: (0, 0)),
                    pl.BlockSpec((None, c, D), lambda i, j, k: (i, 0, 0)),
                    jax.ShapeDtypeStruct((n, c, D), BF16), (n, 1, 1), (c, D), True, False, ex=ex)


def _ffn_dx(dh8, wt, dz, alpha, name, ex=None):
    n, S, c = dh8.shape
    D = wt.shape[2]
    tm, tn = _tile(S, 1024, SUBLANES), _tile(D, 1024)
    return _mm_core(name, dh8, wt, pl.BlockSpec((2, tm, c), lambda i, j, k: (k, i, 0)),
                    pl.BlockSpec((2, c, tn), lambda i, j, k: (k, 0, j)),
                    pl.BlockSpec((tm, tn), lambda i, j, k: (i, j)),
                    jax.ShapeDtypeStruct((S, D), F32), (S // tm, D // tn, n // 2), (tm, tn), False, False,
                    None, dz, alpha, ex=ex)


def _res_ln_fwd(x, y, g, b, alpha, s, name):
    S, D = x.shape
    tm = _tile(S, 256, SUBLANES)

    def body(x_ref, y_ref, g_ref, b_ref, o_ref, ob_ref, xh_ref, r_ref):
        z = alpha * x_ref[...] + s * y_ref[...]
        mu = jnp.mean(z, axis=-1, keepdims=True)
        zc = z - mu
        var = jnp.mean(zc * zc, axis=-1, keepdims=True)
        rstd = lax.rsqrt(var + LN_EPS)
        xh = zc * rstd
        xh_ref[...] = xh
        r_ref[...] = rstd
        out = xh * g_ref[...] + b_ref[...]
        o_ref[...] = out
        ob_ref[...] = out.astype(BF16)

    row = pl.BlockSpec((tm, D), lambda i: (i, 0))
    vec = pl.BlockSpec((1, D), lambda i: (0, 0))
    return pl.pallas_call(
        body, name=name,
        out_shape=(jax.ShapeDtypeStruct((S, D), F32), jax.ShapeDtypeStruct((S, D), BF16),
                   jax.ShapeDtypeStruct((S, D), F32), jax.ShapeDtypeStruct((S, 1), F32)),
        grid=(S // tm,), in_specs=[row, row, vec, vec],
        out_specs=(row, row, row, pl.BlockSpec((tm, 1), lambda i: (i, 0))),
        compiler_params=_params(("parallel",)),
    )(x, y, g.reshape(1, D), b.reshape(1, D))


def _ln_bwd(dout, xh, rstd, g, name):
    S, D = dout.shape
    tm = _tile(S, 256, SUBLANES)

    def body(d_ref, xh_ref, r_ref, g_ref, dz_ref, dg_ref, db_ref):
        i = pl.program_id(0)

        @pl.when(i == 0)
        def _():
            dg_ref[...] = jnp.zeros_like(dg_ref)
            db_ref[...] = jnp.zeros_like(db_ref)

        d = d_ref[...]
        xhv = xh_ref[...]
        dxh = d * g_ref[...]
        m1 = jnp.mean(dxh, axis=-1, keepdims=True)
        m2 = jnp.mean(dxh * xhv, axis=-1, keepdims=True)
        dz_ref[...] = r_ref[...] * (dxh - m1 - xhv * m2)
        dg_ref[...] += jnp.sum(d * xhv, axis=0, keepdims=True)
        db_ref[...] += jnp.sum(d, axis=0, keepdims=True)

    row = pl.BlockSpec((tm, D), lambda i: (i, 0))
    vec = pl.BlockSpec((1, D), lambda i: (0, 0))
    dz, dg, db = pl.pallas_call(
        body, name=name,
        out_shape=(jax.ShapeDtypeStruct((S, D), F32), jax.ShapeDtypeStruct((1, D), F32),
                   jax.ShapeDtypeStruct((1, D), F32)),
        grid=(S // tm,), in_specs=[row, row, pl.BlockSpec((tm, 1), lambda i: (i, 0)), vec],
        out_specs=(row, vec, vec),
        compiler_params=_params(("arbitrary",)),
    )(dout, xh, rstd, g.reshape(1, D))
    return dz, dg[0], db[0]


def _sigmoid(x):
    return 0.5 * jnp.tanh(0.5 * x) + 0.5


def _glu_fwd(vg, name):
    S, D2 = vg.shape
    D = D2 // 2
    tm = _tile(S, 512, SUBLANES)

    def body(v_ref, g_ref, o_ref):
        o_ref[...] = v_ref[...] * _sigmoid(g_ref[...])

    return pl.pallas_call(
        body, name=name, out_shape=jax.ShapeDtypeStruct((S, D), F32), grid=(S // tm,),
        in_specs=[pl.BlockSpec((tm, D), lambda i: (i, 0)), pl.BlockSpec((tm, D), lambda i: (i, 1))],
        out_specs=pl.BlockSpec((tm, D), lambda i: (i, 0)),
        compiler_params=_params(("parallel",)),
    )(vg, vg)


def _glu_bwd(vg, dm, name):
    S, D2 = vg.shape
    D = D2 // 2
    tm = _tile(S, 512, SUBLANES)

    def body(v_ref, g_ref, dm_ref, dv_ref, dg_ref):
        sg = _sigmoid(g_ref[...])
        d = dm_ref[...]
        dv_ref[...] = (d * sg).astype(BF16)
        dg_ref[...] = (d * v_ref[...] * sg * (1.0 - sg)).astype(BF16)

    blk = pl.BlockSpec((tm, D), lambda i: (i, 0))
    dv, dg = pl.pallas_call(
        body, name=name,
        out_shape=(jax.ShapeDtypeStruct((S, D), BF16), jax.ShapeDtypeStruct((S, D), BF16)),
        grid=(S // tm,), in_specs=[blk, pl.BlockSpec((tm, D), lambda i: (i, 1)), blk],
        out_specs=(blk, blk), compiler_params=_params(("parallel",)),
    )(vg, vg, dm)
    return jnp.concatenate([dv, dg], axis=1)


def _loss_fwd_bwd(y, target):
    S, D = y.shape
    tm = _tile(S, 256, SUBLANES)

    def body(y_ref, t_ref, dy_ref, l_ref):
        i = pl.program_id(0)

        @pl.when(i == 0)
        def _():
            l_ref[...] = jnp.zeros_like(l_ref)

        e = y_ref[...] - t_ref[...]
        dy_ref[...] = e * (1.0 / D)
        l_ref[...] += jnp.sum(e * e, axis=0, keepdims=True) * (0.5 / D)

    row = pl.BlockSpec((tm, D), lambda i: (i, 0))
    dy, part = pl.pallas_call(
        body, name="loss", out_shape=(jax.ShapeDtypeStruct((S, D), F32), jax.ShapeDtypeStruct((1, D), F32)),
        grid=(S // tm,), in_specs=[row, row], out_specs=(row, pl.BlockSpec((1, D), lambda i: (0, 0))),
        compiler_params=_params(("arbitrary",)),
    )(y, target)
    return dy, jnp.sum(part)


def _tri(n, lower):
    r = lax.broadcasted_iota(jnp.int32, (n, n), 0)
    c = lax.broadcasted_iota(jnp.int32, (n, n), 1)
    return jnp.where((c <= r) if lower else (c >= r), 1.0, 0.0)


def _fox_gate_fwd(fl, bf):
    S, W = fl.shape
    tm = _tile(S, 256, SUBLANES)

    def body(fl_ref, b_ref, c_ref, carry):
        i = pl.program_id(0)

        @pl.when(i == 0)
        def _():
            carry[...] = jnp.zeros_like(carry)

        x = fl_ref[...] + b_ref[...]
        lf = jnp.minimum(x, 0.0) - jnp.log(1.0 + jnp.exp(-jnp.abs(x)))
        c_ref[...] = jnp.dot(_tri(tm, True), lf, precision=lax.Precision.HIGHEST,
                             preferred_element_type=F32) + carry[...]
        carry[...] += jnp.sum(lf, axis=0, keepdims=True)

    blk = pl.BlockSpec((tm, W), lambda i: (i, 0))
    return pl.pallas_call(
        body, name="fox_gate_fwd", out_shape=jax.ShapeDtypeStruct((S, W), F32), grid=(S // tm,),
        in_specs=[blk, pl.BlockSpec((1, W), lambda i: (0, 0))], out_specs=blk,
        scratch_shapes=[pltpu.VMEM((1, W), F32)], compiler_params=_params(("arbitrary",)),
    )(fl, bf)


def _fox_gate_bwd(dcum, fl, bf):
    S, W = fl.shape
    tm = _tile(S, 256, SUBLANES)
    nb = S // tm

    def body(dc_ref, fl_ref, b_ref, dfl_ref, db_ref, carry):
        i = pl.program_id(0)

        @pl.when(i == 0)
        def _():
            carry[...] = jnp.zeros_like(carry)
            db_ref[...] = jnp.zeros_like(db_ref)

        dc = dc_ref[...]
        r = jnp.dot(_tri(tm, False), dc, precision=lax.Precision.HIGHEST, preferred_element_type=F32) + carry[...]
        carry[...] += jnp.sum(dc, axis=0, keepdims=True)
        x = fl_ref[...] + b_ref[...]
        e = jnp.exp(-jnp.abs(x))
        dfl = r * jnp.where(x >= 0, e, 1.0) / (1.0 + e)
        dfl_ref[...] = dfl
        db_ref[...] += jnp.sum(dfl, axis=0, keepdims=True)

    blk = pl.BlockSpec((tm, W), lambda i: (nb - 1 - i, 0))
    vec = pl.BlockSpec((1, W), lambda i: (0, 0))
    return pl.pallas_call(
        body, name="fox_gate_bwd",
        out_shape=(jax.ShapeDtypeStruct((S, W), F32), jax.ShapeDtypeStruct((1, W), F32)),
        grid=(nb,), in_specs=[blk, blk, vec], out_specs=(blk, vec),
        scratch_shapes=[pltpu.VMEM((1, W), F32)], compiler_params=_params(("arbitrary",)),
    )(dcum, fl, bf)


def _causal(t):
    row = lax.broadcasted_iota(jnp.int32, (t, t), 0)
    col = lax.broadcasted_iota(jnp.int32, (t, t), 1)
    return col <= row


def _first_head(shape):
    return lax.broadcasted_iota(jnp.int32, shape, len(shape) - 1) < HEAD_DIM


def _split3(x):
    hi = x.astype(BF16).astype(F32)
    r = x - hi
    mid = r.astype(BF16).astype(F32)
    return hi, mid, (r - mid).astype(BF16).astype(F32)


def _bias_lanes(c, query):
    lane = lax.broadcasted_iota(jnp.int32, (c.shape[0], LANES), 1)
    hi, mid, lo = _split3(c)
    if query:
        out = jnp.where(lane == 0, hi, jnp.where(lane == 1, mid, jnp.where(lane == 2, lo,
                                                                          jnp.where(lane < 6, 1.0, 0.0))))
    else:
        out = jnp.where(lane < 3, 1.0, jnp.where(lane == 3, -hi, jnp.where(lane == 4, -mid,
                                                                          jnp.where(lane == 5, -lo, 0.0))))
    return out.astype(BF16)


def _flash_fwd(proj, cum2, ex=None):
    S = proj.shape[0]
    D = (proj.shape[1] - LANES) // 3
    HP = D // LANES
    t = _tile(S, ATTN_TILE)
    nq = S // t
    scale = 1.0 / math.sqrt(HEAD_DIM)

    def body(q_ref, k_ref, v_ref, cq_ref, ck_ref, o_ref, lse_ref, kx, vb):
        qi = pl.program_id(1)

        @pl.when(qi == 0)
        def _():
            def prep(kb, c):
                sl = pl.ds(pl.multiple_of(kb * t, t), t)
                k16 = k_ref[sl, :].astype(BF16)
                ckv = ck_ref[0, sl, :]
                for a in range(2):
                    kx[a, sl, 0:LANES] = k16
                    kx[a, sl, LANES:2 * LANES] = _bias_lanes(ckv[:, a:a + 1], False)
                vb[sl, :] = v_ref[sl, :].astype(BF16)
                return c

            lax.fori_loop(0, nq, prep, 0)

        first = _first_head((t, LANES))
        qf = q_ref[...] * (scale * LOG2E)
        cqv = cq_ref[0]
        qx = [jnp.concatenate([jnp.where(keep, qf, 0.0).astype(BF16), _bias_lanes(cqv[:, a:a + 1], True)], axis=1)
              for a, keep in enumerate((first, jnp.logical_not(first)))]

        def block(ki, carry, masked):
            m_old, l_old, acc = carry
            sl = pl.ds(pl.multiple_of(ki * t, t), t)
            vv = vb[sl, :]
            m_new, l_new, corr, pv = [], [], [], []
            for a in range(2):
                s = lax.dot_general(qx[a], kx[a, sl, :], (((1,), (1,)), ((), ())), preferred_element_type=F32)
                if masked:
                    s = jnp.where(_causal(t), s, NEG_INF)
                m_a = jnp.maximum(m_old[a], jnp.max(s, axis=1, keepdims=True))
                p = jnp.exp2(s - m_a)
                c_a = jnp.exp2(m_old[a] - m_a)
                m_new.append(m_a)
                corr.append(c_a)
                l_new.append(c_a * l_old[a] + jnp.sum(p, axis=1, keepdims=True))
                pv.append(jnp.dot(p.astype(BF16), vv, preferred_element_type=F32))
            acc = jnp.where(first, corr[0] * acc + pv[0], corr[1] * acc + pv[1])
            return tuple(m_new), tuple(l_new), acc

        neg = jnp.full((t, 1), NEG_INF, F32)
        zero = jnp.zeros((t, 1), F32)
        carry = lax.fori_loop(0, qi, lambda ki, c: block(ki, c, False),
                              ((neg, neg), (zero, zero), jnp.zeros((t, LANES), F32)))
        m, l, acc = block(qi, carry, True)
        o_ref[...] = acc / jnp.where(first, l[0], l[1])
        lse_ref[0, :, 0:1] = m[0] + jnp.log2(l[0])
        lse_ref[0, :, 1:2] = m[1] + jnp.log2(l[1])

    qblk = pl.BlockSpec((t, LANES), lambda h, i: (i, h))
    r2 = pl.BlockSpec((1, t, 2), lambda h, i: (h, i, 0))
    (o, lse), extra = _call(
        body, name="fox_attn_fwd",
        out_shape=(jax.ShapeDtypeStruct((S, D), F32), jax.ShapeDtypeStruct((HP, S, 2), F32)),
        grid=(HP, nq),
        in_specs=[qblk, pl.BlockSpec((S, LANES), lambda h, i: (0, HP + h)),
                  pl.BlockSpec((S, LANES), lambda h, i: (0, 2 * HP + h)), r2,
                  pl.BlockSpec((1, S, 2), lambda h, i: (h, 0, 0))],
        out_specs=(qblk, r2),
        scratch_shapes=[pltpu.VMEM((2, S, 2 * LANES), BF16), pltpu.VMEM((S, LANES), BF16)],
        args=(proj, proj, proj, cum2, cum2), semantics=("parallel", "arbitrary"), ex=ex)
    return o, lse, extra


def _flash_bwd(proj, rows, o, do, ex=None):
    S = proj.shape[0]
    D = (proj.shape[1] - LANES) // 3
    HP = D // LANES
    t = _tile(S, ATTN_TILE)
    nb = S // t
    scale = 1.0 / math.sqrt(HEAD_DIM)

    def body(q_ref, k_ref, v_ref, rows_ref, o_ref, do_ref, dq_ref, dk_ref, dv_ref, dcq_ref, dck_ref,
             q_s, do_s, dl_s, dq_acc, dk_acc, dv_acc, dc_acc):
        kb = pl.program_id(1)

        @pl.when(kb == 0)
        def _():
            dq_acc[...] = jnp.zeros_like(dq_acc)
            dcq_ref[...] = jnp.zeros_like(dcq_ref)

            def prep(qb, c):
                sl = pl.ds(pl.multiple_of(qb * t, t), t)
                first = _first_head((t, LANES))
                qf = q_ref[sl, :] * (scale * LOG2E)
                dof = do_ref[sl, :]
                prod = dof * o_ref[sl, :]
                rv = rows_ref[0, sl, :]
                for a, keep in enumerate((first, jnp.logical_not(first))):
                    q_s[a, sl, 0:LANES] = jnp.where(keep, qf, 0.0).astype(BF16)
                    q_s[a, sl, LANES:2 * LANES] = _bias_lanes(rv[:, a:a + 1], True)
                    do_s[a, sl, :] = jnp.where(keep, dof, 0.0).astype(BF16)
                    dl_s[sl, a:a + 1] = jnp.sum(jnp.where(keep, prod, 0.0), axis=1, keepdims=True)
                return c

            lax.fori_loop(0, nb, prep, 0)

        dk_acc[...] = jnp.zeros_like(dk_acc)
        dv_acc[...] = jnp.zeros_like(dv_acc)
        dc_acc[...] = jnp.zeros_like(dc_acc)
        first = _first_head((t, LANES))
        kf = k_ref[...]
        kk = kf.astype(BF16)
        k_own = (jnp.where(first, kf, 0.0).astype(BF16), jnp.where(first, 0.0, kf).astype(BF16))
        ckv = rows_ref[0, pl.ds(pl.multiple_of(kb * t, t), t), :]
        kx = [jnp.concatenate([kk, _bias_lanes(ckv[:, a:a + 1], False)], axis=1) for a in range(2)]
        vv = v_ref[...].astype(BF16)

        def block(qb, masked):
            sl = pl.ds(pl.multiple_of(qb * t, t), t)
            rv = rows_ref[0, sl, :]
            dlv = dl_s[sl, :]
            dq_new = dq_acc[sl, :]
            for a in range(2):
                dob = do_s[a, sl, :]
                s = lax.dot_general(q_s[a, sl, :], kx[a], (((1,), (1,)), ((), ())), preferred_element_type=F32)
                if masked:
                    s = jnp.where(_causal(t), s, NEG_INF)
                p = jnp.exp2(s - rv[:, 2 + a:3 + a])
                dv_acc[...] += lax.dot_general(p.astype(BF16), dob, (((0,), (0,)), ((), ())),
                                               preferred_element_type=F32)
                dp = lax.dot_general(dob, vv, (((1,), (1,)), ((), ())), preferred_element_type=F32)
                ds = p * (dp - dlv[:, a:a + 1])
                dsb = ds.astype(BF16)
                dk_acc[...] += lax.dot_general(dsb, q_s[a, sl, 0:LANES], (((0,), (0,)), ((), ())),
                                               preferred_element_type=F32)
                dq_new = dq_new + jnp.dot(dsb, k_own[a], preferred_element_type=F32) * scale
                dcq_ref[0, sl, a:a + 1] += jnp.sum(ds, axis=1, keepdims=True)
                dc_acc[a:a + 1, :] -= jnp.sum(ds, axis=0, keepdims=True)
            dq_acc[sl, :] = dq_new

        block(kb, True)

        def rest(qb, c):
            block(qb, False)
            return c

        lax.fori_loop(kb + 1, nb, rest, 0)
        dk_ref[...] = (dk_acc[...] * (1.0 / LOG2E)).astype(BF16)
        dv_ref[...] = dv_acc[...].astype(BF16)
        dck_ref[0, 0] = dc_acc[0:1, :]
        dck_ref[1, 0] = dc_acc[1:2, :]

        @pl.when(kb == nb - 1)
        def _():
            dq_ref[...] = dq_acc[...].astype(BF16)

    full = lambda c0: pl.BlockSpec((S, LANES), lambda h, j, c0=c0: (0, c0 + h))
    blk = lambda c0: pl.BlockSpec((t, LANES), lambda h, j, c0=c0: (j, c0 + h))
    f32 = lambda *s: jax.ShapeDtypeStruct(s, F32)
    b16 = jax.ShapeDtypeStruct((S, D), BF16)
    outs, extra = _call(
        body, name="fox_attn_bwd",
        out_shape=(b16, b16, b16, f32(HP, S, 2), f32(2 * HP, nb, 1, t)),
        grid=(HP, nb),
        in_specs=[full(0), blk(HP), blk(2 * HP), pl.BlockSpec((1, S, 4), lambda h, j: (h, 0, 0)), full(0), full(0)],
        out_specs=(full(0), blk(0), blk(0), pl.BlockSpec((1, S, 2), lambda h, j: (h, 0, 0)),
                   pl.BlockSpec((2, 1, 1, t), lambda h, j: (h, j, 0, 0))),
        scratch_shapes=[pltpu.VMEM((2, S, 2 * LANES), BF16), pltpu.VMEM((2, S, LANES), BF16),
                        pltpu.VMEM((S, 2), F32), pltpu.VMEM((S, LANES), F32), pltpu.VMEM((t, LANES), F32),
                        pltpu.VMEM((t, LANES), F32), pltpu.VMEM((2, t), F32)],
        args=(proj, proj, proj, rows, o, do), semantics=("parallel", "arbitrary"), ex=ex)
    return (*outs, extra)


def _s5_consts(T):
    rows = T * SUBLANES
    rr = lax.broadcasted_iota(jnp.int32, (rows, T), 0)
    tt = lax.broadcasted_iota(jnp.int32, (rows, T), 1)
    rep = jnp.where(rr // SUBLANES == tt, 1.0, 0.0).astype(BF16)
    r2 = lax.broadcasted_iota(jnp.int32, (rows, S5_PART), 0)
    c2 = lax.broadcasted_iota(jnp.int32, (rows, S5_PART), 1)
    mask = (c2 // (S5_PART // SUBLANES)) == (r2 % SUBLANES)
    return rep, mask


def _gelu(y):
    c = math.sqrt(2.0 / math.pi)
    return 0.5 * y * (1.0 + jnp.tanh(c * (y + 0.044715 * y * y * y)))


def _gelu_grad(y):
    c = math.sqrt(2.0 / math.pi)
    th = jnp.tanh(c * (y + 0.044715 * y * y * y))
    return 0.5 * (1.0 + th) + 0.5 * y * (1.0 - th * th) * c * (1.0 + 3.0 * 0.044715 * y * y)


def _s5_fwd(x, rmat, cmat, lam, dskip, ex=None):
    S, D = x.shape
    NQ = D // S5_PART
    T = _tile(S, 128, SUBLANES)
    rows = T * SUBLANES

    def body(x_ref, r_ref, c_ref, lam_ref, d_ref, y_ref, yg_ref, h_ref, bu_s, carry):
        i = pl.program_id(0)

        @pl.when(i == 0)
        def _():
            carry[...] = jnp.zeros_like(carry)

        rep, mask = _s5_consts(T)
        cols = [pl.ds(q * S5_PART, S5_PART) for q in range(NQ)]
        for q in range(NQ):
            xrep = jnp.dot(rep, x_ref[:, cols[q]].astype(BF16), preferred_element_type=F32)
            lx = jnp.where(mask, xrep, 0.0).astype(BF16)
            bu_s[q] = jnp.dot(lx, r_ref[q], preferred_element_type=F32)
        lam_v = [(lam_ref[q, :, 0:LANES], lam_ref[q, :, LANES:2 * LANES]) for q in range(NQ)]

        def step(t, c):
            o = pl.multiple_of(t * SUBLANES, SUBLANES)
            new = []
            for q in range(NQ):
                hr, hi = c[q]
                ar, ai = lam_v[q]
                sl = bu_s[q, pl.ds(o, SUBLANES), :]
                nhr = ar * hr - ai * hi + sl[:, 0:LANES]
                nhi = ar * hi + ai * hr + sl[:, LANES:2 * LANES]
                h_ref[q, pl.ds(o, SUBLANES), 0:LANES] = nhr
                h_ref[q, pl.ds(o, SUBLANES), LANES:2 * LANES] = nhi
                new.append((nhr, nhi))
            return tuple(new)

        fin = lax.fori_loop(0, T, step,
                            tuple((carry[q, :, 0:LANES], carry[q, :, LANES:2 * LANES]) for q in range(NQ)))
        for q in range(NQ):
            carry[q, :, 0:LANES] = fin[q][0]
            carry[q, :, LANES:2 * LANES] = fin[q][1]
            z = jnp.dot(h_ref[q].astype(BF16), c_ref[q], preferred_element_type=F32)
            z = jnp.where(mask, z, 0.0)
            y = jnp.sum(z.reshape(T, SUBLANES, S5_PART), axis=1) + d_ref[:, cols[q]] * x_ref[:, cols[q]]
            y_ref[:, cols[q]] = y
            yg_ref[:, cols[q]] = _gelu(y).astype(BF16)

    xs = pl.BlockSpec((T, D), lambda i: (i, 0))
    ms = pl.BlockSpec((NQ, S5_PART, S5_PART), lambda i: (0, 0, 0))
    outs, extra = _call(
        body, name="s5_scan_fwd",
        out_shape=(jax.ShapeDtypeStruct((S, D), F32), jax.ShapeDtypeStruct((S, D), BF16),
                   jax.ShapeDtypeStruct((NQ, S * SUBLANES, S5_PART), F32)),
        grid=(S // T,),
        in_specs=[xs, ms, ms, pl.BlockSpec((NQ, SUBLANES, S5_PART), lambda i: (0, 0, 0)),
                  pl.BlockSpec((1, D), lambda i: (0, 0))],
        out_specs=(xs, xs, pl.BlockSpec((NQ, rows, S5_PART), lambda i: (0, i, 0))),
        scratch_shapes=[pltpu.VMEM((NQ, rows, S5_PART), F32), pltpu.VMEM((NQ, SUBLANES, S5_PART), F32)],
        args=(x, rmat, cmat, lam, dskip), semantics=("arbitrary",), ex=ex)
    return (*outs, extra)


def _s5_bwd(x, y, dyg, hs, rmat, cmat, lam, dskip, res, res_scale, ex=None):
    S, D = x.shape
    NQ = D // S5_PART
    T = _tile(S, 128, SUBLANES)
    nb = S // T
    rows = T * SUBLANES

    def body(x_ref, y_ref, dyg_ref, res_ref, h_ref, hp_ref, r_ref, c_ref, lam_ref, d_ref,
             dx_ref, dr_ref, dc_ref, dlam_ref, dd_ref, dh_s, g_s, hs_s, carry):
        i = pl.program_id(0)

        @pl.when(i == 0)
        def _():
            carry[...] = jnp.zeros_like(carry)
            dr_ref[...] = jnp.zeros_like(dr_ref)
            dc_ref[...] = jnp.zeros_like(dc_ref)
            dlam_ref[...] = jnp.zeros_like(dlam_ref)
            dd_ref[...] = jnp.zeros_like(dd_ref)

        rep, mask = _s5_consts(T)
        cols = [pl.ds(q * S5_PART, S5_PART) for q in range(NQ)]
        dys, ldys = [], []
        for q in range(NQ):
            dy = dyg_ref[:, cols[q]] * _gelu_grad(y_ref[:, cols[q]])
            dyrep = jnp.dot(rep, dy.astype(BF16), preferred_element_type=F32)
            ldy = jnp.where(mask, dyrep, 0.0).astype(BF16)
            dh_s[q] = lax.dot_general(ldy, c_ref[q], (((1,), (1,)), ((), ())), preferred_element_type=F32)
            dys.append(dy)
            ldys.append(ldy)
        lam_v = [(lam_ref[q, :, 0:LANES], lam_ref[q, :, LANES:2 * LANES]) for q in range(NQ)]

        def step(n, c):
            o = pl.multiple_of((T - 1 - n) * SUBLANES, SUBLANES)
            new = []
            for q in range(NQ):
                gr, gi = c[q]
                ar, ai = lam_v[q]
                sl = dh_s[q, pl.ds(o, SUBLANES), :]
                ngr = sl[:, 0:LANES] + ar * gr + ai * gi
                ngi = sl[:, LANES:2 * LANES] - ai * gr + ar * gi
                g_s[q, pl.ds(o, SUBLANES), 0:LANES] = ngr
                g_s[q, pl.ds(o, SUBLANES), LANES:2 * LANES] = ngi
                new.append((ngr, ngi))
            return tuple(new)

        fin = lax.fori_loop(0, T, step,
                            tuple((carry[q, :, 0:LANES], carry[q, :, LANES:2 * LANES]) for q in range(NQ)))
        for q in range(NQ):
            carry[q, :, 0:LANES] = fin[q][0]
            carry[q, :, LANES:2 * LANES] = fin[q][1]
            xv = x_ref[:, cols[q]]
            hv = h_ref[q]
            hs_s[0:SUBLANES, :] = jnp.where(i == nb - 1, 0.0, hp_ref[q])
            hs_s[SUBLANES:rows + SUBLANES, :] = hv
            hprev = hs_s[0:rows, :]
            gv = g_s[q]
            g_re, g_im = gv[:, 0:LANES], gv[:, LANES:2 * LANES]
            hp_re, hp_im = hprev[:, 0:LANES], hprev[:, LANES:2 * LANES]
            dar = jnp.sum((g_re * hp_re + g_im * hp_im).reshape(T, SUBLANES, LANES), axis=0)
            dai = jnp.sum((g_im * hp_re - g_re * hp_im).reshape(T, SUBLANES, LANES), axis=0)
            dlam_ref[q, :, 0:LANES] += dar
            dlam_ref[q, :, LANES:2 * LANES] += dai

            gb = gv.astype(BF16)
            xrep = jnp.dot(rep, xv.astype(BF16), preferred_element_type=F32)
            lx = jnp.where(mask, xrep, 0.0).astype(BF16)
            dr_ref[q] += lax.dot_general(lx, gb, (((0,), (0,)), ((), ())), preferred_element_type=F32)
            dc_ref[q] += lax.dot_general(hv.astype(BF16), ldys[q], (((0,), (0,)), ((), ())),
                                         preferred_element_type=F32)
            zx = lax.dot_general(gb, r_ref[q], (((1,), (1,)), ((), ())), preferred_element_type=F32)
            zx = jnp.where(mask, zx, 0.0)
            dx_ref[:, cols[q]] = (jnp.sum(zx.reshape(T, SUBLANES, S5_PART), axis=1) + d_ref[:, cols[q]] * dys[q]
                                  + res_scale * res_ref[:, cols[q]])
            dd_ref[:, cols[q]] += jnp.sum(dys[q] * xv, axis=0, keepdims=True)

    xs = pl.BlockSpec((T, D), lambda i: (nb - 1 - i, 0))
    ms = pl.BlockSpec((NQ, S5_PART, S5_PART), lambda i: (0, 0, 0))
    ls = pl.BlockSpec((NQ, SUBLANES, S5_PART), lambda i: (0, 0, 0))
    ds_ = pl.BlockSpec((1, D), lambda i: (0, 0))
    outs, extra = _call(
        body, name="s5_scan_bwd",
        out_shape=(jax.ShapeDtypeStruct((S, D), F32), jax.ShapeDtypeStruct((NQ, S5_PART, S5_PART), F32),
                   jax.ShapeDtypeStruct((NQ, S5_PART, S5_PART), F32),
                   jax.ShapeDtypeStruct((NQ, SUBLANES, S5_PART), F32), jax.ShapeDtypeStruct((1, D), F32)),
        grid=(nb,),
        in_specs=[xs, xs, xs, xs, pl.BlockSpec((NQ, rows, S5_PART), lambda i: (0, nb - 1 - i, 0)),
                  pl.BlockSpec((NQ, SUBLANES, S5_PART), lambda i: (0, jnp.maximum((nb - 1 - i) * T - 1, 0), 0)),
                  ms, ms, ls, ds_],
        out_specs=(xs, ms, ms, ls, ds_),
        scratch_shapes=[pltpu.VMEM((NQ, rows, S5_PART), F32), pltpu.VMEM((NQ, rows, S5_PART), F32),
                        pltpu.VMEM((rows + SUBLANES, S5_PART), F32), pltpu.VMEM((NQ, SUBLANES, S5_PART), F32)],
        args=(x, y, dyg, res, hs, hs, rmat, cmat, lam, dskip), semantics=("arbitrary",), ex=ex)
    return (*outs, extra)


def _s5_discretise(a_re, a_im, log_dt, b_re, b_im):
    dt = jnp.exp(log_dt)[:, None]
    mag = jnp.exp(a_re * dt)
    ang = a_im * dt
    lb_re = mag * jnp.cos(ang)
    lb_im = mag * jnp.sin(ang)
    den = a_re * a_re + a_im * a_im
    nr = lb_re - 1.0
    ni = lb_im
    z_re = (nr * a_re + ni * a_im) / den
    z_im = (ni * a_re - nr * a_im) / den
    bb_re = z_re[..., None] * b_re - z_im[..., None] * b_im
    bb_im = z_re[..., None] * b_im + z_im[..., None] * b_re
    return lb_re, lb_im, bb_re, bb_im


def _s5_expand(w):
    G = w.shape[0]
    NQ = G // 16
    base = w.reshape(NQ, S5_PART, S5_STATE)
    half = (jnp.arange(S5_PART) // S5_GROUP) % 2
    sel = (half[:, None] == jnp.arange(2)[None, :]).astype(w.dtype)
    out = base[:, :, None, :] * sel[None, :, :, None]
    return out.reshape(NQ, S5_PART, 2 * S5_STATE)


def _s5_extract(m):
    NQ = m.shape[0]
    half = (jnp.arange(S5_PART) // S5_GROUP) % 2
    sel = (half[:, None] == jnp.arange(2)[None, :]).astype(m.dtype)
    base = jnp.sum(m.reshape(NQ, S5_PART, 2, S5_STATE) * sel[None, :, :, None], axis=2)
    return base.reshape(NQ * 16, S5_GROUP, S5_STATE)


def _s5_slab(v):
    return v.reshape(v.shape[0] // 16, SUBLANES, LANES)


def _adamw(w, g, m, v, name):
    R, C = w.shape
    tr = _tile(R, 256, SUBLANES)
    c1 = 1.0 / (1.0 - ADAM_B1 ** ADAM_STEP)
    c2 = 1.0 / (1.0 - ADAM_B2 ** ADAM_STEP)

    def body(w_ref, g_ref, m_ref, v_ref, d_ref, nm_ref, nv_ref):
        gv = g_ref[...]
        nm = ADAM_B1 * m_ref[...] + (1.0 - ADAM_B1) * gv
        nv = ADAM_B2 * v_ref[...] + (1.0 - ADAM_B2) * (gv * gv)
        nm_ref[...] = nm
        nv_ref[...] = nv
        d_ref[...] = -ADAM_LR * ((nm * c1) / (jnp.sqrt(nv * c2) + ADAM_EPS) + ADAM_WD * w_ref[...])

    blk = pl.BlockSpec((tr, C), lambda i: (i, 0))
    sh = jax.ShapeDtypeStruct((R, C), F32)
    return pl.pallas_call(
        body, name=name, out_shape=(sh, sh, sh), grid=(R // tr,), in_specs=[blk] * 4, out_specs=(blk,) * 3,
        compiler_params=_params(("parallel",)),
    )(w, g, m, v)


def _adamw_recv(parts, w, m, v, name):
    L, R, C = w.shape
    tr = _tile(R, 256, SUBLANES)
    c1 = 1.0 / (1.0 - ADAM_B1 ** ADAM_STEP)
    c2 = 1.0 / (1.0 - ADAM_B2 ** ADAM_STEP)

    def body(*refs):
        p_refs = refs[:L]
        w_ref, m_ref, v_ref, g_ref, d_ref, nm_ref, nv_ref = refs[L:]
        li = pl.program_id(0)
        for l in range(L):
            @pl.when(li == l)
            def _(p_ref=p_refs[l]):
                gv = p_ref[0].astype(F32)
                for k in range(1, N_DEV):
                    gv = gv + p_ref[k].astype(F32)
                g_ref[...] = gv
                nm = ADAM_B1 * m_ref[...] + (1.0 - ADAM_B1) * gv
                nv = ADAM_B2 * v_ref[...] + (1.0 - ADAM_B2) * (gv * gv)
                nm_ref[...] = nm
                nv_ref[...] = nv
                d_ref[...] = -ADAM_LR * ((nm * c1) / (jnp.sqrt(nv * c2) + ADAM_EPS) + ADAM_WD * w_ref[...])

    p_specs = [pl.BlockSpec((N_DEV, tr, C), lambda li, i, l=l: (0, jnp.where(li == l, i, 0), 0)) for l in range(L)]
    blk = pl.BlockSpec((None, tr, C), lambda li, i: (li, i, 0))
    sh = jax.ShapeDtypeStruct((L, R, C), F32)
    return pl.pallas_call(
        body, name=name, out_shape=(sh, sh, sh, sh), grid=(L, R // tr),
        in_specs=p_specs + [blk, blk, blk], out_specs=(blk,) * 4,
        compiler_params=_params(("parallel", "parallel")),
    )(*parts, w, m, v)


def _sum8(parts, name):
    _, R, C = parts.shape
    tr = _tile(R, 256, SUBLANES)

    def body(p_ref, o_ref):
        acc = p_ref[0].astype(F32)
        for k in range(1, N_DEV):
            acc = acc + p_ref[k].astype(F32)
        o_ref[...] = acc

    return pl.pallas_call(
        body, name=name, out_shape=jax.ShapeDtypeStruct((R, C), F32), grid=(R // tr,),
        in_specs=[pl.BlockSpec((N_DEV, tr, C), lambda i: (0, i, 0))],
        out_specs=pl.BlockSpec((tr, C), lambda i: (i, 0)), compiler_params=_params(("parallel",)),
    )(parts)


def _peers():
    x, y, c = lax.axis_index("x"), lax.axis_index("y"), lax.axis_index("c")
    me = 4 * x + 2 * y + c
    out = []
    for k in range(1, N_DEV):
        kx, ky, kc = (k >> 2) & 1, (k >> 1) & 1, k & 1
        px, py, pc = x ^ kx, y ^ ky, c ^ kc
        out.append(((px, py, pc), 4 * px + 2 * py + pc))
    return me, out


SIBLING = 1
SAME_CORE = (2, 4, 6)


class Exchange:
    def __init__(self, xs, scatter):
        self.xs = list(xs)
        self.scatter = list(scatter)
        self.n = len(self.xs)

    def out_shapes(self):
        return tuple(jax.ShapeDtypeStruct(x.shape if sc else (N_DEV,) + x.shape, x.dtype)
                     for x, sc in zip(self.xs, self.scatter))

    def sems(self):
        return [pltpu.SemaphoreType.DMA((self.n * N_DEV,)), pltpu.SemaphoreType.DMA((self.n * N_DEV,)),
                pltpu.SemaphoreType.DMA((self.n,))]

    def _copy(self, i, m, src, dst, to, send_sems, recv_sems):
        return pltpu.make_async_remote_copy(src_ref=src, dst_ref=dst, send_sem=send_sems.at[i * N_DEV + m],
                                            recv_sem=recv_sems.at[i * N_DEV + m], device_id=to, device_id_type=MESH)

    def start(self, x_refs, o_refs, send_sems, recv_sems, local_sems):
        me, peers = _peers()
        for i, (x_ref, o_ref, sc) in enumerate(zip(x_refs, o_refs, self.scatter)):
            pltpu.make_async_copy(x_ref.at[me] if sc else x_ref, o_ref.at[me], local_sems.at[i]).start()
            for m in (range(1, N_DEV) if sc else (SIBLING,) + SAME_CORE):
                dev, idx = peers[m - 1]
                self._copy(i, m, x_ref.at[idx] if sc else x_ref, o_ref.at[me], dev, send_sems, recv_sems).start()

    def middle(self, x_refs, o_refs, send_sems, recv_sems, local_sems):
        me, peers = _peers()
        sibling = peers[SIBLING - 1][0]
        for i, (x_ref, o_ref, sc) in enumerate(zip(x_refs, o_refs, self.scatter)):
            if sc:
                continue
            for m in SAME_CORE:
                dev, idx = peers[m - 1]
                self._copy(i, m, x_ref, o_ref.at[idx], dev, send_sems, recv_sems).wait_recv()
                self._copy(i, m ^ 1, o_ref.at[idx], o_ref.at[idx], sibling, send_sems, recv_sems).start()

    def wait(self, x_refs, o_refs, send_sems, recv_sems, local_sems):
        me, peers = _peers()
        for i, (x_ref, o_ref, sc) in enumerate(zip(x_refs, o_refs, self.scatter)):
            for m in range(1, N_DEV):
                dev, idx = peers[m - 1]
                cp = self._copy(i, m, x_ref.at[idx] if sc else x_ref, o_ref.at[idx], dev, send_sems, recv_sems)
                if sc or m not in SAME_CORE:
                    cp.wait_recv()
                cp.wait_send()
            pltpu.make_async_copy(x_ref.at[me] if sc else x_ref, o_ref.at[me], local_sems.at[i]).wait()


def _exchange(ex, name):
    n = ex.n

    def body(*refs):
        x_refs, o_refs, sems = refs[:n], refs[n:2 * n], refs[2 * n:]
        ex.start(x_refs, o_refs, *sems)
        ex.middle(x_refs, o_refs, *sems)
        ex.wait(x_refs, o_refs, *sems)

    hbm = pl.BlockSpec(memory_space=pltpu.HBM)
    return pl.pallas_call(body, name=name, out_shape=ex.out_shapes(), in_specs=[hbm] * n, out_specs=(hbm,) * n,
                          scratch_shapes=ex.sems())(*ex.xs)


def _call(body, *, name, out_shape, grid, in_specs, out_specs, scratch_shapes, args, semantics, ex=None):
    if ex is None:
        return pl.pallas_call(body, name=name, out_shape=out_shape, grid=grid, in_specs=in_specs, out_specs=out_specs,
                              scratch_shapes=scratch_shapes, compiler_params=_params(semantics))(*args), ()
    n, ni, no, ns = ex.n, len(args), len(out_shape), len(scratch_shapes)

    def wrapped(*refs):
        ins, cx = refs[:ni], refs[ni:ni + n]
        outs, co = refs[ni + n:ni + n + no], refs[ni + n + no:ni + 2 * n + no]
        scratch, sems = refs[ni + 2 * n + no:ni + 2 * n + no + ns], refs[ni + 2 * n + no + ns:]
        step = functools.reduce(lambda acc, a: acc * grid[a] + pl.program_id(a), range(len(grid)), 0)
        steps = math.prod(grid)

        @pl.when(step == 0)
        def _():
            ex.start(cx, co, *sems)

        body(*ins, *outs, *scratch)

        @pl.when(step == (steps * 3) // 4 - (steps > 1))
        def _():
            ex.middle(cx, co, *sems)

        @pl.when(step == steps - 1)
        def _():
            ex.wait(cx, co, *sems)

    hbm = pl.BlockSpec(memory_space=pltpu.HBM)
    res = pl.pallas_call(
        wrapped, name=name, out_shape=tuple(out_shape) + ex.out_shapes(), grid=grid,
        in_specs=list(in_specs) + [hbm] * n, out_specs=tuple(out_specs) + (hbm,) * n,
        scratch_shapes=list(scratch_shapes) + ex.sems(),
        compiler_params=_params(("arbitrary",) * len(grid)))(*args, *ex.xs)
    return res[:no], res[no:]


def _pack_flat(arrs):
    cat = jnp.concatenate([a.reshape(-1) for a in arrs])
    per = PACK_COLS * 2 * SUBLANES
    tot = -(-cat.shape[0] // per) * per
    return jnp.pad(cat, (0, tot - cat.shape[0])).reshape(tot // PACK_COLS, PACK_COLS)


def _unpack_flat(packed, shapes):
    flat = packed.reshape(-1)
    out, o = [], 0
    for s in shapes:
        n = math.prod(s)
        out.append(flat[o:o + n].reshape(s))
        o += n
    return out


def _ffn_fwd(x, xb, wt, wo4, g, b, alpha, tag, ex=None):
    h8, a4, extra = _ffn_h(xb, wt, tag + "_h", ex)
    out, outb, xh, rstd = _ffn_y_ln(a4, wo4, x, g, b, alpha, 0.5, tag + "_y_ln")
    return out, outb, (xb, h8, a4, xh, rstd), extra


def _ffn_bwd(dout, saved, wt, wo4, g, alpha, tag, carry=None, last=None):
    x, h8, a4, xh, rstd = saved
    load = dict(carry or {})
    slots = [[], [], [], []]
    for n, k in enumerate(sorted(load, key=lambda k: -load[k].size)):
        slots[min(n, 3)].append(k)
    arrived = {}

    def run(fn, slot, *args):
        keys = slots[slot]
        if not keys:
            return fn(*args)
        res, extra = fn(*args, ex=Exchange([load[k] for k in keys], [k != "small" for k in keys]))
        arrived.update(zip(keys, extra))
        return res

    dz, dg, db = _ln_bwd(dout, xh, rstd, g, tag + "_ln_bwd")
    if last is not None:
        (key_out, key_in), load["small"] = last(dg, db)
        slots[3].append("small")
    dw_out = run(_ffn_dwout, 3, a4, dz, 0.5, tag + "_dwout")
    dw_out = dw_out.reshape(N_DEV, -1, dw_out.shape[-1])
    if last is not None:
        load[key_out] = dw_out
        slots[2].append(key_out)
    dh8 = run(_ffn_dh, 2, dz, wo4, h8, 0.5, tag + "_dh")
    dw_in = run(_ffn_dwin, 0, x, dh8, tag + "_dwin")
    if last is not None:
        load[key_in] = dw_in
        slots[1].append(key_in)
    dx = run(_ffn_dx, 1, dh8, wt, dz, alpha, tag + "_dx")
    return dx, dw_in, dw_out, dg, db, arrived


def _fox_fwd(x, xb, w_in_pad, b_f_pad, w_o, g, b, alpha, tag, ex=None):
    S, D = x.shape
    H = D // HEAD_DIM
    proj = _mm(xb, w_in_pad, name=tag + "_proj")
    fl = proj[:, 3 * D:]
    cum = _fox_gate_fwd(fl, b_f_pad)
    cum2 = (cum[:, :H].T * LOG2E).reshape(H // 2, 2, S).transpose(0, 2, 1)
    o, lse, extra = _flash_fwd(proj, cum2, ex)
    m = _mm(o, w_o, name=tag + "_out")
    out, outb, xh, rstd = _res_ln_fwd(x, m, g, b, alpha, 1.0, tag + "_ln")
    return out, outb, (xb, proj, jnp.concatenate([cum2, lse], axis=2), o, fl, xh, rstd), extra


def _fox_bwd(dout, saved, w_in_pad, b_f_pad, w_o, g, alpha, tag, ex=None):
    x, proj, rows, o, fl, xh, rstd = saved
    S, D = x.shape
    H = D // HEAD_DIM
    dz, dg, db = _ln_bwd(dout, xh, rstd, g, tag + "_ln_bwd")
    dw_o = _mm(o, dz, ta=True, tk=2048, name=tag + "_dwo")
    do = _mm(dz, w_o, tb=True, name=tag + "_do")
    dq, dk, dv, dcq, dck, extra = _flash_bwd(proj, rows, o, do, ex)
    dcq = dcq.transpose(0, 2, 1).reshape(H, S)
    dcum = jnp.pad((dcq + dck.reshape(H, S)).T, ((0, 0), (0, LANES - H)))
    dfl, dbf = _fox_gate_bwd(dcum, fl, b_f_pad)
    dproj = jnp.concatenate([dq, dk, dv, dfl.astype(BF16)], axis=1)
    dw_in = _mm(x, dproj, ta=True, tk=x.shape[0], name=tag + "_dwin")
    dx = _mm(dproj, w_in_pad, tb=True, add=dz, add_scale=alpha, name=tag + "_dx")
    shards = {"fox_w_in": _split(dw_in[None, :, :3 * D + H], True)[:, 0].astype(BF16),
              "fox_w_o": _split(dw_o[None], False)[:, 0].astype(BF16)}
    return dx, shards, {"fox_b_f": dbf[0, :H], "lnm_g": dg, "lnm_b": db}, extra


def _s5_mats(p):
    lb_re, lb_im, bb_re, bb_im = _s5_discretise(p["a_re"], p["a_im"], p["log_dt"], p["b_re"], p["b_im"])
    rmat = jnp.concatenate([_s5_expand(bb_re.transpose(0, 2, 1)), _s5_expand(bb_im.transpose(0, 2, 1))], axis=2)
    cmat = jnp.concatenate([_s5_expand(p["c_re"]).transpose(0, 2, 1), -_s5_expand(p["c_im"]).transpose(0, 2, 1)],
                           axis=1)
    lam = jnp.concatenate([_s5_slab(lb_re), _s5_slab(lb_im)], axis=2)
    return rmat.astype(BF16), cmat.astype(BF16), lam


def _s5_block_fwd(x, p, w_out, g, b, alpha, tag, ex=None):
    S, D = x.shape
    rmat, cmat, lam = _s5_mats(p)
    dskip = p["d"].reshape(1, D)
    y, yg, hs, extra = _s5_fwd(x, rmat, cmat, lam, dskip, ex)
    vg = _mm(yg, w_out, name=tag + "_vg")
    m = _glu_fwd(vg, tag + "_glu")
    out, outb, xh, rstd = _res_ln_fwd(x, m, g, b, alpha, 1.0, tag + "_ln")
    return out, outb, (x, y, yg, hs, vg, rmat, cmat, lam, dskip, xh, rstd), extra


def _s5_block_bwd(dout, saved, p, w_out, g, alpha, tag, ex=None):
    x, y, yg, hs, vg, rmat, cmat, lam, dskip, xh, rstd = saved
    S, D = x.shape
    G = D // S5_GROUP
    dz, dg, db = _ln_bwd(dout, xh, rstd, g, tag + "_ln_bwd")
    dvg = _glu_bwd(vg, dz, tag + "_glu_bwd")
    dw_out = _mm(yg, dvg, ta=True, tk=yg.shape[0], name=tag + "_dwout")
    dyg = _mm(dvg, w_out, tb=True, name=tag + "_dyg")
    dx, dr, dc, dlam, dd, extra = _s5_bwd(x, y, dyg, hs, rmat, cmat, lam, dskip, dz, alpha, ex)
    dbb_re = _s5_extract(dr[:, :, :LANES]).transpose(0, 2, 1)
    dbb_im = _s5_extract(dr[:, :, LANES:]).transpose(0, 2, 1)
    dc_re = _s5_extract(dc[:, :LANES, :].transpose(0, 2, 1))
    dc_im = -_s5_extract(dc[:, LANES:, :].transpose(0, 2, 1))
    dlb_re = dlam[:, :, :LANES].reshape(G, S5_STATE)
    dlb_im = dlam[:, :, LANES:].reshape(G, S5_STATE)
    _, vjp = jax.vjp(_s5_discretise, p["a_re"], p["a_im"], p["log_dt"], p["b_re"], p["b_im"])
    da_re, da_im, dlog_dt, db_re, db_im = vjp((dlb_re, dlb_im, dbb_re, dbb_im))
    small = dict(s5_a_re=da_re, s5_a_im=da_im, s5_log_dt=dlog_dt, s5_b_re=db_re, s5_b_im=db_im, s5_c_re=dc_re,
                 s5_c_im=dc_im, s5_d=dd.reshape(G, S5_GROUP), lnm_g=dg, lnm_b=db)
    return dx, {"s5_w_out": _split(dw_out[None], True)[:, 0].astype(BF16)}, small, extra


FFN_NAMES = ("ffn1_w_in", "ffn1_w_out", "ffn2_w_in", "ffn2_w_out")
BIG = FFN_NAMES + ("fox_w_in", "fox_w_o", "s5_w_out")
BIG_SPLIT_COLS = {"ffn1_w_in": True, "ffn1_w_out": False, "ffn2_w_in": True, "ffn2_w_out": False,
                  "fox_w_in": True, "fox_w_o": False, "s5_w_out": True}
SMALL = ("ln1_g", "ln1_b", "lnm_g", "lnm_b", "ln2_g", "ln2_b", "fox_b_f", "s5_a_re", "s5_a_im", "s5_log_dt",
         "s5_b_re", "s5_b_im", "s5_c_re", "s5_c_im", "s5_d")
WEIGHTS = ("ffn1_w_in", "ffn1_w_out", "ln1_g", "ln1_b", "lnm_g", "lnm_b", "ffn2_w_in", "ffn2_w_out", "ln2_g", "ln2_b",
           "fox_w_in", "fox_b_f", "fox_w_o", "s5_a_re", "s5_a_im", "s5_log_dt", "s5_b_re", "s5_b_im", "s5_c_re",
           "s5_c_im", "s5_d", "s5_w_out")


def _join(gathered, split_cols):
    n, L, r, c = gathered.shape
    if split_cols:
        return gathered.transpose(1, 2, 0, 3).reshape(L, r, n * c)
    return gathered.transpose(1, 0, 2, 3).reshape(L, n * r, c)


def _split(full, split_cols):
    L, R, C = full.shape
    if split_cols:
        return full.reshape(L, R, N_DEV, C // N_DEV).transpose(2, 0, 1, 3)
    return full.reshape(L, N_DEV, R // N_DEV, C).transpose(1, 0, 2, 3)


def _group(i, part):
    if part == "mixer":
        return (("fox_w_in", i // 2), ("fox_w_o", i // 2)) if i % 2 == 0 else (("s5_w_out", i // 2),)
    return ((part + "_w_in", i), (part + "_w_out", i))


def _prepare(name, g8):
    if name in ("ffn1_w_in", "ffn2_w_in"):
        return g8
    if name in FFN_NAMES:
        n, r, c = g8.shape
        return g8.reshape(n // 2, 2 * r, c)
    full = _join(g8[:, None], BIG_SPLIT_COLS[name])[0]
    if name == "fox_w_in":
        full = jnp.pad(full, ((0, 0), (0, LANES - full.shape[0] // HEAD_DIM)))
    return full


def _local_step(x, target, small, shard_of=None, pregathered=None):
    S, D = x.shape
    H = D // HEAD_DIM
    depth = small["ln1_g"].shape[0]
    alpha = (2.0 * depth) ** 0.25
    local = pregathered is not None
    bf_pad = jnp.pad(small["fox_b_f"], ((0, 0), (0, LANES - H)))

    def s5_params(j):
        return {k: small["s5_" + k][j] for k in ("a_re", "a_im", "log_dt", "b_re", "b_im", "c_re", "c_im", "d")}

    if local:
        W = {k: _prepare(k[0], g8) for k, g8 in pregathered.items()}
    else:
        keys = _group(0, "ffn1")
        got = _exchange(Exchange([shard_of(*k) for k in keys], [False] * len(keys)), "gather_first")
        W = {k: _prepare(k[0], g8) for k, g8 in zip(keys, got)}

    def gather(keys):
        return None if local else Exchange([shard_of(*k) for k in keys], [False] * len(keys))

    def landed(keys, extra):
        if not local:
            W.update({k: _prepare(k[0], g8) for k, g8 in zip(keys, extra)})

    saved = []
    h, hb = x, x.astype(BF16)
    for i in range(depth):
        j = i // 2
        keys = _group(i, "mixer")
        h, hb, s1, extra = _ffn_fwd(h, hb, W[("ffn1_w_in", i)], W[("ffn1_w_out", i)], small["ln1_g"][i],
                                    small["ln1_b"][i], alpha, f"l{i}_ffn1", gather(keys))
        landed(keys, extra)
        keys = _group(i, "ffn2") + (_group(i + 1, "ffn1") if i + 1 < depth else ())
        ex = gather(keys)
        if i % 2 == 0:
            h, hb, s2, extra = _fox_fwd(h, hb, W[("fox_w_in", j)], bf_pad[j:j + 1], W[("fox_w_o", j)],
                                        small["lnm_g"][i], small["lnm_b"][i], alpha, f"l{i}_fox", ex)
        else:
            h, hb, s2, extra = _s5_block_fwd(h, s5_params(j), W[("s5_w_out", j)], small["lnm_g"][i],
                                             small["lnm_b"][i], alpha, f"l{i}_s5", ex)
        landed(keys, extra)
        h, hb, s3, _ = _ffn_fwd(h, hb, W[("ffn2_w_in", i)], W[("ffn2_w_out", i)], small["ln2_g"][i],
                                small["ln2_b"][i], alpha, f"l{i}_ffn2")
        saved.append((s1, s2, s3))

    dh, loss_part = _loss_fwd_bwd(h, target)

    arrived = {}
    pending = {}
    gs = {k: [None] * small[k].shape[0] for k in SMALL}
    for i in reversed(range(depth)):
        j = i // 2
        s1, s2, s3 = saved[i]
        held = {k: pending.pop(k) for k in [("ffn1_w_in", i + 1)] if i % 2 == 0 and k in pending}
        dh, dw_in, dw_out, gs["ln2_g"][i], gs["ln2_b"][i], got = _ffn_bwd(
            dh, s3, W[("ffn2_w_in", i)], W[("ffn2_w_out", i)], small["ln2_g"][i], alpha, f"l{i}_ffn2",
            None if local else pending)
        arrived.update(pending if local else got)
        pending = {("ffn2_w_in", i): dw_in, ("ffn2_w_out", i): dw_out, **held}
        keys = list(pending)
        ex = None if local else Exchange([pending[k] for k in keys], [True] * len(keys))
        if i % 2 == 0:
            dh, mix, sg, extra = _fox_bwd(dh, s2, W[("fox_w_in", j)], bf_pad[j:j + 1], W[("fox_w_o", j)],
                                          small["lnm_g"][i], alpha, f"l{i}_fox", ex)
        else:
            dh, mix, sg, extra = _s5_block_bwd(dh, s2, s5_params(j), W[("s5_w_out", j)], small["lnm_g"][i], alpha,
                                               f"l{i}_s5", ex)
        arrived.update(zip(keys, [pending[k] for k in keys] if local else extra))
        pending = {(k, j): val for k, val in mix.items()}
        for k, val in sg.items():
            gs[k][i if k in ("lnm_g", "lnm_b") else j] = val
        def last(dg, db):
            gs["ln1_g"][0], gs["ln1_b"][0] = dg, db
            packed = _pack_flat([jnp.stack(gs[k]) for k in SMALL]).astype(BF16)
            return (("ffn1_w_out", 0), ("ffn1_w_in", 0)), packed

        dh, dw_in, dw_out, gs["ln1_g"][i], gs["ln1_b"][i], got = _ffn_bwd(
            dh, s1, W[("ffn1_w_in", i)], W[("ffn1_w_out", i)], small["ln1_g"][i], alpha, f"l{i}_ffn1",
            None if local else pending, last if (i == 0 and not local) else None)
        arrived.update(pending if local else got)
        pending = {("ffn1_w_in", i): dw_in, ("ffn1_w_out", i): dw_out}
    if local:
        arrived.update(pending)
    return loss_part, dh, arrived, {k: jnp.stack(v) for k, v in gs.items()}


def kernel(x, ffn1_w_in, ffn1_w_out, ln1_g, ln1_b, lnm_g, lnm_b, ffn2_w_in, ffn2_w_out, ln2_g, ln2_b, fox_w_in, fox_b_f, fox_w_o, s5_a_re, s5_a_im, s5_log_dt, s5_b_re, s5_b_im, s5_c_re, s5_c_im, s5_d, s5_w_out, loss_target, m_ffn1_w_in, m_ffn1_w_out, m_ln1_g, m_ln1_b, m_lnm_g, m_lnm_b, m_ffn2_w_in, m_ffn2_w_out, m_ln2_g, m_ln2_b, m_fox_w_in, m_fox_b_f, m_fox_w_o, m_s5_a_re, m_s5_a_im, m_s5_log_dt, m_s5_b_re, m_s5_b_im, m_s5_c_re, m_s5_c_im, m_s5_d, m_s5_w_out, v_ffn1_w_in, v_ffn1_w_out, v_ln1_g, v_ln1_b, v_lnm_g, v_lnm_b, v_ffn2_w_in, v_ffn2_w_out, v_ln2_g, v_ln2_b, v_fox_w_in, v_fox_b_f, v_fox_w_o, v_s5_a_re, v_s5_a_im, v_s5_log_dt, v_s5_b_re, v_s5_b_im, v_s5_c_re, v_s5_c_im, v_s5_d, v_s5_w_out):
    args = dict(locals())
    w = {k: args[k] for k in WEIGHTS}
    m = {k: args["m_" + k] for k in WEIGHTS}
    v = {k: args["v_" + k] for k in WEIGHTS}
    small = {k: w[k] for k in SMALL}

    turned = lambda d: {k: jnp.swapaxes(a, 1, 2) if k in ("ffn1_w_in", "ffn2_w_in") else a for k, a in d.items()}
    w, m, v = turned(w), turned(m), turned(v)
    wb = {k: w[k].astype(BF16) for k in BIG}
    loss_part, dx, arrived, _ = _local_step(x[0], loss_target[0], small, lambda k, idx: wb[k][idx])
    loss = lax.psum(loss_part, ("x", "y", "c"))
    small_shapes = [w[k].shape for k in SMALL]
    g_small_flat = _sum8(arrived["small"], "sum_small_grads")

    grads, delta, new_m, new_v = {}, {}, {}, {}
    for k in BIG:
        parts = [arrived[(k, l)] for l in range(w[k].shape[0])]
        grads[k], delta[k], new_m[k], new_v[k] = _adamw_recv(parts, w[k], m[k], v[k], "adamw_" + k)
    pk = lambda d: _pack_flat([d[k] for k in SMALL])
    d_, m_, v_ = _adamw(pk(w), g_small_flat, pk(m), pk(v), "adamw_small")
    for dst, flat in ((grads, g_small_flat), (delta, d_), (new_m, m_), (new_v, v_)):
        dst.update(zip(SMALL, _unpack_flat(flat, small_shapes)))
    grads, delta, new_m, new_v = turned(grads), turned(delta), turned(new_m), turned(new_v)

    return (loss, dx[None], *[grads[k] for k in WEIGHTS], *[delta[k] for k in WEIGHTS],
            *[new_m[k] for k in WEIGHTS], *[new_v[k] for k in WEIGHTS])
```

```python
import functools
import math

import jax
import jax.numpy as jnp
from jax import lax
from jax.experimental import pallas as pl
from jax.experimental.pallas import tpu as pltpu

F32 = jnp.float32
BF16 = jnp.bfloat16

N_DEV = 8
HEAD_DIM = 64
S5_GROUP = 16
S5_STATE = 64
LANES = 128
SUBLANES = 8
S5_PART = 256
ATTN_TILE = 512
LN_EPS = 1e-5
NEG_INF = -1e30
LOG2E = 1.4426950408889634
ADAM_LR, ADAM_B1, ADAM_B2, ADAM_EPS, ADAM_WD, ADAM_STEP = 0.001, 0.9, 0.999, 1e-08, 0.01, 10
VMEM_LIMIT = 48 * 1024 * 1024
PACK_COLS = 1024

MESH = pl.DeviceIdType.MESH


def _tile(dim, pref, align=LANES):
    if dim <= pref:
        return dim
    t = (pref // align) * align
    while t >= align:
        if dim % t == 0:
            return t
        t -= align
    return dim


def _params(sem):
    return pltpu.CompilerParams(dimension_semantics=sem, vmem_limit_bytes=VMEM_LIMIT)


def _mm(a, b, *, ta=False, tb=False, out_dtype=F32, scale=None, add=None, add_scale=1.0,
        tm=512, tn=1408, tk=1408, name="mm"):
    if ta:
        K, M = a.shape
    else:
        M, K = a.shape
    if tb:
        N, K2 = b.shape
    else:
        K2, N = b.shape
    assert K == K2, (a.shape, b.shape, ta, tb)
    tm, tn, tk = _tile(M, tm), _tile(N, tn), _tile(K, tk)
    a_spec = pl.BlockSpec((tk, tm), lambda i, j, k: (k, i)) if ta else pl.BlockSpec((tm, tk), lambda i, j, k: (i, k))
    b_spec = pl.BlockSpec((tn, tk), lambda i, j, k: (j, k)) if tb else pl.BlockSpec((tk, tn), lambda i, j, k: (k, j))
    o_spec = pl.BlockSpec((tm, tn), lambda i, j, k: (i, j))
    return _mm_core(name, a, b, a_spec, b_spec, o_spec, jax.ShapeDtypeStruct((M, N), out_dtype),
                    (M // tm, N // tn, K // tk), (tm, tn), ta, tb, scale, add, add_scale)


def _mm_core(name, a, b, a_spec, b_spec, o_spec, out_shape, grid, acc_shape, ta, tb, scale=None, add=None,
             add_scale=1.0, ex=None):
    nk = grid[2]

    def body(*refs):
        if add is None:
            a_ref, b_ref, o_ref, acc = refs
            add_ref = None
        else:
            a_ref, b_ref, add_ref, o_ref, acc = refs
        k = pl.program_id(2)

        @pl.when(k == 0)
        def _():
            acc[...] = jnp.zeros_like(acc)

        dims = (((0 if ta else 1,), (1 if tb else 0,)), ((), ()))
        if len(a_ref.shape) == 3:
            part = sum(lax.dot_general(a_ref[u].astype(BF16), b_ref[u].astype(BF16), dims, preferred_element_type=F32)
                       for u in range(a_ref.shape[0]))
        else:
            part = lax.dot_general(a_ref[...].astype(BF16), b_ref[...].astype(BF16), dims, preferred_element_type=F32)
        acc[...] += part

        @pl.when(k == nk - 1)
        def _():
            r = acc[...]
            if scale is not None:
                r = r * scale
            if add_ref is not None:
                r = r + add_scale * add_ref[...]
            o_ref[...] = r.astype(out_shape.dtype)

    in_specs = [a_spec, b_spec]
    args = [a, b]
    if add is not None:
        in_specs.append(o_spec)
        args.append(add)
    (res,), extra = _call(body, name=name, out_shape=(out_shape,), grid=grid, in_specs=in_specs, out_specs=(o_spec,),
                          scratch_shapes=[pltpu.VMEM(acc_shape, F32)], args=args,
                          semantics=("parallel", "parallel", "arbitrary"), ex=ex)
    return res if ex is None else (res, extra)


def _ffn_h(x, wt, name, ex=None):
    S, D = x.shape
    n, c, _ = wt.shape
    nb = n // 2
    tm = _tile(S, 1024, SUBLANES)
    dims = (((1,), (1,)), ((), ()))

    def body(x_ref, wg_ref, wu_ref, h_ref, a_ref):
        xb = x_ref[...]
        g = lax.dot_general(xb, wg_ref[...], dims, preferred_element_type=F32).astype(BF16)
        u = lax.dot_general(xb, wu_ref[...], dims, preferred_element_type=F32).astype(BF16)
        h_ref[0] = g
        h_ref[1] = u
        g = g.astype(F32)
        a_ref[...] = (g * _sigmoid(g) * u.astype(F32)).astype(BF16)

    (h, a4), extra = _call(
        body, name=name,
        out_shape=(jax.ShapeDtypeStruct((2, nb, S, c), BF16), jax.ShapeDtypeStruct((nb, S, c), BF16)),
        grid=(S // tm, nb),
        in_specs=[pl.BlockSpec((tm, D), lambda i, j: (i, 0)), pl.BlockSpec((None, c, D), lambda i, j: (j, 0, 0)),
                  pl.BlockSpec((None, c, D), lambda i, j: (j + nb, 0, 0))],
        out_specs=(pl.BlockSpec((2, None, tm, c), lambda i, j: (0, j, i, 0)),
                   pl.BlockSpec((None, tm, c), lambda i, j: (j, i, 0))),
        scratch_shapes=[], args=(x, wt, wt), semantics=("parallel", "parallel"), ex=ex)
    return h.reshape(n, S, c), a4, extra


def _ffn_y_ln(a4, wo4, x, g, b, alpha, s, name):
    nb, S, c = a4.shape
    D = wo4.shape[-1]
    tm = _tile(S, 512, SUBLANES)

    def body(a_ref, w_ref, x_ref, g_ref, b_ref, o_ref, ob_ref, xh_ref, r_ref):
        y = jnp.dot(a_ref[0], w_ref[0], preferred_element_type=F32)
        for k in range(1, nb):
            y = y + jnp.dot(a_ref[k], w_ref[k], preferred_element_type=F32)
        z = alpha * x_ref[...] + s * y
        mu = jnp.mean(z, axis=-1, keepdims=True)
        zc = z - mu
        rstd = lax.rsqrt(jnp.mean(zc * zc, axis=-1, keepdims=True) + LN_EPS)
        xh = zc * rstd
        xh_ref[...] = xh
        r_ref[...] = rstd
        out = xh * g_ref[...] + b_ref[...]
        o_ref[...] = out
        ob_ref[...] = out.astype(BF16)

    row = pl.BlockSpec((tm, D), lambda i: (i, 0))
    vec = pl.BlockSpec((1, D), lambda i: (0, 0))
    return pl.pallas_call(
        body, name=name,
        out_shape=(jax.ShapeDtypeStruct((S, D), F32), jax.ShapeDtypeStruct((S, D), BF16),
                   jax.ShapeDtypeStruct((S, D), F32), jax.ShapeDtypeStruct((S, 1), F32)),
        grid=(S // tm,),
        in_specs=[pl.BlockSpec((nb, tm, c), lambda i: (0, i, 0)), pl.BlockSpec((nb, c, D), lambda i: (0, 0, 0)),
                  row, vec, vec],
        out_specs=(row, row, row, pl.BlockSpec((tm, 1), lambda i: (i, 0))),
        compiler_params=_params(("parallel",)),
    )(a4, wo4, x, g.reshape(1, D), b.reshape(1, D))


def _ffn_dwout(a4, dz, scale, name, ex=None):
    nb, S, c = a4.shape
    D = dz.shape[1]
    tk, tn = _tile(S, 2048, SUBLANES), _tile(D, 1024)
    return _mm_core(name, a4, dz, pl.BlockSpec((None, tk, c), lambda i, j, k: (i, k, 0)),
                    pl.BlockSpec((tk, tn), lambda i, j, k: (k, j)),
                    pl.BlockSpec((None, c, tn), lambda i, j, k: (i, 0, j)),
                    jax.ShapeDtypeStruct((nb, c, D), BF16), (nb, D // tn, S // tk), (c, tn), True, False, scale,
                    ex=ex)


def _ffn_dh(dz, wo4, h8, scale, name, ex=None):
    S, D = dz.shape
    nb, c = wo4.shape[0], wo4.shape[1]
    tm = _tile(S, 512, SUBLANES)

    def body(dz_ref, w_ref, h_ref, d_ref):
        da = lax.dot_general(dz_ref[...].astype(BF16), w_ref[...], (((1,), (1,)), ((), ())),
                             preferred_element_type=F32) * scale
        g = h_ref[0].astype(F32)
        u = h_ref[1].astype(F32)
        sg = _sigmoid(g)
        silu = g * sg
        d_ref[0] = (da * u * (sg + silu * (1.0 - sg))).astype(BF16)
        d_ref[1] = (da * silu).astype(BF16)

    pair = pl.BlockSpec((2, None, tm, c), lambda i, j: (0, j, i, 0))
    (dh,), extra = _call(
        body, name=name, out_shape=(jax.ShapeDtypeStruct((2, nb, S, c), BF16),), grid=(S // tm, nb),
        in_specs=[pl.BlockSpec((tm, D), lambda i, j: (i, 0)), pl.BlockSpec((None, c, D), lambda i, j: (j, 0, 0)),
                  pair],
        out_specs=(pair,), scratch_shapes=[], args=(dz, wo4, h8.reshape(2, nb, S, c)),
        semantics=("parallel", "parallel"), ex=ex)
    dh = dh.reshape(2 * nb, S, c)
    return dh if ex is None else (dh, extra)


def _ffn_dwin(x, dh8, name, ex=None):
    S, D = x.shape
    n, _, c = dh8.shape
    return _mm_core(name, dh8, x, pl.BlockSpec((None, S, c), lambda i, j, k: (i, 0, 0)),
                    pl.BlockSpec((S, D), lambda i, j, k: (0, 0)),
                    pl.BlockSpec((None, c, D), lambda i, j, k: (i, 0, 0)),
                    jax.ShapeDtypeStruct((n, c, D), BF16), (n, 1, 1), (c, D), True, False, ex=ex)


def _ffn_dx(dh8, wt, dz, alpha, name, ex=None):
    n, S, c = dh8.shape
    D = wt.shape[2]
    tm, tn = _tile(S, 1024, SUBLANES), _tile(D, 1024)
    return _mm_core(name, dh8, wt, pl.BlockSpec((2, tm, c), lambda i, j, k: (k, i, 0)),
                    pl.BlockSpec((2, c, tn), lambda i, j, k: (k, 0, j)),
                    pl.BlockSpec((tm, tn), lambda i, j, k: (i, j)),
                    jax.ShapeDtypeStruct((S, D), F32), (S // tm, D // tn, n // 2), (tm, tn), False, False,
                    None, dz, alpha, ex=ex)


def _res_ln_fwd(x, y, g, b, alpha, s, name):
    S, D = x.shape
    tm = _tile(S, 256, SUBLANES)

    def body(x_ref, y_ref, g_ref, b_ref, o_ref, ob_ref, xh_ref, r_ref):
        z = alpha * x_ref[...] + s * y_ref[...]
        mu = jnp.mean(z, axis=-1, keepdims=True)
        zc = z - mu
        var = jnp.mean(zc * zc, axis=-1, keepdims=True)
        rstd = lax.rsqrt(var + LN_EPS)
        xh = zc * rstd
        xh_ref[...] = xh
        r_ref[...] = rstd
        out = xh * g_ref[...] + b_ref[...]
        o_ref[...] = out
        ob_ref[...] = out.astype(BF16)

    row = pl.BlockSpec((tm, D), lambda i: (i, 0))
    vec = pl.BlockSpec((1, D), lambda i: (0, 0))
    return pl.pallas_call(
        body, name=name,
        out_shape=(jax.ShapeDtypeStruct((S, D), F32), jax.ShapeDtypeStruct((S, D), BF16),
                   jax.ShapeDtypeStruct((S, D), F32), jax.ShapeDtypeStruct((S, 1), F32)),
        grid=(S // tm,), in_specs=[row, row, vec, vec],
        out_specs=(row, row, row, pl.BlockSpec((tm, 1), lambda i: (i, 0))),
        compiler_params=_params(("parallel",)),
    )(x, y, g.reshape(1, D), b.reshape(1, D))


def _ln_bwd(dout, xh, rstd, g, name):
    S, D = dout.shape
    tm = _tile(S, 256, SUBLANES)

    def body(d_ref, xh_ref, r_ref, g_ref, dz_ref, dg_ref, db_ref):
        i = pl.program_id(0)

        @pl.when(i == 0)
        def _():
            dg_ref[...] = jnp.zeros_like(dg_ref)
            db_ref[...] = jnp.zeros_like(db_ref)

        d = d_ref[...]
        xhv = xh_ref[...]
        dxh = d * g_ref[...]
        m1 = jnp.mean(dxh, axis=-1, keepdims=True)
        m2 = jnp.mean(dxh * xhv, axis=-1, keepdims=True)
        dz_ref[...] = r_ref[...] * (dxh - m1 - xhv * m2)
        dg_ref[...] += jnp.sum(d * xhv, axis=0, keepdims=True)
        db_ref[...] += jnp.sum(d, axis=0, keepdims=True)

    row = pl.BlockSpec((tm, D), lambda i: (i, 0))
    vec = pl.BlockSpec((1, D), lambda i: (0, 0))
    dz, dg, db = pl.pallas_call(
        body, name=name,
        out_shape=(jax.ShapeDtypeStruct((S, D), F32), jax.ShapeDtypeStruct((1, D), F32),
                   jax.ShapeDtypeStruct((1, D), F32)),
        grid=(S // tm,), in_specs=[row, row, pl.BlockSpec((tm, 1), lambda i: (i, 0)), vec],
        out_specs=(row, vec, vec),
        compiler_params=_params(("arbitrary",)),
    )(dout, xh, rstd, g.reshape(1, D))
    return dz, dg[0], db[0]


def _sigmoid(x):
    return 0.5 * jnp.tanh(0.5 * x) + 0.5


def _glu_fwd(vg, name):
    S, D2 = vg.shape
    D = D2 // 2
    tm = _tile(S, 512, SUBLANES)

    def body(v_ref, g_ref, o_ref):
        o_ref[...] = v_ref[...] * _sigmoid(g_ref[...])

    return pl.pallas_call(
        body, name=name, out_shape=jax.ShapeDtypeStruct((S, D), F32), grid=(S // tm,),
        in_specs=[pl.BlockSpec((tm, D), lambda i: (i, 0)), pl.BlockSpec((tm, D), lambda i: (i, 1))],
        out_specs=pl.BlockSpec((tm, D), lambda i: (i, 0)),
        compiler_params=_params(("parallel",)),
    )(vg, vg)


def _glu_bwd(vg, dm, name):
    S, D2 = vg.shape
    D = D2 // 2
    tm = _tile(S, 512, SUBLANES)

    def body(v_ref, g_ref, dm_ref, dv_ref, dg_ref):
        sg = _sigmoid(g_ref[...])
        d = dm_ref[...]
        dv_ref[...] = (d * sg).astype(BF16)
        dg_ref[...] = (d * v_ref[...] * sg * (1.0 - sg)).astype(BF16)

    blk = pl.BlockSpec((tm, D), lambda i: (i, 0))
    dv, dg = pl.pallas_call(
        body, name=name,
        out_shape=(jax.ShapeDtypeStruct((S, D), BF16), jax.ShapeDtypeStruct((S, D), BF16)),
        grid=(S // tm,), in_specs=[blk, pl.BlockSpec((tm, D), lambda i: (i, 1)), blk],
        out_specs=(blk, blk), compiler_params=_params(("parallel",)),
    )(vg, vg, dm)
    return jnp.concatenate([dv, dg], axis=1)


def _loss_fwd_bwd(y, target):
    S, D = y.shape
    tm = _tile(S, 256, SUBLANES)

    def body(y_ref, t_ref, dy_ref, l_ref):
        i = pl.program_id(0)

        @pl.when(i == 0)
        def _():
            l_ref[...] = jnp.zeros_like(l_ref)

        e = y_ref[...] - t_ref[...]
        dy_ref[...] = e * (1.0 / D)
        l_ref[...] += jnp.sum(e * e, axis=0, keepdims=True) * (0.5 / D)

    row = pl.BlockSpec((tm, D), lambda i: (i, 0))
    dy, part = pl.pallas_call(
        body, name="loss", out_shape=(jax.ShapeDtypeStruct((S, D), F32), jax.ShapeDtypeStruct((1, D), F32)),
        grid=(S // tm,), in_specs=[row, row], out_specs=(row, pl.BlockSpec((1, D), lambda i: (0, 0))),
        compiler_params=_params(("arbitrary",)),
    )(y, target)
    return dy, jnp.sum(part)


def _tri(n, lower):
    r = lax.broadcasted_iota(jnp.int32, (n, n), 0)
    c = lax.broadcasted_iota(jnp.int32, (n, n), 1)
    return jnp.where((c <= r) if lower else (c >= r), 1.0, 0.0)


def _fox_gate_fwd(fl, bf):
    S, W = fl.shape
    tm = _tile(S, 256, SUBLANES)

    def body(fl_ref, b_ref, c_ref, carry):
        i = pl.program_id(0)

        @pl.when(i == 0)
        def _():
            carry[...] = jnp.zeros_like(carry)

        x = fl_ref[...] + b_ref[...]
        lf = jnp.minimum(x, 0.0) - jnp.log(1.0 + jnp.exp(-jnp.abs(x)))
        c_ref[...] = jnp.dot(_tri(tm, True), lf, precision=lax.Precision.HIGHEST,
                             preferred_element_type=F32) + carry[...]
        carry[...] += jnp.sum(lf, axis=0, keepdims=True)

    blk = pl.BlockSpec((tm, W), lambda i: (i, 0))
    return pl.pallas_call(
        body, name="fox_gate_fwd", out_shape=jax.ShapeDtypeStruct((S, W), F32), grid=(S // tm,),
        in_specs=[blk, pl.BlockSpec((1, W), lambda i: (0, 0))], out_specs=blk,
        scratch_shapes=[pltpu.VMEM((1, W), F32)], compiler_params=_params(("arbitrary",)),
    )(fl, bf)


def _fox_gate_bwd(dcum, fl, bf):
    S, W = fl.shape
    tm = _tile(S, 256, SUBLANES)
    nb = S // tm

    def body(dc_ref, fl_ref, b_ref, dfl_ref, db_ref, carry):
        i = pl.program_id(0)

        @pl.when(i == 0)
        def _():
            carry[...] = jnp.zeros_like(carry)
            db_ref[...] = jnp.zeros_like(db_ref)

        dc = dc_ref[...]
        r = jnp.dot(_tri(tm, False), dc, precision=lax.Precision.HIGHEST, preferred_element_type=F32) + carry[...]
        carry[...] += jnp.sum(dc, axis=0, keepdims=True)
        x = fl_ref[...] + b_ref[...]
        e = jnp.exp(-jnp.abs(x))
        dfl = r * jnp.where(x >= 0, e, 1.0) / (1.0 + e)
        dfl_ref[...] = dfl
        db_ref[...] += jnp.sum(dfl, axis=0, keepdims=True)

    blk = pl.BlockSpec((tm, W), lambda i: (nb - 1 - i, 0))
    vec = pl.BlockSpec((1, W), lambda i: (0, 0))
    return pl.pallas_call(
        body, name="fox_gate_bwd",
        out_shape=(jax.ShapeDtypeStruct((S, W), F32), jax.ShapeDtypeStruct((1, W), F32)),
        grid=(nb,), in_specs=[blk, blk, vec], out_specs=(blk, vec),
        scratch_shapes=[pltpu.VMEM((1, W), F32)], compiler_params=_params(("arbitrary",)),
    )(dcum, fl, bf)


def _causal(t):
    row = lax.broadcasted_iota(jnp.int32, (t, t), 0)
    col = lax.broadcasted_iota(jnp.int32, (t, t), 1)
    return col <= row


def _first_head(shape):
    return lax.broadcasted_iota(jnp.int32, shape, len(shape) - 1) < HEAD_DIM


def _split3(x):
    hi = x.astype(BF16).astype(F32)
    r = x - hi
    mid = r.astype(BF16).astype(F32)
    return hi, mid, (r - mid).astype(BF16).astype(F32)


def _bias_lanes(c, query):
    lane = lax.broadcasted_iota(jnp.int32, (c.shape[0], LANES), 1)
    hi, mid, lo = _split3(c)
    if query:
        out = jnp.where(lane == 0, hi, jnp.where(lane == 1, mid, jnp.where(lane == 2, lo,
                                                                          jnp.where(lane < 6, 1.0, 0.0))))
    else:
        out = jnp.where(lane < 3, 1.0, jnp.where(lane == 3, -hi, jnp.where(lane == 4, -mid,
                                                                          jnp.where(lane == 5, -lo, 0.0))))
    return out.astype(BF16)


def _flash_fwd(proj, cum2, ex=None):
    S = proj.shape[0]
    D = (proj.shape[1] - LANES) // 3
    HP = D // LANES
    t = _tile(S, ATTN_TILE)
    nq = S // t
    scale = 1.0 / math.sqrt(HEAD_DIM)

    def body(q_ref, k_ref, v_ref, cq_ref, ck_ref, o_ref, lse_ref, kx, vb):
        qi = pl.program_id(1)

        @pl.when(qi == 0)
        def _():
            def prep(kb, c):
                sl = pl.ds(pl.multiple_of(kb * t, t), t)
                k16 = k_ref[sl, :].astype(BF16)
                ckv = ck_ref[0, sl, :]
                for a in range(2):
                    kx[a, sl, 0:LANES] = k16
                    kx[a, sl, LANES:2 * LANES] = _bias_lanes(ckv[:, a:a + 1], False)
                vb[sl, :] = v_ref[sl, :].astype(BF16)
                return c

            lax.fori_loop(0, nq, prep, 0)

        first = _first_head((t, LANES))
        qf = q_ref[...] * (scale * LOG2E)
        cqv = cq_ref[0]
        qx = [jnp.concatenate([jnp.where(keep, qf, 0.0).astype(BF16), _bias_lanes(cqv[:, a:a + 1], True)], axis=1)
              for a, keep in enumerate((first, jnp.logical_not(first)))]

        def block(ki, carry, masked):
            m_old, l_old, acc = carry
            sl = pl.ds(pl.multiple_of(ki * t, t), t)
            vv = vb[sl, :]
            m_new, l_new, corr, pv = [], [], [], []
            for a in range(2):
                s = lax.dot_general(qx[a], kx[a, sl, :], (((1,), (1,)), ((), ())), preferred_element_type=F32)
                if masked:
                    s = jnp.where(_causal(t), s, NEG_INF)
                m_a = jnp.maximum(m_old[a], jnp.max(s, axis=1, keepdims=True))
                p = jnp.exp2(s - m_a)
                c_a = jnp.exp2(m_old[a] - m_a)
                m_new.append(m_a)
                corr.append(c_a)
                l_new.append(c_a * l_old[a] + jnp.sum(p, axis=1, keepdims=True))
                pv.append(jnp.dot(p.astype(BF16), vv, preferred_element_type=F32))
            acc = jnp.where(first, corr[0] * acc + pv[0], corr[1] * acc + pv[1])
            return tuple(m_new), tuple(l_new), acc

        neg = jnp.full((t, 1), NEG_INF, F32)
        zero = jnp.zeros((t, 1), F32)
        carry = lax.fori_loop(0, qi, lambda ki, c: block(ki, c, False),
                              ((neg, neg), (zero, zero), jnp.zeros((t, LANES), F32)))
        m, l, acc = block(qi, carry, True)
        o_ref[...] = acc / jnp.where(first, l[0], l[1])
        lse_ref[0, :, 0:1] = m[0] + jnp.log2(l[0])
        lse_ref[0, :, 1:2] = m[1] + jnp.log2(l[1])

    qblk = pl.BlockSpec((t, LANES), lambda h, i: (i, h))
    r2 = pl.BlockSpec((1, t, 2), lambda h, i: (h, i, 0))
    (o, lse), extra = _call(
        body, name="fox_attn_fwd",
        out_shape=(jax.ShapeDtypeStruct((S, D), F32), jax.ShapeDtypeStruct((HP, S, 2), F32)),
        grid=(HP, nq),
        in_specs=[qblk, pl.BlockSpec((S, LANES), lambda h, i: (0, HP + h)),
                  pl.BlockSpec((S, LANES), lambda h, i: (0, 2 * HP + h)), r2,
                  pl.BlockSpec((1, S, 2), lambda h, i: (h, 0, 0))],
        out_specs=(qblk, r2),
        scratch_shapes=[pltpu.VMEM((2, S, 2 * LANES), BF16), pltpu.VMEM((S, LANES), BF16)],
        args=(proj, proj, proj, cum2, cum2), semantics=("parallel", "arbitrary"), ex=ex)
    return o, lse, extra


def _flash_bwd(proj, rows, o, do, ex=None):
    S = proj.shape[0]
    D = (proj.shape[1] - LANES) // 3
    HP = D // LANES
    t = _tile(S, ATTN_TILE)
    nb = S // t
    scale = 1.0 / math.sqrt(HEAD_DIM)

    def body(q_ref, k_ref, v_ref, rows_ref, o_ref, do_ref, dq_ref, dk_ref, dv_ref, dcq_ref, dck_ref,
             q_s, do_s, dl_s, dq_acc, dk_acc, dv_acc, dc_acc):
        kb = pl.program_id(1)

        @pl.when(kb == 0)
        def _():
            dq_acc[...] = jnp.zeros_like(dq_acc)
            dcq_ref[...] = jnp.zeros_like(dcq_ref)

            def prep(qb, c):
                sl = pl.ds(pl.multiple_of(qb * t, t), t)
                first = _first_head((t, LANES))
                qf = q_ref[sl, :] * (scale * LOG2E)
                dof = do_ref[sl, :]
                prod = dof * o_ref[sl, :]
                rv = rows_ref[0, sl, :]
                for a, keep in enumerate((first, jnp.logical_not(first))):
                    q_s[a, sl, 0:LANES] = jnp.where(keep, qf, 0.0).astype(BF16)
                    q_s[a, sl, LANES:2 * LANES] = _bias_lanes(rv[:, a:a + 1], True)
                    do_s[a, sl, :] = jnp.where(keep, dof, 0.0).astype(BF16)
                    dl_s[sl, a:a + 1] = jnp.sum(jnp.where(keep, prod, 0.0), axis=1, keepdims=True)
                return c

            lax.fori_loop(0, nb, prep, 0)

        dk_acc[...] = jnp.zeros_like(dk_acc)
        dv_acc[...] = jnp.zeros_like(dv_acc)
        dc_acc[...] = jnp.zeros_like(dc_acc)
        first = _first_head((t, LANES))
        kf = k_ref[...]
        kk = kf.astype(BF16)
        k_own = (jnp.where(first, kf, 0.0).astype(BF16), jnp.where(first, 0.0, kf).astype(BF16))
        ckv = rows_ref[0, pl.ds(pl.multiple_of(kb * t, t), t), :]
        kx = [jnp.concatenate([kk, _bias_lanes(ckv[:, a:a + 1], False)], axis=1) for a in range(2)]
        vv = v_ref[...].astype(BF16)

        def block(qb, masked):
            sl = pl.ds(pl.multiple_of(qb * t, t), t)
            rv = rows_ref[0, sl, :]
            dlv = dl_s[sl, :]
            dq_new = dq_acc[sl, :]
            for a in range(2):
                dob = do_s[a, sl, :]
                s = lax.dot_general(q_s[a, sl, :], kx[a], (((1,), (1,)), ((), ())), preferred_element_type=F32)
                if masked:
                    s = jnp.where(_causal(t), s, NEG_INF)
                p = jnp.exp2(s - rv[:, 2 + a:3 + a])
                dv_acc[...] += lax.dot_general(p.astype(BF16), dob, (((0,), (0,)), ((), ())),
                                               preferred_element_type=F32)
                dp = lax.dot_general(dob, vv, (((1,), (1,)), ((), ())), preferred_element_type=F32)
                ds = p * (dp - dlv[:, a:a + 1])
                dsb = ds.astype(BF16)
                dk_acc[...] += lax.dot_general(dsb, q_s[a, sl, 0:LANES], (((0,), (0,)), ((), ())),
                                               preferred_element_type=F32)
                dq_new = dq_new + jnp.dot(dsb, k_own[a], preferred_element_type=F32) * scale
                dcq_ref[0, sl, a:a + 1] += jnp.sum(ds, axis=1, keepdims=True)
                dc_acc[a:a + 1, :] -= jnp.sum(ds, axis=0, keepdims=True)
            dq_acc[sl, :] = dq_new

        block(kb, True)

        def rest(qb, c):
            block(qb, False)
            return c

        lax.fori_loop(kb + 1, nb, rest, 0)
        dk_ref[...] = (dk_acc[...] * (1.0 / LOG2E)).astype(BF16)
        dv_ref[...] = dv_acc[...].astype(BF16)
        dck_ref[0, 0] = dc_acc[0:1, :]
        dck_ref[1, 0] = dc_acc[1:2, :]

        @pl.when(kb == nb - 1)
        def _():
            dq_ref[...] = dq_acc[...].astype(BF16)

    full = lambda c0: pl.BlockSpec((S, LANES), lambda h, j, c0=c0: (0, c0 + h))
    blk = lambda c0: pl.BlockSpec((t, LANES), lambda h, j, c0=c0: (j, c0 + h))
    f32 = lambda *s: jax.ShapeDtypeStruct(s, F32)
    b16 = jax.ShapeDtypeStruct((S, D), BF16)
    outs, extra = _call(
        body, name="fox_attn_bwd",
        out_shape=(b16, b16, b16, f32(HP, S, 2), f32(2 * HP, nb, 1, t)),
        grid=(HP, nb),
        in_specs=[full(0), blk(HP), blk(2 * HP), pl.BlockSpec((1, S, 4), lambda h, j: (h, 0, 0)), full(0), full(0)],
        out_specs=(full(0), blk(0), blk(0), pl.BlockSpec((1, S, 2), lambda h, j: (h, 0, 0)),
                   pl.BlockSpec((2, 1, 1, t), lambda h, j: (h, j, 0, 0))),
        scratch_shapes=[pltpu.VMEM((2, S, 2 * LANES), BF16), pltpu.VMEM((2, S, LANES), BF16),
                        pltpu.VMEM((S, 2), F32), pltpu.VMEM((S, LANES), F32), pltpu.VMEM((t, LANES), F32),
                        pltpu.VMEM((t, LANES), F32), pltpu.VMEM((2, t), F32)],
        args=(proj, proj, proj, rows, o, do), semantics=("parallel", "arbitrary"), ex=ex)
    return (*outs, extra)


def _s5_consts(T):
    rows = T * SUBLANES
    rr = lax.broadcasted_iota(jnp.int32, (rows, T), 0)
    tt = lax.broadcasted_iota(jnp.int32, (rows, T), 1)
    rep = jnp.where(rr // SUBLANES == tt, 1.0, 0.0).astype(BF16)
    r2 = lax.broadcasted_iota(jnp.int32, (rows, S5_PART), 0)
    c2 = lax.broadcasted_iota(jnp.int32, (rows, S5_PART), 1)
    mask = (c2 // (S5_PART // SUBLANES)) == (r2 % SUBLANES)
    return rep, mask


def _gelu(y):
    c = math.sqrt(2.0 / math.pi)
    return 0.5 * y * (1.0 + jnp.tanh(c * (y + 0.044715 * y * y * y)))


def _gelu_grad(y):
    c = math.sqrt(2.0 / math.pi)
    th = jnp.tanh(c * (y + 0.044715 * y * y * y))
    return 0.5 * (1.0 + th) + 0.5 * y * (1.0 - th * th) * c * (1.0 + 3.0 * 0.044715 * y * y)


def _s5_fwd(x, rmat, cmat, lam, dskip, ex=None):
    S, D = x.shape
    NQ = D // S5_PART
    T = _tile(S, 128, SUBLANES)
    rows = T * SUBLANES

    def body(x_ref, r_ref, c_ref, lam_ref, d_ref, y_ref, yg_ref, h_ref, bu_s, carry):
        i = pl.program_id(0)

        @pl.when(i == 0)
        def _():
            carry[...] = jnp.zeros_like(carry)

        rep, mask = _s5_consts(T)
        cols = [pl.ds(q * S5_PART, S5_PART) for q in range(NQ)]
        for q in range(NQ):
            xrep = jnp.dot(rep, x_ref[:, cols[q]].astype(BF16), preferred_element_type=F32)
            lx = jnp.where(mask, xrep, 0.0).astype(BF16)
            bu_s[q] = jnp.dot(lx, r_ref[q], preferred_element_type=F32)
        lam_v = [(lam_ref[q, :, 0:LANES], lam_ref[q, :, LANES:2 * LANES]) for q in range(NQ)]

        def step(t, c):
            o = pl.multiple_of(t * SUBLANES, SUBLANES)
            new = []
            for q in range(NQ):
                hr, hi = c[q]
                ar, ai = lam_v[q]
                sl = bu_s[q, pl.ds(o, SUBLANES), :]
                nhr = ar * hr - ai * hi + sl[:, 0:LANES]
                nhi = ar * hi + ai * hr + sl[:, LANES:2 * LANES]
                h_ref[q, pl.ds(o, SUBLANES), 0:LANES] = nhr
                h_ref[q, pl.ds(o, SUBLANES), LANES:2 * LANES] = nhi
                new.append((nhr, nhi))
            return tuple(new)

        fin = lax.fori_loop(0, T, step,
                            tuple((carry[q, :, 0:LANES], carry[q, :, LANES:2 * LANES]) for q in range(NQ)))
        for q in range(NQ):
            carry[q, :, 0:LANES] = fin[q][0]
            carry[q, :, LANES:2 * LANES] = fin[q][1]
            z = jnp.dot(h_ref[q].astype(BF16), c_ref[q], preferred_element_type=F32)
            z = jnp.where(mask, z, 0.0)
            y = jnp.sum(z.reshape(T, SUBLANES, S5_PART), axis=1) + d_ref[:, cols[q]] * x_ref[:, cols[q]]
            y_ref[:, cols[q]] = y
            yg_ref[:, cols[q]] = _gelu(y).astype(BF16)

    xs = pl.BlockSpec((T, D), lambda i: (i, 0))
    ms = pl.BlockSpec((NQ, S5_PART, S5_PART), lambda i: (0, 0, 0))
    outs, extra = _call(
        body, name="s5_scan_fwd",
        out_shape=(jax.ShapeDtypeStruct((S, D), F32), jax.ShapeDtypeStruct((S, D), BF16),
                   jax.ShapeDtypeStruct((NQ, S * SUBLANES, S5_PART), F32)),
        grid=(S // T,),
        in_specs=[xs, ms, ms, pl.BlockSpec((NQ, SUBLANES, S5_PART), lambda i: (0, 0, 0)),
                  pl.BlockSpec((1, D), lambda i: (0, 0))],
        out_specs=(xs, xs, pl.BlockSpec((NQ, rows, S5_PART), lambda i: (0, i, 0))),
        scratch_shapes=[pltpu.VMEM((NQ, rows, S5_PART), F32), pltpu.VMEM((NQ, SUBLANES, S5_PART), F32)],
        args=(x, rmat, cmat, lam, dskip), semantics=("arbitrary",), ex=ex)
    return (*outs, extra)


def _s5_bwd(x, y, dyg, hs, rmat, cmat, lam, dskip, res, res_scale, ex=None):
    S, D = x.shape
    NQ = D // S5_PART
    T = _tile(S, 128, SUBLANES)
    nb = S // T
    rows = T * SUBLANES

    def body(x_ref, y_ref, dyg_ref, res_ref, h_ref, hp_ref, r_ref, c_ref, lam_ref, d_ref,
             dx_ref, dr_ref, dc_ref, dlam_ref, dd_ref, dh_s, g_s, hs_s, carry):
        i = pl.program_id(0)

        @pl.when(i == 0)
        def _():
            carry[...] = jnp.zeros_like(carry)
            dr_ref[...] = jnp.zeros_like(dr_ref)
            dc_ref[...] = jnp.zeros_like(dc_ref)
            dlam_ref[...] = jnp.zeros_like(dlam_ref)
            dd_ref[...] = jnp.zeros_like(dd_ref)

        rep, mask = _s5_consts(T)
        cols = [pl.ds(q * S5_PART, S5_PART) for q in range(NQ)]
        dys, ldys = [], []
        for q in range(NQ):
            dy = dyg_ref[:, cols[q]] * _gelu_grad(y_ref[:, cols[q]])
            dyrep = jnp.dot(rep, dy.astype(BF16), preferred_element_type=F32)
            ldy = jnp.where(mask, dyrep, 0.0).astype(BF16)
            dh_s[q] = lax.dot_general(ldy, c_ref[q], (((1,), (1,)), ((), ())), preferred_element_type=F32)
            dys.append(dy)
            ldys.append(ldy)
        lam_v = [(lam_ref[q, :, 0:LANES], lam_ref[q, :, LANES:2 * LANES]) for q in range(NQ)]

        def step(n, c):
            o = pl.multiple_of((T - 1 - n) * SUBLANES, SUBLANES)
            new = []
            for q in range(NQ):
                gr, gi = c[q]
                ar, ai = lam_v[q]
                sl = dh_s[q, pl.ds(o, SUBLANES), :]
                ngr = sl[:, 0:LANES] + ar * gr + ai * gi
                ngi = sl[:, LANES:2 * LANES] - ai * gr + ar * gi
                g_s[q, pl.ds(o, SUBLANES), 0:LANES] = ngr
                g_s[q, pl.ds(o, SUBLANES), LANES:2 * LANES] = ngi
                new.append((ngr, ngi))
            return tuple(new)

        fin = lax.fori_loop(0, T, step,
                            tuple((carry[q, :, 0:LANES], carry[q, :, LANES:2 * LANES]) for q in range(NQ)))
        for q in range(NQ):
            carry[q, :, 0:LANES] = fin[q][0]
            carry[q, :, LANES:2 * LANES] = fin[q][1]
            xv = x_ref[:, cols[q]]
            hv = h_ref[q]
            hs_s[0:SUBLANES, :] = jnp.where(i == nb - 1, 0.0, hp_ref[q])
            hs_s[SUBLANES:rows + SUBLANES, :] = hv
            hprev = hs_s[0:rows, :]
            gv = g_s[q]
            g_re, g_im = gv[:, 0:LANES], gv[:, LANES:2 * LANES]
            hp_re, hp_im = hprev[:, 0:LANES], hprev[:, LANES:2 * LANES]
            dar = jnp.sum((g_re * hp_re + g_im * hp_im).reshape(T, SUBLANES, LANES), axis=0)
            dai = jnp.sum((g_im * hp_re - g_re * hp_im).reshape(T, SUBLANES, LANES), axis=0)
            dlam_ref[q, :, 0:LANES] += dar
            dlam_ref[q, :, LANES:2 * LANES] += dai

            gb = gv.astype(BF16)
            xrep = jnp.dot(rep, xv.astype(BF16), preferred_element_type=F32)
            lx = jnp.where(mask, xrep, 0.0).astype(BF16)
            dr_ref[q] += lax.dot_general(lx, gb, (((0,), (0,)), ((), ())), preferred_element_type=F32)
            dc_ref[q] += lax.dot_general(hv.astype(BF16), ldys[q], (((0,), (0,)), ((), ())),
                                         preferred_element_type=F32)
            zx = lax.dot_general(gb, r_ref[q], (((1,), (1,)), ((), ())), preferred_element_type=F32)
            zx = jnp.where(mask, zx, 0.0)
            dx_ref[:, cols[q]] = (jnp.sum(zx.reshape(T, SUBLANES, S5_PART), axis=1) + d_ref[:, cols[q]] * dys[q]
                                  + res_scale * res_ref[:, cols[q]])
            dd_ref[:, cols[q]] += jnp.sum(dys[q] * xv, axis=0, keepdims=True)

    xs = pl.BlockSpec((T, D), lambda i: (nb - 1 - i, 0))
    ms = pl.BlockSpec((NQ, S5_PART, S5_PART), lambda i: (0, 0, 0))
    ls = pl.BlockSpec((NQ, SUBLANES, S5_PART), lambda i: (0, 0, 0))
    ds_ = pl.BlockSpec((1, D), lambda i: (0, 0))
    outs, extra = _call(
        body, name="s5_scan_bwd",
        out_shape=(jax.ShapeDtypeStruct((S, D), F32), jax.ShapeDtypeStruct((NQ, S5_PART, S5_PART), F32),
                   jax.ShapeDtypeStruct((NQ, S5_PART, S5_PART), F32),
                   jax.ShapeDtypeStruct((NQ, SUBLANES, S5_PART), F32), jax.ShapeDtypeStruct((1, D), F32)),
        grid=(nb,),
        in_specs=[xs, xs, xs, xs, pl.BlockSpec((NQ, rows, S5_PART), lambda i: (0, nb - 1 - i, 0)),
                  pl.BlockSpec((NQ, SUBLANES, S5_PART), lambda i: (0, jnp.maximum((nb - 1 - i) * T - 1, 0), 0)),
                  ms, ms, ls, ds_],
        out_specs=(xs, ms, ms, ls, ds_),
        scratch_shapes=[pltpu.VMEM((NQ, rows, S5_PART), F32), pltpu.VMEM((NQ, rows, S5_PART), F32),
                        pltpu.VMEM((rows + SUBLANES, S5_PART), F32), pltpu.VMEM((NQ, SUBLANES, S5_PART), F32)],
        args=(x, y, dyg, res, hs, hs, rmat, cmat, lam, dskip), semantics=("arbitrary",), ex=ex)
    return (*outs, extra)


def _s5_discretise(a_re, a_im, log_dt, b_re, b_im):
    dt = jnp.exp(log_dt)[:, None]
    mag = jnp.exp(a_re * dt)
    ang = a_im * dt
    lb_re = mag * jnp.cos(ang)
    lb_im = mag * jnp.sin(ang)
    den = a_re * a_re + a_im * a_im
    nr = lb_re - 1.0
    ni = lb_im
    z_re = (nr * a_re + ni * a_im) / den
    z_im = (ni * a_re - nr * a_im) / den
    bb_re = z_re[..., None] * b_re - z_im[..., None] * b_im
    bb_im = z_re[..., None] * b_im + z_im[..., None] * b_re
    return lb_re, lb_im, bb_re, bb_im


def _s5_expand(w):
    G = w.shape[0]
    NQ = G // 16
    base = w.reshape(NQ, S5_PART, S5_STATE)
    half = (jnp.arange(S5_PART) // S5_GROUP) % 2
    sel = (half[:, None] == jnp.arange(2)[None, :]).astype(w.dtype)
    out = base[:, :, None, :] * sel[None, :, :, None]
    return out.reshape(NQ, S5_PART, 2 * S5_STATE)


def _s5_extract(m):
    NQ = m.shape[0]
    half = (jnp.arange(S5_PART) // S5_GROUP) % 2
    sel = (half[:, None] == jnp.arange(2)[None, :]).astype(m.dtype)
    base = jnp.sum(m.reshape(NQ, S5_PART, 2, S5_STATE) * sel[None, :, :, None], axis=2)
    return base.reshape(NQ * 16, S5_GROUP, S5_STATE)


def _s5_slab(v):
    return v.reshape(v.shape[0] // 16, SUBLANES, LANES)


def _adamw(w, g, m, v, name):
    R, C = w.shape
    tr = _tile(R, 256, SUBLANES)
    c1 = 1.0 / (1.0 - ADAM_B1 ** ADAM_STEP)
    c2 = 1.0 / (1.0 - ADAM_B2 ** ADAM_STEP)

    def body(w_ref, g_ref, m_ref, v_ref, d_ref, nm_ref, nv_ref):
        gv = g_ref[...]
        nm = ADAM_B1 * m_ref[...] + (1.0 - ADAM_B1) * gv
        nv = ADAM_B2 * v_ref[...] + (1.0 - ADAM_B2) * (gv * gv)
        nm_ref[...] = nm
        nv_ref[...] = nv
        d_ref[...] = -ADAM_LR * ((nm * c1) / (jnp.sqrt(nv * c2) + ADAM_EPS) + ADAM_WD * w_ref[...])

    blk = pl.BlockSpec((tr, C), lambda i: (i, 0))
    sh = jax.ShapeDtypeStruct((R, C), F32)
    return pl.pallas_call(
        body, name=name, out_shape=(sh, sh, sh), grid=(R // tr,), in_specs=[blk] * 4, out_specs=(blk,) * 3,
        compiler_params=_params(("parallel",)),
    )(w, g, m, v)


def _adamw_recv(parts, w, m, v, name):
    L, R, C = w.shape
    tr = _tile(R, 256, SUBLANES)
    c1 = 1.0 / (1.0 - ADAM_B1 ** ADAM_STEP)
    c2 = 1.0 / (1.0 - ADAM_B2 ** ADAM_STEP)

    def body(*refs):
        p_refs = refs[:L]
        w_ref, m_ref, v_ref, g_ref, d_ref, nm_ref, nv_ref = refs[L:]
        li = pl.program_id(0)
        for l in range(L):
            @pl.when(li == l)
            def _(p_ref=p_refs[l]):
                gv = p_ref[0].astype(F32)
                for k in range(1, N_DEV):
                    gv = gv + p_ref[k].astype(F32)
                g_ref[...] = gv
                nm = ADAM_B1 * m_ref[...] + (1.0 - ADAM_B1) * gv
                nv = ADAM_B2 * v_ref[...] + (1.0 - ADAM_B2) * (gv * gv)
                nm_ref[...] = nm
                nv_ref[...] = nv
                d_ref[...] = -ADAM_LR * ((nm * c1) / (jnp.sqrt(nv * c2) + ADAM_EPS) + ADAM_WD * w_ref[...])

    p_specs = [pl.BlockSpec((N_DEV, tr, C), lambda li, i, l=l: (0, jnp.where(li == l, i, 0), 0)) for l in range(L)]
    blk = pl.BlockSpec((None, tr, C), lambda li, i: (li, i, 0))
    sh = jax.ShapeDtypeStruct((L, R, C), F32)
    return pl.pallas_call(
        body, name=name, out_shape=(sh, sh, sh, sh), grid=(L, R // tr),
        in_specs=p_specs + [blk, blk, blk], out_specs=(blk,) * 4,
        compiler_params=_params(("parallel", "parallel")),
    )(*parts, w, m, v)


def _sum8(parts, name):
    _, R, C = parts.shape
    tr = _tile(R, 256, SUBLANES)

    def body(p_ref, o_ref):
        acc = p_ref[0].astype(F32)
        for k in range(1, N_DEV):
            acc = acc + p_ref[k].astype(F32)
        o_ref[...] = acc

    return pl.pallas_call(
        body, name=name, out_shape=jax.ShapeDtypeStruct((R, C), F32), grid=(R // tr,),
        in_specs=[pl.BlockSpec((N_DEV, tr, C), lambda i: (0, i, 0))],
        out_specs=pl.BlockSpec((tr, C), lambda i: (i, 0)), compiler_params=_params(("parallel",)),
    )(parts)


def _peers():
    x, y, c = lax.axis_index("x"), lax.axis_index("y"), lax.axis_index("c")
    me = 4 * x + 2 * y + c
    out = []
    for k in range(1, N_DEV):
        kx, ky, kc = (k >> 2) & 1, (k >> 1) & 1, k & 1
        px, py, pc = x ^ kx, y ^ ky, c ^ kc
        out.append(((px, py, pc), 4 * px + 2 * py + pc))
    return me, out


SIBLING = 1
SAME_CORE = (2, 4, 6)


class Exchange:
    def __init__(self, xs, scatter):
        self.xs = list(xs)
        self.scatter = list(scatter)
        self.n = len(self.xs)

    def out_shapes(self):
        return tuple(jax.ShapeDtypeStruct(x.shape if sc else (N_DEV,) + x.shape, x.dtype)
                     for x, sc in zip(self.xs, self.scatter))

    def sems(self):
        return [pltpu.SemaphoreType.DMA((self.n * N_DEV,)), pltpu.SemaphoreType.DMA((self.n * N_DEV,)),
                pltpu.SemaphoreType.DMA((self.n,))]

    def _copy(self, i, m, src, dst, to, send_sems, recv_sems):
        return pltpu.make_async_remote_copy(src_ref=src, dst_ref=dst, send_sem=send_sems.at[i * N_DEV + m],
                                            recv_sem=recv_sems.at[i * N_DEV + m], device_id=to, device_id_type=MESH)

    def start(self, x_refs, o_refs, send_sems, recv_sems, local_sems):
        me, peers = _peers()
        for i, (x_ref, o_ref, sc) in enumerate(zip(x_refs, o_refs, self.scatter)):
            pltpu.make_async_copy(x_ref.at[me] if sc else x_ref, o_ref.at[me], local_sems.at[i]).start()
            for m in (range(1, N_DEV) if sc else (SIBLING,) + SAME_CORE):
                dev, idx = peers[m - 1]
                self._copy(i, m, x_ref.at[idx] if sc else x_ref, o_ref.at[me], dev, send_sems, recv_sems).start()

    def middle(self, x_refs, o_refs, send_sems, recv_sems, local_sems):
        me, peers = _peers()
        sibling = peers[SIBLING - 1][0]
        for i, (x_ref, o_ref, sc) in enumerate(zip(x_refs, o_refs, self.scatter)):
            if sc:
                continue
            for m in SAME_CORE:
                dev, idx = peers[m - 1]
                self._copy(i, m, x_ref, o_ref.at[idx], dev, send_sems, recv_sems).wait_recv()
                self._copy(i, m ^ 1, o_ref.at[idx], o_ref.at[idx], sibling, send_sems, recv_sems).start()

    def wait(self, x_refs, o_refs, send_sems, recv_sems, local_sems):
        me, peers = _peers()
        for i, (x_ref, o_ref, sc) in enumerate(zip(x_refs, o_refs, self.scatter)):
            for m in range(1, N_DEV):
                dev, idx = peers[m - 1]
                cp = self._copy(i, m, x_ref.at[idx] if sc else x_ref, o_ref.at[idx], dev, send_sems, recv_sems)
                if sc or m not in SAME_CORE:
                    cp.wait_recv()
                cp.wait_send()
            pltpu.make_async_copy(x_ref.at[me] if sc else x_ref, o_ref.at[me], local_sems.at[i]).wait()


def _exchange(ex, name):
    n = ex.n

    def body(*refs):
        x_refs, o_refs, sems = refs[:n], refs[n:2 * n], refs[2 * n:]
        ex.start(x_refs, o_refs, *sems)
        ex.middle(x_refs, o_refs, *sems)
        ex.wait(x_refs, o_refs, *sems)

    hbm = pl.BlockSpec(memory_space=pltpu.HBM)
    return pl.pallas_call(body, name=name, out_shape=ex.out_shapes(), in_specs=[hbm] * n, out_specs=(hbm,) * n,
                          scratch_shapes=ex.sems())(*ex.xs)


def _call(body, *, name, out_shape, grid, in_specs, out_specs, scratch_shapes, args, semantics, ex=None):
    if ex is None:
        return pl.pallas_call(body, name=name, out_shape=out_shape, grid=grid, in_specs=in_specs, out_specs=out_specs,
                              scratch_shapes=scratch_shapes, compiler_params=_params(semantics))(*args), ()
    n, ni, no, ns = ex.n, len(args), len(out_shape), len(scratch_shapes)

    def wrapped(*refs):
        ins, cx = refs[:ni], refs[ni:ni + n]
        outs, co = refs[ni + n:ni + n + no], refs[ni + n + no:ni + 2 * n + no]
        scratch, sems = refs[ni + 2 * n + no:ni + 2 * n + no + ns], refs[ni + 2 * n + no + ns:]
        step = functools.reduce(lambda acc, a: acc * grid[a] + pl.program_id(a), range(len(grid)), 0)
        steps = math.prod(grid)

        @pl.when(step == 0)
        def _():
            ex.start(cx, co, *sems)

        body(*ins, *outs, *scratch)

        @pl.when(step == (steps * 3) // 4 - (steps > 1))
        def _():
            ex.middle(cx, co, *sems)

        @pl.when(step == steps - 1)
        def _():
            ex.wait(cx, co, *sems)

    hbm = pl.BlockSpec(memory_space=pltpu.HBM)
    res = pl.pallas_call(
        wrapped, name=name, out_shape=tuple(out_shape) + ex.out_shapes(), grid=grid,
        in_specs=list(in_specs) + [hbm] * n, out_specs=tuple(out_specs) + (hbm,) * n,
        scratch_shapes=list(scratch_shapes) + ex.sems(),
        compiler_params=_params(("arbitrary",) * len(grid)))(*args, *ex.xs)
    return res[:no], res[no:]


def _pack_flat(arrs):
    cat = jnp.concatenate([a.reshape(-1) for a in arrs])
    per = PACK_COLS * 2 * SUBLANES
    tot = -(-cat.shape[0] // per) * per
    return jnp.pad(cat, (0, tot - cat.shape[0])).reshape(tot // PACK_COLS, PACK_COLS)


def _unpack_flat(packed, shapes):
    flat = packed.reshape(-1)
    out, o = [], 0
    for s in shapes:
        n = math.prod(s)
        out.append(flat[o:o + n].reshape(s))
        o += n
    return out


def _ffn_fwd(x, xb, wt, wo4, g, b, alpha, tag, ex=None):
    h8, a4, extra = _ffn_h(xb, wt, tag + "_h", ex)
    out, outb, xh, rstd = _ffn_y_ln(a4, wo4, x, g, b, alpha, 0.5, tag + "_y_ln")
    return out, outb, (xb, h8, a4, xh, rstd), extra


def _ffn_bwd(dout, saved, wt, wo4, g, alpha, tag, carry=None, last=None):
    x, h8, a4, xh, rstd = saved
    load = dict(carry or {})
    slots = [[], [], [], []]
    for n, k in enumerate(sorted(load, key=lambda k: -load[k].size)):
        slots[min(n, 3)].append(k)
    arrived = {}

    def run(fn, slot, *args):
        keys = slots[slot]
        if not keys:
            return fn(*args)
        res, extra = fn(*args, ex=Exchange([load[k] for k in keys], [k != "small" for k in keys]))
        arrived.update(zip(keys, extra))
        return res

    dz, dg, db = _ln_bwd(dout, xh, rstd, g, tag + "_ln_bwd")
    if last is not None:
        (key_out, key_in), load["small"] = last(dg, db)
        slots[3].append("small")
    dw_out = run(_ffn_dwout, 3, a4, dz, 0.5, tag + "_dwout")
    dw_out = dw_out.reshape(N_DEV, -1, dw_out.shape[-1])
    if last is not None:
        load[key_out] = dw_out
        slots[2].append(key_out)
    dh8 = run(_ffn_dh, 2, dz, wo4, h8, 0.5, tag + "_dh")
    dw_in = run(_ffn_dwin, 0, x, dh8, tag + "_dwin")
    if last is not None:
        load[key_in] = dw_in
        slots[1].append(key_in)
    dx = run(_ffn_dx, 1, dh8, wt, dz, alpha, tag + "_dx")
    return dx, dw_in, dw_out, dg, db, arrived


def _fox_fwd(x, xb, w_in_pad, b_f_pad, w_o, g, b, alpha, tag, ex=None):
    S, D = x.shape
    H = D // HEAD_DIM
    proj = _mm(xb, w_in_pad, name=tag + "_proj")
    fl = proj[:, 3 * D:]
    cum = _fox_gate_fwd(fl, b_f_pad)
    cum2 = (cum[:, :H].T * LOG2E).reshape(H // 2, 2, S).transpose(0, 2, 1)
    o, lse, extra = _flash_fwd(proj, cum2, ex)
    m = _mm(o, w_o, name=tag + "_out")
    out, outb, xh, rstd = _res_ln_fwd(x, m, g, b, alpha, 1.0, tag + "_ln")
    return out, outb, (xb, proj, jnp.concatenate([cum2, lse], axis=2), o, fl, xh, rstd), extra


def _fox_bwd(dout, saved, w_in_pad, b_f_pad, w_o, g, alpha, tag, ex=None):
    x, proj, rows, o, fl, xh, rstd = saved
    S, D = x.shape
    H = D // HEAD_DIM
    dz, dg, db = _ln_bwd(dout, xh, rstd, g, tag + "_ln_bwd")
    dw_o = _mm(o, dz, ta=True, tk=2048, name=tag + "_dwo")
    do = _mm(dz, w_o, tb=True, name=tag + "_do")
    dq, dk, dv, dcq, dck, extra = _flash_bwd(proj, rows, o, do, ex)
    dcq = dcq.transpose(0, 2, 1).reshape(H, S)
    dcum = jnp.pad((dcq + dck.reshape(H, S)).T, ((0, 0), (0, LANES - H)))
    dfl, dbf = _fox_gate_bwd(dcum, fl, b_f_pad)
    dproj = jnp.concatenate([dq, dk, dv, dfl.astype(BF16)], axis=1)
    dw_in = _mm(x, dproj, ta=True, tk=x.shape[0], name=tag + "_dwin")
    dx = _mm(dproj, w_in_pad, tb=True, add=dz, add_scale=alpha, tk=dproj.shape[1], name=tag + "_dx")
    shards = {"fox_w_in": _split(dw_in[None, :, :3 * D + H], True)[:, 0].astype(BF16),
              "fox_w_o": _split(dw_o[None], False)[:, 0].astype(BF16)}
    return dx, shards, {"fox_b_f": dbf[0, :H], "lnm_g": dg, "lnm_b": db}, extra


def _s5_mats(p):
    lb_re, lb_im, bb_re, bb_im = _s5_discretise(p["a_re"], p["a_im"], p["log_dt"], p["b_re"], p["b_im"])
    rmat = jnp.concatenate([_s5_expand(bb_re.transpose(0, 2, 1)), _s5_expand(bb_im.transpose(0, 2, 1))], axis=2)
    cmat = jnp.concatenate([_s5_expand(p["c_re"]).transpose(0, 2, 1), -_s5_expand(p["c_im"]).transpose(0, 2, 1)],
                           axis=1)
    lam = jnp.concatenate([_s5_slab(lb_re), _s5_slab(lb_im)], axis=2)
    return rmat.astype(BF16), cmat.astype(BF16), lam


def _s5_block_fwd(x, p, w_out, g, b, alpha, tag, ex=None):
    S, D = x.shape
    rmat, cmat, lam = _s5_mats(p)
    dskip = p["d"].reshape(1, D)
    y, yg, hs, extra = _s5_fwd(x, rmat, cmat, lam, dskip, ex)
    vg = _mm(yg, w_out, name=tag + "_vg")
    m = _glu_fwd(vg, tag + "_glu")
    out, outb, xh, rstd = _res_ln_fwd(x, m, g, b, alpha, 1.0, tag + "_ln")
    return out, outb, (x, y, yg, hs, vg, rmat, cmat, lam, dskip, xh, rstd), extra


def _s5_block_bwd(dout, saved, p, w_out, g, alpha, tag, ex=None):
    x, y, yg, hs, vg, rmat, cmat, lam, dskip, xh, rstd = saved
    S, D = x.shape
    G = D // S5_GROUP
    dz, dg, db = _ln_bwd(dout, xh, rstd, g, tag + "_ln_bwd")
    dvg = _glu_bwd(vg, dz, tag + "_glu_bwd")
    dw_out = _mm(yg, dvg, ta=True, tk=yg.shape[0], name=tag + "_dwout")
    dyg = _mm(dvg, w_out, tb=True, tk=dvg.shape[1], name=tag + "_dyg")
    dx, dr, dc, dlam, dd, extra = _s5_bwd(x, y, dyg, hs, rmat, cmat, lam, dskip, dz, alpha, ex)
    dbb_re = _s5_extract(dr[:, :, :LANES]).transpose(0, 2, 1)
    dbb_im = _s5_extract(dr[:, :, LANES:]).transpose(0, 2, 1)
    dc_re = _s5_extract(dc[:, :LANES, :].transpose(0, 2, 1))
    dc_im = -_s5_extract(dc[:, LANES:, :].transpose(0, 2, 1))
    dlb_re = dlam[:, :, :LANES].reshape(G, S5_STATE)
    dlb_im = dlam[:, :, LANES:].reshape(G, S5_STATE)
    _, vjp = jax.vjp(_s5_discretise, p["a_re"], p["a_im"], p["log_dt"], p["b_re"], p["b_im"])
    da_re, da_im, dlog_dt, db_re, db_im = vjp((dlb_re, dlb_im, dbb_re, dbb_im))
    small = dict(s5_a_re=da_re, s5_a_im=da_im, s5_log_dt=dlog_dt, s5_b_re=db_re, s5_b_im=db_im, s5_c_re=dc_re,
                 s5_c_im=dc_im, s5_d=dd.reshape(G, S5_GROUP), lnm_g=dg, lnm_b=db)
    return dx, {"s5_w_out": _split(dw_out[None], True)[:, 0].astype(BF16)}, small, extra


FFN_NAMES = ("ffn1_w_in", "ffn1_w_out", "ffn2_w_in", "ffn2_w_out")
BIG = FFN_NAMES + ("fox_w_in", "fox_w_o", "s5_w_out")
BIG_SPLIT_COLS = {"ffn1_w_in": True, "ffn1_w_out": False, "ffn2_w_in": True, "ffn2_w_out": False,
                  "fox_w_in": True, "fox_w_o": False, "s5_w_out": True}
SMALL = ("ln1_g", "ln1_b", "lnm_g", "lnm_b", "ln2_g", "ln2_b", "fox_b_f", "s5_a_re", "s5_a_im", "s5_log_dt",
         "s5_b_re", "s5_b_im", "s5_c_re", "s5_c_im", "s5_d")
WEIGHTS = ("ffn1_w_in", "ffn1_w_out", "ln1_g", "ln1_b", "lnm_g", "lnm_b", "ffn2_w_in", "ffn2_w_out", "ln2_g", "ln2_b",
           "fox_w_in", "fox_b_f", "fox_w_o", "s5_a_re", "s5_a_im", "s5_log_dt", "s5_b_re", "s5_b_im", "s5_c_re",
           "s5_c_im", "s5_d", "s5_w_out")


def _join(gathered, split_cols):
    n, L, r, c = gathered.shape
    if split_cols:
        return gathered.transpose(1, 2, 0, 3).reshape(L, r, n * c)
    return gathered.transpose(1, 0, 2, 3).reshape(L, n * r, c)


def _split(full, split_cols):
    L, R, C = full.shape
    if split_cols:
        return full.reshape(L, R, N_DEV, C // N_DEV).transpose(2, 0, 1, 3)
    return full.reshape(L, N_DEV, R // N_DEV, C).transpose(1, 0, 2, 3)


def _group(i, part):
    if part == "mixer":
        return (("fox_w_in", i // 2), ("fox_w_o", i // 2)) if i % 2 == 0 else (("s5_w_out", i // 2),)
    return ((part + "_w_in", i), (part + "_w_out", i))


def _prepare(name, g8):
    if name in ("ffn1_w_in", "ffn2_w_in"):
        return g8
    if name in FFN_NAMES:
        n, r, c = g8.shape
        return g8.reshape(n // 2, 2 * r, c)
    full = _join(g8[:, None], BIG_SPLIT_COLS[name])[0]
    if name == "fox_w_in":
        full = jnp.pad(full, ((0, 0), (0, LANES - full.shape[0] // HEAD_DIM)))
    return full


def _local_step(x, target, small, shard_of=None, pregathered=None):
    S, D = x.shape
    H = D // HEAD_DIM
    depth = small["ln1_g"].shape[0]
    alpha = (2.0 * depth) ** 0.25
    local = pregathered is not None
    bf_pad = jnp.pad(small["fox_b_f"], ((0, 0), (0, LANES - H)))

    def s5_params(j):
        return {k: small["s5_" + k][j] for k in ("a_re", "a_im", "log_dt", "b_re", "b_im", "c_re", "c_im", "d")}

    if local:
        W = {k: _prepare(k[0], g8) for k, g8 in pregathered.items()}
    else:
        keys = _group(0, "ffn1")
        got = _exchange(Exchange([shard_of(*k) for k in keys], [False] * len(keys)), "gather_first")
        W = {k: _prepare(k[0], g8) for k, g8 in zip(keys, got)}

    def gather(keys):
        return None if local else Exchange([shard_of(*k) for k in keys], [False] * len(keys))

    def landed(keys, extra):
        if not local:
            W.update({k: _prepare(k[0], g8) for k, g8 in zip(keys, extra)})

    saved = []
    h, hb = x, x.astype(BF16)
    for i in range(depth):
        j = i // 2
        keys = _group(i, "mixer")
        h, hb, s1, extra = _ffn_fwd(h, hb, W[("ffn1_w_in", i)], W[("ffn1_w_out", i)], small["ln1_g"][i],
                                    small["ln1_b"][i], alpha, f"l{i}_ffn1", gather(keys))
        landed(keys, extra)
        keys = _group(i, "ffn2") + (_group(i + 1, "ffn1") if i + 1 < depth else ())
        ex = gather(keys)
        if i % 2 == 0:
            h, hb, s2, extra = _fox_fwd(h, hb, W[("fox_w_in", j)], bf_pad[j:j + 1], W[("fox_w_o", j)],
                                        small["lnm_g"][i], small["lnm_b"][i], alpha, f"l{i}_fox", ex)
        else:
            h, hb, s2, extra = _s5_block_fwd(h, s5_params(j), W[("s5_w_out", j)], small["lnm_g"][i],
                                             small["lnm_b"][i], alpha, f"l{i}_s5", ex)
        landed(keys, extra)
        h, hb, s3, _ = _ffn_fwd(h, hb, W[("ffn2_w_in", i)], W[("ffn2_w_out", i)], small["ln2_g"][i],
                                small["ln2_b"][i], alpha, f"l{i}_ffn2")
        saved.append((s1, s2, s3))

    dh, loss_part = _loss_fwd_bwd(h, target)

    arrived = {}
    pending = {}
    gs = {k: [None] * small[k].shape[0] for k in SMALL}
    for i in reversed(range(depth)):
        j = i // 2
        s1, s2, s3 = saved[i]
        held = {k: pending.pop(k) for k in [("ffn1_w_in", i + 1)] if i % 2 == 0 and k in pending}
        dh, dw_in, dw_out, gs["ln2_g"][i], gs["ln2_b"][i], got = _ffn_bwd(
            dh, s3, W[("ffn2_w_in", i)], W[("ffn2_w_out", i)], small["ln2_g"][i], alpha, f"l{i}_ffn2",
            None if local else pending)
        arrived.update(pending if local else got)
        pending = {("ffn2_w_in", i): dw_in, ("ffn2_w_out", i): dw_out, **held}
        keys = list(pending)
        ex = None if local else Exchange([pending[k] for k in keys], [True] * len(keys))
        if i % 2 == 0:
            dh, mix, sg, extra = _fox_bwd(dh, s2, W[("fox_w_in", j)], bf_pad[j:j + 1], W[("fox_w_o", j)],
                                          small["lnm_g"][i], alpha, f"l{i}_fox", ex)
        else:
            dh, mix, sg, extra = _s5_block_bwd(dh, s2, s5_params(j), W[("s5_w_out", j)], small["lnm_g"][i], alpha,
                                               f"l{i}_s5", ex)
        arrived.update(zip(keys, [pending[k] for k in keys] if local else extra))
        pending = {(k, j): val for k, val in mix.items()}
        for k, val in sg.items():
            gs[k][i if k in ("lnm_g", "lnm_b") else j] = val
        def last(dg, db):
            gs["ln1_g"][0], gs["ln1_b"][0] = dg, db
            packed = _pack_flat([jnp.stack(gs[k]) for k in SMALL]).astype(BF16)
            return (("ffn1_w_out", 0), ("ffn1_w_in", 0)), packed

        dh, dw_in, dw_out, gs["ln1_g"][i], gs["ln1_b"][i], got = _ffn_bwd(
            dh, s1, W[("ffn1_w_in", i)], W[("ffn1_w_out", i)], small["ln1_g"][i], alpha, f"l{i}_ffn1",
            None if local else pending, last if (i == 0 and not local) else None)
        arrived.update(pending if local else got)
        pending = {("ffn1_w_in", i): dw_in, ("ffn1_w_out", i): dw_out}
    if local:
        arrived.update(pending)
    return loss_part, dh, arrived, {k: jnp.stack(v) for k, v in gs.items()}


def kernel(x, ffn1_w_in, ffn1_w_out, ln1_g, ln1_b, lnm_g, lnm_b, ffn2_w_in, ffn2_w_out, ln2_g, ln2_b, fox_w_in, fox_b_f, fox_w_o, s5_a_re, s5_a_im, s5_log_dt, s5_b_re, s5_b_im, s5_c_re, s5_c_im, s5_d, s5_w_out, loss_target, m_ffn1_w_in, m_ffn1_w_out, m_ln1_g, m_ln1_b, m_lnm_g, m_lnm_b, m_ffn2_w_in, m_ffn2_w_out, m_ln2_g, m_ln2_b, m_fox_w_in, m_fox_b_f, m_fox_w_o, m_s5_a_re, m_s5_a_im, m_s5_log_dt, m_s5_b_re, m_s5_b_im, m_s5_c_re, m_s5_c_im, m_s5_d, m_s5_w_out, v_ffn1_w_in, v_ffn1_w_out, v_ln1_g, v_ln1_b, v_lnm_g, v_lnm_b, v_ffn2_w_in, v_ffn2_w_out, v_ln2_g, v_ln2_b, v_fox_w_in, v_fox_b_f, v_fox_w_o, v_s5_a_re, v_s5_a_im, v_s5_log_dt, v_s5_b_re, v_s5_b_im, v_s5_c_re, v_s5_c_im, v_s5_d, v_s5_w_out):
    args = dict(locals())
    w = {k: args[k] for k in WEIGHTS}
    m = {k: args["m_" + k] for k in WEIGHTS}
    v = {k: args["v_" + k] for k in WEIGHTS}
    small = {k: w[k] for k in SMALL}

    turned = lambda d: {k: jnp.swapaxes(a, 1, 2) if k in ("ffn1_w_in", "ffn2_w_in") else a for k, a in d.items()}
    w, m, v = turned(w), turned(m), turned(v)
    wb = {k: w[k].astype(BF16) for k in BIG}
    loss_part, dx, arrived, _ = _local_step(x[0], loss_target[0], small, lambda k, idx: wb[k][idx])
    loss = lax.psum(loss_part, ("x", "y", "c"))
    small_shapes = [w[k].shape for k in SMALL]
    g_small_flat = _sum8(arrived["small"], "sum_small_grads")

    grads, delta, new_m, new_v = {}, {}, {}, {}
    for k in BIG:
        parts = [arrived[(k, l)] for l in range(w[k].shape[0])]
        grads[k], delta[k], new_m[k], new_v[k] = _adamw_recv(parts, w[k], m[k], v[k], "adamw_" + k)
    pk = lambda d: _pack_flat([d[k] for k in SMALL])
    d_, m_, v_ = _adamw(pk(w), g_small_flat, pk(m), pk(v), "adamw_small")
    for dst, flat in ((grads, g_small_flat), (delta, d_), (new_m, m_), (new_v, v_)):
        dst.update(zip(SMALL, _unpack_flat(flat, small_shapes)))
    grads, delta, new_m, new_v = turned(grads), turned(delta), turned(new_m), turned(new_v)

    return (loss, dx[None], *[grads[k] for k in WEIGHTS], *[delta[k] for k in WEIGHTS],
            *[new_m[k] for k in WEIGHTS], *[new_v[k] for k in WEIGHTS])
```

```python
import functools
import math

import jax
import jax.numpy as jnp
from jax import lax
from jax.experimental import pallas as pl
from jax.experimental.pallas import tpu as pltpu

F32 = jnp.float32
BF16 = jnp.bfloat16

N_DEV = 8
HEAD_DIM = 64
S5_GROUP = 16
S5_STATE = 64
LANES = 128
SUBLANES = 8
S5_PART = 256
ATTN_TILE = 512
LN_EPS = 1e-5
NEG_INF = -1e30
LOG2E = 1.4426950408889634
ADAM_LR, ADAM_B1, ADAM_B2, ADAM_EPS, ADAM_WD, ADAM_STEP = 0.001, 0.9, 0.999, 1e-08, 0.01, 10
VMEM_LIMIT = 48 * 1024 * 1024
PACK_COLS = 1024

MESH = pl.DeviceIdType.MESH


def _tile(dim, pref, align=LANES):
    if dim <= pref:
        return dim
    t = (pref // align) * align
    while t >= align:
        if dim % t == 0:
            return t
        t -= align
    return dim


def _params(sem):
    return pltpu.CompilerParams(dimension_semantics=sem, vmem_limit_bytes=VMEM_LIMIT)


def _mm(a, b, *, ta=False, tb=False, out_dtype=F32, scale=None, add=None, add_scale=1.0,
        tm=512, tn=1408, tk=1408, name="mm"):
    if ta:
        K, M = a.shape
    else:
        M, K = a.shape
    if tb:
        N, K2 = b.shape
    else:
        K2, N = b.shape
    assert K == K2, (a.shape, b.shape, ta, tb)
    tm, tn, tk = _tile(M, tm), _tile(N, tn), _tile(K, tk)
    a_spec = pl.BlockSpec((tk, tm), lambda i, j, k: (k, i)) if ta else pl.BlockSpec((tm, tk), lambda i, j, k: (i, k))
    b_spec = pl.BlockSpec((tn, tk), lambda i, j, k: (j, k)) if tb else pl.BlockSpec((tk, tn), lambda i, j, k: (k, j))
    o_spec = pl.BlockSpec((tm, tn), lambda i, j, k: (i, j))
    return _mm_core(name, a, b, a_spec, b_spec, o_spec, jax.ShapeDtypeStruct((M, N), out_dtype),
                    (M // tm, N // tn, K // tk), (tm, tn), ta, tb, scale, add, add_scale)


def _mm_core(name, a, b, a_spec, b_spec, o_spec, out_shape, grid, acc_shape, ta, tb, scale=None, add=None,
             add_scale=1.0, ex=None):
    nk = grid[2]

    def body(*refs):
        if add is None:
            a_ref, b_ref, o_ref, acc = refs
            add_ref = None
        else:
            a_ref, b_ref, add_ref, o_ref, acc = refs
        k = pl.program_id(2)

        @pl.when(k == 0)
        def _():
            acc[...] = jnp.zeros_like(acc)

        dims = (((0 if ta else 1,), (1 if tb else 0,)), ((), ()))
        if len(a_ref.shape) == 3:
            part = sum(lax.dot_general(a_ref[u].astype(BF16), b_ref[u].astype(BF16), dims, preferred_element_type=F32)
                       for u in range(a_ref.shape[0]))
        else:
            part = lax.dot_general(a_ref[...].astype(BF16), b_ref[...].astype(BF16), dims, preferred_element_type=F32)
        acc[...] += part

        @pl.when(k == nk - 1)
        def _():
            r = acc[...]
            if scale is not None:
                r = r * scale
            if add_ref is not None:
                r = r + add_scale * add_ref[...]
            o_ref[...] = r.astype(out_shape.dtype)

    in_specs = [a_spec, b_spec]
    args = [a, b]
    if add is not None:
        in_specs.append(o_spec)
        args.append(add)
    (res,), extra = _call(body, name=name, out_shape=(out_shape,), grid=grid, in_specs=in_specs, out_specs=(o_spec,),
                          scratch_shapes=[pltpu.VMEM(acc_shape, F32)], args=args,
                          semantics=("parallel", "parallel", "arbitrary"), ex=ex)
    return res if ex is None else (res, extra)


def _ffn_h(x, wt, name, ex=None):
    S, D = x.shape
    n, c, _ = wt.shape
    nb = n // 2
    tm = _tile(S, 1024, SUBLANES)
    dims = (((1,), (1,)), ((), ()))

    def body(x_ref, wg_ref, wu_ref, h_ref, a_ref):
        xb = x_ref[...]
        g = lax.dot_general(xb, wg_ref[...], dims, preferred_element_type=F32).astype(BF16)
        u = lax.dot_general(xb, wu_ref[...], dims, preferred_element_type=F32).astype(BF16)
        h_ref[0] = g
        h_ref[1] = u
        g = g.astype(F32)
        a_ref[...] = (g * _sigmoid(g) * u.astype(F32)).astype(BF16)

    (h, a4), extra = _call(
        body, name=name,
        out_shape=(jax.ShapeDtypeStruct((2, nb, S, c), BF16), jax.ShapeDtypeStruct((nb, S, c), BF16)),
        grid=(S // tm, nb),
        in_specs=[pl.BlockSpec((tm, D), lambda i, j: (i, 0)), pl.BlockSpec((None, c, D), lambda i, j: (j, 0, 0)),
                  pl.BlockSpec((None, c, D), lambda i, j: (j + nb, 0, 0))],
        out_specs=(pl.BlockSpec((2, None, tm, c), lambda i, j: (0, j, i, 0)),
                   pl.BlockSpec((None, tm, c), lambda i, j: (j, i, 0))),
        scratch_shapes=[], args=(x, wt, wt), semantics=("parallel", "parallel"), ex=ex)
    return h.reshape(n, S, c), a4, extra


def _ffn_y_ln(a4, wo4, x, g, b, alpha, s, name):
    nb, S, c = a4.shape
    D = wo4.shape[-1]
    tm = _tile(S, 512, SUBLANES)

    def body(a_ref, w_ref, x_ref, g_ref, b_ref, o_ref, ob_ref, xh_ref, r_ref):
        y = jnp.dot(a_ref[0], w_ref[0], preferred_element_type=F32)
        for k in range(1, nb):
            y = y + jnp.dot(a_ref[k], w_ref[k], preferred_element_type=F32)
        z = alpha * x_ref[...] + s * y
        mu = jnp.mean(z, axis=-1, keepdims=True)
        zc = z - mu
        rstd = lax.rsqrt(jnp.mean(zc * zc, axis=-1, keepdims=True) + LN_EPS)
        xh = zc * rstd
        xh_ref[...] = xh
        r_ref[...] = rstd
        out = xh * g_ref[...] + b_ref[...]
        o_ref[...] = out
        ob_ref[...] = out.astype(BF16)

    row = pl.BlockSpec((tm, D), lambda i: (i, 0))
    vec = pl.BlockSpec((1, D), lambda i: (0, 0))
    return pl.pallas_call(
        body, name=name,
        out_shape=(jax.ShapeDtypeStruct((S, D), F32), jax.ShapeDtypeStruct((S, D), BF16),
                   jax.ShapeDtypeStruct((S, D), F32), jax.ShapeDtypeStruct((S, 1), F32)),
        grid=(S // tm,),
        in_specs=[pl.BlockSpec((nb, tm, c), lambda i: (0, i, 0)), pl.BlockSpec((nb, c, D), lambda i: (0, 0, 0)),
                  row, vec, vec],
        out_specs=(row, row, row, pl.BlockSpec((tm, 1), lambda i: (i, 0))),
        compiler_params=_params(("parallel",)),
    )(a4, wo4, x, g.reshape(1, D), b.reshape(1, D))


def _ffn_dwout(a4, dz, scale, name, ex=None):
    nb, S, c = a4.shape
    D = dz.shape[1]
    tk, tn = _tile(S, 2048, SUBLANES), _tile(D, 1024)
    return _mm_core(name, a4, dz, pl.BlockSpec((None, tk, c), lambda i, j, k: (i, k, 0)),
                    pl.BlockSpec((tk, tn), lambda i, j, k: (k, j)),
                    pl.BlockSpec((None, c, tn), lambda i, j, k: (i, 0, j)),
                    jax.ShapeDtypeStruct((nb, c, D), BF16), (nb, D // tn, S // tk), (c, tn), True, False, scale,
                    ex=ex)


def _ffn_dh(dz, wo4, h8, scale, name, ex=None):
    S, D = dz.shape
    nb, c = wo4.shape[0], wo4.shape[1]
    tm = _tile(S, 512, SUBLANES)

    def body(dz_ref, w_ref, h_ref, d_ref):
        da = lax.dot_general(dz_ref[...].astype(BF16), w_ref[...], (((1,), (1,)), ((), ())),
                             preferred_element_type=F32) * scale
        g = h_ref[0].astype(F32)
        u = h_ref[1].astype(F32)
        sg = _sigmoid(g)
        silu = g * sg
        d_ref[0] = (da * u * (sg + silu * (1.0 - sg))).astype(BF16)
        d_ref[1] = (da * silu).astype(BF16)

    pair = pl.BlockSpec((2, None, tm, c), lambda i, j: (0, j, i, 0))
    (dh,), extra = _call(
        body, name=name, out_shape=(jax.ShapeDtypeStruct((2, nb, S, c), BF16),), grid=(S // tm, nb),
        in_specs=[pl.BlockSpec((tm, D), lambda i, j: (i, 0)), pl.BlockSpec((None, c, D), lambda i, j: (j, 0, 0)),
                  pair],
        out_specs=(pair,), scratch_shapes=[], args=(dz, wo4, h8.reshape(2, nb, S, c)),
        semantics=("parallel", "parallel"), ex=ex)
    dh = dh.reshape(2 * nb, S, c)
    return dh if ex is None else (dh, extra)


def _ffn_dwin(x, dh8, name, ex=None):
    S, D = x.shape
    n, _, c = dh8.shape
    return _mm_core(name, dh8, x, pl.BlockSpec((None, S, c), lambda i, j, k: (i, 0, 0)),
                    pl.BlockSpec((S, D), lambda i, j, k: (0, 0)),
                    pl.BlockSpec((None, c, D), lambda i, j, k: (i, 0, 0)),
                    jax.ShapeDtypeStruct((n, c, D), BF16), (n, 1, 1), (c, D), True, False, ex=ex)


def _ffn_dx(dh8, wt, dz, alpha, name, ex=None):
    n, S, c = dh8.shape
    D = wt.shape[2]
    tm, tn = _tile(S, 512, SUBLANES), _tile(D, 1024)
    return _mm_core(name, dh8, wt, pl.BlockSpec((4, tm, c), lambda i, j, k: (k, i, 0)),
                    pl.BlockSpec((4, c, tn), lambda i, j, k: (k, 0, j)),
                    pl.BlockSpec((tm, tn), lambda i, j, k: (i, j)),
                    jax.ShapeDtypeStruct((S, D), F32), (S // tm, D // tn, n // 4), (tm, tn), False, False,
                    None, dz, alpha, ex=ex)


def _res_ln_fwd(x, y, g, b, alpha, s, name):
    S, D = x.shape
    tm = _tile(S, 256, SUBLANES)

    def body(x_ref, y_ref, g_ref, b_ref, o_ref, ob_ref, xh_ref, r_ref):
        z = alpha * x_ref[...] + s * y_ref[...]
        mu = jnp.mean(z, axis=-1, keepdims=True)
        zc = z - mu
        var = jnp.mean(zc * zc, axis=-1, keepdims=True)
        rstd = lax.rsqrt(var + LN_EPS)
        xh = zc * rstd
        xh_ref[...] = xh
        r_ref[...] = rstd
        out = xh * g_ref[...] + b_ref[...]
        o_ref[...] = out
        ob_ref[...] = out.astype(BF16)

    row = pl.BlockSpec((tm, D), lambda i: (i, 0))
    vec = pl.BlockSpec((1, D), lambda i: (0, 0))
    return pl.pallas_call(
        body, name=name,
        out_shape=(jax.ShapeDtypeStruct((S, D), F32), jax.ShapeDtypeStruct((S, D), BF16),
                   jax.ShapeDtypeStruct((S, D), F32), jax.ShapeDtypeStruct((S, 1), F32)),
        grid=(S // tm,), in_specs=[row, row, vec, vec],
        out_specs=(row, row, row, pl.BlockSpec((tm, 1), lambda i: (i, 0))),
        compiler_params=_params(("parallel",)),
    )(x, y, g.reshape(1, D), b.reshape(1, D))


def _ln_bwd(dout, xh, rstd, g, name):
    S, D = dout.shape
    tm = _tile(S, 256, SUBLANES)

    def body(d_ref, xh_ref, r_ref, g_ref, dz_ref, dg_ref, db_ref):
        i = pl.program_id(0)

        @pl.when(i == 0)
        def _():
            dg_ref[...] = jnp.zeros_like(dg_ref)
            db_ref[...] = jnp.zeros_like(db_ref)

        d = d_ref[...]
        xhv = xh_ref[...]
        dxh = d * g_ref[...]
        m1 = jnp.mean(dxh, axis=-1, keepdims=True)
        m2 = jnp.mean(dxh * xhv, axis=-1, keepdims=True)
        dz_ref[...] = r_ref[...] * (dxh - m1 - xhv * m2)
        dg_ref[...] += jnp.sum(d * xhv, axis=0, keepdims=True)
        db_ref[...] += jnp.sum(d, axis=0, keepdims=True)

    row = pl.BlockSpec((tm, D), lambda i: (i, 0))
    vec = pl.BlockSpec((1, D), lambda i: (0, 0))
    dz, dg, db = pl.pallas_call(
        body, name=name,
        out_shape=(jax.ShapeDtypeStruct((S, D), F32), jax.ShapeDtypeStruct((1, D), F32),
                   jax.ShapeDtypeStruct((1, D), F32)),
        grid=(S // tm,), in_specs=[row, row, pl.BlockSpec((tm, 1), lambda i: (i, 0)), vec],
        out_specs=(row, vec, vec),
        compiler_params=_params(("arbitrary",)),
    )(dout, xh, rstd, g.reshape(1, D))
    return dz, dg[0], db[0]


def _sigmoid(x):
    return 0.5 * jnp.tanh(0.5 * x) + 0.5


def _glu_fwd(vg, name):
    S, D2 = vg.shape
    D = D2 // 2
    tm = _tile(S, 512, SUBLANES)

    def body(v_ref, g_ref, o_ref):
        o_ref[...] = v_ref[...] * _sigmoid(g_ref[...])

    return pl.pallas_call(
        body, name=name, out_shape=jax.ShapeDtypeStruct((S, D), F32), grid=(S // tm,),
        in_specs=[pl.BlockSpec((tm, D), lambda i: (i, 0)), pl.BlockSpec((tm, D), lambda i: (i, 1))],
        out_specs=pl.BlockSpec((tm, D), lambda i: (i, 0)),
        compiler_params=_params(("parallel",)),
    )(vg, vg)


def _glu_bwd(vg, dm, name):
    S, D2 = vg.shape
    D = D2 // 2
    tm = _tile(S, 512, SUBLANES)

    def body(v_ref, g_ref, dm_ref, dv_ref, dg_ref):
        sg = _sigmoid(g_ref[...])
        d = dm_ref[...]
        dv_ref[...] = (d * sg).astype(BF16)
        dg_ref[...] = (d * v_ref[...] * sg * (1.0 - sg)).astype(BF16)

    blk = pl.BlockSpec((tm, D), lambda i: (i, 0))
    dv, dg = pl.pallas_call(
        body, name=name,
        out_shape=(jax.ShapeDtypeStruct((S, D), BF16), jax.ShapeDtypeStruct((S, D), BF16)),
        grid=(S // tm,), in_specs=[blk, pl.BlockSpec((tm, D), lambda i: (i, 1)), blk],
        out_specs=(blk, blk), compiler_params=_params(("parallel",)),
    )(vg, vg, dm)
    return jnp.concatenate([dv, dg], axis=1)


def _loss_fwd_bwd(y, target):
    S, D = y.shape
    tm = _tile(S, 256, SUBLANES)

    def body(y_ref, t_ref, dy_ref, l_ref):
        i = pl.program_id(0)

        @pl.when(i == 0)
        def _():
            l_ref[...] = jnp.zeros_like(l_ref)

        e = y_ref[...] - t_ref[...]
        dy_ref[...] = e * (1.0 / D)
        l_ref[...] += jnp.sum(e * e, axis=0, keepdims=True) * (0.5 / D)

    row = pl.BlockSpec((tm, D), lambda i: (i, 0))
    dy, part = pl.pallas_call(
        body, name="loss", out_shape=(jax.ShapeDtypeStruct((S, D), F32), jax.ShapeDtypeStruct((1, D), F32)),
        grid=(S // tm,), in_specs=[row, row], out_specs=(row, pl.BlockSpec((1, D), lambda i: (0, 0))),
        compiler_params=_params(("arbitrary",)),
    )(y, target)
    return dy, jnp.sum(part)


def _tri(n, lower):
    r = lax.broadcasted_iota(jnp.int32, (n, n), 0)
    c = lax.broadcasted_iota(jnp.int32, (n, n), 1)
    return jnp.where((c <= r) if lower else (c >= r), 1.0, 0.0)


def _fox_gate_fwd(fl, bf):
    S, W = fl.shape
    tm = _tile(S, 256, SUBLANES)

    def body(fl_ref, b_ref, c_ref, carry):
        i = pl.program_id(0)

        @pl.when(i == 0)
        def _():
            carry[...] = jnp.zeros_like(carry)

        x = fl_ref[...] + b_ref[...]
        lf = jnp.minimum(x, 0.0) - jnp.log(1.0 + jnp.exp(-jnp.abs(x)))
        c_ref[...] = jnp.dot(_tri(tm, True), lf, precision=lax.Precision.HIGHEST,
                             preferred_element_type=F32) + carry[...]
        carry[...] += jnp.sum(lf, axis=0, keepdims=True)

    blk = pl.BlockSpec((tm, W), lambda i: (i, 0))
    return pl.pallas_call(
        body, name="fox_gate_fwd", out_shape=jax.ShapeDtypeStruct((S, W), F32), grid=(S // tm,),
        in_specs=[blk, pl.BlockSpec((1, W), lambda i: (0, 0))], out_specs=blk,
        scratch_shapes=[pltpu.VMEM((1, W), F32)], compiler_params=_params(("arbitrary",)),
    )(fl, bf)


def _fox_gate_bwd(dcum, fl, bf):
    S, W = fl.shape
    tm = _tile(S, 256, SUBLANES)
    nb = S // tm

    def body(dc_ref, fl_ref, b_ref, dfl_ref, db_ref, carry):
        i = pl.program_id(0)

        @pl.when(i == 0)
        def _():
            carry[...] = jnp.zeros_like(carry)
            db_ref[...] = jnp.zeros_like(db_ref)

        dc = dc_ref[...]
        r = jnp.dot(_tri(tm, False), dc, precision=lax.Precision.HIGHEST, preferred_element_type=F32) + carry[...]
        carry[...] += jnp.sum(dc, axis=0, keepdims=True)
        x = fl_ref[...] + b_ref[...]
        e = jnp.exp(-jnp.abs(x))
        dfl = r * jnp.where(x >= 0, e, 1.0) / (1.0 + e)
        dfl_ref[...] = dfl
        db_ref[...] += jnp.sum(dfl, axis=0, keepdims=True)

    blk = pl.BlockSpec((tm, W), lambda i: (nb - 1 - i, 0))
    vec = pl.BlockSpec((1, W), lambda i: (0, 0))
    return pl.pallas_call(
        body, name="fox_gate_bwd",
        out_shape=(jax.ShapeDtypeStruct((S, W), F32), jax.ShapeDtypeStruct((1, W), F32)),
        grid=(nb,), in_specs=[blk, blk, vec], out_specs=(blk, vec),
        scratch_shapes=[pltpu.VMEM((1, W), F32)], compiler_params=_params(("arbitrary",)),
    )(dcum, fl, bf)


def _causal(t):
    row = lax.broadcasted_iota(jnp.int32, (t, t), 0)
    col = lax.broadcasted_iota(jnp.int32, (t, t), 1)
    return col <= row


def _first_head(shape):
    return lax.broadcasted_iota(jnp.int32, shape, len(shape) - 1) < HEAD_DIM


def _split3(x):
    hi = x.astype(BF16).astype(F32)
    r = x - hi
    mid = r.astype(BF16).astype(F32)
    return hi, mid, (r - mid).astype(BF16).astype(F32)


def _bias_lanes(c, query):
    lane = lax.broadcasted_iota(jnp.int32, (c.shape[0], LANES), 1)
    hi, mid, lo = _split3(c)
    if query:
        out = jnp.where(lane == 0, hi, jnp.where(lane == 1, mid, jnp.where(lane == 2, lo,
                                                                          jnp.where(lane < 6, 1.0, 0.0))))
    else:
        out = jnp.where(lane < 3, 1.0, jnp.where(lane == 3, -hi, jnp.where(lane == 4, -mid,
                                                                          jnp.where(lane == 5, -lo, 0.0))))
    return out.astype(BF16)


def _flash_fwd(proj, cum2, ex=None):
    S = proj.shape[0]
    D = (proj.shape[1] - LANES) // 3
    HP = D // LANES
    t = _tile(S, ATTN_TILE)
    nq = S // t
    scale = 1.0 / math.sqrt(HEAD_DIM)

    def body(q_ref, k_ref, v_ref, cq_ref, ck_ref, o_ref, lse_ref, kx, vb):
        qi = pl.program_id(1)

        @pl.when(qi == 0)
        def _():
            def prep(kb, c):
                sl = pl.ds(pl.multiple_of(kb * t, t), t)
                k16 = k_ref[sl, :].astype(BF16)
                ckv = ck_ref[0, sl, :]
                for a in range(2):
                    kx[a, sl, 0:LANES] = k16
                    kx[a, sl, LANES:2 * LANES] = _bias_lanes(ckv[:, a:a + 1], False)
                vb[sl, :] = v_ref[sl, :].astype(BF16)
                return c

            lax.fori_loop(0, nq, prep, 0)

        first = _first_head((t, LANES))
        qf = q_ref[...] * (scale * LOG2E)
        cqv = cq_ref[0]
        qx = [jnp.concatenate([jnp.where(keep, qf, 0.0).astype(BF16), _bias_lanes(cqv[:, a:a + 1], True)], axis=1)
              for a, keep in enumerate((first, jnp.logical_not(first)))]

        def block(ki, carry, masked):
            m_old, l_old, acc = carry
            sl = pl.ds(pl.multiple_of(ki * t, t), t)
            vv = vb[sl, :]
            m_new, l_new, corr, pv = [], [], [], []
            for a in range(2):
                s = lax.dot_general(qx[a], kx[a, sl, :], (((1,), (1,)), ((), ())), preferred_element_type=F32)
                if masked:
                    s = jnp.where(_causal(t), s, NEG_INF)
                m_a = jnp.maximum(m_old[a], jnp.max(s, axis=1, keepdims=True))
                p = jnp.exp2(s - m_a)
                c_a = jnp.exp2(m_old[a] - m_a)
                m_new.append(m_a)
                corr.append(c_a)
                l_new.append(c_a * l_old[a] + jnp.sum(p, axis=1, keepdims=True))
                pv.append(jnp.dot(p.astype(BF16), vv, preferred_element_type=F32))
            acc = jnp.where(first, corr[0] * acc + pv[0], corr[1] * acc + pv[1])
            return tuple(m_new), tuple(l_new), acc

        neg = jnp.full((t, 1), NEG_INF, F32)
        zero = jnp.zeros((t, 1), F32)
        carry = lax.fori_loop(0, qi, lambda ki, c: block(ki, c, False),
                              ((neg, neg), (zero, zero), jnp.zeros((t, LANES), F32)))
        m, l, acc = block(qi, carry, True)
        o_ref[...] = acc / jnp.where(first, l[0], l[1])
        lse_ref[0, :, 0:1] = m[0] + jnp.log2(l[0])
        lse_ref[0, :, 1:2] = m[1] + jnp.log2(l[1])

    qblk = pl.BlockSpec((t, LANES), lambda h, i: (i, h))
    r2 = pl.BlockSpec((1, t, 2), lambda h, i: (h, i, 0))
    (o, lse), extra = _call(
        body, name="fox_attn_fwd",
        out_shape=(jax.ShapeDtypeStruct((S, D), F32), jax.ShapeDtypeStruct((HP, S, 2), F32)),
        grid=(HP, nq),
        in_specs=[qblk, pl.BlockSpec((S, LANES), lambda h, i: (0, HP + h)),
                  pl.BlockSpec((S, LANES), lambda h, i: (0, 2 * HP + h)), r2,
                  pl.BlockSpec((1, S, 2), lambda h, i: (h, 0, 0))],
        out_specs=(qblk, r2),
        scratch_shapes=[pltpu.VMEM((2, S, 2 * LANES), BF16), pltpu.VMEM((S, LANES), BF16)],
        args=(proj, proj, proj, cum2, cum2), semantics=("parallel", "arbitrary"), ex=ex)
    return o, lse, extra


def _flash_bwd(proj, rows, o, do, ex=None):
    S = proj.shape[0]
    D = (proj.shape[1] - LANES) // 3
    HP = D // LANES
    t = _tile(S, ATTN_TILE)
    nb = S // t
    scale = 1.0 / math.sqrt(HEAD_DIM)

    def body(q_ref, k_ref, v_ref, rows_ref, o_ref, do_ref, dq_ref, dk_ref, dv_ref, dcq_ref, dck_ref,
             q_s, do_s, dl_s, dq_acc, dk_acc, dv_acc, dc_acc):
        kb = pl.program_id(1)

        @pl.when(kb == 0)
        def _():
            dq_acc[...] = jnp.zeros_like(dq_acc)
            dcq_ref[...] = jnp.zeros_like(dcq_ref)

            def prep(qb, c):
                sl = pl.ds(pl.multiple_of(qb * t, t), t)
                first = _first_head((t, LANES))
                qf = q_ref[sl, :] * (scale * LOG2E)
                dof = do_ref[sl, :]
                prod = dof * o_ref[sl, :]
                rv = rows_ref[0, sl, :]
                for a, keep in enumerate((first, jnp.logical_not(first))):
                    q_s[a, sl, 0:LANES] = jnp.where(keep, qf, 0.0).astype(BF16)
                    q_s[a, sl, LANES:2 * LANES] = _bias_lanes(rv[:, a:a + 1], True)
                    do_s[a, sl, :] = jnp.where(keep, dof, 0.0).astype(BF16)
                    dl_s[sl, a:a + 1] = jnp.sum(jnp.where(keep, prod, 0.0), axis=1, keepdims=True)
                return c

            lax.fori_loop(0, nb, prep, 0)

        dk_acc[...] = jnp.zeros_like(dk_acc)
        dv_acc[...] = jnp.zeros_like(dv_acc)
        dc_acc[...] = jnp.zeros_like(dc_acc)
        first = _first_head((t, LANES))
        kf = k_ref[...]
        kk = kf.astype(BF16)
        k_own = (jnp.where(first, kf, 0.0).astype(BF16), jnp.where(first, 0.0, kf).astype(BF16))
        ckv = rows_ref[0, pl.ds(pl.multiple_of(kb * t, t), t), :]
        kx = [jnp.concatenate([kk, _bias_lanes(ckv[:, a:a + 1], False)], axis=1) for a in range(2)]
        vv = v_ref[...].astype(BF16)

        def block(qb, masked):
            sl = pl.ds(pl.multiple_of(qb * t, t), t)
            rv = rows_ref[0, sl, :]
            dlv = dl_s[sl, :]
            dq_new = dq_acc[sl, :]
            for a in range(2):
                dob = do_s[a, sl, :]
                s = lax.dot_general(q_s[a, sl, :], kx[a], (((1,), (1,)), ((), ())), preferred_element_type=F32)
                if masked:
                    s = jnp.where(_causal(t), s, NEG_INF)
                p = jnp.exp2(s - rv[:, 2 + a:3 + a])
                dv_acc[...] += lax.dot_general(p.astype(BF16), dob, (((0,), (0,)), ((), ())),
                                               preferred_element_type=F32)
                dp = lax.dot_general(dob, vv, (((1,), (1,)), ((), ())), preferred_element_type=F32)
                ds = p * (dp - dlv[:, a:a + 1])
                dsb = ds.astype(BF16)
                dk_acc[...] += lax.dot_general(dsb, q_s[a, sl, 0:LANES], (((0,), (0,)), ((), ())),
                                               preferred_element_type=F32)
                dq_new = dq_new + jnp.dot(dsb, k_own[a], preferred_element_type=F32) * scale
                dcq_ref[0, sl, a:a + 1] += jnp.sum(ds, axis=1, keepdims=True)
                dc_acc[a:a + 1, :] -= jnp.sum(ds, axis=0, keepdims=True)
            dq_acc[sl, :] = dq_new

        block(kb, True)

        def rest(qb, c):
            block(qb, False)
            return c

        lax.fori_loop(kb + 1, nb, rest, 0)
        dk_ref[...] = (dk_acc[...] * (1.0 / LOG2E)).astype(BF16)
        dv_ref[...] = dv_acc[...].astype(BF16)
        dck_ref[0, 0] = dc_acc[0:1, :]
        dck_ref[1, 0] = dc_acc[1:2, :]

        @pl.when(kb == nb - 1)
        def _():
            dq_ref[...] = dq_acc[...].astype(BF16)

    full = lambda c0: pl.BlockSpec((S, LANES), lambda h, j, c0=c0: (0, c0 + h))
    blk = lambda c0: pl.BlockSpec((t, LANES), lambda h, j, c0=c0: (j, c0 + h))
    f32 = lambda *s: jax.ShapeDtypeStruct(s, F32)
    b16 = jax.ShapeDtypeStruct((S, D), BF16)
    outs, extra = _call(
        body, name="fox_attn_bwd",
        out_shape=(b16, b16, b16, f32(HP, S, 2), f32(2 * HP, nb, 1, t)),
        grid=(HP, nb),
        in_specs=[full(0), blk(HP), blk(2 * HP), pl.BlockSpec((1, S, 4), lambda h, j: (h, 0, 0)), full(0), full(0)],
        out_specs=(full(0), blk(0), blk(0), pl.BlockSpec((1, S, 2), lambda h, j: (h, 0, 0)),
                   pl.BlockSpec((2, 1, 1, t), lambda h, j: (h, j, 0, 0))),
        scratch_shapes=[pltpu.VMEM((2, S, 2 * LANES), BF16), pltpu.VMEM((2, S, LANES), BF16),
                        pltpu.VMEM((S, 2), F32), pltpu.VMEM((S, LANES), F32), pltpu.VMEM((t, LANES), F32),
                        pltpu.VMEM((t, LANES), F32), pltpu.VMEM((2, t), F32)],
        args=(proj, proj, proj, rows, o, do), semantics=("parallel", "arbitrary"), ex=ex)
    return (*outs, extra)


def _s5_consts(T):
    rows = T * SUBLANES
    rr = lax.broadcasted_iota(jnp.int32, (rows, T), 0)
    tt = lax.broadcasted_iota(jnp.int32, (rows, T), 1)
    rep = jnp.where(rr // SUBLANES == tt, 1.0, 0.0).astype(BF16)
    r2 = lax.broadcasted_iota(jnp.int32, (rows, S5_PART), 0)
    c2 = lax.broadcasted_iota(jnp.int32, (rows, S5_PART), 1)
    mask = (c2 // (S5_PART // SUBLANES)) == (r2 % SUBLANES)
    return rep, mask


def _gelu(y):
    c = math.sqrt(2.0 / math.pi)
    return 0.5 * y * (1.0 + jnp.tanh(c * (y + 0.044715 * y * y * y)))


def _gelu_grad(y):
    c = math.sqrt(2.0 / math.pi)
    th = jnp.tanh(c * (y + 0.044715 * y * y * y))
    return 0.5 * (1.0 + th) + 0.5 * y * (1.0 - th * th) * c * (1.0 + 3.0 * 0.044715 * y * y)


def _s5_fwd(x, rmat, cmat, lam, dskip, ex=None):
    S, D = x.shape
    NQ = D // S5_PART
    T = _tile(S, 128, SUBLANES)
    rows = T * SUBLANES

    def body(x_ref, r_ref, c_ref, lam_ref, d_ref, y_ref, yg_ref, h_ref, bu_s, carry):
        i = pl.program_id(0)

        @pl.when(i == 0)
        def _():
            carry[...] = jnp.zeros_like(carry)

        rep, mask = _s5_consts(T)
        cols = [pl.ds(q * S5_PART, S5_PART) for q in range(NQ)]
        for q in range(NQ):
            xrep = jnp.dot(rep, x_ref[:, cols[q]].astype(BF16), preferred_element_type=F32)
            lx = jnp.where(mask, xrep, 0.0).astype(BF16)
            bu_s[q] = jnp.dot(lx, r_ref[q], preferred_element_type=F32)
        lam_v = [(lam_ref[q, :, 0:LANES], lam_ref[q, :, LANES:2 * LANES]) for q in range(NQ)]

        def step(t, c):
            o = pl.multiple_of(t * SUBLANES, SUBLANES)
            new = []
            for q in range(NQ):
                hr, hi = c[q]
                ar, ai = lam_v[q]
                sl = bu_s[q, pl.ds(o, SUBLANES), :]
                nhr = ar * hr - ai * hi + sl[:, 0:LANES]
                nhi = ar * hi + ai * hr + sl[:, LANES:2 * LANES]
                h_ref[q, pl.ds(o, SUBLANES), 0:LANES] = nhr
                h_ref[q, pl.ds(o, SUBLANES), LANES:2 * LANES] = nhi
                new.append((nhr, nhi))
            return tuple(new)

        fin = lax.fori_loop(0, T, step,
                            tuple((carry[q, :, 0:LANES], carry[q, :, LANES:2 * LANES]) for q in range(NQ)))
        for q in range(NQ):
            carry[q, :, 0:LANES] = fin[q][0]
            carry[q, :, LANES:2 * LANES] = fin[q][1]
            z = jnp.dot(h_ref[q].astype(BF16), c_ref[q], preferred_element_type=F32)
            z = jnp.where(mask, z, 0.0)
            y = jnp.sum(z.reshape(T, SUBLANES, S5_PART), axis=1) + d_ref[:, cols[q]] * x_ref[:, cols[q]]
            y_ref[:, cols[q]] = y
            yg_ref[:, cols[q]] = _gelu(y).astype(BF16)

    xs = pl.BlockSpec((T, D), lambda i: (i, 0))
    ms = pl.BlockSpec((NQ, S5_PART, S5_PART), lambda i: (0, 0, 0))
    outs, extra = _call(
        body, name="s5_scan_fwd",
        out_shape=(jax.ShapeDtypeStruct((S, D), F32), jax.ShapeDtypeStruct((S, D), BF16),
                   jax.ShapeDtypeStruct((NQ, S * SUBLANES, S5_PART), F32)),
        grid=(S // T,),
        in_specs=[xs, ms, ms, pl.BlockSpec((NQ, SUBLANES, S5_PART), lambda i: (0, 0, 0)),
                  pl.BlockSpec((1, D), lambda i: (0, 0))],
        out_specs=(xs, xs, pl.BlockSpec((NQ, rows, S5_PART), lambda i: (0, i, 0))),
        scratch_shapes=[pltpu.VMEM((NQ, rows, S5_PART), F32), pltpu.VMEM((NQ, SUBLANES, S5_PART), F32)],
        args=(x, rmat, cmat, lam, dskip), semantics=("arbitrary",), ex=ex)
    return (*outs, extra)


def _s5_bwd(x, y, dyg, hs, rmat, cmat, lam, dskip, res, res_scale, ex=None):
    S, D = x.shape
    NQ = D // S5_PART
    T = _tile(S, 128, SUBLANES)
    nb = S // T
    rows = T * SUBLANES

    def body(x_ref, y_ref, dyg_ref, res_ref, h_ref, hp_ref, r_ref, c_ref, lam_ref, d_ref,
             dx_ref, dr_ref, dc_ref, dlam_ref, dd_ref, dh_s, g_s, hs_s, carry):
        i = pl.program_id(0)

        @pl.when(i == 0)
        def _():
            carry[...] = jnp.zeros_like(carry)
            dr_ref[...] = jnp.zeros_like(dr_ref)
            dc_ref[...] = jnp.zeros_like(dc_ref)
            dlam_ref[...] = jnp.zeros_like(dlam_ref)
            dd_ref[...] = jnp.zeros_like(dd_ref)

        rep, mask = _s5_consts(T)
        cols = [pl.ds(q * S5_PART, S5_PART) for q in range(NQ)]
        dys, ldys = [], []
        for q in range(NQ):
            dy = dyg_ref[:, cols[q]] * _gelu_grad(y_ref[:, cols[q]])
            dyrep = jnp.dot(rep, dy.astype(BF16), preferred_element_type=F32)
            ldy = jnp.where(mask, dyrep, 0.0).astype(BF16)
            dh_s[q] = lax.dot_general(ldy, c_ref[q], (((1,), (1,)), ((), ())), preferred_element_type=F32)
            dys.append(dy)
            ldys.append(ldy)
        lam_v = [(lam_ref[q, :, 0:LANES], lam_ref[q, :, LANES:2 * LANES]) for q in range(NQ)]

        def step(n, c):
            o = pl.multiple_of((T - 1 - n) * SUBLANES, SUBLANES)
            new = []
            for q in range(NQ):
                gr, gi = c[q]
                ar, ai = lam_v[q]
                sl = dh_s[q, pl.ds(o, SUBLANES), :]
                ngr = sl[:, 0:LANES] + ar * gr + ai * gi
                ngi = sl[:, LANES:2 * LANES] - ai * gr + ar * gi
                g_s[q, pl.ds(o, SUBLANES), 0:LANES] = ngr
                g_s[q, pl.ds(o, SUBLANES), LANES:2 * LANES] = ngi
                new.append((ngr, ngi))
            return tuple(new)

        fin = lax.fori_loop(0, T, step,
                            tuple((carry[q, :, 0:LANES], carry[q, :, LANES:2 * LANES]) for q in range(NQ)))
        for q in range(NQ):
            carry[q, :, 0:LANES] = fin[q][0]
            carry[q, :, LANES:2 * LANES] = fin[q][1]
            xv = x_ref[:, cols[q]]
            hv = h_ref[q]
            hs_s[0:SUBLANES, :] = jnp.where(i == nb - 1, 0.0, hp_ref[q])
            hs_s[SUBLANES:rows + SUBLANES, :] = hv
            hprev = hs_s[0:rows, :]
            gv = g_s[q]
            g_re, g_im = gv[:, 0:LANES], gv[:, LANES:2 * LANES]
            hp_re, hp_im = hprev[:, 0:LANES], hprev[:, LANES:2 * LANES]
            dar = jnp.sum((g_re * hp_re + g_im * hp_im).reshape(T, SUBLANES, LANES), axis=0)
            dai = jnp.sum((g_im * hp_re - g_re * hp_im).reshape(T, SUBLANES, LANES), axis=0)
            dlam_ref[q, :, 0:LANES] += dar
            dlam_ref[q, :, LANES:2 * LANES] += dai

            gb = gv.astype(BF16)
            xrep = jnp.dot(rep, xv.astype(BF16), preferred_element_type=F32)
            lx = jnp.where(mask, xrep, 0.0).astype(BF16)
            dr_ref[q] += lax.dot_general(lx, gb, (((0,), (0,)), ((), ())), preferred_element_type=F32)
            dc_ref[q] += lax.dot_general(hv.astype(BF16), ldys[q], (((0,), (0,)), ((), ())),
                                         preferred_element_type=F32)
            zx = lax.dot_general(gb, r_ref[q], (((1,), (1,)), ((), ())), preferred_element_type=F32)
            zx = jnp.where(mask, zx, 0.0)
            dx_ref[:, cols[q]] = (jnp.sum(zx.reshape(T, SUBLANES, S5_PART), axis=1) + d_ref[:, cols[q]] * dys[q]
                                  + res_scale * res_ref[:, cols[q]])
            dd_ref[:, cols[q]] += jnp.sum(dys[q] * xv, axis=0, keepdims=True)

    xs = pl.BlockSpec((T, D), lambda i: (nb - 1 - i, 0))
    ms = pl.BlockSpec((NQ, S5_PART, S5_PART), lambda i: (0, 0, 0))
    ls = pl.BlockSpec((NQ, SUBLANES, S5_PART), lambda i: (0, 0, 0))
    ds_ = pl.BlockSpec((1, D), lambda i: (0, 0))
    outs, extra = _call(
        body, name="s5_scan_bwd",
        out_shape=(jax.ShapeDtypeStruct((S, D), F32), jax.ShapeDtypeStruct((NQ, S5_PART, S5_PART), F32),
                   jax.ShapeDtypeStruct((NQ, S5_PART, S5_PART), F32),
                   jax.ShapeDtypeStruct((NQ, SUBLANES, S5_PART), F32), jax.ShapeDtypeStruct((1, D), F32)),
        grid=(nb,),
        in_specs=[xs, xs, xs, xs, pl.BlockSpec((NQ, rows, S5_PART), lambda i: (0, nb - 1 - i, 0)),
                  pl.BlockSpec((NQ, SUBLANES, S5_PART), lambda i: (0, jnp.maximum((nb - 1 - i) * T - 1, 0), 0)),
                  ms, ms, ls, ds_],
        out_specs=(xs, ms, ms, ls, ds_),
        scratch_shapes=[pltpu.VMEM((NQ, rows, S5_PART), F32), pltpu.VMEM((NQ, rows, S5_PART), F32),
                        pltpu.VMEM((rows + SUBLANES, S5_PART), F32), pltpu.VMEM((NQ, SUBLANES, S5_PART), F32)],
        args=(x, y, dyg, res, hs, hs, rmat, cmat, lam, dskip), semantics=("arbitrary",), ex=ex)
    return (*outs, extra)


def _s5_discretise(a_re, a_im, log_dt, b_re, b_im):
    dt = jnp.exp(log_dt)[:, None]
    mag = jnp.exp(a_re * dt)
    ang = a_im * dt
    lb_re = mag * jnp.cos(ang)
    lb_im = mag * jnp.sin(ang)
    den = a_re * a_re + a_im * a_im
    nr = lb_re - 1.0
    ni = lb_im
    z_re = (nr * a_re + ni * a_im) / den
    z_im = (ni * a_re - nr * a_im) / den
    bb_re = z_re[..., None] * b_re - z_im[..., None] * b_im
    bb_im = z_re[..., None] * b_im + z_im[..., None] * b_re
    return lb_re, lb_im, bb_re, bb_im


def _s5_expand(w):
    G = w.shape[0]
    NQ = G // 16
    base = w.reshape(NQ, S5_PART, S5_STATE)
    half = (jnp.arange(S5_PART) // S5_GROUP) % 2
    sel = (half[:, None] == jnp.arange(2)[None, :]).astype(w.dtype)
    out = base[:, :, None, :] * sel[None, :, :, None]
    return out.reshape(NQ, S5_PART, 2 * S5_STATE)


def _s5_extract(m):
    NQ = m.shape[0]
    half = (jnp.arange(S5_PART) // S5_GROUP) % 2
    sel = (half[:, None] == jnp.arange(2)[None, :]).astype(m.dtype)
    base = jnp.sum(m.reshape(NQ, S5_PART, 2, S5_STATE) * sel[None, :, :, None], axis=2)
    return base.reshape(NQ * 16, S5_GROUP, S5_STATE)


def _s5_slab(v):
    return v.reshape(v.shape[0] // 16, SUBLANES, LANES)


def _adamw(w, g, m, v, name):
    R, C = w.shape
    tr = _tile(R, 256, SUBLANES)
    c1 = 1.0 / (1.0 - ADAM_B1 ** ADAM_STEP)
    c2 = 1.0 / (1.0 - ADAM_B2 ** ADAM_STEP)

    def body(w_ref, g_ref, m_ref, v_ref, d_ref, nm_ref, nv_ref):
        gv = g_ref[...]
        nm = ADAM_B1 * m_ref[...] + (1.0 - ADAM_B1) * gv
        nv = ADAM_B2 * v_ref[...] + (1.0 - ADAM_B2) * (gv * gv)
        nm_ref[...] = nm
        nv_ref[...] = nv
        d_ref[...] = -ADAM_LR * ((nm * c1) / (jnp.sqrt(nv * c2) + ADAM_EPS) + ADAM_WD * w_ref[...])

    blk = pl.BlockSpec((tr, C), lambda i: (i, 0))
    sh = jax.ShapeDtypeStruct((R, C), F32)
    return pl.pallas_call(
        body, name=name, out_shape=(sh, sh, sh), grid=(R // tr,), in_specs=[blk] * 4, out_specs=(blk,) * 3,
        compiler_params=_params(("parallel",)),
    )(w, g, m, v)


def _adamw_recv(parts, w, m, v, name):
    L, R, C = w.shape
    tr = _tile(R, 256, SUBLANES)
    c1 = 1.0 / (1.0 - ADAM_B1 ** ADAM_STEP)
    c2 = 1.0 / (1.0 - ADAM_B2 ** ADAM_STEP)

    def body(*refs):
        p_refs = refs[:L]
        w_ref, m_ref, v_ref, g_ref, d_ref, nm_ref, nv_ref = refs[L:]
        li = pl.program_id(0)
        for l in range(L):
            @pl.when(li == l)
            def _(p_ref=p_refs[l]):
                gv = p_ref[0].astype(F32)
                for k in range(1, N_DEV):
                    gv = gv + p_ref[k].astype(F32)
                g_ref[...] = gv
                nm = ADAM_B1 * m_ref[...] + (1.0 - ADAM_B1) * gv
                nv = ADAM_B2 * v_ref[...] + (1.0 - ADAM_B2) * (gv * gv)
                nm_ref[...] = nm
                nv_ref[...] = nv
                d_ref[...] = -ADAM_LR * ((nm * c1) / (jnp.sqrt(nv * c2) + ADAM_EPS) + ADAM_WD * w_ref[...])

    p_specs = [pl.BlockSpec((N_DEV, tr, C), lambda li, i, l=l: (0, jnp.where(li == l, i, 0), 0)) for l in range(L)]
    blk = pl.BlockSpec((None, tr, C), lambda li, i: (li, i, 0))
    sh = jax.ShapeDtypeStruct((L, R, C), F32)
    return pl.pallas_call(
        body, name=name, out_shape=(sh, sh, sh, sh), grid=(L, R // tr),
        in_specs=p_specs + [blk, blk, blk], out_specs=(blk,) * 4,
        compiler_params=_params(("parallel", "parallel")),
    )(*parts, w, m, v)


def _sum8(parts, name):
    _, R, C = parts.shape
    tr = _tile(R, 256, SUBLANES)

    def body(p_ref, o_ref):
        acc = p_ref[0].astype(F32)
        for k in range(1, N_DEV):
            acc = acc + p_ref[k].astype(F32)
        o_ref[...] = acc

    return pl.pallas_call(
        body, name=name, out_shape=jax.ShapeDtypeStruct((R, C), F32), grid=(R // tr,),
        in_specs=[pl.BlockSpec((N_DEV, tr, C), lambda i: (0, i, 0))],
        out_specs=pl.BlockSpec((tr, C), lambda i: (i, 0)), compiler_params=_params(("parallel",)),
    )(parts)


def _peers():
    x, y, c = lax.axis_index("x"), lax.axis_index("y"), lax.axis_index("c")
    me = 4 * x + 2 * y + c
    out = []
    for k in range(1, N_DEV):
        kx, ky, kc = (k >> 2) & 1, (k >> 1) & 1, k & 1
        px, py, pc = x ^ kx, y ^ ky, c ^ kc
        out.append(((px, py, pc), 4 * px + 2 * py + pc))
    return me, out


SIBLING = 1
SAME_CORE = (2, 4, 6)


class Exchange:
    def __init__(self, xs, scatter):
        self.xs = list(xs)
        self.scatter = list(scatter)
        self.n = len(self.xs)

    def out_shapes(self):
        return tuple(jax.ShapeDtypeStruct(x.shape if sc else (N_DEV,) + x.shape, x.dtype)
                     for x, sc in zip(self.xs, self.scatter))

    def sems(self):
        return [pltpu.SemaphoreType.DMA((self.n * N_DEV,)), pltpu.SemaphoreType.DMA((self.n * N_DEV,)),
                pltpu.SemaphoreType.DMA((self.n,))]

    def _copy(self, i, m, src, dst, to, send_sems, recv_sems):
        return pltpu.make_async_remote_copy(src_ref=src, dst_ref=dst, send_sem=send_sems.at[i * N_DEV + m],
                                            recv_sem=recv_sems.at[i * N_DEV + m], device_id=to, device_id_type=MESH)

    def start(self, x_refs, o_refs, send_sems, recv_sems, local_sems):
        me, peers = _peers()
        for i, (x_ref, o_ref, sc) in enumerate(zip(x_refs, o_refs, self.scatter)):
            pltpu.make_async_copy(x_ref.at[me] if sc else x_ref, o_ref.at[me], local_sems.at[i]).start()
            for m in (range(1, N_DEV) if sc else (SIBLING,) + SAME_CORE):
                dev, idx = peers[m - 1]
                self._copy(i, m, x_ref.at[idx] if sc else x_ref, o_ref.at[me], dev, send_sems, recv_sems).start()

    def middle(self, x_refs, o_refs, send_sems, recv_sems, local_sems):
        me, peers = _peers()
        sibling = peers[SIBLING - 1][0]
        for i, (x_ref, o_ref, sc) in enumerate(zip(x_refs, o_refs, self.scatter)):
            if sc:
                continue
            for m in SAME_CORE:
                dev, idx = peers[m - 1]
                self._copy(i, m, x_ref, o_ref.at[idx], dev, send_sems, recv_sems).wait_recv()
                self._copy(i, m ^ 1, o_ref.at[idx], o_ref.at[idx], sibling, send_sems, recv_sems).start()

    def wait(self, x_refs, o_refs, send_sems, recv_sems, local_sems):
        me, peers = _peers()
        for i, (x_ref, o_ref, sc) in enumerate(zip(x_refs, o_refs, self.scatter)):
            for m in range(1, N_DEV):
                dev, idx = peers[m - 1]
                cp = self._copy(i, m, x_ref.at[idx] if sc else x_ref, o_ref.at[idx], dev, send_sems, recv_sems)
                if sc or m not in SAME_CORE:
                    cp.wait_recv()
                cp.wait_send()
            pltpu.make_async_copy(x_ref.at[me] if sc else x_ref, o_ref.at[me], local_sems.at[i]).wait()


def _exchange(ex, name):
    n = ex.n

    def body(*refs):
        x_refs, o_refs, sems = refs[:n], refs[n:2 * n], refs[2 * n:]
        ex.start(x_refs, o_refs, *sems)
        ex.middle(x_refs, o_refs, *sems)
        ex.wait(x_refs, o_refs, *sems)

    hbm = pl.BlockSpec(memory_space=pltpu.HBM)
    return pl.pallas_call(body, name=name, out_shape=ex.out_shapes(), in_specs=[hbm] * n, out_specs=(hbm,) * n,
                          scratch_shapes=ex.sems())(*ex.xs)


def _call(body, *, name, out_shape, grid, in_specs, out_specs, scratch_shapes, args, semantics, ex=None):
    if ex is None:
        return pl.pallas_call(body, name=name, out_shape=out_shape, grid=grid, in_specs=in_specs, out_specs=out_specs,
                              scratch_shapes=scratch_shapes, compiler_params=_params(semantics))(*args), ()
    n, ni, no, ns = ex.n, len(args), len(out_shape), len(scratch_shapes)

    def wrapped(*refs):
        ins, cx = refs[:ni], refs[ni:ni + n]
        outs, co = refs[ni + n:ni + n + no], refs[ni + n + no:ni + 2 * n + no]
        scratch, sems = refs[ni + 2 * n + no:ni + 2 * n + no + ns], refs[ni + 2 * n + no + ns:]
        step = functools.reduce(lambda acc, a: acc * grid[a] + pl.program_id(a), range(len(grid)), 0)
        steps = math.prod(grid)

        @pl.when(step == 0)
        def _():
            ex.start(cx, co, *sems)

        body(*ins, *outs, *scratch)

        @pl.when(step == (steps * 3) // 4 - (steps > 1))
        def _():
            ex.middle(cx, co, *sems)

        @pl.when(step == steps - 1)
        def _():
            ex.wait(cx, co, *sems)

    hbm = pl.BlockSpec(memory_space=pltpu.HBM)
    res = pl.pallas_call(
        wrapped, name=name, out_shape=tuple(out_shape) + ex.out_shapes(), grid=grid,
        in_specs=list(in_specs) + [hbm] * n, out_specs=tuple(out_specs) + (hbm,) * n,
        scratch_shapes=list(scratch_shapes) + ex.sems(),
        compiler_params=_params(("arbitrary",) * len(grid)))(*args, *ex.xs)
    return res[:no], res[no:]


def _pack_flat(arrs):
    cat = jnp.concatenate([a.reshape(-1) for a in arrs])
    per = PACK_COLS * 2 * SUBLANES
    tot = -(-cat.shape[0] // per) * per
    return jnp.pad(cat, (0, tot - cat.shape[0])).reshape(tot // PACK_COLS, PACK_COLS)


def _unpack_flat(packed, shapes):
    flat = packed.reshape(-1)
    out, o = [], 0
    for s in shapes:
        n = math.prod(s)
        out.append(flat[o:o + n].reshape(s))
        o += n
    return out


def _ffn_fwd(x, xb, wt, wo4, g, b, alpha, tag, ex=None):
    h8, a4, extra = _ffn_h(xb, wt, tag + "_h", ex)
    out, outb, xh, rstd = _ffn_y_ln(a4, wo4, x, g, b, alpha, 0.5, tag + "_y_ln")
    return out, outb, (xb, h8, a4, xh, rstd), extra


def _ffn_bwd(dout, saved, wt, wo4, g, alpha, tag, carry=None, last=None):
    x, h8, a4, xh, rstd = saved
    load = dict(carry or {})
    slots = [[], [], [], []]
    for n, k in enumerate(sorted(load, key=lambda k: -load[k].size)):
        slots[min(n, 3)].append(k)
    arrived = {}

    def run(fn, slot, *args):
        keys = slots[slot]
        if not keys:
            return fn(*args)
        res, extra = fn(*args, ex=Exchange([load[k] for k in keys], [k != "small" for k in keys]))
        arrived.update(zip(keys, extra))
        return res

    dz, dg, db = _ln_bwd(dout, xh, rstd, g, tag + "_ln_bwd")
    if last is not None:
        (key_out, key_in), load["small"] = last(dg, db)
        slots[3].append("small")
    dw_out = run(_ffn_dwout, 3, a4, dz, 0.5, tag + "_dwout")
    dw_out = dw_out.reshape(N_DEV, -1, dw_out.shape[-1])
    if last is not None:
        load[key_out] = dw_out
        slots[2].append(key_out)
    dh8 = run(_ffn_dh, 2, dz, wo4, h8, 0.5, tag + "_dh")
    dw_in = run(_ffn_dwin, 0, x, dh8, tag + "_dwin")
    if last is not None:
        load[key_in] = dw_in
        slots[1].append(key_in)
    dx = run(_ffn_dx, 1, dh8, wt, dz, alpha, tag + "_dx")
    return dx, dw_in, dw_out, dg, db, arrived


def _fox_fwd(x, xb, w_in_pad, b_f_pad, w_o, g, b, alpha, tag, ex=None):
    S, D = x.shape
    H = D // HEAD_DIM
    proj = _mm(xb, w_in_pad, name=tag + "_proj")
    fl = proj[:, 3 * D:]
    cum = _fox_gate_fwd(fl, b_f_pad)
    cum2 = (cum[:, :H].T * LOG2E).reshape(H // 2, 2, S).transpose(0, 2, 1)
    o, lse, extra = _flash_fwd(proj, cum2, ex)
    m = _mm(o, w_o, name=tag + "_out")
    out, outb, xh, rstd = _res_ln_fwd(x, m, g, b, alpha, 1.0, tag + "_ln")
    return out, outb, (xb, proj, jnp.concatenate([cum2, lse], axis=2), o, fl, xh, rstd), extra


def _fox_bwd(dout, saved, w_in_pad, b_f_pad, w_o, g, alpha, tag, ex=None):
    x, proj, rows, o, fl, xh, rstd = saved
    S, D = x.shape
    H = D // HEAD_DIM
    dz, dg, db = _ln_bwd(dout, xh, rstd, g, tag + "_ln_bwd")
    dw_o = _mm(o, dz, ta=True, tk=2048, name=tag + "_dwo")
    do = _mm(dz, w_o, tb=True, name=tag + "_do")
    dq, dk, dv, dcq, dck, extra = _flash_bwd(proj, rows, o, do, ex)
    dcq = dcq.transpose(0, 2, 1).reshape(H, S)
    dcum = jnp.pad((dcq + dck.reshape(H, S)).T, ((0, 0), (0, LANES - H)))
    dfl, dbf = _fox_gate_bwd(dcum, fl, b_f_pad)
    dproj = jnp.concatenate([dq, dk, dv, dfl.astype(BF16)], axis=1)
    dw_in = _mm(x, dproj, ta=True, tk=x.shape[0], name=tag + "_dwin")
    dx = _mm(dproj, w_in_pad, tb=True, add=dz, add_scale=alpha, tk=dproj.shape[1], name=tag + "_dx")
    shards = {"fox_w_in": _split(dw_in[None, :, :3 * D + H], True)[:, 0].astype(BF16),
              "fox_w_o": _split(dw_o[None], False)[:, 0].astype(BF16)}
    return dx, shards, {"fox_b_f": dbf[0, :H], "lnm_g": dg, "lnm_b": db}, extra


def _s5_mats(p):
    lb_re, lb_im, bb_re, bb_im = _s5_discretise(p["a_re"], p["a_im"], p["log_dt"], p["b_re"], p["b_im"])
    rmat = jnp.concatenate([_s5_expand(bb_re.transpose(0, 2, 1)), _s5_expand(bb_im.transpose(0, 2, 1))], axis=2)
    cmat = jnp.concatenate([_s5_expand(p["c_re"]).transpose(0, 2, 1), -_s5_expand(p["c_im"]).transpose(0, 2, 1)],
                           axis=1)
    lam = jnp.concatenate([_s5_slab(lb_re), _s5_slab(lb_im)], axis=2)
    return rmat.astype(BF16), cmat.astype(BF16), lam


def _s5_block_fwd(x, p, w_out, g, b, alpha, tag, ex=None):
    S, D = x.shape
    rmat, cmat, lam = _s5_mats(p)
    dskip = p["d"].reshape(1, D)
    y, yg, hs, extra = _s5_fwd(x, rmat, cmat, lam, dskip, ex)
    vg = _mm(yg, w_out, name=tag + "_vg")
    m = _glu_fwd(vg, tag + "_glu")
    out, outb, xh, rstd = _res_ln_fwd(x, m, g, b, alpha, 1.0, tag + "_ln")
    return out, outb, (x, y, yg, hs, vg, rmat, cmat, lam, dskip, xh, rstd), extra


def _s5_block_bwd(dout, saved, p, w_out, g, alpha, tag, ex=None):
    x, y, yg, hs, vg, rmat, cmat, lam, dskip, xh, rstd = saved
    S, D = x.shape
    G = D // S5_GROUP
    dz, dg, db = _ln_bwd(dout, xh, rstd, g, tag + "_ln_bwd")
    dvg = _glu_bwd(vg, dz, tag + "_glu_bwd")
    dw_out = _mm(yg, dvg, ta=True, tk=yg.shape[0], name=tag + "_dwout")
    dyg = _mm(dvg, w_out, tb=True, tk=dvg.shape[1], name=tag + "_dyg")
    dx, dr, dc, dlam, dd, extra = _s5_bwd(x, y, dyg, hs, rmat, cmat, lam, dskip, dz, alpha, ex)
    dbb_re = _s5_extract(dr[:, :, :LANES]).transpose(0, 2, 1)
    dbb_im = _s5_extract(dr[:, :, LANES:]).transpose(0, 2, 1)
    dc_re = _s5_extract(dc[:, :LANES, :].transpose(0, 2, 1))
    dc_im = -_s5_extract(dc[:, LANES:, :].transpose(0, 2, 1))
    dlb_re = dlam[:, :, :LANES].reshape(G, S5_STATE)
    dlb_im = dlam[:, :, LANES:].reshape(G, S5_STATE)
    _, vjp = jax.vjp(_s5_discretise, p["a_re"], p["a_im"], p["log_dt"], p["b_re"], p["b_im"])
    da_re, da_im, dlog_dt, db_re, db_im = vjp((dlb_re, dlb_im, dbb_re, dbb_im))
    small = dict(s5_a_re=da_re, s5_a_im=da_im, s5_log_dt=dlog_dt, s5_b_re=db_re, s5_b_im=db_im, s5_c_re=dc_re,
                 s5_c_im=dc_im, s5_d=dd.reshape(G, S5_GROUP), lnm_g=dg, lnm_b=db)
    return dx, {"s5_w_out": _split(dw_out[None], True)[:, 0].astype(BF16)}, small, extra


FFN_NAMES = ("ffn1_w_in", "ffn1_w_out", "ffn2_w_in", "ffn2_w_out")
BIG = FFN_NAMES + ("fox_w_in", "fox_w_o", "s5_w_out")
BIG_SPLIT_COLS = {"ffn1_w_in": True, "ffn1_w_out": False, "ffn2_w_in": True, "ffn2_w_out": False,
                  "fox_w_in": True, "fox_w_o": False, "s5_w_out": True}
SMALL = ("ln1_g", "ln1_b", "lnm_g", "lnm_b", "ln2_g", "ln2_b", "fox_b_f", "s5_a_re", "s5_a_im", "s5_log_dt",
         "s5_b_re", "s5_b_im", "s5_c_re", "s5_c_im", "s5_d")
WEIGHTS = ("ffn1_w_in", "ffn1_w_out", "ln1_g", "ln1_b", "lnm_g", "lnm_b", "ffn2_w_in", "ffn2_w_out", "ln2_g", "ln2_b",
           "fox_w_in", "fox_b_f", "fox_w_o", "s5_a_re", "s5_a_im", "s5_log_dt", "s5_b_re", "s5_b_im", "s5_c_re",
           "s5_c_im", "s5_d", "s5_w_out")


def _join(gathered, split_cols):
    n, L, r, c = gathered.shape
    if split_cols:
        return gathered.transpose(1, 2, 0, 3).reshape(L, r, n * c)
    return gathered.transpose(1, 0, 2, 3).reshape(L, n * r, c)


def _split(full, split_cols):
    L, R, C = full.shape
    if split_cols:
        return full.reshape(L, R, N_DEV, C // N_DEV).transpose(2, 0, 1, 3)
    return full.reshape(L, N_DEV, R // N_DEV, C).transpose(1, 0, 2, 3)


def _group(i, part):
    if part == "mixer":
        return (("fox_w_in", i // 2), ("fox_w_o", i // 2)) if i % 2 == 0 else (("s5_w_out", i // 2),)
    return ((part + "_w_in", i), (part + "_w_out", i))


def _prepare(name, g8):
    if name in ("ffn1_w_in", "ffn2_w_in"):
        return g8
    if name in FFN_NAMES:
        n, r, c = g8.shape
        return g8.reshape(n // 2, 2 * r, c)
    full = _join(g8[:, None], BIG_SPLIT_COLS[name])[0]
    if name == "fox_w_in":
        full = jnp.pad(full, ((0, 0), (0, LANES - full.shape[0] // HEAD_DIM)))
    return full


def _local_step(x, target, small, shard_of=None, pregathered=None):
    S, D = x.shape
    H = D // HEAD_DIM
    depth = small["ln1_g"].shape[0]
    alpha = (2.0 * depth) ** 0.25
    local = pregathered is not None
    bf_pad = jnp.pad(small["fox_b_f"], ((0, 0), (0, LANES - H)))

    def s5_params(j):
        return {k: small["s5_" + k][j] for k in ("a_re", "a_im", "log_dt", "b_re", "b_im", "c_re", "c_im", "d")}

    if local:
        W = {k: _prepare(k[0], g8) for k, g8 in pregathered.items()}
    else:
        keys = _group(0, "ffn1")
        got = _exchange(Exchange([shard_of(*k) for k in keys], [False] * len(keys)), "gather_first")
        W = {k: _prepare(k[0], g8) for k, g8 in zip(keys, got)}

    def gather(keys):
        return None if local else Exchange([shard_of(*k) for k in keys], [False] * len(keys))

    def landed(keys, extra):
        if not local:
            W.update({k: _prepare(k[0], g8) for k, g8 in zip(keys, extra)})

    saved = []
    h, hb = x, x.astype(BF16)
    for i in range(depth):
        j = i // 2
        keys = _group(i, "mixer")
        h, hb, s1, extra = _ffn_fwd(h, hb, W[("ffn1_w_in", i)], W[("ffn1_w_out", i)], small["ln1_g"][i],
                                    small["ln1_b"][i], alpha, f"l{i}_ffn1", gather(keys))
        landed(keys, extra)
        keys = _group(i, "ffn2") + (_group(i + 1, "ffn1") if i + 1 < depth else ())
        ex = gather(keys)
        if i % 2 == 0:
            h, hb, s2, extra = _fox_fwd(h, hb, W[("fox_w_in", j)], bf_pad[j:j + 1], W[("fox_w_o", j)],
                                        small["lnm_g"][i], small["lnm_b"][i], alpha, f"l{i}_fox", ex)
        else:
            h, hb, s2, extra = _s5_block_fwd(h, s5_params(j), W[("s5_w_out", j)], small["lnm_g"][i],
                                             small["lnm_b"][i], alpha, f"l{i}_s5", ex)
        landed(keys, extra)
        h, hb, s3, _ = _ffn_fwd(h, hb, W[("ffn2_w_in", i)], W[("ffn2_w_out", i)], small["ln2_g"][i],
                                small["ln2_b"][i], alpha, f"l{i}_ffn2")
        saved.append((s1, s2, s3))

    dh, loss_part = _loss_fwd_bwd(h, target)

    arrived = {}
    pending = {}
    gs = {k: [None] * small[k].shape[0] for k in SMALL}
    for i in reversed(range(depth)):
        j = i // 2
        s1, s2, s3 = saved[i]
        held = {k: pending.pop(k) for k in [("ffn1_w_in", i + 1)] if i % 2 == 0 and k in pending}
        dh, dw_in, dw_out, gs["ln2_g"][i], gs["ln2_b"][i], got = _ffn_bwd(
            dh, s3, W[("ffn2_w_in", i)], W[("ffn2_w_out", i)], small["ln2_g"][i], alpha, f"l{i}_ffn2",
            None if local else pending)
        arrived.update(pending if local else got)
        pending = {("ffn2_w_in", i): dw_in, ("ffn2_w_out", i): dw_out, **held}
        keys = list(pending)
        ex = None if local else Exchange([pending[k] for k in keys], [True] * len(keys))
        if i % 2 == 0:
            dh, mix, sg, extra = _fox_bwd(dh, s2, W[("fox_w_in", j)], bf_pad[j:j + 1], W[("fox_w_o", j)],
                                          small["lnm_g"][i], alpha, f"l{i}_fox", ex)
        else:
            dh, mix, sg, extra = _s5_block_bwd(dh, s2, s5_params(j), W[("s5_w_out", j)], small["lnm_g"][i], alpha,
                                               f"l{i}_s5", ex)
        arrived.update(zip(keys, [pending[k] for k in keys] if local else extra))
        pending = {(k, j): val for k, val in mix.items()}
        for k, val in sg.items():
            gs[k][i if k in ("lnm_g", "lnm_b") else j] = val
        def last(dg, db):
            gs["ln1_g"][0], gs["ln1_b"][0] = dg, db
            packed = _pack_flat([jnp.stack(gs[k]) for k in SMALL]).astype(BF16)
            return (("ffn1_w_out", 0), ("ffn1_w_in", 0)), packed

        dh, dw_in, dw_out, gs["ln1_g"][i], gs["ln1_b"][i], got = _ffn_bwd(
            dh, s1, W[("ffn1_w_in", i)], W[("ffn1_w_out", i)], small["ln1_g"][i], alpha, f"l{i}_ffn1",
            None if local else pending, last if (i == 0 and not local) else None)
        arrived.update(pending if local else got)
        pending = {("ffn1_w_in", i): dw_in, ("ffn1_w_out", i): dw_out}
    if local:
        arrived.update(pending)
    return loss_part, dh, arrived, {k: jnp.stack(v) for k, v in gs.items()}


def kernel(x, ffn1_w_in, ffn1_w_out, ln1_g, ln1_b, lnm_g, lnm_b, ffn2_w_in, ffn2_w_out, ln2_g, ln2_b, fox_w_in, fox_b_f, fox_w_o, s5_a_re, s5_a_im, s5_log_dt, s5_b_re, s5_b_im, s5_c_re, s5_c_im, s5_d, s5_w_out, loss_target, m_ffn1_w_in, m_ffn1_w_out, m_ln1_g, m_ln1_b, m_lnm_g, m_lnm_b, m_ffn2_w_in, m_ffn2_w_out, m_ln2_g, m_ln2_b, m_fox_w_in, m_fox_b_f, m_fox_w_o, m_s5_a_re, m_s5_a_im, m_s5_log_dt, m_s5_b_re, m_s5_b_im, m_s5_c_re, m_s5_c_im, m_s5_d, m_s5_w_out, v_ffn1_w_in, v_ffn1_w_out, v_ln1_g, v_ln1_b, v_lnm_g, v_lnm_b, v_ffn2_w_in, v_ffn2_w_out, v_ln2_g, v_ln2_b, v_fox_w_in, v_fox_b_f, v_fox_w_o, v_s5_a_re, v_s5_a_im, v_s5_log_dt, v_s5_b_re, v_s5_b_im, v_s5_c_re, v_s5_c_im, v_s5_d, v_s5_w_out):
    args = dict(locals())
    w = {k: args[k] for k in WEIGHTS}
    m = {k: args["m_" + k] for k in WEIGHTS}
    v = {k: args["v_" + k] for k in WEIGHTS}
    small = {k: w[k] for k in SMALL}

    turned = lambda d: {k: jnp.swapaxes(a, 1, 2) if k in ("ffn1_w_in", "ffn2_w_in") else a for k, a in d.items()}
    w, m, v = turned(w), turned(m), turned(v)
    wb = {k: w[k].astype(BF16) for k in BIG}
    loss_part, dx, arrived, _ = _local_step(x[0], loss_target[0], small, lambda k, idx: wb[k][idx])
    loss = lax.psum(loss_part, ("x", "y", "c"))
    small_shapes = [w[k].shape for k in SMALL]
    g_small_flat = _sum8(arrived["small"], "sum_small_grads")

    grads, delta, new_m, new_v = {}, {}, {}, {}
    for k in BIG:
        parts = [arrived[(k, l)] for l in range(w[k].shape[0])]
        grads[k], delta[k], new_m[k], new_v[k] = _adamw_recv(parts, w[k], m[k], v[k], "adamw_" + k)
    pk = lambda d: _pack_flat([d[k] for k in SMALL])
    d_, m_, v_ = _adamw(pk(w), g_small_flat, pk(m), pk(v), "adamw_small")
    for dst, flat in ((grads, g_small_flat), (delta, d_), (new_m, m_), (new_v, v_)):
        dst.update(zip(SMALL, _unpack_flat(flat, small_shapes)))
    grads, delta, new_m, new_v = turned(grads), turned(delta), turned(new_m), turned(new_v)

    return (loss, dx[None], *[grads[k] for k in WEIGHTS], *[delta[k] for k in WEIGHTS],
            *[new_m[k] for k in WEIGHTS], *[new_v[k] for k in WEIGHTS])
```
